```python
import jax, jax.numpy as jnp
from jax import lax
import numpy as np

D_MODEL = 1024
BATCH = 4
SEQ = 4096
DEPTH = 1

HEAD_DIM = 64
N_MIX_HEADS = D_MODEL // HEAD_DIM
A_Q_HEADS = N_MIX_HEADS // 2
A_KV_HEADS = max(1, A_Q_HEADS // 4)
A_HALF_WINDOW = 128
B_HEADS = N_MIX_HEADS - A_Q_HEADS
B_BRANCHES = ((128, 1), (512, 4), (2048, 16))
ATTN_BLOCK = 128
N_ALIBI_HEADS = A_Q_HEADS + B_HEADS
A_Q_W = A_Q_HEADS * HEAD_DIM
A_KV_W = A_KV_HEADS * HEAD_DIM
B_W = B_HEADS * HEAD_DIM
IN_PROJ_W = A_Q_W + 2 * A_KV_W + 3 * B_W
MIX_W = A_Q_W + B_W
N_EXPERTS = 32
TOP_K = 4
D_FF = D_MODEL
SWIGLU_ALPHA = 1.702
SWIGLU_LIMIT = 7.0
MOE_ROW_BLOCK = 128
NORM_EPS = 1e-5
MASK_VALUE = -1e30

kernel_name = "hymba_style_sink_gqa_dilated_moe_encoder"


def rms_norm(x, g):
    xf = x.astype(jnp.float32)
    y = xf * lax.rsqrt(jnp.mean(xf * xf, axis=-1, keepdims=True) + NORM_EPS)
    return (y * g.astype(jnp.float32)).astype(x.dtype)


def alibi_slopes(n):
    return jnp.exp2(-8.0 * jnp.arange(1, n + 1, dtype=jnp.float32) / n)


def banded_attention(q, k, v, half_w, dist_scale, slopes, sink=None):
    n, h, r, L, hd = q.shape
    blk = min(ATTN_BLOCK, L)
    nb = -(-L // blk)
    lp = nb * blk
    kw = blk + 2 * half_w
    qb = jnp.pad(q, ((0, 0), (0, 0), (0, 0), (0, lp - L), (0, 0))).reshape(n, h, r, nb, blk, hd)
    pad_k = ((0, 0), (0, 0), (half_w, lp - L + half_w), (0, 0))
    kidx = jnp.arange(nb)[:, None] * blk + jnp.arange(kw)[None, :]
    kb = jnp.pad(k, pad_k)[:, :, kidx]
    vb = jnp.pad(v, pad_k)[:, :, kidx]
    s = jnp.einsum('nhrbqd,nhbkd->nhrbqk', qb, kb).astype(jnp.float32) * (hd ** -0.5)
    qpos = jnp.arange(nb)[:, None] * blk + jnp.arange(blk)[None, :]
    kpos = kidx - half_w
    dist = jnp.abs(qpos[:, :, None] - kpos[:, None, :])
    valid = (dist <= half_w) & (kpos >= 0)[:, None, :] & (kpos < L)[:, None, :]
    bias = -slopes.astype(jnp.float32)[:, :, None, None, None] * (dist * dist_scale).astype(jnp.float32)
    s = jnp.where(valid, s + bias, MASK_VALUE)
    m = jnp.max(s, axis=-1)
    if sink is not None:
        sink_f = sink.astype(jnp.float32)[:, :, None, None]
        m = jnp.maximum(m, sink_f)
    p = jnp.exp(s - m[..., None])
    denom = jnp.sum(p, axis=-1)
    if sink is not None:
        denom = denom + jnp.exp(sink_f - m)
    o = jnp.einsum('nhrbqk,nhbkd->nhrbqd', p, vb.astype(jnp.float32)) / denom[..., None]
    o = o.reshape(n, h, r, lp, hd)[:, :, :, :L].astype(q.dtype)
    lse = (m + jnp.log(denom)).reshape(n, h, r, lp)[..., :L]
    return o, lse


def windowed_sink_gqa(q, k, v, q_gain, k_gain, sinks, slopes):
    b, s, hq, hd = q.shape
    hkv = k.shape[2]
    r = hq // hkv
    q = rms_norm(q, q_gain)
    k = rms_norm(k, k_gain)
    qh = q.reshape(b, s, hkv, r, hd).transpose(0, 2, 3, 1, 4)
    kh = k.transpose(0, 2, 1, 3)
    vh = v.transpose(0, 2, 1, 3)
    o, _ = banded_attention(qh, kh, vh, A_HALF_WINDOW, 1,
                            slopes.reshape(hkv, r), sinks.reshape(hkv, r))
    return o.transpose(0, 3, 1, 2, 4).reshape(b, s, hq * hd)


def to_strided(t, dil):
    b, s, h, hd = t.shape
    return t.reshape(b, s // dil, dil, h, hd).transpose(0, 2, 3, 1, 4).reshape(b * dil, h, s // dil, hd)


def dilated_attention(q, k, v, q_gain, k_gain, slopes):
    b, s, h, hd = q.shape
    q = rms_norm(q, q_gain)
    k = rms_norm(k, k_gain)
    outs, lses = [], []
    for window, dil in B_BRANCHES:
        L = s // dil
        o, lse = banded_attention(to_strided(q, dil)[:, :, None], to_strided(k, dil),
                                  to_strided(v, dil), window // (2 * dil), dil, slopes[:, None])
        o = o[:, :, 0].reshape(b, dil, h, L, hd).transpose(0, 3, 1, 2, 4).reshape(b, s, h, hd)
        lse = lse[:, :, 0].reshape(b, dil, h, L).transpose(0, 3, 1, 2).reshape(b, s, h)
        outs.append(o.astype(jnp.float32))
        lses.append(lse)
    w = jax.nn.softmax(jnp.stack(lses), axis=0)
    o = jnp.sum(w[..., None] * jnp.stack(outs), axis=0)
    return o.reshape(b, s, h * hd).astype(q.dtype)


def moe_ffn(h, w_router, b_router, w1, b1, w2, b2):
    bsz, s, d = h.shape
    n = bsz * s
    g = MOE_ROW_BLOCK
    hf = h.reshape(n, d)
    logits = (hf @ w_router).astype(jnp.float32) + b_router.astype(jnp.float32)
    top_v, top_i = lax.top_k(logits, TOP_K)
    gates = jax.nn.softmax(top_v, axis=-1)
    nk = n * TOP_K
    flat_e = top_i.reshape(nk)
    flat_tok = jnp.repeat(jnp.arange(n, dtype=jnp.int32), TOP_K)
    order = jnp.argsort(flat_e)
    se = flat_e[order]
    stok = flat_tok[order]
    sgate = gates.reshape(nk)[order]
    counts = jnp.bincount(flat_e, length=N_EXPERTS)
    starts = jnp.cumsum(counts) - counts
    pcounts = (counts + g - 1) // g * g
    pends = jnp.cumsum(pcounts)
    pstarts = pends - pcounts
    dest = pstarts[se] + jnp.arange(nk, dtype=jnp.int32) - starts[se]
    n_rows = nk + N_EXPERTS * g
    n_blk = n_rows // g
    row_tok = jnp.zeros((n_rows,), jnp.int32).at[dest].set(stok)
    blk_exp = jnp.minimum(jnp.searchsorted(pends, jnp.arange(n_blk, dtype=jnp.int32) * g, side='right'),
                          N_EXPERTS - 1)
    xb = hf[row_tok].reshape(n_blk, g, d)

    def expert_block(args):
        xg, e = args
        a = xg @ w1[e] + b1[e]
        glu, lin = a[:, :D_FF], a[:, D_FF:]
        glu = jnp.minimum(glu, SWIGLU_LIMIT)
        lin = jnp.clip(lin, -SWIGLU_LIMIT, SWIGLU_LIMIT)
        act = glu * jax.nn.sigmoid(SWIGLU_ALPHA * glu) * (lin + 1)
        return act @ w2[e] + b2[e]

    y = lax.map(expert_block, (xb, blk_exp)).reshape(n_rows, d)
    contrib = y[dest] * sgate[:, None].astype(y.dtype)
    return jax.ops.segment_sum(contrib, stok, num_segments=n).reshape(bsz, s, d)


def setup_inputs(seed: int = 0) -> dict:
    key = jax.random.key(seed)
    ks = jax.random.split(key, 16)
    f32 = jnp.float32
    nrm = lambda k, shape, sc: jax.random.normal(k, shape, f32) * sc
    return {
        "x": jax.random.normal(ks[0], (BATCH, SEQ, D_MODEL), f32),
        "attn_norm_g": 1.0 + nrm(ks[1], (DEPTH, D_MODEL), 0.02),
        "w_in": nrm(ks[2], (DEPTH, D_MODEL, IN_PROJ_W), D_MODEL ** -0.5),
        "a_q_norm_g": 1.0 + nrm(ks[3], (DEPTH, HEAD_DIM), 0.02),
        "a_k_norm_g": 1.0 + nrm(ks[4], (DEPTH, HEAD_DIM), 0.02),
        "a_sinks": nrm(ks[5], (DEPTH, A_Q_HEADS), 0.5),
        "b_q_norm_g": 1.0 + nrm(ks[6], (DEPTH, HEAD_DIM), 0.02),
        "b_k_norm_g": 1.0 + nrm(ks[7], (DEPTH, HEAD_DIM), 0.02),
        "w_out": nrm(ks[8], (DEPTH, MIX_W, D_MODEL), MIX_W ** -0.5),
        "ffn_norm_g": 1.0 + nrm(ks[9], (DEPTH, D_MODEL), 0.02),
        "w_router": nrm(ks[10], (DEPTH, D_MODEL, N_EXPERTS), D_MODEL ** -0.5),
        "b_router": nrm(ks[11], (DEPTH, N_EXPERTS), 0.01),
        "w1": nrm(ks[12], (DEPTH, N_EXPERTS, D_MODEL, 2 * D_FF), D_MODEL ** -0.5),
        "b1": nrm(ks[13], (DEPTH, N_EXPERTS, 2 * D_FF), 0.02),
        "w2": nrm(ks[14], (DEPTH, N_EXPERTS, D_FF, D_MODEL), D_FF ** -0.5),
        "b2": nrm(ks[15], (DEPTH, N_EXPERTS, D_MODEL), 0.02),
    }


def reference(x, attn_norm_g, w_in, a_q_norm_g, a_k_norm_g, a_sinks, b_q_norm_g, b_k_norm_g,
              w_out, ffn_norm_g, w_router, b_router, w1, b1, w2, b2):
    bsz, s, _ = x.shape
    slopes = alibi_slopes(N_ALIBI_HEADS)
    slopes_a = slopes[:A_Q_HEADS]
    slopes_b = slopes[A_Q_HEADS:]
    splits = np.cumsum([A_Q_W, A_KV_W, A_KV_W, B_W, B_W]).tolist()
    for i in range(DEPTH):
        hn = rms_norm(x, attn_norm_g[i])
        proj = hn @ w_in[i]
        qa, ka, va, qb, kb, vb = jnp.split(proj, splits, axis=-1)
        out_a = windowed_sink_gqa(qa.reshape(bsz, s, A_Q_HEADS, HEAD_DIM),
                                  ka.reshape(bsz, s, A_KV_HEADS, HEAD_DIM),
                                  va.reshape(bsz, s, A_KV_HEADS, HEAD_DIM),
                                  a_q_norm_g[i], a_k_norm_g[i], a_sinks[i], slopes_a)
        out_b = dilated_attention(qb.reshape(bsz, s, B_HEADS, HEAD_DIM),
                                  kb.reshape(bsz, s, B_HEADS, HEAD_DIM),
                                  vb.reshape(bsz, s, B_HEADS, HEAD_DIM),
                                  b_q_norm_g[i], b_k_norm_g[i], slopes_b)
        x = x + jnp.concatenate([out_a, out_b], axis=-1) @ w_out[i]
        hf = rms_norm(x, ffn_norm_g[i])
        x = x + moe_ffn(hf, w_router[i], b_router[i], w1[i], b1[i], w2[i], b2[i])
    return x
```

```python
import functools

import jax
import jax.numpy as jnp
import numpy as np
from jax import lax
from jax.experimental import pallas as pl
from jax.experimental.pallas import tpu as pltpu

D_MODEL = 1024
HEAD_DIM = 64
LANES = 128
A_Q_HEADS = 8
A_KV_HEADS = 2
B_HEADS = 8
A_HALF_WINDOW = 128
B_BRANCHES = ((128, 1), (512, 4), (2048, 16))
N_ALIBI_HEADS = 16
A_Q_W = A_Q_HEADS * HEAD_DIM
A_KV_W = A_KV_HEADS * HEAD_DIM
B_W = B_HEADS * HEAD_DIM
N_EXPERTS = 32
TOP_K = 4
D_FF = 1024
SWIGLU_ALPHA = 1.702
SWIGLU_LIMIT = 7.0
NORM_EPS = 1e-5
MASK_VALUE = -1e30

Q_TILE = 128
PROJ_ROWS = 512
MOE_ROWS = 256
FF_CHUNK = 512
VMEM_LIMIT = 48 * 1024 * 1024

A_HEAD_ORDER = (0, 4, 1, 5, 2, 6, 3, 7)


def _alibi_slopes():
    return np.exp2(-8.0 * np.arange(1, N_ALIBI_HEADS + 1, dtype=np.float32) / N_ALIBI_HEADS).astype(np.float32)


def _bias_tables(head_slopes, half_w, dist_scale, tk):
    i = np.arange(Q_TILE)[:, None]
    j = np.arange(tk)[None, :]
    tabs = []
    for shift in (0, half_w, tk - Q_TILE):
        dist = np.abs(j - shift - i)
        valid = dist <= half_w
        per_head = []
        for sl in head_slopes:
            b = -np.float32(sl) * (dist * dist_scale).astype(np.float32)
            per_head.append(np.where(valid, b, np.float32(MASK_VALUE)).astype(np.float32))
        tabs.append(np.stack(per_head))
    return jnp.asarray(np.stack(tabs))


def _head_rms(p, blockdiag, gain):
    sq = p * p
    hi = sq.astype(jnp.bfloat16)
    lo = (sq - hi.astype(jnp.float32)).astype(jnp.bfloat16)
    ss = (jnp.dot(hi, blockdiag, preferred_element_type=jnp.float32)
          + jnp.dot(lo, blockdiag, preferred_element_type=jnp.float32))
    return p * lax.rsqrt(ss * (1.0 / HEAD_DIM) + NORM_EPS) * gain


def _in_proj_kernel(x_ref, g_ref, w_ref, gains_ref, qa_ref, ka_ref, va_ref, qb_ref, kb_ref, vb_ref):
    x = x_ref[...]
    xn = x * lax.rsqrt(jnp.mean(x * x, axis=-1, keepdims=True) + NORM_EPS) * g_ref[...]
    xn = xn.astype(jnp.bfloat16)
    r = lax.broadcasted_iota(jnp.int32, (LANES, LANES), 0) // HEAD_DIM
    c = lax.broadcasted_iota(jnp.int32, (LANES, LANES), 1) // HEAD_DIM
    blockdiag = jnp.where(r == c, 1.0, 0.0).astype(jnp.bfloat16)

    def section(out_ref, col0, width, gain_row):
        sec = jnp.dot(xn, w_ref[:, col0:col0 + width], preferred_element_type=jnp.float32)
        for j in range(width // LANES):
            p = sec[:, j * LANES:(j + 1) * LANES]
            if gain_row is not None:
                p = _head_rms(p, blockdiag, gains_ref[gain_row:gain_row + 1, :])
            out_ref[:, j * LANES:(j + 1) * LANES] = p.astype(out_ref.dtype)

    section(qa_ref, 0, A_Q_W, 0)
    section(ka_ref, A_Q_W, A_KV_W, 1)
    section(va_ref, A_Q_W + A_KV_W, A_KV_W, None)
    section(qb_ref, A_Q_W + 2 * A_KV_W, B_W, 2)
    section(kb_ref, A_Q_W + 2 * A_KV_W + B_W, B_W, 3)
    section(vb_ref, A_Q_W + 2 * A_KV_W + 2 * B_W, B_W, None)


def _in_proj(x2, g, w_bf, gains):
    n = x2.shape[0]
    widths = (A_Q_W, A_KV_W, A_KV_W, B_W, B_W, B_W)
    return pl.pallas_call(
        _in_proj_kernel,
        out_shape=[jax.ShapeDtypeStruct((n, w), jnp.bfloat16) for w in widths],
        grid=(n // PROJ_ROWS,),
        in_specs=[
            pl.BlockSpec((PROJ_ROWS, D_MODEL), lambda i: (i, 0)),
            pl.BlockSpec((1, D_MODEL), lambda i: (0, 0)),
            pl.BlockSpec(w_bf.shape, lambda i: (0, 0)),
            pl.BlockSpec(gains.shape, lambda i: (0, 0)),
        ],
        out_specs=[pl.BlockSpec((PROJ_ROWS, w), lambda i: (i, 0)) for w in widths],
        compiler_params=pltpu.CompilerParams(
            dimension_semantics=("arbitrary",), vmem_limit_bytes=VMEM_LIMIT),
        name="in_proj",
    )(x2, g, w_bf, gains)


def _attn_kernel(*refs, n_chunks, kv_chunks, tk, half_w, seq_len, rows, has_sink, has_prev, want_lse):
    it = iter(refs)
    sink_ref = next(it) if has_sink else None
    q_ref, k_ref, v_ref, bias_ref = next(it), next(it), next(it), next(it)
    po_ref = next(it) if has_prev else None
    pl_ref = next(it) if has_prev else None
    o_ref = next(it)
    lse_ref = next(it) if want_lse else None

    n_tiles = seq_len // Q_TILE
    tiles_per_step = rows // Q_TILE
    step = pl.program_id(2)
    lane = lax.broadcasted_iota(jnp.int32, (Q_TILE, LANES), 1)
    low_half = lane < HEAD_DIM

    for t in range(tiles_per_step):
        tile = step * tiles_per_step + t
        q0 = tile * Q_TILE
        start = jnp.clip(q0 - half_w, 0, seq_len - tk)
        start = pl.multiple_of(start, HEAD_DIM)
        variant = jnp.where(tile == 0, 0, jnp.where(tile == n_tiles - 1, 2, 1))
        r0 = t * Q_TILE
        lse_tile = jnp.zeros((Q_TILE, LANES), jnp.float32)
        if has_prev:
            prev_lse = pl_ref[0, r0:r0 + Q_TILE, :]
        for c in range(n_chunks):
            ck = c if kv_chunks == n_chunks else 0
            q2 = q_ref[0, r0:r0 + Q_TILE, c * LANES:(c + 1) * LANES]
            kc = k_ref[0, pl.ds(start, tk), ck * LANES:(ck + 1) * LANES]
            vc = v_ref[0, pl.ds(start, tk), ck * LANES:(ck + 1) * LANES]
            if has_prev:
                prev_o = po_ref[0, r0:r0 + Q_TILE, c * LANES:(c + 1) * LANES].astype(jnp.float32)
            halves = []
            for half in range(2):
                h = 2 * c + half
                qm = jnp.where(low_half if half == 0 else ~low_half, q2, jnp.zeros_like(q2))
                s = lax.dot_general(qm, kc, (((1,), (1,)), ((), ())),
                                    preferred_element_type=jnp.float32)
                s = s + bias_ref[variant, h]
                m = jnp.max(s, axis=-1, keepdims=True)
                if has_sink:
                    m = jnp.maximum(m, sink_ref[h])
                p = jnp.exp(s - m)
                l = jnp.sum(p, axis=-1, keepdims=True)
                if has_sink:
                    l = l + jnp.exp(sink_ref[h] - m)
                o = jnp.dot(p.astype(jnp.bfloat16), vc, preferred_element_type=jnp.float32)
                o = o * (1.0 / l)
                if want_lse or has_prev:
                    lse = m + jnp.log(l)
                if has_prev:
                    lp = prev_lse[:, h:h + 1]
                    mx = jnp.maximum(lse, lp)
                    wc = jnp.exp(lse - mx)
                    wp = jnp.exp(lp - mx)
                    tot = wc + wp
                    inv = 1.0 / tot
                    o = o * (wc * inv) + prev_o * (wp * inv)
                    lse = mx + jnp.log(tot)
                if want_lse:
                    lse_tile = jnp.where(lane == h, lse, lse_tile)
                halves.append(o)
            o2 = jnp.where(low_half, halves[0], halves[1])
            o_ref[0, r0:r0 + Q_TILE, c * LANES:(c + 1) * LANES] = o2.astype(o_ref.dtype)
        if want_lse:
            lse_ref[0, r0:r0 + Q_TILE, :] = lse_tile


def _banded_attention(q, k, v, bias, *, batch, seq, dil, half_w, sink=None, prev=None, want_lse=False):
    n_chunks = q.shape[1] // LANES
    kv_chunks = k.shape[1] // LANES
    L = seq // dil
    tk = Q_TILE + 2 * half_w
    rows = min(512, L)
    qw, kw = q.shape[1], k.shape[1]

    def view(a):
        return a.reshape(batch, L, dil * a.shape[1])

    args, in_specs = [], []
    if sink is not None:
        args.append(sink)
        in_specs.append(pl.BlockSpec(memory_space=pltpu.SMEM))
    args += [view(q), view(k), view(v), bias]
    in_specs += [
        pl.BlockSpec((1, rows, qw), lambda b, r, i: (b, i, r)),
        pl.BlockSpec((1, L, kw), lambda b, r, i: (b, 0, r)),
        pl.BlockSpec((1, L, kw), lambda b, r, i: (b, 0, r)),
        pl.BlockSpec(bias.shape, lambda b, r, i: (0, 0, 0, 0)),
    ]
    if prev is not None:
        args += [view(prev[0]), view(prev[1])]
        in_specs += [
            pl.BlockSpec((1, rows, qw), lambda b, r, i: (b, i, r)),
            pl.BlockSpec((1, rows, LANES), lambda b, r, i: (b, i, r)),
        ]
    out_shape = [jax.ShapeDtypeStruct((batch, L, dil * qw), jnp.bfloat16)]
    out_specs = [pl.BlockSpec((1, rows, qw), lambda b, r, i: (b, i, r))]
    if want_lse:
        out_shape.append(jax.ShapeDtypeStruct((batch, L, dil * LANES), jnp.float32))
        out_specs.append(pl.BlockSpec((1, rows, LANES), lambda b, r, i: (b, i, r)))

    kern = functools.partial(
        _attn_kernel, n_chunks=n_chunks, kv_chunks=kv_chunks, tk=tk, half_w=half_w, seq_len=L,
        rows=rows, has_sink=sink is not None, has_prev=prev is not None, want_lse=want_lse)
    outs = pl.pallas_call(
        kern,
        out_shape=out_shape,
        grid=(batch, dil, L // rows),
        in_specs=in_specs,
        out_specs=out_specs,
        compiler_params=pltpu.CompilerParams(
            dimension_semantics=("arbitrary", "arbitrary", "arbitrary"),
            vmem_limit_bytes=VMEM_LIMIT),
        name=f"banded_attn_d{dil}",
    )(*args)
    o = outs[0].reshape(batch * seq, qw)
    if want_lse:
        return o, outs[1].reshape(batch * seq, LANES)
    return o


def _out_proj_router_kernel(oa_ref, ob_ref, x_ref, wo_ref, g_ref, wr_hi_ref, wr_lo_ref, br_ref,
                            x1_ref, hf_ref, topi_ref, gate_ref, rank_ref, cnt_ref,
                            tri_ref, carry_ref):
    i = pl.program_id(0)
    rows = x_ref.shape[0]

    @pl.when(i == 0)
    def _():
        a = lax.broadcasted_iota(jnp.int32, (rows, rows), 0)
        b = lax.broadcasted_iota(jnp.int32, (rows, rows), 1)
        tri_ref[...] = jnp.where(a <= b, 1.0, 0.0).astype(jnp.bfloat16)
        carry_ref[...] = jnp.zeros_like(carry_ref)

    attn = jnp.concatenate([oa_ref[...], ob_ref[...]], axis=-1)
    x1 = x_ref[...] + jnp.dot(attn, wo_ref[...], preferred_element_type=jnp.float32)
    x1_ref[...] = x1
    hf = x1 * lax.rsqrt(jnp.mean(x1 * x1, axis=-1, keepdims=True) + NORM_EPS) * g_ref[...]
    hf_hi = hf.astype(jnp.bfloat16)
    hf_ref[...] = hf_hi
    hf_lo = (hf - hf_hi.astype(jnp.float32)).astype(jnp.bfloat16)

    nt = (((1,), (1,)), ((), ()))
    logits = (lax.dot_general(wr_hi_ref[...], hf_hi, nt, preferred_element_type=jnp.float32)
              + lax.dot_general(wr_lo_ref[...], hf_hi, nt, preferred_element_type=jnp.float32)
              + lax.dot_general(wr_hi_ref[...], hf_lo, nt, preferred_element_type=jnp.float32))
    logits = logits + br_ref[:, 0:1]

    eidx = lax.broadcasted_iota(jnp.int32, (N_EXPERTS, rows), 0)
    work = logits
    vals, sels = [], []
    for k in range(TOP_K):
        mk = jnp.max(work, axis=0, keepdims=True)
        ik = jnp.min(jnp.where(work == mk, eidx, N_EXPERTS), axis=0, keepdims=True)
        sel = eidx == ik
        work = jnp.where(sel, -jnp.inf, work)
        vals.append(mk)
        sels.append(sel)
        topi_ref[k:k + 1, :] = ik
    exps = [jnp.exp(vk - vals[0]) for vk in vals]
    denom = exps[0] + exps[1] + exps[2] + exps[3]
    inv = 1.0 / denom
    for k in range(TOP_K):
        gate_ref[k:k + 1, :] = exps[k] * inv

    onehot = jnp.zeros((N_EXPERTS, rows), jnp.float32)
    for sel in sels:
        onehot = onehot + jnp.where(sel, 1.0, 0.0)
    incl = jnp.dot(onehot.astype(jnp.bfloat16), tri_ref[...], preferred_element_type=jnp.float32)
    before = incl - onehot + carry_ref[:, 0:1]
    for k in range(TOP_K):
        rk = jnp.sum(jnp.where(sels[k], before, 0.0), axis=0, keepdims=True)
        rank_ref[k:k + 1, :] = rk.astype(jnp.int32)
    carry = carry_ref[...] + jnp.sum(onehot, axis=1, keepdims=True)
    carry_ref[...] = carry
    cnt_ref[...] = carry.astype(jnp.int32)


def _out_proj_router(oa, ob, x2, wo_bf, g, wr_hi, wr_lo, br):
    n = x2.shape[0]
    rows = PROJ_ROWS
    row_spec = lambda w: pl.BlockSpec((rows, w), lambda i: (i, 0))
    full = lambda a: pl.BlockSpec(a.shape, lambda i: (0,) * a.ndim)
    col_spec = pl.BlockSpec((TOP_K, rows), lambda i: (0, i))
    return pl.pallas_call(
        _out_proj_router_kernel,
        out_shape=[
            jax.ShapeDtypeStruct((n, D_MODEL), jnp.float32),
            jax.ShapeDtypeStruct((n, D_MODEL), jnp.bfloat16),
            jax.ShapeDtypeStruct((TOP_K, n), jnp.int32),
            jax.ShapeDtypeStruct((TOP_K, n), jnp.float32),
            jax.ShapeDtypeStruct((TOP_K, n), jnp.int32),
            jax.ShapeDtypeStruct((N_EXPERTS, LANES), jnp.int32),
        ],
        grid=(n // rows,),
        in_specs=[row_spec(A_Q_W), row_spec(B_W), row_spec(D_MODEL), full(wo_bf), full(g),
                  full(wr_hi), full(wr_lo), full(br)],
        out_specs=[row_spec(D_MODEL), row_spec(D_MODEL), col_spec, col_spec, col_spec,
                   pl.BlockSpec((N_EXPERTS, LANES), lambda i: (0, 0))],
        scratch_shapes=[pltpu.VMEM((rows, rows), jnp.bfloat16),
                        pltpu.VMEM((N_EXPERTS, LANES), jnp.float32)],
        compiler_params=pltpu.CompilerParams(
            dimension_semantics=("arbitrary",), vmem_limit_bytes=VMEM_LIMIT),
        name="out_proj_router",
    )(oa, ob, x2, wo_bf, g, wr_hi, wr_lo, br)


def _moe_kernel(blk_exp_ref, n_used_ref, x_ref, w1_ref, b1_ref, w2_ref, b2_ref, y_ref):
    i = pl.program_id(0)

    @pl.when(i < n_used_ref[0])
    def _():
        x = x_ref[...]
        acc = jnp.zeros((MOE_ROWS, D_MODEL), jnp.float32)
        for c in range(D_FF // FF_CHUNK):
            lo = c * FF_CHUNK
            glu = jnp.dot(x, w1_ref[0, :, lo:lo + FF_CHUNK], preferred_element_type=jnp.float32)
            glu = glu + b1_ref[0, :, lo:lo + FF_CHUNK]
            lin = jnp.dot(x, w1_ref[0, :, D_FF + lo:D_FF + lo + FF_CHUNK],
                          preferred_element_type=jnp.float32)
            lin = lin + b1_ref[0, :, D_FF + lo:D_FF + lo + FF_CHUNK]
            glu = jnp.minimum(glu, SWIGLU_LIMIT)
            lin = jnp.clip(lin, -SWIGLU_LIMIT, SWIGLU_LIMIT)
            act = glu * (1.0 / (1.0 + jnp.exp(-SWIGLU_ALPHA * glu))) * (lin + 1.0)
            acc = acc + jnp.dot(act.astype(jnp.bfloat16), w2_ref[0, lo:lo + FF_CHUNK, :],
                                preferred_element_type=jnp.float32)
        y_ref[...] = (acc + b2_ref[0]).astype(y_ref.dtype)


def _moe_experts(blk_exp, n_used, xb, w1_bf, b1, w2_bf, b2):
    n_rows = xb.shape[0]
    n_blk = n_rows // MOE_ROWS

    def blk(i, be, nu):
        return jnp.minimum(i, nu[0] - 1)

    grid_spec = pltpu.PrefetchScalarGridSpec(
        num_scalar_prefetch=2,
        grid=(n_blk,),
        in_specs=[
            pl.BlockSpec((MOE_ROWS, D_MODEL), lambda i, be, nu: (blk(i, be, nu), 0)),
            pl.BlockSpec((1, D_MODEL, 2 * D_FF), lambda i, be, nu: (be[blk(i, be, nu)], 0, 0)),
            pl.BlockSpec((1, 1, 2 * D_FF), lambda i, be, nu: (be[blk(i, be, nu)], 0, 0)),
            pl.BlockSpec((1, D_FF, D_MODEL), lambda i, be, nu: (be[blk(i, be, nu)], 0, 0)),
            pl.BlockSpec((1, 1, D_MODEL), lambda i, be, nu: (be[blk(i, be, nu)], 0, 0)),
        ],
        out_specs=pl.BlockSpec((MOE_ROWS, D_MODEL), lambda i, be, nu: (blk(i, be, nu), 0)),
    )
    return pl.pallas_call(
        _moe_kernel,
        out_shape=jax.ShapeDtypeStruct((n_rows, D_MODEL), jnp.bfloat16),
        grid_spec=grid_spec,
        compiler_params=pltpu.CompilerParams(
            dimension_semantics=("arbitrary",), vmem_limit_bytes=VMEM_LIMIT),
        name="moe_experts",
    )(blk_exp, n_used, xb, w1_bf, b1, w2_bf, b2)


def _combine_kernel(x1_ref, yg_ref, gate_ref, o_ref):
    acc = x1_ref[...]
    for k in range(TOP_K):
        acc = acc + yg_ref[k].astype(jnp.float32) * gate_ref[:, k:k + 1]
    o_ref[...] = acc


def _combine(x1, yg, gates_nk):
    n = x1.shape[0]
    rows = PROJ_ROWS
    return pl.pallas_call(
        _combine_kernel,
        out_shape=jax.ShapeDtypeStruct((n, D_MODEL), jnp.float32),
        grid=(n // rows,),
        in_specs=[pl.BlockSpec((rows, D_MODEL), lambda i: (i, 0)),
                  pl.BlockSpec((TOP_K, rows, D_MODEL), lambda i: (0, i, 0)),
                  pl.BlockSpec((rows, TOP_K), lambda i: (i, 0))],
        out_specs=pl.BlockSpec((rows, D_MODEL), lambda i: (i, 0)),
        compiler_params=pltpu.CompilerParams(
            dimension_semantics=("arbitrary",), vmem_limit_bytes=VMEM_LIMIT),
        name="moe_combine",
    )(x1, yg, gates_nk)


def _layer(x2, batch, seq, attn_norm_g, w_in, a_q_g, a_k_g, a_sinks, b_q_g, b_k_g, w_out,
           ffn_norm_g, w_router, b_router, w1, b1, w2, b2):
    n = x2.shape[0]
    slopes = _alibi_slopes()
    a_order = np.asarray(A_HEAD_ORDER)

    a_cols = (a_order[:, None] * HEAD_DIM + np.arange(HEAD_DIM)[None, :]).reshape(-1)
    col_perm = np.concatenate([a_cols, np.arange(A_Q_W, w_in.shape[1])])
    w_in_bf = w_in[:, col_perm].astype(jnp.bfloat16)
    row_perm = np.concatenate([a_cols, np.arange(A_Q_W, A_Q_W + B_W)])
    w_out_bf = w_out[row_perm].astype(jnp.bfloat16)
    q_scale = HEAD_DIM ** -0.5
    gains = jnp.stack([jnp.tile(a_q_g, 2) * q_scale, jnp.tile(a_k_g, 2),
                       jnp.tile(b_q_g, 2) * q_scale, jnp.tile(b_k_g, 2)]).astype(jnp.float32)

    qa, ka, va, qb, kb, vb = _in_proj(x2, attn_norm_g.reshape(1, -1), w_in_bf, gains)

    bias_a = _bias_tables(slopes[a_order], A_HALF_WINDOW, 1, Q_TILE + 2 * A_HALF_WINDOW)
    out_a = _banded_attention(qa, ka, va, bias_a, batch=batch, seq=seq, dil=1,
                              half_w=A_HALF_WINDOW, sink=a_sinks[a_order].astype(jnp.float32))

    prev = None
    for bi, (window, dil) in enumerate(B_BRANCHES):
        half_w = window // (2 * dil)
        bias_b = _bias_tables(slopes[A_Q_HEADS:], half_w, dil, Q_TILE + 2 * half_w)
        last = bi == len(B_BRANCHES) - 1
        res = _banded_attention(qb, kb, vb, bias_b, batch=batch, seq=seq, dil=dil, half_w=half_w,
                                prev=prev, want_lse=not last)
        prev = res
    out_b = prev

    wr_t = w_router.T.astype(jnp.float32)
    wr_hi = wr_t.astype(jnp.bfloat16)
    wr_lo = (wr_t - wr_hi.astype(jnp.float32)).astype(jnp.bfloat16)
    br = jnp.broadcast_to(b_router.astype(jnp.float32)[:, None], (N_EXPERTS, LANES))
    x1, hf, topi, gates, ranks, counts = _out_proj_router(
        out_a, out_b, x2, w_out_bf, ffn_norm_g.reshape(1, -1), wr_hi, wr_lo, br)

    g = MOE_ROWS
    nk = n * TOP_K
    n_rows = nk + N_EXPERTS * g
    cnt = counts[:, 0]
    pcnt = (cnt + g - 1) // g * g
    pends = jnp.cumsum(pcnt)
    pstarts = pends - pcnt
    dest = pstarts[topi] + ranks
    n_blk = n_rows // g
    blk_exp = jnp.minimum(jnp.searchsorted(pends, jnp.arange(n_blk, dtype=jnp.int32) * g, side='right'),
                          N_EXPERTS - 1).astype(jnp.int32)
    n_used = (pends[-1] // g).astype(jnp.int32).reshape(1)

    tok = jnp.broadcast_to(jnp.arange(n, dtype=jnp.int32)[None, :], (TOP_K, n))
    row_tok = jnp.zeros((n_rows,), jnp.int32).at[dest.reshape(-1)].set(tok.reshape(-1))
    xb = hf[row_tok]

    y = _moe_experts(blk_exp, n_used, xb, w1.astype(jnp.bfloat16), b1[:, None, :],
                     w2.astype(jnp.bfloat16), b2[:, None, :])
    yg = y[dest]
    return _combine(x1, yg, gates.T)


def kernel(x, attn_norm_g, w_in, a_q_norm_g, a_k_norm_g, a_sinks, b_q_norm_g, b_k_norm_g, w_out,
           ffn_norm_g, w_router, b_router, w1, b1, w2, b2):
    batch, seq, d = x.shape
    x2 = x.reshape(batch * seq, d)
    for i in range(attn_norm_g.shape[0]):
        x2 = _layer(x2, batch, seq, attn_norm_g[i], w_in[i], a_q_norm_g[i], a_k_norm_g[i],
                    a_sinks[i], b_q_norm_g[i], b_k_norm_g[i], w_out[i], ffn_norm_g[i],
                    w_router[i], b_router[i], w1[i], b1[i], w2[i], b2[i])
    return x2.reshape(batch, seq, d)
```

```python
import functools

import jax
import jax.numpy as jnp
import numpy as np
from jax import lax
from jax.experimental import pallas as pl
from jax.experimental.pallas import tpu as pltpu
from jax.experimental.pallas import tpu_sc as plsc

D_MODEL = 1024
HALF_D = D_MODEL // 2
HEAD_DIM = 64
LANES = 128
A_Q_HEADS = 8
A_KV_HEADS = 2
B_HEADS = 8
A_HALF_WINDOW = 128
B_BRANCHES = ((128, 1), (512, 4), (2048, 16))
N_ALIBI_HEADS = 16
A_Q_W = A_Q_HEADS * HEAD_DIM
A_KV_W = A_KV_HEADS * HEAD_DIM
B_W = B_HEADS * HEAD_DIM
N_EXPERTS = 32
TOP_K = 4
D_FF = 1024
SWIGLU_ALPHA = 1.702
SWIGLU_LIMIT = 7.0
NORM_EPS = 1e-5
MASK_VALUE = -1e30

Q_TILE = 128
PROJ_ROWS = 512
MOE_ROWS = 256
FF_CHUNK = 512
VMEM_LIMIT = 48 * 1024 * 1024
SC_CORES = 2
SC_SUBCORES = 16
SC_WORKERS = SC_CORES * SC_SUBCORES
SC_CHUNK = 64


def _pack_halves(v):
    lo = v[:, :HALF_D].astype(jnp.bfloat16).astype(jnp.float32)
    hi = v[:, HALF_D:].astype(jnp.bfloat16).astype(jnp.float32)
    return (pltpu.bitcast(lo, jnp.uint32) >> 16) | pltpu.bitcast(hi, jnp.uint32)


def _unpack_halves(w):
    lo = pltpu.bitcast(w << 16, jnp.float32)
    hi = pltpu.bitcast(w & jnp.uint32(0xFFFF0000), jnp.float32)
    return lo, hi

A_HEAD_ORDER = (0, 4, 1, 5, 2, 6, 3, 7)


def _alibi_slopes():
    return np.exp2(-8.0 * np.arange(1, N_ALIBI_HEADS + 1, dtype=np.float32) / N_ALIBI_HEADS).astype(np.float32)


def _bias_tables(head_slopes, half_w, dist_scale, tk):
    i = np.arange(Q_TILE)[:, None]
    j = np.arange(tk)[None, :]
    tabs = []
    for shift in (0, half_w, tk - Q_TILE):
        dist = np.abs(j - shift - i)
        valid = dist <= half_w
        per_head = []
        for sl in head_slopes:
            b = -np.float32(sl) * (dist * dist_scale).astype(np.float32)
            per_head.append(np.where(valid, b, np.float32(MASK_VALUE)).astype(np.float32))
        tabs.append(np.stack(per_head))
    return jnp.asarray(np.stack(tabs))


def _head_rms(p, blockdiag, gain):
    sq = p * p
    hi = sq.astype(jnp.bfloat16)
    lo = (sq - hi.astype(jnp.float32)).astype(jnp.bfloat16)
    ss = (jnp.dot(hi, blockdiag, preferred_element_type=jnp.float32)
          + jnp.dot(lo, blockdiag, preferred_element_type=jnp.float32))
    return p * lax.rsqrt(ss * (1.0 / HEAD_DIM) + NORM_EPS) * gain


def _in_proj_kernel(x_ref, g_ref, w_ref, gains_ref, qa_ref, ka_ref, va_ref, qb_ref, kb_ref, vb_ref):
    x = x_ref[...]
    xn = x * lax.rsqrt(jnp.mean(x * x, axis=-1, keepdims=True) + NORM_EPS) * g_ref[...]
    xn = xn.astype(jnp.bfloat16)
    r = lax.broadcasted_iota(jnp.int32, (LANES, LANES), 0) // HEAD_DIM
    c = lax.broadcasted_iota(jnp.int32, (LANES, LANES), 1) // HEAD_DIM
    blockdiag = jnp.where(r == c, 1.0, 0.0).astype(jnp.bfloat16)

    def section(out_ref, col0, width, gain_row):
        sec = jnp.dot(xn, w_ref[:, col0:col0 + width], preferred_element_type=jnp.float32)
        for j in range(width // LANES):
            p = sec[:, j * LANES:(j + 1) * LANES]
            if gain_row is not None:
                p = _head_rms(p, blockdiag, gains_ref[gain_row:gain_row + 1, :])
            out_ref[:, j * LANES:(j + 1) * LANES] = p.astype(out_ref.dtype)

    section(qa_ref, 0, A_Q_W, 0)
    section(ka_ref, A_Q_W, A_KV_W, 1)
    section(va_ref, A_Q_W + A_KV_W, A_KV_W, None)
    section(qb_ref, A_Q_W + 2 * A_KV_W, B_W, 2)
    section(kb_ref, A_Q_W + 2 * A_KV_W + B_W, B_W, 3)
    section(vb_ref, A_Q_W + 2 * A_KV_W + 2 * B_W, B_W, None)


def _in_proj(x2, g, w_bf, gains):
    n = x2.shape[0]
    widths = (A_Q_W, A_KV_W, A_KV_W, B_W, B_W, B_W)
    return pl.pallas_call(
        _in_proj_kernel,
        out_shape=[jax.ShapeDtypeStruct((n, w), jnp.bfloat16) for w in widths],
        grid=(n // PROJ_ROWS,),
        in_specs=[
            pl.BlockSpec((PROJ_ROWS, D_MODEL), lambda i: (i, 0)),
            pl.BlockSpec((1, D_MODEL), lambda i: (0, 0)),
            pl.BlockSpec(w_bf.shape, lambda i: (0, 0)),
            pl.BlockSpec(gains.shape, lambda i: (0, 0)),
        ],
        out_specs=[pl.BlockSpec((PROJ_ROWS, w), lambda i: (i, 0)) for w in widths],
        compiler_params=pltpu.CompilerParams(
            dimension_semantics=("arbitrary",), vmem_limit_bytes=VMEM_LIMIT),
        name="in_proj",
    )(x2, g, w_bf, gains)


def _attn_kernel(*refs, n_chunks, kv_chunks, tk, half_w, seq_len, rows, has_sink, has_prev, want_lse):
    it = iter(refs)
    sink_ref = next(it) if has_sink else None
    q_ref, k_ref, v_ref, bias_ref = next(it), next(it), next(it), next(it)
    po_ref = next(it) if has_prev else None
    pl_ref = next(it) if has_prev else None
    o_ref = next(it)
    lse_ref = next(it) if want_lse else None

    n_tiles = seq_len // Q_TILE
    tiles_per_step = rows // Q_TILE
    step = pl.program_id(2)
    lane = lax.broadcasted_iota(jnp.int32, (Q_TILE, LANES), 1)
    low_half = lane < HEAD_DIM

    for t in range(tiles_per_step):
        tile = step * tiles_per_step + t
        q0 = tile * Q_TILE
        start = jnp.clip(q0 - half_w, 0, seq_len - tk)
        start = pl.multiple_of(start, HEAD_DIM)
        variant = jnp.where(tile == 0, 0, jnp.where(tile == n_tiles - 1, 2, 1))
        r0 = t * Q_TILE
        lse_tile = jnp.zeros((Q_TILE, LANES), jnp.float32)
        if has_prev:
            prev_lse = pl_ref[0, r0:r0 + Q_TILE, :]
        for c in range(n_chunks):
            ck = c if kv_chunks == n_chunks else 0
            q2 = q_ref[0, r0:r0 + Q_TILE, c * LANES:(c + 1) * LANES]
            kc = k_ref[0, pl.ds(start, tk), ck * LANES:(ck + 1) * LANES]
            vc = v_ref[0, pl.ds(start, tk), ck * LANES:(ck + 1) * LANES]
            if has_prev:
                prev_o = po_ref[0, r0:r0 + Q_TILE, c * LANES:(c + 1) * LANES].astype(jnp.float32)
            halves = []
            for half in range(2):
                h = 2 * c + half
                qm = jnp.where(low_half if half == 0 else ~low_half, q2, jnp.zeros_like(q2))
                s = lax.dot_general(qm, kc, (((1,), (1,)), ((), ())),
                                    preferred_element_type=jnp.float32)
                s = s + bias_ref[variant, h]
                m = jnp.max(s, axis=-1, keepdims=True)
                if has_sink:
                    m = jnp.maximum(m, sink_ref[h])
                p = jnp.exp(s - m)
                l = jnp.sum(p, axis=-1, keepdims=True)
                if has_sink:
                    l = l + jnp.exp(sink_ref[h] - m)
                o = jnp.dot(p.astype(jnp.bfloat16), vc, preferred_element_type=jnp.float32)
                o = o * (1.0 / l)
                if want_lse or has_prev:
                    lse = m + jnp.log(l)
                if has_prev:
                    lp = prev_lse[:, h:h + 1]
                    mx = jnp.maximum(lse, lp)
                    wc = jnp.exp(lse - mx)
                    wp = jnp.exp(lp - mx)
                    tot = wc + wp
                    inv = 1.0 / tot
                    o = o * (wc * inv) + prev_o * (wp * inv)
                    lse = mx + jnp.log(tot)
                if want_lse:
                    lse_tile = jnp.where(lane == h, lse, lse_tile)
                halves.append(o)
            o2 = jnp.where(low_half, halves[0], halves[1])
            o_ref[0, r0:r0 + Q_TILE, c * LANES:(c + 1) * LANES] = o2.astype(o_ref.dtype)
        if want_lse:
            lse_ref[0, r0:r0 + Q_TILE, :] = lse_tile


def _banded_attention(q, k, v, bias, *, batch, seq, dil, half_w, sink=None, prev=None, want_lse=False):
    n_chunks = q.shape[1] // LANES
    kv_chunks = k.shape[1] // LANES
    L = seq // dil
    tk = Q_TILE + 2 * half_w
    rows = min(512, L)
    qw, kw = q.shape[1], k.shape[1]

    def view(a):
        return a.reshape(batch, L, dil * a.shape[1])

    args, in_specs = [], []
    if sink is not None:
        args.append(sink)
        in_specs.append(pl.BlockSpec(memory_space=pltpu.SMEM))
    args += [view(q), view(k), view(v), bias]
    in_specs += [
        pl.BlockSpec((1, rows, qw), lambda b, r, i: (b, i, r)),
        pl.BlockSpec((1, L, kw), lambda b, r, i: (b, 0, r)),
        pl.BlockSpec((1, L, kw), lambda b, r, i: (b, 0, r)),
        pl.BlockSpec(bias.shape, lambda b, r, i: (0, 0, 0, 0)),
    ]
    if prev is not None:
        args += [view(prev[0]), view(prev[1])]
        in_specs += [
            pl.BlockSpec((1, rows, qw), lambda b, r, i: (b, i, r)),
            pl.BlockSpec((1, rows, LANES), lambda b, r, i: (b, i, r)),
        ]
    out_shape = [jax.ShapeDtypeStruct((batch, L, dil * qw), jnp.bfloat16)]
    out_specs = [pl.BlockSpec((1, rows, qw), lambda b, r, i: (b, i, r))]
    if want_lse:
        out_shape.append(jax.ShapeDtypeStruct((batch, L, dil * LANES), jnp.float32))
        out_specs.append(pl.BlockSpec((1, rows, LANES), lambda b, r, i: (b, i, r)))

    kern = functools.partial(
        _attn_kernel, n_chunks=n_chunks, kv_chunks=kv_chunks, tk=tk, half_w=half_w, seq_len=L,
        rows=rows, has_sink=sink is not None, has_prev=prev is not None, want_lse=want_lse)
    outs = pl.pallas_call(
        kern,
        out_shape=out_shape,
        grid=(batch, dil, L // rows),
        in_specs=in_specs,
        out_specs=out_specs,
        compiler_params=pltpu.CompilerParams(
            dimension_semantics=("arbitrary", "arbitrary", "arbitrary"),
            vmem_limit_bytes=VMEM_LIMIT),
        name=f"banded_attn_d{dil}",
    )(*args)
    o = outs[0].reshape(batch * seq, qw)
    if want_lse:
        return o, outs[1].reshape(batch * seq, LANES)
    return o


def _out_proj_router_kernel(oa_ref, ob_ref, x_ref, wo_ref, g_ref, wr_hi_ref, wr_lo_ref, br_ref,
                            x1_ref, hf_ref, topi_ref, gate_ref, rank_ref, cnt_ref,
                            tri_ref, carry_ref):
    i = pl.program_id(0)
    rows = x_ref.shape[0]

    @pl.when(i == 0)
    def _():
        a = lax.broadcasted_iota(jnp.int32, (rows, rows), 0)
        b = lax.broadcasted_iota(jnp.int32, (rows, rows), 1)
        tri_ref[...] = jnp.where(a <= b, 1.0, 0.0).astype(jnp.bfloat16)
        carry_ref[...] = jnp.zeros_like(carry_ref)

    attn = jnp.concatenate([oa_ref[...], ob_ref[...]], axis=-1)
    x1 = x_ref[...] + jnp.dot(attn, wo_ref[...], preferred_element_type=jnp.float32)
    x1_ref[...] = x1
    hf = x1 * lax.rsqrt(jnp.mean(x1 * x1, axis=-1, keepdims=True) + NORM_EPS) * g_ref[...]
    hf_hi = hf.astype(jnp.bfloat16)
    hf_ref[...] = _pack_halves(hf)
    hf_lo = (hf - hf_hi.astype(jnp.float32)).astype(jnp.bfloat16)

    nt = (((1,), (1,)), ((), ()))
    logits = (lax.dot_general(wr_hi_ref[...], hf_hi, nt, preferred_element_type=jnp.float32)
              + lax.dot_general(wr_lo_ref[...], hf_hi, nt, preferred_element_type=jnp.float32)
              + lax.dot_general(wr_hi_ref[...], hf_lo, nt, preferred_element_type=jnp.float32))
    logits = logits + br_ref[:, 0:1]

    eidx = lax.broadcasted_iota(jnp.int32, (N_EXPERTS, rows), 0)
    work = logits
    vals, sels = [], []
    for k in range(TOP_K):
        mk = jnp.max(work, axis=0, keepdims=True)
        ik = jnp.min(jnp.where(work == mk, eidx, N_EXPERTS), axis=0, keepdims=True)
        sel = eidx == ik
        work = jnp.where(sel, -jnp.inf, work)
        vals.append(mk)
        sels.append(sel)
        topi_ref[k:k + 1, :] = ik
    exps = [jnp.exp(vk - vals[0]) for vk in vals]
    denom = exps[0] + exps[1] + exps[2] + exps[3]
    inv = 1.0 / denom
    for k in range(TOP_K):
        gate_ref[k:k + 1, :] = exps[k] * inv

    onehot = jnp.zeros((N_EXPERTS, rows), jnp.float32)
    for sel in sels:
        onehot = onehot + jnp.where(sel, 1.0, 0.0)
    incl = jnp.dot(onehot.astype(jnp.bfloat16), tri_ref[...], preferred_element_type=jnp.float32)
    before = incl - onehot + carry_ref[:, 0:1]
    for k in range(TOP_K):
        rk = jnp.sum(jnp.where(sels[k], before, 0.0), axis=0, keepdims=True)
        rank_ref[k:k + 1, :] = rk.astype(jnp.int32)
    carry = carry_ref[...] + jnp.sum(onehot, axis=1, keepdims=True)
    carry_ref[...] = carry
    cnt_ref[...] = carry.astype(jnp.int32)


def _out_proj_router(oa, ob, x2, wo_bf, g, wr_hi, wr_lo, br):
    n = x2.shape[0]
    rows = PROJ_ROWS
    row_spec = lambda w: pl.BlockSpec((rows, w), lambda i: (i, 0))
    full = lambda a: pl.BlockSpec(a.shape, lambda i: (0,) * a.ndim)
    col_spec = pl.BlockSpec((TOP_K, rows), lambda i: (0, i))
    return pl.pallas_call(
        _out_proj_router_kernel,
        out_shape=[
            jax.ShapeDtypeStruct((n, D_MODEL), jnp.float32),
            jax.ShapeDtypeStruct((n, HALF_D), jnp.uint32),
            jax.ShapeDtypeStruct((TOP_K, n), jnp.int32),
            jax.ShapeDtypeStruct((TOP_K, n), jnp.float32),
            jax.ShapeDtypeStruct((TOP_K, n), jnp.int32),
            jax.ShapeDtypeStruct((N_EXPERTS, LANES), jnp.int32),
        ],
        grid=(n // rows,),
        in_specs=[row_spec(A_Q_W), row_spec(B_W), row_spec(D_MODEL), full(wo_bf), full(g),
                  full(wr_hi), full(wr_lo), full(br)],
        out_specs=[row_spec(D_MODEL), row_spec(HALF_D), col_spec, col_spec, col_spec,
                   pl.BlockSpec((N_EXPERTS, LANES), lambda i: (0, 0))],
        scratch_shapes=[pltpu.VMEM((rows, rows), jnp.bfloat16),
                        pltpu.VMEM((N_EXPERTS, LANES), jnp.float32)],
        compiler_params=pltpu.CompilerParams(
            dimension_semantics=("arbitrary",), vmem_limit_bytes=VMEM_LIMIT),
        name="out_proj_router",
    )(oa, ob, x2, wo_bf, g, wr_hi, wr_lo, br)


def _moe_kernel(blk_exp_ref, n_used_ref, x_ref, w1_ref, b1_ref, w2_ref, b2_ref, y_ref):
    i = pl.program_id(0)

    @pl.when(i < n_used_ref[0])
    def _():
        x = jnp.concatenate(_unpack_halves(x_ref[...]), axis=-1).astype(jnp.bfloat16)
        acc = jnp.zeros((MOE_ROWS, D_MODEL), jnp.float32)
        for c in range(D_FF // FF_CHUNK):
            lo = c * FF_CHUNK
            glu = jnp.dot(x, w1_ref[0, :, lo:lo + FF_CHUNK], preferred_element_type=jnp.float32)
            glu = glu + b1_ref[0, :, lo:lo + FF_CHUNK]
            lin = jnp.dot(x, w1_ref[0, :, D_FF + lo:D_FF + lo + FF_CHUNK],
                          preferred_element_type=jnp.float32)
            lin = lin + b1_ref[0, :, D_FF + lo:D_FF + lo + FF_CHUNK]
            glu = jnp.minimum(glu, SWIGLU_LIMIT)
            lin = jnp.clip(lin, -SWIGLU_LIMIT, SWIGLU_LIMIT)
            act = glu * (1.0 / (1.0 + jnp.exp(-SWIGLU_ALPHA * glu))) * (lin + 1.0)
            acc = acc + jnp.dot(act.astype(jnp.bfloat16), w2_ref[0, lo:lo + FF_CHUNK, :],
                                preferred_element_type=jnp.float32)
        y_ref[...] = _pack_halves(acc + b2_ref[0])


def _moe_experts(blk_exp, n_used, xb, w1_bf, b1, w2_bf, b2):
    n_rows = xb.shape[0]
    n_blk = n_rows // MOE_ROWS

    def blk(i, be, nu):
        return jnp.minimum(i, nu[0] - 1)

    grid_spec = pltpu.PrefetchScalarGridSpec(
        num_scalar_prefetch=2,
        grid=(n_blk,),
        in_specs=[
            pl.BlockSpec((MOE_ROWS, HALF_D), lambda i, be, nu: (blk(i, be, nu), 0)),
            pl.BlockSpec((1, D_MODEL, 2 * D_FF), lambda i, be, nu: (be[blk(i, be, nu)], 0, 0)),
            pl.BlockSpec((1, 1, 2 * D_FF), lambda i, be, nu: (be[blk(i, be, nu)], 0, 0)),
            pl.BlockSpec((1, D_FF, D_MODEL), lambda i, be, nu: (be[blk(i, be, nu)], 0, 0)),
            pl.BlockSpec((1, 1, D_MODEL), lambda i, be, nu: (be[blk(i, be, nu)], 0, 0)),
        ],
        out_specs=pl.BlockSpec((MOE_ROWS, HALF_D), lambda i, be, nu: (blk(i, be, nu), 0)),
    )
    return pl.pallas_call(
        _moe_kernel,
        out_shape=jax.ShapeDtypeStruct((n_rows, HALF_D), jnp.uint32),
        grid_spec=grid_spec,
        compiler_params=pltpu.CompilerParams(
            dimension_semantics=("arbitrary",), vmem_limit_bytes=VMEM_LIMIT),
        name="moe_experts",
    )(blk_exp, n_used, xb, w1_bf, b1, w2_bf, b2)


def _combine_kernel(x1_ref, yg_ref, gate_ref, o_ref):
    acc_lo = x1_ref[:, :HALF_D]
    acc_hi = x1_ref[:, HALF_D:]
    for k in range(TOP_K):
        lo, hi = _unpack_halves(yg_ref[k])
        gk = gate_ref[:, k:k + 1]
        acc_lo = acc_lo + lo * gk
        acc_hi = acc_hi + hi * gk
    o_ref[:, :HALF_D] = acc_lo
    o_ref[:, HALF_D:] = acc_hi


def _combine(x1, yg, gates_nk):
    n = x1.shape[0]
    rows = PROJ_ROWS
    return pl.pallas_call(
        _combine_kernel,
        out_shape=jax.ShapeDtypeStruct((n, D_MODEL), jnp.float32),
        grid=(n // rows,),
        in_specs=[pl.BlockSpec((rows, D_MODEL), lambda i: (i, 0)),
                  pl.BlockSpec((TOP_K, rows, HALF_D), lambda i: (0, i, 0)),
                  pl.BlockSpec((rows, TOP_K), lambda i: (i, 0))],
        out_specs=pl.BlockSpec((rows, D_MODEL), lambda i: (i, 0)),
        compiler_params=pltpu.CompilerParams(
            dimension_semantics=("arbitrary",), vmem_limit_bytes=VMEM_LIMIT),
        name="moe_combine",
    )(x1, yg, gates_nk)


def _sc_worker_id():
    return lax.axis_index("s") * SC_CORES + lax.axis_index("c")


def _sc_dispatch(hf, dest3, n_rows):
    n = hf.shape[0]
    chunks_per_worker = n // SC_CHUNK // SC_WORKERS
    mesh = plsc.VectorSubcoreMesh(core_axis_name="c", subcore_axis_name="s")

    @functools.partial(
        pl.kernel, mesh=mesh,
        out_type=jax.ShapeDtypeStruct((n_rows, HALF_D), hf.dtype),
        scratch_types=[pltpu.VMEM((TOP_K, SC_CHUNK), jnp.int32),
                       pltpu.VMEM((SC_CHUNK, HALF_D), hf.dtype)],
        name="sc_dispatch")
    def run(hf_hbm, dest_hbm, xb_hbm, idx_v, rows_v):
        first = _sc_worker_id() * chunks_per_worker

        @pl.loop(0, chunks_per_worker)
        def _(j):
            ch = first + j
            pltpu.sync_copy(dest_hbm.at[ch], idx_v)
            pltpu.sync_copy(hf_hbm.at[pl.ds(ch * SC_CHUNK, SC_CHUNK)], rows_v)
            for k in range(TOP_K):
                pltpu.sync_copy(rows_v, xb_hbm.at[idx_v.at[k]])

    return run(hf, dest3)


def _sc_collect(y, dest3):
    n = dest3.shape[0] * SC_CHUNK
    chunks_per_worker = n // SC_CHUNK // SC_WORKERS
    mesh = plsc.VectorSubcoreMesh(core_axis_name="c", subcore_axis_name="s")

    @functools.partial(
        pl.kernel, mesh=mesh,
        out_type=jax.ShapeDtypeStruct((TOP_K, n, HALF_D), y.dtype),
        scratch_types=[pltpu.VMEM((TOP_K, SC_CHUNK), jnp.int32),
                       pltpu.VMEM((SC_CHUNK, HALF_D), y.dtype)],
        name="sc_collect")
    def run(y_hbm, dest_hbm, yg_hbm, idx_v, rows_v):
        first = _sc_worker_id() * chunks_per_worker

        @pl.loop(0, chunks_per_worker)
        def _(j):
            ch = first + j
            pltpu.sync_copy(dest_hbm.at[ch], idx_v)
            for k in range(TOP_K):
                pltpu.sync_copy(y_hbm.at[idx_v.at[k]], rows_v)
                pltpu.sync_copy(rows_v, yg_hbm.at[k, pl.ds(ch * SC_CHUNK, SC_CHUNK)])

    return run(y, dest3)


def _layer(x2, batch, seq, attn_norm_g, w_in, a_q_g, a_k_g, a_sinks, b_q_g, b_k_g, w_out,
           ffn_norm_g, w_router, b_router, w1, b1, w2, b2):
    n = x2.shape[0]
    slopes = _alibi_slopes()
    a_order = np.asarray(A_HEAD_ORDER)

    a_cols = (a_order[:, None] * HEAD_DIM + np.arange(HEAD_DIM)[None, :]).reshape(-1)
    col_perm = np.concatenate([a_cols, np.arange(A_Q_W, w_in.shape[1])])
    w_in_bf = w_in[:, col_perm].astype(jnp.bfloat16)
    row_perm = np.concatenate([a_cols, np.arange(A_Q_W, A_Q_W + B_W)])
    w_out_bf = w_out[row_perm].astype(jnp.bfloat16)
    q_scale = HEAD_DIM ** -0.5
    gains = jnp.stack([jnp.tile(a_q_g, 2) * q_scale, jnp.tile(a_k_g, 2),
                       jnp.tile(b_q_g, 2) * q_scale, jnp.tile(b_k_g, 2)]).astype(jnp.float32)

    qa, ka, va, qb, kb, vb = _in_proj(x2, attn_norm_g.reshape(1, -1), w_in_bf, gains)

    bias_a = _bias_tables(slopes[a_order], A_HALF_WINDOW, 1, Q_TILE + 2 * A_HALF_WINDOW)
    out_a = _banded_attention(qa, ka, va, bias_a, batch=batch, seq=seq, dil=1,
                              half_w=A_HALF_WINDOW, sink=a_sinks[a_order].astype(jnp.float32))

    prev = None
    for bi, (window, dil) in enumerate(B_BRANCHES):
        half_w = window // (2 * dil)
        bias_b = _bias_tables(slopes[A_Q_HEADS:], half_w, dil, Q_TILE + 2 * half_w)
        last = bi == len(B_BRANCHES) - 1
        res = _banded_attention(qb, kb, vb, bias_b, batch=batch, seq=seq, dil=dil, half_w=half_w,
                                prev=prev, want_lse=not last)
        prev = res
    out_b = prev

    wr_t = w_router.T.astype(jnp.float32)
    wr_hi = wr_t.astype(jnp.bfloat16)
    wr_lo = (wr_t - wr_hi.astype(jnp.float32)).astype(jnp.bfloat16)
    br = jnp.broadcast_to(b_router.astype(jnp.float32)[:, None], (N_EXPERTS, LANES))
    x1, hf, topi, gates, ranks, counts = _out_proj_router(
        out_a, out_b, x2, w_out_bf, ffn_norm_g.reshape(1, -1), wr_hi, wr_lo, br)

    g = MOE_ROWS
    nk = n * TOP_K
    n_rows = nk + N_EXPERTS * g
    cnt = counts[:, 0]
    pcnt = (cnt + g - 1) // g * g
    pends = jnp.cumsum(pcnt)
    pstarts = pends - pcnt
    experts = jnp.arange(N_EXPERTS, dtype=jnp.int32)
    start_of = jnp.sum(jnp.where(topi[:, :, None] == experts, pstarts, 0), axis=-1)
    dest = (start_of + ranks).astype(jnp.int32)
    n_blk = n_rows // g
    blk_row0 = jnp.arange(n_blk, dtype=jnp.int32) * g
    blk_exp = jnp.minimum(jnp.sum(pends[None, :] <= blk_row0[:, None], axis=-1),
                          N_EXPERTS - 1).astype(jnp.int32)
    n_used = (pends[-1] // g).astype(jnp.int32).reshape(1)
    dest3 = dest.reshape(TOP_K, n // SC_CHUNK, SC_CHUNK).transpose(1, 0, 2)

    xb = _sc_dispatch(hf, dest3, n_rows)
    y = _moe_experts(blk_exp, n_used, xb, w1.astype(jnp.bfloat16), b1[:, None, :],
                     w2.astype(jnp.bfloat16), b2[:, None, :])
    yg = _sc_collect(y, dest3)
    return _combine(x1, yg, gates.T)


def kernel(x, attn_norm_g, w_in, a_q_norm_g, a_k_norm_g, a_sinks, b_q_norm_g, b_k_norm_g, w_out,
           ffn_norm_g, w_router, b_router, w1, b1, w2, b2):
    batch, seq, d = x.shape
    x2 = x.reshape(batch * seq, d)
    for i in range(attn_norm_g.shape[0]):
        x2 = _layer(x2, batch, seq, attn_norm_g[i], w_in[i], a_q_norm_g[i], a_k_norm_g[i],
                    a_sinks[i], b_q_norm_g[i], b_k_norm_g[i], w_out[i], ffn_norm_g[i],
                    w_router[i], b_router[i], w1[i], b1[i], w2[i], b2[i])
    return x2.reshape(batch, seq, d)
```

```python
import functools

import jax
import jax.numpy as jnp
import numpy as np
from jax import lax
from jax.experimental import pallas as pl
from jax.experimental.pallas import tpu as pltpu
from jax.experimental.pallas import tpu_sc as plsc

D_MODEL = 1024
HALF_D = D_MODEL // 2
HEAD_DIM = 64
LANES = 128
MXU_DIM = 256
A_Q_HEADS = 8
A_KV_HEADS = 2
B_HEADS = 8
A_HALF_WINDOW = 128
B_BRANCHES = ((128, 1), (512, 4), (2048, 16))
B_DILS = tuple(d for _, d in B_BRANCHES)
N_ALIBI_HEADS = 16
A_Q_W = A_Q_HEADS * HEAD_DIM
A_KV_W = A_KV_HEADS * HEAD_DIM
B_W = B_HEADS * HEAD_DIM
N_EXPERTS = 32
TOP_K = 4
D_FF = 1024
SWIGLU_ALPHA = 1.702
SWIGLU_LIMIT = 7.0
NORM_EPS = 1e-5
MASK_VALUE = -1e30

Q_TILE = 128
PROJ_ROWS = 512
MOE_ROWS = 256
FF_CHUNK = 512
VMEM_LIMIT = 48 * 1024 * 1024
MOE_VMEM_LIMIT = 58 * 1024 * 1024
SC_CORES = 2
SC_SUBCORES = 16
SC_WORKERS = SC_CORES * SC_SUBCORES
SC_CHUNK = 64

A_HEAD_ORDER = (0, 4, 1, 5, 2, 6, 3, 7)


def _pack_halves(v):
    lo = v[:, :HALF_D].astype(jnp.bfloat16).astype(jnp.float32)
    hi = v[:, HALF_D:].astype(jnp.bfloat16).astype(jnp.float32)
    return (pltpu.bitcast(lo, jnp.uint32) >> 16) | pltpu.bitcast(hi, jnp.uint32)


def _unpack_halves(w):
    lo = pltpu.bitcast(w << 16, jnp.float32)
    hi = pltpu.bitcast(w & jnp.uint32(0xFFFF0000), jnp.float32)
    return lo, hi


def _alibi_slopes():
    return np.exp2(-8.0 * np.arange(1, N_ALIBI_HEADS + 1, dtype=np.float32) / N_ALIBI_HEADS).astype(np.float32)


def _bias_tables(head_slopes, heads_per_group, half_w, dist_scale, tk):
    i = np.arange(Q_TILE)[:, None]
    j = np.arange(tk)[None, :]
    tabs = []
    for shift in (0, half_w, tk - Q_TILE):
        dist = np.abs(j - shift - i)
        valid = dist <= half_w
        per_head = []
        for sl in head_slopes:
            b = -np.float32(sl) * (dist * dist_scale).astype(np.float32)
            per_head.append(np.where(valid, b, np.float32(MASK_VALUE)).astype(np.float32))
        t = np.stack(per_head).reshape(-1, heads_per_group * Q_TILE, tk)
        tabs.append(t)
    return jnp.asarray(np.stack(tabs))


def _in_proj_kernel(x_ref, g_ref, w_ref, gains_ref, qa_ref, ka_ref, va_ref, *rest):
    b_refs, scr_ref = rest[:-1], rest[-1]
    x = x_ref[...]
    xn = x * lax.rsqrt(jnp.mean(x * x, axis=-1, keepdims=True) + NORM_EPS) * g_ref[...]
    xn = xn.astype(jnp.bfloat16)
    r = lax.broadcasted_iota(jnp.int32, (MXU_DIM, MXU_DIM), 0) // HEAD_DIM
    c = lax.broadcasted_iota(jnp.int32, (MXU_DIM, MXU_DIM), 1) // HEAD_DIM
    blockdiag = jnp.where(r == c, 1.0, 0.0).astype(jnp.bfloat16)

    def project(col0, width, gain_row):
        sec = jnp.dot(xn, w_ref[:, col0:col0 + width], preferred_element_type=jnp.float32)
        if gain_row is None:
            return [sec]
        parts = []
        step = min(width, MXU_DIM)
        for j in range(width // step):
            p = sec[:, j * step:(j + 1) * step]
            ss = jnp.dot((p * p).astype(jnp.bfloat16), blockdiag[:step, :step],
                         preferred_element_type=jnp.float32)
            parts.append(p * lax.rsqrt(ss * (1.0 / HEAD_DIM) + NORM_EPS)
                         * gains_ref[gain_row:gain_row + 1, :step])
        return parts

    def store(out_ref, parts):
        w = parts[0].shape[1]
        for j, p in enumerate(parts):
            out_ref[:, j * w:(j + 1) * w] = p.astype(out_ref.dtype)

    store(qa_ref, project(0, A_Q_W, 0))
    store(ka_ref, project(A_Q_W, A_KV_W, 1))
    store(va_ref, project(A_Q_W + A_KV_W, A_KV_W, None))

    rows = x_ref.shape[0]
    col0 = A_Q_W + 2 * A_KV_W
    for t, gain_row in enumerate((2, 3, None)):
        parts = project(col0 + t * B_W, B_W, gain_row)
        sec = jnp.concatenate(parts, axis=-1) if len(parts) > 1 else parts[0]
        for j in range(B_W // LANES):
            scr_ref[j] = sec[:, j * LANES:(j + 1) * LANES]
        for bi, dil in enumerate(B_DILS):
            out_ref = b_refs[t * len(B_DILS) + bi]
            if dil == 1:
                out_ref[...] = sec.astype(out_ref.dtype)
            else:
                for res in range(dil):
                    for j in range(B_W // LANES):
                        out_ref[0, res, :, j * LANES:(j + 1) * LANES] = (
                            scr_ref[j, pl.ds(res, rows // dil, stride=dil), :].astype(out_ref.dtype))


def _in_proj(x2, g, w_bf, gains, batch, seq):
    n = x2.shape[0]
    rows = PROJ_ROWS
    steps = seq // rows
    out_shape = [jax.ShapeDtypeStruct((n, w), jnp.bfloat16) for w in (A_Q_W, A_KV_W, A_KV_W)]
    out_specs = [pl.BlockSpec((rows, w), lambda i: (i, 0)) for w in (A_Q_W, A_KV_W, A_KV_W)]
    for _ in range(3):
        for dil in B_DILS:
            if dil == 1:
                out_shape.append(jax.ShapeDtypeStruct((n, B_W), jnp.bfloat16))
                out_specs.append(pl.BlockSpec((rows, B_W), lambda i: (i, 0)))
            else:
                out_shape.append(jax.ShapeDtypeStruct((batch, dil, seq // dil, B_W), jnp.bfloat16))
                out_specs.append(pl.BlockSpec((1, dil, rows // dil, B_W),
                                              lambda i: (i // steps, 0, i % steps, 0)))
    return pl.pallas_call(
        _in_proj_kernel,
        out_shape=out_shape,
        grid=(n // rows,),
        in_specs=[
            pl.BlockSpec((rows, D_MODEL), lambda i: (i, 0)),
            pl.BlockSpec((1, D_MODEL), lambda i: (0, 0)),
            pl.BlockSpec(w_bf.shape, lambda i: (0, 0)),
            pl.BlockSpec(gains.shape, lambda i: (0, 0)),
        ],
        out_specs=out_specs,
        scratch_shapes=[pltpu.VMEM((B_W // LANES, rows, LANES), jnp.float32)],
        compiler_params=pltpu.CompilerParams(
            dimension_semantics=("arbitrary",), vmem_limit_bytes=VMEM_LIMIT),
        name="in_proj",
    )(x2, g, w_bf, gains)


def _attn_kernel(*refs, n_chunks, kv_chunks, tk, half_w, seq_len, rows, has_sink, want_lse):
    it = iter(refs)
    q_ref, k_ref, v_ref, bias_ref = next(it), next(it), next(it), next(it)
    sink_ref = next(it) if has_sink else None
    o_ref = next(it)
    lse_ref = next(it) if want_lse else None

    n_tiles = seq_len // Q_TILE
    tiles_per_step = rows // Q_TILE
    chunks_per_group = n_chunks // kv_chunks
    step = pl.program_id(1)
    lane = lax.broadcasted_iota(jnp.int32, (Q_TILE, LANES), 1)
    low_half = lane < HEAD_DIM
    ones = jnp.ones((tk, LANES), jnp.bfloat16)

    for t in range(tiles_per_step):
        tile = step * tiles_per_step + t
        q0 = tile * Q_TILE
        start = pl.multiple_of(jnp.clip(q0 - half_w, 0, seq_len - tk), HEAD_DIM)
        variant = jnp.where(tile == 0, 0, jnp.where(tile == n_tiles - 1, 2, 1))
        r0 = t * Q_TILE
        lse_tile = jnp.zeros((Q_TILE, LANES), jnp.float32)
        for g in range(kv_chunks):
            kc = k_ref[0, pl.ds(start, tk), g * LANES:(g + 1) * LANES]
            vc = v_ref[0, pl.ds(start, tk), g * LANES:(g + 1) * LANES]
            v_aug = jnp.concatenate([vc, ones], axis=1)
            chunks = range(g * chunks_per_group, (g + 1) * chunks_per_group)
            q_parts = []
            for c in chunks:
                q2 = q_ref[0, r0:r0 + Q_TILE, c * LANES:(c + 1) * LANES]
                q_parts.append(jnp.where(low_half, q2, jnp.zeros_like(q2)))
                q_parts.append(jnp.where(low_half, jnp.zeros_like(q2), q2))
            qs = jnp.concatenate(q_parts, axis=0)
            s = lax.dot_general(qs, kc, (((1,), (1,)), ((), ())),
                                preferred_element_type=jnp.float32)
            s = s + bias_ref[variant, g]
            m = jnp.max(s, axis=-1, keepdims=True)
            if has_sink:
                m = jnp.maximum(m, sink_ref[g])
            p = jnp.exp(s - m)
            ov = jnp.dot(p.astype(jnp.bfloat16), v_aug, preferred_element_type=jnp.float32)
            l = ov[:, LANES:]
            if has_sink:
                l = l + jnp.exp(sink_ref[g] - m)
            o = ov[:, :LANES] * (1.0 / l)
            if want_lse:
                lse = m + jnp.log(l)
            for idx, c in enumerate(chunks):
                a0, b0 = 2 * idx * Q_TILE, (2 * idx + 1) * Q_TILE
                o2 = jnp.where(low_half, o[a0:a0 + Q_TILE], o[b0:b0 + Q_TILE])
                o_ref[0, r0:r0 + Q_TILE, c * LANES:(c + 1) * LANES] = o2.astype(o_ref.dtype)
                if want_lse:
                    lse_tile = jnp.where(lane == 2 * c, lse[a0:a0 + Q_TILE],
                                         jnp.where(lane == 2 * c + 1, lse[b0:b0 + Q_TILE], lse_tile))
        if want_lse:
            lse_ref[0, r0:r0 + Q_TILE, :] = lse_tile


def _banded_attention(q, k, v, bias, *, half_w, sink=None, want_lse=False, name):
    n_seq, L, qw = q.shape
    kw = k.shape[2]
    tk = Q_TILE + 2 * half_w
    rows = min(512, L)

    args = [q, k, v, bias]
    in_specs = [
        pl.BlockSpec((1, rows, qw), lambda s, i: (s, i, 0)),
        pl.BlockSpec((1, L, kw), lambda s, i: (s, 0, 0)),
        pl.BlockSpec((1, L, kw), lambda s, i: (s, 0, 0)),
        pl.BlockSpec(bias.shape, lambda s, i: (0, 0, 0, 0)),
    ]
    if sink is not None:
        args.append(sink)
        in_specs.append(pl.BlockSpec(sink.shape, lambda s, i: (0, 0, 0)))
    out_shape = [jax.ShapeDtypeStruct((n_seq, L, qw), jnp.bfloat16)]
    out_specs = [pl.BlockSpec((1, rows, qw), lambda s, i: (s, i, 0))]
    if want_lse:
        out_shape.append(jax.ShapeDtypeStruct((n_seq, L, LANES), jnp.float32))
        out_specs.append(pl.BlockSpec((1, rows, LANES), lambda s, i: (s, i, 0)))

    kern = functools.partial(
        _attn_kernel, n_chunks=qw // LANES, kv_chunks=kw // LANES, tk=tk, half_w=half_w, seq_len=L,
        rows=rows, has_sink=sink is not None, want_lse=want_lse)
    return pl.pallas_call(
        kern,
        out_shape=out_shape,
        grid=(n_seq, L // rows),
        in_specs=in_specs,
        out_specs=out_specs,
        compiler_params=pltpu.CompilerParams(
            dimension_semantics=("arbitrary", "arbitrary"), vmem_limit_bytes=VMEM_LIMIT),
        name=name,
    )(*args)


def _out_proj_router_kernel(*refs):
    nb = len(B_DILS)
    oa_ref = refs[0]
    o_refs = refs[1:1 + nb]
    lse_refs = refs[1 + nb:1 + 2 * nb]
    (x_ref, wo_ref, g_ref, wr_hi_ref, wr_lo_ref, br_ref,
     x1_ref, hf_ref, topi_ref, gate_ref, rank_ref, cnt_ref,
     tri_ref, carry_ref, so_ref, sl_ref) = refs[1 + 2 * nb:]
    i = pl.program_id(0)
    rows = x_ref.shape[0]

    @pl.when(i == 0)
    def _():
        a = lax.broadcasted_iota(jnp.int32, (rows, rows), 0)
        b = lax.broadcasted_iota(jnp.int32, (rows, rows), 1)
        tri_ref[...] = jnp.where(a <= b, 1.0, 0.0).astype(jnp.bfloat16)
        carry_ref[...] = jnp.zeros_like(carry_ref)

    outs, lses = [], []
    for bi, dil in enumerate(B_DILS):
        if dil == 1:
            outs.append(o_refs[bi][...].astype(jnp.float32))
            lses.append(lse_refs[bi][...])
        else:
            for res in range(dil):
                for j in range(B_W // LANES):
                    so_ref[bi, j, pl.ds(res, rows // dil, stride=dil), :] = (
                        o_refs[bi][0, res, :, j * LANES:(j + 1) * LANES].astype(jnp.float32))
                sl_ref[bi, pl.ds(res, rows // dil, stride=dil), :] = lse_refs[bi][0, res]
            outs.append(jnp.concatenate([so_ref[bi, j] for j in range(B_W // LANES)], axis=-1))
            lses.append(sl_ref[bi])

    mx = functools.reduce(jnp.maximum, lses)
    es = [jnp.exp(l - mx) for l in lses]
    inv = 1.0 / functools.reduce(lambda a, b: a + b, es)
    eh = lax.broadcasted_iota(jnp.int32, (LANES, B_W), 0)
    ej = lax.broadcasted_iota(jnp.int32, (LANES, B_W), 1) // HEAD_DIM
    expand = jnp.where(eh == ej, 1.0, 0.0).astype(jnp.bfloat16)
    ob = jnp.zeros((rows, B_W), jnp.float32)
    for e, o in zip(es, outs):
        w = e * inv
        w_hi = w.astype(jnp.bfloat16)
        w_lo = (w - w_hi.astype(jnp.float32)).astype(jnp.bfloat16)
        wide = (jnp.dot(w_hi, expand, preferred_element_type=jnp.float32)
                + jnp.dot(w_lo, expand, preferred_element_type=jnp.float32))
        ob = ob + wide * o

    attn = jnp.concatenate([oa_ref[...], ob.astype(jnp.bfloat16)], axis=-1)
    x1 = x_ref[...] + jnp.dot(attn, wo_ref[...], preferred_element_type=jnp.float32)
    x1_ref[...] = x1
    hf = x1 * lax.rsqrt(jnp.mean(x1 * x1, axis=-1, keepdims=True) + NORM_EPS) * g_ref[...]
    hf_hi = hf.astype(jnp.bfloat16)
    hf_ref[...] = _pack_halves(hf)
    hf_lo = (hf - hf_hi.astype(jnp.float32)).astype(jnp.bfloat16)

    nt = (((1,), (1,)), ((), ()))
    logits = (lax.dot_general(wr_hi_ref[...], hf_hi, nt, preferred_element_type=jnp.float32)
              + lax.dot_general(wr_lo_ref[...], hf_hi, nt, preferred_element_type=jnp.float32)
              + lax.dot_general(wr_hi_ref[...], hf_lo, nt, preferred_element_type=jnp.float32))
    logits = logits + br_ref[:, 0:1]

    eidx = lax.broadcasted_iota(jnp.int32, (N_EXPERTS, rows), 0)
    work = logits
    vals, sels = [], []
    for k in range(TOP_K):
        mk = jnp.max(work, axis=0, keepdims=True)
        ik = jnp.min(jnp.where(work == mk, eidx, N_EXPERTS), axis=0, keepdims=True)
        sel = eidx == ik
        work = jnp.where(sel, -jnp.inf, work)
        vals.append(mk)
        sels.append(sel)
        topi_ref[k:k + 1, :] = ik
    exps = [jnp.exp(vk - vals[0]) for vk in vals]
    denom = exps[0] + exps[1] + exps[2] + exps[3]
    ginv = 1.0 / denom
    for k in range(TOP_K):
        gate_ref[k:k + 1, :] = exps[k] * ginv

    onehot = jnp.zeros((N_EXPERTS, rows), jnp.float32)
    for sel in sels:
        onehot = onehot + jnp.where(sel, 1.0, 0.0)
    incl = jnp.dot(onehot.astype(jnp.bfloat16), tri_ref[...], preferred_element_type=jnp.float32)
    before = incl - onehot + carry_ref[:, 0:1]
    for k in range(TOP_K):
        rk = jnp.sum(jnp.where(sels[k], before, 0.0), axis=0, keepdims=True)
        rank_ref[k:k + 1, :] = rk.astype(jnp.int32)
    carry = carry_ref[...] + jnp.sum(onehot, axis=1, keepdims=True)
    carry_ref[...] = carry
    cnt_ref[...] = carry.astype(jnp.int32)


def _out_proj_router(oa, outs_b, lses_b, x2, wo_bf, g, wr_hi, wr_lo, br, seq):
    n = x2.shape[0]
    rows = PROJ_ROWS
    steps = seq // rows
    row_spec = lambda w: pl.BlockSpec((rows, w), lambda i: (i, 0))
    full = lambda a: pl.BlockSpec(a.shape, lambda i: (0,) * a.ndim)
    col_spec = pl.BlockSpec((TOP_K, rows), lambda i: (0, i))

    def branch_spec(dil, w):
        if dil == 1:
            return row_spec(w)
        return pl.BlockSpec((1, dil, rows // dil, w), lambda i: (i // steps, 0, i % steps, 0))

    in_specs = ([row_spec(A_Q_W)]
                + [branch_spec(d, B_W) for d in B_DILS]
                + [branch_spec(d, LANES) for d in B_DILS]
                + [row_spec(D_MODEL), full(wo_bf), full(g), full(wr_hi), full(wr_lo), full(br)])
    return pl.pallas_call(
        _out_proj_router_kernel,
        out_shape=[
            jax.ShapeDtypeStruct((n, D_MODEL), jnp.float32),
            jax.ShapeDtypeStruct((n, HALF_D), jnp.uint32),
            jax.ShapeDtypeStruct((TOP_K, n), jnp.int32),
            jax.ShapeDtypeStruct((TOP_K, n), jnp.float32),
            jax.ShapeDtypeStruct((TOP_K, n), jnp.int32),
            jax.ShapeDtypeStruct((N_EXPERTS, LANES), jnp.int32),
        ],
        grid=(n // rows,),
        in_specs=in_specs,
        out_specs=[row_spec(D_MODEL), row_spec(HALF_D), col_spec, col_spec, col_spec,
                   pl.BlockSpec((N_EXPERTS, LANES), lambda i: (0, 0))],
        scratch_shapes=[pltpu.VMEM((rows, rows), jnp.bfloat16),
                        pltpu.VMEM((N_EXPERTS, LANES), jnp.float32),
                        pltpu.VMEM((len(B_DILS), B_W // LANES, rows, LANES), jnp.float32),
                        pltpu.VMEM((len(B_DILS), rows, LANES), jnp.float32)],
        compiler_params=pltpu.CompilerParams(
            dimension_semantics=("arbitrary",), vmem_limit_bytes=VMEM_LIMIT),
        name="out_proj_router",
    )(oa, *outs_b, *lses_b, x2, wo_bf, g, wr_hi, wr_lo, br)


def _moe_kernel(blk_exp_ref, n_used_ref, x_ref, w1_ref, b1_ref, w2_ref, b2_ref, y_ref,
                w1_bf_ref, w2_bf_ref):
    i = pl.program_id(0)

    @pl.when(i < n_used_ref[0])
    def _():
        new_expert = (i == 0) | (blk_exp_ref[i] != blk_exp_ref[jnp.maximum(i - 1, 0)])

        @pl.when(new_expert)
        def _():
            w1_bf_ref[...] = w1_ref[0].astype(jnp.bfloat16)
            w2_bf_ref[...] = w2_ref[0].astype(jnp.bfloat16)

        x = jnp.concatenate(_unpack_halves(x_ref[...]), axis=-1).astype(jnp.bfloat16)
        acc = jnp.zeros((MOE_ROWS, D_MODEL), jnp.float32)
        for c in range(D_FF // FF_CHUNK):
            lo = c * FF_CHUNK
            glu = jnp.dot(x, w1_bf_ref[:, lo:lo + FF_CHUNK], preferred_element_type=jnp.float32)
            glu = glu + b1_ref[0, :, lo:lo + FF_CHUNK]
            lin = jnp.dot(x, w1_bf_ref[:, D_FF + lo:D_FF + lo + FF_CHUNK],
                          preferred_element_type=jnp.float32)
            lin = lin + b1_ref[0, :, D_FF + lo:D_FF + lo + FF_CHUNK]
            glu = jnp.minimum(glu, SWIGLU_LIMIT)
            lin = jnp.clip(lin, -SWIGLU_LIMIT, SWIGLU_LIMIT)
            act = glu * (1.0 / (1.0 + jnp.exp(-SWIGLU_ALPHA * glu))) * (lin + 1.0)
            acc = acc + jnp.dot(act.astype(jnp.bfloat16), w2_bf_ref[lo:lo + FF_CHUNK, :],
                                preferred_element_type=jnp.float32)
        y_ref[...] = _pack_halves(acc + b2_ref[0])


def _moe_experts(blk_exp, n_used, xb, w1, b1, w2, b2):
    n_rows = xb.shape[0]
    n_blk = n_rows // MOE_ROWS

    def blk(i, be, nu):
        return jnp.minimum(i, nu[0] - 1)

    grid_spec = pltpu.PrefetchScalarGridSpec(
        num_scalar_prefetch=2,
        grid=(n_blk,),
        in_specs=[
            pl.BlockSpec((MOE_ROWS, HALF_D), lambda i, be, nu: (blk(i, be, nu), 0)),
            pl.BlockSpec((1, D_MODEL, 2 * D_FF), lambda i, be, nu: (be[blk(i, be, nu)], 0, 0)),
            pl.BlockSpec((1, 1, 2 * D_FF), lambda i, be, nu: (be[blk(i, be, nu)], 0, 0)),
            pl.BlockSpec((1, D_FF, D_MODEL), lambda i, be, nu: (be[blk(i, be, nu)], 0, 0)),
            pl.BlockSpec((1, 1, D_MODEL), lambda i, be, nu: (be[blk(i, be, nu)], 0, 0)),
        ],
        out_specs=pl.BlockSpec((MOE_ROWS, HALF_D), lambda i, be, nu: (blk(i, be, nu), 0)),
        scratch_shapes=[pltpu.VMEM((D_MODEL, 2 * D_FF), jnp.bfloat16),
                        pltpu.VMEM((D_FF, D_MODEL), jnp.bfloat16)],
    )
    return pl.pallas_call(
        _moe_kernel,
        out_shape=jax.ShapeDtypeStruct((n_rows, HALF_D), jnp.uint32),
        grid_spec=grid_spec,
        compiler_params=pltpu.CompilerParams(
            dimension_semantics=("arbitrary",), vmem_limit_bytes=MOE_VMEM_LIMIT),
        name="moe_experts",
    )(blk_exp, n_used, xb, w1, b1, w2, b2)


def _combine_kernel(x1_ref, yg_ref, gate_ref, o_ref):
    acc_lo = x1_ref[:, :HALF_D]
    acc_hi = x1_ref[:, HALF_D:]
    for k in range(TOP_K):
        lo, hi = _unpack_halves(yg_ref[k])
        gk = gate_ref[:, k:k + 1]
        acc_lo = acc_lo + lo * gk
        acc_hi = acc_hi + hi * gk
    o_ref[:, :HALF_D] = acc_lo
    o_ref[:, HALF_D:] = acc_hi


def _combine(x1, yg, gates_nk):
    n = x1.shape[0]
    rows = PROJ_ROWS
    return pl.pallas_call(
        _combine_kernel,
        out_shape=jax.ShapeDtypeStruct((n, D_MODEL), jnp.float32),
        grid=(n // rows,),
        in_specs=[pl.BlockSpec((rows, D_MODEL), lambda i: (i, 0)),
                  pl.BlockSpec((TOP_K, rows, HALF_D), lambda i: (0, i, 0)),
                  pl.BlockSpec((rows, TOP_K), lambda i: (i, 0))],
        out_specs=pl.BlockSpec((rows, D_MODEL), lambda i: (i, 0)),
        compiler_params=pltpu.CompilerParams(
            dimension_semantics=("arbitrary",), vmem_limit_bytes=VMEM_LIMIT),
        name="moe_combine",
    )(x1, yg, gates_nk)


def _sc_worker_id():
    return lax.axis_index("s") * SC_CORES + lax.axis_index("c")


def _sc_dispatch(hf, dest3, n_rows):
    n = hf.shape[0]
    chunks_per_worker = n // SC_CHUNK // SC_WORKERS
    mesh = plsc.VectorSubcoreMesh(core_axis_name="c", subcore_axis_name="s")

    @functools.partial(
        pl.kernel, mesh=mesh,
        out_type=jax.ShapeDtypeStruct((n_rows, HALF_D), hf.dtype),
        scratch_types=[pltpu.VMEM((TOP_K, SC_CHUNK), jnp.int32),
                       pltpu.VMEM((SC_CHUNK, HALF_D), hf.dtype)],
        name="sc_dispatch")
    def run(hf_hbm, dest_hbm, xb_hbm, idx_v, rows_v):
        first = _sc_worker_id() * chunks_per_worker

        @pl.loop(0, chunks_per_worker)
        def _(j):
            ch = first + j
            pltpu.sync_copy(dest_hbm.at[ch], idx_v)
            pltpu.sync_copy(hf_hbm.at[pl.ds(ch * SC_CHUNK, SC_CHUNK)], rows_v)
            for k in range(TOP_K):
                pltpu.sync_copy(rows_v, xb_hbm.at[idx_v.at[k]])

    return run(hf, dest3)


def _sc_collect(y, dest3):
    n = dest3.shape[0] * SC_CHUNK
    chunks_per_worker = n // SC_CHUNK // SC_WORKERS
    mesh = plsc.VectorSubcoreMesh(core_axis_name="c", subcore_axis_name="s")

    @functools.partial(
        pl.kernel, mesh=mesh,
        out_type=jax.ShapeDtypeStruct((TOP_K, n, HALF_D), y.dtype),
        scratch_types=[pltpu.VMEM((TOP_K, SC_CHUNK), jnp.int32),
                       pltpu.VMEM((SC_CHUNK, HALF_D), y.dtype)],
        name="sc_collect")
    def run(y_hbm, dest_hbm, yg_hbm, idx_v, rows_v):
        first = _sc_worker_id() * chunks_per_worker

        @pl.loop(0, chunks_per_worker)
        def _(j):
            ch = first + j
            pltpu.sync_copy(dest_hbm.at[ch], idx_v)
            for k in range(TOP_K):
                pltpu.sync_copy(y_hbm.at[idx_v.at[k]], rows_v)
                pltpu.sync_copy(rows_v, yg_hbm.at[k, pl.ds(ch * SC_CHUNK, SC_CHUNK)])

    return run(y, dest3)


def _layer(x2, batch, seq, attn_norm_g, w_in, a_q_g, a_k_g, a_sinks, b_q_g, b_k_g, w_out,
           ffn_norm_g, w_router, b_router, w1, b1, w2, b2):
    n = x2.shape[0]
    slopes = _alibi_slopes()
    a_order = np.asarray(A_HEAD_ORDER)

    a_cols = (a_order[:, None] * HEAD_DIM + np.arange(HEAD_DIM)[None, :]).reshape(-1)
    col_perm = np.concatenate([a_cols, np.arange(A_Q_W, w_in.shape[1])])
    w_in_bf = w_in[:, col_perm].astype(jnp.bfloat16)
    row_perm = np.concatenate([a_cols, np.arange(A_Q_W, A_Q_W + B_W)])
    w_out_bf = w_out[row_perm].astype(jnp.bfloat16)
    q_scale = HEAD_DIM ** -0.5
    reps = MXU_DIM // HEAD_DIM
    gains = jnp.stack([jnp.tile(a_q_g, reps) * q_scale, jnp.tile(a_k_g, reps),
                       jnp.tile(b_q_g, reps) * q_scale, jnp.tile(b_k_g, reps)]).astype(jnp.float32)

    proj = _in_proj(x2, attn_norm_g.reshape(1, -1), w_in_bf, gains, batch, seq)
    qa, ka, va = proj[:3]
    nb = len(B_DILS)
    qbs, kbs, vbs = proj[3:3 + nb], proj[3 + nb:3 + 2 * nb], proj[3 + 2 * nb:]

    bias_a = _bias_tables(slopes[a_order], A_Q_HEADS, A_HALF_WINDOW, 1, Q_TILE + 2 * A_HALF_WINDOW)
    sink_col = jnp.repeat(a_sinks[a_order].astype(jnp.float32), Q_TILE).reshape(1, A_Q_HEADS * Q_TILE, 1)
    as_seqs = lambda a: a.reshape(batch, seq, a.shape[-1])
    out_a = _banded_attention(as_seqs(qa), as_seqs(ka), as_seqs(va), bias_a, half_w=A_HALF_WINDOW,
                              sink=sink_col, name="attn_a")[0].reshape(n, A_Q_W)

    outs_b, lses_b = [], []
    for bi, (window, dil) in enumerate(B_BRANCHES):
        half_w = window // (2 * dil)
        bias_b = _bias_tables(slopes[A_Q_HEADS:], 2, half_w, dil, Q_TILE + 2 * half_w)
        L = seq // dil
        to_seqs = lambda a: a.reshape(batch * dil, L, a.shape[-1])
        o, lse = _banded_attention(to_seqs(qbs[bi]), to_seqs(kbs[bi]), to_seqs(vbs[bi]), bias_b,
                                   half_w=half_w, want_lse=True, name=f"attn_b_d{dil}")
        if dil == 1:
            outs_b.append(o.reshape(n, B_W))
            lses_b.append(lse.reshape(n, LANES))
        else:
            outs_b.append(o.reshape(batch, dil, L, B_W))
            lses_b.append(lse.reshape(batch, dil, L, LANES))

    wr_t = w_router.T.astype(jnp.float32)
    wr_hi = wr_t.astype(jnp.bfloat16)
    wr_lo = (wr_t - wr_hi.astype(jnp.float32)).astype(jnp.bfloat16)
    br = jnp.broadcast_to(b_router.astype(jnp.float32)[:, None], (N_EXPERTS, LANES))
    x1, hf, topi, gates, ranks, counts = _out_proj_router(
        out_a, outs_b, lses_b, x2, w_out_bf, ffn_norm_g.reshape(1, -1), wr_hi, wr_lo, br, seq)

    g = MOE_ROWS
    nk = n * TOP_K
    n_rows = nk + N_EXPERTS * g
    cnt = counts[:, 0]
    pcnt = (cnt + g - 1) // g * g
    pends = jnp.cumsum(pcnt)
    pstarts = pends - pcnt
    experts = jnp.arange(N_EXPERTS, dtype=jnp.int32)
    start_of = jnp.sum(jnp.where(topi[:, :, None] == experts, pstarts, 0), axis=-1)
    dest = (start_of + ranks).astype(jnp.int32)
    n_blk = n_rows // g
    blk_row0 = jnp.arange(n_blk, dtype=jnp.int32) * g
    blk_exp = jnp.minimum(jnp.sum(pends[None, :] <= blk_row0[:, None], axis=-1),
                          N_EXPERTS - 1).astype(jnp.int32)
    n_used = (pends[-1] // g).astype(jnp.int32).reshape(1)
    dest3 = dest.reshape(TOP_K, n // SC_CHUNK, SC_CHUNK).transpose(1, 0, 2)

    xb = _sc_dispatch(hf, dest3, n_rows)
    y = _moe_experts(blk_exp, n_used, xb, w1, b1[:, None, :], w2, b2[:, None, :])
    yg = _sc_collect(y, dest3)
    return _combine(x1, yg, gates.T)


def kernel(x, attn_norm_g, w_in, a_q_norm_g, a_k_norm_g, a_sinks, b_q_norm_g, b_k_norm_g, w_out,
           ffn_norm_g, w_router, b_router, w1, b1, w2, b2):
    batch, seq, d = x.shape
    x2 = x.reshape(batch * seq, d)
    for i in range(attn_norm_g.shape[0]):
        x2 = _layer(x2, batch, seq, attn_norm_g[i], w_in[i], a_q_norm_g[i], a_k_norm_g[i],
                    a_sinks[i], b_q_norm_g[i], b_k_norm_g[i], w_out[i], ffn_norm_g[i],
                    w_router[i], b_router[i], w1[i], b1[i], w2[i], b2[i])
    return x2.reshape(batch, seq, d)
```

```python
import functools

import jax
import jax.numpy as jnp
import numpy as np
from jax import lax
from jax.experimental import pallas as pl
from jax.experimental.pallas import tpu as pltpu
from jax.experimental.pallas import tpu_sc as plsc

D_MODEL = 1024
HALF_D = D_MODEL // 2
HEAD_DIM = 64
LANES = 128
MXU_DIM = 256
A_Q_HEADS = 8
A_KV_HEADS = 2
B_HEADS = 8
A_HALF_WINDOW = 128
B_BRANCHES = ((128, 1), (512, 4), (2048, 16))
B_DILS = tuple(d for _, d in B_BRANCHES)
N_ALIBI_HEADS = 16
A_Q_W = A_Q_HEADS * HEAD_DIM
A_KV_W = A_KV_HEADS * HEAD_DIM
B_W = B_HEADS * HEAD_DIM
N_EXPERTS = 32
TOP_K = 4
D_FF = 1024
SWIGLU_ALPHA = 1.702
SWIGLU_LIMIT = 7.0
NORM_EPS = 1e-5
MASK_VALUE = -1e30

Q_TILE = 128
A_STACK_HEADS = 2
PROJ_ROWS = 512
MOE_ROWS = 256
FF_CHUNK = 512
VMEM_LIMIT = 48 * 1024 * 1024
MOE_VMEM_LIMIT = 58 * 1024 * 1024
SC_CORES = 2
SC_SUBCORES = 16
SC_WORKERS = SC_CORES * SC_SUBCORES
SC_CHUNK = 64

A_HEAD_ORDER = (0, 4, 1, 5, 2, 6, 3, 7)


def _pack_halves(v):
    lo = v[:, :HALF_D].astype(jnp.bfloat16).astype(jnp.float32)
    hi = v[:, HALF_D:].astype(jnp.bfloat16).astype(jnp.float32)
    return (pltpu.bitcast(lo, jnp.uint32) >> 16) | pltpu.bitcast(hi, jnp.uint32)


def _unpack_halves(w):
    lo = pltpu.bitcast(w << 16, jnp.float32)
    hi = pltpu.bitcast(w & jnp.uint32(0xFFFF0000), jnp.float32)
    return lo, hi


def _alibi_slopes():
    return np.exp2(-8.0 * np.arange(1, N_ALIBI_HEADS + 1, dtype=np.float32) / N_ALIBI_HEADS).astype(np.float32)


def _bias_tables(head_slopes, heads_per_group, half_w, dist_scale, tk):
    i = np.arange(Q_TILE)[:, None]
    j = np.arange(tk)[None, :]
    tabs = []
    for shift in (0, half_w, tk - Q_TILE):
        dist = np.abs(j - shift - i)
        valid = dist <= half_w
        per_head = []
        for sl in head_slopes:
            b = -np.float32(sl) * (dist * dist_scale).astype(np.float32)
            per_head.append(np.where(valid, b, np.float32(MASK_VALUE)).astype(np.float32))
        t = np.stack(per_head).reshape(-1, heads_per_group * Q_TILE, tk)
        tabs.append(t)
    return jnp.asarray(np.stack(tabs))


def _in_proj_kernel(x_ref, g_ref, w_ref, gains_ref, qa_ref, ka_ref, va_ref, *rest):
    b_refs, scr_ref = rest[:-1], rest[-1]
    x = x_ref[...]
    xn = x * lax.rsqrt(jnp.mean(x * x, axis=-1, keepdims=True) + NORM_EPS) * g_ref[...]
    xn = xn.astype(jnp.bfloat16)
    r = lax.broadcasted_iota(jnp.int32, (MXU_DIM, MXU_DIM), 0) // HEAD_DIM
    c = lax.broadcasted_iota(jnp.int32, (MXU_DIM, MXU_DIM), 1) // HEAD_DIM
    blockdiag = jnp.where(r == c, 1.0, 0.0).astype(jnp.bfloat16)

    def project(col0, width, gain_row):
        sec = jnp.dot(xn, w_ref[:, col0:col0 + width], preferred_element_type=jnp.float32)
        if gain_row is None:
            return [sec]
        parts = []
        step = min(width, MXU_DIM)
        for j in range(width // step):
            p = sec[:, j * step:(j + 1) * step]
            ss = jnp.dot((p * p).astype(jnp.bfloat16), blockdiag[:step, :step],
                         preferred_element_type=jnp.float32)
            parts.append(p * lax.rsqrt(ss * (1.0 / HEAD_DIM) + NORM_EPS)
                         * gains_ref[gain_row:gain_row + 1, :step])
        return parts

    def store(out_ref, parts):
        w = parts[0].shape[1]
        for j, p in enumerate(parts):
            out_ref[:, j * w:(j + 1) * w] = p.astype(out_ref.dtype)

    store(qa_ref, project(0, A_Q_W, 0))
    store(ka_ref, project(A_Q_W, A_KV_W, 1))
    store(va_ref, project(A_Q_W + A_KV_W, A_KV_W, None))

    rows = x_ref.shape[0]
    col0 = A_Q_W + 2 * A_KV_W
    for t, gain_row in enumerate((2, 3, None)):
        parts = project(col0 + t * B_W, B_W, gain_row)
        sec = jnp.concatenate(parts, axis=-1) if len(parts) > 1 else parts[0]
        for j in range(B_W // LANES):
            scr_ref[j] = sec[:, j * LANES:(j + 1) * LANES]
        for bi, dil in enumerate(B_DILS):
            out_ref = b_refs[t * len(B_DILS) + bi]
            if dil == 1:
                out_ref[...] = sec.astype(out_ref.dtype)
            else:
                for res in range(dil):
                    for j in range(B_W // LANES):
                        out_ref[0, res, :, j * LANES:(j + 1) * LANES] = (
                            scr_ref[j, pl.ds(res, rows // dil, stride=dil), :].astype(out_ref.dtype))


def _in_proj(x2, g, w_bf, gains, batch, seq):
    n = x2.shape[0]
    rows = PROJ_ROWS
    steps = seq // rows
    out_shape = [jax.ShapeDtypeStruct((n, w), jnp.bfloat16) for w in (A_Q_W, A_KV_W, A_KV_W)]
    out_specs = [pl.BlockSpec((rows, w), lambda i: (i, 0)) for w in (A_Q_W, A_KV_W, A_KV_W)]
    for _ in range(3):
        for dil in B_DILS:
            if dil == 1:
                out_shape.append(jax.ShapeDtypeStruct((n, B_W), jnp.bfloat16))
                out_specs.append(pl.BlockSpec((rows, B_W), lambda i: (i, 0)))
            else:
                out_shape.append(jax.ShapeDtypeStruct((batch, dil, seq // dil, B_W), jnp.bfloat16))
                out_specs.append(pl.BlockSpec((1, dil, rows // dil, B_W),
                                              lambda i: (i // steps, 0, i % steps, 0)))
    return pl.pallas_call(
        _in_proj_kernel,
        out_shape=out_shape,
        grid=(n // rows,),
        in_specs=[
            pl.BlockSpec((rows, D_MODEL), lambda i: (i, 0)),
            pl.BlockSpec((1, D_MODEL), lambda i: (0, 0)),
            pl.BlockSpec(w_bf.shape, lambda i: (0, 0)),
            pl.BlockSpec(gains.shape, lambda i: (0, 0)),
        ],
        out_specs=out_specs,
        scratch_shapes=[pltpu.VMEM((B_W // LANES, rows, LANES), jnp.float32)],
        compiler_params=pltpu.CompilerParams(
            dimension_semantics=("arbitrary",), vmem_limit_bytes=VMEM_LIMIT),
        name="in_proj",
    )(x2, g, w_bf, gains)


def _attn_kernel(*refs, n_chunks, kv_chunks, chunks_per_stack, tk, half_w, seq_len, rows, has_sink,
                 want_lse):
    it = iter(refs)
    q_ref, k_ref, v_ref, bias_ref = next(it), next(it), next(it), next(it)
    sink_ref = next(it) if has_sink else None
    o_ref = next(it)
    lse_ref = next(it) if want_lse else None

    n_tiles = seq_len // Q_TILE
    tiles_per_step = rows // Q_TILE
    chunks_per_group = n_chunks // kv_chunks
    step = pl.program_id(1)
    lane = lax.broadcasted_iota(jnp.int32, (Q_TILE, LANES), 1)
    low_half = lane < HEAD_DIM
    ones = jnp.ones((tk, LANES), jnp.bfloat16)

    for t in range(tiles_per_step):
        tile = step * tiles_per_step + t
        q0 = tile * Q_TILE
        start = pl.multiple_of(jnp.clip(q0 - half_w, 0, seq_len - tk), HEAD_DIM)
        variant = jnp.where(tile == 0, 0, jnp.where(tile == n_tiles - 1, 2, 1))
        r0 = t * Q_TILE
        lse_tile = jnp.zeros((Q_TILE, LANES), jnp.float32)
        for g in range(n_chunks // chunks_per_stack):
            chunks = range(g * chunks_per_stack, (g + 1) * chunks_per_stack)
            kv = chunks[0] // chunks_per_group
            kc = k_ref[0, pl.ds(start, tk), kv * LANES:(kv + 1) * LANES]
            vc = v_ref[0, pl.ds(start, tk), kv * LANES:(kv + 1) * LANES]
            v_aug = jnp.concatenate([vc, ones], axis=1)
            q_parts = []
            for c in chunks:
                q2 = q_ref[0, r0:r0 + Q_TILE, c * LANES:(c + 1) * LANES]
                q_parts.append(jnp.where(low_half, q2, jnp.zeros_like(q2)))
                q_parts.append(jnp.where(low_half, jnp.zeros_like(q2), q2))
            qs = jnp.concatenate(q_parts, axis=0)
            s = lax.dot_general(qs, kc, (((1,), (1,)), ((), ())),
                                preferred_element_type=jnp.float32)
            s = s + bias_ref[variant, g]
            m = jnp.max(s, axis=-1, keepdims=True)
            if has_sink:
                m = jnp.maximum(m, sink_ref[g])
            p = jnp.exp(s - m)
            ov = jnp.dot(p.astype(jnp.bfloat16), v_aug, preferred_element_type=jnp.float32)
            l = ov[:, LANES:]
            if has_sink:
                l = l + jnp.exp(sink_ref[g] - m)
            o = ov[:, :LANES] * (1.0 / l)
            if want_lse:
                lse = m + jnp.log(l)
            for idx, c in enumerate(chunks):
                a0, b0 = 2 * idx * Q_TILE, (2 * idx + 1) * Q_TILE
                o2 = jnp.where(low_half, o[a0:a0 + Q_TILE], o[b0:b0 + Q_TILE])
                o_ref[0, r0:r0 + Q_TILE, c * LANES:(c + 1) * LANES] = o2.astype(o_ref.dtype)
                if want_lse:
                    lse_tile = jnp.where(lane == 2 * c, lse[a0:a0 + Q_TILE],
                                         jnp.where(lane == 2 * c + 1, lse[b0:b0 + Q_TILE], lse_tile))
        if want_lse:
            lse_ref[0, r0:r0 + Q_TILE, :] = lse_tile


def _banded_attention(q, k, v, bias, *, half_w, sink=None, want_lse=False, name):
    n_seq, L, qw = q.shape
    kw = k.shape[2]
    tk = Q_TILE + 2 * half_w
    rows = min(512, L)

    args = [q, k, v, bias]
    in_specs = [
        pl.BlockSpec((1, rows, qw), lambda s, i: (s, i, 0)),
        pl.BlockSpec((1, L, kw), lambda s, i: (s, 0, 0)),
        pl.BlockSpec((1, L, kw), lambda s, i: (s, 0, 0)),
        pl.BlockSpec(bias.shape, lambda s, i: (0, 0, 0, 0)),
    ]
    if sink is not None:
        args.append(sink)
        in_specs.append(pl.BlockSpec(sink.shape, lambda s, i: (0, 0, 0)))
    out_shape = [jax.ShapeDtypeStruct((n_seq, L, qw), jnp.bfloat16)]
    out_specs = [pl.BlockSpec((1, rows, qw), lambda s, i: (s, i, 0))]
    if want_lse:
        out_shape.append(jax.ShapeDtypeStruct((n_seq, L, LANES), jnp.float32))
        out_specs.append(pl.BlockSpec((1, rows, LANES), lambda s, i: (s, i, 0)))

    kern = functools.partial(
        _attn_kernel, n_chunks=qw // LANES, kv_chunks=kw // LANES,
        chunks_per_stack=bias.shape[2] // (2 * Q_TILE), tk=tk, half_w=half_w, seq_len=L,
        rows=rows, has_sink=sink is not None, want_lse=want_lse)
    return pl.pallas_call(
        kern,
        out_shape=out_shape,
        grid=(n_seq, L // rows),
        in_specs=in_specs,
        out_specs=out_specs,
        compiler_params=pltpu.CompilerParams(
            dimension_semantics=("arbitrary", "arbitrary"), vmem_limit_bytes=VMEM_LIMIT),
        name=name,
    )(*args)


def _out_proj_router_kernel(*refs):
    nb = len(B_DILS)
    oa_ref = refs[0]
    o_refs = refs[1:1 + nb]
    lse_refs = refs[1 + nb:1 + 2 * nb]
    (x_ref, wo_ref, g_ref, wr_hi_ref, wr_lo_ref, br_ref,
     x1_ref, hf_ref, topi_ref, gate_ref, rank_ref, cnt_ref,
     tri_ref, carry_ref, so_ref, sl_ref) = refs[1 + 2 * nb:]
    i = pl.program_id(0)
    rows = x_ref.shape[0]

    @pl.when(i == 0)
    def _():
        a = lax.broadcasted_iota(jnp.int32, (rows, rows), 0)
        b = lax.broadcasted_iota(jnp.int32, (rows, rows), 1)
        tri_ref[...] = jnp.where(a <= b, 1.0, 0.0).astype(jnp.bfloat16)
        carry_ref[...] = jnp.zeros_like(carry_ref)

    outs, lses = [], []
    for bi, dil in enumerate(B_DILS):
        if dil == 1:
            outs.append(o_refs[bi][...].astype(jnp.float32))
            lses.append(lse_refs[bi][...])
        else:
            for res in range(dil):
                for j in range(B_W // LANES):
                    so_ref[bi, j, pl.ds(res, rows // dil, stride=dil), :] = (
                        o_refs[bi][0, res, :, j * LANES:(j + 1) * LANES].astype(jnp.float32))
                sl_ref[bi, pl.ds(res, rows // dil, stride=dil), :] = lse_refs[bi][0, res]
            outs.append(jnp.concatenate([so_ref[bi, j] for j in range(B_W // LANES)], axis=-1))
            lses.append(sl_ref[bi])

    mx = functools.reduce(jnp.maximum, lses)
    es = [jnp.exp(l - mx) for l in lses]
    inv = 1.0 / functools.reduce(lambda a, b: a + b, es)
    eh = lax.broadcasted_iota(jnp.int32, (LANES, B_W), 0)
    ej = lax.broadcasted_iota(jnp.int32, (LANES, B_W), 1) // HEAD_DIM
    expand = jnp.where(eh == ej, 1.0, 0.0).astype(jnp.bfloat16)
    ob = jnp.zeros((rows, B_W), jnp.float32)
    for e, o in zip(es, outs):
        w = e * inv
        w_hi = w.astype(jnp.bfloat16)
        w_lo = (w - w_hi.astype(jnp.float32)).astype(jnp.bfloat16)
        wide = (jnp.dot(w_hi, expand, preferred_element_type=jnp.float32)
                + jnp.dot(w_lo, expand, preferred_element_type=jnp.float32))
        ob = ob + wide * o

    attn = jnp.concatenate([oa_ref[...], ob.astype(jnp.bfloat16)], axis=-1)
    x1 = x_ref[...] + jnp.dot(attn, wo_ref[...], preferred_element_type=jnp.float32)
    x1_ref[...] = x1
    hf = x1 * lax.rsqrt(jnp.mean(x1 * x1, axis=-1, keepdims=True) + NORM_EPS) * g_ref[...]
    hf_hi = hf.astype(jnp.bfloat16)
    hf_ref[...] = _pack_halves(hf)
    hf_lo = (hf - hf_hi.astype(jnp.float32)).astype(jnp.bfloat16)

    nt = (((1,), (1,)), ((), ()))
    logits = (lax.dot_general(wr_hi_ref[...], hf_hi, nt, preferred_element_type=jnp.float32)
              + lax.dot_general(wr_lo_ref[...], hf_hi, nt, preferred_element_type=jnp.float32)
              + lax.dot_general(wr_hi_ref[...], hf_lo, nt, preferred_element_type=jnp.float32))
    logits = logits + br_ref[:, 0:1]

    eidx = lax.broadcasted_iota(jnp.int32, (N_EXPERTS, rows), 0)
    work = logits
    vals, sels = [], []
    for k in range(TOP_K):
        mk = jnp.max(work, axis=0, keepdims=True)
        ik = jnp.min(jnp.where(work == mk, eidx, N_EXPERTS), axis=0, keepdims=True)
        sel = eidx == ik
        work = jnp.where(sel, -jnp.inf, work)
        vals.append(mk)
        sels.append(sel)
        topi_ref[k:k + 1, :] = ik
    exps = [jnp.exp(vk - vals[0]) for vk in vals]
    denom = exps[0] + exps[1] + exps[2] + exps[3]
    ginv = 1.0 / denom
    for k in range(TOP_K):
        gate_ref[k:k + 1, :] = exps[k] * ginv

    onehot = jnp.zeros((N_EXPERTS, rows), jnp.float32)
    for sel in sels:
        onehot = onehot + jnp.where(sel, 1.0, 0.0)
    incl = jnp.dot(onehot.astype(jnp.bfloat16), tri_ref[...], preferred_element_type=jnp.float32)
    before = incl - onehot + carry_ref[:, 0:1]
    for k in range(TOP_K):
        rk = jnp.sum(jnp.where(sels[k], before, 0.0), axis=0, keepdims=True)
        rank_ref[k:k + 1, :] = rk.astype(jnp.int32)
    carry = carry_ref[...] + jnp.sum(onehot, axis=1, keepdims=True)
    carry_ref[...] = carry
    cnt_ref[...] = carry.astype(jnp.int32)


def _out_proj_router(oa, outs_b, lses_b, x2, wo_bf, g, wr_hi, wr_lo, br, seq):
    n = x2.shape[0]
    rows = PROJ_ROWS
    steps = seq // rows
    row_spec = lambda w: pl.BlockSpec((rows, w), lambda i: (i, 0))
    full = lambda a: pl.BlockSpec(a.shape, lambda i: (0,) * a.ndim)
    col_spec = pl.BlockSpec((TOP_K, rows), lambda i: (0, i))

    def branch_spec(dil, w):
        if dil == 1:
            return row_spec(w)
        return pl.BlockSpec((1, dil, rows // dil, w), lambda i: (i // steps, 0, i % steps, 0))

    in_specs = ([row_spec(A_Q_W)]
                + [branch_spec(d, B_W) for d in B_DILS]
                + [branch_spec(d, LANES) for d in B_DILS]
                + [row_spec(D_MODEL), full(wo_bf), full(g), full(wr_hi), full(wr_lo), full(br)])
    return pl.pallas_call(
        _out_proj_router_kernel,
        out_shape=[
            jax.ShapeDtypeStruct((n, D_MODEL), jnp.float32),
            jax.ShapeDtypeStruct((n, HALF_D), jnp.uint32),
            jax.ShapeDtypeStruct((TOP_K, n), jnp.int32),
            jax.ShapeDtypeStruct((TOP_K, n), jnp.float32),
            jax.ShapeDtypeStruct((TOP_K, n), jnp.int32),
            jax.ShapeDtypeStruct((N_EXPERTS, LANES), jnp.int32),
        ],
        grid=(n // rows,),
        in_specs=in_specs,
        out_specs=[row_spec(D_MODEL), row_spec(HALF_D), col_spec, col_spec, col_spec,
                   pl.BlockSpec((N_EXPERTS, LANES), lambda i: (0, 0))],
        scratch_shapes=[pltpu.VMEM((rows, rows), jnp.bfloat16),
                        pltpu.VMEM((N_EXPERTS, LANES), jnp.float32),
                        pltpu.VMEM((len(B_DILS), B_W // LANES, rows, LANES), jnp.float32),
                        pltpu.VMEM((len(B_DILS), rows, LANES), jnp.float32)],
        compiler_params=pltpu.CompilerParams(
            dimension_semantics=("arbitrary",), vmem_limit_bytes=VMEM_LIMIT),
        name="out_proj_router",
    )(oa, *outs_b, *lses_b, x2, wo_bf, g, wr_hi, wr_lo, br)


def _mxu_dot(a_bf, w_f32):
    return lax.dot_general(a_bf, w_f32, (((1,), (0,)), ((), ())), preferred_element_type=jnp.float32)


def _moe_kernel(blk_exp_ref, first_ref, slot_ref, next_exp_ref, n_used_ref,
                x_ref, w1_hbm, b1_ref, w2_hbm, b2_ref, y_ref, w1_buf, w2_buf, sem):
    i = pl.program_id(0)

    def weight_copies(expert, slot):
        return (pltpu.make_async_copy(w1_hbm.at[expert], w1_buf.at[slot], sem.at[slot, 0]),
                pltpu.make_async_copy(w2_hbm.at[expert], w2_buf.at[slot], sem.at[slot, 1]))

    @pl.when(i < n_used_ref[0])
    def _():
        slot = slot_ref[i]

        @pl.when(i == 0)
        def _():
            for cp in weight_copies(blk_exp_ref[0], slot):
                cp.start()

        @pl.when(first_ref[i] == 1)
        def _():
            for cp in weight_copies(blk_exp_ref[i], slot):
                cp.wait()

            @pl.when(next_exp_ref[i] >= 0)
            def _():
                for cp in weight_copies(next_exp_ref[i], 1 - slot):
                    cp.start()

        x = jnp.concatenate(_unpack_halves(x_ref[...]), axis=-1).astype(jnp.bfloat16)
        acc = jnp.zeros((MOE_ROWS, D_MODEL), jnp.float32)
        for c in range(D_FF // FF_CHUNK):
            lo = c * FF_CHUNK
            glu = _mxu_dot(x, w1_buf[slot, :, lo:lo + FF_CHUNK]) + b1_ref[0, :, lo:lo + FF_CHUNK]
            lin = (_mxu_dot(x, w1_buf[slot, :, D_FF + lo:D_FF + lo + FF_CHUNK])
                   + b1_ref[0, :, D_FF + lo:D_FF + lo + FF_CHUNK])
            glu = jnp.minimum(glu, SWIGLU_LIMIT)
            lin = jnp.clip(lin, -SWIGLU_LIMIT, SWIGLU_LIMIT)
            act = glu * (1.0 / (1.0 + jnp.exp(-SWIGLU_ALPHA * glu))) * (lin + 1.0)
            acc = acc + _mxu_dot(act.astype(jnp.bfloat16), w2_buf[slot, lo:lo + FF_CHUNK, :])
        y_ref[...] = _pack_halves(acc + b2_ref[0])


def _moe_plan(pends, n_blk):
    g = MOE_ROWS
    blk_row0 = jnp.arange(n_blk, dtype=jnp.int32) * g
    blk_exp = jnp.minimum(jnp.sum(pends[None, :] <= blk_row0[:, None], axis=-1),
                          N_EXPERTS - 1).astype(jnp.int32)
    n_used = (pends[-1] // g).astype(jnp.int32)
    used = blk_row0 < pends[-1]
    prev_exp = jnp.concatenate([jnp.full((1,), -1, jnp.int32), blk_exp[:-1]])
    first = (used & (blk_exp != prev_exp)).astype(jnp.int32)
    slot = ((jnp.cumsum(first) - 1) % 2).astype(jnp.int32)
    pstarts = jnp.concatenate([jnp.zeros((1,), pends.dtype), pends[:-1]])
    nonempty = pends > pstarts
    experts = jnp.arange(N_EXPERTS, dtype=jnp.int32)
    later = nonempty[None, :] & (experts[None, :] > experts[:, None])
    next_nonempty = jnp.min(jnp.where(later, experts[None, :], N_EXPERTS), axis=-1)
    next_nonempty = jnp.where(next_nonempty == N_EXPERTS, -1, next_nonempty).astype(jnp.int32)
    next_exp = jnp.sum(jnp.where(blk_exp[:, None] == experts[None, :], next_nonempty[None, :], 0),
                       axis=-1).astype(jnp.int32)
    return blk_exp, first, slot, next_exp, n_used.reshape(1)


def _moe_experts(plan, xb, w1, b1, w2, b2):
    n_rows = xb.shape[0]
    n_blk = n_rows // MOE_ROWS

    def blk(i, *p):
        return jnp.minimum(i, p[-1][0] - 1)

    grid_spec = pltpu.PrefetchScalarGridSpec(
        num_scalar_prefetch=len(plan),
        grid=(n_blk,),
        in_specs=[
            pl.BlockSpec((MOE_ROWS, HALF_D), lambda i, *p: (blk(i, *p), 0)),
            pl.BlockSpec(memory_space=pl.ANY),
            pl.BlockSpec((1, 1, 2 * D_FF), lambda i, *p: (p[0][blk(i, *p)], 0, 0)),
            pl.BlockSpec(memory_space=pl.ANY),
            pl.BlockSpec((1, 1, D_MODEL), lambda i, *p: (p[0][blk(i, *p)], 0, 0)),
        ],
        out_specs=pl.BlockSpec((MOE_ROWS, HALF_D), lambda i, *p: (blk(i, *p), 0)),
        scratch_shapes=[pltpu.VMEM((2, D_MODEL, 2 * D_FF), jnp.float32),
                        pltpu.VMEM((2, D_FF, D_MODEL), jnp.float32),
                        pltpu.SemaphoreType.DMA((2, 2))],
    )
    return pl.pallas_call(
        _moe_kernel,
        out_shape=jax.ShapeDtypeStruct((n_rows, HALF_D), jnp.uint32),
        grid_spec=grid_spec,
        compiler_params=pltpu.CompilerParams(
            dimension_semantics=("arbitrary",), vmem_limit_bytes=MOE_VMEM_LIMIT),
        name="moe_experts",
    )(*plan, xb, w1, b1, w2, b2)


def _combine_kernel(x1_ref, yg_ref, gate_ref, o_ref):
    acc_lo = x1_ref[:, :HALF_D]
    acc_hi = x1_ref[:, HALF_D:]
    for k in range(TOP_K):
        lo, hi = _unpack_halves(yg_ref[k])
        gk = gate_ref[:, k:k + 1]
        acc_lo = acc_lo + lo * gk
        acc_hi = acc_hi + hi * gk
    o_ref[:, :HALF_D] = acc_lo
    o_ref[:, HALF_D:] = acc_hi


def _combine(x1, yg, gates_nk):
    n = x1.shape[0]
    rows = PROJ_ROWS
    return pl.pallas_call(
        _combine_kernel,
        out_shape=jax.ShapeDtypeStruct((n, D_MODEL), jnp.float32),
        grid=(n // rows,),
        in_specs=[pl.BlockSpec((rows, D_MODEL), lambda i: (i, 0)),
                  pl.BlockSpec((TOP_K, rows, HALF_D), lambda i: (0, i, 0)),
                  pl.BlockSpec((rows, TOP_K), lambda i: (i, 0))],
        out_specs=pl.BlockSpec((rows, D_MODEL), lambda i: (i, 0)),
        compiler_params=pltpu.CompilerParams(
            dimension_semantics=("arbitrary",), vmem_limit_bytes=VMEM_LIMIT),
        name="moe_combine",
    )(x1, yg, gates_nk)


def _sc_worker_id():
    return lax.axis_index("s") * SC_CORES + lax.axis_index("c")


def _sc_dispatch(hf, dest3, n_rows):
    n = hf.shape[0]
    chunks_per_worker = n // SC_CHUNK // SC_WORKERS
    mesh = plsc.VectorSubcoreMesh(core_axis_name="c", subcore_axis_name="s")

    @functools.partial(
        pl.kernel, mesh=mesh,
        out_type=jax.ShapeDtypeStruct((n_rows, HALF_D), hf.dtype),
        scratch_types=[pltpu.VMEM((TOP_K, SC_CHUNK), jnp.int32),
                       pltpu.VMEM((SC_CHUNK, HALF_D), hf.dtype)],
        name="sc_dispatch")
    def run(hf_hbm, dest_hbm, xb_hbm, idx_v, rows_v):
        first = _sc_worker_id() * chunks_per_worker

        @pl.loop(0, chunks_per_worker)
        def _(j):
            ch = first + j
            pltpu.sync_copy(dest_hbm.at[ch], idx_v)
            pltpu.sync_copy(hf_hbm.at[pl.ds(ch * SC_CHUNK, SC_CHUNK)], rows_v)
            for k in range(TOP_K):
                pltpu.sync_copy(rows_v, xb_hbm.at[idx_v.at[k]])

    return run(hf, dest3)


def _sc_collect(y, dest3):
    n = dest3.shape[0] * SC_CHUNK
    chunks_per_worker = n // SC_CHUNK // SC_WORKERS
    mesh = plsc.VectorSubcoreMesh(core_axis_name="c", subcore_axis_name="s")

    @functools.partial(
        pl.kernel, mesh=mesh,
        out_type=jax.ShapeDtypeStruct((TOP_K, n, HALF_D), y.dtype),
        scratch_types=[pltpu.VMEM((TOP_K, SC_CHUNK), jnp.int32),
                       pltpu.VMEM((SC_CHUNK, HALF_D), y.dtype)],
        name="sc_collect")
    def run(y_hbm, dest_hbm, yg_hbm, idx_v, rows_v):
        first = _sc_worker_id() * chunks_per_worker

        @pl.loop(0, chunks_per_worker)
        def _(j):
            ch = first + j
            pltpu.sync_copy(dest_hbm.at[ch], idx_v)
            for k in range(TOP_K):
                pltpu.sync_copy(y_hbm.at[idx_v.at[k]], rows_v)
                pltpu.sync_copy(rows_v, yg_hbm.at[k, pl.ds(ch * SC_CHUNK, SC_CHUNK)])

    return run(y, dest3)


def _layer(x2, batch, seq, attn_norm_g, w_in, a_q_g, a_k_g, a_sinks, b_q_g, b_k_g, w_out,
           ffn_norm_g, w_router, b_router, w1, b1, w2, b2):
    n = x2.shape[0]
    slopes = _alibi_slopes()
    a_order = np.asarray(A_HEAD_ORDER)

    a_cols = (a_order[:, None] * HEAD_DIM + np.arange(HEAD_DIM)[None, :]).reshape(-1)
    col_perm = np.concatenate([a_cols, np.arange(A_Q_W, w_in.shape[1])])
    w_in_bf = w_in[:, col_perm].astype(jnp.bfloat16)
    row_perm = np.concatenate([a_cols, np.arange(A_Q_W, A_Q_W + B_W)])
    w_out_bf = w_out[row_perm].astype(jnp.bfloat16)
    q_scale = HEAD_DIM ** -0.5
    reps = MXU_DIM // HEAD_DIM
    gains = jnp.stack([jnp.tile(a_q_g, reps) * q_scale, jnp.tile(a_k_g, reps),
                       jnp.tile(b_q_g, reps) * q_scale, jnp.tile(b_k_g, reps)]).astype(jnp.float32)

    proj = _in_proj(x2, attn_norm_g.reshape(1, -1), w_in_bf, gains, batch, seq)
    qa, ka, va = proj[:3]
    nb = len(B_DILS)
    qbs, kbs, vbs = proj[3:3 + nb], proj[3 + nb:3 + 2 * nb], proj[3 + 2 * nb:]

    bias_a = _bias_tables(slopes[a_order], A_STACK_HEADS, A_HALF_WINDOW, 1, Q_TILE + 2 * A_HALF_WINDOW)
    sink_col = jnp.repeat(a_sinks[a_order].astype(jnp.float32), Q_TILE).reshape(
        A_Q_HEADS // A_STACK_HEADS, A_STACK_HEADS * Q_TILE, 1)
    as_seqs = lambda a: a.reshape(batch, seq, a.shape[-1])
    out_a = _banded_attention(as_seqs(qa), as_seqs(ka), as_seqs(va), bias_a, half_w=A_HALF_WINDOW,
                              sink=sink_col, name="attn_a")[0].reshape(n, A_Q_W)

    outs_b, lses_b = [], []
    for bi, (window, dil) in enumerate(B_BRANCHES):
        half_w = window // (2 * dil)
        bias_b = _bias_tables(slopes[A_Q_HEADS:], 2, half_w, dil, Q_TILE + 2 * half_w)
        L = seq // dil
        to_seqs = lambda a: a.reshape(batch * dil, L, a.shape[-1])
        o, lse = _banded_attention(to_seqs(qbs[bi]), to_seqs(kbs[bi]), to_seqs(vbs[bi]), bias_b,
                                   half_w=half_w, want_lse=True, name=f"attn_b_d{dil}")
        if dil == 1:
            outs_b.append(o.reshape(n, B_W))
            lses_b.append(lse.reshape(n, LANES))
        else:
            outs_b.append(o.reshape(batch, dil, L, B_W))
            lses_b.append(lse.reshape(batch, dil, L, LANES))

    wr_t = w_router.T.astype(jnp.float32)
    wr_hi = wr_t.astype(jnp.bfloat16)
    wr_lo = (wr_t - wr_hi.astype(jnp.float32)).astype(jnp.bfloat16)
    br = jnp.broadcast_to(b_router.astype(jnp.float32)[:, None], (N_EXPERTS, LANES))
    x1, hf, topi, gates, ranks, counts = _out_proj_router(
        out_a, outs_b, lses_b, x2, w_out_bf, ffn_norm_g.reshape(1, -1), wr_hi, wr_lo, br, seq)

    g = MOE_ROWS
    nk = n * TOP_K
    n_rows = nk + N_EXPERTS * g
    cnt = counts[:, 0]
    pcnt = (cnt + g - 1) // g * g
    pends = jnp.cumsum(pcnt)
    pstarts = pends - pcnt
    experts = jnp.arange(N_EXPERTS, dtype=jnp.int32)
    start_of = jnp.sum(jnp.where(topi[:, :, None] == experts, pstarts, 0), axis=-1)
    dest = (start_of + ranks).astype(jnp.int32)
    plan = _moe_plan(pends, n_rows // g)
    dest3 = dest.reshape(TOP_K, n // SC_CHUNK, SC_CHUNK).transpose(1, 0, 2)

    xb = _sc_dispatch(hf, dest3, n_rows)
    y = _moe_experts(plan, xb, w1, b1[:, None, :], w2, b2[:, None, :])
    yg = _sc_collect(y, dest3)
    return _combine(x1, yg, gates.T)


def kernel(x, attn_norm_g, w_in, a_q_norm_g, a_k_norm_g, a_sinks, b_q_norm_g, b_k_norm_g, w_out,
           ffn_norm_g, w_router, b_router, w1, b1, w2, b2):
    batch, seq, d = x.shape
    x2 = x.reshape(batch * seq, d)
    for i in range(attn_norm_g.shape[0]):
        x2 = _layer(x2, batch, seq, attn_norm_g[i], w_in[i], a_q_norm_g[i], a_k_norm_g[i],
                    a_sinks[i], b_q_norm_g[i], b_k_norm_g[i], w_out[i], ffn_norm_g[i],
                    w_router[i], b_router[i], w1[i], b1[i], w2[i], b2[i])
    return x2.reshape(batch, seq, d)
```

```python
import functools

import jax
import jax.numpy as jnp
import numpy as np
from jax import lax
from jax.experimental import pallas as pl
from jax.experimental.pallas import tpu as pltpu
from jax.experimental.pallas import tpu_sc as plsc

D_MODEL = 1024
HALF_D = D_MODEL // 2
HEAD_DIM = 64
LANES = 128
MXU_DIM = 256
A_Q_HEADS = 8
A_KV_HEADS = 2
B_HEADS = 8
A_HALF_WINDOW = 128
B_BRANCHES = ((128, 1), (512, 4), (2048, 16))
B_DILS = tuple(d for _, d in B_BRANCHES)
N_ALIBI_HEADS = 16
A_Q_W = A_Q_HEADS * HEAD_DIM
A_KV_W = A_KV_HEADS * HEAD_DIM
B_W = B_HEADS * HEAD_DIM
N_EXPERTS = 32
TOP_K = 4
D_FF = 1024
SWIGLU_ALPHA = 1.702
SWIGLU_LIMIT = 7.0
NORM_EPS = 1e-5
MASK_VALUE = -1e30

Q_TILE = 128
A_STACK_HEADS = 2
PROJ_ROWS = 512
MOE_ROWS = 256
FF_CHUNK = 512
VMEM_LIMIT = 48 * 1024 * 1024
MOE_VMEM_LIMIT = 58 * 1024 * 1024
SC_CORES = 2
SC_SUBCORES = 16
SC_WORKERS = SC_CORES * SC_SUBCORES
SC_CHUNK = 64
SC_COLLECT_CHUNK = 32
COMBINE_SPLITS = 4

A_HEAD_ORDER = (0, 4, 1, 5, 2, 6, 3, 7)


def _pack_halves(v):
    lo = v[:, :HALF_D].astype(jnp.bfloat16).astype(jnp.float32)
    hi = v[:, HALF_D:].astype(jnp.bfloat16).astype(jnp.float32)
    return (pltpu.bitcast(lo, jnp.uint32) >> 16) | pltpu.bitcast(hi, jnp.uint32)


def _unpack_halves(w):
    lo = pltpu.bitcast(w << 16, jnp.float32)
    hi = pltpu.bitcast(w & jnp.uint32(0xFFFF0000), jnp.float32)
    return lo, hi


def _alibi_slopes():
    return np.exp2(-8.0 * np.arange(1, N_ALIBI_HEADS + 1, dtype=np.float32) / N_ALIBI_HEADS).astype(np.float32)


def _bias_tables(head_slopes, heads_per_group, half_w, dist_scale, tk):
    i = np.arange(Q_TILE)[:, None]
    j = np.arange(tk)[None, :]
    tabs = []
    for shift in (0, half_w, tk - Q_TILE):
        dist = np.abs(j - shift - i)
        valid = dist <= half_w
        per_head = []
        for sl in head_slopes:
            b = -np.float32(sl) * (dist * dist_scale).astype(np.float32)
            per_head.append(np.where(valid, b, np.float32(MASK_VALUE)).astype(np.float32))
        t = np.stack(per_head).reshape(-1, heads_per_group * Q_TILE, tk)
        tabs.append(t)
    return jnp.asarray(np.stack(tabs))


def _in_proj_kernel(x_ref, g_ref, w_ref, gains_ref, qa_ref, ka_ref, va_ref, *rest):
    b_refs, scr_ref = rest[:-1], rest[-1]
    x = x_ref[...]
    xn = x * lax.rsqrt(jnp.mean(x * x, axis=-1, keepdims=True) + NORM_EPS) * g_ref[...]
    xn = xn.astype(jnp.bfloat16)
    r = lax.broadcasted_iota(jnp.int32, (MXU_DIM, MXU_DIM), 0) // HEAD_DIM
    c = lax.broadcasted_iota(jnp.int32, (MXU_DIM, MXU_DIM), 1) // HEAD_DIM
    blockdiag = jnp.where(r == c, 1.0, 0.0).astype(jnp.bfloat16)

    def project(col0, width, gain_row):
        sec = jnp.dot(xn, w_ref[:, col0:col0 + width], preferred_element_type=jnp.float32)
        if gain_row is None:
            return [sec]
        parts = []
        step = min(width, MXU_DIM)
        for j in range(width // step):
            p = sec[:, j * step:(j + 1) * step]
            ss = jnp.dot((p * p).astype(jnp.bfloat16), blockdiag[:step, :step],
                         preferred_element_type=jnp.float32)
            parts.append(p * lax.rsqrt(ss * (1.0 / HEAD_DIM) + NORM_EPS)
                         * gains_ref[gain_row:gain_row + 1, :step])
        return parts

    def store(out_ref, parts):
        w = parts[0].shape[1]
        for j, p in enumerate(parts):
            out_ref[:, j * w:(j + 1) * w] = p.astype(out_ref.dtype)

    store(qa_ref, project(0, A_Q_W, 0))
    store(ka_ref, project(A_Q_W, A_KV_W, 1))
    store(va_ref, project(A_Q_W + A_KV_W, A_KV_W, None))

    rows = x_ref.shape[0]
    col0 = A_Q_W + 2 * A_KV_W
    for t, gain_row in enumerate((2, 3, None)):
        parts = project(col0 + t * B_W, B_W, gain_row)
        sec = jnp.concatenate(parts, axis=-1) if len(parts) > 1 else parts[0]
        for j in range(B_W // LANES):
            scr_ref[j] = sec[:, j * LANES:(j + 1) * LANES]
        for bi, dil in enumerate(B_DILS):
            out_ref = b_refs[t * len(B_DILS) + bi]
            if dil == 1:
                out_ref[...] = sec.astype(out_ref.dtype)
            else:
                for res in range(dil):
                    for j in range(B_W // LANES):
                        out_ref[0, res, :, j * LANES:(j + 1) * LANES] = (
                            scr_ref[j, pl.ds(res, rows // dil, stride=dil), :].astype(out_ref.dtype))


def _in_proj(x2, g, w_bf, gains, batch, seq):
    n = x2.shape[0]
    rows = PROJ_ROWS
    steps = seq // rows
    out_shape = [jax.ShapeDtypeStruct((n, w), jnp.bfloat16) for w in (A_Q_W, A_KV_W, A_KV_W)]
    out_specs = [pl.BlockSpec((rows, w), lambda i: (i, 0)) for w in (A_Q_W, A_KV_W, A_KV_W)]
    for _ in range(3):
        for dil in B_DILS:
            if dil == 1:
                out_shape.append(jax.ShapeDtypeStruct((n, B_W), jnp.bfloat16))
                out_specs.append(pl.BlockSpec((rows, B_W), lambda i: (i, 0)))
            else:
                out_shape.append(jax.ShapeDtypeStruct((batch, dil, seq // dil, B_W), jnp.bfloat16))
                out_specs.append(pl.BlockSpec((1, dil, rows // dil, B_W),
                                              lambda i: (i // steps, 0, i % steps, 0)))
    return pl.pallas_call(
        _in_proj_kernel,
        out_shape=out_shape,
        grid=(n // rows,),
        in_specs=[
            pl.BlockSpec((rows, D_MODEL), lambda i: (i, 0)),
            pl.BlockSpec((1, D_MODEL), lambda i: (0, 0)),
            pl.BlockSpec(w_bf.shape, lambda i: (0, 0)),
            pl.BlockSpec(gains.shape, lambda i: (0, 0)),
        ],
        out_specs=out_specs,
        scratch_shapes=[pltpu.VMEM((B_W // LANES, rows, LANES), jnp.float32)],
        compiler_params=pltpu.CompilerParams(
            dimension_semantics=("arbitrary",), vmem_limit_bytes=VMEM_LIMIT),
        name="in_proj",
    )(x2, g, w_bf, gains)


def _attn_kernel(*refs, n_chunks, kv_chunks, chunks_per_stack, tk, half_w, seq_len, rows, has_sink,
                 want_lse):
    it = iter(refs)
    q_ref, k_ref, v_ref, bias_ref = next(it), next(it), next(it), next(it)
    sink_ref = next(it) if has_sink else None
    o_ref = next(it)
    lse_ref = next(it) if want_lse else None

    n_tiles = seq_len // Q_TILE
    tiles_per_step = rows // Q_TILE
    chunks_per_group = n_chunks // kv_chunks
    step = pl.program_id(1)
    lane = lax.broadcasted_iota(jnp.int32, (Q_TILE, LANES), 1)
    low_half = lane < HEAD_DIM
    ones = jnp.ones((tk, LANES), jnp.bfloat16)

    for t in range(tiles_per_step):
        tile = step * tiles_per_step + t
        q0 = tile * Q_TILE
        start = pl.multiple_of(jnp.clip(q0 - half_w, 0, seq_len - tk), HEAD_DIM)
        variant = jnp.where(tile == 0, 0, jnp.where(tile == n_tiles - 1, 2, 1))
        r0 = t * Q_TILE
        lse_tile = jnp.zeros((Q_TILE, LANES), jnp.float32)
        for g in range(n_chunks // chunks_per_stack):
            chunks = range(g * chunks_per_stack, (g + 1) * chunks_per_stack)
            kv = chunks[0] // chunks_per_group
            kc = k_ref[0, pl.ds(start, tk), kv * LANES:(kv + 1) * LANES]
            vc = v_ref[0, pl.ds(start, tk), kv * LANES:(kv + 1) * LANES]
            v_aug = jnp.concatenate([vc, ones], axis=1)
            q_parts = []
            for c in chunks:
                q2 = q_ref[0, r0:r0 + Q_TILE, c * LANES:(c + 1) * LANES]
                q_parts.append(jnp.where(low_half, q2, jnp.zeros_like(q2)))
                q_parts.append(jnp.where(low_half, jnp.zeros_like(q2), q2))
            qs = jnp.concatenate(q_parts, axis=0)
            s = lax.dot_general(qs, kc, (((1,), (1,)), ((), ())),
                                preferred_element_type=jnp.float32)
            s = s + bias_ref[variant, g]
            m = jnp.max(s, axis=-1, keepdims=True)
            if has_sink:
                m = jnp.maximum(m, sink_ref[g])
            p = jnp.exp(s - m)
            ov = jnp.dot(p.astype(jnp.bfloat16), v_aug, preferred_element_type=jnp.float32)
            l = ov[:, LANES:]
            if has_sink:
                l = l + jnp.exp(sink_ref[g] - m)
            o = ov[:, :LANES] * (1.0 / l)
            if want_lse:
                lse = m + jnp.log(l)
            for idx, c in enumerate(chunks):
                a0, b0 = 2 * idx * Q_TILE, (2 * idx + 1) * Q_TILE
                o2 = jnp.where(low_half, o[a0:a0 + Q_TILE], o[b0:b0 + Q_TILE])
                o_ref[0, r0:r0 + Q_TILE, c * LANES:(c + 1) * LANES] = o2.astype(o_ref.dtype)
                if want_lse:
                    lse_tile = jnp.where(lane == 2 * c, lse[a0:a0 + Q_TILE],
                                         jnp.where(lane == 2 * c + 1, lse[b0:b0 + Q_TILE], lse_tile))
        if want_lse:
            lse_ref[0, r0:r0 + Q_TILE, :] = lse_tile


def _banded_attention(q, k, v, bias, *, half_w, sink=None, want_lse=False, name):
    n_seq, L, qw = q.shape
    kw = k.shape[2]
    tk = Q_TILE + 2 * half_w
    rows = min(512, L)

    args = [q, k, v, bias]
    in_specs = [
        pl.BlockSpec((1, rows, qw), lambda s, i: (s, i, 0)),
        pl.BlockSpec((1, L, kw), lambda s, i: (s, 0, 0)),
        pl.BlockSpec((1, L, kw), lambda s, i: (s, 0, 0)),
        pl.BlockSpec(bias.shape, lambda s, i: (0, 0, 0, 0)),
    ]
    if sink is not None:
        args.append(sink)
        in_specs.append(pl.BlockSpec(sink.shape, lambda s, i: (0, 0, 0)))
    out_shape = [jax.ShapeDtypeStruct((n_seq, L, qw), jnp.bfloat16)]
    out_specs = [pl.BlockSpec((1, rows, qw), lambda s, i: (s, i, 0))]
    if want_lse:
        out_shape.append(jax.ShapeDtypeStruct((n_seq, L, LANES), jnp.float32))
        out_specs.append(pl.BlockSpec((1, rows, LANES), lambda s, i: (s, i, 0)))

    kern = functools.partial(
        _attn_kernel, n_chunks=qw // LANES, kv_chunks=kw // LANES,
        chunks_per_stack=bias.shape[2] // (2 * Q_TILE), tk=tk, half_w=half_w, seq_len=L,
        rows=rows, has_sink=sink is not None, want_lse=want_lse)
    return pl.pallas_call(
        kern,
        out_shape=out_shape,
        grid=(n_seq, L // rows),
        in_specs=in_specs,
        out_specs=out_specs,
        compiler_params=pltpu.CompilerParams(
            dimension_semantics=("arbitrary", "arbitrary"), vmem_limit_bytes=VMEM_LIMIT),
        name=name,
    )(*args)


def _out_proj_router_kernel(*refs):
    nb = len(B_DILS)
    oa_ref = refs[0]
    o_refs = refs[1:1 + nb]
    lse_refs = refs[1 + nb:1 + 2 * nb]
    (x_ref, wo_ref, g_ref, wr_ref, br_ref,
     x1_ref, hf_ref, topi_ref, gate_ref, rank_ref, cnt_ref,
     tri_ref, carry_ref, so_ref, sl_ref) = refs[1 + 2 * nb:]
    i = pl.program_id(0)
    rows = x_ref.shape[0]

    @pl.when(i == 0)
    def _():
        a = lax.broadcasted_iota(jnp.int32, (rows, rows), 0)
        b = lax.broadcasted_iota(jnp.int32, (rows, rows), 1)
        tri_ref[...] = jnp.where(a <= b, 1.0, 0.0).astype(jnp.bfloat16)
        carry_ref[...] = jnp.zeros_like(carry_ref)

    outs, lses = [], []
    for bi, dil in enumerate(B_DILS):
        if dil == 1:
            outs.append(o_refs[bi][...].astype(jnp.float32))
            lses.append(lse_refs[bi][...])
        else:
            for res in range(dil):
                for j in range(B_W // LANES):
                    so_ref[bi, j, pl.ds(res, rows // dil, stride=dil), :] = (
                        o_refs[bi][0, res, :, j * LANES:(j + 1) * LANES].astype(jnp.float32))
                sl_ref[bi, pl.ds(res, rows // dil, stride=dil), :] = lse_refs[bi][0, res]
            outs.append(jnp.concatenate([so_ref[bi, j] for j in range(B_W // LANES)], axis=-1))
            lses.append(sl_ref[bi])

    mx = functools.reduce(jnp.maximum, lses)
    es = [jnp.exp(l - mx) for l in lses]
    inv = 1.0 / functools.reduce(lambda a, b: a + b, es)
    eh = lax.broadcasted_iota(jnp.int32, (LANES, B_W), 0)
    ej = lax.broadcasted_iota(jnp.int32, (LANES, B_W), 1) // HEAD_DIM
    expand = jnp.where(eh == ej, 1.0, 0.0).astype(jnp.bfloat16)
    ob = jnp.zeros((rows, B_W), jnp.float32)
    for e, o in zip(es, outs):
        w = e * inv
        wide = jnp.dot(w.astype(jnp.bfloat16), expand, preferred_element_type=jnp.float32)
        ob = ob + wide * o

    attn = jnp.concatenate([oa_ref[...], ob.astype(jnp.bfloat16)], axis=-1)
    x1 = x_ref[...] + jnp.dot(attn, wo_ref[...], preferred_element_type=jnp.float32)
    x1_ref[...] = x1
    hf = x1 * lax.rsqrt(jnp.mean(x1 * x1, axis=-1, keepdims=True) + NORM_EPS) * g_ref[...]
    hf_hi = hf.astype(jnp.bfloat16)
    hf_ref[...] = _pack_halves(hf)
    hf_lo = (hf - hf_hi.astype(jnp.float32)).astype(jnp.bfloat16)

    nt = (((1,), (1,)), ((), ()))
    lg_hi = lax.dot_general(wr_ref[...], hf_hi, nt, preferred_element_type=jnp.float32)
    lg_lo = lax.dot_general(wr_ref[0:N_EXPERTS, :], hf_lo, nt, preferred_element_type=jnp.float32)
    logits = lg_hi[0:N_EXPERTS] + lg_hi[N_EXPERTS:] + lg_lo + br_ref[:, 0:1]

    eidx = lax.broadcasted_iota(jnp.int32, (N_EXPERTS, rows), 0)
    work = logits
    vals, sels = [], []
    for k in range(TOP_K):
        mk = jnp.max(work, axis=0, keepdims=True)
        ik = jnp.min(jnp.where(work == mk, eidx, N_EXPERTS), axis=0, keepdims=True)
        sel = eidx == ik
        work = jnp.where(sel, -jnp.inf, work)
        vals.append(mk)
        sels.append(sel)
        topi_ref[k:k + 1, :] = ik
    exps = [jnp.exp(vk - vals[0]) for vk in vals]
    denom = exps[0] + exps[1] + exps[2] + exps[3]
    ginv = 1.0 / denom
    for k in range(TOP_K):
        gate_ref[k:k + 1, :] = exps[k] * ginv

    onehot = jnp.zeros((N_EXPERTS, rows), jnp.float32)
    for sel in sels:
        onehot = onehot + jnp.where(sel, 1.0, 0.0)
    incl = jnp.dot(onehot.astype(jnp.bfloat16), tri_ref[...], preferred_element_type=jnp.float32)
    before = incl - onehot + carry_ref[:, 0:1]
    for k in range(TOP_K):
        rk = jnp.sum(jnp.where(sels[k], before, 0.0), axis=0, keepdims=True)
        rank_ref[k:k + 1, :] = rk.astype(jnp.int32)
    carry = carry_ref[...] + jnp.sum(onehot, axis=1, keepdims=True)
    carry_ref[...] = carry
    cnt_ref[...] = carry.astype(jnp.int32)


def _out_proj_router(oa, outs_b, lses_b, x2, wo_bf, g, wr, br, seq):
    n = x2.shape[0]
    rows = PROJ_ROWS
    steps = seq // rows
    row_spec = lambda w: pl.BlockSpec((rows, w), lambda i: (i, 0))
    full = lambda a: pl.BlockSpec(a.shape, lambda i: (0,) * a.ndim)
    col_spec = pl.BlockSpec((TOP_K, rows), lambda i: (0, i))

    def branch_spec(dil, w):
        if dil == 1:
            return row_spec(w)
        return pl.BlockSpec((1, dil, rows // dil, w), lambda i: (i // steps, 0, i % steps, 0))

    in_specs = ([row_spec(A_Q_W)]
                + [branch_spec(d, B_W) for d in B_DILS]
                + [branch_spec(d, LANES) for d in B_DILS]
                + [row_spec(D_MODEL), full(wo_bf), full(g), full(wr), full(br)])
    return pl.pallas_call(
        _out_proj_router_kernel,
        out_shape=[
            jax.ShapeDtypeStruct((n, D_MODEL), jnp.float32),
            jax.ShapeDtypeStruct((n, HALF_D), jnp.uint32),
            jax.ShapeDtypeStruct((TOP_K, n), jnp.int32),
            jax.ShapeDtypeStruct((TOP_K, n), jnp.float32),
            jax.ShapeDtypeStruct((TOP_K, n), jnp.int32),
            jax.ShapeDtypeStruct((N_EXPERTS, LANES), jnp.int32),
        ],
        grid=(n // rows,),
        in_specs=in_specs,
        out_specs=[row_spec(D_MODEL), row_spec(HALF_D), col_spec, col_spec, col_spec,
                   pl.BlockSpec((N_EXPERTS, LANES), lambda i: (0, 0))],
        scratch_shapes=[pltpu.VMEM((rows, rows), jnp.bfloat16),
                        pltpu.VMEM((N_EXPERTS, LANES), jnp.float32),
                        pltpu.VMEM((len(B_DILS), B_W // LANES, rows, LANES), jnp.float32),
                        pltpu.VMEM((len(B_DILS), rows, LANES), jnp.float32)],
        compiler_params=pltpu.CompilerParams(
            dimension_semantics=("arbitrary",), vmem_limit_bytes=VMEM_LIMIT),
        name="out_proj_router",
    )(oa, *outs_b, *lses_b, x2, wo_bf, g, wr, br)


def _mxu_dot(a_bf, w_f32):
    return lax.dot_general(a_bf, w_f32, (((1,), (0,)), ((), ())), preferred_element_type=jnp.float32)


def _moe_kernel(blk_exp_ref, first_ref, slot_ref, next_exp_ref, n_used_ref,
                x_ref, w1_hbm, b1_ref, w2_hbm, b2_ref, y_ref, w1_buf, w2_buf, sem):
    i = pl.program_id(0)

    def weight_copies(expert, slot):
        return (pltpu.make_async_copy(w1_hbm.at[expert], w1_buf.at[slot], sem.at[slot, 0]),
                pltpu.make_async_copy(w2_hbm.at[expert], w2_buf.at[slot], sem.at[slot, 1]))

    @pl.when(i < n_used_ref[0])
    def _():
        slot = slot_ref[i]

        @pl.when(i == 0)
        def _():
            for cp in weight_copies(blk_exp_ref[0], slot):
                cp.start()

        @pl.when(first_ref[i] == 1)
        def _():
            for cp in weight_copies(blk_exp_ref[i], slot):
                cp.wait()

            @pl.when(next_exp_ref[i] >= 0)
            def _():
                for cp in weight_copies(next_exp_ref[i], 1 - slot):
                    cp.start()

        x = jnp.concatenate(_unpack_halves(x_ref[...]), axis=-1).astype(jnp.bfloat16)
        acc = jnp.zeros((MOE_ROWS, D_MODEL), jnp.float32)
        for c in range(D_FF // FF_CHUNK):
            lo = c * FF_CHUNK
            glu = _mxu_dot(x, w1_buf[slot, :, lo:lo + FF_CHUNK]) + b1_ref[0, :, lo:lo + FF_CHUNK]
            lin = (_mxu_dot(x, w1_buf[slot, :, D_FF + lo:D_FF + lo + FF_CHUNK])
                   + b1_ref[0, :, D_FF + lo:D_FF + lo + FF_CHUNK])
            glu = jnp.minimum(glu, SWIGLU_LIMIT)
            lin = jnp.clip(lin, -SWIGLU_LIMIT, SWIGLU_LIMIT)
            act = glu * (1.0 / (1.0 + jnp.exp(-SWIGLU_ALPHA * glu))) * (lin + 1.0)
            acc = acc + _mxu_dot(act.astype(jnp.bfloat16), w2_buf[slot, lo:lo + FF_CHUNK, :])
        y_ref[...] = _pack_halves(acc + b2_ref[0])


def _moe_plan(pends, n_blk):
    g = MOE_ROWS
    blk_row0 = jnp.arange(n_blk, dtype=jnp.int32) * g
    blk_exp = jnp.minimum(jnp.sum(pends[None, :] <= blk_row0[:, None], axis=-1),
                          N_EXPERTS - 1).astype(jnp.int32)
    n_used = (pends[-1] // g).astype(jnp.int32)
    used = blk_row0 < pends[-1]
    prev_exp = jnp.concatenate([jnp.full((1,), -1, jnp.int32), blk_exp[:-1]])
    first = (used & (blk_exp != prev_exp)).astype(jnp.int32)
    slot = ((jnp.cumsum(first) - 1) % 2).astype(jnp.int32)
    pstarts = jnp.concatenate([jnp.zeros((1,), pends.dtype), pends[:-1]])
    nonempty = pends > pstarts
    experts = jnp.arange(N_EXPERTS, dtype=jnp.int32)
    later = nonempty[None, :] & (experts[None, :] > experts[:, None])
    next_nonempty = jnp.min(jnp.where(later, experts[None, :], N_EXPERTS), axis=-1)
    next_nonempty = jnp.where(next_nonempty == N_EXPERTS, -1, next_nonempty).astype(jnp.int32)
    next_exp = jnp.sum(jnp.where(blk_exp[:, None] == experts[None, :], next_nonempty[None, :], 0),
                       axis=-1).astype(jnp.int32)
    return blk_exp, first, slot, next_exp, n_used.reshape(1)


def _moe_experts(plan, xb, w1, b1, w2, b2):
    n_rows = xb.shape[0]
    n_blk = n_rows // MOE_ROWS

    def blk(i, *p):
        return jnp.minimum(i, p[-1][0] - 1)

    grid_spec = pltpu.PrefetchScalarGridSpec(
        num_scalar_prefetch=len(plan),
        grid=(n_blk,),
        in_specs=[
            pl.BlockSpec((MOE_ROWS, HALF_D), lambda i, *p: (blk(i, *p), 0)),
            pl.BlockSpec(memory_space=pl.ANY),
            pl.BlockSpec((1, 1, 2 * D_FF), lambda i, *p: (p[0][blk(i, *p)], 0, 0)),
            pl.BlockSpec(memory_space=pl.ANY),
            pl.BlockSpec((1, 1, D_MODEL), lambda i, *p: (p[0][blk(i, *p)], 0, 0)),
        ],
        out_specs=pl.BlockSpec((MOE_ROWS, HALF_D), lambda i, *p: (blk(i, *p), 0)),
        scratch_shapes=[pltpu.VMEM((2, D_MODEL, 2 * D_FF), jnp.float32),
                        pltpu.VMEM((2, D_FF, D_MODEL), jnp.float32),
                        pltpu.SemaphoreType.DMA((2, 2))],
    )
    return pl.pallas_call(
        _moe_kernel,
        out_shape=jax.ShapeDtypeStruct((n_rows, HALF_D), jnp.uint32),
        grid_spec=grid_spec,
        compiler_params=pltpu.CompilerParams(
            dimension_semantics=("arbitrary",), vmem_limit_bytes=MOE_VMEM_LIMIT),
        name="moe_experts",
    )(*plan, xb, w1, b1, w2, b2)


def _combine_kernel(x1_ref, yg_ref, gate_ref, o_ref):
    acc_lo = x1_ref[:, :HALF_D]
    acc_hi = x1_ref[:, HALF_D:]
    for k in range(TOP_K):
        lo, hi = _unpack_halves(yg_ref[k])
        gk = gate_ref[:, k:k + 1]
        acc_lo = acc_lo + lo * gk
        acc_hi = acc_hi + hi * gk
    o_ref[:, :HALF_D] = acc_lo
    o_ref[:, HALF_D:] = acc_hi


def _combine(acc, yg, gates_nk, split):
    n = acc.shape[0]
    rows = PROJ_ROWS
    steps = yg.shape[1] // rows
    first = split * steps
    return pl.pallas_call(
        _combine_kernel,
        out_shape=jax.ShapeDtypeStruct((n, D_MODEL), jnp.float32),
        grid=(steps,),
        in_specs=[pl.BlockSpec((rows, D_MODEL), lambda i: (first + i, 0)),
                  pl.BlockSpec((TOP_K, rows, HALF_D), lambda i: (0, i, 0)),
                  pl.BlockSpec((rows, TOP_K), lambda i: (first + i, 0))],
        out_specs=pl.BlockSpec((rows, D_MODEL), lambda i: (first + i, 0)),
        input_output_aliases={0: 0},
        compiler_params=pltpu.CompilerParams(
            dimension_semantics=("arbitrary",), vmem_limit_bytes=VMEM_LIMIT),
        name="moe_combine",
    )(acc, yg, gates_nk)


def _sc_worker_id():
    return lax.axis_index("s") * SC_CORES + lax.axis_index("c")


def _sc_dispatch(hf, dest3, n_rows):
    n = hf.shape[0]
    chunks_per_worker = n // SC_CHUNK // SC_WORKERS
    mesh = plsc.VectorSubcoreMesh(core_axis_name="c", subcore_axis_name="s")

    @functools.partial(
        pl.kernel, mesh=mesh,
        out_type=jax.ShapeDtypeStruct((n_rows, HALF_D), hf.dtype),
        scratch_types=[pltpu.VMEM((TOP_K, SC_CHUNK), jnp.int32),
                       pltpu.VMEM((SC_CHUNK, HALF_D), hf.dtype)],
        name="sc_dispatch")
    def run(hf_hbm, dest_hbm, xb_hbm, idx_v, rows_v):
        first = _sc_worker_id() * chunks_per_worker

        @pl.loop(0, chunks_per_worker)
        def _(j):
            ch = first + j
            pltpu.sync_copy(dest_hbm.at[ch], idx_v)
            pltpu.sync_copy(hf_hbm.at[pl.ds(ch * SC_CHUNK, SC_CHUNK)], rows_v)
            for k in range(TOP_K):
                pltpu.sync_copy(rows_v, xb_hbm.at[idx_v.at[k]])

    return run(hf, dest3)


def _sc_collect(y, dest3):
    n = dest3.shape[0] * SC_COLLECT_CHUNK
    chunks_per_worker = n // SC_COLLECT_CHUNK // SC_WORKERS
    mesh = plsc.VectorSubcoreMesh(core_axis_name="c", subcore_axis_name="s")

    @functools.partial(
        pl.kernel, mesh=mesh,
        out_type=jax.ShapeDtypeStruct((TOP_K, n, HALF_D), y.dtype),
        scratch_types=[pltpu.VMEM((TOP_K, SC_COLLECT_CHUNK), jnp.int32),
                       pltpu.VMEM((TOP_K, SC_COLLECT_CHUNK, HALF_D), y.dtype),
                       pltpu.SemaphoreType.DMA((TOP_K,)),
                       pltpu.SemaphoreType.DMA((TOP_K,))],
        name="sc_collect")
    def run(y_hbm, dest_hbm, yg_hbm, idx_v, rows_v, gather_sem, write_sem):
        first = _sc_worker_id() * chunks_per_worker

        @pl.loop(0, chunks_per_worker)
        def _(j):
            ch = first + j
            pltpu.sync_copy(dest_hbm.at[ch], idx_v)
            gathers = [pltpu.async_copy(y_hbm.at[idx_v.at[k]], rows_v.at[k], gather_sem.at[k])
                       for k in range(TOP_K)]
            writes = []
            for k in range(TOP_K):
                gathers[k].wait()
                writes.append(pltpu.async_copy(
                    rows_v.at[k], yg_hbm.at[k, pl.ds(ch * SC_COLLECT_CHUNK, SC_COLLECT_CHUNK)],
                    write_sem.at[k]))
            for w in writes:
                w.wait()

    return run(y, dest3)


def _layer(x2, batch, seq, attn_norm_g, w_in, a_q_g, a_k_g, a_sinks, b_q_g, b_k_g, w_out,
           ffn_norm_g, w_router, b_router, w1, b1, w2, b2):
    n = x2.shape[0]
    slopes = _alibi_slopes()
    a_order = np.asarray(A_HEAD_ORDER)

    a_cols = (a_order[:, None] * HEAD_DIM + np.arange(HEAD_DIM)[None, :]).reshape(-1)
    col_perm = np.concatenate([a_cols, np.arange(A_Q_W, w_in.shape[1])])
    w_in_bf = w_in[:, col_perm].astype(jnp.bfloat16)
    row_perm = np.concatenate([a_cols, np.arange(A_Q_W, A_Q_W + B_W)])
    w_out_bf = w_out[row_perm].astype(jnp.bfloat16)
    q_scale = HEAD_DIM ** -0.5
    reps = MXU_DIM // HEAD_DIM
    gains = jnp.stack([jnp.tile(a_q_g, reps) * q_scale, jnp.tile(a_k_g, reps),
                       jnp.tile(b_q_g, reps) * q_scale, jnp.tile(b_k_g, reps)]).astype(jnp.float32)

    proj = _in_proj(x2, attn_norm_g.reshape(1, -1), w_in_bf, gains, batch, seq)
    qa, ka, va = proj[:3]
    nb = len(B_DILS)
    qbs, kbs, vbs = proj[3:3 + nb], proj[3 + nb:3 + 2 * nb], proj[3 + 2 * nb:]

    bias_a = _bias_tables(slopes[a_order], A_STACK_HEADS, A_HALF_WINDOW, 1, Q_TILE + 2 * A_HALF_WINDOW)
    sink_col = jnp.repeat(a_sinks[a_order].astype(jnp.float32), Q_TILE).reshape(
        A_Q_HEADS // A_STACK_HEADS, A_STACK_HEADS * Q_TILE, 1)
    as_seqs = lambda a: a.reshape(batch, seq, a.shape[-1])
    out_a = _banded_attention(as_seqs(qa), as_seqs(ka), as_seqs(va), bias_a, half_w=A_HALF_WINDOW,
                              sink=sink_col, name="attn_a")[0].reshape(n, A_Q_W)

    outs_b, lses_b = [], []
    for bi, (window, dil) in enumerate(B_BRANCHES):
        half_w = window // (2 * dil)
        bias_b = _bias_tables(slopes[A_Q_HEADS:], 2, half_w, dil, Q_TILE + 2 * half_w)
        L = seq // dil
        to_seqs = lambda a: a.reshape(batch * dil, L, a.shape[-1])
        o, lse = _banded_attention(to_seqs(qbs[bi]), to_seqs(kbs[bi]), to_seqs(vbs[bi]), bias_b,
                                   half_w=half_w, want_lse=True, name=f"attn_b_d{dil}")
        if dil == 1:
            outs_b.append(o.reshape(n, B_W))
            lses_b.append(lse.reshape(n, LANES))
        else:
            outs_b.append(o.reshape(batch, dil, L, B_W))
            lses_b.append(lse.reshape(batch, dil, L, LANES))

    wr_t = w_router.T.astype(jnp.float32)
    wr_hi = wr_t.astype(jnp.bfloat16)
    wr_lo = (wr_t - wr_hi.astype(jnp.float32)).astype(jnp.bfloat16)
    wr = jnp.concatenate([wr_hi, wr_lo], axis=0)
    br = jnp.broadcast_to(b_router.astype(jnp.float32)[:, None], (N_EXPERTS, LANES))
    x1, hf, topi, gates, ranks, counts = _out_proj_router(
        out_a, outs_b, lses_b, x2, w_out_bf, ffn_norm_g.reshape(1, -1), wr, br, seq)

    g = MOE_ROWS
    nk = n * TOP_K
    n_rows = nk + N_EXPERTS * g
    cnt = counts[:, 0]
    pcnt = (cnt + g - 1) // g * g
    pends = jnp.cumsum(pcnt)
    pstarts = pends - pcnt
    experts = jnp.arange(N_EXPERTS, dtype=jnp.int32)
    start_of = jnp.sum(jnp.where(topi[:, :, None] == experts, pstarts, 0), axis=-1)
    dest = (start_of + ranks).astype(jnp.int32)
    plan = _moe_plan(pends, n_rows // g)
    dest3 = dest.reshape(TOP_K, n // SC_CHUNK, SC_CHUNK).transpose(1, 0, 2)

    xb = _sc_dispatch(hf, dest3, n_rows)
    y = _moe_experts(plan, xb, w1, b1[:, None, :], w2, b2[:, None, :])
    per_split = n // COMBINE_SPLITS
    dest_c = dest.reshape(TOP_K, n // SC_COLLECT_CHUNK, SC_COLLECT_CHUNK).transpose(1, 0, 2)
    chunks_per_split = per_split // SC_COLLECT_CHUNK
    gates_nk = gates.T
    out = x1
    for s in range(COMBINE_SPLITS):
        yg = _sc_collect(y, dest_c[s * chunks_per_split:(s + 1) * chunks_per_split])
        out = _combine(out, yg, gates_nk, s)
    return out


def kernel(x, attn_norm_g, w_in, a_q_norm_g, a_k_norm_g, a_sinks, b_q_norm_g, b_k_norm_g, w_out,
           ffn_norm_g, w_router, b_router, w1, b1, w2, b2):
    batch, seq, d = x.shape
    x2 = x.reshape(batch * seq, d)
    for i in range(attn_norm_g.shape[0]):
        x2 = _layer(x2, batch, seq, attn_norm_g[i], w_in[i], a_q_norm_g[i], a_k_norm_g[i],
                    a_sinks[i], b_q_norm_g[i], b_k_norm_g[i], w_out[i], ffn_norm_g[i],
                    w_router[i], b_router[i], w1[i], b1[i], w2[i], b2[i])
    return x2.reshape(batch, seq, d)
```

```python
import functools

import jax
import jax.numpy as jnp
import numpy as np
from jax import lax
from jax.experimental import pallas as pl
from jax.experimental.pallas import tpu as pltpu
from jax.experimental.pallas import tpu_sc as plsc

D_MODEL = 1024
HALF_D = D_MODEL // 2
HEAD_DIM = 64
LANES = 128
MXU_DIM = 256
A_Q_HEADS = 8
A_KV_HEADS = 2
B_HEADS = 8
A_HALF_WINDOW = 128
B_BRANCHES = ((128, 1), (512, 4), (2048, 16))
B_DILS = tuple(d for _, d in B_BRANCHES)
N_ALIBI_HEADS = 16
A_Q_W = A_Q_HEADS * HEAD_DIM
A_KV_W = A_KV_HEADS * HEAD_DIM
B_W = B_HEADS * HEAD_DIM
N_EXPERTS = 32
TOP_K = 4
D_FF = 1024
SWIGLU_ALPHA = 1.702
SWIGLU_LIMIT = 7.0
NORM_EPS = 1e-5
MASK_VALUE = -1e30

Q_TILE = 128
A_STACK_HEADS = 2
PROJ_ROWS = 512
MOE_ROWS = 512
FF_CHUNK = 512
MOE_SUB_ROWS = 512
VMEM_LIMIT = 48 * 1024 * 1024
MOE_VMEM_LIMIT = 58 * 1024 * 1024
SC_CORES = 2
SC_SUBCORES = 16
SC_WORKERS = SC_CORES * SC_SUBCORES
SC_CHUNK = 64
SC_COLLECT_CHUNK = 32
COMBINE_SPLITS = 4


def _pack_halves(v):
    lo = v[:, :HALF_D].astype(jnp.bfloat16).astype(jnp.float32)
    hi = v[:, HALF_D:].astype(jnp.bfloat16).astype(jnp.float32)
    return (pltpu.bitcast(lo, jnp.uint32) >> 16) | pltpu.bitcast(hi, jnp.uint32)


def _unpack_halves(w):
    lo = pltpu.bitcast(w << 16, jnp.float32)
    hi = pltpu.bitcast(w & jnp.uint32(0xFFFF0000), jnp.float32)
    return lo, hi


def _alibi_slopes():
    return np.exp2(-8.0 * np.arange(1, N_ALIBI_HEADS + 1, dtype=np.float32) / N_ALIBI_HEADS).astype(np.float32)


def _bias_tables(head_slopes, heads_per_group, half_w, dist_scale, tk):
    i = np.arange(Q_TILE)[:, None]
    j = np.arange(tk)[None, :]
    tabs = []
    for shift in (0, half_w, tk - Q_TILE):
        dist = np.abs(j - shift - i)
        valid = dist <= half_w
        per_head = []
        for sl in head_slopes:
            b = -np.float32(sl) * (dist * dist_scale).astype(np.float32)
            per_head.append(np.where(valid, b, np.float32(MASK_VALUE)).astype(np.float32))
        t = np.stack(per_head).reshape(-1, heads_per_group * Q_TILE, tk)
        tabs.append(t)
    return jnp.asarray(np.stack(tabs))


def _in_proj_kernel(x_ref, g_ref, w_hbm, gains_ref, qa_ref, ka_ref, va_ref, *rest):
    b_refs, (scr_ref, w_ref, stage_ref, sem) = rest[:-4], rest[-4:]

    @pl.when(pl.program_id(0) == 0)
    def _():
        width = stage_ref.shape[1]
        for c0 in range(0, w_ref.shape[1], width):
            cp = pltpu.make_async_copy(w_hbm.at[:, c0:c0 + width], stage_ref, sem)
            cp.start()
            cp.wait()
            w_ref[:, c0:c0 + width] = stage_ref[...].astype(jnp.bfloat16)

    x = x_ref[...]
    xn = x * lax.rsqrt(jnp.mean(x * x, axis=-1, keepdims=True) + NORM_EPS) * g_ref[...]
    xn = xn.astype(jnp.bfloat16)
    r = lax.broadcasted_iota(jnp.int32, (MXU_DIM, MXU_DIM), 0) // HEAD_DIM
    c = lax.broadcasted_iota(jnp.int32, (MXU_DIM, MXU_DIM), 1) // HEAD_DIM
    blockdiag = jnp.where(r == c, 1.0, 0.0).astype(jnp.bfloat16)

    def project(col0, width, gain_row):
        sec = jnp.dot(xn, w_ref[:, col0:col0 + width], preferred_element_type=jnp.float32)
        if gain_row is None:
            return [sec]
        parts = []
        step = min(width, MXU_DIM)
        for j in range(width // step):
            p = sec[:, j * step:(j + 1) * step]
            ss = jnp.dot((p * p).astype(jnp.bfloat16), blockdiag[:step, :step],
                         preferred_element_type=jnp.float32)
            parts.append(p * lax.rsqrt(ss * (1.0 / HEAD_DIM) + NORM_EPS)
                         * gains_ref[gain_row:gain_row + 1, :step])
        return parts

    def store(out_ref, parts):
        w = parts[0].shape[1]
        for j, p in enumerate(parts):
            out_ref[:, j * w:(j + 1) * w] = p.astype(out_ref.dtype)

    def per_kv_head(p):
        lane = lax.broadcasted_iota(jnp.int32, p.shape, 1)
        swapped = pltpu.roll(p, HEAD_DIM, axis=1)
        low = lane < HEAD_DIM
        return [jnp.where(low, p, swapped), jnp.where(low, swapped, p)]

    store(qa_ref, project(0, A_Q_W, 0))
    store(ka_ref, per_kv_head(project(A_Q_W, A_KV_W, 1)[0]))
    store(va_ref, per_kv_head(project(A_Q_W + A_KV_W, A_KV_W, None)[0]))

    rows = x_ref.shape[0]
    col0 = A_Q_W + 2 * A_KV_W
    for t, gain_row in enumerate((2, 3, None)):
        parts = project(col0 + t * B_W, B_W, gain_row)
        sec = jnp.concatenate(parts, axis=-1) if len(parts) > 1 else parts[0]
        for j in range(B_W // LANES):
            scr_ref[j] = sec[:, j * LANES:(j + 1) * LANES]
        for bi, dil in enumerate(B_DILS):
            out_ref = b_refs[t * len(B_DILS) + bi]
            if dil == 1:
                out_ref[...] = sec.astype(out_ref.dtype)
            else:
                for res in range(dil):
                    for j in range(B_W // LANES):
                        out_ref[0, res, :, j * LANES:(j + 1) * LANES] = (
                            scr_ref[j, pl.ds(res, rows // dil, stride=dil), :].astype(out_ref.dtype))


def _in_proj(x2, g, w_in, gains, batch, seq):
    n = x2.shape[0]
    rows = PROJ_ROWS
    steps = seq // rows
    a_widths = (A_Q_W, 2 * A_KV_W, 2 * A_KV_W)
    out_shape = [jax.ShapeDtypeStruct((n, w), jnp.bfloat16) for w in a_widths]
    out_specs = [pl.BlockSpec((rows, w), lambda i: (i, 0)) for w in a_widths]
    for _ in range(3):
        for dil in B_DILS:
            if dil == 1:
                out_shape.append(jax.ShapeDtypeStruct((n, B_W), jnp.bfloat16))
                out_specs.append(pl.BlockSpec((rows, B_W), lambda i: (i, 0)))
            else:
                out_shape.append(jax.ShapeDtypeStruct((batch, dil, seq // dil, B_W), jnp.bfloat16))
                out_specs.append(pl.BlockSpec((1, dil, rows // dil, B_W),
                                              lambda i: (i // steps, 0, i % steps, 0)))
    return pl.pallas_call(
        _in_proj_kernel,
        out_shape=out_shape,
        grid=(n // rows,),
        in_specs=[
            pl.BlockSpec((rows, D_MODEL), lambda i: (i, 0)),
            pl.BlockSpec((1, D_MODEL), lambda i: (0, 0)),
            pl.BlockSpec(memory_space=pl.ANY),
            pl.BlockSpec(gains.shape, lambda i: (0, 0)),
        ],
        out_specs=out_specs,
        scratch_shapes=[pltpu.VMEM((B_W // LANES, rows, LANES), jnp.float32),
                        pltpu.VMEM(w_in.shape, jnp.bfloat16),
                        pltpu.VMEM((w_in.shape[0], w_in.shape[1] // 3), jnp.float32),
                        pltpu.SemaphoreType.DMA],
        compiler_params=pltpu.CompilerParams(
            dimension_semantics=("arbitrary",), vmem_limit_bytes=VMEM_LIMIT),
        name="in_proj",
    )(x2, g, w_in, gains)


def _attn_kernel(*refs, n_chunks, kv_chunks, chunks_per_stack, tk, half_w, seq_len, rows, has_sink,
                 want_lse):
    it = iter(refs)
    q_ref, k_ref, v_ref, bias_ref = next(it), next(it), next(it), next(it)
    sink_ref = next(it) if has_sink else None
    o_ref = next(it)
    lse_ref = next(it) if want_lse else None

    n_tiles = seq_len // Q_TILE
    tiles_per_step = rows // Q_TILE
    chunks_per_group = n_chunks // kv_chunks
    step = pl.program_id(1)
    lane = lax.broadcasted_iota(jnp.int32, (Q_TILE, LANES), 1)
    low_half = lane < HEAD_DIM
    ones = jnp.ones((tk, LANES), jnp.bfloat16)

    for t in range(tiles_per_step):
        tile = step * tiles_per_step + t
        q0 = tile * Q_TILE
        start = pl.multiple_of(jnp.clip(q0 - half_w, 0, seq_len - tk), HEAD_DIM)
        variant = jnp.where(tile == 0, 0, jnp.where(tile == n_tiles - 1, 2, 1))
        r0 = t * Q_TILE
        lse_tile = jnp.zeros((Q_TILE, LANES), jnp.float32)
        for g in range(n_chunks // chunks_per_stack):
            chunks = range(g * chunks_per_stack, (g + 1) * chunks_per_stack)
            kv = chunks[0] // chunks_per_group
            kc = k_ref[0, pl.ds(start, tk), kv * LANES:(kv + 1) * LANES]
            vc = v_ref[0, pl.ds(start, tk), kv * LANES:(kv + 1) * LANES]
            v_aug = jnp.concatenate([vc, ones], axis=1)
            q_parts = []
            for c in chunks:
                q2 = q_ref[0, r0:r0 + Q_TILE, c * LANES:(c + 1) * LANES]
                q_parts.append(jnp.where(low_half, q2, jnp.zeros_like(q2)))
                q_parts.append(jnp.where(low_half, jnp.zeros_like(q2), q2))
            qs = jnp.concatenate(q_parts, axis=0)
            s = lax.dot_general(qs, kc, (((1,), (1,)), ((), ())),
                                preferred_element_type=jnp.float32)
            s = s + bias_ref[variant, g]
            m = jnp.max(s, axis=-1, keepdims=True)
            if has_sink:
                m = jnp.maximum(m, sink_ref[g])
            p = jnp.exp(s - m)
            ov = jnp.dot(p.astype(jnp.bfloat16), v_aug, preferred_element_type=jnp.float32)
            l = ov[:, LANES:]
            if has_sink:
                l = l + jnp.exp(sink_ref[g] - m)
            o = ov[:, :LANES] * (1.0 / l)
            if want_lse:
                lse = m + jnp.log(l)
            for idx, c in enumerate(chunks):
                a0, b0 = 2 * idx * Q_TILE, (2 * idx + 1) * Q_TILE
                o2 = jnp.where(low_half, o[a0:a0 + Q_TILE], o[b0:b0 + Q_TILE])
                o_ref[0, r0:r0 + Q_TILE, c * LANES:(c + 1) * LANES] = o2.astype(o_ref.dtype)
                if want_lse:
                    lse_tile = jnp.where(lane == 2 * c, lse[a0:a0 + Q_TILE],
                                         jnp.where(lane == 2 * c + 1, lse[b0:b0 + Q_TILE], lse_tile))
        if want_lse:
            lse_ref[0, r0:r0 + Q_TILE, :] = lse_tile


def _banded_attention(q, k, v, bias, *, half_w, sink=None, want_lse=False, name):
    n_seq, L, qw = q.shape
    kw = k.shape[2]
    tk = Q_TILE + 2 * half_w
    rows = min(512, L)

    args = [q, k, v, bias]
    in_specs = [
        pl.BlockSpec((1, rows, qw), lambda s, i: (s, i, 0)),
        pl.BlockSpec((1, L, kw), lambda s, i: (s, 0, 0)),
        pl.BlockSpec((1, L, kw), lambda s, i: (s, 0, 0)),
        pl.BlockSpec(bias.shape, lambda s, i: (0, 0, 0, 0)),
    ]
    if sink is not None:
        args.append(sink)
        in_specs.append(pl.BlockSpec(sink.shape, lambda s, i: (0, 0, 0)))
    out_shape = [jax.ShapeDtypeStruct((n_seq, L, qw), jnp.bfloat16)]
    out_specs = [pl.BlockSpec((1, rows, qw), lambda s, i: (s, i, 0))]
    if want_lse:
        out_shape.append(jax.ShapeDtypeStruct((n_seq, L, LANES), jnp.float32))
        out_specs.append(pl.BlockSpec((1, rows, LANES), lambda s, i: (s, i, 0)))

    kern = functools.partial(
        _attn_kernel, n_chunks=qw // LANES, kv_chunks=kw // LANES,
        chunks_per_stack=bias.shape[2] // (2 * Q_TILE), tk=tk, half_w=half_w, seq_len=L,
        rows=rows, has_sink=sink is not None, want_lse=want_lse)
    return pl.pallas_call(
        kern,
        out_shape=out_shape,
        grid=(n_seq, L // rows),
        in_specs=in_specs,
        out_specs=out_specs,
        compiler_params=pltpu.CompilerParams(
            dimension_semantics=("arbitrary", "arbitrary"), vmem_limit_bytes=VMEM_LIMIT),
        name=name,
    )(*args)


def _out_proj_router_kernel(*refs):
    nb = len(B_DILS)
    oa_ref = refs[0]
    o_refs = refs[1:1 + nb]
    lse_refs = refs[1 + nb:1 + 2 * nb]
    (x_ref, wo_ref, g_ref, wr_ref, br_ref,
     x1_ref, hf_ref, topi_ref, gate_ref, rank_ref, cnt_ref,
     tri_ref, carry_ref, so_ref, sl_ref, wo_bf_ref) = refs[1 + 2 * nb:]
    i = pl.program_id(0)
    rows = x_ref.shape[0]

    @pl.when(i == 0)
    def _():
        a = lax.broadcasted_iota(jnp.int32, (rows, rows), 0)
        b = lax.broadcasted_iota(jnp.int32, (rows, rows), 1)
        tri_ref[...] = jnp.where(a <= b, 1.0, 0.0).astype(jnp.bfloat16)
        carry_ref[...] = jnp.zeros_like(carry_ref)
        wo_bf_ref[...] = wo_ref[...].astype(jnp.bfloat16)

    outs, lses = [], []
    for bi, dil in enumerate(B_DILS):
        if dil == 1:
            outs.append(o_refs[bi][...].astype(jnp.float32))
            lses.append(lse_refs[bi][...])
        else:
            for res in range(dil):
                for j in range(B_W // LANES):
                    so_ref[bi, j, pl.ds(res, rows // dil, stride=dil), :] = (
                        o_refs[bi][0, res, :, j * LANES:(j + 1) * LANES].astype(jnp.float32))
                sl_ref[bi, pl.ds(res, rows // dil, stride=dil), :] = lse_refs[bi][0, res]
            outs.append(jnp.concatenate([so_ref[bi, j] for j in range(B_W // LANES)], axis=-1))
            lses.append(sl_ref[bi])

    mx = functools.reduce(jnp.maximum, lses)
    es = [jnp.exp(l - mx) for l in lses]
    inv = 1.0 / functools.reduce(lambda a, b: a + b, es)
    eh = lax.broadcasted_iota(jnp.int32, (LANES, B_W), 0)
    ej = lax.broadcasted_iota(jnp.int32, (LANES, B_W), 1) // HEAD_DIM
    expand = jnp.where(eh == ej, 1.0, 0.0).astype(jnp.bfloat16)
    ob = jnp.zeros((rows, B_W), jnp.float32)
    for e, o in zip(es, outs):
        w = e * inv
        wide = jnp.dot(w.astype(jnp.bfloat16), expand, preferred_element_type=jnp.float32)
        ob = ob + wide * o

    attn = jnp.concatenate([oa_ref[...], ob.astype(jnp.bfloat16)], axis=-1)
    x1 = x_ref[...] + jnp.dot(attn, wo_bf_ref[...], preferred_element_type=jnp.float32)
    x1_ref[...] = x1
    hf = x1 * lax.rsqrt(jnp.mean(x1 * x1, axis=-1, keepdims=True) + NORM_EPS) * g_ref[...]
    hf_hi = hf.astype(jnp.bfloat16)
    hf_ref[...] = _pack_halves(hf)
    hf_lo = (hf - hf_hi.astype(jnp.float32)).astype(jnp.bfloat16)

    nt = (((1,), (1,)), ((), ()))
    lg_hi = lax.dot_general(wr_ref[...], hf_hi, nt, preferred_element_type=jnp.float32)
    lg_lo = lax.dot_general(wr_ref[0:N_EXPERTS, :], hf_lo, nt, preferred_element_type=jnp.float32)
    logits = lg_hi[0:N_EXPERTS] + lg_hi[N_EXPERTS:] + lg_lo + br_ref[:, 0:1]

    eidx = lax.broadcasted_iota(jnp.int32, (N_EXPERTS, rows), 0)
    work = logits
    vals, sels = [], []
    for k in range(TOP_K):
        mk = jnp.max(work, axis=0, keepdims=True)
        ik = jnp.min(jnp.where(work == mk, eidx, N_EXPERTS), axis=0, keepdims=True)
        sel = eidx == ik
        work = jnp.where(sel, -jnp.inf, work)
        vals.append(mk)
        sels.append(sel)
        topi_ref[k:k + 1, :] = ik
    exps = [jnp.exp(vk - vals[0]) for vk in vals]
    denom = exps[0] + exps[1] + exps[2] + exps[3]
    ginv = 1.0 / denom
    for k in range(TOP_K):
        gate_ref[k:k + 1, :] = exps[k] * ginv

    onehot = jnp.zeros((N_EXPERTS, rows), jnp.float32)
    for sel in sels:
        onehot = onehot + jnp.where(sel, 1.0, 0.0)
    incl = jnp.dot(onehot.astype(jnp.bfloat16), tri_ref[...], preferred_element_type=jnp.float32)
    before = incl - onehot + carry_ref[:, 0:1]
    for k in range(TOP_K):
        rk = jnp.sum(jnp.where(sels[k], before, 0.0), axis=0, keepdims=True)
        rank_ref[k:k + 1, :] = rk.astype(jnp.int32)
    carry = carry_ref[...] + jnp.sum(onehot, axis=1, keepdims=True)
    carry_ref[...] = carry
    cnt_ref[...] = carry.astype(jnp.int32)


def _out_proj_router(oa, outs_b, lses_b, x2, wo_bf, g, wr, br, seq):
    n = x2.shape[0]
    rows = PROJ_ROWS
    steps = seq // rows
    row_spec = lambda w: pl.BlockSpec((rows, w), lambda i: (i, 0))
    full = lambda a: pl.BlockSpec(a.shape, lambda i: (0,) * a.ndim)
    col_spec = pl.BlockSpec((TOP_K, rows), lambda i: (0, i))

    def branch_spec(dil, w):
        if dil == 1:
            return row_spec(w)
        return pl.BlockSpec((1, dil, rows // dil, w), lambda i: (i // steps, 0, i % steps, 0))

    in_specs = ([row_spec(A_Q_W)]
                + [branch_spec(d, B_W) for d in B_DILS]
                + [branch_spec(d, LANES) for d in B_DILS]
                + [row_spec(D_MODEL), full(wo_bf), full(g), full(wr), full(br)])
    return pl.pallas_call(
        _out_proj_router_kernel,
        out_shape=[
            jax.ShapeDtypeStruct((n, D_MODEL), jnp.float32),
            jax.ShapeDtypeStruct((n, HALF_D), jnp.uint32),
            jax.ShapeDtypeStruct((TOP_K, n), jnp.int32),
            jax.ShapeDtypeStruct((TOP_K, n), jnp.float32),
            jax.ShapeDtypeStruct((TOP_K, n), jnp.int32),
            jax.ShapeDtypeStruct((N_EXPERTS, LANES), jnp.int32),
        ],
        grid=(n // rows,),
        in_specs=in_specs,
        out_specs=[row_spec(D_MODEL), row_spec(HALF_D), col_spec, col_spec, col_spec,
                   pl.BlockSpec((N_EXPERTS, LANES), lambda i: (0, 0))],
        scratch_shapes=[pltpu.VMEM((rows, rows), jnp.bfloat16),
                        pltpu.VMEM((N_EXPERTS, LANES), jnp.float32),
                        pltpu.VMEM((len(B_DILS), B_W // LANES, rows, LANES), jnp.float32),
                        pltpu.VMEM((len(B_DILS), rows, LANES), jnp.float32),
                        pltpu.VMEM(wo_bf.shape, jnp.bfloat16)],
        compiler_params=pltpu.CompilerParams(
            dimension_semantics=("arbitrary",), vmem_limit_bytes=VMEM_LIMIT),
        name="out_proj_router",
    )(oa, *outs_b, *lses_b, x2, wo_bf, g, wr, br)


def _mxu_dot(a_bf, w_f32):
    return lax.dot_general(a_bf, w_f32, (((1,), (0,)), ((), ())), preferred_element_type=jnp.float32)


def _moe_kernel(blk_exp_ref, first_ref, slot_ref, next_exp_ref, n_used_ref,
                x_ref, w1_hbm, b1_ref, w2_hbm, b2_ref, y_ref, w1_buf, w2_buf, sem):
    i = pl.program_id(0)

    def weight_copies(expert, slot):
        return (pltpu.make_async_copy(w1_hbm.at[expert], w1_buf.at[slot], sem.at[slot, 0]),
                pltpu.make_async_copy(w2_hbm.at[expert], w2_buf.at[slot], sem.at[slot, 1]))

    @pl.when(i < n_used_ref[0])
    def _():
        slot = slot_ref[i]

        @pl.when(i == 0)
        def _():
            for cp in weight_copies(blk_exp_ref[0], slot):
                cp.start()

        @pl.when(first_ref[i] == 1)
        def _():
            for cp in weight_copies(blk_exp_ref[i], slot):
                cp.wait()

            @pl.when(next_exp_ref[i] >= 0)
            def _():
                for cp in weight_copies(next_exp_ref[i], 1 - slot):
                    cp.start()

        for r0 in range(0, MOE_ROWS, MOE_SUB_ROWS):
            rs = slice(r0, r0 + MOE_SUB_ROWS)
            x = jnp.concatenate(_unpack_halves(x_ref[rs, :]), axis=-1).astype(jnp.bfloat16)
            acc = jnp.zeros((MOE_SUB_ROWS, D_MODEL), jnp.float32)
            for c in range(D_FF // FF_CHUNK):
                lo = c * FF_CHUNK
                glu = _mxu_dot(x, w1_buf[slot, :, lo:lo + FF_CHUNK]) + b1_ref[0, :, lo:lo + FF_CHUNK]
                lin = (_mxu_dot(x, w1_buf[slot, :, D_FF + lo:D_FF + lo + FF_CHUNK])
                       + b1_ref[0, :, D_FF + lo:D_FF + lo + FF_CHUNK])
                glu = jnp.minimum(glu, SWIGLU_LIMIT)
                lin = jnp.clip(lin, -SWIGLU_LIMIT, SWIGLU_LIMIT)
                act = glu * (1.0 / (1.0 + jnp.exp(-SWIGLU_ALPHA * glu))) * (lin + 1.0)
                acc = acc + _mxu_dot(act.astype(jnp.bfloat16), w2_buf[slot, lo:lo + FF_CHUNK, :])
            y_ref[rs, :] = _pack_halves(acc + b2_ref[0])


def _moe_plan(pends, n_blk):
    g = MOE_ROWS
    blk_row0 = jnp.arange(n_blk, dtype=jnp.int32) * g
    blk_exp = jnp.minimum(jnp.sum(pends[None, :] <= blk_row0[:, None], axis=-1),
                          N_EXPERTS - 1).astype(jnp.int32)
    n_used = (pends[-1] // g).astype(jnp.int32)
    used = blk_row0 < pends[-1]
    prev_exp = jnp.concatenate([jnp.full((1,), -1, jnp.int32), blk_exp[:-1]])
    first = (used & (blk_exp != prev_exp)).astype(jnp.int32)
    slot = ((jnp.cumsum(first) - 1) % 2).astype(jnp.int32)
    pstarts = jnp.concatenate([jnp.zeros((1,), pends.dtype), pends[:-1]])
    nonempty = pends > pstarts
    experts = jnp.arange(N_EXPERTS, dtype=jnp.int32)
    later = nonempty[None, :] & (experts[None, :] > experts[:, None])
    next_nonempty = jnp.min(jnp.where(later, experts[None, :], N_EXPERTS), axis=-1)
    next_nonempty = jnp.where(next_nonempty == N_EXPERTS, -1, next_nonempty).astype(jnp.int32)
    next_exp = jnp.sum(jnp.where(blk_exp[:, None] == experts[None, :], next_nonempty[None, :], 0),
                       axis=-1).astype(jnp.int32)
    return blk_exp, first, slot, next_exp, n_used.reshape(1)


def _moe_experts(plan, xb, w1, b1, w2, b2):
    n_rows = xb.shape[0]
    n_blk = n_rows // MOE_ROWS

    def blk(i, *p):
        return jnp.minimum(i, p[-1][0] - 1)

    grid_spec = pltpu.PrefetchScalarGridSpec(
        num_scalar_prefetch=len(plan),
        grid=(n_blk,),
        in_specs=[
            pl.BlockSpec((MOE_ROWS, HALF_D), lambda i, *p: (blk(i, *p), 0)),
            pl.BlockSpec(memory_space=pl.ANY),
            pl.BlockSpec((1, 1, 2 * D_FF), lambda i, *p: (p[0][blk(i, *p)], 0, 0)),
            pl.BlockSpec(memory_space=pl.ANY),
            pl.BlockSpec((1, 1, D_MODEL), lambda i, *p: (p[0][blk(i, *p)], 0, 0)),
        ],
        out_specs=pl.BlockSpec((MOE_ROWS, HALF_D), lambda i, *p: (blk(i, *p), 0)),
        scratch_shapes=[pltpu.VMEM((2, D_MODEL, 2 * D_FF), jnp.float32),
                        pltpu.VMEM((2, D_FF, D_MODEL), jnp.float32),
                        pltpu.SemaphoreType.DMA((2, 2))],
    )
    return pl.pallas_call(
        _moe_kernel,
        out_shape=jax.ShapeDtypeStruct((n_rows, HALF_D), jnp.uint32),
        grid_spec=grid_spec,
        compiler_params=pltpu.CompilerParams(
            dimension_semantics=("arbitrary",), vmem_limit_bytes=MOE_VMEM_LIMIT),
        name="moe_experts",
    )(*plan, xb, w1, b1, w2, b2)


def _combine_kernel(x1_ref, yg_ref, gate_ref, o_ref):
    acc_lo = x1_ref[:, :HALF_D]
    acc_hi = x1_ref[:, HALF_D:]
    rows = x1_ref.shape[0]
    gates = jnp.concatenate([gate_ref[...], jnp.zeros((8 - TOP_K, rows), jnp.float32)], axis=0).T
    for k in range(TOP_K):
        lo, hi = _unpack_halves(yg_ref[k])
        gk = gates[:, k:k + 1]
        acc_lo = acc_lo + lo * gk
        acc_hi = acc_hi + hi * gk
    o_ref[:, :HALF_D] = acc_lo
    o_ref[:, HALF_D:] = acc_hi


def _combine(acc, yg, gates_nk, split):
    n = acc.shape[0]
    rows = PROJ_ROWS
    steps = yg.shape[1] // rows
    first = split * steps
    return pl.pallas_call(
        _combine_kernel,
        out_shape=jax.ShapeDtypeStruct((n, D_MODEL), jnp.float32),
        grid=(steps,),
        in_specs=[pl.BlockSpec((rows, D_MODEL), lambda i: (first + i, 0)),
                  pl.BlockSpec((TOP_K, rows, HALF_D), lambda i: (0, i, 0)),
                  pl.BlockSpec((TOP_K, rows), lambda i: (0, first + i))],
        out_specs=pl.BlockSpec((rows, D_MODEL), lambda i: (first + i, 0)),
        input_output_aliases={0: 0},
        compiler_params=pltpu.CompilerParams(
            dimension_semantics=("arbitrary",), vmem_limit_bytes=VMEM_LIMIT),
        name="moe_combine",
    )(acc, yg, gates_nk)


def _sc_worker_id():
    return lax.axis_index("s") * SC_CORES + lax.axis_index("c")


def _sc_dispatch(hf, dest3, n_rows):
    n = hf.shape[0]
    chunks_per_worker = n // SC_CHUNK // SC_WORKERS
    mesh = plsc.VectorSubcoreMesh(core_axis_name="c", subcore_axis_name="s")

    @functools.partial(
        pl.kernel, mesh=mesh,
        out_type=jax.ShapeDtypeStruct((n_rows, HALF_D), hf.dtype),
        scratch_types=[pltpu.VMEM((TOP_K, SC_CHUNK), jnp.int32),
                       pltpu.VMEM((SC_CHUNK, HALF_D), hf.dtype)],
        name="sc_dispatch")
    def run(hf_hbm, dest_hbm, xb_hbm, idx_v, rows_v):
        first = _sc_worker_id() * chunks_per_worker

        @pl.loop(0, chunks_per_worker)
        def _(j):
            ch = first + j
            pltpu.sync_copy(dest_hbm.at[ch], idx_v)
            pltpu.sync_copy(hf_hbm.at[pl.ds(ch * SC_CHUNK, SC_CHUNK)], rows_v)
            for k in range(TOP_K):
                pltpu.sync_copy(rows_v, xb_hbm.at[idx_v.at[k]])

    return run(hf, dest3)


def _sc_collect(y, dest3):
    n = dest3.shape[0] * SC_COLLECT_CHUNK
    chunks_per_worker = n // SC_COLLECT_CHUNK // SC_WORKERS
    mesh = plsc.VectorSubcoreMesh(core_axis_name="c", subcore_axis_name="s")

    @functools.partial(
        pl.kernel, mesh=mesh,
        out_type=jax.ShapeDtypeStruct((TOP_K, n, HALF_D), y.dtype),
        scratch_types=[pltpu.VMEM((TOP_K, SC_COLLECT_CHUNK), jnp.int32),
                       pltpu.VMEM((TOP_K, SC_COLLECT_CHUNK, HALF_D), y.dtype),
                       pltpu.SemaphoreType.DMA((TOP_K,)),
                       pltpu.SemaphoreType.DMA((TOP_K,))],
        name="sc_collect")
    def run(y_hbm, dest_hbm, yg_hbm, idx_v, rows_v, gather_sem, write_sem):
        first = _sc_worker_id() * chunks_per_worker

        @pl.loop(0, chunks_per_worker)
        def _(j):
            ch = first + j
            pltpu.sync_copy(dest_hbm.at[ch], idx_v)
            gathers = [pltpu.async_copy(y_hbm.at[idx_v.at[k]], rows_v.at[k], gather_sem.at[k])
                       for k in range(TOP_K)]
            writes = []
            for k in range(TOP_K):
                gathers[k].wait()
                writes.append(pltpu.async_copy(
                    rows_v.at[k], yg_hbm.at[k, pl.ds(ch * SC_COLLECT_CHUNK, SC_COLLECT_CHUNK)],
                    write_sem.at[k]))
            for w in writes:
                w.wait()

    return run(y, dest3)


def _layer(x2, batch, seq, attn_norm_g, w_in, a_q_g, a_k_g, a_sinks, b_q_g, b_k_g, w_out,
           ffn_norm_g, w_router, b_router, w1, b1, w2, b2):
    n = x2.shape[0]
    slopes = _alibi_slopes()
    q_scale = HEAD_DIM ** -0.5
    reps = MXU_DIM // HEAD_DIM
    gains = jnp.stack([jnp.tile(a_q_g, reps) * q_scale, jnp.tile(a_k_g, reps),
                       jnp.tile(b_q_g, reps) * q_scale, jnp.tile(b_k_g, reps)]).astype(jnp.float32)

    proj = _in_proj(x2, attn_norm_g.reshape(1, -1), w_in, gains, batch, seq)
    qa, ka, va = proj[:3]
    nb = len(B_DILS)
    qbs, kbs, vbs = proj[3:3 + nb], proj[3 + nb:3 + 2 * nb], proj[3 + 2 * nb:]

    bias_a = _bias_tables(slopes[:A_Q_HEADS], A_STACK_HEADS, A_HALF_WINDOW, 1, Q_TILE + 2 * A_HALF_WINDOW)
    sink_col = jnp.repeat(a_sinks.astype(jnp.float32), Q_TILE).reshape(
        A_Q_HEADS // A_STACK_HEADS, A_STACK_HEADS * Q_TILE, 1)
    as_seqs = lambda a: a.reshape(batch, seq, a.shape[-1])
    out_a = _banded_attention(as_seqs(qa), as_seqs(ka), as_seqs(va), bias_a, half_w=A_HALF_WINDOW,
                              sink=sink_col, name="attn_a")[0].reshape(n, A_Q_W)

    outs_b, lses_b = [], []
    for bi, (window, dil) in enumerate(B_BRANCHES):
        half_w = window // (2 * dil)
        bias_b = _bias_tables(slopes[A_Q_HEADS:], 2, half_w, dil, Q_TILE + 2 * half_w)
        L = seq // dil
        to_seqs = lambda a: a.reshape(batch * dil, L, a.shape[-1])
        o, lse = _banded_attention(to_seqs(qbs[bi]), to_seqs(kbs[bi]), to_seqs(vbs[bi]), bias_b,
                                   half_w=half_w, want_lse=True, name=f"attn_b_d{dil}")
        if dil == 1:
            outs_b.append(o.reshape(n, B_W))
            lses_b.append(lse.reshape(n, LANES))
        else:
            outs_b.append(o.reshape(batch, dil, L, B_W))
            lses_b.append(lse.reshape(batch, dil, L, LANES))

    wr_t = w_router.T.astype(jnp.float32)
    wr_hi = wr_t.astype(jnp.bfloat16)
    wr_lo = (wr_t - wr_hi.astype(jnp.float32)).astype(jnp.bfloat16)
    wr = jnp.concatenate([wr_hi, wr_lo], axis=0)
    br = jnp.broadcast_to(b_router.astype(jnp.float32)[:, None], (N_EXPERTS, LANES))
    x1, hf, topi, gates, ranks, counts = _out_proj_router(
        out_a, outs_b, lses_b, x2, w_out, ffn_norm_g.reshape(1, -1), wr, br, seq)

    g = MOE_ROWS
    nk = n * TOP_K
    n_rows = nk + N_EXPERTS * g
    cnt = counts[:, 0]
    pcnt = (cnt + g - 1) // g * g
    pends = jnp.cumsum(pcnt)
    pstarts = pends - pcnt
    experts = jnp.arange(N_EXPERTS, dtype=jnp.int32)
    start_of = jnp.sum(jnp.where(topi[:, :, None] == experts, pstarts, 0), axis=-1)
    dest = (start_of + ranks).astype(jnp.int32)
    plan = _moe_plan(pends, n_rows // g)
    dest3 = dest.reshape(TOP_K, n // SC_CHUNK, SC_CHUNK).transpose(1, 0, 2)

    xb = _sc_dispatch(hf, dest3, n_rows)
    y = _moe_experts(plan, xb, w1, b1[:, None, :], w2, b2[:, None, :])
    per_split = n // COMBINE_SPLITS
    dest_c = dest.reshape(TOP_K, n // SC_COLLECT_CHUNK, SC_COLLECT_CHUNK).transpose(1, 0, 2)
    chunks_per_split = per_split // SC_COLLECT_CHUNK
    out = x1
    for s in range(COMBINE_SPLITS):
        yg = _sc_collect(y, dest_c[s * chunks_per_split:(s + 1) * chunks_per_split])
        out = _combine(out, yg, gates, s)
    return out


def kernel(x, attn_norm_g, w_in, a_q_norm_g, a_k_norm_g, a_sinks, b_q_norm_g, b_k_norm_g, w_out,
           ffn_norm_g, w_router, b_router, w1, b1, w2, b2):
    batch, seq, d = x.shape
    x2 = x.reshape(batch * seq, d)
    for i in range(attn_norm_g.shape[0]):
        x2 = _layer(x2, batch, seq, attn_norm_g[i], w_in[i], a_q_norm_g[i], a_k_norm_g[i],
                    a_sinks[i], b_q_norm_g[i], b_k_norm_g[i], w_out[i], ffn_norm_g[i],
                    w_router[i], b_router[i], w1[i], b1[i], w2[i], b2[i])
    return x2.reshape(batch, seq, d)
```

```python
import functools

import jax
import jax.numpy as jnp
import numpy as np
from jax import lax
from jax.experimental import pallas as pl
from jax.experimental.pallas import tpu as pltpu
from jax.experimental.pallas import tpu_sc as plsc

D_MODEL = 1024
HALF_D = D_MODEL // 2
HEAD_DIM = 64
LANES = 128
MXU_DIM = 256
A_Q_HEADS = 8
A_KV_HEADS = 2
B_HEADS = 8
A_HALF_WINDOW = 128
B_BRANCHES = ((128, 1), (512, 4), (2048, 16))
B_DILS = tuple(d for _, d in B_BRANCHES)
RESIDUE_STRIDE = 4
N_ALIBI_HEADS = 16
A_Q_W = A_Q_HEADS * HEAD_DIM
A_KV_W = A_KV_HEADS * HEAD_DIM
B_W = B_HEADS * HEAD_DIM
N_EXPERTS = 32
TOP_K = 4
D_FF = 1024
SWIGLU_ALPHA = 1.702
SWIGLU_LIMIT = 7.0
NORM_EPS = 1e-5
MASK_VALUE = -1e30
LOG2E = 1.4426950408889634

Q_TILE = 128
A_STACK_HEADS = 4
B_STACK_HEADS = 2
PROJ_ROWS = 512
MOE_ROWS = 512
FF_CHUNK = 512
MOE_SUB_ROWS = 512
VMEM_LIMIT = 48 * 1024 * 1024
MOE_VMEM_LIMIT = 58 * 1024 * 1024
SC_CORES = 2
SC_SUBCORES = 16
SC_WORKERS = SC_CORES * SC_SUBCORES
SC_CHUNK = 64
SC_COLLECT_CHUNK = 32
COMBINE_SPLITS = 4


def _pack_halves(v):
    lo = v[:, :HALF_D].astype(jnp.bfloat16).astype(jnp.float32)
    hi = v[:, HALF_D:].astype(jnp.bfloat16).astype(jnp.float32)
    return (pltpu.bitcast(lo, jnp.uint32) >> 16) | pltpu.bitcast(hi, jnp.uint32)


def _unpack_halves(w):
    lo = pltpu.bitcast(w << 16, jnp.float32)
    hi = pltpu.bitcast(w & jnp.uint32(0xFFFF0000), jnp.float32)
    return lo, hi


def _alibi_slopes():
    return np.exp2(-8.0 * np.arange(1, N_ALIBI_HEADS + 1, dtype=np.float32) / N_ALIBI_HEADS).astype(np.float32)


def _bias_tables(head_slopes, heads_per_group, half_w, dist_scale, tk):
    i = np.arange(Q_TILE)[:, None]
    j = np.arange(tk)[None, :]
    tabs = []
    for shift in (0, half_w, tk - Q_TILE):
        dist = np.abs(j - shift - i)
        valid = dist <= half_w
        per_head = []
        for sl in head_slopes:
            b = (-np.float64(sl) * LOG2E * (dist * dist_scale)).astype(np.float32)
            per_head.append(np.where(valid, b, np.float32(MASK_VALUE)).astype(np.float32))
        t = np.stack(per_head).reshape(-1, heads_per_group * Q_TILE, tk)
        tabs.append(t)
    return jnp.asarray(np.stack(tabs))


def _in_proj_kernel(x_ref, g_ref, w_hbm, gains_ref, qa_ref, ka_ref, va_ref, *rest):
    b_refs, (scr_ref, scr2_ref, w_ref, stage_ref, sem) = rest[:-5], rest[-5:]

    @pl.when(pl.program_id(0) == 0)
    def _():
        width = stage_ref.shape[1]
        for c0 in range(0, w_ref.shape[1], width):
            cp = pltpu.make_async_copy(w_hbm.at[:, c0:c0 + width], stage_ref, sem)
            cp.start()
            cp.wait()
            w_ref[:, c0:c0 + width] = stage_ref[...].astype(jnp.bfloat16)

    x = x_ref[...]
    xn = x * lax.rsqrt(jnp.mean(x * x, axis=-1, keepdims=True) + NORM_EPS) * g_ref[...]
    xn = xn.astype(jnp.bfloat16)
    r = lax.broadcasted_iota(jnp.int32, (MXU_DIM, MXU_DIM), 0) // HEAD_DIM
    c = lax.broadcasted_iota(jnp.int32, (MXU_DIM, MXU_DIM), 1) // HEAD_DIM
    blockdiag = jnp.where(r == c, 1.0, 0.0).astype(jnp.bfloat16)

    def project(col0, width, gain_row):
        sec = jnp.dot(xn, w_ref[:, col0:col0 + width], preferred_element_type=jnp.float32)
        if gain_row is None:
            return [sec]
        parts = []
        step = min(width, MXU_DIM)
        for j in range(width // step):
            p = sec[:, j * step:(j + 1) * step]
            ss = jnp.dot((p * p).astype(jnp.bfloat16), blockdiag[:step, :step],
                         preferred_element_type=jnp.float32)
            parts.append(p * lax.rsqrt(ss * (1.0 / HEAD_DIM) + NORM_EPS)
                         * gains_ref[gain_row:gain_row + 1, :step])
        return parts

    def store(out_ref, parts):
        w = parts[0].shape[1]
        for j, p in enumerate(parts):
            out_ref[:, j * w:(j + 1) * w] = p.astype(out_ref.dtype)

    def per_kv_head(p):
        lane = lax.broadcasted_iota(jnp.int32, p.shape, 1)
        swapped = pltpu.roll(p, HEAD_DIM, axis=1)
        low = lane < HEAD_DIM
        return [jnp.where(low, p, swapped), jnp.where(low, swapped, p)]

    store(qa_ref, project(0, A_Q_W, 0))
    store(ka_ref, per_kv_head(project(A_Q_W, A_KV_W, 1)[0]))
    store(va_ref, per_kv_head(project(A_Q_W + A_KV_W, A_KV_W, None)[0]))

    rows = x_ref.shape[0]
    col0 = A_Q_W + 2 * A_KV_W
    for t, gain_row in enumerate((2, 3, None)):
        parts = project(col0 + t * B_W, B_W, gain_row)
        sec = jnp.concatenate(parts, axis=-1) if len(parts) > 1 else parts[0]
        for j in range(B_W // LANES):
            scr_ref[j] = sec[:, j * LANES:(j + 1) * LANES]
        n_lane_chunks = B_W // LANES
        prev_dil = 1
        for bi, dil in enumerate(B_DILS):
            out_ref = b_refs[t * len(B_DILS) + bi]
            if dil == 1:
                out_ref[...] = sec.astype(out_ref.dtype)
                continue
            assert dil == prev_dil * RESIDUE_STRIDE
            last = dil == B_DILS[-1]
            for res in range(dil):
                r_prev, r_sub = res % prev_dil, res // prev_dil
                for j in range(n_lane_chunks):
                    if prev_dil == 1:
                        v = scr_ref[j, pl.ds(r_sub, rows // dil, stride=RESIDUE_STRIDE), :]
                    else:
                        v = scr2_ref[r_prev * n_lane_chunks + j,
                                     pl.ds(r_sub, rows // dil, stride=RESIDUE_STRIDE), :]
                    out_ref[0, res, :, j * LANES:(j + 1) * LANES] = v.astype(out_ref.dtype)
                    if not last:
                        scr2_ref[res * n_lane_chunks + j] = v
            prev_dil = dil


def _in_proj(x2, g, w_in, gains, batch, seq):
    n = x2.shape[0]
    rows = PROJ_ROWS
    steps = seq // rows
    a_widths = (A_Q_W, 2 * A_KV_W, 2 * A_KV_W)
    out_shape = [jax.ShapeDtypeStruct((n, w), jnp.bfloat16) for w in a_widths]
    out_specs = [pl.BlockSpec((rows, w), lambda i: (i, 0)) for w in a_widths]
    for _ in range(3):
        for dil in B_DILS:
            if dil == 1:
                out_shape.append(jax.ShapeDtypeStruct((n, B_W), jnp.bfloat16))
                out_specs.append(pl.BlockSpec((rows, B_W), lambda i: (i, 0)))
            else:
                out_shape.append(jax.ShapeDtypeStruct((batch, dil, seq // dil, B_W), jnp.bfloat16))
                out_specs.append(pl.BlockSpec((1, dil, rows // dil, B_W),
                                              lambda i: (i // steps, 0, i % steps, 0)))
    return pl.pallas_call(
        _in_proj_kernel,
        out_shape=out_shape,
        grid=(n // rows,),
        in_specs=[
            pl.BlockSpec((rows, D_MODEL), lambda i: (i, 0)),
            pl.BlockSpec((1, D_MODEL), lambda i: (0, 0)),
            pl.BlockSpec(memory_space=pl.ANY),
            pl.BlockSpec(gains.shape, lambda i: (0, 0)),
        ],
        out_specs=out_specs,
        scratch_shapes=[pltpu.VMEM((B_W // LANES, rows, LANES), jnp.float32),
                        pltpu.VMEM((RESIDUE_STRIDE * B_W // LANES, rows // RESIDUE_STRIDE, LANES),
                                   jnp.float32),
                        pltpu.VMEM(w_in.shape, jnp.bfloat16),
                        pltpu.VMEM((w_in.shape[0], w_in.shape[1] // 3), jnp.float32),
                        pltpu.SemaphoreType.DMA],
        compiler_params=pltpu.CompilerParams(
            dimension_semantics=("arbitrary",), vmem_limit_bytes=VMEM_LIMIT),
        name="in_proj",
    )(x2, g, w_in, gains)


def _attn_kernel(*refs, n_chunks, kv_chunks, heads_per_stack, tk, half_w, seq_len, rows, has_sink,
                 want_lse):
    it = iter(refs)
    q_ref, k_ref, v_ref, bias_ref = next(it), next(it), next(it), next(it)
    sink_ref = next(it) if has_sink else None
    o_ref = next(it)
    lse_ref = next(it) if want_lse else None

    n_tiles = seq_len // Q_TILE
    tiles_per_step = rows // Q_TILE
    chunks_per_group = n_chunks // kv_chunks
    assert (2 * chunks_per_group) % heads_per_stack == 0
    step = pl.program_id(1)
    lane = lax.broadcasted_iota(jnp.int32, (Q_TILE, LANES), 1)
    low_half = lane < HEAD_DIM
    ones = jnp.ones((tk, LANES), jnp.bfloat16)

    for t in range(tiles_per_step):
        tile = step * tiles_per_step + t
        q0 = tile * Q_TILE
        start = pl.multiple_of(jnp.clip(q0 - half_w, 0, seq_len - tk), HEAD_DIM)
        variant = jnp.where(tile == 0, 0, jnp.where(tile == n_tiles - 1, 2, 1))
        r0 = t * Q_TILE
        lse_tile = jnp.zeros((Q_TILE, LANES), jnp.float32)
        head_o, head_lse = {}, {}
        for g in range(2 * n_chunks // heads_per_stack):
            heads = range(g * heads_per_stack, (g + 1) * heads_per_stack)
            kv = (heads[0] // 2) // chunks_per_group
            kc = k_ref[0, pl.ds(start, tk), kv * LANES:(kv + 1) * LANES]
            vc = v_ref[0, pl.ds(start, tk), kv * LANES:(kv + 1) * LANES]
            v_aug = jnp.concatenate([vc, ones], axis=1)
            q_parts = []
            for h in heads:
                c = h // 2
                q2 = q_ref[0, r0:r0 + Q_TILE, c * LANES:(c + 1) * LANES]
                keep = low_half if h % 2 == 0 else ~low_half
                q_parts.append(jnp.where(keep, q2, jnp.zeros_like(q2)))
            qs = q_parts[0] if len(q_parts) == 1 else jnp.concatenate(q_parts, axis=0)
            s = lax.dot_general(qs, kc, (((1,), (1,)), ((), ())),
                                preferred_element_type=jnp.float32)
            s = s + bias_ref[variant, g]
            m = jnp.max(s, axis=-1, keepdims=True)
            if has_sink:
                m = jnp.maximum(m, sink_ref[g])
            p = jnp.exp2(s - m)
            ov = jnp.dot(p.astype(jnp.bfloat16), v_aug, preferred_element_type=jnp.float32)
            o, l = ov[:, :LANES], ov[:, LANES:]
            if has_sink:
                l = l + jnp.exp2(sink_ref[g] - m)
            o = o * (1.0 / l)
            if want_lse:
                lse = m + jnp.log(l) * LOG2E
            for idx, h in enumerate(heads):
                head_o[h] = o[idx * Q_TILE:(idx + 1) * Q_TILE]
                if want_lse:
                    head_lse[h] = lse[idx * Q_TILE:(idx + 1) * Q_TILE]
                if h % 2 == 1:
                    c = h // 2
                    o2 = jnp.where(low_half, head_o.pop(h - 1), head_o.pop(h))
                    o_ref[0, r0:r0 + Q_TILE, c * LANES:(c + 1) * LANES] = o2.astype(o_ref.dtype)
                    if want_lse:
                        lse_tile = jnp.where(lane == h - 1, head_lse.pop(h - 1),
                                             jnp.where(lane == h, head_lse.pop(h), lse_tile))
        if want_lse:
            lse_ref[0, r0:r0 + Q_TILE, :] = lse_tile


def _banded_attention(q, k, v, bias, *, half_w, sink=None, want_lse=False, name):
    n_seq, L, qw = q.shape
    kw = k.shape[2]
    tk = Q_TILE + 2 * half_w
    rows = min(512, L)

    args = [q, k, v, bias]
    in_specs = [
        pl.BlockSpec((1, rows, qw), lambda s, i: (s, i, 0)),
        pl.BlockSpec((1, L, kw), lambda s, i: (s, 0, 0)),
        pl.BlockSpec((1, L, kw), lambda s, i: (s, 0, 0)),
        pl.BlockSpec(bias.shape, lambda s, i: (0, 0, 0, 0)),
    ]
    if sink is not None:
        args.append(sink)
        in_specs.append(pl.BlockSpec(sink.shape, lambda s, i: (0, 0, 0)))
    out_shape = [jax.ShapeDtypeStruct((n_seq, L, qw), jnp.bfloat16)]
    out_specs = [pl.BlockSpec((1, rows, qw), lambda s, i: (s, i, 0))]
    if want_lse:
        out_shape.append(jax.ShapeDtypeStruct((n_seq, L, LANES), jnp.float32))
        out_specs.append(pl.BlockSpec((1, rows, LANES), lambda s, i: (s, i, 0)))

    kern = functools.partial(
        _attn_kernel, n_chunks=qw // LANES, kv_chunks=kw // LANES,
        heads_per_stack=bias.shape[2] // Q_TILE, tk=tk, half_w=half_w, seq_len=L,
        rows=rows, has_sink=sink is not None, want_lse=want_lse)
    return pl.pallas_call(
        kern,
        out_shape=out_shape,
        grid=(n_seq, L // rows),
        in_specs=in_specs,
        out_specs=out_specs,
        compiler_params=pltpu.CompilerParams(
            dimension_semantics=("arbitrary", "arbitrary"), vmem_limit_bytes=VMEM_LIMIT),
        name=name,
    )(*args)


def _out_proj_router_kernel(*refs):
    nb = len(B_DILS)
    oa_ref = refs[0]
    o_refs = refs[1:1 + nb]
    lse_refs = refs[1 + nb:1 + 2 * nb]
    (x_ref, wo_ref, g_ref, wr_ref, br_ref,
     x1_ref, hf_ref, topi_ref, gate_ref, rank_ref, cnt_ref,
     tri_ref, carry_ref, so_ref, sl_ref, wo_bf_ref) = refs[1 + 2 * nb:]
    i = pl.program_id(0)
    rows = x_ref.shape[0]

    @pl.when(i == 0)
    def _():
        a = lax.broadcasted_iota(jnp.int32, (rows, rows), 0)
        b = lax.broadcasted_iota(jnp.int32, (rows, rows), 1)
        tri_ref[...] = jnp.where(a <= b, 1.0, 0.0).astype(jnp.bfloat16)
        carry_ref[...] = jnp.zeros_like(carry_ref)
        wo_bf_ref[...] = wo_ref[...].astype(jnp.bfloat16)

    outs, lses = [], []
    for bi, dil in enumerate(B_DILS):
        if dil == 1:
            outs.append(o_refs[bi][...].astype(jnp.float32))
            lses.append(lse_refs[bi][...])
        else:
            for res in range(dil):
                for j in range(B_W // LANES):
                    so_ref[bi, j, pl.ds(res, rows // dil, stride=dil), :] = (
                        o_refs[bi][0, res, :, j * LANES:(j + 1) * LANES].astype(jnp.float32))
                sl_ref[bi, pl.ds(res, rows // dil, stride=dil), :] = lse_refs[bi][0, res]
            outs.append(jnp.concatenate([so_ref[bi, j] for j in range(B_W // LANES)], axis=-1))
            lses.append(sl_ref[bi])

    mx = functools.reduce(jnp.maximum, lses)
    es = [jnp.exp2(l - mx) for l in lses]
    inv = 1.0 / functools.reduce(lambda a, b: a + b, es)
    eh = lax.broadcasted_iota(jnp.int32, (LANES, B_W), 0)
    ej = lax.broadcasted_iota(jnp.int32, (LANES, B_W), 1) // HEAD_DIM
    expand = jnp.where(eh == ej, 1.0, 0.0).astype(jnp.bfloat16)
    ob = jnp.zeros((rows, B_W), jnp.float32)
    for e, o in zip(es, outs):
        w = e * inv
        wide = jnp.dot(w.astype(jnp.bfloat16), expand, preferred_element_type=jnp.float32)
        ob = ob + wide * o

    attn = jnp.concatenate([oa_ref[...], ob.astype(jnp.bfloat16)], axis=-1)
    x1 = x_ref[...] + jnp.dot(attn, wo_bf_ref[...], preferred_element_type=jnp.float32)
    x1_ref[...] = x1
    hf = x1 * lax.rsqrt(jnp.mean(x1 * x1, axis=-1, keepdims=True) + NORM_EPS) * g_ref[...]
    hf_hi = hf.astype(jnp.bfloat16)
    hf_ref[...] = _pack_halves(hf)
    hf_lo = (hf - hf_hi.astype(jnp.float32)).astype(jnp.bfloat16)

    nt = (((1,), (1,)), ((), ()))
    lg_hi = lax.dot_general(wr_ref[...], hf_hi, nt, preferred_element_type=jnp.float32)
    lg_lo = lax.dot_general(wr_ref[0:N_EXPERTS, :], hf_lo, nt, preferred_element_type=jnp.float32)
    logits = lg_hi[0:N_EXPERTS] + lg_hi[N_EXPERTS:] + lg_lo + br_ref[:, 0:1]

    eidx = lax.broadcasted_iota(jnp.int32, (N_EXPERTS, rows), 0)
    work = logits
    vals, sels = [], []
    for k in range(TOP_K):
        mk = jnp.max(work, axis=0, keepdims=True)
        ik = jnp.min(jnp.where(work == mk, eidx, N_EXPERTS), axis=0, keepdims=True)
        sel = eidx == ik
        work = jnp.where(sel, -jnp.inf, work)
        vals.append(mk)
        sels.append(sel)
        topi_ref[k:k + 1, :] = ik
    exps = [jnp.exp(vk - vals[0]) for vk in vals]
    denom = exps[0] + exps[1] + exps[2] + exps[3]
    ginv = 1.0 / denom
    for k in range(TOP_K):
        gate_ref[k:k + 1, :] = exps[k] * ginv

    onehot = jnp.zeros((N_EXPERTS, rows), jnp.float32)
    for sel in sels:
        onehot = onehot + jnp.where(sel, 1.0, 0.0)
    incl = jnp.dot(onehot.astype(jnp.bfloat16), tri_ref[...], preferred_element_type=jnp.float32)
    before = incl - onehot + carry_ref[:, 0:1]
    for k in range(TOP_K):
        rk = jnp.sum(jnp.where(sels[k], before, 0.0), axis=0, keepdims=True)
        rank_ref[k:k + 1, :] = rk.astype(jnp.int32)
    carry = carry_ref[...] + jnp.sum(onehot, axis=1, keepdims=True)
    carry_ref[...] = carry
    cnt_ref[...] = carry.astype(jnp.int32)


def _out_proj_router(oa, outs_b, lses_b, x2, wo_bf, g, wr, br, seq):
    n = x2.shape[0]
    rows = PROJ_ROWS
    steps = seq // rows
    row_spec = lambda w: pl.BlockSpec((rows, w), lambda i: (i, 0))
    full = lambda a: pl.BlockSpec(a.shape, lambda i: (0,) * a.ndim)
    col_spec = pl.BlockSpec((TOP_K, rows), lambda i: (0, i))

    def branch_spec(dil, w):
        if dil == 1:
            return row_spec(w)
        return pl.BlockSpec((1, dil, rows // dil, w), lambda i: (i // steps, 0, i % steps, 0))

    in_specs = ([row_spec(A_Q_W)]
                + [branch_spec(d, B_W) for d in B_DILS]
                + [branch_spec(d, LANES) for d in B_DILS]
                + [row_spec(D_MODEL), full(wo_bf), full(g), full(wr), full(br)])
    return pl.pallas_call(
        _out_proj_router_kernel,
        out_shape=[
            jax.ShapeDtypeStruct((n, D_MODEL), jnp.float32),
            jax.ShapeDtypeStruct((n, HALF_D), jnp.uint32),
            jax.ShapeDtypeStruct((TOP_K, n), jnp.int32),
            jax.ShapeDtypeStruct((TOP_K, n), jnp.float32),
            jax.ShapeDtypeStruct((TOP_K, n), jnp.int32),
            jax.ShapeDtypeStruct((N_EXPERTS, LANES), jnp.int32),
        ],
        grid=(n // rows,),
        in_specs=in_specs,
        out_specs=[row_spec(D_MODEL), row_spec(HALF_D), col_spec, col_spec, col_spec,
                   pl.BlockSpec((N_EXPERTS, LANES), lambda i: (0, 0))],
        scratch_shapes=[pltpu.VMEM((rows, rows), jnp.bfloat16),
                        pltpu.VMEM((N_EXPERTS, LANES), jnp.float32),
                        pltpu.VMEM((len(B_DILS), B_W // LANES, rows, LANES), jnp.float32),
                        pltpu.VMEM((len(B_DILS), rows, LANES), jnp.float32),
                        pltpu.VMEM(wo_bf.shape, jnp.bfloat16)],
        compiler_params=pltpu.CompilerParams(
            dimension_semantics=("arbitrary",), vmem_limit_bytes=VMEM_LIMIT),
        name="out_proj_router",
    )(oa, *outs_b, *lses_b, x2, wo_bf, g, wr, br)


def _mxu_dot(a_bf, w_f32):
    return lax.dot_general(a_bf, w_f32, (((1,), (0,)), ((), ())), preferred_element_type=jnp.float32)


def _moe_kernel(blk_exp_ref, first_ref, slot_ref, next_exp_ref, n_used_ref,
                x_ref, w1_hbm, b1_ref, w2_hbm, b2_ref, y_ref, w1_buf, w2_buf, sem):
    i = pl.program_id(0)

    def weight_copies(expert, slot):
        return (pltpu.make_async_copy(w1_hbm.at[expert], w1_buf.at[slot], sem.at[slot, 0]),
                pltpu.make_async_copy(w2_hbm.at[expert], w2_buf.at[slot], sem.at[slot, 1]))

    @pl.when(i < n_used_ref[0])
    def _():
        slot = slot_ref[i]

        @pl.when(i == 0)
        def _():
            for cp in weight_copies(blk_exp_ref[0], slot):
                cp.start()

        @pl.when(first_ref[i] == 1)
        def _():
            for cp in weight_copies(blk_exp_ref[i], slot):
                cp.wait()

            @pl.when(next_exp_ref[i] >= 0)
            def _():
                for cp in weight_copies(next_exp_ref[i], 1 - slot):
                    cp.start()

        for r0 in range(0, MOE_ROWS, MOE_SUB_ROWS):
            rs = slice(r0, r0 + MOE_SUB_ROWS)
            x = jnp.concatenate(_unpack_halves(x_ref[rs, :]), axis=-1).astype(jnp.bfloat16)
            acc = jnp.zeros((MOE_SUB_ROWS, D_MODEL), jnp.float32)
            for c in range(D_FF // FF_CHUNK):
                lo = c * FF_CHUNK
                glu = _mxu_dot(x, w1_buf[slot, :, lo:lo + FF_CHUNK]) + b1_ref[0, :, lo:lo + FF_CHUNK]
                lin = (_mxu_dot(x, w1_buf[slot, :, D_FF + lo:D_FF + lo + FF_CHUNK])
                       + b1_ref[0, :, D_FF + lo:D_FF + lo + FF_CHUNK])
                glu = jnp.minimum(glu, SWIGLU_LIMIT)
                lin = jnp.clip(lin, -SWIGLU_LIMIT, SWIGLU_LIMIT)
                act = glu * (1.0 / (1.0 + jnp.exp(-SWIGLU_ALPHA * glu))) * (lin + 1.0)
                acc = acc + _mxu_dot(act.astype(jnp.bfloat16), w2_buf[slot, lo:lo + FF_CHUNK, :])
            y_ref[rs, :] = _pack_halves(acc + b2_ref[0])


def _moe_plan(pends, n_blk):
    g = MOE_ROWS
    blk_row0 = jnp.arange(n_blk, dtype=jnp.int32) * g
    blk_exp = jnp.minimum(jnp.sum(pends[None, :] <= blk_row0[:, None], axis=-1),
                          N_EXPERTS - 1).astype(jnp.int32)
    n_used = (pends[-1] // g).astype(jnp.int32)
    used = blk_row0 < pends[-1]
    prev_exp = jnp.concatenate([jnp.full((1,), -1, jnp.int32), blk_exp[:-1]])
    first = (used & (blk_exp != prev_exp)).astype(jnp.int32)
    slot = ((jnp.cumsum(first) - 1) % 2).astype(jnp.int32)
    pstarts = jnp.concatenate([jnp.zeros((1,), pends.dtype), pends[:-1]])
    nonempty = pends > pstarts
    experts = jnp.arange(N_EXPERTS, dtype=jnp.int32)
    later = nonempty[None, :] & (experts[None, :] > experts[:, None])
    next_nonempty = jnp.min(jnp.where(later, experts[None, :], N_EXPERTS), axis=-1)
    next_nonempty = jnp.where(next_nonempty == N_EXPERTS, -1, next_nonempty).astype(jnp.int32)
    next_exp = jnp.sum(jnp.where(blk_exp[:, None] == experts[None, :], next_nonempty[None, :], 0),
                       axis=-1).astype(jnp.int32)
    return blk_exp, first, slot, next_exp, n_used.reshape(1)


def _moe_experts(plan, xb, w1, b1, w2, b2):
    n_rows = xb.shape[0]
    n_blk = n_rows // MOE_ROWS

    def blk(i, *p):
        return jnp.minimum(i, p[-1][0] - 1)

    grid_spec = pltpu.PrefetchScalarGridSpec(
        num_scalar_prefetch=len(plan),
        grid=(n_blk,),
        in_specs=[
            pl.BlockSpec((MOE_ROWS, HALF_D), lambda i, *p: (blk(i, *p), 0)),
            pl.BlockSpec(memory_space=pl.ANY),
            pl.BlockSpec((1, 1, 2 * D_FF), lambda i, *p: (p[0][blk(i, *p)], 0, 0)),
            pl.BlockSpec(memory_space=pl.ANY),
            pl.BlockSpec((1, 1, D_MODEL), lambda i, *p: (p[0][blk(i, *p)], 0, 0)),
        ],
        out_specs=pl.BlockSpec((MOE_ROWS, HALF_D), lambda i, *p: (blk(i, *p), 0)),
        scratch_shapes=[pltpu.VMEM((2, D_MODEL, 2 * D_FF), jnp.float32),
                        pltpu.VMEM((2, D_FF, D_MODEL), jnp.float32),
                        pltpu.SemaphoreType.DMA((2, 2))],
    )
    return pl.pallas_call(
        _moe_kernel,
        out_shape=jax.ShapeDtypeStruct((n_rows, HALF_D), jnp.uint32),
        grid_spec=grid_spec,
        compiler_params=pltpu.CompilerParams(
            dimension_semantics=("arbitrary",), vmem_limit_bytes=MOE_VMEM_LIMIT),
        name="moe_experts",
    )(*plan, xb, w1, b1, w2, b2)


def _combine_kernel(x1_ref, yg_ref, gate_ref, o_ref):
    acc_lo = x1_ref[:, :HALF_D]
    acc_hi = x1_ref[:, HALF_D:]
    rows = x1_ref.shape[0]
    gates = jnp.concatenate([gate_ref[...], jnp.zeros((8 - TOP_K, rows), jnp.float32)], axis=0).T
    for k in range(TOP_K):
        lo, hi = _unpack_halves(yg_ref[k])
        gk = gates[:, k:k + 1]
        acc_lo = acc_lo + lo * gk
        acc_hi = acc_hi + hi * gk
    o_ref[:, :HALF_D] = acc_lo
    o_ref[:, HALF_D:] = acc_hi


def _combine(acc, yg, gates_nk, split):
    n = acc.shape[0]
    rows = PROJ_ROWS
    steps = yg.shape[1] // rows
    first = split * steps
    return pl.pallas_call(
        _combine_kernel,
        out_shape=jax.ShapeDtypeStruct((n, D_MODEL), jnp.float32),
        grid=(steps,),
        in_specs=[pl.BlockSpec((rows, D_MODEL), lambda i: (first + i, 0)),
                  pl.BlockSpec((TOP_K, rows, HALF_D), lambda i: (0, i, 0)),
                  pl.BlockSpec((TOP_K, rows), lambda i: (0, first + i))],
        out_specs=pl.BlockSpec((rows, D_MODEL), lambda i: (first + i, 0)),
        input_output_aliases={0: 0},
        compiler_params=pltpu.CompilerParams(
            dimension_semantics=("arbitrary",), vmem_limit_bytes=VMEM_LIMIT),
        name="moe_combine",
    )(acc, yg, gates_nk)


def _sc_worker_id():
    return lax.axis_index("s") * SC_CORES + lax.axis_index("c")


def _sc_dispatch(hf, dest3, n_rows):
    n = hf.shape[0]
    chunks_per_worker = n // SC_CHUNK // SC_WORKERS
    mesh = plsc.VectorSubcoreMesh(core_axis_name="c", subcore_axis_name="s")

    @functools.partial(
        pl.kernel, mesh=mesh,
        out_type=jax.ShapeDtypeStruct((n_rows, HALF_D), hf.dtype),
        scratch_types=[pltpu.VMEM((TOP_K, SC_CHUNK), jnp.int32),
                       pltpu.VMEM((SC_CHUNK, HALF_D), hf.dtype)],
        name="sc_dispatch")
    def run(hf_hbm, dest_hbm, xb_hbm, idx_v, rows_v):
        first = _sc_worker_id() * chunks_per_worker

        @pl.loop(0, chunks_per_worker)
        def _(j):
            ch = first + j
            pltpu.sync_copy(dest_hbm.at[ch], idx_v)
            pltpu.sync_copy(hf_hbm.at[pl.ds(ch * SC_CHUNK, SC_CHUNK)], rows_v)
            for k in range(TOP_K):
                pltpu.sync_copy(rows_v, xb_hbm.at[idx_v.at[k]])

    return run(hf, dest3)


def _sc_collect(y, dest3):
    n = dest3.shape[0] * SC_COLLECT_CHUNK
    chunks_per_worker = n // SC_COLLECT_CHUNK // SC_WORKERS
    mesh = plsc.VectorSubcoreMesh(core_axis_name="c", subcore_axis_name="s")

    @functools.partial(
        pl.kernel, mesh=mesh,
        out_type=jax.ShapeDtypeStruct((TOP_K, n, HALF_D), y.dtype),
        scratch_types=[pltpu.VMEM((TOP_K, SC_COLLECT_CHUNK), jnp.int32),
                       pltpu.VMEM((TOP_K, SC_COLLECT_CHUNK, HALF_D), y.dtype),
                       pltpu.SemaphoreType.DMA((TOP_K,)),
                       pltpu.SemaphoreType.DMA((TOP_K,))],
        name="sc_collect")
    def run(y_hbm, dest_hbm, yg_hbm, idx_v, rows_v, gather_sem, write_sem):
        first = _sc_worker_id() * chunks_per_worker

        @pl.loop(0, chunks_per_worker)
        def _(j):
            ch = first + j
            pltpu.sync_copy(dest_hbm.at[ch], idx_v)
            gathers = [pltpu.async_copy(y_hbm.at[idx_v.at[k]], rows_v.at[k], gather_sem.at[k])
                       for k in range(TOP_K)]
            writes = []
            for k in range(TOP_K):
                gathers[k].wait()
                writes.append(pltpu.async_copy(
                    rows_v.at[k], yg_hbm.at[k, pl.ds(ch * SC_COLLECT_CHUNK, SC_COLLECT_CHUNK)],
                    write_sem.at[k]))
            for w in writes:
                w.wait()

    return run(y, dest3)


def _layer(x2, batch, seq, attn_norm_g, w_in, a_q_g, a_k_g, a_sinks, b_q_g, b_k_g, w_out,
           ffn_norm_g, w_router, b_router, w1, b1, w2, b2):
    n = x2.shape[0]
    slopes = _alibi_slopes()
    q_scale = HEAD_DIM ** -0.5 * LOG2E
    reps = MXU_DIM // HEAD_DIM
    gains = jnp.stack([jnp.tile(a_q_g, reps) * q_scale, jnp.tile(a_k_g, reps),
                       jnp.tile(b_q_g, reps) * q_scale, jnp.tile(b_k_g, reps)]).astype(jnp.float32)

    proj = _in_proj(x2, attn_norm_g.reshape(1, -1), w_in, gains, batch, seq)
    qa, ka, va = proj[:3]
    nb = len(B_DILS)
    qbs, kbs, vbs = proj[3:3 + nb], proj[3 + nb:3 + 2 * nb], proj[3 + 2 * nb:]

    bias_a = _bias_tables(slopes[:A_Q_HEADS], A_STACK_HEADS, A_HALF_WINDOW, 1, Q_TILE + 2 * A_HALF_WINDOW)
    sink_col = jnp.repeat(a_sinks.astype(jnp.float32) * LOG2E, Q_TILE).reshape(
        A_Q_HEADS // A_STACK_HEADS, A_STACK_HEADS * Q_TILE, 1)
    as_seqs = lambda a: a.reshape(batch, seq, a.shape[-1])
    out_a = _banded_attention(as_seqs(qa), as_seqs(ka), as_seqs(va), bias_a, half_w=A_HALF_WINDOW,
                              sink=sink_col, name="attn_a")[0].reshape(n, A_Q_W)

    outs_b, lses_b = [], []
    for bi, (window, dil) in enumerate(B_BRANCHES):
        half_w = window // (2 * dil)
        bias_b = _bias_tables(slopes[A_Q_HEADS:], B_STACK_HEADS, half_w, dil, Q_TILE + 2 * half_w)
        L = seq // dil
        to_seqs = lambda a: a.reshape(batch * dil, L, a.shape[-1])
        o, lse = _banded_attention(to_seqs(qbs[bi]), to_seqs(kbs[bi]), to_seqs(vbs[bi]), bias_b,
                                   half_w=half_w, want_lse=True, name=f"attn_b_d{dil}")
        if dil == 1:
            outs_b.append(o.reshape(n, B_W))
            lses_b.append(lse.reshape(n, LANES))
        else:
            outs_b.append(o.reshape(batch, dil, L, B_W))
            lses_b.append(lse.reshape(batch, dil, L, LANES))

    wr_t = w_router.T.astype(jnp.float32)
    wr_hi = wr_t.astype(jnp.bfloat16)
    wr_lo = (wr_t - wr_hi.astype(jnp.float32)).astype(jnp.bfloat16)
    wr = jnp.concatenate([wr_hi, wr_lo], axis=0)
    br = jnp.broadcast_to(b_router.astype(jnp.float32)[:, None], (N_EXPERTS, LANES))
    x1, hf, topi, gates, ranks, counts = _out_proj_router(
        out_a, outs_b, lses_b, x2, w_out, ffn_norm_g.reshape(1, -1), wr, br, seq)

    g = MOE_ROWS
    nk = n * TOP_K
    n_rows = nk + N_EXPERTS * g
    cnt = counts[:, 0]
    pcnt = (cnt + g - 1) // g * g
    pends = jnp.cumsum(pcnt)
    pstarts = pends - pcnt
    experts = jnp.arange(N_EXPERTS, dtype=jnp.int32)
    start_of = jnp.sum(jnp.where(topi[:, :, None] == experts, pstarts, 0), axis=-1)
    dest = (start_of + ranks).astype(jnp.int32)
    plan = _moe_plan(pends, n_rows // g)
    dest3 = dest.reshape(TOP_K, n // SC_CHUNK, SC_CHUNK).transpose(1, 0, 2)

    xb = _sc_dispatch(hf, dest3, n_rows)
    y = _moe_experts(plan, xb, w1, b1[:, None, :], w2, b2[:, None, :])
    per_split = n // COMBINE_SPLITS
    dest_c = dest.reshape(TOP_K, n // SC_COLLECT_CHUNK, SC_COLLECT_CHUNK).transpose(1, 0, 2)
    chunks_per_split = per_split // SC_COLLECT_CHUNK
    out = x1
    for s in range(COMBINE_SPLITS):
        yg = _sc_collect(y, dest_c[s * chunks_per_split:(s + 1) * chunks_per_split])
        out = _combine(out, yg, gates, s)
    return out


def kernel(x, attn_norm_g, w_in, a_q_norm_g, a_k_norm_g, a_sinks, b_q_norm_g, b_k_norm_g, w_out,
           ffn_norm_g, w_router, b_router, w1, b1, w2, b2):
    batch, seq, d = x.shape
    x2 = x.reshape(batch * seq, d)
    for i in range(attn_norm_g.shape[0]):
        x2 = _layer(x2, batch, seq, attn_norm_g[i], w_in[i], a_q_norm_g[i], a_k_norm_g[i],
                    a_sinks[i], b_q_norm_g[i], b_k_norm_g[i], w_out[i], ffn_norm_g[i],
                    w_router[i], b_router[i], w1[i], b1[i], w2[i], b2[i])
    return x2.reshape(batch, seq, d)
```

```python
import functools

import jax
import jax.numpy as jnp
import numpy as np
from jax import lax
from jax.experimental import pallas as pl
from jax.experimental.pallas import tpu as pltpu
from jax.experimental.pallas import tpu_sc as plsc

D_MODEL = 1024
HALF_D = D_MODEL // 2
HEAD_DIM = 64
LANES = 128
MXU_DIM = 256
A_Q_HEADS = 8
A_KV_HEADS = 2
B_HEADS = 8
A_HALF_WINDOW = 128
B_BRANCHES = ((128, 1), (512, 4), (2048, 16))
B_DILS = tuple(d for _, d in B_BRANCHES)
RESIDUE_STRIDE = 4
N_ALIBI_HEADS = 16
A_Q_W = A_Q_HEADS * HEAD_DIM
A_KV_W = A_KV_HEADS * HEAD_DIM
B_W = B_HEADS * HEAD_DIM
N_EXPERTS = 32
TOP_K = 4
D_FF = 1024
SWIGLU_ALPHA = 1.702
SWIGLU_LIMIT = 7.0
NORM_EPS = 1e-5
MASK_VALUE = -1e30
LOG2E = 1.4426950408889634

Q_TILE = 128
ATTN_STEP_ROWS = 512
A_STACK_HEADS = 4
B_STACK_HEADS = 2
PROJ_ROWS = 512
MOE_ROWS = 512
FF_CHUNK = 512
MOE_SUB_ROWS = 512
VMEM_LIMIT = 48 * 1024 * 1024
MOE_VMEM_LIMIT = 58 * 1024 * 1024
SC_CORES = 2
SC_SUBCORES = 16
SC_WORKERS = SC_CORES * SC_SUBCORES
SC_CHUNK = 64
SC_COLLECT_CHUNK = 32
COMBINE_SPLITS = 4


def _pack_halves(v):
    lo = v[:, :HALF_D].astype(jnp.bfloat16).astype(jnp.float32)
    hi = v[:, HALF_D:].astype(jnp.bfloat16).astype(jnp.float32)
    return (pltpu.bitcast(lo, jnp.uint32) >> 16) | pltpu.bitcast(hi, jnp.uint32)


def _unpack_halves(w):
    lo = pltpu.bitcast(w << 16, jnp.float32)
    hi = pltpu.bitcast(w & jnp.uint32(0xFFFF0000), jnp.float32)
    return lo, hi


def _alibi_slopes():
    return np.exp2(-8.0 * np.arange(1, N_ALIBI_HEADS + 1, dtype=np.float32) / N_ALIBI_HEADS).astype(np.float32)


def _bias_tables(head_slopes, heads_per_group, half_w, dist_scale, tk):
    i = np.arange(Q_TILE)[:, None]
    j = np.arange(tk)[None, :]
    tabs = []
    for shift in (0, half_w, tk - Q_TILE):
        dist = np.abs(j - shift - i)
        valid = dist <= half_w
        per_head = []
        for sl in head_slopes:
            b = (-np.float64(sl) * LOG2E * (dist * dist_scale)).astype(np.float32)
            per_head.append(np.where(valid, b, np.float32(MASK_VALUE)).astype(np.float32))
        t = np.stack(per_head).reshape(-1, heads_per_group * Q_TILE, tk)
        tabs.append(t)
    return jnp.asarray(np.stack(tabs))


def _in_proj_kernel(x_ref, g_ref, w_hbm, gains_ref, qa_ref, ka_ref, va_ref, *rest):
    b_refs, (scr_ref, scr2_ref, w_ref, stage_ref, sem) = rest[:-5], rest[-5:]

    @pl.when(pl.program_id(0) == 0)
    def _():
        width = stage_ref.shape[1]
        for c0 in range(0, w_ref.shape[1], width):
            cp = pltpu.make_async_copy(w_hbm.at[:, c0:c0 + width], stage_ref, sem)
            cp.start()
            cp.wait()
            w_ref[:, c0:c0 + width] = stage_ref[...].astype(jnp.bfloat16)

    x = x_ref[...]
    xn = x * lax.rsqrt(jnp.mean(x * x, axis=-1, keepdims=True) + NORM_EPS) * g_ref[...]
    xn = xn.astype(jnp.bfloat16)
    r = lax.broadcasted_iota(jnp.int32, (MXU_DIM, MXU_DIM), 0) // HEAD_DIM
    c = lax.broadcasted_iota(jnp.int32, (MXU_DIM, MXU_DIM), 1) // HEAD_DIM
    blockdiag = jnp.where(r == c, 1.0, 0.0).astype(jnp.bfloat16)

    def head_rms(sec, gain_row):
        width = sec.shape[1]
        parts = []
        step = min(width, MXU_DIM)
        for j in range(width // step):
            p = sec[:, j * step:(j + 1) * step]
            ss = jnp.dot((p * p).astype(jnp.bfloat16), blockdiag[:step, :step],
                         preferred_element_type=jnp.float32)
            parts.append(p * lax.rsqrt(ss * (1.0 / HEAD_DIM) + NORM_EPS)
                         * gains_ref[gain_row:gain_row + 1, :step])
        return parts

    def project(col0, width, gain_row):
        sec = jnp.dot(xn, w_ref[:, col0:col0 + width], preferred_element_type=jnp.float32)
        return [sec] if gain_row is None else head_rms(sec, gain_row)

    def store(out_ref, parts):
        w = parts[0].shape[1]
        for j, p in enumerate(parts):
            out_ref[:, j * w:(j + 1) * w] = p.astype(out_ref.dtype)

    def per_kv_head(p):
        lane = lax.broadcasted_iota(jnp.int32, p.shape, 1)
        swapped = pltpu.roll(p, HEAD_DIM, axis=1)
        low = lane < HEAD_DIM
        return [jnp.where(low, p, swapped), jnp.where(low, swapped, p)]

    store(qa_ref, project(0, A_Q_W, 0))
    kva = project(A_Q_W, 2 * A_KV_W, None)[0]
    store(ka_ref, per_kv_head(head_rms(kva[:, :A_KV_W], 1)[0]))
    store(va_ref, per_kv_head(kva[:, A_KV_W:]))

    rows = x_ref.shape[0]
    col0 = A_Q_W + 2 * A_KV_W
    for t, gain_row in enumerate((2, 3, None)):
        parts = project(col0 + t * B_W, B_W, gain_row)
        sec = jnp.concatenate(parts, axis=-1) if len(parts) > 1 else parts[0]
        for j in range(B_W // LANES):
            scr_ref[j] = sec[:, j * LANES:(j + 1) * LANES]
        n_lane_chunks = B_W // LANES
        prev_dil = 1
        for bi, dil in enumerate(B_DILS):
            out_ref = b_refs[t * len(B_DILS) + bi]
            if dil == 1:
                out_ref[...] = sec.astype(out_ref.dtype)
                continue
            assert dil == prev_dil * RESIDUE_STRIDE
            last = dil == B_DILS[-1]
            for res in range(dil):
                r_prev, r_sub = res % prev_dil, res // prev_dil
                for j in range(n_lane_chunks):
                    if prev_dil == 1:
                        v = scr_ref[j, pl.ds(r_sub, rows // dil, stride=RESIDUE_STRIDE), :]
                    else:
                        v = scr2_ref[r_prev * n_lane_chunks + j,
                                     pl.ds(r_sub, rows // dil, stride=RESIDUE_STRIDE), :]
                    out_ref[0, res, :, j * LANES:(j + 1) * LANES] = v.astype(out_ref.dtype)
                    if not last:
                        scr2_ref[res * n_lane_chunks + j] = v
            prev_dil = dil


def _in_proj(x2, g, w_in, gains, batch, seq):
    n = x2.shape[0]
    rows = PROJ_ROWS
    steps = seq // rows
    a_widths = (A_Q_W, 2 * A_KV_W, 2 * A_KV_W)
    out_shape = [jax.ShapeDtypeStruct((n, w), jnp.bfloat16) for w in a_widths]
    out_specs = [pl.BlockSpec((rows, w), lambda i: (i, 0)) for w in a_widths]
    for _ in range(3):
        for dil in B_DILS:
            if dil == 1:
                out_shape.append(jax.ShapeDtypeStruct((n, B_W), jnp.bfloat16))
                out_specs.append(pl.BlockSpec((rows, B_W), lambda i: (i, 0)))
            else:
                out_shape.append(jax.ShapeDtypeStruct((batch, dil, seq // dil, B_W), jnp.bfloat16))
                out_specs.append(pl.BlockSpec((1, dil, rows // dil, B_W),
                                              lambda i: (i // steps, 0, i % steps, 0)))
    return pl.pallas_call(
        _in_proj_kernel,
        out_shape=out_shape,
        grid=(n // rows,),
        in_specs=[
            pl.BlockSpec((rows, D_MODEL), lambda i: (i, 0)),
            pl.BlockSpec((1, D_MODEL), lambda i: (0, 0)),
            pl.BlockSpec(memory_space=pl.ANY),
            pl.BlockSpec(gains.shape, lambda i: (0, 0)),
        ],
        out_specs=out_specs,
        scratch_shapes=[pltpu.VMEM((B_W // LANES, rows, LANES), jnp.float32),
                        pltpu.VMEM((RESIDUE_STRIDE * B_W // LANES, rows // RESIDUE_STRIDE, LANES),
                                   jnp.float32),
                        pltpu.VMEM(w_in.shape, jnp.bfloat16),
                        pltpu.VMEM((w_in.shape[0], w_in.shape[1] // 3), jnp.float32),
                        pltpu.SemaphoreType.DMA],
        compiler_params=pltpu.CompilerParams(
            dimension_semantics=("arbitrary",), vmem_limit_bytes=VMEM_LIMIT),
        name="in_proj",
    )(x2, g, w_in, gains)


def _attn_kernel(*refs, n_chunks, kv_chunks, heads_per_stack, tk, half_w, seq_len, rows, has_sink,
                 want_lse):
    it = iter(refs)
    q_ref, k_ref, v_ref, bias_ref = next(it), next(it), next(it), next(it)
    sink_ref = next(it) if has_sink else None
    o_ref = next(it)
    lse_ref = next(it) if want_lse else None

    n_tiles = seq_len // Q_TILE
    tiles_per_step = rows // Q_TILE
    chunks_per_group = n_chunks // kv_chunks
    assert (2 * chunks_per_group) % heads_per_stack == 0
    step = pl.program_id(1)
    lane = lax.broadcasted_iota(jnp.int32, (Q_TILE, LANES), 1)
    low_half = lane < HEAD_DIM
    ones = jnp.ones((tk, LANES), jnp.bfloat16)

    for sq, t in [(a, b) for a in range(q_ref.shape[0]) for b in range(tiles_per_step)]:
        tile = step * tiles_per_step + t
        q0 = tile * Q_TILE
        kv_rows = k_ref.shape[1]
        kv_row0 = jnp.clip(step * rows - half_w, 0, seq_len - kv_rows)
        start = pl.multiple_of(jnp.clip(q0 - half_w, 0, seq_len - tk) - kv_row0, HEAD_DIM)
        variant = jnp.where(tile == 0, 0, jnp.where(tile == n_tiles - 1, 2, 1))
        r0 = t * Q_TILE
        lse_tile = jnp.zeros((Q_TILE, LANES), jnp.float32)
        head_o, head_lse = {}, {}
        for g in range(2 * n_chunks // heads_per_stack):
            heads = range(g * heads_per_stack, (g + 1) * heads_per_stack)
            kv = (heads[0] // 2) // chunks_per_group
            kc = k_ref[sq, pl.ds(start, tk), kv * LANES:(kv + 1) * LANES]
            vc = v_ref[sq, pl.ds(start, tk), kv * LANES:(kv + 1) * LANES]
            v_aug = jnp.concatenate([vc, ones], axis=1)
            q_parts = []
            for h in heads:
                c = h // 2
                q2 = q_ref[sq, r0:r0 + Q_TILE, c * LANES:(c + 1) * LANES]
                keep = low_half if h % 2 == 0 else ~low_half
                q_parts.append(jnp.where(keep, q2, jnp.zeros_like(q2)))
            qs = q_parts[0] if len(q_parts) == 1 else jnp.concatenate(q_parts, axis=0)
            s = lax.dot_general(qs, kc, (((1,), (1,)), ((), ())),
                                preferred_element_type=jnp.float32)
            s = s + bias_ref[variant, g]
            m = jnp.max(s, axis=-1, keepdims=True)
            if has_sink:
                m = jnp.maximum(m, sink_ref[g])
            p = jnp.exp2(s - m)
            ov = jnp.dot(p.astype(jnp.bfloat16), v_aug, preferred_element_type=jnp.float32)
            o, l = ov[:, :LANES], ov[:, LANES:]
            if has_sink:
                l = l + jnp.exp2(sink_ref[g] - m)
            o = o * (1.0 / l)
            if want_lse:
                lse = m + jnp.log(l) * LOG2E
            for idx, h in enumerate(heads):
                head_o[h] = o[idx * Q_TILE:(idx + 1) * Q_TILE]
                if want_lse:
                    head_lse[h] = lse[idx * Q_TILE:(idx + 1) * Q_TILE]
                if h % 2 == 1:
                    c = h // 2
                    o2 = jnp.where(low_half, head_o.pop(h - 1), head_o.pop(h))
                    o_ref[sq, r0:r0 + Q_TILE, c * LANES:(c + 1) * LANES] = o2.astype(o_ref.dtype)
                    if want_lse:
                        lse_tile = jnp.where(lane == h - 1, head_lse.pop(h - 1),
                                             jnp.where(lane == h, head_lse.pop(h), lse_tile))
        if want_lse:
            lse_ref[sq, r0:r0 + Q_TILE, :] = lse_tile


def _banded_attention(q, k, v, bias, *, half_w, sink=None, want_lse=False, name):
    n_seq, L, qw = q.shape
    kw = k.shape[2]
    tk = Q_TILE + 2 * half_w
    rows = min(ATTN_STEP_ROWS, L)
    seqs = ATTN_STEP_ROWS // rows
    kv_rows = min(L, rows + 2 * half_w)

    def kv_index(s, i):
        row0 = pl.multiple_of(jnp.clip(i * rows - half_w, 0, L - kv_rows), HEAD_DIM)
        return s * seqs, row0, 0

    args = [q, k, v, bias]
    in_specs = [
        pl.BlockSpec((seqs, rows, qw), lambda s, i: (s, i, 0)),
        pl.BlockSpec((pl.Element(seqs), pl.Element(kv_rows), pl.Element(kw)), kv_index),
        pl.BlockSpec((pl.Element(seqs), pl.Element(kv_rows), pl.Element(kw)), kv_index),
        pl.BlockSpec(bias.shape, lambda s, i: (0, 0, 0, 0)),
    ]
    if sink is not None:
        args.append(sink)
        in_specs.append(pl.BlockSpec(sink.shape, lambda s, i: (0, 0, 0)))
    out_shape = [jax.ShapeDtypeStruct((n_seq, L, qw), jnp.bfloat16)]
    out_specs = [pl.BlockSpec((seqs, rows, qw), lambda s, i: (s, i, 0))]
    if want_lse:
        out_shape.append(jax.ShapeDtypeStruct((n_seq, L, LANES), jnp.float32))
        out_specs.append(pl.BlockSpec((seqs, rows, LANES), lambda s, i: (s, i, 0)))

    kern = functools.partial(
        _attn_kernel, n_chunks=qw // LANES, kv_chunks=kw // LANES,
        heads_per_stack=bias.shape[2] // Q_TILE, tk=tk, half_w=half_w, seq_len=L,
        rows=rows, has_sink=sink is not None, want_lse=want_lse)
    return pl.pallas_call(
        kern,
        out_shape=out_shape,
        grid=(n_seq // seqs, L // rows),
        in_specs=in_specs,
        out_specs=out_specs,
        compiler_params=pltpu.CompilerParams(
            dimension_semantics=("arbitrary", "arbitrary"), vmem_limit_bytes=VMEM_LIMIT),
        name=name,
    )(*args)


def _out_proj_router_kernel(*refs):
    nb = len(B_DILS)
    oa_ref = refs[0]
    o_refs = refs[1:1 + nb]
    lse_refs = refs[1 + nb:1 + 2 * nb]
    (x_ref, wo_ref, g_ref, wr_ref, br_ref,
     x1_ref, hf_ref, topi_ref, gate_ref, rank_ref, cnt_ref,
     tri_ref, carry_ref, so_ref, sl_ref, wo_bf_ref) = refs[1 + 2 * nb:]
    i = pl.program_id(0)
    rows = x_ref.shape[0]

    @pl.when(i == 0)
    def _():
        a = lax.broadcasted_iota(jnp.int32, (rows, rows), 0)
        b = lax.broadcasted_iota(jnp.int32, (rows, rows), 1)
        tri_ref[...] = jnp.where(a <= b, 1.0, 0.0).astype(jnp.bfloat16)
        carry_ref[...] = jnp.zeros_like(carry_ref)
        wo_bf_ref[...] = wo_ref[...].astype(jnp.bfloat16)

    outs, lses = [], []
    for bi, dil in enumerate(B_DILS):
        if dil == 1:
            outs.append(o_refs[bi][...].astype(jnp.float32))
            lses.append(lse_refs[bi][...])
        else:
            for res in range(dil):
                for j in range(B_W // LANES):
                    so_ref[bi, j, pl.ds(res, rows // dil, stride=dil), :] = (
                        o_refs[bi][0, res, :, j * LANES:(j + 1) * LANES].astype(jnp.float32))
                sl_ref[bi, pl.ds(res, rows // dil, stride=dil), :] = lse_refs[bi][0, res]
            outs.append(jnp.concatenate([so_ref[bi, j] for j in range(B_W // LANES)], axis=-1))
            lses.append(sl_ref[bi])

    mx = functools.reduce(jnp.maximum, lses)
    es = [jnp.exp2(l - mx) for l in lses]
    inv = 1.0 / functools.reduce(lambda a, b: a + b, es)
    eh = lax.broadcasted_iota(jnp.int32, (LANES, B_W), 0)
    ej = lax.broadcasted_iota(jnp.int32, (LANES, B_W), 1) // HEAD_DIM
    expand = jnp.where(eh == ej, 1.0, 0.0).astype(jnp.bfloat16)
    ob = jnp.zeros((rows, B_W), jnp.float32)
    for e, o in zip(es, outs):
        w = e * inv
        wide = jnp.dot(w.astype(jnp.bfloat16), expand, preferred_element_type=jnp.float32)
        ob = ob + wide * o

    attn = jnp.concatenate([oa_ref[...], ob.astype(jnp.bfloat16)], axis=-1)
    x1 = x_ref[...] + jnp.dot(attn, wo_bf_ref[...], preferred_element_type=jnp.float32)
    x1_ref[...] = x1
    hf = x1 * lax.rsqrt(jnp.mean(x1 * x1, axis=-1, keepdims=True) + NORM_EPS) * g_ref[...]
    hf_hi = hf.astype(jnp.bfloat16)
    hf_ref[...] = _pack_halves(hf)
    hf_lo = (hf - hf_hi.astype(jnp.float32)).astype(jnp.bfloat16)

    nt = (((1,), (1,)), ((), ()))
    lg_hi = lax.dot_general(wr_ref[...], hf_hi, nt, preferred_element_type=jnp.float32)
    lg_lo = lax.dot_general(wr_ref[0:N_EXPERTS, :], hf_lo, nt, preferred_element_type=jnp.float32)
    logits = lg_hi[0:N_EXPERTS] + lg_hi[N_EXPERTS:] + lg_lo + br_ref[:, 0:1]

    eidx = lax.broadcasted_iota(jnp.int32, (N_EXPERTS, rows), 0)
    work = logits
    vals, sels = [], []
    for k in range(TOP_K):
        mk = jnp.max(work, axis=0, keepdims=True)
        ik = jnp.min(jnp.where(work == mk, eidx, N_EXPERTS), axis=0, keepdims=True)
        sel = eidx == ik
        work = jnp.where(sel, -jnp.inf, work)
        vals.append(mk)
        sels.append(sel)
        topi_ref[k:k + 1, :] = ik
    exps = [jnp.exp(vk - vals[0]) for vk in vals]
    denom = exps[0] + exps[1] + exps[2] + exps[3]
    ginv = 1.0 / denom
    for k in range(TOP_K):
        gate_ref[k:k + 1, :] = exps[k] * ginv

    onehot = jnp.zeros((N_EXPERTS, rows), jnp.float32)
    for sel in sels:
        onehot = onehot + jnp.where(sel, 1.0, 0.0)
    incl = jnp.dot(onehot.astype(jnp.bfloat16), tri_ref[...], preferred_element_type=jnp.float32)
    before = incl - onehot + carry_ref[:, 0:1]
    for k in range(TOP_K):
        rk = jnp.sum(jnp.where(sels[k], before, 0.0), axis=0, keepdims=True)
        rank_ref[k:k + 1, :] = rk.astype(jnp.int32)
    carry = carry_ref[...] + jnp.sum(onehot, axis=1, keepdims=True)
    carry_ref[...] = carry
    cnt_ref[...] = carry.astype(jnp.int32)


def _out_proj_router(oa, outs_b, lses_b, x2, wo_bf, g, wr, br, seq):
    n = x2.shape[0]
    rows = PROJ_ROWS
    steps = seq // rows
    row_spec = lambda w: pl.BlockSpec((rows, w), lambda i: (i, 0))
    full = lambda a: pl.BlockSpec(a.shape, lambda i: (0,) * a.ndim)
    col_spec = pl.BlockSpec((TOP_K, rows), lambda i: (0, i))

    def branch_spec(dil, w):
        if dil == 1:
            return row_spec(w)
        return pl.BlockSpec((1, dil, rows // dil, w), lambda i: (i // steps, 0, i % steps, 0))

    in_specs = ([row_spec(A_Q_W)]
                + [branch_spec(d, B_W) for d in B_DILS]
                + [branch_spec(d, LANES) for d in B_DILS]
                + [row_spec(D_MODEL), full(wo_bf), full(g), full(wr), full(br)])
    return pl.pallas_call(
        _out_proj_router_kernel,
        out_shape=[
            jax.ShapeDtypeStruct((n, D_MODEL), jnp.float32),
            jax.ShapeDtypeStruct((n, HALF_D), jnp.uint32),
            jax.ShapeDtypeStruct((TOP_K, n), jnp.int32),
            jax.ShapeDtypeStruct((TOP_K, n), jnp.float32),
            jax.ShapeDtypeStruct((TOP_K, n), jnp.int32),
            jax.ShapeDtypeStruct((N_EXPERTS, LANES), jnp.int32),
        ],
        grid=(n // rows,),
        in_specs=in_specs,
        out_specs=[row_spec(D_MODEL), row_spec(HALF_D), col_spec, col_spec, col_spec,
                   pl.BlockSpec((N_EXPERTS, LANES), lambda i: (0, 0))],
        scratch_shapes=[pltpu.VMEM((rows, rows), jnp.bfloat16),
                        pltpu.VMEM((N_EXPERTS, LANES), jnp.float32),
                        pltpu.VMEM((len(B_DILS), B_W // LANES, rows, LANES), jnp.float32),
                        pltpu.VMEM((len(B_DILS), rows, LANES), jnp.float32),
                        pltpu.VMEM(wo_bf.shape, jnp.bfloat16)],
        compiler_params=pltpu.CompilerParams(
            dimension_semantics=("arbitrary",), vmem_limit_bytes=VMEM_LIMIT),
        name="out_proj_router",
    )(oa, *outs_b, *lses_b, x2, wo_bf, g, wr, br)


def _mxu_dot(a_bf, w_f32):
    return lax.dot_general(a_bf, w_f32, (((1,), (0,)), ((), ())), preferred_element_type=jnp.float32)


def _moe_kernel(blk_exp_ref, first_ref, slot_ref, next_exp_ref, n_used_ref,
                x_ref, w1_hbm, b1_ref, w2_hbm, b2_ref, y_ref, w1_buf, w2_buf, sem):
    i = pl.program_id(0)

    def weight_copies(expert, slot):
        return (pltpu.make_async_copy(w1_hbm.at[expert], w1_buf.at[slot], sem.at[slot, 0]),
                pltpu.make_async_copy(w2_hbm.at[expert], w2_buf.at[slot], sem.at[slot, 1]))

    @pl.when(i < n_used_ref[0])
    def _():
        slot = slot_ref[i]

        @pl.when(i == 0)
        def _():
            for cp in weight_copies(blk_exp_ref[0], slot):
                cp.start()

        @pl.when(first_ref[i] == 1)
        def _():
            for cp in weight_copies(blk_exp_ref[i], slot):
                cp.wait()

            @pl.when(next_exp_ref[i] >= 0)
            def _():
                for cp in weight_copies(next_exp_ref[i], 1 - slot):
                    cp.start()

        for r0 in range(0, MOE_ROWS, MOE_SUB_ROWS):
            rs = slice(r0, r0 + MOE_SUB_ROWS)
            x = jnp.concatenate(_unpack_halves(x_ref[rs, :]), axis=-1).astype(jnp.bfloat16)
            acc = jnp.zeros((MOE_SUB_ROWS, D_MODEL), jnp.float32)
            for c in range(D_FF // FF_CHUNK):
                lo = c * FF_CHUNK
                glu = _mxu_dot(x, w1_buf[slot, :, lo:lo + FF_CHUNK]) + b1_ref[0, :, lo:lo + FF_CHUNK]
                lin = (_mxu_dot(x, w1_buf[slot, :, D_FF + lo:D_FF + lo + FF_CHUNK])
                       + b1_ref[0, :, D_FF + lo:D_FF + lo + FF_CHUNK])
                glu = jnp.minimum(glu, SWIGLU_LIMIT)
                lin = jnp.clip(lin, -SWIGLU_LIMIT, SWIGLU_LIMIT)
                act = glu * (1.0 / (1.0 + jnp.exp(-SWIGLU_ALPHA * glu))) * (lin + 1.0)
                acc = acc + _mxu_dot(act.astype(jnp.bfloat16), w2_buf[slot, lo:lo + FF_CHUNK, :])
            y_ref[rs, :] = _pack_halves(acc + b2_ref[0])


def _moe_plan(pends, n_blk):
    g = MOE_ROWS
    blk_row0 = jnp.arange(n_blk, dtype=jnp.int32) * g
    blk_exp = jnp.minimum(jnp.sum(pends[None, :] <= blk_row0[:, None], axis=-1),
                          N_EXPERTS - 1).astype(jnp.int32)
    n_used = (pends[-1] // g).astype(jnp.int32)
    used = blk_row0 < pends[-1]
    prev_exp = jnp.concatenate([jnp.full((1,), -1, jnp.int32), blk_exp[:-1]])
    first = (used & (blk_exp != prev_exp)).astype(jnp.int32)
    slot = ((jnp.cumsum(first) - 1) % 2).astype(jnp.int32)
    pstarts = jnp.concatenate([jnp.zeros((1,), pends.dtype), pends[:-1]])
    nonempty = pends > pstarts
    experts = jnp.arange(N_EXPERTS, dtype=jnp.int32)
    later = nonempty[None, :] & (experts[None, :] > experts[:, None])
    next_nonempty = jnp.min(jnp.where(later, experts[None, :], N_EXPERTS), axis=-1)
    next_nonempty = jnp.where(next_nonempty == N_EXPERTS, -1, next_nonempty).astype(jnp.int32)
    next_exp = jnp.sum(jnp.where(blk_exp[:, None] == experts[None, :], next_nonempty[None, :], 0),
                       axis=-1).astype(jnp.int32)
    return blk_exp, first, slot, next_exp, n_used.reshape(1)


def _moe_experts(plan, xb, w1, b1, w2, b2):
    n_rows = xb.shape[0]
    n_blk = n_rows // MOE_ROWS

    def blk(i, *p):
        return jnp.minimum(i, p[-1][0] - 1)

    grid_spec = pltpu.PrefetchScalarGridSpec(
        num_scalar_prefetch=len(plan),
        grid=(n_blk,),
        in_specs=[
            pl.BlockSpec((MOE_ROWS, HALF_D), lambda i, *p: (blk(i, *p), 0)),
            pl.BlockSpec(memory_space=pl.ANY),
            pl.BlockSpec((1, 1, 2 * D_FF), lambda i, *p: (p[0][blk(i, *p)], 0, 0)),
            pl.BlockSpec(memory_space=pl.ANY),
            pl.BlockSpec((1, 1, D_MODEL), lambda i, *p: (p[0][blk(i, *p)], 0, 0)),
        ],
        out_specs=pl.BlockSpec((MOE_ROWS, HALF_D), lambda i, *p: (blk(i, *p), 0)),
        scratch_shapes=[pltpu.VMEM((2, D_MODEL, 2 * D_FF), jnp.float32),
                        pltpu.VMEM((2, D_FF, D_MODEL), jnp.float32),
                        pltpu.SemaphoreType.DMA((2, 2))],
    )
    return pl.pallas_call(
        _moe_kernel,
        out_shape=jax.ShapeDtypeStruct((n_rows, HALF_D), jnp.uint32),
        grid_spec=grid_spec,
        compiler_params=pltpu.CompilerParams(
            dimension_semantics=("arbitrary",), vmem_limit_bytes=MOE_VMEM_LIMIT),
        name="moe_experts",
    )(*plan, xb, w1, b1, w2, b2)


def _combine_kernel(x1_ref, yg_ref, gate_ref, o_ref):
    acc_lo = x1_ref[:, :HALF_D]
    acc_hi = x1_ref[:, HALF_D:]
    rows = x1_ref.shape[0]
    gates = jnp.concatenate([gate_ref[...], jnp.zeros((8 - TOP_K, rows), jnp.float32)], axis=0).T
    for k in range(TOP_K):
        lo, hi = _unpack_halves(yg_ref[k])
        gk = gates[:, k:k + 1]
        acc_lo = acc_lo + lo * gk
        acc_hi = acc_hi + hi * gk
    o_ref[:, :HALF_D] = acc_lo
    o_ref[:, HALF_D:] = acc_hi


def _combine(acc, yg, gates_nk, split):
    n = acc.shape[0]
    rows = PROJ_ROWS
    steps = yg.shape[1] // rows
    first = split * steps
    return pl.pallas_call(
        _combine_kernel,
        out_shape=jax.ShapeDtypeStruct((n, D_MODEL), jnp.float32),
        grid=(steps,),
        in_specs=[pl.BlockSpec((rows, D_MODEL), lambda i: (first + i, 0)),
                  pl.BlockSpec((TOP_K, rows, HALF_D), lambda i: (0, i, 0)),
                  pl.BlockSpec((TOP_K, rows), lambda i: (0, first + i))],
        out_specs=pl.BlockSpec((rows, D_MODEL), lambda i: (first + i, 0)),
        input_output_aliases={0: 0},
        compiler_params=pltpu.CompilerParams(
            dimension_semantics=("arbitrary",), vmem_limit_bytes=VMEM_LIMIT),
        name="moe_combine",
    )(acc, yg, gates_nk)


def _sc_worker_id():
    return lax.axis_index("s") * SC_CORES + lax.axis_index("c")


def _sc_dispatch(hf, dest3, n_rows):
    n = hf.shape[0]
    chunks_per_worker = n // SC_CHUNK // SC_WORKERS
    mesh = plsc.VectorSubcoreMesh(core_axis_name="c", subcore_axis_name="s")

    @functools.partial(
        pl.kernel, mesh=mesh,
        out_type=jax.ShapeDtypeStruct((n_rows, HALF_D), hf.dtype),
        scratch_types=[pltpu.VMEM((2, TOP_K, SC_CHUNK), jnp.int32),
                       pltpu.VMEM((2, SC_CHUNK, HALF_D), hf.dtype),
                       pltpu.SemaphoreType.DMA((2,)),
                       pltpu.SemaphoreType.DMA((2,))],
        name="sc_dispatch")
    def run(hf_hbm, dest_hbm, xb_hbm, idx_v, rows_v, load_sem, scatter_sem):
        first = _sc_worker_id() * chunks_per_worker

        def load(j):
            slot = j % 2
            pltpu.sync_copy(dest_hbm.at[first + j], idx_v.at[slot])
            return pltpu.async_copy(hf_hbm.at[pl.ds((first + j) * SC_CHUNK, SC_CHUNK)],
                                    rows_v.at[slot], load_sem.at[slot])

        loads = {0: load(0)}
        scatters = {}
        for j in range(chunks_per_worker):
            slot = j % 2
            loads.pop(j).wait()
            scatters[j] = [pltpu.async_copy(rows_v.at[slot], xb_hbm.at[idx_v.at[slot, k]],
                                            scatter_sem.at[slot]) for k in range(TOP_K)]
            if j >= 1:
                for cp in scatters.pop(j - 1):
                    cp.wait()
            if j + 1 < chunks_per_worker:
                loads[j + 1] = load(j + 1)
        for cp in scatters.pop(chunks_per_worker - 1):
            cp.wait()

    return run(hf, dest3)


def _sc_collect(y, dest3):
    n = dest3.shape[0] * SC_COLLECT_CHUNK
    chunks_per_worker = n // SC_COLLECT_CHUNK // SC_WORKERS
    mesh = plsc.VectorSubcoreMesh(core_axis_name="c", subcore_axis_name="s")

    @functools.partial(
        pl.kernel, mesh=mesh,
        out_type=jax.ShapeDtypeStruct((TOP_K, n, HALF_D), y.dtype),
        scratch_types=[pltpu.VMEM((TOP_K, SC_COLLECT_CHUNK), jnp.int32),
                       pltpu.VMEM((TOP_K, SC_COLLECT_CHUNK, HALF_D), y.dtype),
                       pltpu.SemaphoreType.DMA((TOP_K,)),
                       pltpu.SemaphoreType.DMA((TOP_K,))],
        name="sc_collect")
    def run(y_hbm, dest_hbm, yg_hbm, idx_v, rows_v, gather_sem, write_sem):
        first = _sc_worker_id() * chunks_per_worker

        @pl.loop(0, chunks_per_worker)
        def _(j):
            ch = first + j
            pltpu.sync_copy(dest_hbm.at[ch], idx_v)
            gathers = [pltpu.async_copy(y_hbm.at[idx_v.at[k]], rows_v.at[k], gather_sem.at[k])
                       for k in range(TOP_K)]
            writes = []
            for k in range(TOP_K):
                gathers[k].wait()
                writes.append(pltpu.async_copy(
                    rows_v.at[k], yg_hbm.at[k, pl.ds(ch * SC_COLLECT_CHUNK, SC_COLLECT_CHUNK)],
                    write_sem.at[k]))
            for w in writes:
                w.wait()

    return run(y, dest3)


def _layer(x2, batch, seq, attn_norm_g, w_in, a_q_g, a_k_g, a_sinks, b_q_g, b_k_g, w_out,
           ffn_norm_g, w_router, b_router, w1, b1, w2, b2):
    n = x2.shape[0]
    slopes = _alibi_slopes()
    q_scale = HEAD_DIM ** -0.5 * LOG2E
    reps = MXU_DIM // HEAD_DIM
    gains = jnp.stack([jnp.tile(a_q_g, reps) * q_scale, jnp.tile(a_k_g, reps),
                       jnp.tile(b_q_g, reps) * q_scale, jnp.tile(b_k_g, reps)]).astype(jnp.float32)

    proj = _in_proj(x2, attn_norm_g.reshape(1, -1), w_in, gains, batch, seq)
    qa, ka, va = proj[:3]
    nb = len(B_DILS)
    qbs, kbs, vbs = proj[3:3 + nb], proj[3 + nb:3 + 2 * nb], proj[3 + 2 * nb:]

    bias_a = _bias_tables(slopes[:A_Q_HEADS], A_STACK_HEADS, A_HALF_WINDOW, 1, Q_TILE + 2 * A_HALF_WINDOW)
    sink_col = jnp.repeat(a_sinks.astype(jnp.float32) * LOG2E, Q_TILE).reshape(
        A_Q_HEADS // A_STACK_HEADS, A_STACK_HEADS * Q_TILE, 1)
    as_seqs = lambda a: a.reshape(batch, seq, a.shape[-1])
    out_a = _banded_attention(as_seqs(qa), as_seqs(ka), as_seqs(va), bias_a, half_w=A_HALF_WINDOW,
                              sink=sink_col, name="attn_a")[0].reshape(n, A_Q_W)

    outs_b, lses_b = [], []
    for bi, (window, dil) in enumerate(B_BRANCHES):
        half_w = window // (2 * dil)
        bias_b = _bias_tables(slopes[A_Q_HEADS:], B_STACK_HEADS, half_w, dil, Q_TILE + 2 * half_w)
        L = seq // dil
        to_seqs = lambda a: a.reshape(batch * dil, L, a.shape[-1])
        o, lse = _banded_attention(to_seqs(qbs[bi]), to_seqs(kbs[bi]), to_seqs(vbs[bi]), bias_b,
                                   half_w=half_w, want_lse=True, name=f"attn_b_d{dil}")
        if dil == 1:
            outs_b.append(o.reshape(n, B_W))
            lses_b.append(lse.reshape(n, LANES))
        else:
            outs_b.append(o.reshape(batch, dil, L, B_W))
            lses_b.append(lse.reshape(batch, dil, L, LANES))

    wr_t = w_router.T.astype(jnp.float32)
    wr_hi = wr_t.astype(jnp.bfloat16)
    wr_lo = (wr_t - wr_hi.astype(jnp.float32)).astype(jnp.bfloat16)
    wr = jnp.concatenate([wr_hi, wr_lo], axis=0)
    br = jnp.broadcast_to(b_router.astype(jnp.float32)[:, None], (N_EXPERTS, LANES))
    x1, hf, topi, gates, ranks, counts = _out_proj_router(
        out_a, outs_b, lses_b, x2, w_out, ffn_norm_g.reshape(1, -1), wr, br, seq)

    g = MOE_ROWS
    nk = n * TOP_K
    n_rows = nk + N_EXPERTS * g
    cnt = counts[:, 0]
    pcnt = (cnt + g - 1) // g * g
    pends = jnp.cumsum(pcnt)
    pstarts = pends - pcnt
    experts = jnp.arange(N_EXPERTS, dtype=jnp.int32)
    start_of = jnp.sum(jnp.where(topi[:, :, None] == experts, pstarts, 0), axis=-1)
    dest = (start_of + ranks).astype(jnp.int32)
    plan = _moe_plan(pends, n_rows // g)
    dest3 = dest.reshape(TOP_K, n // SC_CHUNK, SC_CHUNK).transpose(1, 0, 2)

    xb = _sc_dispatch(hf, dest3, n_rows)
    y = _moe_experts(plan, xb, w1, b1[:, None, :], w2, b2[:, None, :])
    per_split = n // COMBINE_SPLITS
    dest_c = dest.reshape(TOP_K, n // SC_COLLECT_CHUNK, SC_COLLECT_CHUNK).transpose(1, 0, 2)
    chunks_per_split = per_split // SC_COLLECT_CHUNK
    out = x1
    for s in range(COMBINE_SPLITS):
        yg = _sc_collect(y, dest_c[s * chunks_per_split:(s + 1) * chunks_per_split])
        out = _combine(out, yg, gates, s)
    return out


def kernel(x, attn_norm_g, w_in, a_q_norm_g, a_k_norm_g, a_sinks, b_q_norm_g, b_k_norm_g, w_out,
           ffn_norm_g, w_router, b_router, w1, b1, w2, b2):
    batch, seq, d = x.shape
    x2 = x.reshape(batch * seq, d)
    for i in range(attn_norm_g.shape[0]):
        x2 = _layer(x2, batch, seq, attn_norm_g[i], w_in[i], a_q_norm_g[i], a_k_norm_g[i],
                    a_sinks[i], b_q_norm_g[i], b_k_norm_g[i], w_out[i], ffn_norm_g[i],
                    w_router[i], b_router[i], w1[i], b1[i], w2[i], b2[i])
    return x2.reshape(batch, seq, d)
```

```python
import functools

import jax
import jax.numpy as jnp
import numpy as np
from jax import lax
from jax.experimental import pallas as pl
from jax.experimental.pallas import tpu as pltpu
from jax.experimental.pallas import tpu_sc as plsc

D_MODEL = 1024
HALF_D = D_MODEL // 2
HEAD_DIM = 64
LANES = 128
MXU_DIM = 256
A_Q_HEADS = 8
A_KV_HEADS = 2
B_HEADS = 8
A_HALF_WINDOW = 128
B_BRANCHES = ((128, 1), (512, 4), (2048, 16))
B_DILS = tuple(d for _, d in B_BRANCHES)
RESIDUE_STRIDE = 4
N_ALIBI_HEADS = 16
A_Q_W = A_Q_HEADS * HEAD_DIM
A_KV_W = A_KV_HEADS * HEAD_DIM
B_W = B_HEADS * HEAD_DIM
N_EXPERTS = 32
TOP_K = 4
D_FF = 1024
SWIGLU_ALPHA = 1.702
SWIGLU_LIMIT = 7.0
NORM_EPS = 1e-5
MASK_VALUE = -1e30
LOG2E = 1.4426950408889634

Q_TILE = 128
ATTN_STEP_ROWS = 512
A_STACK_HEADS = 4
B_STACK_HEADS = 2
PROJ_ROWS = 512
MOE_ROWS = 512
FF_CHUNK = 512
MOE_STEP_BLOCKS = 2
MOE_WEIGHT_BUFS = 3
VMEM_LIMIT = 48 * 1024 * 1024
MOE_VMEM_LIMIT = 58 * 1024 * 1024
SC_CORES = 2
SC_SUBCORES = 16
SC_WORKERS = SC_CORES * SC_SUBCORES
SC_CHUNK = 64
SC_COLLECT_CHUNK = 32
COMBINE_SPLITS = 4


def _pack_halves(v):
    lo = v[:, :HALF_D].astype(jnp.bfloat16).astype(jnp.float32)
    hi = v[:, HALF_D:].astype(jnp.bfloat16).astype(jnp.float32)
    return (pltpu.bitcast(lo, jnp.uint32) >> 16) | pltpu.bitcast(hi, jnp.uint32)


def _unpack_halves(w):
    lo = pltpu.bitcast(w << 16, jnp.float32)
    hi = pltpu.bitcast(w & jnp.uint32(0xFFFF0000), jnp.float32)
    return lo, hi


def _alibi_slopes():
    return np.exp2(-8.0 * np.arange(1, N_ALIBI_HEADS + 1, dtype=np.float32) / N_ALIBI_HEADS).astype(np.float32)


def _bias_tables(head_slopes, heads_per_group, half_w, dist_scale, tk):
    i = np.arange(Q_TILE)[:, None]
    j = np.arange(tk)[None, :]
    tabs = []
    for shift in (0, half_w, tk - Q_TILE):
        dist = np.abs(j - shift - i)
        valid = dist <= half_w
        per_head = []
        for sl in head_slopes:
            b = (-np.float64(sl) * LOG2E * (dist * dist_scale)).astype(np.float32)
            per_head.append(np.where(valid, b, np.float32(MASK_VALUE)).astype(np.float32))
        t = np.stack(per_head).reshape(-1, heads_per_group * Q_TILE, tk)
        tabs.append(t)
    return jnp.asarray(np.stack(tabs))


def _in_proj_kernel(x_ref, g_ref, w_hbm, gains_ref, qa_ref, ka_ref, va_ref, *rest):
    b_refs, (scr_ref, scr2_ref, w_ref, stage_ref, sem) = rest[:-5], rest[-5:]

    @pl.when(pl.program_id(0) == 0)
    def _():
        width = stage_ref.shape[1]
        for c0 in range(0, w_ref.shape[1], width):
            cp = pltpu.make_async_copy(w_hbm.at[:, c0:c0 + width], stage_ref, sem)
            cp.start()
            cp.wait()
            w_ref[:, c0:c0 + width] = stage_ref[...].astype(jnp.bfloat16)

    x = x_ref[...]
    xn = x * lax.rsqrt(jnp.mean(x * x, axis=-1, keepdims=True) + NORM_EPS) * g_ref[...]
    xn = xn.astype(jnp.bfloat16)
    r = lax.broadcasted_iota(jnp.int32, (MXU_DIM, MXU_DIM), 0) // HEAD_DIM
    c = lax.broadcasted_iota(jnp.int32, (MXU_DIM, MXU_DIM), 1) // HEAD_DIM
    blockdiag = jnp.where(r == c, 1.0, 0.0).astype(jnp.bfloat16)

    def head_rms(sec, gain_row):
        width = sec.shape[1]
        parts = []
        step = min(width, MXU_DIM)
        for j in range(width // step):
            p = sec[:, j * step:(j + 1) * step]
            ss = jnp.dot((p * p).astype(jnp.bfloat16), blockdiag[:step, :step],
                         preferred_element_type=jnp.float32)
            parts.append(p * lax.rsqrt(ss * (1.0 / HEAD_DIM) + NORM_EPS)
                         * gains_ref[gain_row:gain_row + 1, :step])
        return parts

    def project(col0, width, gain_row):
        sec = jnp.dot(xn, w_ref[:, col0:col0 + width], preferred_element_type=jnp.float32)
        return [sec] if gain_row is None else head_rms(sec, gain_row)

    def store(out_ref, parts):
        w = parts[0].shape[1]
        for j, p in enumerate(parts):
            out_ref[:, j * w:(j + 1) * w] = p.astype(out_ref.dtype)

    def per_kv_head(p):
        lane = lax.broadcasted_iota(jnp.int32, p.shape, 1)
        swapped = pltpu.roll(p, HEAD_DIM, axis=1)
        low = lane < HEAD_DIM
        return [jnp.where(low, p, swapped), jnp.where(low, swapped, p)]

    store(qa_ref, project(0, A_Q_W, 0))
    kva = project(A_Q_W, 2 * A_KV_W, None)[0]
    store(ka_ref, per_kv_head(head_rms(kva[:, :A_KV_W], 1)[0]))
    store(va_ref, per_kv_head(kva[:, A_KV_W:]))

    rows = x_ref.shape[0]
    col0 = A_Q_W + 2 * A_KV_W
    for t, gain_row in enumerate((2, 3, None)):
        parts = project(col0 + t * B_W, B_W, gain_row)
        sec = jnp.concatenate(parts, axis=-1) if len(parts) > 1 else parts[0]
        for j in range(B_W // LANES):
            scr_ref[j] = sec[:, j * LANES:(j + 1) * LANES]
        n_lane_chunks = B_W // LANES
        prev_dil = 1
        for bi, dil in enumerate(B_DILS):
            out_ref = b_refs[t * len(B_DILS) + bi]
            if dil == 1:
                out_ref[...] = sec.astype(out_ref.dtype)
                continue
            assert dil == prev_dil * RESIDUE_STRIDE
            last = dil == B_DILS[-1]
            for res in range(dil):
                r_prev, r_sub = res % prev_dil, res // prev_dil
                for j in range(n_lane_chunks):
                    if prev_dil == 1:
                        v = scr_ref[j, pl.ds(r_sub, rows // dil, stride=RESIDUE_STRIDE), :]
                    else:
                        v = scr2_ref[r_prev * n_lane_chunks + j,
                                     pl.ds(r_sub, rows // dil, stride=RESIDUE_STRIDE), :]
                    out_ref[0, res, :, j * LANES:(j + 1) * LANES] = v.astype(out_ref.dtype)
                    if not last:
                        scr2_ref[res * n_lane_chunks + j] = v
            prev_dil = dil


def _in_proj(x2, g, w_in, gains, batch, seq):
    n = x2.shape[0]
    rows = PROJ_ROWS
    steps = seq // rows
    a_widths = (A_Q_W, 2 * A_KV_W, 2 * A_KV_W)
    out_shape = [jax.ShapeDtypeStruct((n, w), jnp.bfloat16) for w in a_widths]
    out_specs = [pl.BlockSpec((rows, w), lambda i: (i, 0)) for w in a_widths]
    for _ in range(3):
        for dil in B_DILS:
            if dil == 1:
                out_shape.append(jax.ShapeDtypeStruct((n, B_W), jnp.bfloat16))
                out_specs.append(pl.BlockSpec((rows, B_W), lambda i: (i, 0)))
            else:
                out_shape.append(jax.ShapeDtypeStruct((batch, dil, seq // dil, B_W), jnp.bfloat16))
                out_specs.append(pl.BlockSpec((1, dil, rows // dil, B_W),
                                              lambda i: (i // steps, 0, i % steps, 0)))
    return pl.pallas_call(
        _in_proj_kernel,
        out_shape=out_shape,
        grid=(n // rows,),
        in_specs=[
            pl.BlockSpec((rows, D_MODEL), lambda i: (i, 0)),
            pl.BlockSpec((1, D_MODEL), lambda i: (0, 0)),
            pl.BlockSpec(memory_space=pl.ANY),
            pl.BlockSpec(gains.shape, lambda i: (0, 0)),
        ],
        out_specs=out_specs,
        scratch_shapes=[pltpu.VMEM((B_W // LANES, rows, LANES), jnp.float32),
                        pltpu.VMEM((RESIDUE_STRIDE * B_W // LANES, rows // RESIDUE_STRIDE, LANES),
                                   jnp.float32),
                        pltpu.VMEM(w_in.shape, jnp.bfloat16),
                        pltpu.VMEM((w_in.shape[0], w_in.shape[1] // 3), jnp.float32),
                        pltpu.SemaphoreType.DMA],
        compiler_params=pltpu.CompilerParams(
            dimension_semantics=("arbitrary",), vmem_limit_bytes=VMEM_LIMIT),
        name="in_proj",
    )(x2, g, w_in, gains)


def _attn_kernel(*refs, n_chunks, kv_chunks, heads_per_stack, tk, half_w, seq_len, rows, has_sink,
                 want_lse):
    it = iter(refs)
    q_ref, k_ref, v_ref, bias_ref = next(it), next(it), next(it), next(it)
    sink_ref = next(it) if has_sink else None
    o_ref = next(it)
    lse_ref = next(it) if want_lse else None

    n_tiles = seq_len // Q_TILE
    tiles_per_step = rows // Q_TILE
    chunks_per_group = n_chunks // kv_chunks
    assert (2 * chunks_per_group) % heads_per_stack == 0
    step = pl.program_id(1)
    lane = lax.broadcasted_iota(jnp.int32, (Q_TILE, LANES), 1)
    low_half = lane < HEAD_DIM
    ones = jnp.ones((tk, LANES), jnp.bfloat16)

    for sq, t in [(a, b) for a in range(q_ref.shape[0]) for b in range(tiles_per_step)]:
        tile = step * tiles_per_step + t
        q0 = tile * Q_TILE
        kv_rows = k_ref.shape[1]
        kv_row0 = jnp.clip(step * rows - half_w, 0, seq_len - kv_rows)
        start = pl.multiple_of(jnp.clip(q0 - half_w, 0, seq_len - tk) - kv_row0, HEAD_DIM)
        variant = jnp.where(tile == 0, 0, jnp.where(tile == n_tiles - 1, 2, 1))
        r0 = t * Q_TILE
        lse_tile = jnp.zeros((Q_TILE, LANES), jnp.float32)
        head_o, head_lse = {}, {}
        for g in range(2 * n_chunks // heads_per_stack):
            heads = range(g * heads_per_stack, (g + 1) * heads_per_stack)
            kv = (heads[0] // 2) // chunks_per_group
            kc = k_ref[sq, pl.ds(start, tk), kv * LANES:(kv + 1) * LANES]
            vc = v_ref[sq, pl.ds(start, tk), kv * LANES:(kv + 1) * LANES]
            v_aug = jnp.concatenate([vc, ones], axis=1)
            q_parts = []
            for h in heads:
                c = h // 2
                q2 = q_ref[sq, r0:r0 + Q_TILE, c * LANES:(c + 1) * LANES]
                keep = low_half if h % 2 == 0 else ~low_half
                q_parts.append(jnp.where(keep, q2, jnp.zeros_like(q2)))
            qs = q_parts[0] if len(q_parts) == 1 else jnp.concatenate(q_parts, axis=0)
            s = lax.dot_general(qs, kc, (((1,), (1,)), ((), ())),
                                preferred_element_type=jnp.float32)
            s = s + bias_ref[variant, g]
            m = jnp.max(s, axis=-1, keepdims=True)
            if has_sink:
                m = jnp.maximum(m, sink_ref[g])
            p = jnp.exp2(s - m)
            ov = jnp.dot(p.astype(jnp.bfloat16), v_aug, preferred_element_type=jnp.float32)
            o, l = ov[:, :LANES], ov[:, LANES:]
            if has_sink:
                l = l + jnp.exp2(sink_ref[g] - m)
            o = o * (1.0 / l)
            if want_lse:
                lse = m + jnp.log(l) * LOG2E
            for idx, h in enumerate(heads):
                head_o[h] = o[idx * Q_TILE:(idx + 1) * Q_TILE]
                if want_lse:
                    head_lse[h] = lse[idx * Q_TILE:(idx + 1) * Q_TILE]
                if h % 2 == 1:
                    c = h // 2
                    o2 = jnp.where(low_half, head_o.pop(h - 1), head_o.pop(h))
                    o_ref[sq, r0:r0 + Q_TILE, c * LANES:(c + 1) * LANES] = o2.astype(o_ref.dtype)
                    if want_lse:
                        lse_tile = jnp.where(lane == h - 1, head_lse.pop(h - 1),
                                             jnp.where(lane == h, head_lse.pop(h), lse_tile))
        if want_lse:
            lse_ref[sq, r0:r0 + Q_TILE, :] = lse_tile


def _banded_attention(q, k, v, bias, *, half_w, sink=None, want_lse=False, name):
    n_seq, L, qw = q.shape
    kw = k.shape[2]
    tk = Q_TILE + 2 * half_w
    rows = min(ATTN_STEP_ROWS, L)
    seqs = ATTN_STEP_ROWS // rows
    kv_rows = min(L, rows + 2 * half_w)

    def kv_index(s, i):
        row0 = pl.multiple_of(jnp.clip(i * rows - half_w, 0, L - kv_rows), HEAD_DIM)
        return s * seqs, row0, 0

    args = [q, k, v, bias]
    in_specs = [
        pl.BlockSpec((seqs, rows, qw), lambda s, i: (s, i, 0)),
        pl.BlockSpec((pl.Element(seqs), pl.Element(kv_rows), pl.Element(kw)), kv_index),
        pl.BlockSpec((pl.Element(seqs), pl.Element(kv_rows), pl.Element(kw)), kv_index),
        pl.BlockSpec(bias.shape, lambda s, i: (0, 0, 0, 0)),
    ]
    if sink is not None:
        args.append(sink)
        in_specs.append(pl.BlockSpec(sink.shape, lambda s, i: (0, 0, 0)))
    out_shape = [jax.ShapeDtypeStruct((n_seq, L, qw), jnp.bfloat16)]
    out_specs = [pl.BlockSpec((seqs, rows, qw), lambda s, i: (s, i, 0))]
    if want_lse:
        out_shape.append(jax.ShapeDtypeStruct((n_seq, L, LANES), jnp.float32))
        out_specs.append(pl.BlockSpec((seqs, rows, LANES), lambda s, i: (s, i, 0)))

    kern = functools.partial(
        _attn_kernel, n_chunks=qw // LANES, kv_chunks=kw // LANES,
        heads_per_stack=bias.shape[2] // Q_TILE, tk=tk, half_w=half_w, seq_len=L,
        rows=rows, has_sink=sink is not None, want_lse=want_lse)
    return pl.pallas_call(
        kern,
        out_shape=out_shape,
        grid=(n_seq // seqs, L // rows),
        in_specs=in_specs,
        out_specs=out_specs,
        compiler_params=pltpu.CompilerParams(
            dimension_semantics=("arbitrary", "arbitrary"), vmem_limit_bytes=VMEM_LIMIT),
        name=name,
    )(*args)


def _out_proj_router_kernel(*refs):
    nb = len(B_DILS)
    oa_ref = refs[0]
    o_refs = refs[1:1 + nb]
    lse_refs = refs[1 + nb:1 + 2 * nb]
    (x_ref, wo_ref, g_ref, wr_ref, br_ref,
     x1_ref, hf_ref, topi_ref, gate_ref, rank_ref, cnt_ref,
     tri_ref, carry_ref, so_ref, sl_ref, wo_bf_ref) = refs[1 + 2 * nb:]
    i = pl.program_id(0)
    rows = x_ref.shape[0]

    @pl.when(i == 0)
    def _():
        a = lax.broadcasted_iota(jnp.int32, (rows, rows), 0)
        b = lax.broadcasted_iota(jnp.int32, (rows, rows), 1)
        tri_ref[...] = jnp.where(a <= b, 1.0, 0.0).astype(jnp.bfloat16)
        carry_ref[...] = jnp.zeros_like(carry_ref)
        wo_bf_ref[...] = wo_ref[...].astype(jnp.bfloat16)

    outs, lses = [], []
    for bi, dil in enumerate(B_DILS):
        if dil == 1:
            outs.append(o_refs[bi][...].astype(jnp.float32))
            lses.append(lse_refs[bi][...])
        else:
            for res in range(dil):
                for j in range(B_W // LANES):
                    so_ref[bi, j, pl.ds(res, rows // dil, stride=dil), :] = (
                        o_refs[bi][0, res, :, j * LANES:(j + 1) * LANES].astype(jnp.float32))
                sl_ref[bi, pl.ds(res, rows // dil, stride=dil), :] = lse_refs[bi][0, res]
            outs.append(jnp.concatenate([so_ref[bi, j] for j in range(B_W // LANES)], axis=-1))
            lses.append(sl_ref[bi])

    mx = functools.reduce(jnp.maximum, lses)
    es = [jnp.exp2(l - mx) for l in lses]
    inv = 1.0 / functools.reduce(lambda a, b: a + b, es)
    eh = lax.broadcasted_iota(jnp.int32, (LANES, B_W), 0)
    ej = lax.broadcasted_iota(jnp.int32, (LANES, B_W), 1) // HEAD_DIM
    expand = jnp.where(eh == ej, 1.0, 0.0).astype(jnp.bfloat16)
    ob = jnp.zeros((rows, B_W), jnp.float32)
    for e, o in zip(es, outs):
        w = e * inv
        wide = jnp.dot(w.astype(jnp.bfloat16), expand, preferred_element_type=jnp.float32)
        ob = ob + wide * o

    attn = jnp.concatenate([oa_ref[...], ob.astype(jnp.bfloat16)], axis=-1)
    x1 = x_ref[...] + jnp.dot(attn, wo_bf_ref[...], preferred_element_type=jnp.float32)
    x1_ref[...] = x1
    hf = x1 * lax.rsqrt(jnp.mean(x1 * x1, axis=-1, keepdims=True) + NORM_EPS) * g_ref[...]
    hf_hi = hf.astype(jnp.bfloat16)
    hf_ref[...] = _pack_halves(hf)
    hf_lo = (hf - hf_hi.astype(jnp.float32)).astype(jnp.bfloat16)

    nt = (((1,), (1,)), ((), ()))
    lg_hi = lax.dot_general(wr_ref[...], hf_hi, nt, preferred_element_type=jnp.float32)
    lg_lo = lax.dot_general(wr_ref[0:N_EXPERTS, :], hf_lo, nt, preferred_element_type=jnp.float32)
    logits = lg_hi[0:N_EXPERTS] + lg_hi[N_EXPERTS:] + lg_lo + br_ref[:, 0:1]

    eidx = lax.broadcasted_iota(jnp.int32, (N_EXPERTS, rows), 0)
    work = logits
    vals, sels = [], []
    for k in range(TOP_K):
        mk = jnp.max(work, axis=0, keepdims=True)
        ik = jnp.min(jnp.where(work == mk, eidx, N_EXPERTS), axis=0, keepdims=True)
        sel = eidx == ik
        work = jnp.where(sel, -jnp.inf, work)
        vals.append(mk)
        sels.append(sel)
        topi_ref[k:k + 1, :] = ik
    exps = [jnp.exp(vk - vals[0]) for vk in vals]
    denom = exps[0] + exps[1] + exps[2] + exps[3]
    ginv = 1.0 / denom
    for k in range(TOP_K):
        gate_ref[k:k + 1, :] = exps[k] * ginv

    onehot = jnp.zeros((N_EXPERTS, rows), jnp.float32)
    for sel in sels:
        onehot = onehot + jnp.where(sel, 1.0, 0.0)
    incl = jnp.dot(onehot.astype(jnp.bfloat16), tri_ref[...], preferred_element_type=jnp.float32)
    before = incl - onehot + carry_ref[:, 0:1]
    for k in range(TOP_K):
        rk = jnp.sum(jnp.where(sels[k], before, 0.0), axis=0, keepdims=True)
        rank_ref[k:k + 1, :] = rk.astype(jnp.int32)
    carry = carry_ref[...] + jnp.sum(onehot, axis=1, keepdims=True)
    carry_ref[...] = carry
    cnt_ref[...] = carry.astype(jnp.int32)


def _out_proj_router(oa, outs_b, lses_b, x2, wo_bf, g, wr, br, seq):
    n = x2.shape[0]
    rows = PROJ_ROWS
    steps = seq // rows
    row_spec = lambda w: pl.BlockSpec((rows, w), lambda i: (i, 0))
    full = lambda a: pl.BlockSpec(a.shape, lambda i: (0,) * a.ndim)
    col_spec = pl.BlockSpec((TOP_K, rows), lambda i: (0, i))

    def branch_spec(dil, w):
        if dil == 1:
            return row_spec(w)
        return pl.BlockSpec((1, dil, rows // dil, w), lambda i: (i // steps, 0, i % steps, 0))

    in_specs = ([row_spec(A_Q_W)]
                + [branch_spec(d, B_W) for d in B_DILS]
                + [branch_spec(d, LANES) for d in B_DILS]
                + [row_spec(D_MODEL), full(wo_bf), full(g), full(wr), full(br)])
    return pl.pallas_call(
        _out_proj_router_kernel,
        out_shape=[
            jax.ShapeDtypeStruct((n, D_MODEL), jnp.float32),
            jax.ShapeDtypeStruct((n, HALF_D), jnp.uint32),
            jax.ShapeDtypeStruct((TOP_K, n), jnp.int32),
            jax.ShapeDtypeStruct((TOP_K, n), jnp.float32),
            jax.ShapeDtypeStruct((TOP_K, n), jnp.int32),
            jax.ShapeDtypeStruct((N_EXPERTS, LANES), jnp.int32),
        ],
        grid=(n // rows,),
        in_specs=in_specs,
        out_specs=[row_spec(D_MODEL), row_spec(HALF_D), col_spec, col_spec, col_spec,
                   pl.BlockSpec((N_EXPERTS, LANES), lambda i: (0, 0))],
        scratch_shapes=[pltpu.VMEM((rows, rows), jnp.bfloat16),
                        pltpu.VMEM((N_EXPERTS, LANES), jnp.float32),
                        pltpu.VMEM((len(B_DILS), B_W // LANES, rows, LANES), jnp.float32),
                        pltpu.VMEM((len(B_DILS), rows, LANES), jnp.float32),
                        pltpu.VMEM(wo_bf.shape, jnp.bfloat16)],
        compiler_params=pltpu.CompilerParams(
            dimension_semantics=("arbitrary",), vmem_limit_bytes=VMEM_LIMIT),
        name="out_proj_router",
    )(oa, *outs_b, *lses_b, x2, wo_bf, g, wr, br)


def _mxu_dot(a_bf, w_f32):
    return lax.dot_general(a_bf, w_f32, (((1,), (0,)), ((), ())), preferred_element_type=jnp.float32)


def _moe_kernel(blk_exp_ref, first_ref, slot_ref, next_exp_ref, n_used_ref,
                x_ref, w1_hbm, b1_ref, w2_hbm, b2_ref, y_ref, w1_buf, w2_buf, sem):
    step = pl.program_id(0)

    def weight_copies(expert, slot):
        return (pltpu.make_async_copy(w1_hbm.at[expert], w1_buf.at[slot], sem.at[slot, 0]),
                pltpu.make_async_copy(w2_hbm.at[expert], w2_buf.at[slot], sem.at[slot, 1]))

    @pl.when(step * MOE_STEP_BLOCKS < n_used_ref[0])
    def _():
        @pl.when(step == 0)
        def _():
            for cp in weight_copies(blk_exp_ref[0], slot_ref[0]):
                cp.start()

        for j in range(MOE_STEP_BLOCKS):
            i = step * MOE_STEP_BLOCKS + j

            @pl.when(first_ref[i] == 1)
            def _():
                slot = slot_ref[i]
                for cp in weight_copies(blk_exp_ref[i], slot):
                    cp.wait()

                @pl.when(next_exp_ref[i] >= 0)
                def _():
                    for cp in weight_copies(next_exp_ref[i], (slot + 1) % MOE_WEIGHT_BUFS):
                        cp.start()

        for j in range(MOE_STEP_BLOCKS):
            i = step * MOE_STEP_BLOCKS + j
            slot = slot_ref[i]
            expert = blk_exp_ref[i]
            rs = slice(j * MOE_ROWS, (j + 1) * MOE_ROWS)
            x = jnp.concatenate(_unpack_halves(x_ref[rs, :]), axis=-1).astype(jnp.bfloat16)
            acc = jnp.zeros((MOE_ROWS, D_MODEL), jnp.float32)
            for c in range(D_FF // FF_CHUNK):
                lo = c * FF_CHUNK
                glu = _mxu_dot(x, w1_buf[slot, :, lo:lo + FF_CHUNK]) + b1_ref[expert, :, lo:lo + FF_CHUNK]
                lin = (_mxu_dot(x, w1_buf[slot, :, D_FF + lo:D_FF + lo + FF_CHUNK])
                       + b1_ref[expert, :, D_FF + lo:D_FF + lo + FF_CHUNK])
                glu = jnp.minimum(glu, SWIGLU_LIMIT)
                lin = jnp.clip(lin, -SWIGLU_LIMIT, SWIGLU_LIMIT)
                act = glu * (1.0 / (1.0 + jnp.exp(-SWIGLU_ALPHA * glu))) * (lin + 1.0)
                acc = acc + _mxu_dot(act.astype(jnp.bfloat16), w2_buf[slot, lo:lo + FF_CHUNK, :])
            y_ref[rs, :] = _pack_halves(acc + b2_ref[expert])


def _moe_plan(pends, n_blk):
    g = MOE_ROWS
    blk_row0 = jnp.arange(n_blk, dtype=jnp.int32) * g
    blk_exp = jnp.minimum(jnp.sum(pends[None, :] <= blk_row0[:, None], axis=-1),
                          N_EXPERTS - 1).astype(jnp.int32)
    n_used = (pends[-1] // g).astype(jnp.int32)
    used = blk_row0 < pends[-1]
    prev_exp = jnp.concatenate([jnp.full((1,), -1, jnp.int32), blk_exp[:-1]])
    first = (used & (blk_exp != prev_exp)).astype(jnp.int32)
    slot = ((jnp.cumsum(first) - 1) % MOE_WEIGHT_BUFS).astype(jnp.int32)
    pstarts = jnp.concatenate([jnp.zeros((1,), pends.dtype), pends[:-1]])
    nonempty = pends > pstarts
    experts = jnp.arange(N_EXPERTS, dtype=jnp.int32)
    later = nonempty[None, :] & (experts[None, :] > experts[:, None])
    next_nonempty = jnp.min(jnp.where(later, experts[None, :], N_EXPERTS), axis=-1)
    next_nonempty = jnp.where(next_nonempty == N_EXPERTS, -1, next_nonempty).astype(jnp.int32)
    next_exp = jnp.sum(jnp.where(blk_exp[:, None] == experts[None, :], next_nonempty[None, :], 0),
                       axis=-1).astype(jnp.int32)
    return blk_exp, first, slot, next_exp, n_used.reshape(1)


def _moe_experts(plan, xb, w1, b1, w2, b2):
    n_rows = xb.shape[0]
    n_blk = n_rows // MOE_ROWS

    step_rows = MOE_STEP_BLOCKS * MOE_ROWS

    def blk(i, *p):
        return jnp.minimum(i, (p[-1][0] - 1) // MOE_STEP_BLOCKS)

    grid_spec = pltpu.PrefetchScalarGridSpec(
        num_scalar_prefetch=len(plan),
        grid=(n_blk // MOE_STEP_BLOCKS,),
        in_specs=[
            pl.BlockSpec((step_rows, HALF_D), lambda i, *p: (blk(i, *p), 0)),
            pl.BlockSpec(memory_space=pl.ANY),
            pl.BlockSpec(b1.shape, lambda i, *p: (0, 0, 0)),
            pl.BlockSpec(memory_space=pl.ANY),
            pl.BlockSpec(b2.shape, lambda i, *p: (0, 0, 0)),
        ],
        out_specs=pl.BlockSpec((step_rows, HALF_D), lambda i, *p: (blk(i, *p), 0)),
        scratch_shapes=[pltpu.VMEM((MOE_WEIGHT_BUFS, D_MODEL, 2 * D_FF), jnp.float32),
                        pltpu.VMEM((MOE_WEIGHT_BUFS, D_FF, D_MODEL), jnp.float32),
                        pltpu.SemaphoreType.DMA((MOE_WEIGHT_BUFS, 2))],
    )
    return pl.pallas_call(
        _moe_kernel,
        out_shape=jax.ShapeDtypeStruct((n_rows, HALF_D), jnp.uint32),
        grid_spec=grid_spec,
        compiler_params=pltpu.CompilerParams(
            dimension_semantics=("arbitrary",), vmem_limit_bytes=MOE_VMEM_LIMIT),
        name="moe_experts",
    )(*plan, xb, w1, b1, w2, b2)


def _combine_kernel(x1_ref, yg_ref, gate_ref, o_ref):
    acc_lo = x1_ref[:, :HALF_D]
    acc_hi = x1_ref[:, HALF_D:]
    rows = x1_ref.shape[0]
    gates = jnp.concatenate([gate_ref[...], jnp.zeros((8 - TOP_K, rows), jnp.float32)], axis=0).T
    for k in range(TOP_K):
        lo, hi = _unpack_halves(yg_ref[k])
        gk = gates[:, k:k + 1]
        acc_lo = acc_lo + lo * gk
        acc_hi = acc_hi + hi * gk
    o_ref[:, :HALF_D] = acc_lo
    o_ref[:, HALF_D:] = acc_hi


def _combine(acc, yg, gates_nk, split):
    n = acc.shape[0]
    rows = PROJ_ROWS
    steps = yg.shape[1] // rows
    first = split * steps
    return pl.pallas_call(
        _combine_kernel,
        out_shape=jax.ShapeDtypeStruct((n, D_MODEL), jnp.float32),
        grid=(steps,),
        in_specs=[pl.BlockSpec((rows, D_MODEL), lambda i: (first + i, 0)),
                  pl.BlockSpec((TOP_K, rows, HALF_D), lambda i: (0, i, 0)),
                  pl.BlockSpec((TOP_K, rows), lambda i: (0, first + i))],
        out_specs=pl.BlockSpec((rows, D_MODEL), lambda i: (first + i, 0)),
        input_output_aliases={0: 0},
        compiler_params=pltpu.CompilerParams(
            dimension_semantics=("arbitrary",), vmem_limit_bytes=VMEM_LIMIT),
        name="moe_combine",
    )(acc, yg, gates_nk)


def _sc_worker_id():
    return lax.axis_index("s") * SC_CORES + lax.axis_index("c")


def _sc_dispatch(hf, dest3, n_rows):
    n = hf.shape[0]
    chunks_per_worker = n // SC_CHUNK // SC_WORKERS
    mesh = plsc.VectorSubcoreMesh(core_axis_name="c", subcore_axis_name="s")

    @functools.partial(
        pl.kernel, mesh=mesh,
        out_type=jax.ShapeDtypeStruct((n_rows, HALF_D), hf.dtype),
        scratch_types=[pltpu.VMEM((2, TOP_K, SC_CHUNK), jnp.int32),
                       pltpu.VMEM((2, SC_CHUNK, HALF_D), hf.dtype),
                       pltpu.SemaphoreType.DMA((2,)),
                       pltpu.SemaphoreType.DMA((2,))],
        name="sc_dispatch")
    def run(hf_hbm, dest_hbm, xb_hbm, idx_v, rows_v, load_sem, scatter_sem):
        first = _sc_worker_id() * chunks_per_worker

        def load(j):
            slot = j % 2
            pltpu.sync_copy(dest_hbm.at[first + j], idx_v.at[slot])
            return pltpu.async_copy(hf_hbm.at[pl.ds((first + j) * SC_CHUNK, SC_CHUNK)],
                                    rows_v.at[slot], load_sem.at[slot])

        loads = {0: load(0)}
        scatters = {}
        for j in range(chunks_per_worker):
            slot = j % 2
            loads.pop(j).wait()
            scatters[j] = [pltpu.async_copy(rows_v.at[slot], xb_hbm.at[idx_v.at[slot, k]],
                                            scatter_sem.at[slot]) for k in range(TOP_K)]
            if j >= 1:
                for cp in scatters.pop(j - 1):
                    cp.wait()
            if j + 1 < chunks_per_worker:
                loads[j + 1] = load(j + 1)
        for cp in scatters.pop(chunks_per_worker - 1):
            cp.wait()

    return run(hf, dest3)


def _sc_collect(y, dest3):
    n = dest3.shape[0] * SC_COLLECT_CHUNK
    chunks_per_worker = n // SC_COLLECT_CHUNK // SC_WORKERS
    mesh = plsc.VectorSubcoreMesh(core_axis_name="c", subcore_axis_name="s")

    @functools.partial(
        pl.kernel, mesh=mesh,
        out_type=jax.ShapeDtypeStruct((TOP_K, n, HALF_D), y.dtype),
        scratch_types=[pltpu.VMEM((TOP_K, SC_COLLECT_CHUNK), jnp.int32),
                       pltpu.VMEM((TOP_K, SC_COLLECT_CHUNK, HALF_D), y.dtype),
                       pltpu.SemaphoreType.DMA((TOP_K,)),
                       pltpu.SemaphoreType.DMA((TOP_K,))],
        name="sc_collect")
    def run(y_hbm, dest_hbm, yg_hbm, idx_v, rows_v, gather_sem, write_sem):
        first = _sc_worker_id() * chunks_per_worker

        @pl.loop(0, chunks_per_worker)
        def _(j):
            ch = first + j
            pltpu.sync_copy(dest_hbm.at[ch], idx_v)
            gathers = [pltpu.async_copy(y_hbm.at[idx_v.at[k]], rows_v.at[k], gather_sem.at[k])
                       for k in range(TOP_K)]
            writes = []
            for k in range(TOP_K):
                gathers[k].wait()
                writes.append(pltpu.async_copy(
                    rows_v.at[k], yg_hbm.at[k, pl.ds(ch * SC_COLLECT_CHUNK, SC_COLLECT_CHUNK)],
                    write_sem.at[k]))
            for w in writes:
                w.wait()

    return run(y, dest3)


def _layer(x2, batch, seq, attn_norm_g, w_in, a_q_g, a_k_g, a_sinks, b_q_g, b_k_g, w_out,
           ffn_norm_g, w_router, b_router, w1, b1, w2, b2):
    n = x2.shape[0]
    slopes = _alibi_slopes()
    q_scale = HEAD_DIM ** -0.5 * LOG2E
    reps = MXU_DIM // HEAD_DIM
    gains = jnp.stack([jnp.tile(a_q_g, reps) * q_scale, jnp.tile(a_k_g, reps),
                       jnp.tile(b_q_g, reps) * q_scale, jnp.tile(b_k_g, reps)]).astype(jnp.float32)

    proj = _in_proj(x2, attn_norm_g.reshape(1, -1), w_in, gains, batch, seq)
    qa, ka, va = proj[:3]
    nb = len(B_DILS)
    qbs, kbs, vbs = proj[3:3 + nb], proj[3 + nb:3 + 2 * nb], proj[3 + 2 * nb:]

    bias_a = _bias_tables(slopes[:A_Q_HEADS], A_STACK_HEADS, A_HALF_WINDOW, 1, Q_TILE + 2 * A_HALF_WINDOW)
    sink_col = jnp.repeat(a_sinks.astype(jnp.float32) * LOG2E, Q_TILE).reshape(
        A_Q_HEADS // A_STACK_HEADS, A_STACK_HEADS * Q_TILE, 1)
    as_seqs = lambda a: a.reshape(batch, seq, a.shape[-1])
    out_a = _banded_attention(as_seqs(qa), as_seqs(ka), as_seqs(va), bias_a, half_w=A_HALF_WINDOW,
                              sink=sink_col, name="attn_a")[0].reshape(n, A_Q_W)

    outs_b, lses_b = [], []
    for bi, (window, dil) in enumerate(B_BRANCHES):
        half_w = window // (2 * dil)
        bias_b = _bias_tables(slopes[A_Q_HEADS:], B_STACK_HEADS, half_w, dil, Q_TILE + 2 * half_w)
        L = seq // dil
        to_seqs = lambda a: a.reshape(batch * dil, L, a.shape[-1])
        o, lse = _banded_attention(to_seqs(qbs[bi]), to_seqs(kbs[bi]), to_seqs(vbs[bi]), bias_b,
                                   half_w=half_w, want_lse=True, name=f"attn_b_d{dil}")
        if dil == 1:
            outs_b.append(o.reshape(n, B_W))
            lses_b.append(lse.reshape(n, LANES))
        else:
            outs_b.append(o.reshape(batch, dil, L, B_W))
            lses_b.append(lse.reshape(batch, dil, L, LANES))

    wr_t = w_router.T.astype(jnp.float32)
    wr_hi = wr_t.astype(jnp.bfloat16)
    wr_lo = (wr_t - wr_hi.astype(jnp.float32)).astype(jnp.bfloat16)
    wr = jnp.concatenate([wr_hi, wr_lo], axis=0)
    br = jnp.broadcast_to(b_router.astype(jnp.float32)[:, None], (N_EXPERTS, LANES))
    x1, hf, topi, gates, ranks, counts = _out_proj_router(
        out_a, outs_b, lses_b, x2, w_out, ffn_norm_g.reshape(1, -1), wr, br, seq)

    g = MOE_ROWS
    nk = n * TOP_K
    n_rows = nk + N_EXPERTS * g
    cnt = counts[:, 0]
    pcnt = (cnt + g - 1) // g * g
    pends = jnp.cumsum(pcnt)
    pstarts = pends - pcnt
    experts = jnp.arange(N_EXPERTS, dtype=jnp.int32)
    start_of = jnp.sum(jnp.where(topi[:, :, None] == experts, pstarts, 0), axis=-1)
    dest = (start_of + ranks).astype(jnp.int32)
    plan = _moe_plan(pends, n_rows // g)
    dest3 = dest.reshape(TOP_K, n // SC_CHUNK, SC_CHUNK).transpose(1, 0, 2)

    xb = _sc_dispatch(hf, dest3, n_rows)
    y = _moe_experts(plan, xb, w1, b1[:, None, :], w2, b2[:, None, :])
    per_split = n // COMBINE_SPLITS
    dest_c = dest.reshape(TOP_K, n // SC_COLLECT_CHUNK, SC_COLLECT_CHUNK).transpose(1, 0, 2)
    chunks_per_split = per_split // SC_COLLECT_CHUNK
    out = x1
    for s in range(COMBINE_SPLITS):
        yg = _sc_collect(y, dest_c[s * chunks_per_split:(s + 1) * chunks_per_split])
        out = _combine(out, yg, gates, s)
    return out


def kernel(x, attn_norm_g, w_in, a_q_norm_g, a_k_norm_g, a_sinks, b_q_norm_g, b_k_norm_g, w_out,
           ffn_norm_g, w_router, b_router, w1, b1, w2, b2):
    batch, seq, d = x.shape
    x2 = x.reshape(batch * seq, d)
    for i in range(attn_norm_g.shape[0]):
        x2 = _layer(x2, batch, seq, attn_norm_g[i], w_in[i], a_q_norm_g[i], a_k_norm_g[i],
                    a_sinks[i], b_q_norm_g[i], b_k_norm_g[i], w_out[i], ffn_norm_g[i],
                    w_router[i], b_router[i], w1[i], b1[i], w2[i], b2[i])
    return x2.reshape(batch, seq, d)
```

```python
import functools

import jax
import jax.numpy as jnp
import numpy as np
from jax import lax
from jax.experimental import pallas as pl
from jax.experimental.pallas import tpu as pltpu
from jax.experimental.pallas import tpu_sc as plsc

D_MODEL = 1024
HALF_D = D_MODEL // 2
HEAD_DIM = 64
LANES = 128
MXU_DIM = 256
A_Q_HEADS = 8
A_KV_HEADS = 2
B_HEADS = 8
A_HALF_WINDOW = 128
B_BRANCHES = ((128, 1), (512, 4), (2048, 16))
B_DILS = tuple(d for _, d in B_BRANCHES)
RESIDUE_STRIDE = 4
N_ALIBI_HEADS = 16
A_Q_W = A_Q_HEADS * HEAD_DIM
A_KV_W = A_KV_HEADS * HEAD_DIM
B_W = B_HEADS * HEAD_DIM
N_EXPERTS = 32
TOP_K = 4
D_FF = 1024
SWIGLU_ALPHA = 1.702
SWIGLU_LIMIT = 7.0
NORM_EPS = 1e-5
MASK_VALUE = -1e30
LOG2E = 1.4426950408889634

Q_TILE = 128
ATTN_STEP_ROWS = 512
A_STACK_HEADS = 4
B_STACK_HEADS = 2
PROJ_ROWS = 512
OUT_PROJ_SUB_ROWS = 128
MOE_ROWS = 512
FF_CHUNK = 512
MOE_STEP_BLOCKS = 2
MOE_WEIGHT_BUFS = 3
VMEM_LIMIT = 48 * 1024 * 1024
MOE_VMEM_LIMIT = 58 * 1024 * 1024
SC_CORES = 2
SC_SUBCORES = 16
SC_WORKERS = SC_CORES * SC_SUBCORES
SC_CHUNK = 64
SC_COLLECT_CHUNK = 32
COMBINE_SPLITS = 4


def _pack_halves(v):
    lo = v[:, :HALF_D].astype(jnp.bfloat16).astype(jnp.float32)
    hi = v[:, HALF_D:].astype(jnp.bfloat16).astype(jnp.float32)
    return (pltpu.bitcast(lo, jnp.uint32) >> 16) | pltpu.bitcast(hi, jnp.uint32)


def _unpack_halves(w):
    lo = pltpu.bitcast(w << 16, jnp.float32)
    hi = pltpu.bitcast(w & jnp.uint32(0xFFFF0000), jnp.float32)
    return lo, hi


def _alibi_slopes():
    return np.exp2(-8.0 * np.arange(1, N_ALIBI_HEADS + 1, dtype=np.float32) / N_ALIBI_HEADS).astype(np.float32)


def _bias_tables(head_slopes, heads_per_group, half_w, dist_scale, tk):
    i = np.arange(Q_TILE)[:, None]
    j = np.arange(tk)[None, :]
    tabs = []
    for shift in (0, half_w, tk - Q_TILE):
        dist = np.abs(j - shift - i)
        valid = dist <= half_w
        per_head = []
        for sl in head_slopes:
            b = (-np.float64(sl) * LOG2E * (dist * dist_scale)).astype(np.float32)
            per_head.append(np.where(valid, b, np.float32(MASK_VALUE)).astype(np.float32))
        t = np.stack(per_head).reshape(-1, heads_per_group * Q_TILE, tk)
        tabs.append(t)
    return jnp.asarray(np.stack(tabs))


def _in_proj_kernel(x_ref, g_ref, w_hbm, gains_ref, qa_ref, ka_ref, va_ref, *rest):
    b_refs, (scr_ref, scr2_ref, w_ref, stage_ref, sem) = rest[:-5], rest[-5:]

    @pl.when(pl.program_id(0) == 0)
    def _():
        width = stage_ref.shape[1]
        for c0 in range(0, w_ref.shape[1], width):
            cp = pltpu.make_async_copy(w_hbm.at[:, c0:c0 + width], stage_ref, sem)
            cp.start()
            cp.wait()
            w_ref[:, c0:c0 + width] = stage_ref[...].astype(jnp.bfloat16)

    x = x_ref[...]
    xn = x * lax.rsqrt(jnp.mean(x * x, axis=-1, keepdims=True) + NORM_EPS) * g_ref[...]
    xn = xn.astype(jnp.bfloat16)
    r = lax.broadcasted_iota(jnp.int32, (MXU_DIM, MXU_DIM), 0) // HEAD_DIM
    c = lax.broadcasted_iota(jnp.int32, (MXU_DIM, MXU_DIM), 1) // HEAD_DIM
    blockdiag = jnp.where(r == c, 1.0, 0.0).astype(jnp.bfloat16)

    def head_rms(sec, gain_row):
        width = sec.shape[1]
        parts = []
        step = min(width, MXU_DIM)
        for j in range(width // step):
            p = sec[:, j * step:(j + 1) * step]
            ss = jnp.dot((p * p).astype(jnp.bfloat16), blockdiag[:step, :step],
                         preferred_element_type=jnp.float32)
            parts.append(p * lax.rsqrt(ss * (1.0 / HEAD_DIM) + NORM_EPS)
                         * gains_ref[gain_row:gain_row + 1, :step])
        return parts

    def project(col0, width, gain_row):
        sec = jnp.dot(xn, w_ref[:, col0:col0 + width], preferred_element_type=jnp.float32)
        return [sec] if gain_row is None else head_rms(sec, gain_row)

    def store(out_ref, parts):
        w = parts[0].shape[1]
        for j, p in enumerate(parts):
            out_ref[:, j * w:(j + 1) * w] = p.astype(out_ref.dtype)

    def per_kv_head(p):
        lane = lax.broadcasted_iota(jnp.int32, p.shape, 1)
        swapped = pltpu.roll(p, HEAD_DIM, axis=1)
        low = lane < HEAD_DIM
        return [jnp.where(low, p, swapped), jnp.where(low, swapped, p)]

    store(qa_ref, project(0, A_Q_W, 0))
    kva = project(A_Q_W, 2 * A_KV_W, None)[0]
    store(ka_ref, per_kv_head(head_rms(kva[:, :A_KV_W], 1)[0]))
    store(va_ref, per_kv_head(kva[:, A_KV_W:]))

    rows = x_ref.shape[0]
    col0 = A_Q_W + 2 * A_KV_W
    for t, gain_row in enumerate((2, 3, None)):
        parts = project(col0 + t * B_W, B_W, gain_row)
        sec = jnp.concatenate(parts, axis=-1) if len(parts) > 1 else parts[0]
        for j in range(B_W // LANES):
            scr_ref[j] = sec[:, j * LANES:(j + 1) * LANES]
        n_lane_chunks = B_W // LANES
        prev_dil = 1
        for bi, dil in enumerate(B_DILS):
            out_ref = b_refs[t * len(B_DILS) + bi]
            if dil == 1:
                out_ref[...] = sec.astype(out_ref.dtype)
                continue
            assert dil == prev_dil * RESIDUE_STRIDE
            last = dil == B_DILS[-1]
            for res in range(dil):
                r_prev, r_sub = res % prev_dil, res // prev_dil
                for j in range(n_lane_chunks):
                    if prev_dil == 1:
                        v = scr_ref[j, pl.ds(r_sub, rows // dil, stride=RESIDUE_STRIDE), :]
                    else:
                        v = scr2_ref[r_prev * n_lane_chunks + j,
                                     pl.ds(r_sub, rows // dil, stride=RESIDUE_STRIDE), :]
                    out_ref[0, res, :, j * LANES:(j + 1) * LANES] = v.astype(out_ref.dtype)
                    if not last:
                        scr2_ref[res * n_lane_chunks + j] = v
            prev_dil = dil


def _in_proj(x2, g, w_in, gains, batch, seq):
    n = x2.shape[0]
    rows = PROJ_ROWS
    steps = seq // rows
    a_widths = (A_Q_W, 2 * A_KV_W, 2 * A_KV_W)
    out_shape = [jax.ShapeDtypeStruct((n, w), jnp.bfloat16) for w in a_widths]
    out_specs = [pl.BlockSpec((rows, w), lambda i: (i, 0)) for w in a_widths]
    for _ in range(3):
        for dil in B_DILS:
            if dil == 1:
                out_shape.append(jax.ShapeDtypeStruct((n, B_W), jnp.bfloat16))
                out_specs.append(pl.BlockSpec((rows, B_W), lambda i: (i, 0)))
            else:
                out_shape.append(jax.ShapeDtypeStruct((batch, dil, seq // dil, B_W), jnp.bfloat16))
                out_specs.append(pl.BlockSpec((1, dil, rows // dil, B_W),
                                              lambda i: (i // steps, 0, i % steps, 0)))
    return pl.pallas_call(
        _in_proj_kernel,
        out_shape=out_shape,
        grid=(n // rows,),
        in_specs=[
            pl.BlockSpec((rows, D_MODEL), lambda i: (i, 0)),
            pl.BlockSpec((1, D_MODEL), lambda i: (0, 0)),
            pl.BlockSpec(memory_space=pl.ANY),
            pl.BlockSpec(gains.shape, lambda i: (0, 0)),
        ],
        out_specs=out_specs,
        scratch_shapes=[pltpu.VMEM((B_W // LANES, rows, LANES), jnp.float32),
                        pltpu.VMEM((RESIDUE_STRIDE * B_W // LANES, rows // RESIDUE_STRIDE, LANES),
                                   jnp.float32),
                        pltpu.VMEM(w_in.shape, jnp.bfloat16),
                        pltpu.VMEM((w_in.shape[0], w_in.shape[1] // 3), jnp.float32),
                        pltpu.SemaphoreType.DMA],
        compiler_params=pltpu.CompilerParams(
            dimension_semantics=("arbitrary",), vmem_limit_bytes=VMEM_LIMIT),
        name="in_proj",
    )(x2, g, w_in, gains)


def _attn_kernel(*refs, n_chunks, kv_chunks, heads_per_stack, tk, half_w, seq_len, rows, has_sink,
                 want_lse):
    it = iter(refs)
    q_ref, k_ref, v_ref, bias_ref = next(it), next(it), next(it), next(it)
    sink_ref = next(it) if has_sink else None
    o_ref = next(it)
    lse_ref = next(it) if want_lse else None

    n_tiles = seq_len // Q_TILE
    tiles_per_step = rows // Q_TILE
    chunks_per_group = n_chunks // kv_chunks
    assert (2 * chunks_per_group) % heads_per_stack == 0
    step = pl.program_id(1)
    lane = lax.broadcasted_iota(jnp.int32, (Q_TILE, LANES), 1)
    low_half = lane < HEAD_DIM
    ones = jnp.ones((tk, LANES), jnp.bfloat16)

    for sq, t in [(a, b) for a in range(q_ref.shape[0]) for b in range(tiles_per_step)]:
        tile = step * tiles_per_step + t
        q0 = tile * Q_TILE
        kv_rows = k_ref.shape[1]
        kv_row0 = jnp.clip(step * rows - half_w, 0, seq_len - kv_rows)
        start = pl.multiple_of(jnp.clip(q0 - half_w, 0, seq_len - tk) - kv_row0, HEAD_DIM)
        variant = jnp.where(tile == 0, 0, jnp.where(tile == n_tiles - 1, 2, 1))
        r0 = t * Q_TILE
        lse_tile = jnp.zeros((Q_TILE, LANES), jnp.float32)
        head_o, head_lse = {}, {}
        for g in range(2 * n_chunks // heads_per_stack):
            heads = range(g * heads_per_stack, (g + 1) * heads_per_stack)
            kv = (heads[0] // 2) // chunks_per_group
            kc = k_ref[sq, pl.ds(start, tk), kv * LANES:(kv + 1) * LANES]
            vc = v_ref[sq, pl.ds(start, tk), kv * LANES:(kv + 1) * LANES]
            v_aug = jnp.concatenate([vc, ones], axis=1)
            q_parts = []
            for h in heads:
                c = h // 2
                q2 = q_ref[sq, r0:r0 + Q_TILE, c * LANES:(c + 1) * LANES]
                keep = low_half if h % 2 == 0 else ~low_half
                q_parts.append(jnp.where(keep, q2, jnp.zeros_like(q2)))
            qs = q_parts[0] if len(q_parts) == 1 else jnp.concatenate(q_parts, axis=0)
            s = lax.dot_general(qs, kc, (((1,), (1,)), ((), ())),
                                preferred_element_type=jnp.float32)
            s = s + bias_ref[variant, g]
            m = jnp.max(s, axis=-1, keepdims=True)
            if has_sink:
                m = jnp.maximum(m, sink_ref[g])
            p = jnp.exp2(s - m)
            ov = jnp.dot(p.astype(jnp.bfloat16), v_aug, preferred_element_type=jnp.float32)
            o, l = ov[:, :LANES], ov[:, LANES:]
            if has_sink:
                l = l + jnp.exp2(sink_ref[g] - m)
            o = o * (1.0 / l)
            if want_lse:
                lse = m + jnp.log(l) * LOG2E
            for idx, h in enumerate(heads):
                head_o[h] = o[idx * Q_TILE:(idx + 1) * Q_TILE]
                if want_lse:
                    head_lse[h] = lse[idx * Q_TILE:(idx + 1) * Q_TILE]
                if h % 2 == 1:
                    c = h // 2
                    o2 = jnp.where(low_half, head_o.pop(h - 1), head_o.pop(h))
                    o_ref[sq, r0:r0 + Q_TILE, c * LANES:(c + 1) * LANES] = o2.astype(o_ref.dtype)
                    if want_lse:
                        lse_tile = jnp.where(lane == h - 1, head_lse.pop(h - 1),
                                             jnp.where(lane == h, head_lse.pop(h), lse_tile))
        if want_lse:
            lse_ref[sq, r0:r0 + Q_TILE, :] = lse_tile


def _banded_attention(q, k, v, bias, *, half_w, sink=None, want_lse=False, name):
    n_seq, L, qw = q.shape
    kw = k.shape[2]
    tk = Q_TILE + 2 * half_w
    rows = min(ATTN_STEP_ROWS, L)
    seqs = ATTN_STEP_ROWS // rows
    kv_rows = min(L, rows + 2 * half_w)

    def kv_index(s, i):
        row0 = pl.multiple_of(jnp.clip(i * rows - half_w, 0, L - kv_rows), HEAD_DIM)
        return s * seqs, row0, 0

    args = [q, k, v, bias]
    in_specs = [
        pl.BlockSpec((seqs, rows, qw), lambda s, i: (s, i, 0)),
        pl.BlockSpec((pl.Element(seqs), pl.Element(kv_rows), pl.Element(kw)), kv_index),
        pl.BlockSpec((pl.Element(seqs), pl.Element(kv_rows), pl.Element(kw)), kv_index),
        pl.BlockSpec(bias.shape, lambda s, i: (0, 0, 0, 0)),
    ]
    if sink is not None:
        args.append(sink)
        in_specs.append(pl.BlockSpec(sink.shape, lambda s, i: (0, 0, 0)))
    out_shape = [jax.ShapeDtypeStruct((n_seq, L, qw), jnp.bfloat16)]
    out_specs = [pl.BlockSpec((seqs, rows, qw), lambda s, i: (s, i, 0))]
    if want_lse:
        out_shape.append(jax.ShapeDtypeStruct((n_seq, L, LANES), jnp.float32))
        out_specs.append(pl.BlockSpec((seqs, rows, LANES), lambda s, i: (s, i, 0)))

    kern = functools.partial(
        _attn_kernel, n_chunks=qw // LANES, kv_chunks=kw // LANES,
        heads_per_stack=bias.shape[2] // Q_TILE, tk=tk, half_w=half_w, seq_len=L,
        rows=rows, has_sink=sink is not None, want_lse=want_lse)
    return pl.pallas_call(
        kern,
        out_shape=out_shape,
        grid=(n_seq // seqs, L // rows),
        in_specs=in_specs,
        out_specs=out_specs,
        compiler_params=pltpu.CompilerParams(
            dimension_semantics=("arbitrary", "arbitrary"), vmem_limit_bytes=VMEM_LIMIT),
        name=name,
    )(*args)


def _out_proj_router_kernel(*refs):
    nb = len(B_DILS)
    oa_ref = refs[0]
    o_refs = refs[1:1 + nb]
    lse_refs = refs[1 + nb:1 + 2 * nb]
    (x_ref, wo_ref, g_ref, wr_ref, br_ref,
     x1_ref, hf_ref, topi_ref, gate_ref, rank_ref, cnt_ref,
     tri_ref, carry_ref, so_ref, sl_ref, wo_bf_ref, hf_hi_ref, hf_lo_ref) = refs[1 + 2 * nb:]
    i = pl.program_id(0)
    rows = x_ref.shape[0]

    @pl.when(i == 0)
    def _():
        a = lax.broadcasted_iota(jnp.int32, (rows, rows), 0)
        b = lax.broadcasted_iota(jnp.int32, (rows, rows), 1)
        tri_ref[...] = jnp.where(a <= b, 1.0, 0.0).astype(jnp.bfloat16)
        carry_ref[...] = jnp.zeros_like(carry_ref)
        wo_bf_ref[...] = wo_ref[...].astype(jnp.bfloat16)

    for bi, dil in enumerate(B_DILS):
        for res in range(dil if dil > 1 else 0):
            for j in range(B_W // LANES):
                so_ref[bi, j, pl.ds(res, rows // dil, stride=dil), :] = (
                    o_refs[bi][0, res, :, j * LANES:(j + 1) * LANES].astype(jnp.float32))
            sl_ref[bi, pl.ds(res, rows // dil, stride=dil), :] = lse_refs[bi][0, res]

    eh = lax.broadcasted_iota(jnp.int32, (LANES, B_W), 0)
    ej = lax.broadcasted_iota(jnp.int32, (LANES, B_W), 1) // HEAD_DIM
    expand = jnp.where(eh == ej, 1.0, 0.0).astype(jnp.bfloat16)

    for r0 in range(0, rows, OUT_PROJ_SUB_ROWS):
        rs = slice(r0, r0 + OUT_PROJ_SUB_ROWS)
        outs, lses = [], []
        for bi, dil in enumerate(B_DILS):
            if dil == 1:
                outs.append(o_refs[bi][rs, :].astype(jnp.float32))
                lses.append(lse_refs[bi][rs, :])
            else:
                outs.append(jnp.concatenate([so_ref[bi, j, rs, :] for j in range(B_W // LANES)], axis=-1))
                lses.append(sl_ref[bi, rs, :])

        mx = functools.reduce(jnp.maximum, lses)
        es = [jnp.exp2(l - mx) for l in lses]
        inv = 1.0 / functools.reduce(lambda a, b: a + b, es)
        ob = jnp.zeros((OUT_PROJ_SUB_ROWS, B_W), jnp.float32)
        for e, o in zip(es, outs):
            w = e * inv
            wide = jnp.dot(w.astype(jnp.bfloat16), expand, preferred_element_type=jnp.float32)
            ob = ob + wide * o

        attn = jnp.concatenate([oa_ref[rs, :], ob.astype(jnp.bfloat16)], axis=-1)
        x1 = x_ref[rs, :] + jnp.dot(attn, wo_bf_ref[...], preferred_element_type=jnp.float32)
        x1_ref[rs, :] = x1
        hf = x1 * lax.rsqrt(jnp.mean(x1 * x1, axis=-1, keepdims=True) + NORM_EPS) * g_ref[...]
        hf_ref[rs, :] = _pack_halves(hf)
        hf_hi_part = hf.astype(jnp.bfloat16)
        hf_hi_ref[rs, :] = hf_hi_part
        hf_lo_ref[rs, :] = (hf - hf_hi_part.astype(jnp.float32)).astype(jnp.bfloat16)

    nt = (((1,), (1,)), ((), ()))
    lg_hi = lax.dot_general(wr_ref[...], hf_hi_ref[...], nt, preferred_element_type=jnp.float32)
    lg_lo = lax.dot_general(wr_ref[0:N_EXPERTS, :], hf_lo_ref[...], nt,
                            preferred_element_type=jnp.float32)
    logits = lg_hi[0:N_EXPERTS] + lg_hi[N_EXPERTS:] + lg_lo + br_ref[:, 0:1]


    eidx = lax.broadcasted_iota(jnp.int32, (N_EXPERTS, rows), 0)
    work = logits
    vals, sels = [], []
    for k in range(TOP_K):
        mk = jnp.max(work, axis=0, keepdims=True)
        ik = jnp.min(jnp.where(work == mk, eidx, N_EXPERTS), axis=0, keepdims=True)
        sel = eidx == ik
        work = jnp.where(sel, -jnp.inf, work)
        vals.append(mk)
        sels.append(sel)
        topi_ref[k:k + 1, :] = ik
    exps = [jnp.exp(vk - vals[0]) for vk in vals]
    denom = exps[0] + exps[1] + exps[2] + exps[3]
    ginv = 1.0 / denom
    for k in range(TOP_K):
        gate_ref[k:k + 1, :] = exps[k] * ginv

    onehot = jnp.zeros((N_EXPERTS, rows), jnp.float32)
    for sel in sels:
        onehot = onehot + jnp.where(sel, 1.0, 0.0)
    incl = jnp.dot(onehot.astype(jnp.bfloat16), tri_ref[...], preferred_element_type=jnp.float32)
    before = incl - onehot + carry_ref[:, 0:1]
    for k in range(TOP_K):
        rk = jnp.sum(jnp.where(sels[k], before, 0.0), axis=0, keepdims=True)
        rank_ref[k:k + 1, :] = rk.astype(jnp.int32)
    carry = carry_ref[...] + jnp.sum(onehot, axis=1, keepdims=True)
    carry_ref[...] = carry
    cnt_ref[...] = carry.astype(jnp.int32)


def _out_proj_router(oa, outs_b, lses_b, x2, wo_bf, g, wr, br, seq):
    n = x2.shape[0]
    rows = PROJ_ROWS
    steps = seq // rows
    row_spec = lambda w: pl.BlockSpec((rows, w), lambda i: (i, 0))
    full = lambda a: pl.BlockSpec(a.shape, lambda i: (0,) * a.ndim)
    col_spec = pl.BlockSpec((TOP_K, rows), lambda i: (0, i))

    def branch_spec(dil, w):
        if dil == 1:
            return row_spec(w)
        return pl.BlockSpec((1, dil, rows // dil, w), lambda i: (i // steps, 0, i % steps, 0))

    in_specs = ([row_spec(A_Q_W)]
                + [branch_spec(d, B_W) for d in B_DILS]
                + [branch_spec(d, LANES) for d in B_DILS]
                + [row_spec(D_MODEL), full(wo_bf), full(g), full(wr), full(br)])
    return pl.pallas_call(
        _out_proj_router_kernel,
        out_shape=[
            jax.ShapeDtypeStruct((n, D_MODEL), jnp.float32),
            jax.ShapeDtypeStruct((n, HALF_D), jnp.uint32),
            jax.ShapeDtypeStruct((TOP_K, n), jnp.int32),
            jax.ShapeDtypeStruct((TOP_K, n), jnp.float32),
            jax.ShapeDtypeStruct((TOP_K, n), jnp.int32),
            jax.ShapeDtypeStruct((N_EXPERTS, LANES), jnp.int32),
        ],
        grid=(n // rows,),
        in_specs=in_specs,
        out_specs=[row_spec(D_MODEL), row_spec(HALF_D), col_spec, col_spec, col_spec,
                   pl.BlockSpec((N_EXPERTS, LANES), lambda i: (0, 0))],
        scratch_shapes=[pltpu.VMEM((rows, rows), jnp.bfloat16),
                        pltpu.VMEM((N_EXPERTS, LANES), jnp.float32),
                        pltpu.VMEM((len(B_DILS), B_W // LANES, rows, LANES), jnp.float32),
                        pltpu.VMEM((len(B_DILS), rows, LANES), jnp.float32),
                        pltpu.VMEM(wo_bf.shape, jnp.bfloat16),
                        pltpu.VMEM((rows, D_MODEL), jnp.bfloat16),
                        pltpu.VMEM((rows, D_MODEL), jnp.bfloat16)],
        compiler_params=pltpu.CompilerParams(
            dimension_semantics=("arbitrary",), vmem_limit_bytes=VMEM_LIMIT),
        name="out_proj_router",
    )(oa, *outs_b, *lses_b, x2, wo_bf, g, wr, br)


def _mxu_dot(a_bf, w_f32):
    return lax.dot_general(a_bf, w_f32, (((1,), (0,)), ((), ())), preferred_element_type=jnp.float32)


def _moe_kernel(blk_exp_ref, first_ref, slot_ref, next_exp_ref, n_used_ref,
                x_ref, w1_hbm, b1_ref, w2_hbm, b2_ref, y_ref, w1_buf, w2_buf, sem):
    step = pl.program_id(0)

    def weight_copies(expert, slot):
        return (pltpu.make_async_copy(w1_hbm.at[expert], w1_buf.at[slot], sem.at[slot, 0]),
                pltpu.make_async_copy(w2_hbm.at[expert], w2_buf.at[slot], sem.at[slot, 1]))

    @pl.when(step * MOE_STEP_BLOCKS < n_used_ref[0])
    def _():
        @pl.when(step == 0)
        def _():
            for cp in weight_copies(blk_exp_ref[0], slot_ref[0]):
                cp.start()

        for j in range(MOE_STEP_BLOCKS):
            i = step * MOE_STEP_BLOCKS + j

            @pl.when(first_ref[i] == 1)
            def _():
                slot = slot_ref[i]
                for cp in weight_copies(blk_exp_ref[i], slot):
                    cp.wait()

                @pl.when(next_exp_ref[i] >= 0)
                def _():
                    for cp in weight_copies(next_exp_ref[i], (slot + 1) % MOE_WEIGHT_BUFS):
                        cp.start()

        for j in range(MOE_STEP_BLOCKS):
            i = step * MOE_STEP_BLOCKS + j
            slot = slot_ref[i]
            expert = blk_exp_ref[i]
            rs = slice(j * MOE_ROWS, (j + 1) * MOE_ROWS)
            x = jnp.concatenate(_unpack_halves(x_ref[rs, :]), axis=-1).astype(jnp.bfloat16)
            acc = jnp.zeros((MOE_ROWS, D_MODEL), jnp.float32)
            for c in range(D_FF // FF_CHUNK):
                lo = c * FF_CHUNK
                glu = _mxu_dot(x, w1_buf[slot, :, lo:lo + FF_CHUNK]) + b1_ref[expert, :, lo:lo + FF_CHUNK]
                lin = (_mxu_dot(x, w1_buf[slot, :, D_FF + lo:D_FF + lo + FF_CHUNK])
                       + b1_ref[expert, :, D_FF + lo:D_FF + lo + FF_CHUNK])
                glu = jnp.minimum(glu, SWIGLU_LIMIT)
                lin = jnp.clip(lin, -SWIGLU_LIMIT, SWIGLU_LIMIT)
                act = glu * (1.0 / (1.0 + jnp.exp(-SWIGLU_ALPHA * glu))) * (lin + 1.0)
                acc = acc + _mxu_dot(act.astype(jnp.bfloat16), w2_buf[slot, lo:lo + FF_CHUNK, :])
            y_ref[rs, :] = _pack_halves(acc + b2_ref[expert])


def _moe_plan(pends, n_blk):
    g = MOE_ROWS
    blk_row0 = jnp.arange(n_blk, dtype=jnp.int32) * g
    blk_exp = jnp.minimum(jnp.sum(pends[None, :] <= blk_row0[:, None], axis=-1),
                          N_EXPERTS - 1).astype(jnp.int32)
    n_used = (pends[-1] // g).astype(jnp.int32)
    used = blk_row0 < pends[-1]
    prev_exp = jnp.concatenate([jnp.full((1,), -1, jnp.int32), blk_exp[:-1]])
    first = (used & (blk_exp != prev_exp)).astype(jnp.int32)
    slot = ((jnp.cumsum(first) - 1) % MOE_WEIGHT_BUFS).astype(jnp.int32)
    pstarts = jnp.concatenate([jnp.zeros((1,), pends.dtype), pends[:-1]])
    nonempty = pends > pstarts
    experts = jnp.arange(N_EXPERTS, dtype=jnp.int32)
    later = nonempty[None, :] & (experts[None, :] > experts[:, None])
    next_nonempty = jnp.min(jnp.where(later, experts[None, :], N_EXPERTS), axis=-1)
    next_nonempty = jnp.where(next_nonempty == N_EXPERTS, -1, next_nonempty).astype(jnp.int32)
    next_exp = jnp.sum(jnp.where(blk_exp[:, None] == experts[None, :], next_nonempty[None, :], 0),
                       axis=-1).astype(jnp.int32)
    return blk_exp, first, slot, next_exp, n_used.reshape(1)


def _moe_experts(plan, xb, w1, b1, w2, b2):
    n_rows = xb.shape[0]
    n_blk = n_rows // MOE_ROWS

    step_rows = MOE_STEP_BLOCKS * MOE_ROWS

    def blk(i, *p):
        return jnp.minimum(i, (p[-1][0] - 1) // MOE_STEP_BLOCKS)

    grid_spec = pltpu.PrefetchScalarGridSpec(
        num_scalar_prefetch=len(plan),
        grid=(n_blk // MOE_STEP_BLOCKS,),
        in_specs=[
            pl.BlockSpec((step_rows, HALF_D), lambda i, *p: (blk(i, *p), 0)),
            pl.BlockSpec(memory_space=pl.ANY),
            pl.BlockSpec(b1.shape, lambda i, *p: (0, 0, 0)),
            pl.BlockSpec(memory_space=pl.ANY),
            pl.BlockSpec(b2.shape, lambda i, *p: (0, 0, 0)),
        ],
        out_specs=pl.BlockSpec((step_rows, HALF_D), lambda i, *p: (blk(i, *p), 0)),
        scratch_shapes=[pltpu.VMEM((MOE_WEIGHT_BUFS, D_MODEL, 2 * D_FF), jnp.float32),
                        pltpu.VMEM((MOE_WEIGHT_BUFS, D_FF, D_MODEL), jnp.float32),
                        pltpu.SemaphoreType.DMA((MOE_WEIGHT_BUFS, 2))],
    )
    return pl.pallas_call(
        _moe_kernel,
        out_shape=jax.ShapeDtypeStruct((n_rows, HALF_D), jnp.uint32),
        grid_spec=grid_spec,
        compiler_params=pltpu.CompilerParams(
            dimension_semantics=("arbitrary",), vmem_limit_bytes=MOE_VMEM_LIMIT),
        name="moe_experts",
    )(*plan, xb, w1, b1, w2, b2)


def _combine_kernel(x1_ref, yg_ref, gate_ref, o_ref):
    acc_lo = x1_ref[:, :HALF_D]
    acc_hi = x1_ref[:, HALF_D:]
    rows = x1_ref.shape[0]
    gates = jnp.concatenate([gate_ref[...], jnp.zeros((8 - TOP_K, rows), jnp.float32)], axis=0).T
    for k in range(TOP_K):
        lo, hi = _unpack_halves(yg_ref[k])
        gk = gates[:, k:k + 1]
        acc_lo = acc_lo + lo * gk
        acc_hi = acc_hi + hi * gk
    o_ref[:, :HALF_D] = acc_lo
    o_ref[:, HALF_D:] = acc_hi


def _combine(acc, yg, gates_nk, split):
    n = acc.shape[0]
    rows = PROJ_ROWS
    steps = yg.shape[1] // rows
    first = split * steps
    return pl.pallas_call(
        _combine_kernel,
        out_shape=jax.ShapeDtypeStruct((n, D_MODEL), jnp.float32),
        grid=(steps,),
        in_specs=[pl.BlockSpec((rows, D_MODEL), lambda i: (first + i, 0)),
                  pl.BlockSpec((TOP_K, rows, HALF_D), lambda i: (0, i, 0)),
                  pl.BlockSpec((TOP_K, rows), lambda i: (0, first + i))],
        out_specs=pl.BlockSpec((rows, D_MODEL), lambda i: (first + i, 0)),
        input_output_aliases={0: 0},
        compiler_params=pltpu.CompilerParams(
            dimension_semantics=("arbitrary",), vmem_limit_bytes=VMEM_LIMIT),
        name="moe_combine",
    )(acc, yg, gates_nk)


def _sc_worker_id():
    return lax.axis_index("s") * SC_CORES + lax.axis_index("c")


def _sc_dispatch(hf, dest3, n_rows):
    n = hf.shape[0]
    chunks_per_worker = n // SC_CHUNK // SC_WORKERS
    mesh = plsc.VectorSubcoreMesh(core_axis_name="c", subcore_axis_name="s")

    @functools.partial(
        pl.kernel, mesh=mesh,
        out_type=jax.ShapeDtypeStruct((n_rows, HALF_D), hf.dtype),
        scratch_types=[pltpu.VMEM((2, TOP_K, SC_CHUNK), jnp.int32),
                       pltpu.VMEM((2, SC_CHUNK, HALF_D), hf.dtype),
                       pltpu.SemaphoreType.DMA((2,)),
                       pltpu.SemaphoreType.DMA((2,))],
        name="sc_dispatch")
    def run(hf_hbm, dest_hbm, xb_hbm, idx_v, rows_v, load_sem, scatter_sem):
        first = _sc_worker_id() * chunks_per_worker

        def load(j):
            slot = j % 2
            pltpu.sync_copy(dest_hbm.at[first + j], idx_v.at[slot])
            return pltpu.async_copy(hf_hbm.at[pl.ds((first + j) * SC_CHUNK, SC_CHUNK)],
                                    rows_v.at[slot], load_sem.at[slot])

        loads = {0: load(0)}
        scatters = {}
        for j in range(chunks_per_worker):
            slot = j % 2
            loads.pop(j).wait()
            scatters[j] = [pltpu.async_copy(rows_v.at[slot], xb_hbm.at[idx_v.at[slot, k]],
                                            scatter_sem.at[slot]) for k in range(TOP_K)]
            if j >= 1:
                for cp in scatters.pop(j - 1):
                    cp.wait()
            if j + 1 < chunks_per_worker:
                loads[j + 1] = load(j + 1)
        for cp in scatters.pop(chunks_per_worker - 1):
            cp.wait()

    return run(hf, dest3)


def _sc_collect(y, dest3):
    n = dest3.shape[0] * SC_COLLECT_CHUNK
    chunks_per_worker = n // SC_COLLECT_CHUNK // SC_WORKERS
    mesh = plsc.VectorSubcoreMesh(core_axis_name="c", subcore_axis_name="s")

    @functools.partial(
        pl.kernel, mesh=mesh,
        out_type=jax.ShapeDtypeStruct((TOP_K, n, HALF_D), y.dtype),
        scratch_types=[pltpu.VMEM((TOP_K, SC_COLLECT_CHUNK), jnp.int32),
                       pltpu.VMEM((TOP_K, SC_COLLECT_CHUNK, HALF_D), y.dtype),
                       pltpu.SemaphoreType.DMA((TOP_K,)),
                       pltpu.SemaphoreType.DMA((TOP_K,))],
        name="sc_collect")
    def run(y_hbm, dest_hbm, yg_hbm, idx_v, rows_v, gather_sem, write_sem):
        first = _sc_worker_id() * chunks_per_worker

        @pl.loop(0, chunks_per_worker)
        def _(j):
            ch = first + j
            pltpu.sync_copy(dest_hbm.at[ch], idx_v)
            gathers = [pltpu.async_copy(y_hbm.at[idx_v.at[k]], rows_v.at[k], gather_sem.at[k])
                       for k in range(TOP_K)]
            writes = []
            for k in range(TOP_K):
                gathers[k].wait()
                writes.append(pltpu.async_copy(
                    rows_v.at[k], yg_hbm.at[k, pl.ds(ch * SC_COLLECT_CHUNK, SC_COLLECT_CHUNK)],
                    write_sem.at[k]))
            for w in writes:
                w.wait()

    return run(y, dest3)


def _layer(x2, batch, seq, attn_norm_g, w_in, a_q_g, a_k_g, a_sinks, b_q_g, b_k_g, w_out,
           ffn_norm_g, w_router, b_router, w1, b1, w2, b2):
    n = x2.shape[0]
    slopes = _alibi_slopes()
    q_scale = HEAD_DIM ** -0.5 * LOG2E
    reps = MXU_DIM // HEAD_DIM
    gains = jnp.stack([jnp.tile(a_q_g, reps) * q_scale, jnp.tile(a_k_g, reps),
                       jnp.tile(b_q_g, reps) * q_scale, jnp.tile(b_k_g, reps)]).astype(jnp.float32)

    proj = _in_proj(x2, attn_norm_g.reshape(1, -1), w_in, gains, batch, seq)
    qa, ka, va = proj[:3]
    nb = len(B_DILS)
    qbs, kbs, vbs = proj[3:3 + nb], proj[3 + nb:3 + 2 * nb], proj[3 + 2 * nb:]

    bias_a = _bias_tables(slopes[:A_Q_HEADS], A_STACK_HEADS, A_HALF_WINDOW, 1, Q_TILE + 2 * A_HALF_WINDOW)
    sink_col = jnp.repeat(a_sinks.astype(jnp.float32) * LOG2E, Q_TILE).reshape(
        A_Q_HEADS // A_STACK_HEADS, A_STACK_HEADS * Q_TILE, 1)
    as_seqs = lambda a: a.reshape(batch, seq, a.shape[-1])
    out_a = _banded_attention(as_seqs(qa), as_seqs(ka), as_seqs(va), bias_a, half_w=A_HALF_WINDOW,
                              sink=sink_col, name="attn_a")[0].reshape(n, A_Q_W)

    outs_b, lses_b = [], []
    for bi, (window, dil) in enumerate(B_BRANCHES):
        half_w = window // (2 * dil)
        bias_b = _bias_tables(slopes[A_Q_HEADS:], B_STACK_HEADS, half_w, dil, Q_TILE + 2 * half_w)
        L = seq // dil
        to_seqs = lambda a: a.reshape(batch * dil, L, a.shape[-1])
        o, lse = _banded_attention(to_seqs(qbs[bi]), to_seqs(kbs[bi]), to_seqs(vbs[bi]), bias_b,
                                   half_w=half_w, want_lse=True, name=f"attn_b_d{dil}")
        if dil == 1:
            outs_b.append(o.reshape(n, B_W))
            lses_b.append(lse.reshape(n, LANES))
        else:
            outs_b.append(o.reshape(batch, dil, L, B_W))
            lses_b.append(lse.reshape(batch, dil, L, LANES))

    wr_t = w_router.T.astype(jnp.float32)
    wr_hi = wr_t.astype(jnp.bfloat16)
    wr_lo = (wr_t - wr_hi.astype(jnp.float32)).astype(jnp.bfloat16)
    wr = jnp.concatenate([wr_hi, wr_lo], axis=0)
    br = jnp.broadcast_to(b_router.astype(jnp.float32)[:, None], (N_EXPERTS, LANES))
    x1, hf, topi, gates, ranks, counts = _out_proj_router(
        out_a, outs_b, lses_b, x2, w_out, ffn_norm_g.reshape(1, -1), wr, br, seq)

    g = MOE_ROWS
    nk = n * TOP_K
    n_rows = nk + N_EXPERTS * g
    cnt = counts[:, 0]
    pcnt = (cnt + g - 1) // g * g
    pends = jnp.cumsum(pcnt)
    pstarts = pends - pcnt
    experts = jnp.arange(N_EXPERTS, dtype=jnp.int32)
    start_of = jnp.sum(jnp.where(topi[:, :, None] == experts, pstarts, 0), axis=-1)
    dest = (start_of + ranks).astype(jnp.int32)
    plan = _moe_plan(pends, n_rows // g)
    dest3 = dest.reshape(TOP_K, n // SC_CHUNK, SC_CHUNK).transpose(1, 0, 2)

    xb = _sc_dispatch(hf, dest3, n_rows)
    y = _moe_experts(plan, xb, w1, b1[:, None, :], w2, b2[:, None, :])
    per_split = n // COMBINE_SPLITS
    dest_c = dest.reshape(TOP_K, n // SC_COLLECT_CHUNK, SC_COLLECT_CHUNK).transpose(1, 0, 2)
    chunks_per_split = per_split // SC_COLLECT_CHUNK
    out = x1
    for s in range(COMBINE_SPLITS):
        yg = _sc_collect(y, dest_c[s * chunks_per_split:(s + 1) * chunks_per_split])
        out = _combine(out, yg, gates, s)
    return out


def kernel(x, attn_norm_g, w_in, a_q_norm_g, a_k_norm_g, a_sinks, b_q_norm_g, b_k_norm_g, w_out,
           ffn_norm_g, w_router, b_router, w1, b1, w2, b2):
    batch, seq, d = x.shape
    x2 = x.reshape(batch * seq, d)
    for i in range(attn_norm_g.shape[0]):
        x2 = _layer(x2, batch, seq, attn_norm_g[i], w_in[i], a_q_norm_g[i], a_k_norm_g[i],
                    a_sinks[i], b_q_norm_g[i], b_k_norm_g[i], w_out[i], ffn_norm_g[i],
                    w_router[i], b_router[i], w1[i], b1[i], w2[i], b2[i])
    return x2.reshape(batch, seq, d)
```

```python
import functools

import jax
import jax.numpy as jnp
import numpy as np
from jax import lax
from jax.experimental import pallas as pl
from jax.experimental.pallas import tpu as pltpu
from jax.experimental.pallas import tpu_sc as plsc

D_MODEL = 1024
HALF_D = D_MODEL // 2
HEAD_DIM = 64
LANES = 128
MXU_DIM = 256
A_Q_HEADS = 8
A_KV_HEADS = 2
B_HEADS = 8
A_HALF_WINDOW = 128
B_BRANCHES = ((128, 1), (512, 4), (2048, 16))
B_DILS = tuple(d for _, d in B_BRANCHES)
RESIDUE_STRIDE = 4
N_ALIBI_HEADS = 16
A_Q_W = A_Q_HEADS * HEAD_DIM
A_KV_W = A_KV_HEADS * HEAD_DIM
B_W = B_HEADS * HEAD_DIM
N_EXPERTS = 32
TOP_K = 4
D_FF = 1024
SWIGLU_ALPHA = 1.702
SWIGLU_LIMIT = 7.0
NORM_EPS = 1e-5
MASK_VALUE = -1e30
LOG2E = 1.4426950408889634

Q_TILE = 128
ATTN_STEP_ROWS = 512
A_STACK_HEADS = 4
B_STACK_HEADS = 2
PROJ_ROWS = 512
MOE_ROWS = 512
FF_CHUNK = 512
MOE_STEP_BLOCKS = 2
MOE_WEIGHT_BUFS = 3
VMEM_LIMIT = 48 * 1024 * 1024
MOE_VMEM_LIMIT = 58 * 1024 * 1024
SC_CORES = 2
SC_SUBCORES = 16
SC_WORKERS = SC_CORES * SC_SUBCORES
SC_CHUNK = 64
SC_COLLECT_CHUNK = 32
COMBINE_SPLITS = 8


def _pack_halves(v):
    lo = v[:, :HALF_D].astype(jnp.bfloat16).astype(jnp.float32)
    hi = v[:, HALF_D:].astype(jnp.bfloat16).astype(jnp.float32)
    return (pltpu.bitcast(lo, jnp.uint32) >> 16) | pltpu.bitcast(hi, jnp.uint32)


def _unpack_halves(w):
    lo = pltpu.bitcast(w << 16, jnp.float32)
    hi = pltpu.bitcast(w & jnp.uint32(0xFFFF0000), jnp.float32)
    return lo, hi


def _alibi_slopes():
    return np.exp2(-8.0 * np.arange(1, N_ALIBI_HEADS + 1, dtype=np.float32) / N_ALIBI_HEADS).astype(np.float32)


def _bias_tables(head_slopes, heads_per_group, half_w, dist_scale, tk):
    i = np.arange(Q_TILE)[:, None]
    j = np.arange(tk)[None, :]
    tabs = []
    for shift in (0, half_w, tk - Q_TILE):
        dist = np.abs(j - shift - i)
        valid = dist <= half_w
        per_head = []
        for sl in head_slopes:
            b = (-np.float64(sl) * LOG2E * (dist * dist_scale)).astype(np.float32)
            per_head.append(np.where(valid, b, np.float32(MASK_VALUE)).astype(np.float32))
        t = np.stack(per_head).reshape(-1, heads_per_group * Q_TILE, tk)
        tabs.append(t)
    return jnp.asarray(np.stack(tabs))


def _in_proj_kernel(x_ref, g_ref, w_hbm, gains_ref, qa_ref, ka_ref, va_ref, *rest):
    b_refs, (scr_ref, scr2_ref, w_ref, stage_ref, sem) = rest[:-5], rest[-5:]

    @pl.when(pl.program_id(0) == 0)
    def _():
        width = stage_ref.shape[1]
        for c0 in range(0, w_ref.shape[1], width):
            cp = pltpu.make_async_copy(w_hbm.at[:, c0:c0 + width], stage_ref, sem)
            cp.start()
            cp.wait()
            w_ref[:, c0:c0 + width] = stage_ref[...].astype(jnp.bfloat16)

    x = x_ref[...]
    xn = x * lax.rsqrt(jnp.mean(x * x, axis=-1, keepdims=True) + NORM_EPS) * g_ref[...]
    xn = xn.astype(jnp.bfloat16)
    r = lax.broadcasted_iota(jnp.int32, (MXU_DIM, MXU_DIM), 0) // HEAD_DIM
    c = lax.broadcasted_iota(jnp.int32, (MXU_DIM, MXU_DIM), 1) // HEAD_DIM
    blockdiag = jnp.where(r == c, 1.0, 0.0).astype(jnp.bfloat16)

    def head_rms(sec, gain_row):
        width = sec.shape[1]
        parts = []
        step = min(width, MXU_DIM)
        for j in range(width // step):
            p = sec[:, j * step:(j + 1) * step]
            ss = jnp.dot((p * p).astype(jnp.bfloat16), blockdiag[:step, :step],
                         preferred_element_type=jnp.float32)
            parts.append(p * lax.rsqrt(ss * (1.0 / HEAD_DIM) + NORM_EPS)
                         * gains_ref[gain_row:gain_row + 1, :step])
        return parts

    def project(col0, width, gain_row):
        sec = jnp.dot(xn, w_ref[:, col0:col0 + width], preferred_element_type=jnp.float32)
        return [sec] if gain_row is None else head_rms(sec, gain_row)

    def store(out_ref, parts):
        w = parts[0].shape[1]
        for j, p in enumerate(parts):
            out_ref[:, j * w:(j + 1) * w] = p.astype(out_ref.dtype)

    def per_kv_head(p):
        lane = lax.broadcasted_iota(jnp.int32, p.shape, 1)
        swapped = pltpu.roll(p, HEAD_DIM, axis=1)
        low = lane < HEAD_DIM
        return [jnp.where(low, p, swapped), jnp.where(low, swapped, p)]

    store(qa_ref, project(0, A_Q_W, 0))
    kva = project(A_Q_W, 2 * A_KV_W, None)[0]
    store(ka_ref, per_kv_head(head_rms(kva[:, :A_KV_W], 1)[0]))
    store(va_ref, per_kv_head(kva[:, A_KV_W:]))

    rows = x_ref.shape[0]
    col0 = A_Q_W + 2 * A_KV_W
    for t, gain_row in enumerate((2, 3, None)):
        parts = project(col0 + t * B_W, B_W, gain_row)
        sec = jnp.concatenate(parts, axis=-1) if len(parts) > 1 else parts[0]
        for j in range(B_W // LANES):
            scr_ref[j] = sec[:, j * LANES:(j + 1) * LANES]
        n_lane_chunks = B_W // LANES
        prev_dil = 1
        for bi, dil in enumerate(B_DILS):
            out_ref = b_refs[t * len(B_DILS) + bi]
            if dil == 1:
                out_ref[...] = sec.astype(out_ref.dtype)
                continue
            assert dil == prev_dil * RESIDUE_STRIDE
            last = dil == B_DILS[-1]
            for res in range(dil):
                r_prev, r_sub = res % prev_dil, res // prev_dil
                for j in range(n_lane_chunks):
                    if prev_dil == 1:
                        v = scr_ref[j, pl.ds(r_sub, rows // dil, stride=RESIDUE_STRIDE), :]
                    else:
                        v = scr2_ref[r_prev * n_lane_chunks + j,
                                     pl.ds(r_sub, rows // dil, stride=RESIDUE_STRIDE), :]
                    out_ref[0, res, :, j * LANES:(j + 1) * LANES] = v.astype(out_ref.dtype)
                    if not last:
                        scr2_ref[res * n_lane_chunks + j] = v
            prev_dil = dil


def _in_proj(x2, g, w_in, gains, batch, seq):
    n = x2.shape[0]
    rows = PROJ_ROWS
    steps = seq // rows
    a_widths = (A_Q_W, 2 * A_KV_W, 2 * A_KV_W)
    out_shape = [jax.ShapeDtypeStruct((n, w), jnp.bfloat16) for w in a_widths]
    out_specs = [pl.BlockSpec((rows, w), lambda i: (i, 0)) for w in a_widths]
    for _ in range(3):
        for dil in B_DILS:
            if dil == 1:
                out_shape.append(jax.ShapeDtypeStruct((n, B_W), jnp.bfloat16))
                out_specs.append(pl.BlockSpec((rows, B_W), lambda i: (i, 0)))
            else:
                out_shape.append(jax.ShapeDtypeStruct((batch, dil, seq // dil, B_W), jnp.bfloat16))
                out_specs.append(pl.BlockSpec((1, dil, rows // dil, B_W),
                                              lambda i: (i // steps, 0, i % steps, 0)))
    return pl.pallas_call(
        _in_proj_kernel,
        out_shape=out_shape,
        grid=(n // rows,),
        in_specs=[
            pl.BlockSpec((rows, D_MODEL), lambda i: (i, 0)),
            pl.BlockSpec((1, D_MODEL), lambda i: (0, 0)),
            pl.BlockSpec(memory_space=pl.ANY),
            pl.BlockSpec(gains.shape, lambda i: (0, 0)),
        ],
        out_specs=out_specs,
        scratch_shapes=[pltpu.VMEM((B_W // LANES, rows, LANES), jnp.float32),
                        pltpu.VMEM((RESIDUE_STRIDE * B_W // LANES, rows // RESIDUE_STRIDE, LANES),
                                   jnp.float32),
                        pltpu.VMEM(w_in.shape, jnp.bfloat16),
                        pltpu.VMEM((w_in.shape[0], w_in.shape[1] // 3), jnp.float32),
                        pltpu.SemaphoreType.DMA],
        compiler_params=pltpu.CompilerParams(
            dimension_semantics=("arbitrary",), vmem_limit_bytes=VMEM_LIMIT),
        name="in_proj",
    )(x2, g, w_in, gains)


def _attn_kernel(*refs, n_chunks, kv_chunks, heads_per_stack, tk, half_w, seq_len, rows, has_sink,
                 want_lse):
    it = iter(refs)
    q_ref, k_ref, v_ref, bias_ref = next(it), next(it), next(it), next(it)
    sink_ref = next(it) if has_sink else None
    o_ref = next(it)
    lse_ref = next(it) if want_lse else None

    n_tiles = seq_len // Q_TILE
    tiles_per_step = rows // Q_TILE
    chunks_per_group = n_chunks // kv_chunks
    assert (2 * chunks_per_group) % heads_per_stack == 0
    step = pl.program_id(1)
    lane = lax.broadcasted_iota(jnp.int32, (Q_TILE, LANES), 1)
    low_half = lane < HEAD_DIM
    ones = jnp.ones((tk, LANES), jnp.bfloat16)

    for sq, t in [(a, b) for a in range(q_ref.shape[0]) for b in range(tiles_per_step)]:
        tile = step * tiles_per_step + t
        q0 = tile * Q_TILE
        kv_rows = k_ref.shape[1]
        kv_row0 = jnp.clip(step * rows - half_w, 0, seq_len - kv_rows)
        start = pl.multiple_of(jnp.clip(q0 - half_w, 0, seq_len - tk) - kv_row0, HEAD_DIM)
        variant = jnp.where(tile == 0, 0, jnp.where(tile == n_tiles - 1, 2, 1))
        r0 = t * Q_TILE
        lse_tile = jnp.zeros((Q_TILE, LANES), jnp.float32)
        head_o, head_lse = {}, {}
        for g in range(2 * n_chunks // heads_per_stack):
            heads = range(g * heads_per_stack, (g + 1) * heads_per_stack)
            kv = (heads[0] // 2) // chunks_per_group
            kc = k_ref[sq, pl.ds(start, tk), kv * LANES:(kv + 1) * LANES]
            vc = v_ref[sq, pl.ds(start, tk), kv * LANES:(kv + 1) * LANES]
            v_aug = jnp.concatenate([vc, ones], axis=1)
            q_parts = []
            for h in heads:
                c = h // 2
                q2 = q_ref[sq, r0:r0 + Q_TILE, c * LANES:(c + 1) * LANES]
                keep = low_half if h % 2 == 0 else ~low_half
                q_parts.append(jnp.where(keep, q2, jnp.zeros_like(q2)))
            qs = q_parts[0] if len(q_parts) == 1 else jnp.concatenate(q_parts, axis=0)
            s = lax.dot_general(qs, kc, (((1,), (1,)), ((), ())),
                                preferred_element_type=jnp.float32)
            s = s + bias_ref[variant, g]
            m = jnp.max(s, axis=-1, keepdims=True)
            if has_sink:
                m = jnp.maximum(m, sink_ref[g])
            p = jnp.exp2(s - m)
            ov = jnp.dot(p.astype(jnp.bfloat16), v_aug, preferred_element_type=jnp.float32)
            o, l = ov[:, :LANES], ov[:, LANES:]
            if has_sink:
                l = l + jnp.exp2(sink_ref[g] - m)
            o = o * (1.0 / l)
            if want_lse:
                lse = m + jnp.log(l) * LOG2E
            for idx, h in enumerate(heads):
                head_o[h] = o[idx * Q_TILE:(idx + 1) * Q_TILE]
                if want_lse:
                    head_lse[h] = lse[idx * Q_TILE:(idx + 1) * Q_TILE]
                if h % 2 == 1:
                    c = h // 2
                    o2 = jnp.where(low_half, head_o.pop(h - 1), head_o.pop(h))
                    o_ref[sq, r0:r0 + Q_TILE, c * LANES:(c + 1) * LANES] = o2.astype(o_ref.dtype)
                    if want_lse:
                        lse_tile = jnp.where(lane == h - 1, head_lse.pop(h - 1),
                                             jnp.where(lane == h, head_lse.pop(h), lse_tile))
        if want_lse:
            lse_ref[sq, r0:r0 + Q_TILE, :] = lse_tile


def _banded_attention(q, k, v, bias, *, half_w, sink=None, want_lse=False, name):
    n_seq, L, qw = q.shape
    kw = k.shape[2]
    tk = Q_TILE + 2 * half_w
    rows = min(ATTN_STEP_ROWS, L)
    seqs = ATTN_STEP_ROWS // rows
    kv_rows = min(L, rows + 2 * half_w)

    def kv_index(s, i):
        row0 = pl.multiple_of(jnp.clip(i * rows - half_w, 0, L - kv_rows), HEAD_DIM)
        return s * seqs, row0, 0

    args = [q, k, v, bias]
    in_specs = [
        pl.BlockSpec((seqs, rows, qw), lambda s, i: (s, i, 0)),
        pl.BlockSpec((pl.Element(seqs), pl.Element(kv_rows), pl.Element(kw)), kv_index),
        pl.BlockSpec((pl.Element(seqs), pl.Element(kv_rows), pl.Element(kw)), kv_index),
        pl.BlockSpec(bias.shape, lambda s, i: (0, 0, 0, 0)),
    ]
    if sink is not None:
        args.append(sink)
        in_specs.append(pl.BlockSpec(sink.shape, lambda s, i: (0, 0, 0)))
    out_shape = [jax.ShapeDtypeStruct((n_seq, L, qw), jnp.bfloat16)]
    out_specs = [pl.BlockSpec((seqs, rows, qw), lambda s, i: (s, i, 0))]
    if want_lse:
        out_shape.append(jax.ShapeDtypeStruct((n_seq, L, LANES), jnp.float32))
        out_specs.append(pl.BlockSpec((seqs, rows, LANES), lambda s, i: (s, i, 0)))

    kern = functools.partial(
        _attn_kernel, n_chunks=qw // LANES, kv_chunks=kw // LANES,
        heads_per_stack=bias.shape[2] // Q_TILE, tk=tk, half_w=half_w, seq_len=L,
        rows=rows, has_sink=sink is not None, want_lse=want_lse)
    return pl.pallas_call(
        kern,
        out_shape=out_shape,
        grid=(n_seq // seqs, L // rows),
        in_specs=in_specs,
        out_specs=out_specs,
        compiler_params=pltpu.CompilerParams(
            dimension_semantics=("arbitrary", "arbitrary"), vmem_limit_bytes=VMEM_LIMIT),
        name=name,
    )(*args)


def _out_proj_router_kernel(*refs):
    nb = len(B_DILS)
    oa_ref = refs[0]
    o_refs = refs[1:1 + nb]
    lse_refs = refs[1 + nb:1 + 2 * nb]
    (x_ref, wo_ref, g_ref, wr_ref, br_ref,
     x1_ref, hf_ref, topi_ref, gate_ref, rank_ref, cnt_ref,
     tri_ref, carry_ref, so_ref, sl_ref, wo_bf_ref) = refs[1 + 2 * nb:]
    i = pl.program_id(0)
    rows = x_ref.shape[0]

    @pl.when(i == 0)
    def _():
        a = lax.broadcasted_iota(jnp.int32, (rows, rows), 0)
        b = lax.broadcasted_iota(jnp.int32, (rows, rows), 1)
        tri_ref[...] = jnp.where(a <= b, 1.0, 0.0).astype(jnp.bfloat16)
        carry_ref[...] = jnp.zeros_like(carry_ref)
        wo_bf_ref[...] = wo_ref[...].astype(jnp.bfloat16)

    outs, lses = [], []
    for bi, dil in enumerate(B_DILS):
        if dil == 1:
            outs.append(o_refs[bi][...].astype(jnp.float32))
            lses.append(lse_refs[bi][...])
        else:
            for res in range(dil):
                for j in range(B_W // LANES):
                    so_ref[bi, j, pl.ds(res, rows // dil, stride=dil), :] = (
                        o_refs[bi][0, res, :, j * LANES:(j + 1) * LANES].astype(jnp.float32))
                sl_ref[bi, pl.ds(res, rows // dil, stride=dil), :] = lse_refs[bi][0, res]
            outs.append(jnp.concatenate([so_ref[bi, j] for j in range(B_W // LANES)], axis=-1))
            lses.append(sl_ref[bi])

    mx = functools.reduce(jnp.maximum, lses)
    es = [jnp.exp2(l - mx) for l in lses]
    inv = 1.0 / functools.reduce(lambda a, b: a + b, es)
    eh = lax.broadcasted_iota(jnp.int32, (LANES, B_W), 0)
    ej = lax.broadcasted_iota(jnp.int32, (LANES, B_W), 1) // HEAD_DIM
    expand = jnp.where(eh == ej, 1.0, 0.0).astype(jnp.bfloat16)
    ob = jnp.zeros((rows, B_W), jnp.float32)
    for e, o in zip(es, outs):
        w = e * inv
        wide = jnp.dot(w.astype(jnp.bfloat16), expand, preferred_element_type=jnp.float32)
        ob = ob + wide * o

    attn = jnp.concatenate([oa_ref[...], ob.astype(jnp.bfloat16)], axis=-1)
    x1 = x_ref[...] + jnp.dot(attn, wo_bf_ref[...], preferred_element_type=jnp.float32)
    x1_ref[...] = x1
    hf = x1 * lax.rsqrt(jnp.mean(x1 * x1, axis=-1, keepdims=True) + NORM_EPS) * g_ref[...]
    hf_hi = hf.astype(jnp.bfloat16)
    hf_ref[...] = _pack_halves(hf)
    hf_lo = (hf - hf_hi.astype(jnp.float32)).astype(jnp.bfloat16)

    nt = (((1,), (1,)), ((), ()))
    lg_hi = lax.dot_general(wr_ref[...], hf_hi, nt, preferred_element_type=jnp.float32)
    lg_lo = lax.dot_general(wr_ref[0:N_EXPERTS, :], hf_lo, nt, preferred_element_type=jnp.float32)
    logits = lg_hi[0:N_EXPERTS] + lg_hi[N_EXPERTS:] + lg_lo + br_ref[:, 0:1]


    eidx = lax.broadcasted_iota(jnp.int32, (N_EXPERTS, rows), 0)
    work = logits
    vals, sels = [], []
    for k in range(TOP_K):
        mk = jnp.max(work, axis=0, keepdims=True)
        ik = jnp.min(jnp.where(work == mk, eidx, N_EXPERTS), axis=0, keepdims=True)
        sel = eidx == ik
        work = jnp.where(sel, -jnp.inf, work)
        vals.append(mk)
        sels.append(sel)
        topi_ref[k:k + 1, :] = ik
    exps = [jnp.exp(vk - vals[0]) for vk in vals]
    denom = exps[0] + exps[1] + exps[2] + exps[3]
    ginv = 1.0 / denom
    for k in range(TOP_K):
        gate_ref[k:k + 1, :] = exps[k] * ginv

    onehot = jnp.zeros((N_EXPERTS, rows), jnp.float32)
    for sel in sels:
        onehot = onehot + jnp.where(sel, 1.0, 0.0)
    incl = jnp.dot(onehot.astype(jnp.bfloat16), tri_ref[...], preferred_element_type=jnp.float32)
    before = incl - onehot + carry_ref[:, 0:1]
    for k in range(TOP_K):
        rk = jnp.sum(jnp.where(sels[k], before, 0.0), axis=0, keepdims=True)
        rank_ref[k:k + 1, :] = rk.astype(jnp.int32)
    carry = carry_ref[...] + jnp.sum(onehot, axis=1, keepdims=True)
    carry_ref[...] = carry
    cnt_ref[...] = carry.astype(jnp.int32)


def _out_proj_router(oa, outs_b, lses_b, x2, wo_bf, g, wr, br, seq):
    n = x2.shape[0]
    rows = PROJ_ROWS
    steps = seq // rows
    row_spec = lambda w: pl.BlockSpec((rows, w), lambda i: (i, 0))
    full = lambda a: pl.BlockSpec(a.shape, lambda i: (0,) * a.ndim)
    col_spec = pl.BlockSpec((TOP_K, rows), lambda i: (0, i))

    def branch_spec(dil, w):
        if dil == 1:
            return row_spec(w)
        return pl.BlockSpec((1, dil, rows // dil, w), lambda i: (i // steps, 0, i % steps, 0))

    in_specs = ([row_spec(A_Q_W)]
                + [branch_spec(d, B_W) for d in B_DILS]
                + [branch_spec(d, LANES) for d in B_DILS]
                + [row_spec(D_MODEL), full(wo_bf), full(g), full(wr), full(br)])
    return pl.pallas_call(
        _out_proj_router_kernel,
        out_shape=[
            jax.ShapeDtypeStruct((n, D_MODEL), jnp.float32),
            jax.ShapeDtypeStruct((n, HALF_D), jnp.uint32),
            jax.ShapeDtypeStruct((TOP_K, n), jnp.int32),
            jax.ShapeDtypeStruct((TOP_K, n), jnp.float32),
            jax.ShapeDtypeStruct((TOP_K, n), jnp.int32),
            jax.ShapeDtypeStruct((N_EXPERTS, LANES), jnp.int32),
        ],
        grid=(n // rows,),
        in_specs=in_specs,
        out_specs=[row_spec(D_MODEL), row_spec(HALF_D), col_spec, col_spec, col_spec,
                   pl.BlockSpec((N_EXPERTS, LANES), lambda i: (0, 0))],
        scratch_shapes=[pltpu.VMEM((rows, rows), jnp.bfloat16),
                        pltpu.VMEM((N_EXPERTS, LANES), jnp.float32),
                        pltpu.VMEM((len(B_DILS), B_W // LANES, rows, LANES), jnp.float32),
                        pltpu.VMEM((len(B_DILS), rows, LANES), jnp.float32),
                        pltpu.VMEM(wo_bf.shape, jnp.bfloat16)],
        compiler_params=pltpu.CompilerParams(
            dimension_semantics=("arbitrary",), vmem_limit_bytes=VMEM_LIMIT),
        name="out_proj_router",
    )(oa, *outs_b, *lses_b, x2, wo_bf, g, wr, br)


def _mxu_dot(a_bf, w_f32):
    return lax.dot_general(a_bf, w_f32, (((1,), (0,)), ((), ())), preferred_element_type=jnp.float32)


def _moe_kernel(blk_exp_ref, first_ref, slot_ref, next_exp_ref, n_used_ref,
                x_ref, w1_hbm, b1_ref, w2_hbm, b2_ref, y_ref, w1_buf, w2_buf, sem):
    step = pl.program_id(0)

    def weight_copies(expert, slot):
        return (pltpu.make_async_copy(w1_hbm.at[expert], w1_buf.at[slot], sem.at[slot, 0]),
                pltpu.make_async_copy(w2_hbm.at[expert], w2_buf.at[slot], sem.at[slot, 1]))

    @pl.when(step * MOE_STEP_BLOCKS < n_used_ref[0])
    def _():
        @pl.when(step == 0)
        def _():
            for cp in weight_copies(blk_exp_ref[0], slot_ref[0]):
                cp.start()

        for j in range(MOE_STEP_BLOCKS):
            i = step * MOE_STEP_BLOCKS + j

            @pl.when(first_ref[i] == 1)
            def _():
                slot = slot_ref[i]
                for cp in weight_copies(blk_exp_ref[i], slot):
                    cp.wait()

                @pl.when(next_exp_ref[i] >= 0)
                def _():
                    for cp in weight_copies(next_exp_ref[i], (slot + 1) % MOE_WEIGHT_BUFS):
                        cp.start()

        for j in range(MOE_STEP_BLOCKS):
            i = step * MOE_STEP_BLOCKS + j
            slot = slot_ref[i]
            expert = blk_exp_ref[i]
            rs = slice(j * MOE_ROWS, (j + 1) * MOE_ROWS)
            x = jnp.concatenate(_unpack_halves(x_ref[rs, :]), axis=-1).astype(jnp.bfloat16)
            acc = jnp.zeros((MOE_ROWS, D_MODEL), jnp.float32)
            for c in range(D_FF // FF_CHUNK):
                lo = c * FF_CHUNK
                glu = _mxu_dot(x, w1_buf[slot, :, lo:lo + FF_CHUNK]) + b1_ref[expert, :, lo:lo + FF_CHUNK]
                lin = (_mxu_dot(x, w1_buf[slot, :, D_FF + lo:D_FF + lo + FF_CHUNK])
                       + b1_ref[expert, :, D_FF + lo:D_FF + lo + FF_CHUNK])
                glu = jnp.minimum(glu, SWIGLU_LIMIT)
                lin = jnp.clip(lin, -SWIGLU_LIMIT, SWIGLU_LIMIT)
                act = glu * (1.0 / (1.0 + jnp.exp(-SWIGLU_ALPHA * glu))) * (lin + 1.0)
                acc = acc + _mxu_dot(act.astype(jnp.bfloat16), w2_buf[slot, lo:lo + FF_CHUNK, :])
            y_ref[rs, :] = _pack_halves(acc + b2_ref[expert])


def _moe_plan(pends, n_blk):
    g = MOE_ROWS
    blk_row0 = jnp.arange(n_blk, dtype=jnp.int32) * g
    blk_exp = jnp.minimum(jnp.sum(pends[None, :] <= blk_row0[:, None], axis=-1),
                          N_EXPERTS - 1).astype(jnp.int32)
    n_used = (pends[-1] // g).astype(jnp.int32)
    used = blk_row0 < pends[-1]
    prev_exp = jnp.concatenate([jnp.full((1,), -1, jnp.int32), blk_exp[:-1]])
    first = (used & (blk_exp != prev_exp)).astype(jnp.int32)
    slot = ((jnp.cumsum(first) - 1) % MOE_WEIGHT_BUFS).astype(jnp.int32)
    pstarts = jnp.concatenate([jnp.zeros((1,), pends.dtype), pends[:-1]])
    nonempty = pends > pstarts
    experts = jnp.arange(N_EXPERTS, dtype=jnp.int32)
    later = nonempty[None, :] & (experts[None, :] > experts[:, None])
    next_nonempty = jnp.min(jnp.where(later, experts[None, :], N_EXPERTS), axis=-1)
    next_nonempty = jnp.where(next_nonempty == N_EXPERTS, -1, next_nonempty).astype(jnp.int32)
    next_exp = jnp.sum(jnp.where(blk_exp[:, None] == experts[None, :], next_nonempty[None, :], 0),
                       axis=-1).astype(jnp.int32)
    return blk_exp, first, slot, next_exp, n_used.reshape(1)


def _moe_experts(plan, xb, w1, b1, w2, b2):
    n_rows = xb.shape[0]
    n_blk = n_rows // MOE_ROWS

    step_rows = MOE_STEP_BLOCKS * MOE_ROWS

    def blk(i, *p):
        return jnp.minimum(i, (p[-1][0] - 1) // MOE_STEP_BLOCKS)

    grid_spec = pltpu.PrefetchScalarGridSpec(
        num_scalar_prefetch=len(plan),
        grid=(n_blk // MOE_STEP_BLOCKS,),
        in_specs=[
            pl.BlockSpec((step_rows, HALF_D), lambda i, *p: (blk(i, *p), 0)),
            pl.BlockSpec(memory_space=pl.ANY),
            pl.BlockSpec(b1.shape, lambda i, *p: (0, 0, 0)),
            pl.BlockSpec(memory_space=pl.ANY),
            pl.BlockSpec(b2.shape, lambda i, *p: (0, 0, 0)),
        ],
        out_specs=pl.BlockSpec((step_rows, HALF_D), lambda i, *p: (blk(i, *p), 0)),
        scratch_shapes=[pltpu.VMEM((MOE_WEIGHT_BUFS, D_MODEL, 2 * D_FF), jnp.float32),
                        pltpu.VMEM((MOE_WEIGHT_BUFS, D_FF, D_MODEL), jnp.float32),
                        pltpu.SemaphoreType.DMA((MOE_WEIGHT_BUFS, 2))],
    )
    return pl.pallas_call(
        _moe_kernel,
        out_shape=jax.ShapeDtypeStruct((n_rows, HALF_D), jnp.uint32),
        grid_spec=grid_spec,
        compiler_params=pltpu.CompilerParams(
            dimension_semantics=("arbitrary",), vmem_limit_bytes=MOE_VMEM_LIMIT),
        name="moe_experts",
    )(*plan, xb, w1, b1, w2, b2)


def _combine_kernel(x1_ref, yg_ref, gate_ref, o_ref):
    acc_lo = x1_ref[:, :HALF_D]
    acc_hi = x1_ref[:, HALF_D:]
    rows = x1_ref.shape[0]
    gates = jnp.concatenate([gate_ref[...], jnp.zeros((8 - TOP_K, rows), jnp.float32)], axis=0).T
    for k in range(TOP_K):
        lo, hi = _unpack_halves(yg_ref[k])
        gk = gates[:, k:k + 1]
        acc_lo = acc_lo + lo * gk
        acc_hi = acc_hi + hi * gk
    o_ref[:, :HALF_D] = acc_lo
    o_ref[:, HALF_D:] = acc_hi


def _combine(acc, yg, gates_nk, split):
    n = acc.shape[0]
    rows = PROJ_ROWS
    steps = yg.shape[1] // rows
    first = split * steps
    return pl.pallas_call(
        _combine_kernel,
        out_shape=jax.ShapeDtypeStruct((n, D_MODEL), jnp.float32),
        grid=(steps,),
        in_specs=[pl.BlockSpec((rows, D_MODEL), lambda i: (first + i, 0)),
                  pl.BlockSpec((TOP_K, rows, HALF_D), lambda i: (0, i, 0)),
                  pl.BlockSpec((TOP_K, rows), lambda i: (0, first + i))],
        out_specs=pl.BlockSpec((rows, D_MODEL), lambda i: (first + i, 0)),
        input_output_aliases={0: 0},
        compiler_params=pltpu.CompilerParams(
            dimension_semantics=("arbitrary",), vmem_limit_bytes=VMEM_LIMIT),
        name="moe_combine",
    )(acc, yg, gates_nk)


def _sc_worker_id():
    return lax.axis_index("s") * SC_CORES + lax.axis_index("c")


def _sc_dispatch(hf, dest3, n_rows):
    n = hf.shape[0]
    chunks_per_worker = n // SC_CHUNK // SC_WORKERS
    mesh = plsc.VectorSubcoreMesh(core_axis_name="c", subcore_axis_name="s")

    @functools.partial(
        pl.kernel, mesh=mesh,
        out_type=jax.ShapeDtypeStruct((n_rows, HALF_D), hf.dtype),
        scratch_types=[pltpu.VMEM((2, TOP_K, SC_CHUNK), jnp.int32),
                       pltpu.VMEM((2, SC_CHUNK, HALF_D), hf.dtype),
                       pltpu.SemaphoreType.DMA((2,)),
                       pltpu.SemaphoreType.DMA((2,))],
        name="sc_dispatch")
    def run(hf_hbm, dest_hbm, xb_hbm, idx_v, rows_v, load_sem, scatter_sem):
        first = _sc_worker_id() * chunks_per_worker

        def load(j):
            slot = j % 2
            pltpu.sync_copy(dest_hbm.at[first + j], idx_v.at[slot])
            return pltpu.async_copy(hf_hbm.at[pl.ds((first + j) * SC_CHUNK, SC_CHUNK)],
                                    rows_v.at[slot], load_sem.at[slot])

        loads = {0: load(0)}
        scatters = {}
        for j in range(chunks_per_worker):
            slot = j % 2
            loads.pop(j).wait()
            scatters[j] = [pltpu.async_copy(rows_v.at[slot], xb_hbm.at[idx_v.at[slot, k]],
                                            scatter_sem.at[slot]) for k in range(TOP_K)]
            if j >= 1:
                for cp in scatters.pop(j - 1):
                    cp.wait()
            if j + 1 < chunks_per_worker:
                loads[j + 1] = load(j + 1)
        for cp in scatters.pop(chunks_per_worker - 1):
            cp.wait()

    return run(hf, dest3)


def _sc_collect(y, dest3):
    n = dest3.shape[0] * SC_COLLECT_CHUNK
    chunks_per_worker = n // SC_COLLECT_CHUNK // SC_WORKERS
    mesh = plsc.VectorSubcoreMesh(core_axis_name="c", subcore_axis_name="s")

    @functools.partial(
        pl.kernel, mesh=mesh,
        out_type=jax.ShapeDtypeStruct((TOP_K, n, HALF_D), y.dtype),
        scratch_types=[pltpu.VMEM((TOP_K, SC_COLLECT_CHUNK), jnp.int32),
                       pltpu.VMEM((TOP_K, SC_COLLECT_CHUNK, HALF_D), y.dtype),
                       pltpu.SemaphoreType.DMA((TOP_K,)),
                       pltpu.SemaphoreType.DMA((TOP_K,))],
        name="sc_collect")
    def run(y_hbm, dest_hbm, yg_hbm, idx_v, rows_v, gather_sem, write_sem):
        first = _sc_worker_id() * chunks_per_worker

        @pl.loop(0, chunks_per_worker)
        def _(j):
            ch = first + j
            pltpu.sync_copy(dest_hbm.at[ch], idx_v)
            gathers = [pltpu.async_copy(y_hbm.at[idx_v.at[k]], rows_v.at[k], gather_sem.at[k])
                       for k in range(TOP_K)]
            writes = []
            for k in range(TOP_K):
                gathers[k].wait()
                writes.append(pltpu.async_copy(
                    rows_v.at[k], yg_hbm.at[k, pl.ds(ch * SC_COLLECT_CHUNK, SC_COLLECT_CHUNK)],
                    write_sem.at[k]))
            for w in writes:
                w.wait()

    return run(y, dest3)


def _layer(x2, batch, seq, attn_norm_g, w_in, a_q_g, a_k_g, a_sinks, b_q_g, b_k_g, w_out,
           ffn_norm_g, w_router, b_router, w1, b1, w2, b2):
    n = x2.shape[0]
    slopes = _alibi_slopes()
    q_scale = HEAD_DIM ** -0.5 * LOG2E
    reps = MXU_DIM // HEAD_DIM
    gains = jnp.stack([jnp.tile(a_q_g, reps) * q_scale, jnp.tile(a_k_g, reps),
                       jnp.tile(b_q_g, reps) * q_scale, jnp.tile(b_k_g, reps)]).astype(jnp.float32)

    proj = _in_proj(x2, attn_norm_g.reshape(1, -1), w_in, gains, batch, seq)
    qa, ka, va = proj[:3]
    nb = len(B_DILS)
    qbs, kbs, vbs = proj[3:3 + nb], proj[3 + nb:3 + 2 * nb], proj[3 + 2 * nb:]

    bias_a = _bias_tables(slopes[:A_Q_HEADS], A_STACK_HEADS, A_HALF_WINDOW, 1, Q_TILE + 2 * A_HALF_WINDOW)
    sink_col = jnp.repeat(a_sinks.astype(jnp.float32) * LOG2E, Q_TILE).reshape(
        A_Q_HEADS // A_STACK_HEADS, A_STACK_HEADS * Q_TILE, 1)
    as_seqs = lambda a: a.reshape(batch, seq, a.shape[-1])
    out_a = _banded_attention(as_seqs(qa), as_seqs(ka), as_seqs(va), bias_a, half_w=A_HALF_WINDOW,
                              sink=sink_col, name="attn_a")[0].reshape(n, A_Q_W)

    outs_b, lses_b = [], []
    for bi, (window, dil) in enumerate(B_BRANCHES):
        half_w = window // (2 * dil)
        bias_b = _bias_tables(slopes[A_Q_HEADS:], B_STACK_HEADS, half_w, dil, Q_TILE + 2 * half_w)
        L = seq // dil
        to_seqs = lambda a: a.reshape(batch * dil, L, a.shape[-1])
        o, lse = _banded_attention(to_seqs(qbs[bi]), to_seqs(kbs[bi]), to_seqs(vbs[bi]), bias_b,
                                   half_w=half_w, want_lse=True, name=f"attn_b_d{dil}")
        if dil == 1:
            outs_b.append(o.reshape(n, B_W))
            lses_b.append(lse.reshape(n, LANES))
        else:
            outs_b.append(o.reshape(batch, dil, L, B_W))
            lses_b.append(lse.reshape(batch, dil, L, LANES))

    wr_t = w_router.T.astype(jnp.float32)
    wr_hi = wr_t.astype(jnp.bfloat16)
    wr_lo = (wr_t - wr_hi.astype(jnp.float32)).astype(jnp.bfloat16)
    wr = jnp.concatenate([wr_hi, wr_lo], axis=0)
    br = jnp.broadcast_to(b_router.astype(jnp.float32)[:, None], (N_EXPERTS, LANES))
    x1, hf, topi, gates, ranks, counts = _out_proj_router(
        out_a, outs_b, lses_b, x2, w_out, ffn_norm_g.reshape(1, -1), wr, br, seq)

    g = MOE_ROWS
    nk = n * TOP_K
    n_rows = nk + N_EXPERTS * g
    cnt = counts[:, 0]
    pcnt = (cnt + g - 1) // g * g
    pends = jnp.cumsum(pcnt)
    pstarts = pends - pcnt
    experts = jnp.arange(N_EXPERTS, dtype=jnp.int32)
    start_of = jnp.sum(jnp.where(topi[:, :, None] == experts, pstarts, 0), axis=-1)
    dest = (start_of + ranks).astype(jnp.int32)
    plan = _moe_plan(pends, n_rows // g)
    dest3 = dest.reshape(TOP_K, n // SC_CHUNK, SC_CHUNK).transpose(1, 0, 2)

    xb = _sc_dispatch(hf, dest3, n_rows)
    y = _moe_experts(plan, xb, w1, b1[:, None, :], w2, b2[:, None, :])
    per_split = n // COMBINE_SPLITS
    dest_c = dest.reshape(TOP_K, n // SC_COLLECT_CHUNK, SC_COLLECT_CHUNK).transpose(1, 0, 2)
    chunks_per_split = per_split // SC_COLLECT_CHUNK
    out = x1
    for s in range(COMBINE_SPLITS):
        yg = _sc_collect(y, dest_c[s * chunks_per_split:(s + 1) * chunks_per_split])
        out = _combine(out, yg, gates, s)
    return out


def kernel(x, attn_norm_g, w_in, a_q_norm_g, a_k_norm_g, a_sinks, b_q_norm_g, b_k_norm_g, w_out,
           ffn_norm_g, w_router, b_router, w1, b1, w2, b2):
    batch, seq, d = x.shape
    x2 = x.reshape(batch * seq, d)
    for i in range(attn_norm_g.shape[0]):
        x2 = _layer(x2, batch, seq, attn_norm_g[i], w_in[i], a_q_norm_g[i], a_k_norm_g[i],
                    a_sinks[i], b_q_norm_g[i], b_k_norm_g[i], w_out[i], ffn_norm_g[i],
                    w_router[i], b_router[i], w1[i], b1[i], w2[i], b2[i])
    return x2.reshape(batch, seq, d)
```

```python
import functools

import jax
import jax.numpy as jnp
import numpy as np
from jax import lax
from jax.experimental import pallas as pl
from jax.experimental.pallas import tpu as pltpu
from jax.experimental.pallas import tpu_sc as plsc

D_MODEL = 1024
HALF_D = D_MODEL // 2
HEAD_DIM = 64
LANES = 128
MXU_DIM = 256
A_Q_HEADS = 8
A_KV_HEADS = 2
B_HEADS = 8
A_HALF_WINDOW = 128
B_BRANCHES = ((128, 1), (512, 4), (2048, 16))
B_DILS = tuple(d for _, d in B_BRANCHES)
RESIDUE_STRIDE = 4
N_ALIBI_HEADS = 16
A_Q_W = A_Q_HEADS * HEAD_DIM
A_KV_W = A_KV_HEADS * HEAD_DIM
B_W = B_HEADS * HEAD_DIM
N_EXPERTS = 32
TOP_K = 4
D_FF = 1024
SWIGLU_ALPHA = 1.702
SWIGLU_LIMIT = 7.0
NORM_EPS = 1e-5
MASK_VALUE = -1e30
LOG2E = 1.4426950408889634

Q_TILE = 128
ATTN_STEP_ROWS = 2048
A_STACK_HEADS = 4
B_STACK_HEADS = 2
PROJ_ROWS = 512
MOE_ROWS = 512
FF_CHUNK = 512
MOE_STEP_BLOCKS = 2
MOE_WEIGHT_BUFS = 3
VMEM_LIMIT = 48 * 1024 * 1024
MOE_VMEM_LIMIT = 58 * 1024 * 1024
SC_CORES = 2
SC_SUBCORES = 16
SC_WORKERS = SC_CORES * SC_SUBCORES
SC_CHUNK = 64
SC_COLLECT_CHUNK = 32
COMBINE_SPLITS = 8


def _pack_halves(v):
    lo = v[:, :HALF_D].astype(jnp.bfloat16).astype(jnp.float32)
    hi = v[:, HALF_D:].astype(jnp.bfloat16).astype(jnp.float32)
    return (pltpu.bitcast(lo, jnp.uint32) >> 16) | pltpu.bitcast(hi, jnp.uint32)


def _unpack_halves(w):
    lo = pltpu.bitcast(w << 16, jnp.float32)
    hi = pltpu.bitcast(w & jnp.uint32(0xFFFF0000), jnp.float32)
    return lo, hi


def _alibi_slopes():
    return np.exp2(-8.0 * np.arange(1, N_ALIBI_HEADS + 1, dtype=np.float32) / N_ALIBI_HEADS).astype(np.float32)


def _bias_tables(head_slopes, heads_per_group, half_w, dist_scale, tk):
    i = np.arange(Q_TILE)[:, None]
    j = np.arange(tk)[None, :]
    tabs = []
    for shift in (0, half_w, tk - Q_TILE):
        dist = np.abs(j - shift - i)
        valid = dist <= half_w
        per_head = []
        for sl in head_slopes:
            b = (-np.float64(sl) * LOG2E * (dist * dist_scale)).astype(np.float32)
            per_head.append(np.where(valid, b, np.float32(MASK_VALUE)).astype(np.float32))
        t = np.stack(per_head).reshape(-1, heads_per_group * Q_TILE, tk)
        tabs.append(t)
    return jnp.asarray(np.stack(tabs))


def _in_proj_kernel(x_ref, g_ref, w_hbm, gains_ref, qa_ref, ka_ref, va_ref, *rest):
    b_refs, (scr_ref, scr2_ref, w_ref, stage_ref, sem) = rest[:-5], rest[-5:]

    @pl.when(pl.program_id(0) == 0)
    def _():
        width = stage_ref.shape[1]
        for c0 in range(0, w_ref.shape[1], width):
            cp = pltpu.make_async_copy(w_hbm.at[:, c0:c0 + width], stage_ref, sem)
            cp.start()
            cp.wait()
            w_ref[:, c0:c0 + width] = stage_ref[...].astype(jnp.bfloat16)

    x = x_ref[...]
    xn = x * lax.rsqrt(jnp.mean(x * x, axis=-1, keepdims=True) + NORM_EPS) * g_ref[...]
    xn = xn.astype(jnp.bfloat16)
    r = lax.broadcasted_iota(jnp.int32, (MXU_DIM, MXU_DIM), 0) // HEAD_DIM
    c = lax.broadcasted_iota(jnp.int32, (MXU_DIM, MXU_DIM), 1) // HEAD_DIM
    blockdiag = jnp.where(r == c, 1.0, 0.0).astype(jnp.bfloat16)

    def head_rms(sec, gain_row):
        width = sec.shape[1]
        parts = []
        step = min(width, MXU_DIM)
        for j in range(width // step):
            p = sec[:, j * step:(j + 1) * step]
            ss = jnp.dot((p * p).astype(jnp.bfloat16), blockdiag[:step, :step],
                         preferred_element_type=jnp.float32)
            parts.append(p * lax.rsqrt(ss * (1.0 / HEAD_DIM) + NORM_EPS)
                         * gains_ref[gain_row:gain_row + 1, :step])
        return parts

    def project(col0, width, gain_row):
        sec = jnp.dot(xn, w_ref[:, col0:col0 + width], preferred_element_type=jnp.float32)
        return [sec] if gain_row is None else head_rms(sec, gain_row)

    def store(out_ref, parts):
        w = parts[0].shape[1]
        for j, p in enumerate(parts):
            out_ref[:, j * w:(j + 1) * w] = p.astype(out_ref.dtype)

    def per_kv_head(p):
        lane = lax.broadcasted_iota(jnp.int32, p.shape, 1)
        swapped = pltpu.roll(p, HEAD_DIM, axis=1)
        low = lane < HEAD_DIM
        return [jnp.where(low, p, swapped), jnp.where(low, swapped, p)]

    store(qa_ref, project(0, A_Q_W, 0))
    kva = project(A_Q_W, 2 * A_KV_W, None)[0]
    store(ka_ref, per_kv_head(head_rms(kva[:, :A_KV_W], 1)[0]))
    store(va_ref, per_kv_head(kva[:, A_KV_W:]))

    rows = x_ref.shape[0]
    col0 = A_Q_W + 2 * A_KV_W
    for t, gain_row in enumerate((2, 3, None)):
        parts = project(col0 + t * B_W, B_W, gain_row)
        sec = jnp.concatenate(parts, axis=-1) if len(parts) > 1 else parts[0]
        for j in range(B_W // LANES):
            scr_ref[j] = sec[:, j * LANES:(j + 1) * LANES]
        n_lane_chunks = B_W // LANES
        prev_dil = 1
        for bi, dil in enumerate(B_DILS):
            out_ref = b_refs[t * len(B_DILS) + bi]
            if dil == 1:
                out_ref[...] = sec.astype(out_ref.dtype)
                continue
            assert dil == prev_dil * RESIDUE_STRIDE
            last = dil == B_DILS[-1]
            for res in range(dil):
                r_prev, r_sub = res % prev_dil, res // prev_dil
                for j in range(n_lane_chunks):
                    if prev_dil == 1:
                        v = scr_ref[j, pl.ds(r_sub, rows // dil, stride=RESIDUE_STRIDE), :]
                    else:
                        v = scr2_ref[r_prev * n_lane_chunks + j,
                                     pl.ds(r_sub, rows // dil, stride=RESIDUE_STRIDE), :]
                    out_ref[0, res, :, j * LANES:(j + 1) * LANES] = v.astype(out_ref.dtype)
                    if not last:
                        scr2_ref[res * n_lane_chunks + j] = v
            prev_dil = dil


def _in_proj(x2, g, w_in, gains, batch, seq):
    n = x2.shape[0]
    rows = PROJ_ROWS
    steps = seq // rows
    a_widths = (A_Q_W, 2 * A_KV_W, 2 * A_KV_W)
    out_shape = [jax.ShapeDtypeStruct((n, w), jnp.bfloat16) for w in a_widths]
    out_specs = [pl.BlockSpec((rows, w), lambda i: (i, 0)) for w in a_widths]
    for _ in range(3):
        for dil in B_DILS:
            if dil == 1:
                out_shape.append(jax.ShapeDtypeStruct((n, B_W), jnp.bfloat16))
                out_specs.append(pl.BlockSpec((rows, B_W), lambda i: (i, 0)))
            else:
                out_shape.append(jax.ShapeDtypeStruct((batch, dil, seq // dil, B_W), jnp.bfloat16))
                out_specs.append(pl.BlockSpec((1, dil, rows // dil, B_W),
                                              lambda i: (i // steps, 0, i % steps, 0)))
    return pl.pallas_call(
        _in_proj_kernel,
        out_shape=out_shape,
        grid=(n // rows,),
        in_specs=[
            pl.BlockSpec((rows, D_MODEL), lambda i: (i, 0)),
            pl.BlockSpec((1, D_MODEL), lambda i: (0, 0)),
            pl.BlockSpec(memory_space=pl.ANY),
            pl.BlockSpec(gains.shape, lambda i: (0, 0)),
        ],
        out_specs=out_specs,
        scratch_shapes=[pltpu.VMEM((B_W // LANES, rows, LANES), jnp.float32),
                        pltpu.VMEM((RESIDUE_STRIDE * B_W // LANES, rows // RESIDUE_STRIDE, LANES),
                                   jnp.float32),
                        pltpu.VMEM(w_in.shape, jnp.bfloat16),
                        pltpu.VMEM((w_in.shape[0], w_in.shape[1] // 3), jnp.float32),
                        pltpu.SemaphoreType.DMA],
        compiler_params=pltpu.CompilerParams(
            dimension_semantics=("arbitrary",), vmem_limit_bytes=VMEM_LIMIT),
        name="in_proj",
    )(x2, g, w_in, gains)


def _attn_kernel(*refs, n_chunks, kv_chunks, heads_per_stack, tk, half_w, seq_len, rows, has_sink,
                 want_lse):
    it = iter(refs)
    q_ref, k_ref, v_ref, bias_ref = next(it), next(it), next(it), next(it)
    sink_ref = next(it) if has_sink else None
    o_ref = next(it)
    lse_ref = next(it) if want_lse else None

    n_tiles = seq_len // Q_TILE
    tiles_per_step = rows // Q_TILE
    chunks_per_group = n_chunks // kv_chunks
    assert (2 * chunks_per_group) % heads_per_stack == 0
    step = pl.program_id(1)
    lane = lax.broadcasted_iota(jnp.int32, (Q_TILE, LANES), 1)
    low_half = lane < HEAD_DIM
    ones = jnp.ones((tk, LANES), jnp.bfloat16)

    for sq, t in [(a, b) for a in range(q_ref.shape[0]) for b in range(tiles_per_step)]:
        tile = step * tiles_per_step + t
        q0 = tile * Q_TILE
        kv_rows = k_ref.shape[1]
        kv_row0 = jnp.clip(step * rows - half_w, 0, seq_len - kv_rows)
        start = pl.multiple_of(jnp.clip(q0 - half_w, 0, seq_len - tk) - kv_row0, HEAD_DIM)
        variant = jnp.where(tile == 0, 0, jnp.where(tile == n_tiles - 1, 2, 1))
        r0 = t * Q_TILE
        lse_tile = jnp.zeros((Q_TILE, LANES), jnp.float32)
        head_o, head_lse = {}, {}
        for g in range(2 * n_chunks // heads_per_stack):
            heads = range(g * heads_per_stack, (g + 1) * heads_per_stack)
            kv = (heads[0] // 2) // chunks_per_group
            kc = k_ref[sq, pl.ds(start, tk), kv * LANES:(kv + 1) * LANES]
            vc = v_ref[sq, pl.ds(start, tk), kv * LANES:(kv + 1) * LANES]
            v_aug = jnp.concatenate([vc, ones], axis=1)
            q_parts = []
            for h in heads:
                c = h // 2
                q2 = q_ref[sq, r0:r0 + Q_TILE, c * LANES:(c + 1) * LANES]
                keep = low_half if h % 2 == 0 else ~low_half
                q_parts.append(jnp.where(keep, q2, jnp.zeros_like(q2)))
            qs = q_parts[0] if len(q_parts) == 1 else jnp.concatenate(q_parts, axis=0)
            s = lax.dot_general(qs, kc, (((1,), (1,)), ((), ())),
                                preferred_element_type=jnp.float32)
            s = s + bias_ref[variant, g]
            m = jnp.max(s, axis=-1, keepdims=True)
            if has_sink:
                m = jnp.maximum(m, sink_ref[g])
            p = jnp.exp2(s - m)
            ov = jnp.dot(p.astype(jnp.bfloat16), v_aug, preferred_element_type=jnp.float32)
            o, l = ov[:, :LANES], ov[:, LANES:]
            if has_sink:
                l = l + jnp.exp2(sink_ref[g] - m)
            o = o * (1.0 / l)
            if want_lse:
                lse = m + jnp.log(l) * LOG2E
            for idx, h in enumerate(heads):
                head_o[h] = o[idx * Q_TILE:(idx + 1) * Q_TILE]
                if want_lse:
                    head_lse[h] = lse[idx * Q_TILE:(idx + 1) * Q_TILE]
                if h % 2 == 1:
                    c = h // 2
                    o2 = jnp.where(low_half, head_o.pop(h - 1), head_o.pop(h))
                    o_ref[sq, r0:r0 + Q_TILE, c * LANES:(c + 1) * LANES] = o2.astype(o_ref.dtype)
                    if want_lse:
                        lse_tile = jnp.where(lane == h - 1, head_lse.pop(h - 1),
                                             jnp.where(lane == h, head_lse.pop(h), lse_tile))
        if want_lse:
            lse_ref[sq, r0:r0 + Q_TILE, :] = lse_tile


def _banded_attention(q, k, v, bias, *, half_w, sink=None, want_lse=False, name):
    n_seq, L, qw = q.shape
    kw = k.shape[2]
    tk = Q_TILE + 2 * half_w
    rows = min(ATTN_STEP_ROWS, L)
    seqs = ATTN_STEP_ROWS // rows
    kv_rows = min(L, rows + 2 * half_w)

    def kv_index(s, i):
        row0 = pl.multiple_of(jnp.clip(i * rows - half_w, 0, L - kv_rows), HEAD_DIM)
        return s * seqs, row0, 0

    args = [q, k, v, bias]
    in_specs = [
        pl.BlockSpec((seqs, rows, qw), lambda s, i: (s, i, 0)),
        pl.BlockSpec((pl.Element(seqs), pl.Element(kv_rows), pl.Element(kw)), kv_index),
        pl.BlockSpec((pl.Element(seqs), pl.Element(kv_rows), pl.Element(kw)), kv_index),
        pl.BlockSpec(bias.shape, lambda s, i: (0, 0, 0, 0)),
    ]
    if sink is not None:
        args.append(sink)
        in_specs.append(pl.BlockSpec(sink.shape, lambda s, i: (0, 0, 0)))
    out_shape = [jax.ShapeDtypeStruct((n_seq, L, qw), jnp.bfloat16)]
    out_specs = [pl.BlockSpec((seqs, rows, qw), lambda s, i: (s, i, 0))]
    if want_lse:
        out_shape.append(jax.ShapeDtypeStruct((n_seq, L, LANES), jnp.float32))
        out_specs.append(pl.BlockSpec((seqs, rows, LANES), lambda s, i: (s, i, 0)))

    kern = functools.partial(
        _attn_kernel, n_chunks=qw // LANES, kv_chunks=kw // LANES,
        heads_per_stack=bias.shape[2] // Q_TILE, tk=tk, half_w=half_w, seq_len=L,
        rows=rows, has_sink=sink is not None, want_lse=want_lse)
    return pl.pallas_call(
        kern,
        out_shape=out_shape,
        grid=(n_seq // seqs, L // rows),
        in_specs=in_specs,
        out_specs=out_specs,
        compiler_params=pltpu.CompilerParams(
            dimension_semantics=("arbitrary", "arbitrary"), vmem_limit_bytes=VMEM_LIMIT),
        name=name,
    )(*args)


def _out_proj_router_kernel(*refs):
    nb = len(B_DILS)
    oa_ref = refs[0]
    o_refs = refs[1:1 + nb]
    lse_refs = refs[1 + nb:1 + 2 * nb]
    (x_ref, wo_ref, g_ref, wr_ref, br_ref,
     x1_ref, hf_ref, topi_ref, gate_ref, rank_ref, cnt_ref,
     tri_ref, carry_ref, so_ref, sl_ref, wo_bf_ref) = refs[1 + 2 * nb:]
    i = pl.program_id(0)
    rows = x_ref.shape[0]

    @pl.when(i == 0)
    def _():
        a = lax.broadcasted_iota(jnp.int32, (rows, rows), 0)
        b = lax.broadcasted_iota(jnp.int32, (rows, rows), 1)
        tri_ref[...] = jnp.where(a <= b, 1.0, 0.0).astype(jnp.bfloat16)
        carry_ref[...] = jnp.zeros_like(carry_ref)
        wo_bf_ref[...] = wo_ref[...].astype(jnp.bfloat16)

    outs, lses = [], []
    for bi, dil in enumerate(B_DILS):
        if dil == 1:
            outs.append(o_refs[bi][...].astype(jnp.float32))
            lses.append(lse_refs[bi][...])
        else:
            for res in range(dil):
                for j in range(B_W // LANES):
                    so_ref[bi, j, pl.ds(res, rows // dil, stride=dil), :] = (
                        o_refs[bi][0, res, :, j * LANES:(j + 1) * LANES].astype(jnp.float32))
                sl_ref[bi, pl.ds(res, rows // dil, stride=dil), :] = lse_refs[bi][0, res]
            outs.append(jnp.concatenate([so_ref[bi, j] for j in range(B_W // LANES)], axis=-1))
            lses.append(sl_ref[bi])

    mx = functools.reduce(jnp.maximum, lses)
    es = [jnp.exp2(l - mx) for l in lses]
    inv = 1.0 / functools.reduce(lambda a, b: a + b, es)
    eh = lax.broadcasted_iota(jnp.int32, (LANES, B_W), 0)
    ej = lax.broadcasted_iota(jnp.int32, (LANES, B_W), 1) // HEAD_DIM
    expand = jnp.where(eh == ej, 1.0, 0.0).astype(jnp.bfloat16)
    ob = jnp.zeros((rows, B_W), jnp.float32)
    for e, o in zip(es, outs):
        w = e * inv
        wide = jnp.dot(w.astype(jnp.bfloat16), expand, preferred_element_type=jnp.float32)
        ob = ob + wide * o

    attn = jnp.concatenate([oa_ref[...], ob.astype(jnp.bfloat16)], axis=-1)
    x1 = x_ref[...] + jnp.dot(attn, wo_bf_ref[...], preferred_element_type=jnp.float32)
    x1_ref[...] = x1
    hf = x1 * lax.rsqrt(jnp.mean(x1 * x1, axis=-1, keepdims=True) + NORM_EPS) * g_ref[...]
    hf_hi = hf.astype(jnp.bfloat16)
    hf_ref[...] = _pack_halves(hf)
    hf_lo = (hf - hf_hi.astype(jnp.float32)).astype(jnp.bfloat16)

    nt = (((1,), (1,)), ((), ()))
    lg_hi = lax.dot_general(wr_ref[...], hf_hi, nt, preferred_element_type=jnp.float32)
    lg_lo = lax.dot_general(wr_ref[0:N_EXPERTS, :], hf_lo, nt, preferred_element_type=jnp.float32)
    logits = lg_hi[0:N_EXPERTS] + lg_hi[N_EXPERTS:] + lg_lo + br_ref[:, 0:1]


    eidx = lax.broadcasted_iota(jnp.int32, (N_EXPERTS, rows), 0)
    work = logits
    vals, sels = [], []
    for k in range(TOP_K):
        mk = jnp.max(work, axis=0, keepdims=True)
        ik = jnp.min(jnp.where(work == mk, eidx, N_EXPERTS), axis=0, keepdims=True)
        sel = eidx == ik
        work = jnp.where(sel, -jnp.inf, work)
        vals.append(mk)
        sels.append(sel)
        topi_ref[k:k + 1, :] = ik
    exps = [jnp.exp(vk - vals[0]) for vk in vals]
    denom = exps[0] + exps[1] + exps[2] + exps[3]
    ginv = 1.0 / denom
    for k in range(TOP_K):
        gate_ref[k:k + 1, :] = exps[k] * ginv

    onehot = jnp.zeros((N_EXPERTS, rows), jnp.float32)
    for sel in sels:
        onehot = onehot + jnp.where(sel, 1.0, 0.0)
    incl = jnp.dot(onehot.astype(jnp.bfloat16), tri_ref[...], preferred_element_type=jnp.float32)
    before = incl - onehot + carry_ref[:, 0:1]
    for k in range(TOP_K):
        rk = jnp.sum(jnp.where(sels[k], before, 0.0), axis=0, keepdims=True)
        rank_ref[k:k + 1, :] = rk.astype(jnp.int32)
    carry = carry_ref[...] + jnp.sum(onehot, axis=1, keepdims=True)
    carry_ref[...] = carry
    cnt_ref[...] = carry.astype(jnp.int32)


def _out_proj_router(oa, outs_b, lses_b, x2, wo_bf, g, wr, br, seq):
    n = x2.shape[0]
    rows = PROJ_ROWS
    steps = seq // rows
    row_spec = lambda w: pl.BlockSpec((rows, w), lambda i: (i, 0))
    full = lambda a: pl.BlockSpec(a.shape, lambda i: (0,) * a.ndim)
    col_spec = pl.BlockSpec((TOP_K, rows), lambda i: (0, i))

    def branch_spec(dil, w):
        if dil == 1:
            return row_spec(w)
        return pl.BlockSpec((1, dil, rows // dil, w), lambda i: (i // steps, 0, i % steps, 0))

    in_specs = ([row_spec(A_Q_W)]
                + [branch_spec(d, B_W) for d in B_DILS]
                + [branch_spec(d, LANES) for d in B_DILS]
                + [row_spec(D_MODEL), full(wo_bf), full(g), full(wr), full(br)])
    return pl.pallas_call(
        _out_proj_router_kernel,
        out_shape=[
            jax.ShapeDtypeStruct((n, D_MODEL), jnp.float32),
            jax.ShapeDtypeStruct((n, HALF_D), jnp.uint32),
            jax.ShapeDtypeStruct((TOP_K, n), jnp.int32),
            jax.ShapeDtypeStruct((TOP_K, n), jnp.float32),
            jax.ShapeDtypeStruct((TOP_K, n), jnp.int32),
            jax.ShapeDtypeStruct((N_EXPERTS, LANES), jnp.int32),
        ],
        grid=(n // rows,),
        in_specs=in_specs,
        out_specs=[row_spec(D_MODEL), row_spec(HALF_D), col_spec, col_spec, col_spec,
                   pl.BlockSpec((N_EXPERTS, LANES), lambda i: (0, 0))],
        scratch_shapes=[pltpu.VMEM((rows, rows), jnp.bfloat16),
                        pltpu.VMEM((N_EXPERTS, LANES), jnp.float32),
                        pltpu.VMEM((len(B_DILS), B_W // LANES, rows, LANES), jnp.float32),
                        pltpu.VMEM((len(B_DILS), rows, LANES), jnp.float32),
                        pltpu.VMEM(wo_bf.shape, jnp.bfloat16)],
        compiler_params=pltpu.CompilerParams(
            dimension_semantics=("arbitrary",), vmem_limit_bytes=VMEM_LIMIT),
        name="out_proj_router",
    )(oa, *outs_b, *lses_b, x2, wo_bf, g, wr, br)


def _mxu_dot(a_bf, w_f32):
    return lax.dot_general(a_bf, w_f32, (((1,), (0,)), ((), ())), preferred_element_type=jnp.float32)


def _moe_kernel(blk_exp_ref, first_ref, slot_ref, next_exp_ref, n_used_ref,
                x_ref, w1_hbm, b1_ref, w2_hbm, b2_ref, y_ref, w1_buf, w2_buf, sem):
    step = pl.program_id(0)

    def weight_copies(expert, slot):
        return (pltpu.make_async_copy(w1_hbm.at[expert], w1_buf.at[slot], sem.at[slot, 0]),
                pltpu.make_async_copy(w2_hbm.at[expert], w2_buf.at[slot], sem.at[slot, 1]))

    @pl.when(step * MOE_STEP_BLOCKS < n_used_ref[0])
    def _():
        @pl.when(step == 0)
        def _():
            for cp in weight_copies(blk_exp_ref[0], slot_ref[0]):
                cp.start()

        for j in range(MOE_STEP_BLOCKS):
            i = step * MOE_STEP_BLOCKS + j

            @pl.when(first_ref[i] == 1)
            def _():
                slot = slot_ref[i]
                for cp in weight_copies(blk_exp_ref[i], slot):
                    cp.wait()

                @pl.when(next_exp_ref[i] >= 0)
                def _():
                    for cp in weight_copies(next_exp_ref[i], (slot + 1) % MOE_WEIGHT_BUFS):
                        cp.start()

        for j in range(MOE_STEP_BLOCKS):
            i = step * MOE_STEP_BLOCKS + j
            slot = slot_ref[i]
            expert = blk_exp_ref[i]
            rs = slice(j * MOE_ROWS, (j + 1) * MOE_ROWS)
            x = jnp.concatenate(_unpack_halves(x_ref[rs, :]), axis=-1).astype(jnp.bfloat16)
            acc = jnp.zeros((MOE_ROWS, D_MODEL), jnp.float32)
            for c in range(D_FF // FF_CHUNK):
                lo = c * FF_CHUNK
                glu = _mxu_dot(x, w1_buf[slot, :, lo:lo + FF_CHUNK]) + b1_ref[expert, :, lo:lo + FF_CHUNK]
                lin = (_mxu_dot(x, w1_buf[slot, :, D_FF + lo:D_FF + lo + FF_CHUNK])
                       + b1_ref[expert, :, D_FF + lo:D_FF + lo + FF_CHUNK])
                glu = jnp.minimum(glu, SWIGLU_LIMIT)
                lin = jnp.clip(lin, -SWIGLU_LIMIT, SWIGLU_LIMIT)
                act = glu * (1.0 / (1.0 + jnp.exp(-SWIGLU_ALPHA * glu))) * (lin + 1.0)
                acc = acc + _mxu_dot(act.astype(jnp.bfloat16), w2_buf[slot, lo:lo + FF_CHUNK, :])
            y_ref[rs, :] = _pack_halves(acc + b2_ref[expert])


def _moe_plan(pends, n_blk):
    g = MOE_ROWS
    blk_row0 = jnp.arange(n_blk, dtype=jnp.int32) * g
    blk_exp = jnp.minimum(jnp.sum(pends[None, :] <= blk_row0[:, None], axis=-1),
                          N_EXPERTS - 1).astype(jnp.int32)
    n_used = (pends[-1] // g).astype(jnp.int32)
    used = blk_row0 < pends[-1]
    prev_exp = jnp.concatenate([jnp.full((1,), -1, jnp.int32), blk_exp[:-1]])
    first = (used & (blk_exp != prev_exp)).astype(jnp.int32)
    slot = ((jnp.cumsum(first) - 1) % MOE_WEIGHT_BUFS).astype(jnp.int32)
    pstarts = jnp.concatenate([jnp.zeros((1,), pends.dtype), pends[:-1]])
    nonempty = pends > pstarts
    experts = jnp.arange(N_EXPERTS, dtype=jnp.int32)
    later = nonempty[None, :] & (experts[None, :] > experts[:, None])
    next_nonempty = jnp.min(jnp.where(later, experts[None, :], N_EXPERTS), axis=-1)
    next_nonempty = jnp.where(next_nonempty == N_EXPERTS, -1, next_nonempty).astype(jnp.int32)
    next_exp = jnp.sum(jnp.where(blk_exp[:, None] == experts[None, :], next_nonempty[None, :], 0),
                       axis=-1).astype(jnp.int32)
    return blk_exp, first, slot, next_exp, n_used.reshape(1)


def _moe_experts(plan, xb, w1, b1, w2, b2):
    n_rows = xb.shape[0]
    n_blk = n_rows // MOE_ROWS

    step_rows = MOE_STEP_BLOCKS * MOE_ROWS

    def blk(i, *p):
        return jnp.minimum(i, (p[-1][0] - 1) // MOE_STEP_BLOCKS)

    grid_spec = pltpu.PrefetchScalarGridSpec(
        num_scalar_prefetch=len(plan),
        grid=(n_blk // MOE_STEP_BLOCKS,),
        in_specs=[
            pl.BlockSpec((step_rows, HALF_D), lambda i, *p: (blk(i, *p), 0)),
            pl.BlockSpec(memory_space=pl.ANY),
            pl.BlockSpec(b1.shape, lambda i, *p: (0, 0, 0)),
            pl.BlockSpec(memory_space=pl.ANY),
            pl.BlockSpec(b2.shape, lambda i, *p: (0, 0, 0)),
        ],
        out_specs=pl.BlockSpec((step_rows, HALF_D), lambda i, *p: (blk(i, *p), 0)),
        scratch_shapes=[pltpu.VMEM((MOE_WEIGHT_BUFS, D_MODEL, 2 * D_FF), jnp.float32),
                        pltpu.VMEM((MOE_WEIGHT_BUFS, D_FF, D_MODEL), jnp.float32),
                        pltpu.SemaphoreType.DMA((MOE_WEIGHT_BUFS, 2))],
    )
    return pl.pallas_call(
        _moe_kernel,
        out_shape=jax.ShapeDtypeStruct((n_rows, HALF_D), jnp.uint32),
        grid_spec=grid_spec,
        compiler_params=pltpu.CompilerParams(
            dimension_semantics=("arbitrary",), vmem_limit_bytes=MOE_VMEM_LIMIT),
        name="moe_experts",
    )(*plan, xb, w1, b1, w2, b2)


def _combine_kernel(x1_ref, yg_ref, gate_ref, o_ref):
    acc_lo = x1_ref[:, :HALF_D]
    acc_hi = x1_ref[:, HALF_D:]
    rows = x1_ref.shape[0]
    gates = jnp.concatenate([gate_ref[...], jnp.zeros((8 - TOP_K, rows), jnp.float32)], axis=0).T
    for k in range(TOP_K):
        lo, hi = _unpack_halves(yg_ref[k])
        gk = gates[:, k:k + 1]
        acc_lo = acc_lo + lo * gk
        acc_hi = acc_hi + hi * gk
    o_ref[:, :HALF_D] = acc_lo
    o_ref[:, HALF_D:] = acc_hi


def _combine(acc, yg, gates_nk, split):
    n = acc.shape[0]
    rows = PROJ_ROWS
    steps = yg.shape[1] // rows
    first = split * steps
    return pl.pallas_call(
        _combine_kernel,
        out_shape=jax.ShapeDtypeStruct((n, D_MODEL), jnp.float32),
        grid=(steps,),
        in_specs=[pl.BlockSpec((rows, D_MODEL), lambda i: (first + i, 0)),
                  pl.BlockSpec((TOP_K, rows, HALF_D), lambda i: (0, i, 0)),
                  pl.BlockSpec((TOP_K, rows), lambda i: (0, first + i))],
        out_specs=pl.BlockSpec((rows, D_MODEL), lambda i: (first + i, 0)),
        input_output_aliases={0: 0},
        compiler_params=pltpu.CompilerParams(
            dimension_semantics=("arbitrary",), vmem_limit_bytes=VMEM_LIMIT),
        name="moe_combine",
    )(acc, yg, gates_nk)


def _sc_worker_id():
    return lax.axis_index("s") * SC_CORES + lax.axis_index("c")


def _sc_dispatch(hf, dest3, n_rows):
    n = hf.shape[0]
    chunks_per_worker = n // SC_CHUNK // SC_WORKERS
    mesh = plsc.VectorSubcoreMesh(core_axis_name="c", subcore_axis_name="s")

    @functools.partial(
        pl.kernel, mesh=mesh,
        out_type=jax.ShapeDtypeStruct((n_rows, HALF_D), hf.dtype),
        scratch_types=[pltpu.VMEM((2, TOP_K, SC_CHUNK), jnp.int32),
                       pltpu.VMEM((2, SC_CHUNK, HALF_D), hf.dtype),
                       pltpu.SemaphoreType.DMA((2,)),
                       pltpu.SemaphoreType.DMA((2,))],
        name="sc_dispatch")
    def run(hf_hbm, dest_hbm, xb_hbm, idx_v, rows_v, load_sem, scatter_sem):
        first = _sc_worker_id() * chunks_per_worker

        def load(j):
            slot = j % 2
            pltpu.sync_copy(dest_hbm.at[first + j], idx_v.at[slot])
            return pltpu.async_copy(hf_hbm.at[pl.ds((first + j) * SC_CHUNK, SC_CHUNK)],
                                    rows_v.at[slot], load_sem.at[slot])

        loads = {0: load(0)}
        scatters = {}
        for j in range(chunks_per_worker):
            slot = j % 2
            loads.pop(j).wait()
            scatters[j] = [pltpu.async_copy(rows_v.at[slot], xb_hbm.at[idx_v.at[slot, k]],
                                            scatter_sem.at[slot]) for k in range(TOP_K)]
            if j >= 1:
                for cp in scatters.pop(j - 1):
                    cp.wait()
            if j + 1 < chunks_per_worker:
                loads[j + 1] = load(j + 1)
        for cp in scatters.pop(chunks_per_worker - 1):
            cp.wait()

    return run(hf, dest3)


def _sc_collect(y, dest3):
    n = dest3.shape[0] * SC_COLLECT_CHUNK
    chunks_per_worker = n // SC_COLLECT_CHUNK // SC_WORKERS
    mesh = plsc.VectorSubcoreMesh(core_axis_name="c", subcore_axis_name="s")

    @functools.partial(
        pl.kernel, mesh=mesh,
        out_type=jax.ShapeDtypeStruct((TOP_K, n, HALF_D), y.dtype),
        scratch_types=[pltpu.VMEM((TOP_K, SC_COLLECT_CHUNK), jnp.int32),
                       pltpu.VMEM((TOP_K, SC_COLLECT_CHUNK, HALF_D), y.dtype),
                       pltpu.SemaphoreType.DMA((TOP_K,)),
                       pltpu.SemaphoreType.DMA((TOP_K,))],
        name="sc_collect")
    def run(y_hbm, dest_hbm, yg_hbm, idx_v, rows_v, gather_sem, write_sem):
        first = _sc_worker_id() * chunks_per_worker

        @pl.loop(0, chunks_per_worker)
        def _(j):
            ch = first + j
            pltpu.sync_copy(dest_hbm.at[ch], idx_v)
            gathers = [pltpu.async_copy(y_hbm.at[idx_v.at[k]], rows_v.at[k], gather_sem.at[k])
                       for k in range(TOP_K)]
            writes = []
            for k in range(TOP_K):
                gathers[k].wait()
                writes.append(pltpu.async_copy(
                    rows_v.at[k], yg_hbm.at[k, pl.ds(ch * SC_COLLECT_CHUNK, SC_COLLECT_CHUNK)],
                    write_sem.at[k]))
            for w in writes:
                w.wait()

    return run(y, dest3)


def _layer(x2, batch, seq, attn_norm_g, w_in, a_q_g, a_k_g, a_sinks, b_q_g, b_k_g, w_out,
           ffn_norm_g, w_router, b_router, w1, b1, w2, b2):
    n = x2.shape[0]
    slopes = _alibi_slopes()
    q_scale = HEAD_DIM ** -0.5 * LOG2E
    reps = MXU_DIM // HEAD_DIM
    gains = jnp.stack([jnp.tile(a_q_g, reps) * q_scale, jnp.tile(a_k_g, reps),
                       jnp.tile(b_q_g, reps) * q_scale, jnp.tile(b_k_g, reps)]).astype(jnp.float32)

    proj = _in_proj(x2, attn_norm_g.reshape(1, -1), w_in, gains, batch, seq)
    qa, ka, va = proj[:3]
    nb = len(B_DILS)
    qbs, kbs, vbs = proj[3:3 + nb], proj[3 + nb:3 + 2 * nb], proj[3 + 2 * nb:]

    bias_a = _bias_tables(slopes[:A_Q_HEADS], A_STACK_HEADS, A_HALF_WINDOW, 1, Q_TILE + 2 * A_HALF_WINDOW)
    sink_col = jnp.repeat(a_sinks.astype(jnp.float32) * LOG2E, Q_TILE).reshape(
        A_Q_HEADS // A_STACK_HEADS, A_STACK_HEADS * Q_TILE, 1)
    as_seqs = lambda a: a.reshape(batch, seq, a.shape[-1])
    out_a = _banded_attention(as_seqs(qa), as_seqs(ka), as_seqs(va), bias_a, half_w=A_HALF_WINDOW,
                              sink=sink_col, name="attn_a")[0].reshape(n, A_Q_W)

    outs_b, lses_b = [], []
    for bi, (window, dil) in enumerate(B_BRANCHES):
        half_w = window // (2 * dil)
        bias_b = _bias_tables(slopes[A_Q_HEADS:], B_STACK_HEADS, half_w, dil, Q_TILE + 2 * half_w)
        L = seq // dil
        to_seqs = lambda a: a.reshape(batch * dil, L, a.shape[-1])
        o, lse = _banded_attention(to_seqs(qbs[bi]), to_seqs(kbs[bi]), to_seqs(vbs[bi]), bias_b,
                                   half_w=half_w, want_lse=True, name=f"attn_b_d{dil}")
        if dil == 1:
            outs_b.append(o.reshape(n, B_W))
            lses_b.append(lse.reshape(n, LANES))
        else:
            outs_b.append(o.reshape(batch, dil, L, B_W))
            lses_b.append(lse.reshape(batch, dil, L, LANES))

    wr_t = w_router.T.astype(jnp.float32)
    wr_hi = wr_t.astype(jnp.bfloat16)
    wr_lo = (wr_t - wr_hi.astype(jnp.float32)).astype(jnp.bfloat16)
    wr = jnp.concatenate([wr_hi, wr_lo], axis=0)
    br = jnp.broadcast_to(b_router.astype(jnp.float32)[:, None], (N_EXPERTS, LANES))
    x1, hf, topi, gates, ranks, counts = _out_proj_router(
        out_a, outs_b, lses_b, x2, w_out, ffn_norm_g.reshape(1, -1), wr, br, seq)

    g = MOE_ROWS
    nk = n * TOP_K
    n_rows = nk + N_EXPERTS * g
    cnt = counts[:, 0]
    pcnt = (cnt + g - 1) // g * g
    pends = jnp.cumsum(pcnt)
    pstarts = pends - pcnt
    experts = jnp.arange(N_EXPERTS, dtype=jnp.int32)
    start_of = jnp.sum(jnp.where(topi[:, :, None] == experts, pstarts, 0), axis=-1)
    dest = (start_of + ranks).astype(jnp.int32)
    plan = _moe_plan(pends, n_rows // g)
    dest3 = dest.reshape(TOP_K, n // SC_CHUNK, SC_CHUNK).transpose(1, 0, 2)

    xb = _sc_dispatch(hf, dest3, n_rows)
    y = _moe_experts(plan, xb, w1, b1[:, None, :], w2, b2[:, None, :])
    per_split = n // COMBINE_SPLITS
    dest_c = dest.reshape(TOP_K, n // SC_COLLECT_CHUNK, SC_COLLECT_CHUNK).transpose(1, 0, 2)
    chunks_per_split = per_split // SC_COLLECT_CHUNK
    out = x1
    for s in range(COMBINE_SPLITS):
        yg = _sc_collect(y, dest_c[s * chunks_per_split:(s + 1) * chunks_per_split])
        out = _combine(out, yg, gates, s)
    return out


def kernel(x, attn_norm_g, w_in, a_q_norm_g, a_k_norm_g, a_sinks, b_q_norm_g, b_k_norm_g, w_out,
           ffn_norm_g, w_router, b_router, w1, b1, w2, b2):
    batch, seq, d = x.shape
    x2 = x.reshape(batch * seq, d)
    for i in range(attn_norm_g.shape[0]):
        x2 = _layer(x2, batch, seq, attn_norm_g[i], w_in[i], a_q_norm_g[i], a_k_norm_g[i],
                    a_sinks[i], b_q_norm_g[i], b_k_norm_g[i], w_out[i], ffn_norm_g[i],
                    w_router[i], b_router[i], w1[i], b1[i], w2[i], b2[i])
    return x2.reshape(batch, seq, d)
```

```python
import functools

import jax
import jax.numpy as jnp
import numpy as np
from jax import lax
from jax.experimental import pallas as pl
from jax.experimental.pallas import tpu as pltpu
from jax.experimental.pallas import tpu_sc as plsc

D_MODEL = 1024
HALF_D = D_MODEL // 2
HEAD_DIM = 64
LANES = 128
MXU_DIM = 256
A_Q_HEADS = 8
A_KV_HEADS = 2
B_HEADS = 8
A_HALF_WINDOW = 128
B_BRANCHES = ((128, 1), (512, 4), (2048, 16))
B_DILS = tuple(d for _, d in B_BRANCHES)
RESIDUE_STRIDE = 4
N_ALIBI_HEADS = 16
A_Q_W = A_Q_HEADS * HEAD_DIM
A_KV_W = A_KV_HEADS * HEAD_DIM
B_W = B_HEADS * HEAD_DIM
N_EXPERTS = 32
TOP_K = 4
D_FF = 1024
SWIGLU_ALPHA = 1.702
SWIGLU_LIMIT = 7.0
NORM_EPS = 1e-5
MASK_VALUE = -1e30
LOG2E = 1.4426950408889634

Q_TILE = 128
ATTN_STEP_ROWS = 2048
A_STACK_HEADS = 4
B_STACK_HEADS = 2
PROJ_ROWS = 512
IN_PROJ_ROWS = 1024
MOE_ROWS = 512
FF_CHUNK = 512
MOE_STEP_BLOCKS = 2
MOE_WEIGHT_BUFS = 3
VMEM_LIMIT = 48 * 1024 * 1024
MOE_VMEM_LIMIT = 58 * 1024 * 1024
SC_CORES = 2
SC_SUBCORES = 16
SC_WORKERS = SC_CORES * SC_SUBCORES
SC_CHUNK = 64
SC_COLLECT_CHUNK = 32
COMBINE_SPLITS = 8


def _pack_halves(v):
    lo = v[:, :HALF_D].astype(jnp.bfloat16).astype(jnp.float32)
    hi = v[:, HALF_D:].astype(jnp.bfloat16).astype(jnp.float32)
    return (pltpu.bitcast(lo, jnp.uint32) >> 16) | pltpu.bitcast(hi, jnp.uint32)


def _unpack_halves(w):
    lo = pltpu.bitcast(w << 16, jnp.float32)
    hi = pltpu.bitcast(w & jnp.uint32(0xFFFF0000), jnp.float32)
    return lo, hi


def _alibi_slopes():
    return np.exp2(-8.0 * np.arange(1, N_ALIBI_HEADS + 1, dtype=np.float32) / N_ALIBI_HEADS).astype(np.float32)


def _bias_tables(head_slopes, heads_per_group, half_w, dist_scale, tk):
    i = np.arange(Q_TILE)[:, None]
    j = np.arange(tk)[None, :]
    tabs = []
    for shift in (0, half_w, tk - Q_TILE):
        dist = np.abs(j - shift - i)
        valid = dist <= half_w
        per_head = []
        for sl in head_slopes:
            b = (-np.float64(sl) * LOG2E * (dist * dist_scale)).astype(np.float32)
            per_head.append(np.where(valid, b, np.float32(MASK_VALUE)).astype(np.float32))
        t = np.stack(per_head).reshape(-1, heads_per_group * Q_TILE, tk)
        tabs.append(t)
    return jnp.asarray(np.stack(tabs))


def _in_proj_kernel(x_ref, g_ref, w_hbm, gains_ref, qa_ref, ka_ref, va_ref, *rest):
    b_refs, (scr_ref, scr2_ref, w_ref, stage_ref, sem) = rest[:-5], rest[-5:]

    @pl.when(pl.program_id(0) == 0)
    def _():
        width = stage_ref.shape[1]
        for c0 in range(0, w_ref.shape[1], width):
            cp = pltpu.make_async_copy(w_hbm.at[:, c0:c0 + width], stage_ref, sem)
            cp.start()
            cp.wait()
            w_ref[:, c0:c0 + width] = stage_ref[...].astype(jnp.bfloat16)

    x = x_ref[...]
    xn = x * lax.rsqrt(jnp.mean(x * x, axis=-1, keepdims=True) + NORM_EPS) * g_ref[...]
    xn = xn.astype(jnp.bfloat16)
    r = lax.broadcasted_iota(jnp.int32, (MXU_DIM, MXU_DIM), 0) // HEAD_DIM
    c = lax.broadcasted_iota(jnp.int32, (MXU_DIM, MXU_DIM), 1) // HEAD_DIM
    blockdiag = jnp.where(r == c, 1.0, 0.0).astype(jnp.bfloat16)

    def head_rms(sec, gain_row):
        width = sec.shape[1]
        parts = []
        step = min(width, MXU_DIM)
        for j in range(width // step):
            p = sec[:, j * step:(j + 1) * step]
            ss = jnp.dot((p * p).astype(jnp.bfloat16), blockdiag[:step, :step],
                         preferred_element_type=jnp.float32)
            parts.append(p * lax.rsqrt(ss * (1.0 / HEAD_DIM) + NORM_EPS)
                         * gains_ref[gain_row:gain_row + 1, :step])
        return parts

    def project(col0, width, gain_row):
        sec = jnp.dot(xn, w_ref[:, col0:col0 + width], preferred_element_type=jnp.float32)
        return [sec] if gain_row is None else head_rms(sec, gain_row)

    def store(out_ref, parts):
        w = parts[0].shape[1]
        for j, p in enumerate(parts):
            out_ref[:, j * w:(j + 1) * w] = p.astype(out_ref.dtype)

    def per_kv_head(p):
        lane = lax.broadcasted_iota(jnp.int32, p.shape, 1)
        swapped = pltpu.roll(p, HEAD_DIM, axis=1)
        low = lane < HEAD_DIM
        return [jnp.where(low, p, swapped), jnp.where(low, swapped, p)]

    store(qa_ref, project(0, A_Q_W, 0))
    kva = project(A_Q_W, 2 * A_KV_W, None)[0]
    store(ka_ref, per_kv_head(head_rms(kva[:, :A_KV_W], 1)[0]))
    store(va_ref, per_kv_head(kva[:, A_KV_W:]))

    rows = x_ref.shape[0]
    col0 = A_Q_W + 2 * A_KV_W
    for t, gain_row in enumerate((2, 3, None)):
        parts = project(col0 + t * B_W, B_W, gain_row)
        sec = jnp.concatenate(parts, axis=-1) if len(parts) > 1 else parts[0]
        for j in range(B_W // LANES):
            scr_ref[j] = sec[:, j * LANES:(j + 1) * LANES]
        n_lane_chunks = B_W // LANES
        prev_dil = 1
        for bi, dil in enumerate(B_DILS):
            out_ref = b_refs[t * len(B_DILS) + bi]
            if dil == 1:
                out_ref[...] = sec.astype(out_ref.dtype)
                continue
            assert dil == prev_dil * RESIDUE_STRIDE
            last = dil == B_DILS[-1]
            for res in range(dil):
                r_prev, r_sub = res % prev_dil, res // prev_dil
                for j in range(n_lane_chunks):
                    if prev_dil == 1:
                        v = scr_ref[j, pl.ds(r_sub, rows // dil, stride=RESIDUE_STRIDE), :]
                    else:
                        v = scr2_ref[r_prev * n_lane_chunks + j,
                                     pl.ds(r_sub, rows // dil, stride=RESIDUE_STRIDE), :]
                    out_ref[0, res, :, j * LANES:(j + 1) * LANES] = v.astype(out_ref.dtype)
                    if not last:
                        scr2_ref[res * n_lane_chunks + j] = v
            prev_dil = dil


def _in_proj(x2, g, w_in, gains, batch, seq):
    n = x2.shape[0]
    rows = IN_PROJ_ROWS
    steps = seq // rows
    a_widths = (A_Q_W, 2 * A_KV_W, 2 * A_KV_W)
    out_shape = [jax.ShapeDtypeStruct((n, w), jnp.bfloat16) for w in a_widths]
    out_specs = [pl.BlockSpec((rows, w), lambda i: (i, 0)) for w in a_widths]
    for _ in range(3):
        for dil in B_DILS:
            if dil == 1:
                out_shape.append(jax.ShapeDtypeStruct((n, B_W), jnp.bfloat16))
                out_specs.append(pl.BlockSpec((rows, B_W), lambda i: (i, 0)))
            else:
                out_shape.append(jax.ShapeDtypeStruct((batch, dil, seq // dil, B_W), jnp.bfloat16))
                out_specs.append(pl.BlockSpec((1, dil, rows // dil, B_W),
                                              lambda i: (i // steps, 0, i % steps, 0)))
    return pl.pallas_call(
        _in_proj_kernel,
        out_shape=out_shape,
        grid=(n // rows,),
        in_specs=[
            pl.BlockSpec((rows, D_MODEL), lambda i: (i, 0)),
            pl.BlockSpec((1, D_MODEL), lambda i: (0, 0)),
            pl.BlockSpec(memory_space=pl.ANY),
            pl.BlockSpec(gains.shape, lambda i: (0, 0)),
        ],
        out_specs=out_specs,
        scratch_shapes=[pltpu.VMEM((B_W // LANES, rows, LANES), jnp.float32),
                        pltpu.VMEM((RESIDUE_STRIDE * B_W // LANES, rows // RESIDUE_STRIDE, LANES),
                                   jnp.float32),
                        pltpu.VMEM(w_in.shape, jnp.bfloat16),
                        pltpu.VMEM((w_in.shape[0], w_in.shape[1] // 3), jnp.float32),
                        pltpu.SemaphoreType.DMA],
        compiler_params=pltpu.CompilerParams(
            dimension_semantics=("arbitrary",), vmem_limit_bytes=MOE_VMEM_LIMIT),
        name="in_proj",
    )(x2, g, w_in, gains)


def _attn_kernel(*refs, n_chunks, kv_chunks, heads_per_stack, tk, half_w, seq_len, rows, has_sink,
                 want_lse):
    it = iter(refs)
    q_ref, k_ref, v_ref, bias_ref = next(it), next(it), next(it), next(it)
    sink_ref = next(it) if has_sink else None
    o_ref = next(it)
    lse_ref = next(it) if want_lse else None

    n_tiles = seq_len // Q_TILE
    tiles_per_step = rows // Q_TILE
    chunks_per_group = n_chunks // kv_chunks
    assert (2 * chunks_per_group) % heads_per_stack == 0
    step = pl.program_id(1)
    lane = lax.broadcasted_iota(jnp.int32, (Q_TILE, LANES), 1)
    low_half = lane < HEAD_DIM
    ones = jnp.ones((tk, LANES), jnp.bfloat16)

    for sq, t in [(a, b) for a in range(q_ref.shape[0]) for b in range(tiles_per_step)]:
        tile = step * tiles_per_step + t
        q0 = tile * Q_TILE
        kv_rows = k_ref.shape[1]
        kv_row0 = jnp.clip(step * rows - half_w, 0, seq_len - kv_rows)
        start = pl.multiple_of(jnp.clip(q0 - half_w, 0, seq_len - tk) - kv_row0, HEAD_DIM)
        variant = jnp.where(tile == 0, 0, jnp.where(tile == n_tiles - 1, 2, 1))
        r0 = t * Q_TILE
        lse_tile = jnp.zeros((Q_TILE, LANES), jnp.float32)
        head_o, head_lse = {}, {}
        for g in range(2 * n_chunks // heads_per_stack):
            heads = range(g * heads_per_stack, (g + 1) * heads_per_stack)
            kv = (heads[0] // 2) // chunks_per_group
            kc = k_ref[sq, pl.ds(start, tk), kv * LANES:(kv + 1) * LANES]
            vc = v_ref[sq, pl.ds(start, tk), kv * LANES:(kv + 1) * LANES]
            v_aug = jnp.concatenate([vc, ones], axis=1)
            q_parts = []
            for h in heads:
                c = h // 2
                q2 = q_ref[sq, r0:r0 + Q_TILE, c * LANES:(c + 1) * LANES]
                keep = low_half if h % 2 == 0 else ~low_half
                q_parts.append(jnp.where(keep, q2, jnp.zeros_like(q2)))
            qs = q_parts[0] if len(q_parts) == 1 else jnp.concatenate(q_parts, axis=0)
            s = lax.dot_general(qs, kc, (((1,), (1,)), ((), ())),
                                preferred_element_type=jnp.float32)
            s = s + bias_ref[variant, g]
            m = jnp.max(s, axis=-1, keepdims=True)
            if has_sink:
                m = jnp.maximum(m, sink_ref[g])
            p = jnp.exp2(s - m)
            ov = jnp.dot(p.astype(jnp.bfloat16), v_aug, preferred_element_type=jnp.float32)
            o, l = ov[:, :LANES], ov[:, LANES:]
            if has_sink:
                l = l + jnp.exp2(sink_ref[g] - m)
            o = o * (1.0 / l)
            if want_lse:
                lse = m + jnp.log(l) * LOG2E
            for idx, h in enumerate(heads):
                head_o[h] = o[idx * Q_TILE:(idx + 1) * Q_TILE]
                if want_lse:
                    head_lse[h] = lse[idx * Q_TILE:(idx + 1) * Q_TILE]
                if h % 2 == 1:
                    c = h // 2
                    o2 = jnp.where(low_half, head_o.pop(h - 1), head_o.pop(h))
                    o_ref[sq, r0:r0 + Q_TILE, c * LANES:(c + 1) * LANES] = o2.astype(o_ref.dtype)
                    if want_lse:
                        lse_tile = jnp.where(lane == h - 1, head_lse.pop(h - 1),
                                             jnp.where(lane == h, head_lse.pop(h), lse_tile))
        if want_lse:
            lse_ref[sq, r0:r0 + Q_TILE, :] = lse_tile


def _banded_attention(q, k, v, bias, *, half_w, sink=None, want_lse=False, name):
    n_seq, L, qw = q.shape
    kw = k.shape[2]
    tk = Q_TILE + 2 * half_w
    rows = min(ATTN_STEP_ROWS, L)
    seqs = ATTN_STEP_ROWS // rows
    kv_rows = min(L, rows + 2 * half_w)

    def kv_index(s, i):
        row0 = pl.multiple_of(jnp.clip(i * rows - half_w, 0, L - kv_rows), HEAD_DIM)
        return s * seqs, row0, 0

    args = [q, k, v, bias]
    in_specs = [
        pl.BlockSpec((seqs, rows, qw), lambda s, i: (s, i, 0)),
        pl.BlockSpec((pl.Element(seqs), pl.Element(kv_rows), pl.Element(kw)), kv_index),
        pl.BlockSpec((pl.Element(seqs), pl.Element(kv_rows), pl.Element(kw)), kv_index),
        pl.BlockSpec(bias.shape, lambda s, i: (0, 0, 0, 0)),
    ]
    if sink is not None:
        args.append(sink)
        in_specs.append(pl.BlockSpec(sink.shape, lambda s, i: (0, 0, 0)))
    out_shape = [jax.ShapeDtypeStruct((n_seq, L, qw), jnp.bfloat16)]
    out_specs = [pl.BlockSpec((seqs, rows, qw), lambda s, i: (s, i, 0))]
    if want_lse:
        out_shape.append(jax.ShapeDtypeStruct((n_seq, L, LANES), jnp.float32))
        out_specs.append(pl.BlockSpec((seqs, rows, LANES), lambda s, i: (s, i, 0)))

    kern = functools.partial(
        _attn_kernel, n_chunks=qw // LANES, kv_chunks=kw // LANES,
        heads_per_stack=bias.shape[2] // Q_TILE, tk=tk, half_w=half_w, seq_len=L,
        rows=rows, has_sink=sink is not None, want_lse=want_lse)
    return pl.pallas_call(
        kern,
        out_shape=out_shape,
        grid=(n_seq // seqs, L // rows),
        in_specs=in_specs,
        out_specs=out_specs,
        compiler_params=pltpu.CompilerParams(
            dimension_semantics=("arbitrary", "arbitrary"), vmem_limit_bytes=VMEM_LIMIT),
        name=name,
    )(*args)


def _out_proj_router_kernel(*refs):
    nb = len(B_DILS)
    oa_ref = refs[0]
    o_refs = refs[1:1 + nb]
    lse_refs = refs[1 + nb:1 + 2 * nb]
    (x_ref, wo_ref, g_ref, wr_ref, br_ref,
     x1_ref, hf_ref, topi_ref, gate_ref, rank_ref, cnt_ref,
     tri_ref, carry_ref, so_ref, sl_ref, wo_bf_ref) = refs[1 + 2 * nb:]
    i = pl.program_id(0)
    rows = x_ref.shape[0]

    @pl.when(i == 0)
    def _():
        a = lax.broadcasted_iota(jnp.int32, (rows, rows), 0)
        b = lax.broadcasted_iota(jnp.int32, (rows, rows), 1)
        tri_ref[...] = jnp.where(a <= b, 1.0, 0.0).astype(jnp.bfloat16)
        carry_ref[...] = jnp.zeros_like(carry_ref)
        wo_bf_ref[...] = wo_ref[...].astype(jnp.bfloat16)

    outs, lses = [], []
    for bi, dil in enumerate(B_DILS):
        if dil == 1:
            outs.append(o_refs[bi][...].astype(jnp.float32))
            lses.append(lse_refs[bi][...])
        else:
            for res in range(dil):
                for j in range(B_W // LANES):
                    so_ref[bi, j, pl.ds(res, rows // dil, stride=dil), :] = (
                        o_refs[bi][0, res, :, j * LANES:(j + 1) * LANES].astype(jnp.float32))
                sl_ref[bi, pl.ds(res, rows // dil, stride=dil), :] = lse_refs[bi][0, res]
            outs.append(jnp.concatenate([so_ref[bi, j] for j in range(B_W // LANES)], axis=-1))
            lses.append(sl_ref[bi])

    mx = functools.reduce(jnp.maximum, lses)
    es = [jnp.exp2(l - mx) for l in lses]
    inv = 1.0 / functools.reduce(lambda a, b: a + b, es)
    eh = lax.broadcasted_iota(jnp.int32, (LANES, B_W), 0)
    ej = lax.broadcasted_iota(jnp.int32, (LANES, B_W), 1) // HEAD_DIM
    expand = jnp.where(eh == ej, 1.0, 0.0).astype(jnp.bfloat16)
    ob = jnp.zeros((rows, B_W), jnp.float32)
    for e, o in zip(es, outs):
        w = e * inv
        wide = jnp.dot(w.astype(jnp.bfloat16), expand, preferred_element_type=jnp.float32)
        ob = ob + wide * o

    attn = jnp.concatenate([oa_ref[...], ob.astype(jnp.bfloat16)], axis=-1)
    x1 = x_ref[...] + jnp.dot(attn, wo_bf_ref[...], preferred_element_type=jnp.float32)
    x1_ref[...] = x1
    hf = x1 * lax.rsqrt(jnp.mean(x1 * x1, axis=-1, keepdims=True) + NORM_EPS) * g_ref[...]
    hf_hi = hf.astype(jnp.bfloat16)
    hf_ref[...] = _pack_halves(hf)
    hf_lo = (hf - hf_hi.astype(jnp.float32)).astype(jnp.bfloat16)

    nt = (((1,), (1,)), ((), ()))
    lg_hi = lax.dot_general(wr_ref[...], hf_hi, nt, preferred_element_type=jnp.float32)
    lg_lo = lax.dot_general(wr_ref[0:N_EXPERTS, :], hf_lo, nt, preferred_element_type=jnp.float32)
    logits = lg_hi[0:N_EXPERTS] + lg_hi[N_EXPERTS:] + lg_lo + br_ref[:, 0:1]


    eidx = lax.broadcasted_iota(jnp.int32, (N_EXPERTS, rows), 0)
    work = logits
    vals, sels = [], []
    for k in range(TOP_K):
        mk = jnp.max(work, axis=0, keepdims=True)
        ik = jnp.min(jnp.where(work == mk, eidx, N_EXPERTS), axis=0, keepdims=True)
        sel = eidx == ik
        work = jnp.where(sel, -jnp.inf, work)
        vals.append(mk)
        sels.append(sel)
        topi_ref[k:k + 1, :] = ik
    exps = [jnp.exp(vk - vals[0]) for vk in vals]
    denom = exps[0] + exps[1] + exps[2] + exps[3]
    ginv = 1.0 / denom
    for k in range(TOP_K):
        gate_ref[k:k + 1, :] = exps[k] * ginv

    onehot = jnp.zeros((N_EXPERTS, rows), jnp.float32)
    for sel in sels:
        onehot = onehot + jnp.where(sel, 1.0, 0.0)
    incl = jnp.dot(onehot.astype(jnp.bfloat16), tri_ref[...], preferred_element_type=jnp.float32)
    before = incl - onehot + carry_ref[:, 0:1]
    for k in range(TOP_K):
        rk = jnp.sum(jnp.where(sels[k], before, 0.0), axis=0, keepdims=True)
        rank_ref[k:k + 1, :] = rk.astype(jnp.int32)
    carry = carry_ref[...] + jnp.sum(onehot, axis=1, keepdims=True)
    carry_ref[...] = carry
    cnt_ref[...] = carry.astype(jnp.int32)


def _out_proj_router(oa, outs_b, lses_b, x2, wo_bf, g, wr, br, seq):
    n = x2.shape[0]
    rows = PROJ_ROWS
    steps = seq // rows
    row_spec = lambda w: pl.BlockSpec((rows, w), lambda i: (i, 0))
    full = lambda a: pl.BlockSpec(a.shape, lambda i: (0,) * a.ndim)
    col_spec = pl.BlockSpec((TOP_K, rows), lambda i: (0, i))

    def branch_spec(dil, w):
        if dil == 1:
            return row_spec(w)
        return pl.BlockSpec((1, dil, rows // dil, w), lambda i: (i // steps, 0, i % steps, 0))

    in_specs = ([row_spec(A_Q_W)]
                + [branch_spec(d, B_W) for d in B_DILS]
                + [branch_spec(d, LANES) for d in B_DILS]
                + [row_spec(D_MODEL), full(wo_bf), full(g), full(wr), full(br)])
    return pl.pallas_call(
        _out_proj_router_kernel,
        out_shape=[
            jax.ShapeDtypeStruct((n, D_MODEL), jnp.float32),
            jax.ShapeDtypeStruct((n, HALF_D), jnp.uint32),
            jax.ShapeDtypeStruct((TOP_K, n), jnp.int32),
            jax.ShapeDtypeStruct((TOP_K, n), jnp.float32),
            jax.ShapeDtypeStruct((TOP_K, n), jnp.int32),
            jax.ShapeDtypeStruct((N_EXPERTS, LANES), jnp.int32),
        ],
        grid=(n // rows,),
        in_specs=in_specs,
        out_specs=[row_spec(D_MODEL), row_spec(HALF_D), col_spec, col_spec, col_spec,
                   pl.BlockSpec((N_EXPERTS, LANES), lambda i: (0, 0))],
        scratch_shapes=[pltpu.VMEM((rows, rows), jnp.bfloat16),
                        pltpu.VMEM((N_EXPERTS, LANES), jnp.float32),
                        pltpu.VMEM((len(B_DILS), B_W // LANES, rows, LANES), jnp.float32),
                        pltpu.VMEM((len(B_DILS), rows, LANES), jnp.float32),
                        pltpu.VMEM(wo_bf.shape, jnp.bfloat16)],
        compiler_params=pltpu.CompilerParams(
            dimension_semantics=("arbitrary",), vmem_limit_bytes=VMEM_LIMIT),
        name="out_proj_router",
    )(oa, *outs_b, *lses_b, x2, wo_bf, g, wr, br)


def _mxu_dot(a_bf, w_f32):
    return lax.dot_general(a_bf, w_f32, (((1,), (0,)), ((), ())), preferred_element_type=jnp.float32)


def _moe_kernel(blk_exp_ref, first_ref, slot_ref, next_exp_ref, n_used_ref,
                x_ref, w1_hbm, b1_ref, w2_hbm, b2_ref, y_ref, w1_buf, w2_buf, sem):
    step = pl.program_id(0)

    def weight_copies(expert, slot):
        return (pltpu.make_async_copy(w1_hbm.at[expert], w1_buf.at[slot], sem.at[slot, 0]),
                pltpu.make_async_copy(w2_hbm.at[expert], w2_buf.at[slot], sem.at[slot, 1]))

    @pl.when(step * MOE_STEP_BLOCKS < n_used_ref[0])
    def _():
        @pl.when(step == 0)
        def _():
            for cp in weight_copies(blk_exp_ref[0], slot_ref[0]):
                cp.start()

        for j in range(MOE_STEP_BLOCKS):
            i = step * MOE_STEP_BLOCKS + j

            @pl.when(first_ref[i] == 1)
            def _():
                slot = slot_ref[i]
                for cp in weight_copies(blk_exp_ref[i], slot):
                    cp.wait()

                @pl.when(next_exp_ref[i] >= 0)
                def _():
                    for cp in weight_copies(next_exp_ref[i], (slot + 1) % MOE_WEIGHT_BUFS):
                        cp.start()

        for j in range(MOE_STEP_BLOCKS):
            i = step * MOE_STEP_BLOCKS + j
            slot = slot_ref[i]
            expert = blk_exp_ref[i]
            rs = slice(j * MOE_ROWS, (j + 1) * MOE_ROWS)
            x = jnp.concatenate(_unpack_halves(x_ref[rs, :]), axis=-1).astype(jnp.bfloat16)
            acc = jnp.zeros((MOE_ROWS, D_MODEL), jnp.float32)
            for c in range(D_FF // FF_CHUNK):
                lo = c * FF_CHUNK
                glu = _mxu_dot(x, w1_buf[slot, :, lo:lo + FF_CHUNK]) + b1_ref[expert, :, lo:lo + FF_CHUNK]
                lin = (_mxu_dot(x, w1_buf[slot, :, D_FF + lo:D_FF + lo + FF_CHUNK])
                       + b1_ref[expert, :, D_FF + lo:D_FF + lo + FF_CHUNK])
                glu = jnp.minimum(glu, SWIGLU_LIMIT)
                lin = jnp.clip(lin, -SWIGLU_LIMIT, SWIGLU_LIMIT)
                act = glu * (1.0 / (1.0 + jnp.exp(-SWIGLU_ALPHA * glu))) * (lin + 1.0)
                acc = acc + _mxu_dot(act.astype(jnp.bfloat16), w2_buf[slot, lo:lo + FF_CHUNK, :])
            y_ref[rs, :] = _pack_halves(acc + b2_ref[expert])


def _moe_plan(pends, n_blk):
    g = MOE_ROWS
    blk_row0 = jnp.arange(n_blk, dtype=jnp.int32) * g
    blk_exp = jnp.minimum(jnp.sum(pends[None, :] <= blk_row0[:, None], axis=-1),
                          N_EXPERTS - 1).astype(jnp.int32)
    n_used = (pends[-1] // g).astype(jnp.int32)
    used = blk_row0 < pends[-1]
    prev_exp = jnp.concatenate([jnp.full((1,), -1, jnp.int32), blk_exp[:-1]])
    first = (used & (blk_exp != prev_exp)).astype(jnp.int32)
    slot = ((jnp.cumsum(first) - 1) % MOE_WEIGHT_BUFS).astype(jnp.int32)
    pstarts = jnp.concatenate([jnp.zeros((1,), pends.dtype), pends[:-1]])
    nonempty = pends > pstarts
    experts = jnp.arange(N_EXPERTS, dtype=jnp.int32)
    later = nonempty[None, :] & (experts[None, :] > experts[:, None])
    next_nonempty = jnp.min(jnp.where(later, experts[None, :], N_EXPERTS), axis=-1)
    next_nonempty = jnp.where(next_nonempty == N_EXPERTS, -1, next_nonempty).astype(jnp.int32)
    next_exp = jnp.sum(jnp.where(blk_exp[:, None] == experts[None, :], next_nonempty[None, :], 0),
                       axis=-1).astype(jnp.int32)
    return blk_exp, first, slot, next_exp, n_used.reshape(1)


def _moe_experts(plan, xb, w1, b1, w2, b2):
    n_rows = xb.shape[0]
    n_blk = n_rows // MOE_ROWS

    step_rows = MOE_STEP_BLOCKS * MOE_ROWS

    def blk(i, *p):
        return jnp.minimum(i, (p[-1][0] - 1) // MOE_STEP_BLOCKS)

    grid_spec = pltpu.PrefetchScalarGridSpec(
        num_scalar_prefetch=len(plan),
        grid=(n_blk // MOE_STEP_BLOCKS,),
        in_specs=[
            pl.BlockSpec((step_rows, HALF_D), lambda i, *p: (blk(i, *p), 0)),
            pl.BlockSpec(memory_space=pl.ANY),
            pl.BlockSpec(b1.shape, lambda i, *p: (0, 0, 0)),
            pl.BlockSpec(memory_space=pl.ANY),
            pl.BlockSpec(b2.shape, lambda i, *p: (0, 0, 0)),
        ],
        out_specs=pl.BlockSpec((step_rows, HALF_D), lambda i, *p: (blk(i, *p), 0)),
        scratch_shapes=[pltpu.VMEM((MOE_WEIGHT_BUFS, D_MODEL, 2 * D_FF), jnp.float32),
                        pltpu.VMEM((MOE_WEIGHT_BUFS, D_FF, D_MODEL), jnp.float32),
                        pltpu.SemaphoreType.DMA((MOE_WEIGHT_BUFS, 2))],
    )
    return pl.pallas_call(
        _moe_kernel,
        out_shape=jax.ShapeDtypeStruct((n_rows, HALF_D), jnp.uint32),
        grid_spec=grid_spec,
        compiler_params=pltpu.CompilerParams(
            dimension_semantics=("arbitrary",), vmem_limit_bytes=MOE_VMEM_LIMIT),
        name="moe_experts",
    )(*plan, xb, w1, b1, w2, b2)


def _combine_kernel(x1_ref, yg_ref, gate_ref, o_ref):
    acc_lo = x1_ref[:, :HALF_D]
    acc_hi = x1_ref[:, HALF_D:]
    rows = x1_ref.shape[0]
    gates = jnp.concatenate([gate_ref[...], jnp.zeros((8 - TOP_K, rows), jnp.float32)], axis=0).T
    for k in range(TOP_K):
        lo, hi = _unpack_halves(yg_ref[k])
        gk = gates[:, k:k + 1]
        acc_lo = acc_lo + lo * gk
        acc_hi = acc_hi + hi * gk
    o_ref[:, :HALF_D] = acc_lo
    o_ref[:, HALF_D:] = acc_hi


def _combine(acc, yg, gates_nk, split):
    n = acc.shape[0]
    rows = PROJ_ROWS
    steps = yg.shape[1] // rows
    first = split * steps
    return pl.pallas_call(
        _combine_kernel,
        out_shape=jax.ShapeDtypeStruct((n, D_MODEL), jnp.float32),
        grid=(steps,),
        in_specs=[pl.BlockSpec((rows, D_MODEL), lambda i: (first + i, 0)),
                  pl.BlockSpec((TOP_K, rows, HALF_D), lambda i: (0, i, 0)),
                  pl.BlockSpec((TOP_K, rows), lambda i: (0, first + i))],
        out_specs=pl.BlockSpec((rows, D_MODEL), lambda i: (first + i, 0)),
        input_output_aliases={0: 0},
        compiler_params=pltpu.CompilerParams(
            dimension_semantics=("arbitrary",), vmem_limit_bytes=VMEM_LIMIT),
        name="moe_combine",
    )(acc, yg, gates_nk)


def _sc_worker_id():
    return lax.axis_index("s") * SC_CORES + lax.axis_index("c")


def _sc_dispatch(hf, dest3, n_rows):
    n = hf.shape[0]
    chunks_per_worker = n // SC_CHUNK // SC_WORKERS
    mesh = plsc.VectorSubcoreMesh(core_axis_name="c", subcore_axis_name="s")

    @functools.partial(
        pl.kernel, mesh=mesh,
        out_type=jax.ShapeDtypeStruct((n_rows, HALF_D), hf.dtype),
        scratch_types=[pltpu.VMEM((2, TOP_K, SC_CHUNK), jnp.int32),
                       pltpu.VMEM((2, SC_CHUNK, HALF_D), hf.dtype),
                       pltpu.SemaphoreType.DMA((2,)),
                       pltpu.SemaphoreType.DMA((2,))],
        name="sc_dispatch")
    def run(hf_hbm, dest_hbm, xb_hbm, idx_v, rows_v, load_sem, scatter_sem):
        first = _sc_worker_id() * chunks_per_worker

        def load(j):
            slot = j % 2
            pltpu.sync_copy(dest_hbm.at[first + j], idx_v.at[slot])
            return pltpu.async_copy(hf_hbm.at[pl.ds((first + j) * SC_CHUNK, SC_CHUNK)],
                                    rows_v.at[slot], load_sem.at[slot])

        loads = {0: load(0)}
        scatters = {}
        for j in range(chunks_per_worker):
            slot = j % 2
            loads.pop(j).wait()
            scatters[j] = [pltpu.async_copy(rows_v.at[slot], xb_hbm.at[idx_v.at[slot, k]],
                                            scatter_sem.at[slot]) for k in range(TOP_K)]
            if j >= 1:
                for cp in scatters.pop(j - 1):
                    cp.wait()
            if j + 1 < chunks_per_worker:
                loads[j + 1] = load(j + 1)
        for cp in scatters.pop(chunks_per_worker - 1):
            cp.wait()

    return run(hf, dest3)


def _sc_collect(y, dest3):
    n = dest3.shape[0] * SC_COLLECT_CHUNK
    chunks_per_worker = n // SC_COLLECT_CHUNK // SC_WORKERS
    mesh = plsc.VectorSubcoreMesh(core_axis_name="c", subcore_axis_name="s")

    @functools.partial(
        pl.kernel, mesh=mesh,
        out_type=jax.ShapeDtypeStruct((TOP_K, n, HALF_D), y.dtype),
        scratch_types=[pltpu.VMEM((TOP_K, SC_COLLECT_CHUNK), jnp.int32),
                       pltpu.VMEM((TOP_K, SC_COLLECT_CHUNK, HALF_D), y.dtype),
                       pltpu.SemaphoreType.DMA((TOP_K,)),
                       pltpu.SemaphoreType.DMA((TOP_K,))],
        name="sc_collect")
    def run(y_hbm, dest_hbm, yg_hbm, idx_v, rows_v, gather_sem, write_sem):
        first = _sc_worker_id() * chunks_per_worker

        @pl.loop(0, chunks_per_worker)
        def _(j):
            ch = first + j
            pltpu.sync_copy(dest_hbm.at[ch], idx_v)
            gathers = [pltpu.async_copy(y_hbm.at[idx_v.at[k]], rows_v.at[k], gather_sem.at[k])
                       for k in range(TOP_K)]
            writes = []
            for k in range(TOP_K):
                gathers[k].wait()
                writes.append(pltpu.async_copy(
                    rows_v.at[k], yg_hbm.at[k, pl.ds(ch * SC_COLLECT_CHUNK, SC_COLLECT_CHUNK)],
                    write_sem.at[k]))
            for w in writes:
                w.wait()

    return run(y, dest3)


def _layer(x2, batch, seq, attn_norm_g, w_in, a_q_g, a_k_g, a_sinks, b_q_g, b_k_g, w_out,
           ffn_norm_g, w_router, b_router, w1, b1, w2, b2):
    n = x2.shape[0]
    slopes = _alibi_slopes()
    q_scale = HEAD_DIM ** -0.5 * LOG2E
    reps = MXU_DIM // HEAD_DIM
    gains = jnp.stack([jnp.tile(a_q_g, reps) * q_scale, jnp.tile(a_k_g, reps),
                       jnp.tile(b_q_g, reps) * q_scale, jnp.tile(b_k_g, reps)]).astype(jnp.float32)

    proj = _in_proj(x2, attn_norm_g.reshape(1, -1), w_in, gains, batch, seq)
    qa, ka, va = proj[:3]
    nb = len(B_DILS)
    qbs, kbs, vbs = proj[3:3 + nb], proj[3 + nb:3 + 2 * nb], proj[3 + 2 * nb:]

    bias_a = _bias_tables(slopes[:A_Q_HEADS], A_STACK_HEADS, A_HALF_WINDOW, 1, Q_TILE + 2 * A_HALF_WINDOW)
    sink_col = jnp.repeat(a_sinks.astype(jnp.float32) * LOG2E, Q_TILE).reshape(
        A_Q_HEADS // A_STACK_HEADS, A_STACK_HEADS * Q_TILE, 1)
    as_seqs = lambda a: a.reshape(batch, seq, a.shape[-1])
    out_a = _banded_attention(as_seqs(qa), as_seqs(ka), as_seqs(va), bias_a, half_w=A_HALF_WINDOW,
                              sink=sink_col, name="attn_a")[0].reshape(n, A_Q_W)

    outs_b, lses_b = [], []
    for bi, (window, dil) in enumerate(B_BRANCHES):
        half_w = window // (2 * dil)
        bias_b = _bias_tables(slopes[A_Q_HEADS:], B_STACK_HEADS, half_w, dil, Q_TILE + 2 * half_w)
        L = seq // dil
        to_seqs = lambda a: a.reshape(batch * dil, L, a.shape[-1])
        o, lse = _banded_attention(to_seqs(qbs[bi]), to_seqs(kbs[bi]), to_seqs(vbs[bi]), bias_b,
                                   half_w=half_w, want_lse=True, name=f"attn_b_d{dil}")
        if dil == 1:
            outs_b.append(o.reshape(n, B_W))
            lses_b.append(lse.reshape(n, LANES))
        else:
            outs_b.append(o.reshape(batch, dil, L, B_W))
            lses_b.append(lse.reshape(batch, dil, L, LANES))

    wr_t = w_router.T.astype(jnp.float32)
    wr_hi = wr_t.astype(jnp.bfloat16)
    wr_lo = (wr_t - wr_hi.astype(jnp.float32)).astype(jnp.bfloat16)
    wr = jnp.concatenate([wr_hi, wr_lo], axis=0)
    br = jnp.broadcast_to(b_router.astype(jnp.float32)[:, None], (N_EXPERTS, LANES))
    x1, hf, topi, gates, ranks, counts = _out_proj_router(
        out_a, outs_b, lses_b, x2, w_out, ffn_norm_g.reshape(1, -1), wr, br, seq)

    g = MOE_ROWS
    nk = n * TOP_K
    n_rows = nk + N_EXPERTS * g
    cnt = counts[:, 0]
    pcnt = (cnt + g - 1) // g * g
    pends = jnp.cumsum(pcnt)
    pstarts = pends - pcnt
    experts = jnp.arange(N_EXPERTS, dtype=jnp.int32)
    start_of = jnp.sum(jnp.where(topi[:, :, None] == experts, pstarts, 0), axis=-1)
    dest = (start_of + ranks).astype(jnp.int32)
    plan = _moe_plan(pends, n_rows // g)
    dest3 = dest.reshape(TOP_K, n // SC_CHUNK, SC_CHUNK).transpose(1, 0, 2)

    xb = _sc_dispatch(hf, dest3, n_rows)
    y = _moe_experts(plan, xb, w1, b1[:, None, :], w2, b2[:, None, :])
    per_split = n // COMBINE_SPLITS
    dest_c = dest.reshape(TOP_K, n // SC_COLLECT_CHUNK, SC_COLLECT_CHUNK).transpose(1, 0, 2)
    chunks_per_split = per_split // SC_COLLECT_CHUNK
    out = x1
    for s in range(COMBINE_SPLITS):
        yg = _sc_collect(y, dest_c[s * chunks_per_split:(s + 1) * chunks_per_split])
        out = _combine(out, yg, gates, s)
    return out


def kernel(x, attn_norm_g, w_in, a_q_norm_g, a_k_norm_g, a_sinks, b_q_norm_g, b_k_norm_g, w_out,
           ffn_norm_g, w_router, b_router, w1, b1, w2, b2):
    batch, seq, d = x.shape
    x2 = x.reshape(batch * seq, d)
    for i in range(attn_norm_g.shape[0]):
        x2 = _layer(x2, batch, seq, attn_norm_g[i], w_in[i], a_q_norm_g[i], a_k_norm_g[i],
                    a_sinks[i], b_q_norm_g[i], b_k_norm_g[i], w_out[i], ffn_norm_g[i],
                    w_router[i], b_router[i], w1[i], b1[i], w2[i], b2[i])
    return x2.reshape(batch, seq, d)
```

```python
import functools

import jax
import jax.numpy as jnp
import numpy as np
from jax import lax
from jax.experimental import pallas as pl
from jax.experimental.pallas import tpu as pltpu
from jax.experimental.pallas import tpu_sc as plsc

D_MODEL = 1024
HALF_D = D_MODEL // 2
HEAD_DIM = 64
LANES = 128
MXU_DIM = 256
A_Q_HEADS = 8
A_KV_HEADS = 2
B_HEADS = 8
A_HALF_WINDOW = 128
B_BRANCHES = ((128, 1), (512, 4), (2048, 16))
B_DILS = tuple(d for _, d in B_BRANCHES)
RESIDUE_STRIDE = 4
N_ALIBI_HEADS = 16
A_Q_W = A_Q_HEADS * HEAD_DIM
A_KV_W = A_KV_HEADS * HEAD_DIM
B_W = B_HEADS * HEAD_DIM
N_EXPERTS = 32
TOP_K = 4
D_FF = 1024
SWIGLU_ALPHA = 1.702
SWIGLU_LIMIT = 7.0
NORM_EPS = 1e-5
MASK_VALUE = -1e30
LOG2E = 1.4426950408889634

Q_TILE = 128
ATTN_STEP_ROWS = 2048
A_STACK_HEADS = 4
B_STACK_HEADS = 2
PROJ_ROWS = 1024
IN_PROJ_ROWS = 1024
MOE_ROWS = 512
FF_CHUNK = 512
MOE_STEP_BLOCKS = 2
MOE_WEIGHT_BUFS = 3
VMEM_LIMIT = 48 * 1024 * 1024
LARGE_VMEM_LIMIT = 58 * 1024 * 1024
SC_CORES = 2
SC_SUBCORES = 16
SC_WORKERS = SC_CORES * SC_SUBCORES
SC_CHUNK = 64
SC_COLLECT_CHUNK = 32
COMBINE_SPLITS = 8


def _pack_halves(v):
    lo = v[:, :HALF_D].astype(jnp.bfloat16).astype(jnp.float32)
    hi = v[:, HALF_D:].astype(jnp.bfloat16).astype(jnp.float32)
    return (pltpu.bitcast(lo, jnp.uint32) >> 16) | pltpu.bitcast(hi, jnp.uint32)


def _unpack_halves(w):
    lo = pltpu.bitcast(w << 16, jnp.float32)
    hi = pltpu.bitcast(w & jnp.uint32(0xFFFF0000), jnp.float32)
    return lo, hi


def _alibi_slopes():
    return np.exp2(-8.0 * np.arange(1, N_ALIBI_HEADS + 1, dtype=np.float32) / N_ALIBI_HEADS).astype(np.float32)


def _bias_tables(head_slopes, heads_per_group, half_w, dist_scale, tk):
    i = np.arange(Q_TILE)[:, None]
    j = np.arange(tk)[None, :]
    tabs = []
    for shift in (0, half_w, tk - Q_TILE):
        dist = np.abs(j - shift - i)
        valid = dist <= half_w
        per_head = []
        for sl in head_slopes:
            b = (-np.float64(sl) * LOG2E * (dist * dist_scale)).astype(np.float32)
            per_head.append(np.where(valid, b, np.float32(MASK_VALUE)).astype(np.float32))
        t = np.stack(per_head).reshape(-1, heads_per_group * Q_TILE, tk)
        tabs.append(t)
    return jnp.asarray(np.stack(tabs))


def _in_proj_kernel(x_ref, g_ref, w_hbm, gains_ref, qa_ref, ka_ref, va_ref, *rest):
    b_refs, (scr_ref, scr2_ref, w_ref, stage_ref, sem) = rest[:-5], rest[-5:]

    @pl.when(pl.program_id(0) == 0)
    def _():
        width = stage_ref.shape[1]
        for c0 in range(0, w_ref.shape[1], width):
            cp = pltpu.make_async_copy(w_hbm.at[:, c0:c0 + width], stage_ref, sem)
            cp.start()
            cp.wait()
            w_ref[:, c0:c0 + width] = stage_ref[...].astype(jnp.bfloat16)

    x = x_ref[...]
    xn = x * lax.rsqrt(jnp.mean(x * x, axis=-1, keepdims=True) + NORM_EPS) * g_ref[...]
    xn = xn.astype(jnp.bfloat16)
    r = lax.broadcasted_iota(jnp.int32, (MXU_DIM, MXU_DIM), 0) // HEAD_DIM
    c = lax.broadcasted_iota(jnp.int32, (MXU_DIM, MXU_DIM), 1) // HEAD_DIM
    blockdiag = jnp.where(r == c, 1.0, 0.0).astype(jnp.bfloat16)

    def head_rms(sec, gain_row):
        width = sec.shape[1]
        parts = []
        step = min(width, MXU_DIM)
        for j in range(width // step):
            p = sec[:, j * step:(j + 1) * step]
            ss = jnp.dot((p * p).astype(jnp.bfloat16), blockdiag[:step, :step],
                         preferred_element_type=jnp.float32)
            parts.append(p * lax.rsqrt(ss * (1.0 / HEAD_DIM) + NORM_EPS)
                         * gains_ref[gain_row:gain_row + 1, :step])
        return parts

    def project(col0, width, gain_row):
        sec = jnp.dot(xn, w_ref[:, col0:col0 + width], preferred_element_type=jnp.float32)
        return [sec] if gain_row is None else head_rms(sec, gain_row)

    def store(out_ref, parts):
        w = parts[0].shape[1]
        for j, p in enumerate(parts):
            out_ref[:, j * w:(j + 1) * w] = p.astype(out_ref.dtype)

    def per_kv_head(p):
        lane = lax.broadcasted_iota(jnp.int32, p.shape, 1)
        swapped = pltpu.roll(p, HEAD_DIM, axis=1)
        low = lane < HEAD_DIM
        return [jnp.where(low, p, swapped), jnp.where(low, swapped, p)]

    store(qa_ref, project(0, A_Q_W, 0))
    kva = project(A_Q_W, 2 * A_KV_W, None)[0]
    store(ka_ref, per_kv_head(head_rms(kva[:, :A_KV_W], 1)[0]))
    store(va_ref, per_kv_head(kva[:, A_KV_W:]))

    rows = x_ref.shape[0]
    col0 = A_Q_W + 2 * A_KV_W
    for t, gain_row in enumerate((2, 3, None)):
        parts = project(col0 + t * B_W, B_W, gain_row)
        sec = jnp.concatenate(parts, axis=-1) if len(parts) > 1 else parts[0]
        for j in range(B_W // LANES):
            scr_ref[j] = sec[:, j * LANES:(j + 1) * LANES]
        n_lane_chunks = B_W // LANES
        prev_dil = 1
        for bi, dil in enumerate(B_DILS):
            out_ref = b_refs[t * len(B_DILS) + bi]
            if dil == 1:
                out_ref[...] = sec.astype(out_ref.dtype)
                continue
            assert dil == prev_dil * RESIDUE_STRIDE
            last = dil == B_DILS[-1]
            for res in range(dil):
                r_prev, r_sub = res % prev_dil, res // prev_dil
                for j in range(n_lane_chunks):
                    if prev_dil == 1:
                        v = scr_ref[j, pl.ds(r_sub, rows // dil, stride=RESIDUE_STRIDE), :]
                    else:
                        v = scr2_ref[r_prev * n_lane_chunks + j,
                                     pl.ds(r_sub, rows // dil, stride=RESIDUE_STRIDE), :]
                    out_ref[0, res, :, j * LANES:(j + 1) * LANES] = v.astype(out_ref.dtype)
                    if not last:
                        scr2_ref[res * n_lane_chunks + j] = v
            prev_dil = dil


def _in_proj(x2, g, w_in, gains, batch, seq):
    n = x2.shape[0]
    rows = IN_PROJ_ROWS
    steps = seq // rows
    a_widths = (A_Q_W, 2 * A_KV_W, 2 * A_KV_W)
    out_shape = [jax.ShapeDtypeStruct((n, w), jnp.bfloat16) for w in a_widths]
    out_specs = [pl.BlockSpec((rows, w), lambda i: (i, 0)) for w in a_widths]
    for _ in range(3):
        for dil in B_DILS:
            if dil == 1:
                out_shape.append(jax.ShapeDtypeStruct((n, B_W), jnp.bfloat16))
                out_specs.append(pl.BlockSpec((rows, B_W), lambda i: (i, 0)))
            else:
                out_shape.append(jax.ShapeDtypeStruct((batch, dil, seq // dil, B_W), jnp.bfloat16))
                out_specs.append(pl.BlockSpec((1, dil, rows // dil, B_W),
                                              lambda i: (i // steps, 0, i % steps, 0)))
    return pl.pallas_call(
        _in_proj_kernel,
        out_shape=out_shape,
        grid=(n // rows,),
        in_specs=[
            pl.BlockSpec((rows, D_MODEL), lambda i: (i, 0)),
            pl.BlockSpec((1, D_MODEL), lambda i: (0, 0)),
            pl.BlockSpec(memory_space=pl.ANY),
            pl.BlockSpec(gains.shape, lambda i: (0, 0)),
        ],
        out_specs=out_specs,
        scratch_shapes=[pltpu.VMEM((B_W // LANES, rows, LANES), jnp.float32),
                        pltpu.VMEM((RESIDUE_STRIDE * B_W // LANES, rows // RESIDUE_STRIDE, LANES),
                                   jnp.float32),
                        pltpu.VMEM(w_in.shape, jnp.bfloat16),
                        pltpu.VMEM((w_in.shape[0], w_in.shape[1] // 3), jnp.float32),
                        pltpu.SemaphoreType.DMA],
        compiler_params=pltpu.CompilerParams(
            dimension_semantics=("arbitrary",), vmem_limit_bytes=LARGE_VMEM_LIMIT),
        name="in_proj",
    )(x2, g, w_in, gains)


def _attn_kernel(*refs, n_chunks, kv_chunks, heads_per_stack, tk, half_w, seq_len, rows, has_sink,
                 want_lse):
    it = iter(refs)
    q_ref, k_ref, v_ref, bias_ref = next(it), next(it), next(it), next(it)
    sink_ref = next(it) if has_sink else None
    o_ref = next(it)
    lse_ref = next(it) if want_lse else None

    n_tiles = seq_len // Q_TILE
    tiles_per_step = rows // Q_TILE
    chunks_per_group = n_chunks // kv_chunks
    assert (2 * chunks_per_group) % heads_per_stack == 0
    step = pl.program_id(1)
    lane = lax.broadcasted_iota(jnp.int32, (Q_TILE, LANES), 1)
    low_half = lane < HEAD_DIM
    ones = jnp.ones((tk, LANES), jnp.bfloat16)

    for sq, t in [(a, b) for a in range(q_ref.shape[0]) for b in range(tiles_per_step)]:
        tile = step * tiles_per_step + t
        q0 = tile * Q_TILE
        kv_rows = k_ref.shape[1]
        kv_row0 = jnp.clip(step * rows - half_w, 0, seq_len - kv_rows)
        start = pl.multiple_of(jnp.clip(q0 - half_w, 0, seq_len - tk) - kv_row0, HEAD_DIM)
        variant = jnp.where(tile == 0, 0, jnp.where(tile == n_tiles - 1, 2, 1))
        r0 = t * Q_TILE
        lse_tile = jnp.zeros((Q_TILE, LANES), jnp.float32)
        head_o, head_lse = {}, {}
        for g in range(2 * n_chunks // heads_per_stack):
            heads = range(g * heads_per_stack, (g + 1) * heads_per_stack)
            kv = (heads[0] // 2) // chunks_per_group
            kc = k_ref[sq, pl.ds(start, tk), kv * LANES:(kv + 1) * LANES]
            vc = v_ref[sq, pl.ds(start, tk), kv * LANES:(kv + 1) * LANES]
            v_aug = jnp.concatenate([vc, ones], axis=1)
            q_parts = []
            for h in heads:
                c = h // 2
                q2 = q_ref[sq, r0:r0 + Q_TILE, c * LANES:(c + 1) * LANES]
                keep = low_half if h % 2 == 0 else ~low_half
                q_parts.append(jnp.where(keep, q2, jnp.zeros_like(q2)))
            qs = q_parts[0] if len(q_parts) == 1 else jnp.concatenate(q_parts, axis=0)
            s = lax.dot_general(qs, kc, (((1,), (1,)), ((), ())),
                                preferred_element_type=jnp.float32)
            s = s + bias_ref[variant, g]
            m = jnp.max(s, axis=-1, keepdims=True)
            if has_sink:
                m = jnp.maximum(m, sink_ref[g])
            p = jnp.exp2(s - m)
            ov = jnp.dot(p.astype(jnp.bfloat16), v_aug, preferred_element_type=jnp.float32)
            o, l = ov[:, :LANES], ov[:, LANES:]
            if has_sink:
                l = l + jnp.exp2(sink_ref[g] - m)
            o = o * (1.0 / l)
            if want_lse:
                lse = m + jnp.log(l) * LOG2E
            for idx, h in enumerate(heads):
                head_o[h] = o[idx * Q_TILE:(idx + 1) * Q_TILE]
                if want_lse:
                    head_lse[h] = lse[idx * Q_TILE:(idx + 1) * Q_TILE]
                if h % 2 == 1:
                    c = h // 2
                    o2 = jnp.where(low_half, head_o.pop(h - 1), head_o.pop(h))
                    o_ref[sq, r0:r0 + Q_TILE, c * LANES:(c + 1) * LANES] = o2.astype(o_ref.dtype)
                    if want_lse:
                        lse_tile = jnp.where(lane == h - 1, head_lse.pop(h - 1),
                                             jnp.where(lane == h, head_lse.pop(h), lse_tile))
        if want_lse:
            lse_ref[sq, r0:r0 + Q_TILE, :] = lse_tile


def _banded_attention(q, k, v, bias, *, half_w, sink=None, want_lse=False, name):
    n_seq, L, qw = q.shape
    kw = k.shape[2]
    tk = Q_TILE + 2 * half_w
    rows = min(ATTN_STEP_ROWS, L)
    seqs = ATTN_STEP_ROWS // rows
    kv_rows = min(L, rows + 2 * half_w)

    def kv_index(s, i):
        row0 = pl.multiple_of(jnp.clip(i * rows - half_w, 0, L - kv_rows), HEAD_DIM)
        return s * seqs, row0, 0

    args = [q, k, v, bias]
    in_specs = [
        pl.BlockSpec((seqs, rows, qw), lambda s, i: (s, i, 0)),
        pl.BlockSpec((pl.Element(seqs), pl.Element(kv_rows), pl.Element(kw)), kv_index),
        pl.BlockSpec((pl.Element(seqs), pl.Element(kv_rows), pl.Element(kw)), kv_index),
        pl.BlockSpec(bias.shape, lambda s, i: (0, 0, 0, 0)),
    ]
    if sink is not None:
        args.append(sink)
        in_specs.append(pl.BlockSpec(sink.shape, lambda s, i: (0, 0, 0)))
    out_shape = [jax.ShapeDtypeStruct((n_seq, L, qw), jnp.bfloat16)]
    out_specs = [pl.BlockSpec((seqs, rows, qw), lambda s, i: (s, i, 0))]
    if want_lse:
        out_shape.append(jax.ShapeDtypeStruct((n_seq, L, LANES), jnp.float32))
        out_specs.append(pl.BlockSpec((seqs, rows, LANES), lambda s, i: (s, i, 0)))

    kern = functools.partial(
        _attn_kernel, n_chunks=qw // LANES, kv_chunks=kw // LANES,
        heads_per_stack=bias.shape[2] // Q_TILE, tk=tk, half_w=half_w, seq_len=L,
        rows=rows, has_sink=sink is not None, want_lse=want_lse)
    return pl.pallas_call(
        kern,
        out_shape=out_shape,
        grid=(n_seq // seqs, L // rows),
        in_specs=in_specs,
        out_specs=out_specs,
        compiler_params=pltpu.CompilerParams(
            dimension_semantics=("arbitrary", "arbitrary"), vmem_limit_bytes=VMEM_LIMIT),
        name=name,
    )(*args)


def _out_proj_router_kernel(*refs):
    nb = len(B_DILS)
    oa_ref = refs[0]
    o_refs = refs[1:1 + nb]
    lse_refs = refs[1 + nb:1 + 2 * nb]
    (x_ref, wo_ref, g_ref, wr_ref, br_ref,
     x1_ref, hf_ref, topi_ref, gate_ref, rank_ref, cnt_ref,
     tri_ref, carry_ref, so_ref, sl_ref, wo_bf_ref) = refs[1 + 2 * nb:]
    i = pl.program_id(0)
    rows = x_ref.shape[0]

    @pl.when(i == 0)
    def _():
        a = lax.broadcasted_iota(jnp.int32, (rows, rows), 0)
        b = lax.broadcasted_iota(jnp.int32, (rows, rows), 1)
        tri_ref[...] = jnp.where(a <= b, 1.0, 0.0).astype(jnp.bfloat16)
        carry_ref[...] = jnp.zeros_like(carry_ref)
        wo_bf_ref[...] = wo_ref[...].astype(jnp.bfloat16)

    outs, lses = [], []
    for bi, dil in enumerate(B_DILS):
        if dil == 1:
            outs.append(o_refs[bi][...].astype(jnp.float32))
            lses.append(lse_refs[bi][...])
        else:
            for res in range(dil):
                for j in range(B_W // LANES):
                    so_ref[bi, j, pl.ds(res, rows // dil, stride=dil), :] = (
                        o_refs[bi][0, res, :, j * LANES:(j + 1) * LANES].astype(jnp.float32))
                sl_ref[bi, pl.ds(res, rows // dil, stride=dil), :] = lse_refs[bi][0, res]
            outs.append(jnp.concatenate([so_ref[bi, j] for j in range(B_W // LANES)], axis=-1))
            lses.append(sl_ref[bi])

    mx = functools.reduce(jnp.maximum, lses)
    es = [jnp.exp2(l - mx) for l in lses]
    inv = 1.0 / functools.reduce(lambda a, b: a + b, es)
    eh = lax.broadcasted_iota(jnp.int32, (LANES, B_W), 0)
    ej = lax.broadcasted_iota(jnp.int32, (LANES, B_W), 1) // HEAD_DIM
    expand = jnp.where(eh == ej, 1.0, 0.0).astype(jnp.bfloat16)
    ob = jnp.zeros((rows, B_W), jnp.float32)
    for e, o in zip(es, outs):
        w = e * inv
        wide = jnp.dot(w.astype(jnp.bfloat16), expand, preferred_element_type=jnp.float32)
        ob = ob + wide * o

    attn = jnp.concatenate([oa_ref[...], ob.astype(jnp.bfloat16)], axis=-1)
    x1 = x_ref[...] + jnp.dot(attn, wo_bf_ref[...], preferred_element_type=jnp.float32)
    x1_ref[...] = x1
    hf = x1 * lax.rsqrt(jnp.mean(x1 * x1, axis=-1, keepdims=True) + NORM_EPS) * g_ref[...]
    hf_hi = hf.astype(jnp.bfloat16)
    hf_ref[...] = _pack_halves(hf)
    hf_lo = (hf - hf_hi.astype(jnp.float32)).astype(jnp.bfloat16)

    nt = (((1,), (1,)), ((), ()))
    lg_hi = lax.dot_general(wr_ref[...], hf_hi, nt, preferred_element_type=jnp.float32)
    lg_lo = lax.dot_general(wr_ref[0:N_EXPERTS, :], hf_lo, nt, preferred_element_type=jnp.float32)
    logits = lg_hi[0:N_EXPERTS] + lg_hi[N_EXPERTS:] + lg_lo + br_ref[:, 0:1]


    eidx = lax.broadcasted_iota(jnp.int32, (N_EXPERTS, rows), 0)
    work = logits
    vals, sels = [], []
    for k in range(TOP_K):
        mk = jnp.max(work, axis=0, keepdims=True)
        ik = jnp.min(jnp.where(work == mk, eidx, N_EXPERTS), axis=0, keepdims=True)
        sel = eidx == ik
        work = jnp.where(sel, -jnp.inf, work)
        vals.append(mk)
        sels.append(sel)
        topi_ref[k:k + 1, :] = ik
    exps = [jnp.exp(vk - vals[0]) for vk in vals]
    denom = exps[0] + exps[1] + exps[2] + exps[3]
    ginv = 1.0 / denom
    for k in range(TOP_K):
        gate_ref[k:k + 1, :] = exps[k] * ginv

    onehot = jnp.zeros((N_EXPERTS, rows), jnp.float32)
    for sel in sels:
        onehot = onehot + jnp.where(sel, 1.0, 0.0)
    incl = jnp.dot(onehot.astype(jnp.bfloat16), tri_ref[...], preferred_element_type=jnp.float32)
    before = incl - onehot + carry_ref[:, 0:1]
    for k in range(TOP_K):
        rk = jnp.sum(jnp.where(sels[k], before, 0.0), axis=0, keepdims=True)
        rank_ref[k:k + 1, :] = rk.astype(jnp.int32)
    carry = carry_ref[...] + jnp.sum(onehot, axis=1, keepdims=True)
    carry_ref[...] = carry
    cnt_ref[...] = carry.astype(jnp.int32)


def _out_proj_router(oa, outs_b, lses_b, x2, wo_bf, g, wr, br, seq):
    n = x2.shape[0]
    rows = PROJ_ROWS
    steps = seq // rows
    row_spec = lambda w: pl.BlockSpec((rows, w), lambda i: (i, 0))
    full = lambda a: pl.BlockSpec(a.shape, lambda i: (0,) * a.ndim)
    col_spec = pl.BlockSpec((TOP_K, rows), lambda i: (0, i))

    def branch_spec(dil, w):
        if dil == 1:
            return row_spec(w)
        return pl.BlockSpec((1, dil, rows // dil, w), lambda i: (i // steps, 0, i % steps, 0))

    in_specs = ([row_spec(A_Q_W)]
                + [branch_spec(d, B_W) for d in B_DILS]
                + [branch_spec(d, LANES) for d in B_DILS]
                + [row_spec(D_MODEL), full(wo_bf), full(g), full(wr), full(br)])
    return pl.pallas_call(
        _out_proj_router_kernel,
        out_shape=[
            jax.ShapeDtypeStruct((n, D_MODEL), jnp.float32),
            jax.ShapeDtypeStruct((n, HALF_D), jnp.uint32),
            jax.ShapeDtypeStruct((TOP_K, n), jnp.int32),
            jax.ShapeDtypeStruct((TOP_K, n), jnp.float32),
            jax.ShapeDtypeStruct((TOP_K, n), jnp.int32),
            jax.ShapeDtypeStruct((N_EXPERTS, LANES), jnp.int32),
        ],
        grid=(n // rows,),
        in_specs=in_specs,
        out_specs=[row_spec(D_MODEL), row_spec(HALF_D), col_spec, col_spec, col_spec,
                   pl.BlockSpec((N_EXPERTS, LANES), lambda i: (0, 0))],
        scratch_shapes=[pltpu.VMEM((rows, rows), jnp.bfloat16),
                        pltpu.VMEM((N_EXPERTS, LANES), jnp.float32),
                        pltpu.VMEM((len(B_DILS), B_W // LANES, rows, LANES), jnp.float32),
                        pltpu.VMEM((len(B_DILS), rows, LANES), jnp.float32),
                        pltpu.VMEM(wo_bf.shape, jnp.bfloat16)],
        compiler_params=pltpu.CompilerParams(
            dimension_semantics=("arbitrary",), vmem_limit_bytes=LARGE_VMEM_LIMIT),
        name="out_proj_router",
    )(oa, *outs_b, *lses_b, x2, wo_bf, g, wr, br)


def _mxu_dot(a_bf, w_f32):
    return lax.dot_general(a_bf, w_f32, (((1,), (0,)), ((), ())), preferred_element_type=jnp.float32)


def _moe_kernel(blk_exp_ref, first_ref, slot_ref, next_exp_ref, n_used_ref,
                x_ref, w1_hbm, b1_ref, w2_hbm, b2_ref, y_ref, w1_buf, w2_buf, sem):
    step = pl.program_id(0)

    def weight_copies(expert, slot):
        return (pltpu.make_async_copy(w1_hbm.at[expert], w1_buf.at[slot], sem.at[slot, 0]),
                pltpu.make_async_copy(w2_hbm.at[expert], w2_buf.at[slot], sem.at[slot, 1]))

    @pl.when(step * MOE_STEP_BLOCKS < n_used_ref[0])
    def _():
        @pl.when(step == 0)
        def _():
            for cp in weight_copies(blk_exp_ref[0], slot_ref[0]):
                cp.start()

        for j in range(MOE_STEP_BLOCKS):
            i = step * MOE_STEP_BLOCKS + j

            @pl.when(first_ref[i] == 1)
            def _():
                slot = slot_ref[i]
                for cp in weight_copies(blk_exp_ref[i], slot):
                    cp.wait()

                @pl.when(next_exp_ref[i] >= 0)
                def _():
                    for cp in weight_copies(next_exp_ref[i], (slot + 1) % MOE_WEIGHT_BUFS):
                        cp.start()

        for j in range(MOE_STEP_BLOCKS):
            i = step * MOE_STEP_BLOCKS + j
            slot = slot_ref[i]
            expert = blk_exp_ref[i]
            rs = slice(j * MOE_ROWS, (j + 1) * MOE_ROWS)
            x = jnp.concatenate(_unpack_halves(x_ref[rs, :]), axis=-1).astype(jnp.bfloat16)
            acc = jnp.zeros((MOE_ROWS, D_MODEL), jnp.float32)
            for c in range(D_FF // FF_CHUNK):
                lo = c * FF_CHUNK
                glu = _mxu_dot(x, w1_buf[slot, :, lo:lo + FF_CHUNK]) + b1_ref[expert, :, lo:lo + FF_CHUNK]
                lin = (_mxu_dot(x, w1_buf[slot, :, D_FF + lo:D_FF + lo + FF_CHUNK])
                       + b1_ref[expert, :, D_FF + lo:D_FF + lo + FF_CHUNK])
                glu = jnp.minimum(glu, SWIGLU_LIMIT)
                lin = jnp.clip(lin, -SWIGLU_LIMIT, SWIGLU_LIMIT)
                act = glu * (1.0 / (1.0 + jnp.exp(-SWIGLU_ALPHA * glu))) * (lin + 1.0)
                acc = acc + _mxu_dot(act.astype(jnp.bfloat16), w2_buf[slot, lo:lo + FF_CHUNK, :])
            y_ref[rs, :] = _pack_halves(acc + b2_ref[expert])


def _moe_plan(pends, n_blk):
    g = MOE_ROWS
    blk_row0 = jnp.arange(n_blk, dtype=jnp.int32) * g
    blk_exp = jnp.minimum(jnp.sum(pends[None, :] <= blk_row0[:, None], axis=-1),
                          N_EXPERTS - 1).astype(jnp.int32)
    n_used = (pends[-1] // g).astype(jnp.int32)
    used = blk_row0 < pends[-1]
    prev_exp = jnp.concatenate([jnp.full((1,), -1, jnp.int32), blk_exp[:-1]])
    first = (used & (blk_exp != prev_exp)).astype(jnp.int32)
    slot = ((jnp.cumsum(first) - 1) % MOE_WEIGHT_BUFS).astype(jnp.int32)
    pstarts = jnp.concatenate([jnp.zeros((1,), pends.dtype), pends[:-1]])
    nonempty = pends > pstarts
    experts = jnp.arange(N_EXPERTS, dtype=jnp.int32)
    later = nonempty[None, :] & (experts[None, :] > experts[:, None])
    next_nonempty = jnp.min(jnp.where(later, experts[None, :], N_EXPERTS), axis=-1)
    next_nonempty = jnp.where(next_nonempty == N_EXPERTS, -1, next_nonempty).astype(jnp.int32)
    next_exp = jnp.sum(jnp.where(blk_exp[:, None] == experts[None, :], next_nonempty[None, :], 0),
                       axis=-1).astype(jnp.int32)
    return blk_exp, first, slot, next_exp, n_used.reshape(1)


def _moe_experts(plan, xb, w1, b1, w2, b2):
    n_rows = xb.shape[0]
    n_blk = n_rows // MOE_ROWS

    step_rows = MOE_STEP_BLOCKS * MOE_ROWS

    def blk(i, *p):
        return jnp.minimum(i, (p[-1][0] - 1) // MOE_STEP_BLOCKS)

    grid_spec = pltpu.PrefetchScalarGridSpec(
        num_scalar_prefetch=len(plan),
        grid=(n_blk // MOE_STEP_BLOCKS,),
        in_specs=[
            pl.BlockSpec((step_rows, HALF_D), lambda i, *p: (blk(i, *p), 0)),
            pl.BlockSpec(memory_space=pl.ANY),
            pl.BlockSpec(b1.shape, lambda i, *p: (0, 0, 0)),
            pl.BlockSpec(memory_space=pl.ANY),
            pl.BlockSpec(b2.shape, lambda i, *p: (0, 0, 0)),
        ],
        out_specs=pl.BlockSpec((step_rows, HALF_D), lambda i, *p: (blk(i, *p), 0)),
        scratch_shapes=[pltpu.VMEM((MOE_WEIGHT_BUFS, D_MODEL, 2 * D_FF), jnp.float32),
                        pltpu.VMEM((MOE_WEIGHT_BUFS, D_FF, D_MODEL), jnp.float32),
                        pltpu.SemaphoreType.DMA((MOE_WEIGHT_BUFS, 2))],
    )
    return pl.pallas_call(
        _moe_kernel,
        out_shape=jax.ShapeDtypeStruct((n_rows, HALF_D), jnp.uint32),
        grid_spec=grid_spec,
        compiler_params=pltpu.CompilerParams(
            dimension_semantics=("arbitrary",), vmem_limit_bytes=LARGE_VMEM_LIMIT),
        name="moe_experts",
    )(*plan, xb, w1, b1, w2, b2)


def _combine_kernel(x1_ref, yg_ref, gate_ref, o_ref):
    acc_lo = x1_ref[:, :HALF_D]
    acc_hi = x1_ref[:, HALF_D:]
    rows = x1_ref.shape[0]
    gates = jnp.concatenate([gate_ref[...], jnp.zeros((8 - TOP_K, rows), jnp.float32)], axis=0).T
    for k in range(TOP_K):
        lo, hi = _unpack_halves(yg_ref[k])
        gk = gates[:, k:k + 1]
        acc_lo = acc_lo + lo * gk
        acc_hi = acc_hi + hi * gk
    o_ref[:, :HALF_D] = acc_lo
    o_ref[:, HALF_D:] = acc_hi


def _combine(acc, yg, gates_nk, split):
    n = acc.shape[0]
    rows = PROJ_ROWS
    steps = yg.shape[1] // rows
    first = split * steps
    return pl.pallas_call(
        _combine_kernel,
        out_shape=jax.ShapeDtypeStruct((n, D_MODEL), jnp.float32),
        grid=(steps,),
        in_specs=[pl.BlockSpec((rows, D_MODEL), lambda i: (first + i, 0)),
                  pl.BlockSpec((TOP_K, rows, HALF_D), lambda i: (0, i, 0)),
                  pl.BlockSpec((TOP_K, rows), lambda i: (0, first + i))],
        out_specs=pl.BlockSpec((rows, D_MODEL), lambda i: (first + i, 0)),
        input_output_aliases={0: 0},
        compiler_params=pltpu.CompilerParams(
            dimension_semantics=("arbitrary",), vmem_limit_bytes=VMEM_LIMIT),
        name="moe_combine",
    )(acc, yg, gates_nk)


def _sc_worker_id():
    return lax.axis_index("s") * SC_CORES + lax.axis_index("c")


def _sc_dispatch(hf, dest3, n_rows):
    n = hf.shape[0]
    chunks_per_worker = n // SC_CHUNK // SC_WORKERS
    mesh = plsc.VectorSubcoreMesh(core_axis_name="c", subcore_axis_name="s")

    @functools.partial(
        pl.kernel, mesh=mesh,
        out_type=jax.ShapeDtypeStruct((n_rows, HALF_D), hf.dtype),
        scratch_types=[pltpu.VMEM((2, TOP_K, SC_CHUNK), jnp.int32),
                       pltpu.VMEM((2, SC_CHUNK, HALF_D), hf.dtype),
                       pltpu.SemaphoreType.DMA((2,)),
                       pltpu.SemaphoreType.DMA((2,))],
        name="sc_dispatch")
    def run(hf_hbm, dest_hbm, xb_hbm, idx_v, rows_v, load_sem, scatter_sem):
        first = _sc_worker_id() * chunks_per_worker

        def load(j):
            slot = j % 2
            pltpu.sync_copy(dest_hbm.at[first + j], idx_v.at[slot])
            return pltpu.async_copy(hf_hbm.at[pl.ds((first + j) * SC_CHUNK, SC_CHUNK)],
                                    rows_v.at[slot], load_sem.at[slot])

        loads = {0: load(0)}
        scatters = {}
        for j in range(chunks_per_worker):
            slot = j % 2
            loads.pop(j).wait()
            scatters[j] = [pltpu.async_copy(rows_v.at[slot], xb_hbm.at[idx_v.at[slot, k]],
                                            scatter_sem.at[slot]) for k in range(TOP_K)]
            if j >= 1:
                for cp in scatters.pop(j - 1):
                    cp.wait()
            if j + 1 < chunks_per_worker:
                loads[j + 1] = load(j + 1)
        for cp in scatters.pop(chunks_per_worker - 1):
            cp.wait()

    return run(hf, dest3)


def _sc_collect(y, dest3):
    n = dest3.shape[0] * SC_COLLECT_CHUNK
    chunks_per_worker = n // SC_COLLECT_CHUNK // SC_WORKERS
    mesh = plsc.VectorSubcoreMesh(core_axis_name="c", subcore_axis_name="s")

    @functools.partial(
        pl.kernel, mesh=mesh,
        out_type=jax.ShapeDtypeStruct((TOP_K, n, HALF_D), y.dtype),
        scratch_types=[pltpu.VMEM((TOP_K, SC_COLLECT_CHUNK), jnp.int32),
                       pltpu.VMEM((TOP_K, SC_COLLECT_CHUNK, HALF_D), y.dtype),
                       pltpu.SemaphoreType.DMA((TOP_K,)),
                       pltpu.SemaphoreType.DMA((TOP_K,))],
        name="sc_collect")
    def run(y_hbm, dest_hbm, yg_hbm, idx_v, rows_v, gather_sem, write_sem):
        first = _sc_worker_id() * chunks_per_worker

        @pl.loop(0, chunks_per_worker)
        def _(j):
            ch = first + j
            pltpu.sync_copy(dest_hbm.at[ch], idx_v)
            gathers = [pltpu.async_copy(y_hbm.at[idx_v.at[k]], rows_v.at[k], gather_sem.at[k])
                       for k in range(TOP_K)]
            writes = []
            for k in range(TOP_K):
                gathers[k].wait()
                writes.append(pltpu.async_copy(
                    rows_v.at[k], yg_hbm.at[k, pl.ds(ch * SC_COLLECT_CHUNK, SC_COLLECT_CHUNK)],
                    write_sem.at[k]))
            for w in writes:
                w.wait()

    return run(y, dest3)


def _layer(x2, batch, seq, attn_norm_g, w_in, a_q_g, a_k_g, a_sinks, b_q_g, b_k_g, w_out,
           ffn_norm_g, w_router, b_router, w1, b1, w2, b2):
    n = x2.shape[0]
    slopes = _alibi_slopes()
    q_scale = HEAD_DIM ** -0.5 * LOG2E
    reps = MXU_DIM // HEAD_DIM
    gains = jnp.stack([jnp.tile(a_q_g, reps) * q_scale, jnp.tile(a_k_g, reps),
                       jnp.tile(b_q_g, reps) * q_scale, jnp.tile(b_k_g, reps)]).astype(jnp.float32)

    proj = _in_proj(x2, attn_norm_g.reshape(1, -1), w_in, gains, batch, seq)
    qa, ka, va = proj[:3]
    nb = len(B_DILS)
    qbs, kbs, vbs = proj[3:3 + nb], proj[3 + nb:3 + 2 * nb], proj[3 + 2 * nb:]

    bias_a = _bias_tables(slopes[:A_Q_HEADS], A_STACK_HEADS, A_HALF_WINDOW, 1, Q_TILE + 2 * A_HALF_WINDOW)
    sink_col = jnp.repeat(a_sinks.astype(jnp.float32) * LOG2E, Q_TILE).reshape(
        A_Q_HEADS // A_STACK_HEADS, A_STACK_HEADS * Q_TILE, 1)
    as_seqs = lambda a: a.reshape(batch, seq, a.shape[-1])
    out_a = _banded_attention(as_seqs(qa), as_seqs(ka), as_seqs(va), bias_a, half_w=A_HALF_WINDOW,
                              sink=sink_col, name="attn_a")[0].reshape(n, A_Q_W)

    outs_b, lses_b = [], []
    for bi, (window, dil) in enumerate(B_BRANCHES):
        half_w = window // (2 * dil)
        bias_b = _bias_tables(slopes[A_Q_HEADS:], B_STACK_HEADS, half_w, dil, Q_TILE + 2 * half_w)
        L = seq // dil
        to_seqs = lambda a: a.reshape(batch * dil, L, a.shape[-1])
        o, lse = _banded_attention(to_seqs(qbs[bi]), to_seqs(kbs[bi]), to_seqs(vbs[bi]), bias_b,
                                   half_w=half_w, want_lse=True, name=f"attn_b_d{dil}")
        if dil == 1:
            outs_b.append(o.reshape(n, B_W))
            lses_b.append(lse.reshape(n, LANES))
        else:
            outs_b.append(o.reshape(batch, dil, L, B_W))
            lses_b.append(lse.reshape(batch, dil, L, LANES))

    wr_t = w_router.T.astype(jnp.float32)
    wr_hi = wr_t.astype(jnp.bfloat16)
    wr_lo = (wr_t - wr_hi.astype(jnp.float32)).astype(jnp.bfloat16)
    wr = jnp.concatenate([wr_hi, wr_lo], axis=0)
    br = jnp.broadcast_to(b_router.astype(jnp.float32)[:, None], (N_EXPERTS, LANES))
    x1, hf, topi, gates, ranks, counts = _out_proj_router(
        out_a, outs_b, lses_b, x2, w_out, ffn_norm_g.reshape(1, -1), wr, br, seq)

    g = MOE_ROWS
    nk = n * TOP_K
    step_rows = MOE_STEP_BLOCKS * g
    n_rows = -(-(nk + N_EXPERTS * g) // step_rows) * step_rows
    cnt = counts[:, 0]
    pcnt = (cnt + g - 1) // g * g
    pends = jnp.cumsum(pcnt)
    pstarts = pends - pcnt
    experts = jnp.arange(N_EXPERTS, dtype=jnp.int32)
    start_of = jnp.sum(jnp.where(topi[:, :, None] == experts, pstarts, 0), axis=-1)
    dest = (start_of + ranks).astype(jnp.int32)
    plan = _moe_plan(pends, n_rows // g)
    dest3 = dest.reshape(TOP_K, n // SC_CHUNK, SC_CHUNK).transpose(1, 0, 2)

    xb = _sc_dispatch(hf, dest3, n_rows)
    y = _moe_experts(plan, xb, w1, b1[:, None, :], w2, b2[:, None, :])
    per_split = n // COMBINE_SPLITS
    dest_c = dest.reshape(TOP_K, n // SC_COLLECT_CHUNK, SC_COLLECT_CHUNK).transpose(1, 0, 2)
    chunks_per_split = per_split // SC_COLLECT_CHUNK
    out = x1
    for s in range(COMBINE_SPLITS):
        yg = _sc_collect(y, dest_c[s * chunks_per_split:(s + 1) * chunks_per_split])
        out = _combine(out, yg, gates, s)
    return out


def kernel(x, attn_norm_g, w_in, a_q_norm_g, a_k_norm_g, a_sinks, b_q_norm_g, b_k_norm_g, w_out,
           ffn_norm_g, w_router, b_router, w1, b1, w2, b2):
    batch, seq, d = x.shape
    x2 = x.reshape(batch * seq, d)
    for i in range(attn_norm_g.shape[0]):
        x2 = _layer(x2, batch, seq, attn_norm_g[i], w_in[i], a_q_norm_g[i], a_k_norm_g[i],
                    a_sinks[i], b_q_norm_g[i], b_k_norm_g[i], w_out[i], ffn_norm_g[i],
                    w_router[i], b_router[i], w1[i], b1[i], w2[i], b2[i])
    return x2.reshape(batch, seq, d)
```

```python
import functools

import jax
import jax.numpy as jnp
import numpy as np
from jax import lax
from jax.experimental import pallas as pl
from jax.experimental.pallas import tpu as pltpu
from jax.experimental.pallas import tpu_sc as plsc

D_MODEL = 1024
HALF_D = D_MODEL // 2
HEAD_DIM = 64
LANES = 128
MXU_DIM = 256
A_Q_HEADS = 8
A_KV_HEADS = 2
B_HEADS = 8
A_HALF_WINDOW = 128
B_BRANCHES = ((128, 1), (512, 4), (2048, 16))
B_DILS = tuple(d for _, d in B_BRANCHES)
RESIDUE_STRIDE = 4
N_ALIBI_HEADS = 16
A_Q_W = A_Q_HEADS * HEAD_DIM
A_KV_W = A_KV_HEADS * HEAD_DIM
B_W = B_HEADS * HEAD_DIM
N_EXPERTS = 32
TOP_K = 4
D_FF = 1024
SWIGLU_ALPHA = 1.702
SWIGLU_LIMIT = 7.0
NORM_EPS = 1e-5
MASK_VALUE = -1e30
LOG2E = 1.4426950408889634

Q_TILE = 128
ATTN_STEP_ROWS = 2048
A_STACK_HEADS = 4
B_STACK_HEADS = 2
PROJ_ROWS = 1024
IN_PROJ_ROWS = 1024
MOE_ROWS = 512
FF_CHUNK = 512
MOE_STEP_BLOCKS = 2
MOE_WEIGHT_BUFS = 3
VMEM_LIMIT = 48 * 1024 * 1024
LARGE_VMEM_LIMIT = 58 * 1024 * 1024
SC_CORES = 2
SC_SUBCORES = 16
SC_WORKERS = SC_CORES * SC_SUBCORES
SC_CHUNK = 64
SC_COLLECT_CHUNK = 32
COMBINE_SPLITS = 8


def _pack_halves(v):
    lo = v[:, :HALF_D].astype(jnp.bfloat16).astype(jnp.float32)
    hi = v[:, HALF_D:].astype(jnp.bfloat16).astype(jnp.float32)
    return (pltpu.bitcast(lo, jnp.uint32) >> 16) | pltpu.bitcast(hi, jnp.uint32)


def _unpack_halves(w):
    lo = pltpu.bitcast(w << 16, jnp.float32)
    hi = pltpu.bitcast(w & jnp.uint32(0xFFFF0000), jnp.float32)
    return lo, hi


def _alibi_slopes():
    return np.exp2(-8.0 * np.arange(1, N_ALIBI_HEADS + 1, dtype=np.float32) / N_ALIBI_HEADS).astype(np.float32)


def _bias_tables(head_slopes, heads_per_group, half_w, dist_scale, tk):
    i = np.arange(Q_TILE)[:, None]
    j = np.arange(tk)[None, :]
    tabs = []
    for shift in (0, half_w, tk - Q_TILE):
        dist = np.abs(j - shift - i)
        valid = dist <= half_w
        per_head = []
        for sl in head_slopes:
            b = (-np.float64(sl) * LOG2E * (dist * dist_scale)).astype(np.float32)
            per_head.append(np.where(valid, b, np.float32(MASK_VALUE)).astype(np.float32))
        t = np.stack(per_head).reshape(-1, heads_per_group * Q_TILE, tk)
        tabs.append(t)
    return jnp.asarray(np.stack(tabs))


def _in_proj_kernel(x_ref, g_ref, w_hbm, gains_ref, qa_ref, ka_ref, va_ref, *rest):
    b_refs, (scr_ref, scr2_ref, w_ref, stage_ref, sem) = rest[:-5], rest[-5:]

    @pl.when(pl.program_id(0) == 0)
    def _():
        width = stage_ref.shape[1]
        for c0 in range(0, w_ref.shape[1], width):
            cp = pltpu.make_async_copy(w_hbm.at[:, c0:c0 + width], stage_ref, sem)
            cp.start()
            cp.wait()
            w_ref[:, c0:c0 + width] = stage_ref[...].astype(jnp.bfloat16)

    x = x_ref[...]
    xn = x * lax.rsqrt(jnp.mean(x * x, axis=-1, keepdims=True) + NORM_EPS) * g_ref[...]
    xn = xn.astype(jnp.bfloat16)
    r = lax.broadcasted_iota(jnp.int32, (MXU_DIM, MXU_DIM), 0) // HEAD_DIM
    c = lax.broadcasted_iota(jnp.int32, (MXU_DIM, MXU_DIM), 1) // HEAD_DIM
    blockdiag = jnp.where(r == c, 1.0, 0.0).astype(jnp.bfloat16)

    def head_rms(sec, gain_row):
        width = sec.shape[1]
        parts = []
        step = min(width, MXU_DIM)
        for j in range(width // step):
            p = sec[:, j * step:(j + 1) * step]
            ss = jnp.dot((p * p).astype(jnp.bfloat16), blockdiag[:step, :step],
                         preferred_element_type=jnp.float32)
            parts.append(p * lax.rsqrt(ss * (1.0 / HEAD_DIM) + NORM_EPS)
                         * gains_ref[gain_row:gain_row + 1, :step])
        return parts

    def project(col0, width, gain_row):
        sec = jnp.dot(xn, w_ref[:, col0:col0 + width], preferred_element_type=jnp.float32)
        return [sec] if gain_row is None else head_rms(sec, gain_row)

    def store(out_ref, parts):
        w = parts[0].shape[1]
        for j, p in enumerate(parts):
            out_ref[:, j * w:(j + 1) * w] = p.astype(out_ref.dtype)

    def per_kv_head(p):
        lane = lax.broadcasted_iota(jnp.int32, p.shape, 1)
        swapped = pltpu.roll(p, HEAD_DIM, axis=1)
        low = lane < HEAD_DIM
        return [jnp.where(low, p, swapped), jnp.where(low, swapped, p)]

    store(qa_ref, project(0, A_Q_W, 0))
    kva = project(A_Q_W, 2 * A_KV_W, None)[0]
    store(ka_ref, per_kv_head(head_rms(kva[:, :A_KV_W], 1)[0]))
    store(va_ref, per_kv_head(kva[:, A_KV_W:]))

    rows = x_ref.shape[0]
    col0 = A_Q_W + 2 * A_KV_W
    for t, gain_row in enumerate((2, 3, None)):
        parts = project(col0 + t * B_W, B_W, gain_row)
        sec = jnp.concatenate(parts, axis=-1) if len(parts) > 1 else parts[0]
        for j in range(B_W // LANES):
            scr_ref[j] = sec[:, j * LANES:(j + 1) * LANES]
        n_lane_chunks = B_W // LANES
        prev_dil = 1
        for bi, dil in enumerate(B_DILS):
            out_ref = b_refs[t * len(B_DILS) + bi]
            if dil == 1:
                out_ref[...] = sec.astype(out_ref.dtype)
                continue
            assert dil == prev_dil * RESIDUE_STRIDE
            last = dil == B_DILS[-1]
            for res in range(dil):
                r_prev, r_sub = res % prev_dil, res // prev_dil
                for j in range(n_lane_chunks):
                    if prev_dil == 1:
                        v = scr_ref[j, pl.ds(r_sub, rows // dil, stride=RESIDUE_STRIDE), :]
                    else:
                        v = scr2_ref[r_prev * n_lane_chunks + j,
                                     pl.ds(r_sub, rows // dil, stride=RESIDUE_STRIDE), :]
                    out_ref[0, res, :, j * LANES:(j + 1) * LANES] = v.astype(out_ref.dtype)
                    if not last:
                        scr2_ref[res * n_lane_chunks + j] = v
            prev_dil = dil


def _in_proj(x2, g, w_in, gains, batch, seq):
    n = x2.shape[0]
    rows = IN_PROJ_ROWS
    steps = seq // rows
    a_widths = (A_Q_W, 2 * A_KV_W, 2 * A_KV_W)
    out_shape = [jax.ShapeDtypeStruct((n, w), jnp.bfloat16) for w in a_widths]
    out_specs = [pl.BlockSpec((rows, w), lambda i: (i, 0)) for w in a_widths]
    for _ in range(3):
        for dil in B_DILS:
            if dil == 1:
                out_shape.append(jax.ShapeDtypeStruct((n, B_W), jnp.bfloat16))
                out_specs.append(pl.BlockSpec((rows, B_W), lambda i: (i, 0)))
            else:
                out_shape.append(jax.ShapeDtypeStruct((batch, dil, seq // dil, B_W), jnp.bfloat16))
                out_specs.append(pl.BlockSpec((1, dil, rows // dil, B_W),
                                              lambda i: (i // steps, 0, i % steps, 0)))
    return pl.pallas_call(
        _in_proj_kernel,
        out_shape=out_shape,
        grid=(n // rows,),
        in_specs=[
            pl.BlockSpec((rows, D_MODEL), lambda i: (i, 0)),
            pl.BlockSpec((1, D_MODEL), lambda i: (0, 0)),
            pl.BlockSpec(memory_space=pl.ANY),
            pl.BlockSpec(gains.shape, lambda i: (0, 0)),
        ],
        out_specs=out_specs,
        scratch_shapes=[pltpu.VMEM((B_W // LANES, rows, LANES), jnp.float32),
                        pltpu.VMEM((RESIDUE_STRIDE * B_W // LANES, rows // RESIDUE_STRIDE, LANES),
                                   jnp.float32),
                        pltpu.VMEM(w_in.shape, jnp.bfloat16),
                        pltpu.VMEM((w_in.shape[0], w_in.shape[1] // 3), jnp.float32),
                        pltpu.SemaphoreType.DMA],
        compiler_params=pltpu.CompilerParams(
            dimension_semantics=("arbitrary",), vmem_limit_bytes=LARGE_VMEM_LIMIT),
        name="in_proj",
    )(x2, g, w_in, gains)


def _attn_kernel(*refs, n_chunks, kv_chunks, heads_per_stack, tk, half_w, seq_len, rows, has_sink,
                 want_lse):
    it = iter(refs)
    q_ref, k_ref, v_ref, bias_ref = next(it), next(it), next(it), next(it)
    sink_ref = next(it) if has_sink else None
    o_ref = next(it)
    lse_ref = next(it) if want_lse else None

    n_tiles = seq_len // Q_TILE
    tiles_per_step = rows // Q_TILE
    chunks_per_group = n_chunks // kv_chunks
    assert (2 * chunks_per_group) % heads_per_stack == 0
    step = pl.program_id(1)
    lane = lax.broadcasted_iota(jnp.int32, (Q_TILE, LANES), 1)
    low_half = lane < HEAD_DIM
    ones = jnp.ones((tk, LANES), jnp.bfloat16)

    for sq, t in [(a, b) for a in range(q_ref.shape[0]) for b in range(tiles_per_step)]:
        tile = step * tiles_per_step + t
        q0 = tile * Q_TILE
        kv_rows = k_ref.shape[1]
        kv_row0 = jnp.clip(step * rows - half_w, 0, seq_len - kv_rows)
        start = pl.multiple_of(jnp.clip(q0 - half_w, 0, seq_len - tk) - kv_row0, HEAD_DIM)
        variant = jnp.where(tile == 0, 0, jnp.where(tile == n_tiles - 1, 2, 1))
        r0 = t * Q_TILE
        lse_tile = jnp.zeros((Q_TILE, LANES), jnp.float32)
        head_o, head_lse = {}, {}
        for g in range(2 * n_chunks // heads_per_stack):
            heads = range(g * heads_per_stack, (g + 1) * heads_per_stack)
            kv = (heads[0] // 2) // chunks_per_group
            kc = k_ref[sq, pl.ds(start, tk), kv * LANES:(kv + 1) * LANES]
            vc = v_ref[sq, pl.ds(start, tk), kv * LANES:(kv + 1) * LANES]
            v_aug = jnp.concatenate([vc, ones], axis=1)
            q_parts = []
            for h in heads:
                c = h // 2
                q2 = q_ref[sq, r0:r0 + Q_TILE, c * LANES:(c + 1) * LANES]
                keep = low_half if h % 2 == 0 else ~low_half
                q_parts.append(jnp.where(keep, q2, jnp.zeros_like(q2)))
            qs = q_parts[0] if len(q_parts) == 1 else jnp.concatenate(q_parts, axis=0)
            s = lax.dot_general(qs, kc, (((1,), (1,)), ((), ())),
                                preferred_element_type=jnp.float32)
            s = (s + bias_ref[variant, g]).astype(jnp.bfloat16)
            m_bf = jnp.max(s, axis=-1, keepdims=True)
            if has_sink:
                m_bf = jnp.maximum(m_bf.astype(jnp.float32), sink_ref[g]).astype(jnp.bfloat16)
            m = m_bf.astype(jnp.float32)
            p = jnp.exp2(s - m_bf)
            ov = jnp.dot(p, v_aug, preferred_element_type=jnp.float32)
            o, l = ov[:, :LANES], ov[:, LANES:]
            if has_sink:
                l = l + jnp.exp2(sink_ref[g] - m)
            o = o * (1.0 / l)
            if want_lse:
                lse = m + jnp.log(l) * LOG2E
            for idx, h in enumerate(heads):
                head_o[h] = o[idx * Q_TILE:(idx + 1) * Q_TILE]
                if want_lse:
                    head_lse[h] = lse[idx * Q_TILE:(idx + 1) * Q_TILE]
                if h % 2 == 1:
                    c = h // 2
                    o2 = jnp.where(low_half, head_o.pop(h - 1), head_o.pop(h))
                    o_ref[sq, r0:r0 + Q_TILE, c * LANES:(c + 1) * LANES] = o2.astype(o_ref.dtype)
                    if want_lse:
                        lse_tile = jnp.where(lane == h - 1, head_lse.pop(h - 1),
                                             jnp.where(lane == h, head_lse.pop(h), lse_tile))
        if want_lse:
            lse_ref[sq, r0:r0 + Q_TILE, :] = lse_tile


def _banded_attention(q, k, v, bias, *, half_w, sink=None, want_lse=False, name):
    n_seq, L, qw = q.shape
    kw = k.shape[2]
    tk = Q_TILE + 2 * half_w
    rows = min(ATTN_STEP_ROWS, L)
    seqs = ATTN_STEP_ROWS // rows
    kv_rows = min(L, rows + 2 * half_w)

    def kv_index(s, i):
        row0 = pl.multiple_of(jnp.clip(i * rows - half_w, 0, L - kv_rows), HEAD_DIM)
        return s * seqs, row0, 0

    args = [q, k, v, bias]
    in_specs = [
        pl.BlockSpec((seqs, rows, qw), lambda s, i: (s, i, 0)),
        pl.BlockSpec((pl.Element(seqs), pl.Element(kv_rows), pl.Element(kw)), kv_index),
        pl.BlockSpec((pl.Element(seqs), pl.Element(kv_rows), pl.Element(kw)), kv_index),
        pl.BlockSpec(bias.shape, lambda s, i: (0, 0, 0, 0)),
    ]
    if sink is not None:
        args.append(sink)
        in_specs.append(pl.BlockSpec(sink.shape, lambda s, i: (0, 0, 0)))
    out_shape = [jax.ShapeDtypeStruct((n_seq, L, qw), jnp.bfloat16)]
    out_specs = [pl.BlockSpec((seqs, rows, qw), lambda s, i: (s, i, 0))]
    if want_lse:
        out_shape.append(jax.ShapeDtypeStruct((n_seq, L, LANES), jnp.float32))
        out_specs.append(pl.BlockSpec((seqs, rows, LANES), lambda s, i: (s, i, 0)))

    kern = functools.partial(
        _attn_kernel, n_chunks=qw // LANES, kv_chunks=kw // LANES,
        heads_per_stack=bias.shape[2] // Q_TILE, tk=tk, half_w=half_w, seq_len=L,
        rows=rows, has_sink=sink is not None, want_lse=want_lse)
    return pl.pallas_call(
        kern,
        out_shape=out_shape,
        grid=(n_seq // seqs, L // rows),
        in_specs=in_specs,
        out_specs=out_specs,
        compiler_params=pltpu.CompilerParams(
            dimension_semantics=("arbitrary", "arbitrary"), vmem_limit_bytes=VMEM_LIMIT),
        name=name,
    )(*args)


def _out_proj_router_kernel(*refs):
    nb = len(B_DILS)
    oa_ref = refs[0]
    o_refs = refs[1:1 + nb]
    lse_refs = refs[1 + nb:1 + 2 * nb]
    (x_ref, wo_ref, g_ref, wr_ref, br_ref,
     x1_ref, hf_ref, topi_ref, gate_ref, rank_ref, cnt_ref,
     tri_ref, carry_ref, so_ref, sl_ref, wo_bf_ref) = refs[1 + 2 * nb:]
    i = pl.program_id(0)
    rows = x_ref.shape[0]

    @pl.when(i == 0)
    def _():
        a = lax.broadcasted_iota(jnp.int32, (rows, rows), 0)
        b = lax.broadcasted_iota(jnp.int32, (rows, rows), 1)
        tri_ref[...] = jnp.where(a <= b, 1.0, 0.0).astype(jnp.bfloat16)
        carry_ref[...] = jnp.zeros_like(carry_ref)
        wo_bf_ref[...] = wo_ref[...].astype(jnp.bfloat16)

    outs, lses = [], []
    for bi, dil in enumerate(B_DILS):
        if dil == 1:
            outs.append(o_refs[bi][...].astype(jnp.float32))
            lses.append(lse_refs[bi][...])
        else:
            for res in range(dil):
                for j in range(B_W // LANES):
                    so_ref[bi, j, pl.ds(res, rows // dil, stride=dil), :] = (
                        o_refs[bi][0, res, :, j * LANES:(j + 1) * LANES].astype(jnp.float32))
                sl_ref[bi, pl.ds(res, rows // dil, stride=dil), :] = lse_refs[bi][0, res]
            outs.append(jnp.concatenate([so_ref[bi, j] for j in range(B_W // LANES)], axis=-1))
            lses.append(sl_ref[bi])

    mx = functools.reduce(jnp.maximum, lses)
    es = [jnp.exp2(l - mx) for l in lses]
    inv = 1.0 / functools.reduce(lambda a, b: a + b, es)
    eh = lax.broadcasted_iota(jnp.int32, (LANES, B_W), 0)
    ej = lax.broadcasted_iota(jnp.int32, (LANES, B_W), 1) // HEAD_DIM
    expand = jnp.where(eh == ej, 1.0, 0.0).astype(jnp.bfloat16)
    ob = jnp.zeros((rows, B_W), jnp.float32)
    for e, o in zip(es, outs):
        w = e * inv
        wide = jnp.dot(w.astype(jnp.bfloat16), expand, preferred_element_type=jnp.float32)
        ob = ob + wide * o

    attn = jnp.concatenate([oa_ref[...], ob.astype(jnp.bfloat16)], axis=-1)
    x1 = x_ref[...] + jnp.dot(attn, wo_bf_ref[...], preferred_element_type=jnp.float32)
    x1_ref[...] = x1
    hf = x1 * lax.rsqrt(jnp.mean(x1 * x1, axis=-1, keepdims=True) + NORM_EPS) * g_ref[...]
    hf_hi = hf.astype(jnp.bfloat16)
    hf_ref[...] = _pack_halves(hf)
    hf_lo = (hf - hf_hi.astype(jnp.float32)).astype(jnp.bfloat16)

    nt = (((1,), (1,)), ((), ()))
    lg_hi = lax.dot_general(wr_ref[...], hf_hi, nt, preferred_element_type=jnp.float32)
    lg_lo = lax.dot_general(wr_ref[0:N_EXPERTS, :], hf_lo, nt, preferred_element_type=jnp.float32)
    logits = lg_hi[0:N_EXPERTS] + lg_hi[N_EXPERTS:] + lg_lo + br_ref[:, 0:1]


    eidx = lax.broadcasted_iota(jnp.int32, (N_EXPERTS, rows), 0)
    work = logits
    vals, sels = [], []
    for k in range(TOP_K):
        mk = jnp.max(work, axis=0, keepdims=True)
        ik = jnp.min(jnp.where(work == mk, eidx, N_EXPERTS), axis=0, keepdims=True)
        sel = eidx == ik
        work = jnp.where(sel, -jnp.inf, work)
        vals.append(mk)
        sels.append(sel)
        topi_ref[k:k + 1, :] = ik
    exps = [jnp.exp(vk - vals[0]) for vk in vals]
    denom = exps[0] + exps[1] + exps[2] + exps[3]
    ginv = 1.0 / denom
    for k in range(TOP_K):
        gate_ref[k:k + 1, :] = exps[k] * ginv

    onehot = jnp.zeros((N_EXPERTS, rows), jnp.float32)
    for sel in sels:
        onehot = onehot + jnp.where(sel, 1.0, 0.0)
    incl = jnp.dot(onehot.astype(jnp.bfloat16), tri_ref[...], preferred_element_type=jnp.float32)
    before = incl - onehot + carry_ref[:, 0:1]
    for k in range(TOP_K):
        rk = jnp.sum(jnp.where(sels[k], before, 0.0), axis=0, keepdims=True)
        rank_ref[k:k + 1, :] = rk.astype(jnp.int32)
    carry = carry_ref[...] + jnp.sum(onehot, axis=1, keepdims=True)
    carry_ref[...] = carry
    cnt_ref[...] = carry.astype(jnp.int32)


def _out_proj_router(oa, outs_b, lses_b, x2, wo_bf, g, wr, br, seq):
    n = x2.shape[0]
    rows = PROJ_ROWS
    steps = seq // rows
    row_spec = lambda w: pl.BlockSpec((rows, w), lambda i: (i, 0))
    full = lambda a: pl.BlockSpec(a.shape, lambda i: (0,) * a.ndim)
    col_spec = pl.BlockSpec((TOP_K, rows), lambda i: (0, i))

    def branch_spec(dil, w):
        if dil == 1:
            return row_spec(w)
        return pl.BlockSpec((1, dil, rows // dil, w), lambda i: (i // steps, 0, i % steps, 0))

    in_specs = ([row_spec(A_Q_W)]
                + [branch_spec(d, B_W) for d in B_DILS]
                + [branch_spec(d, LANES) for d in B_DILS]
                + [row_spec(D_MODEL), full(wo_bf), full(g), full(wr), full(br)])
    return pl.pallas_call(
        _out_proj_router_kernel,
        out_shape=[
            jax.ShapeDtypeStruct((n, D_MODEL), jnp.float32),
            jax.ShapeDtypeStruct((n, HALF_D), jnp.uint32),
            jax.ShapeDtypeStruct((TOP_K, n), jnp.int32),
            jax.ShapeDtypeStruct((TOP_K, n), jnp.float32),
            jax.ShapeDtypeStruct((TOP_K, n), jnp.int32),
            jax.ShapeDtypeStruct((N_EXPERTS, LANES), jnp.int32),
        ],
        grid=(n // rows,),
        in_specs=in_specs,
        out_specs=[row_spec(D_MODEL), row_spec(HALF_D), col_spec, col_spec, col_spec,
                   pl.BlockSpec((N_EXPERTS, LANES), lambda i: (0, 0))],
        scratch_shapes=[pltpu.VMEM((rows, rows), jnp.bfloat16),
                        pltpu.VMEM((N_EXPERTS, LANES), jnp.float32),
                        pltpu.VMEM((len(B_DILS), B_W // LANES, rows, LANES), jnp.float32),
                        pltpu.VMEM((len(B_DILS), rows, LANES), jnp.float32),
                        pltpu.VMEM(wo_bf.shape, jnp.bfloat16)],
        compiler_params=pltpu.CompilerParams(
            dimension_semantics=("arbitrary",), vmem_limit_bytes=LARGE_VMEM_LIMIT),
        name="out_proj_router",
    )(oa, *outs_b, *lses_b, x2, wo_bf, g, wr, br)


def _mxu_dot(a_bf, w_f32):
    return lax.dot_general(a_bf, w_f32, (((1,), (0,)), ((), ())), preferred_element_type=jnp.float32)


def _moe_kernel(blk_exp_ref, first_ref, slot_ref, next_exp_ref, n_used_ref,
                x_ref, w1_hbm, b1_ref, w2_hbm, b2_ref, y_ref, w1_buf, w2_buf, sem):
    step = pl.program_id(0)

    def weight_copies(expert, slot):
        return (pltpu.make_async_copy(w1_hbm.at[expert], w1_buf.at[slot], sem.at[slot, 0]),
                pltpu.make_async_copy(w2_hbm.at[expert], w2_buf.at[slot], sem.at[slot, 1]))

    @pl.when(step * MOE_STEP_BLOCKS < n_used_ref[0])
    def _():
        @pl.when(step == 0)
        def _():
            for cp in weight_copies(blk_exp_ref[0], slot_ref[0]):
                cp.start()

        for j in range(MOE_STEP_BLOCKS):
            i = step * MOE_STEP_BLOCKS + j

            @pl.when(first_ref[i] == 1)
            def _():
                slot = slot_ref[i]
                for cp in weight_copies(blk_exp_ref[i], slot):
                    cp.wait()

                @pl.when(next_exp_ref[i] >= 0)
                def _():
                    for cp in weight_copies(next_exp_ref[i], (slot + 1) % MOE_WEIGHT_BUFS):
                        cp.start()

        for j in range(MOE_STEP_BLOCKS):
            i = step * MOE_STEP_BLOCKS + j
            slot = slot_ref[i]
            expert = blk_exp_ref[i]
            rs = slice(j * MOE_ROWS, (j + 1) * MOE_ROWS)
            x = jnp.concatenate(_unpack_halves(x_ref[rs, :]), axis=-1).astype(jnp.bfloat16)
            acc = jnp.zeros((MOE_ROWS, D_MODEL), jnp.float32)
            for c in range(D_FF // FF_CHUNK):
                lo = c * FF_CHUNK
                glu = _mxu_dot(x, w1_buf[slot, :, lo:lo + FF_CHUNK]) + b1_ref[expert, :, lo:lo + FF_CHUNK]
                lin = (_mxu_dot(x, w1_buf[slot, :, D_FF + lo:D_FF + lo + FF_CHUNK])
                       + b1_ref[expert, :, D_FF + lo:D_FF + lo + FF_CHUNK])
                glu = jnp.minimum(glu, SWIGLU_LIMIT)
                lin = jnp.clip(lin, -SWIGLU_LIMIT, SWIGLU_LIMIT)
                act = glu * (1.0 / (1.0 + jnp.exp(-SWIGLU_ALPHA * glu))) * (lin + 1.0)
                acc = acc + _mxu_dot(act.astype(jnp.bfloat16), w2_buf[slot, lo:lo + FF_CHUNK, :])
            y_ref[rs, :] = _pack_halves(acc + b2_ref[expert])


def _moe_plan(pends, n_blk):
    g = MOE_ROWS
    blk_row0 = jnp.arange(n_blk, dtype=jnp.int32) * g
    blk_exp = jnp.minimum(jnp.sum(pends[None, :] <= blk_row0[:, None], axis=-1),
                          N_EXPERTS - 1).astype(jnp.int32)
    n_used = (pends[-1] // g).astype(jnp.int32)
    used = blk_row0 < pends[-1]
    prev_exp = jnp.concatenate([jnp.full((1,), -1, jnp.int32), blk_exp[:-1]])
    first = (used & (blk_exp != prev_exp)).astype(jnp.int32)
    slot = ((jnp.cumsum(first) - 1) % MOE_WEIGHT_BUFS).astype(jnp.int32)
    pstarts = jnp.concatenate([jnp.zeros((1,), pends.dtype), pends[:-1]])
    nonempty = pends > pstarts
    experts = jnp.arange(N_EXPERTS, dtype=jnp.int32)
    later = nonempty[None, :] & (experts[None, :] > experts[:, None])
    next_nonempty = jnp.min(jnp.where(later, experts[None, :], N_EXPERTS), axis=-1)
    next_nonempty = jnp.where(next_nonempty == N_EXPERTS, -1, next_nonempty).astype(jnp.int32)
    next_exp = jnp.sum(jnp.where(blk_exp[:, None] == experts[None, :], next_nonempty[None, :], 0),
                       axis=-1).astype(jnp.int32)
    return blk_exp, first, slot, next_exp, n_used.reshape(1)


def _moe_experts(plan, xb, w1, b1, w2, b2):
    n_rows = xb.shape[0]
    n_blk = n_rows // MOE_ROWS

    step_rows = MOE_STEP_BLOCKS * MOE_ROWS

    def blk(i, *p):
        return jnp.minimum(i, (p[-1][0] - 1) // MOE_STEP_BLOCKS)

    grid_spec = pltpu.PrefetchScalarGridSpec(
        num_scalar_prefetch=len(plan),
        grid=(n_blk // MOE_STEP_BLOCKS,),
        in_specs=[
            pl.BlockSpec((step_rows, HALF_D), lambda i, *p: (blk(i, *p), 0)),
            pl.BlockSpec(memory_space=pl.ANY),
            pl.BlockSpec(b1.shape, lambda i, *p: (0, 0, 0)),
            pl.BlockSpec(memory_space=pl.ANY),
            pl.BlockSpec(b2.shape, lambda i, *p: (0, 0, 0)),
        ],
        out_specs=pl.BlockSpec((step_rows, HALF_D), lambda i, *p: (blk(i, *p), 0)),
        scratch_shapes=[pltpu.VMEM((MOE_WEIGHT_BUFS, D_MODEL, 2 * D_FF), jnp.float32),
                        pltpu.VMEM((MOE_WEIGHT_BUFS, D_FF, D_MODEL), jnp.float32),
                        pltpu.SemaphoreType.DMA((MOE_WEIGHT_BUFS, 2))],
    )
    return pl.pallas_call(
        _moe_kernel,
        out_shape=jax.ShapeDtypeStruct((n_rows, HALF_D), jnp.uint32),
        grid_spec=grid_spec,
        compiler_params=pltpu.CompilerParams(
            dimension_semantics=("arbitrary",), vmem_limit_bytes=LARGE_VMEM_LIMIT),
        name="moe_experts",
    )(*plan, xb, w1, b1, w2, b2)


def _combine_kernel(x1_ref, yg_ref, gate_ref, o_ref):
    acc_lo = x1_ref[:, :HALF_D]
    acc_hi = x1_ref[:, HALF_D:]
    rows = x1_ref.shape[0]
    gates = jnp.concatenate([gate_ref[...], jnp.zeros((8 - TOP_K, rows), jnp.float32)], axis=0).T
    for k in range(TOP_K):
        lo, hi = _unpack_halves(yg_ref[k])
        gk = gates[:, k:k + 1]
        acc_lo = acc_lo + lo * gk
        acc_hi = acc_hi + hi * gk
    o_ref[:, :HALF_D] = acc_lo
    o_ref[:, HALF_D:] = acc_hi


def _combine(acc, yg, gates_nk, split):
    n = acc.shape[0]
    rows = PROJ_ROWS
    steps = yg.shape[1] // rows
    first = split * steps
    return pl.pallas_call(
        _combine_kernel,
        out_shape=jax.ShapeDtypeStruct((n, D_MODEL), jnp.float32),
        grid=(steps,),
        in_specs=[pl.BlockSpec((rows, D_MODEL), lambda i: (first + i, 0)),
                  pl.BlockSpec((TOP_K, rows, HALF_D), lambda i: (0, i, 0)),
                  pl.BlockSpec((TOP_K, rows), lambda i: (0, first + i))],
        out_specs=pl.BlockSpec((rows, D_MODEL), lambda i: (first + i, 0)),
        input_output_aliases={0: 0},
        compiler_params=pltpu.CompilerParams(
            dimension_semantics=("arbitrary",), vmem_limit_bytes=VMEM_LIMIT),
        name="moe_combine",
    )(acc, yg, gates_nk)


def _sc_worker_id():
    return lax.axis_index("s") * SC_CORES + lax.axis_index("c")


def _sc_dispatch(hf, dest3, n_rows):
    n = hf.shape[0]
    chunks_per_worker = n // SC_CHUNK // SC_WORKERS
    mesh = plsc.VectorSubcoreMesh(core_axis_name="c", subcore_axis_name="s")

    @functools.partial(
        pl.kernel, mesh=mesh,
        out_type=jax.ShapeDtypeStruct((n_rows, HALF_D), hf.dtype),
        scratch_types=[pltpu.VMEM((2, TOP_K, SC_CHUNK), jnp.int32),
                       pltpu.VMEM((2, SC_CHUNK, HALF_D), hf.dtype),
                       pltpu.SemaphoreType.DMA((2,)),
                       pltpu.SemaphoreType.DMA((2,))],
        name="sc_dispatch")
    def run(hf_hbm, dest_hbm, xb_hbm, idx_v, rows_v, load_sem, scatter_sem):
        first = _sc_worker_id() * chunks_per_worker

        def load(j):
            slot = j % 2
            pltpu.sync_copy(dest_hbm.at[first + j], idx_v.at[slot])
            return pltpu.async_copy(hf_hbm.at[pl.ds((first + j) * SC_CHUNK, SC_CHUNK)],
                                    rows_v.at[slot], load_sem.at[slot])

        loads = {0: load(0)}
        scatters = {}
        for j in range(chunks_per_worker):
            slot = j % 2
            loads.pop(j).wait()
            scatters[j] = [pltpu.async_copy(rows_v.at[slot], xb_hbm.at[idx_v.at[slot, k]],
                                            scatter_sem.at[slot]) for k in range(TOP_K)]
            if j >= 1:
                for cp in scatters.pop(j - 1):
                    cp.wait()
            if j + 1 < chunks_per_worker:
                loads[j + 1] = load(j + 1)
        for cp in scatters.pop(chunks_per_worker - 1):
            cp.wait()

    return run(hf, dest3)


def _sc_collect(y, dest3):
    n = dest3.shape[0] * SC_COLLECT_CHUNK
    chunks_per_worker = n // SC_COLLECT_CHUNK // SC_WORKERS
    mesh = plsc.VectorSubcoreMesh(core_axis_name="c", subcore_axis_name="s")

    @functools.partial(
        pl.kernel, mesh=mesh,
        out_type=jax.ShapeDtypeStruct((TOP_K, n, HALF_D), y.dtype),
        scratch_types=[pltpu.VMEM((TOP_K, SC_COLLECT_CHUNK), jnp.int32),
                       pltpu.VMEM((TOP_K, SC_COLLECT_CHUNK, HALF_D), y.dtype),
                       pltpu.SemaphoreType.DMA((TOP_K,)),
                       pltpu.SemaphoreType.DMA((TOP_K,))],
        name="sc_collect")
    def run(y_hbm, dest_hbm, yg_hbm, idx_v, rows_v, gather_sem, write_sem):
        first = _sc_worker_id() * chunks_per_worker

        @pl.loop(0, chunks_per_worker)
        def _(j):
            ch = first + j
            pltpu.sync_copy(dest_hbm.at[ch], idx_v)
            gathers = [pltpu.async_copy(y_hbm.at[idx_v.at[k]], rows_v.at[k], gather_sem.at[k])
                       for k in range(TOP_K)]
            writes = []
            for k in range(TOP_K):
                gathers[k].wait()
                writes.append(pltpu.async_copy(
                    rows_v.at[k], yg_hbm.at[k, pl.ds(ch * SC_COLLECT_CHUNK, SC_COLLECT_CHUNK)],
                    write_sem.at[k]))
            for w in writes:
                w.wait()

    return run(y, dest3)


def _layer(x2, batch, seq, attn_norm_g, w_in, a_q_g, a_k_g, a_sinks, b_q_g, b_k_g, w_out,
           ffn_norm_g, w_router, b_router, w1, b1, w2, b2):
    n = x2.shape[0]
    slopes = _alibi_slopes()
    q_scale = HEAD_DIM ** -0.5 * LOG2E
    reps = MXU_DIM // HEAD_DIM
    gains = jnp.stack([jnp.tile(a_q_g, reps) * q_scale, jnp.tile(a_k_g, reps),
                       jnp.tile(b_q_g, reps) * q_scale, jnp.tile(b_k_g, reps)]).astype(jnp.float32)

    proj = _in_proj(x2, attn_norm_g.reshape(1, -1), w_in, gains, batch, seq)
    qa, ka, va = proj[:3]
    nb = len(B_DILS)
    qbs, kbs, vbs = proj[3:3 + nb], proj[3 + nb:3 + 2 * nb], proj[3 + 2 * nb:]

    bias_a = _bias_tables(slopes[:A_Q_HEADS], A_STACK_HEADS, A_HALF_WINDOW, 1, Q_TILE + 2 * A_HALF_WINDOW)
    sink_col = jnp.repeat(a_sinks.astype(jnp.float32) * LOG2E, Q_TILE).reshape(
        A_Q_HEADS // A_STACK_HEADS, A_STACK_HEADS * Q_TILE, 1)
    as_seqs = lambda a: a.reshape(batch, seq, a.shape[-1])
    out_a = _banded_attention(as_seqs(qa), as_seqs(ka), as_seqs(va), bias_a, half_w=A_HALF_WINDOW,
                              sink=sink_col, name="attn_a")[0].reshape(n, A_Q_W)

    outs_b, lses_b = [], []
    for bi, (window, dil) in enumerate(B_BRANCHES):
        half_w = window // (2 * dil)
        bias_b = _bias_tables(slopes[A_Q_HEADS:], B_STACK_HEADS, half_w, dil, Q_TILE + 2 * half_w)
        L = seq // dil
        to_seqs = lambda a: a.reshape(batch * dil, L, a.shape[-1])
        o, lse = _banded_attention(to_seqs(qbs[bi]), to_seqs(kbs[bi]), to_seqs(vbs[bi]), bias_b,
                                   half_w=half_w, want_lse=True, name=f"attn_b_d{dil}")
        if dil == 1:
            outs_b.append(o.reshape(n, B_W))
            lses_b.append(lse.reshape(n, LANES))
        else:
            outs_b.append(o.reshape(batch, dil, L, B_W))
            lses_b.append(lse.reshape(batch, dil, L, LANES))

    wr_t = w_router.T.astype(jnp.float32)
    wr_hi = wr_t.astype(jnp.bfloat16)
    wr_lo = (wr_t - wr_hi.astype(jnp.float32)).astype(jnp.bfloat16)
    wr = jnp.concatenate([wr_hi, wr_lo], axis=0)
    br = jnp.broadcast_to(b_router.astype(jnp.float32)[:, None], (N_EXPERTS, LANES))
    x1, hf, topi, gates, ranks, counts = _out_proj_router(
        out_a, outs_b, lses_b, x2, w_out, ffn_norm_g.reshape(1, -1), wr, br, seq)

    g = MOE_ROWS
    nk = n * TOP_K
    step_rows = MOE_STEP_BLOCKS * g
    n_rows = -(-(nk + N_EXPERTS * g) // step_rows) * step_rows
    cnt = counts[:, 0]
    pcnt = (cnt + g - 1) // g * g
    pends = jnp.cumsum(pcnt)
    pstarts = pends - pcnt
    experts = jnp.arange(N_EXPERTS, dtype=jnp.int32)
    start_of = jnp.sum(jnp.where(topi[:, :, None] == experts, pstarts, 0), axis=-1)
    dest = (start_of + ranks).astype(jnp.int32)
    plan = _moe_plan(pends, n_rows // g)
    dest3 = dest.reshape(TOP_K, n // SC_CHUNK, SC_CHUNK).transpose(1, 0, 2)

    xb = _sc_dispatch(hf, dest3, n_rows)
    y = _moe_experts(plan, xb, w1, b1[:, None, :], w2, b2[:, None, :])
    smallest_range = max(PROJ_ROWS, SC_COLLECT_CHUNK * SC_WORKERS)
    n_splits = max(1, min(COMBINE_SPLITS, n // smallest_range))
    per_split = n // n_splits
    dest_c = dest.reshape(TOP_K, n // SC_COLLECT_CHUNK, SC_COLLECT_CHUNK).transpose(1, 0, 2)
    chunks_per_split = per_split // SC_COLLECT_CHUNK
    out = x1
    for s in range(n_splits):
        yg = _sc_collect(y, dest_c[s * chunks_per_split:(s + 1) * chunks_per_split])
        out = _combine(out, yg, gates, s)
    return out


def kernel(x, attn_norm_g, w_in, a_q_norm_g, a_k_norm_g, a_sinks, b_q_norm_g, b_k_norm_g, w_out,
           ffn_norm_g, w_router, b_router, w1, b1, w2, b2):
    batch, seq, d = x.shape
    x2 = x.reshape(batch * seq, d)
    for i in range(attn_norm_g.shape[0]):
        x2 = _layer(x2, batch, seq, attn_norm_g[i], w_in[i], a_q_norm_g[i], a_k_norm_g[i],
                    a_sinks[i], b_q_norm_g[i], b_k_norm_g[i], w_out[i], ffn_norm_g[i],
                    w_router[i], b_router[i], w1[i], b1[i], w2[i], b2[i])
    return x2.reshape(batch, seq, d)
```

```python
import functools

import jax
import jax.numpy as jnp
import numpy as np
from jax import lax
from jax.experimental import pallas as pl
from jax.experimental.pallas import tpu as pltpu
from jax.experimental.pallas import tpu_sc as plsc

D_MODEL = 1024
HALF_D = D_MODEL // 2
HEAD_DIM = 64
LANES = 128
MXU_DIM = 256
A_Q_HEADS = 8
A_KV_HEADS = 2
B_HEADS = 8
A_HALF_WINDOW = 128
B_BRANCHES = ((128, 1), (512, 4), (2048, 16))
B_DILS = tuple(d for _, d in B_BRANCHES)
RESIDUE_STRIDE = 4
N_ALIBI_HEADS = 16
A_Q_W = A_Q_HEADS * HEAD_DIM
A_KV_W = A_KV_HEADS * HEAD_DIM
B_W = B_HEADS * HEAD_DIM
N_EXPERTS = 32
TOP_K = 4
D_FF = 1024
SWIGLU_ALPHA = 1.702
SWIGLU_LIMIT = 7.0
NORM_EPS = 1e-5
MASK_VALUE = -1e30
LOG2E = 1.4426950408889634

Q_TILE = 128
ATTN_STEP_ROWS = 2048
A_STACK_HEADS = 4
B_STACK_HEADS = 2
PROJ_ROWS = 1024
IN_PROJ_ROWS = 1024
MOE_ROWS = 512
FF_CHUNK = 512
MOE_STEP_BLOCKS = 2
MOE_WEIGHT_BUFS = 3
VMEM_LIMIT = 48 * 1024 * 1024
LARGE_VMEM_LIMIT = 58 * 1024 * 1024
SC_CORES = 2
SC_SUBCORES = 16
SC_WORKERS = SC_CORES * SC_SUBCORES
SC_CHUNK = 64
SC_COLLECT_CHUNK = 32
COMBINE_SPLITS = 8


def _pack_halves(v):
    lo = v[:, :HALF_D].astype(jnp.bfloat16).astype(jnp.float32)
    hi = v[:, HALF_D:].astype(jnp.bfloat16).astype(jnp.float32)
    return (pltpu.bitcast(lo, jnp.uint32) >> 16) | pltpu.bitcast(hi, jnp.uint32)


def _unpack_halves(w):
    lo = pltpu.bitcast(w << 16, jnp.float32)
    hi = pltpu.bitcast(w & jnp.uint32(0xFFFF0000), jnp.float32)
    return lo, hi


def _alibi_slopes():
    return np.exp2(-8.0 * np.arange(1, N_ALIBI_HEADS + 1, dtype=np.float32) / N_ALIBI_HEADS).astype(np.float32)


def _bias_tables(head_slopes, heads_per_group, half_w, dist_scale, tk):
    i = np.arange(Q_TILE)[:, None]
    j = np.arange(tk)[None, :]
    tabs = []
    for shift in (0, half_w, tk - Q_TILE):
        dist = np.abs(j - shift - i)
        valid = dist <= half_w
        per_head = []
        for sl in head_slopes:
            b = (-np.float64(sl) * LOG2E * (dist * dist_scale)).astype(np.float32)
            per_head.append(np.where(valid, b, np.float32(MASK_VALUE)).astype(np.float32))
        t = np.stack(per_head).reshape(-1, heads_per_group * Q_TILE, tk)
        tabs.append(t)
    return jnp.asarray(np.stack(tabs))


def _in_proj_kernel(x_ref, g_ref, w_hbm, gains_ref, qa_ref, ka_ref, va_ref, *rest):
    b_refs, (scr_ref, scr2_ref, w_ref, stage_ref, sem) = rest[:-5], rest[-5:]

    @pl.when(pl.program_id(0) == 0)
    def _():
        width = stage_ref.shape[1]
        for c0 in range(0, w_ref.shape[1], width):
            cp = pltpu.make_async_copy(w_hbm.at[:, c0:c0 + width], stage_ref, sem)
            cp.start()
            cp.wait()
            w_ref[:, c0:c0 + width] = stage_ref[...].astype(jnp.bfloat16)

    x = x_ref[...]
    xn = x * lax.rsqrt(jnp.mean(x * x, axis=-1, keepdims=True) + NORM_EPS) * g_ref[...]
    xn = xn.astype(jnp.bfloat16)
    r = lax.broadcasted_iota(jnp.int32, (MXU_DIM, MXU_DIM), 0) // HEAD_DIM
    c = lax.broadcasted_iota(jnp.int32, (MXU_DIM, MXU_DIM), 1) // HEAD_DIM
    blockdiag = jnp.where(r == c, 1.0, 0.0).astype(jnp.bfloat16)

    def head_rms(sec, gain_row):
        width = sec.shape[1]
        parts = []
        step = min(width, MXU_DIM)
        for j in range(width // step):
            p = sec[:, j * step:(j + 1) * step]
            ss = jnp.dot((p * p).astype(jnp.bfloat16), blockdiag[:step, :step],
                         preferred_element_type=jnp.float32)
            parts.append(p * lax.rsqrt(ss * (1.0 / HEAD_DIM) + NORM_EPS)
                         * gains_ref[gain_row:gain_row + 1, :step])
        return parts

    def project(col0, width, gain_row):
        sec = jnp.dot(xn, w_ref[:, col0:col0 + width], preferred_element_type=jnp.float32)
        return [sec] if gain_row is None else head_rms(sec, gain_row)

    def store(out_ref, parts):
        w = parts[0].shape[1]
        for j, p in enumerate(parts):
            out_ref[:, j * w:(j + 1) * w] = p.astype(out_ref.dtype)

    def per_kv_head(p):
        lane = lax.broadcasted_iota(jnp.int32, p.shape, 1)
        swapped = pltpu.roll(p, HEAD_DIM, axis=1)
        low = lane < HEAD_DIM
        return [jnp.where(low, p, swapped), jnp.where(low, swapped, p)]

    store(qa_ref, project(0, A_Q_W, 0))
    kva = project(A_Q_W, 2 * A_KV_W, None)[0]
    store(ka_ref, per_kv_head(head_rms(kva[:, :A_KV_W], 1)[0]))
    store(va_ref, per_kv_head(kva[:, A_KV_W:]))

    rows = x_ref.shape[0]
    col0 = A_Q_W + 2 * A_KV_W
    for t, gain_row in enumerate((2, 3, None)):
        parts = project(col0 + t * B_W, B_W, gain_row)
        sec = jnp.concatenate(parts, axis=-1) if len(parts) > 1 else parts[0]
        for j in range(B_W // LANES):
            scr_ref[j] = sec[:, j * LANES:(j + 1) * LANES]
        n_lane_chunks = B_W // LANES
        prev_dil = 1
        for bi, dil in enumerate(B_DILS):
            out_ref = b_refs[t * len(B_DILS) + bi]
            if dil == 1:
                out_ref[...] = sec.astype(out_ref.dtype)
                continue
            assert dil == prev_dil * RESIDUE_STRIDE
            last = dil == B_DILS[-1]
            for res in range(dil):
                r_prev, r_sub = res % prev_dil, res // prev_dil
                for j in range(n_lane_chunks):
                    if prev_dil == 1:
                        v = scr_ref[j, pl.ds(r_sub, rows // dil, stride=RESIDUE_STRIDE), :]
                    else:
                        v = scr2_ref[r_prev * n_lane_chunks + j,
                                     pl.ds(r_sub, rows // dil, stride=RESIDUE_STRIDE), :]
                    out_ref[0, res, :, j * LANES:(j + 1) * LANES] = v.astype(out_ref.dtype)
                    if not last:
                        scr2_ref[res * n_lane_chunks + j] = v
            prev_dil = dil


def _in_proj(x2, g, w_in, gains, batch, seq):
    n = x2.shape[0]
    rows = IN_PROJ_ROWS
    steps = seq // rows
    a_widths = (A_Q_W, 2 * A_KV_W, 2 * A_KV_W)
    out_shape = [jax.ShapeDtypeStruct((n, w), jnp.bfloat16) for w in a_widths]
    out_specs = [pl.BlockSpec((rows, w), lambda i: (i, 0)) for w in a_widths]
    for _ in range(3):
        for dil in B_DILS:
            if dil == 1:
                out_shape.append(jax.ShapeDtypeStruct((n, B_W), jnp.bfloat16))
                out_specs.append(pl.BlockSpec((rows, B_W), lambda i: (i, 0)))
            else:
                out_shape.append(jax.ShapeDtypeStruct((batch, dil, seq // dil, B_W), jnp.bfloat16))
                out_specs.append(pl.BlockSpec((1, dil, rows // dil, B_W),
                                              lambda i: (i // steps, 0, i % steps, 0)))
    return pl.pallas_call(
        _in_proj_kernel,
        out_shape=out_shape,
        grid=(n // rows,),
        in_specs=[
            pl.BlockSpec((rows, D_MODEL), lambda i: (i, 0)),
            pl.BlockSpec((1, D_MODEL), lambda i: (0, 0)),
            pl.BlockSpec(memory_space=pl.ANY),
            pl.BlockSpec(gains.shape, lambda i: (0, 0)),
        ],
        out_specs=out_specs,
        scratch_shapes=[pltpu.VMEM((B_W // LANES, rows, LANES), jnp.float32),
                        pltpu.VMEM((RESIDUE_STRIDE * B_W // LANES, rows // RESIDUE_STRIDE, LANES),
                                   jnp.float32),
                        pltpu.VMEM(w_in.shape, jnp.bfloat16),
                        pltpu.VMEM((w_in.shape[0], w_in.shape[1] // 3), jnp.float32),
                        pltpu.SemaphoreType.DMA],
        compiler_params=pltpu.CompilerParams(
            dimension_semantics=("arbitrary",), vmem_limit_bytes=LARGE_VMEM_LIMIT),
        name="in_proj",
    )(x2, g, w_in, gains)


def _attn_kernel(*refs, n_chunks, kv_chunks, heads_per_stack, tk, half_w, seq_len, rows, has_sink,
                 want_lse):
    it = iter(refs)
    q_ref, k_ref, v_ref, bias_ref = next(it), next(it), next(it), next(it)
    sink_ref = next(it) if has_sink else None
    o_ref = next(it)
    lse_ref = next(it) if want_lse else None

    n_tiles = seq_len // Q_TILE
    tiles_per_step = rows // Q_TILE
    chunks_per_group = n_chunks // kv_chunks
    assert (2 * chunks_per_group) % heads_per_stack == 0
    step = pl.program_id(1)
    lane = lax.broadcasted_iota(jnp.int32, (Q_TILE, LANES), 1)
    low_half = lane < HEAD_DIM
    ones = jnp.ones((tk, LANES), jnp.bfloat16)

    for sq, t in [(a, b) for a in range(q_ref.shape[0]) for b in range(tiles_per_step)]:
        tile = step * tiles_per_step + t
        q0 = tile * Q_TILE
        kv_rows = k_ref.shape[1]
        kv_row0 = jnp.clip(step * rows - half_w, 0, seq_len - kv_rows)
        start = pl.multiple_of(jnp.clip(q0 - half_w, 0, seq_len - tk) - kv_row0, HEAD_DIM)
        variant = jnp.where(tile == 0, 0, jnp.where(tile == n_tiles - 1, 2, 1))
        r0 = t * Q_TILE
        lse_tile = jnp.zeros((Q_TILE, LANES), jnp.float32)
        head_o, head_lse = {}, {}
        for g in range(2 * n_chunks // heads_per_stack):
            heads = range(g * heads_per_stack, (g + 1) * heads_per_stack)
            kv = (heads[0] // 2) // chunks_per_group
            kc = k_ref[sq, pl.ds(start, tk), kv * LANES:(kv + 1) * LANES]
            vc = v_ref[sq, pl.ds(start, tk), kv * LANES:(kv + 1) * LANES]
            v_aug = jnp.concatenate([vc, ones], axis=1)
            q_parts = []
            for h in heads:
                c = h // 2
                q2 = q_ref[sq, r0:r0 + Q_TILE, c * LANES:(c + 1) * LANES]
                keep = low_half if h % 2 == 0 else ~low_half
                q_parts.append(jnp.where(keep, q2, jnp.zeros_like(q2)))
            qs = q_parts[0] if len(q_parts) == 1 else jnp.concatenate(q_parts, axis=0)
            s = lax.dot_general(qs, kc, (((1,), (1,)), ((), ())),
                                preferred_element_type=jnp.float32)
            s = s + bias_ref[variant, g]
            m = jnp.max(s, axis=-1, keepdims=True)
            if has_sink:
                m = jnp.maximum(m, sink_ref[g])
            p = jnp.exp2(s - m)
            ov = jnp.dot(p.astype(jnp.bfloat16), v_aug, preferred_element_type=jnp.float32)
            o, l = ov[:, :LANES], ov[:, LANES:]
            if has_sink:
                l = l + jnp.exp2(sink_ref[g] - m)
            o = o * (1.0 / l)
            if want_lse:
                lse = m + jnp.log(l) * LOG2E
            for idx, h in enumerate(heads):
                head_o[h] = o[idx * Q_TILE:(idx + 1) * Q_TILE]
                if want_lse:
                    head_lse[h] = lse[idx * Q_TILE:(idx + 1) * Q_TILE]
                if h % 2 == 1:
                    c = h // 2
                    o2 = jnp.where(low_half, head_o.pop(h - 1), head_o.pop(h))
                    o_ref[sq, r0:r0 + Q_TILE, c * LANES:(c + 1) * LANES] = o2.astype(o_ref.dtype)
                    if want_lse:
                        lse_tile = jnp.where(lane == h - 1, head_lse.pop(h - 1),
                                             jnp.where(lane == h, head_lse.pop(h), lse_tile))
        if want_lse:
            lse_ref[sq, r0:r0 + Q_TILE, :] = lse_tile


def _banded_attention(q, k, v, bias, *, half_w, sink=None, want_lse=False, name):
    n_seq, L, qw = q.shape
    kw = k.shape[2]
    tk = Q_TILE + 2 * half_w
    rows = min(ATTN_STEP_ROWS, L)
    seqs = ATTN_STEP_ROWS // rows
    kv_rows = min(L, rows + 2 * half_w)

    def kv_index(s, i):
        row0 = pl.multiple_of(jnp.clip(i * rows - half_w, 0, L - kv_rows), HEAD_DIM)
        return s * seqs, row0, 0

    args = [q, k, v, bias]
    in_specs = [
        pl.BlockSpec((seqs, rows, qw), lambda s, i: (s, i, 0)),
        pl.BlockSpec((pl.Element(seqs), pl.Element(kv_rows), pl.Element(kw)), kv_index),
        pl.BlockSpec((pl.Element(seqs), pl.Element(kv_rows), pl.Element(kw)), kv_index),
        pl.BlockSpec(bias.shape, lambda s, i: (0, 0, 0, 0)),
    ]
    if sink is not None:
        args.append(sink)
        in_specs.append(pl.BlockSpec(sink.shape, lambda s, i: (0, 0, 0)))
    out_shape = [jax.ShapeDtypeStruct((n_seq, L, qw), jnp.bfloat16)]
    out_specs = [pl.BlockSpec((seqs, rows, qw), lambda s, i: (s, i, 0))]
    if want_lse:
        out_shape.append(jax.ShapeDtypeStruct((n_seq, L, LANES), jnp.float32))
        out_specs.append(pl.BlockSpec((seqs, rows, LANES), lambda s, i: (s, i, 0)))

    kern = functools.partial(
        _attn_kernel, n_chunks=qw // LANES, kv_chunks=kw // LANES,
        heads_per_stack=bias.shape[2] // Q_TILE, tk=tk, half_w=half_w, seq_len=L,
        rows=rows, has_sink=sink is not None, want_lse=want_lse)
    return pl.pallas_call(
        kern,
        out_shape=out_shape,
        grid=(n_seq // seqs, L // rows),
        in_specs=in_specs,
        out_specs=out_specs,
        compiler_params=pltpu.CompilerParams(
            dimension_semantics=("arbitrary", "arbitrary"), vmem_limit_bytes=VMEM_LIMIT),
        name=name,
    )(*args)


def _out_proj_router_kernel(*refs):
    nb = len(B_DILS)
    oa_ref = refs[0]
    o_refs = refs[1:1 + nb]
    lse_refs = refs[1 + nb:1 + 2 * nb]
    (x_ref, wo_ref, g_ref, wr_ref, br_ref,
     x1_ref, hf_ref, topi_ref, gate_ref, rank_ref, cnt_ref,
     tri_ref, carry_ref, so_ref, sl_ref, wo_bf_ref) = refs[1 + 2 * nb:]
    i = pl.program_id(0)
    rows = x_ref.shape[0]

    @pl.when(i == 0)
    def _():
        a = lax.broadcasted_iota(jnp.int32, (rows, rows), 0)
        b = lax.broadcasted_iota(jnp.int32, (rows, rows), 1)
        tri_ref[...] = jnp.where(a <= b, 1.0, 0.0).astype(jnp.bfloat16)
        carry_ref[...] = jnp.zeros_like(carry_ref)
        wo_bf_ref[...] = wo_ref[...].astype(jnp.bfloat16)

    outs, lses = [], []
    for bi, dil in enumerate(B_DILS):
        if dil == 1:
            outs.append(o_refs[bi][...].astype(jnp.float32))
            lses.append(lse_refs[bi][...])
        else:
            for res in range(dil):
                for j in range(B_W // LANES):
                    so_ref[bi, j, pl.ds(res, rows // dil, stride=dil), :] = (
                        o_refs[bi][0, res, :, j * LANES:(j + 1) * LANES].astype(jnp.float32))
                sl_ref[bi, pl.ds(res, rows // dil, stride=dil), :] = lse_refs[bi][0, res]
            outs.append(jnp.concatenate([so_ref[bi, j] for j in range(B_W // LANES)], axis=-1))
            lses.append(sl_ref[bi])

    mx = functools.reduce(jnp.maximum, lses)
    es = [jnp.exp2(l - mx) for l in lses]
    inv = 1.0 / functools.reduce(lambda a, b: a + b, es)
    eh = lax.broadcasted_iota(jnp.int32, (LANES, B_W), 0)
    ej = lax.broadcasted_iota(jnp.int32, (LANES, B_W), 1) // HEAD_DIM
    expand = jnp.where(eh == ej, 1.0, 0.0).astype(jnp.bfloat16)
    ob = jnp.zeros((rows, B_W), jnp.float32)
    for e, o in zip(es, outs):
        w = e * inv
        wide = jnp.dot(w.astype(jnp.bfloat16), expand, preferred_element_type=jnp.float32)
        ob = ob + wide * o

    attn = jnp.concatenate([oa_ref[...], ob.astype(jnp.bfloat16)], axis=-1)
    x1 = x_ref[...] + jnp.dot(attn, wo_bf_ref[...], preferred_element_type=jnp.float32)
    x1_ref[...] = x1
    hf = x1 * lax.rsqrt(jnp.mean(x1 * x1, axis=-1, keepdims=True) + NORM_EPS) * g_ref[...]
    hf_hi = hf.astype(jnp.bfloat16)
    hf_ref[...] = _pack_halves(hf)
    hf_lo = (hf - hf_hi.astype(jnp.float32)).astype(jnp.bfloat16)

    nt = (((1,), (1,)), ((), ()))
    lg_hi = lax.dot_general(wr_ref[...], hf_hi, nt, preferred_element_type=jnp.float32)
    lg_lo = lax.dot_general(wr_ref[0:N_EXPERTS, :], hf_lo, nt, preferred_element_type=jnp.float32)
    logits = lg_hi[0:N_EXPERTS] + lg_hi[N_EXPERTS:] + lg_lo + br_ref[:, 0:1]


    eidx = lax.broadcasted_iota(jnp.int32, (N_EXPERTS, rows), 0)
    work = logits
    vals, sels = [], []
    for k in range(TOP_K):
        mk = jnp.max(work, axis=0, keepdims=True)
        ik = jnp.min(jnp.where(work == mk, eidx, N_EXPERTS), axis=0, keepdims=True)
        sel = eidx == ik
        work = jnp.where(sel, -jnp.inf, work)
        vals.append(mk)
        sels.append(sel)
        topi_ref[k:k + 1, :] = ik
    exps = [jnp.exp(vk - vals[0]) for vk in vals]
    denom = exps[0] + exps[1] + exps[2] + exps[3]
    ginv = 1.0 / denom
    for k in range(TOP_K):
        gate_ref[k:k + 1, :] = exps[k] * ginv

    onehot = jnp.zeros((N_EXPERTS, rows), jnp.float32)
    for sel in sels:
        onehot = onehot + jnp.where(sel, 1.0, 0.0)
    incl = jnp.dot(onehot.astype(jnp.bfloat16), tri_ref[...], preferred_element_type=jnp.float32)
    before = incl - onehot + carry_ref[:, 0:1]
    for k in range(TOP_K):
        rk = jnp.sum(jnp.where(sels[k], before, 0.0), axis=0, keepdims=True)
        rank_ref[k:k + 1, :] = rk.astype(jnp.int32)
    carry = carry_ref[...] + jnp.sum(onehot, axis=1, keepdims=True)
    carry_ref[...] = carry
    cnt_ref[...] = carry.astype(jnp.int32)


def _out_proj_router(oa, outs_b, lses_b, x2, wo_bf, g, wr, br, seq):
    n = x2.shape[0]
    rows = PROJ_ROWS
    steps = seq // rows
    row_spec = lambda w: pl.BlockSpec((rows, w), lambda i: (i, 0))
    full = lambda a: pl.BlockSpec(a.shape, lambda i: (0,) * a.ndim)
    col_spec = pl.BlockSpec((TOP_K, rows), lambda i: (0, i))

    def branch_spec(dil, w):
        if dil == 1:
            return row_spec(w)
        return pl.BlockSpec((1, dil, rows // dil, w), lambda i: (i // steps, 0, i % steps, 0))

    in_specs = ([row_spec(A_Q_W)]
                + [branch_spec(d, B_W) for d in B_DILS]
                + [branch_spec(d, LANES) for d in B_DILS]
                + [row_spec(D_MODEL), full(wo_bf), full(g), full(wr), full(br)])
    return pl.pallas_call(
        _out_proj_router_kernel,
        out_shape=[
            jax.ShapeDtypeStruct((n, D_MODEL), jnp.float32),
            jax.ShapeDtypeStruct((n, HALF_D), jnp.uint32),
            jax.ShapeDtypeStruct((TOP_K, n), jnp.int32),
            jax.ShapeDtypeStruct((TOP_K, n), jnp.float32),
            jax.ShapeDtypeStruct((TOP_K, n), jnp.int32),
            jax.ShapeDtypeStruct((N_EXPERTS, LANES), jnp.int32),
        ],
        grid=(n // rows,),
        in_specs=in_specs,
        out_specs=[row_spec(D_MODEL), row_spec(HALF_D), col_spec, col_spec, col_spec,
                   pl.BlockSpec((N_EXPERTS, LANES), lambda i: (0, 0))],
        scratch_shapes=[pltpu.VMEM((rows, rows), jnp.bfloat16),
                        pltpu.VMEM((N_EXPERTS, LANES), jnp.float32),
                        pltpu.VMEM((len(B_DILS), B_W // LANES, rows, LANES), jnp.float32),
                        pltpu.VMEM((len(B_DILS), rows, LANES), jnp.float32),
                        pltpu.VMEM(wo_bf.shape, jnp.bfloat16)],
        compiler_params=pltpu.CompilerParams(
            dimension_semantics=("arbitrary",), vmem_limit_bytes=LARGE_VMEM_LIMIT),
        name="out_proj_router",
    )(oa, *outs_b, *lses_b, x2, wo_bf, g, wr, br)


def _mxu_dot(a_bf, w_f32):
    return lax.dot_general(a_bf, w_f32, (((1,), (0,)), ((), ())), preferred_element_type=jnp.float32)


def _moe_kernel(blk_exp_ref, first_ref, slot_ref, next_exp_ref, n_used_ref,
                x_ref, w1_hbm, b1_ref, w2_hbm, b2_ref, y_ref, w1_buf, w2_buf, sem):
    step = pl.program_id(0)

    def weight_copies(expert, slot):
        return (pltpu.make_async_copy(w1_hbm.at[expert], w1_buf.at[slot], sem.at[slot, 0]),
                pltpu.make_async_copy(w2_hbm.at[expert], w2_buf.at[slot], sem.at[slot, 1]))

    @pl.when(step * MOE_STEP_BLOCKS < n_used_ref[0])
    def _():
        @pl.when(step == 0)
        def _():
            for cp in weight_copies(blk_exp_ref[0], slot_ref[0]):
                cp.start()

        for j in range(MOE_STEP_BLOCKS):
            i = step * MOE_STEP_BLOCKS + j

            @pl.when(first_ref[i] == 1)
            def _():
                slot = slot_ref[i]
                for cp in weight_copies(blk_exp_ref[i], slot):
                    cp.wait()

                @pl.when(next_exp_ref[i] >= 0)
                def _():
                    for cp in weight_copies(next_exp_ref[i], (slot + 1) % MOE_WEIGHT_BUFS):
                        cp.start()

        for j in range(MOE_STEP_BLOCKS):
            i = step * MOE_STEP_BLOCKS + j
            slot = slot_ref[i]
            expert = blk_exp_ref[i]
            rs = slice(j * MOE_ROWS, (j + 1) * MOE_ROWS)
            x = jnp.concatenate(_unpack_halves(x_ref[rs, :]), axis=-1).astype(jnp.bfloat16)
            acc = jnp.zeros((MOE_ROWS, D_MODEL), jnp.float32)
            for c in range(D_FF // FF_CHUNK):
                lo = c * FF_CHUNK
                glu = _mxu_dot(x, w1_buf[slot, :, lo:lo + FF_CHUNK]) + b1_ref[expert, :, lo:lo + FF_CHUNK]
                lin = (_mxu_dot(x, w1_buf[slot, :, D_FF + lo:D_FF + lo + FF_CHUNK])
                       + b1_ref[expert, :, D_FF + lo:D_FF + lo + FF_CHUNK])
                glu = jnp.minimum(glu, SWIGLU_LIMIT)
                lin = jnp.clip(lin, -SWIGLU_LIMIT, SWIGLU_LIMIT)
                act = glu * (1.0 / (1.0 + jnp.exp(-SWIGLU_ALPHA * glu))) * (lin + 1.0)
                acc = acc + _mxu_dot(act.astype(jnp.bfloat16), w2_buf[slot, lo:lo + FF_CHUNK, :])
            y_ref[rs, :] = _pack_halves(acc + b2_ref[expert])


def _moe_plan(pends, n_blk):
    g = MOE_ROWS
    blk_row0 = jnp.arange(n_blk, dtype=jnp.int32) * g
    blk_exp = jnp.minimum(jnp.sum(pends[None, :] <= blk_row0[:, None], axis=-1),
                          N_EXPERTS - 1).astype(jnp.int32)
    n_used = (pends[-1] // g).astype(jnp.int32)
    used = blk_row0 < pends[-1]
    prev_exp = jnp.concatenate([jnp.full((1,), -1, jnp.int32), blk_exp[:-1]])
    first = (used & (blk_exp != prev_exp)).astype(jnp.int32)
    slot = ((jnp.cumsum(first) - 1) % MOE_WEIGHT_BUFS).astype(jnp.int32)
    pstarts = jnp.concatenate([jnp.zeros((1,), pends.dtype), pends[:-1]])
    nonempty = pends > pstarts
    experts = jnp.arange(N_EXPERTS, dtype=jnp.int32)
    later = nonempty[None, :] & (experts[None, :] > experts[:, None])
    next_nonempty = jnp.min(jnp.where(later, experts[None, :], N_EXPERTS), axis=-1)
    next_nonempty = jnp.where(next_nonempty == N_EXPERTS, -1, next_nonempty).astype(jnp.int32)
    next_exp = jnp.sum(jnp.where(blk_exp[:, None] == experts[None, :], next_nonempty[None, :], 0),
                       axis=-1).astype(jnp.int32)
    return blk_exp, first, slot, next_exp, n_used.reshape(1)


def _moe_experts(plan, xb, w1, b1, w2, b2):
    n_rows = xb.shape[0]
    n_blk = n_rows // MOE_ROWS

    step_rows = MOE_STEP_BLOCKS * MOE_ROWS

    def blk(i, *p):
        return jnp.minimum(i, (p[-1][0] - 1) // MOE_STEP_BLOCKS)

    grid_spec = pltpu.PrefetchScalarGridSpec(
        num_scalar_prefetch=len(plan),
        grid=(n_blk // MOE_STEP_BLOCKS,),
        in_specs=[
            pl.BlockSpec((step_rows, HALF_D), lambda i, *p: (blk(i, *p), 0)),
            pl.BlockSpec(memory_space=pl.ANY),
            pl.BlockSpec(b1.shape, lambda i, *p: (0, 0, 0)),
            pl.BlockSpec(memory_space=pl.ANY),
            pl.BlockSpec(b2.shape, lambda i, *p: (0, 0, 0)),
        ],
        out_specs=pl.BlockSpec((step_rows, HALF_D), lambda i, *p: (blk(i, *p), 0)),
        scratch_shapes=[pltpu.VMEM((MOE_WEIGHT_BUFS, D_MODEL, 2 * D_FF), jnp.float32),
                        pltpu.VMEM((MOE_WEIGHT_BUFS, D_FF, D_MODEL), jnp.float32),
                        pltpu.SemaphoreType.DMA((MOE_WEIGHT_BUFS, 2))],
    )
    return pl.pallas_call(
        _moe_kernel,
        out_shape=jax.ShapeDtypeStruct((n_rows, HALF_D), jnp.uint32),
        grid_spec=grid_spec,
        compiler_params=pltpu.CompilerParams(
            dimension_semantics=("arbitrary",), vmem_limit_bytes=LARGE_VMEM_LIMIT),
        name="moe_experts",
    )(*plan, xb, w1, b1, w2, b2)


def _combine_kernel(x1_ref, yg_ref, gate_ref, o_ref):
    acc_lo = x1_ref[:, :HALF_D]
    acc_hi = x1_ref[:, HALF_D:]
    rows = x1_ref.shape[0]
    gates = jnp.concatenate([gate_ref[...], jnp.zeros((8 - TOP_K, rows), jnp.float32)], axis=0).T
    for k in range(TOP_K):
        lo, hi = _unpack_halves(yg_ref[k])
        gk = gates[:, k:k + 1]
        acc_lo = acc_lo + lo * gk
        acc_hi = acc_hi + hi * gk
    o_ref[:, :HALF_D] = acc_lo
    o_ref[:, HALF_D:] = acc_hi


def _combine(acc, yg, gates_nk, split):
    n = acc.shape[0]
    rows = PROJ_ROWS
    steps = yg.shape[1] // rows
    first = split * steps
    return pl.pallas_call(
        _combine_kernel,
        out_shape=jax.ShapeDtypeStruct((n, D_MODEL), jnp.float32),
        grid=(steps,),
        in_specs=[pl.BlockSpec((rows, D_MODEL), lambda i: (first + i, 0)),
                  pl.BlockSpec((TOP_K, rows, HALF_D), lambda i: (0, i, 0)),
                  pl.BlockSpec((TOP_K, rows), lambda i: (0, first + i))],
        out_specs=pl.BlockSpec((rows, D_MODEL), lambda i: (first + i, 0)),
        input_output_aliases={0: 0},
        compiler_params=pltpu.CompilerParams(
            dimension_semantics=("arbitrary",), vmem_limit_bytes=VMEM_LIMIT),
        name="moe_combine",
    )(acc, yg, gates_nk)


def _sc_worker_id():
    return lax.axis_index("s") * SC_CORES + lax.axis_index("c")


def _sc_dispatch(hf, dest3, n_rows):
    n = hf.shape[0]
    chunks_per_worker = n // SC_CHUNK // SC_WORKERS
    mesh = plsc.VectorSubcoreMesh(core_axis_name="c", subcore_axis_name="s")

    @functools.partial(
        pl.kernel, mesh=mesh,
        out_type=jax.ShapeDtypeStruct((n_rows, HALF_D), hf.dtype),
        scratch_types=[pltpu.VMEM((2, TOP_K, SC_CHUNK), jnp.int32),
                       pltpu.VMEM((2, SC_CHUNK, HALF_D), hf.dtype),
                       pltpu.SemaphoreType.DMA((2,)),
                       pltpu.SemaphoreType.DMA((2,))],
        name="sc_dispatch")
    def run(hf_hbm, dest_hbm, xb_hbm, idx_v, rows_v, load_sem, scatter_sem):
        first = _sc_worker_id() * chunks_per_worker

        def load(j):
            slot = j % 2
            pltpu.sync_copy(dest_hbm.at[first + j], idx_v.at[slot])
            return pltpu.async_copy(hf_hbm.at[pl.ds((first + j) * SC_CHUNK, SC_CHUNK)],
                                    rows_v.at[slot], load_sem.at[slot])

        loads = {0: load(0)}
        scatters = {}
        for j in range(chunks_per_worker):
            slot = j % 2
            loads.pop(j).wait()
            scatters[j] = [pltpu.async_copy(rows_v.at[slot], xb_hbm.at[idx_v.at[slot, k]],
                                            scatter_sem.at[slot]) for k in range(TOP_K)]
            if j >= 1:
                for cp in scatters.pop(j - 1):
                    cp.wait()
            if j + 1 < chunks_per_worker:
                loads[j + 1] = load(j + 1)
        for cp in scatters.pop(chunks_per_worker - 1):
            cp.wait()

    return run(hf, dest3)


def _sc_collect(y, dest3):
    n = dest3.shape[0] * SC_COLLECT_CHUNK
    chunks_per_worker = n // SC_COLLECT_CHUNK // SC_WORKERS
    mesh = plsc.VectorSubcoreMesh(core_axis_name="c", subcore_axis_name="s")

    @functools.partial(
        pl.kernel, mesh=mesh,
        out_type=jax.ShapeDtypeStruct((TOP_K, n, HALF_D), y.dtype),
        scratch_types=[pltpu.VMEM((TOP_K, SC_COLLECT_CHUNK), jnp.int32),
                       pltpu.VMEM((TOP_K, SC_COLLECT_CHUNK, HALF_D), y.dtype),
                       pltpu.SemaphoreType.DMA((TOP_K,)),
                       pltpu.SemaphoreType.DMA((TOP_K,))],
        name="sc_collect")
    def run(y_hbm, dest_hbm, yg_hbm, idx_v, rows_v, gather_sem, write_sem):
        first = _sc_worker_id() * chunks_per_worker

        @pl.loop(0, chunks_per_worker)
        def _(j):
            ch = first + j
            pltpu.sync_copy(dest_hbm.at[ch], idx_v)
            gathers = [pltpu.async_copy(y_hbm.at[idx_v.at[k]], rows_v.at[k], gather_sem.at[k])
                       for k in range(TOP_K)]
            writes = []
            for k in range(TOP_K):
                gathers[k].wait()
                writes.append(pltpu.async_copy(
                    rows_v.at[k], yg_hbm.at[k, pl.ds(ch * SC_COLLECT_CHUNK, SC_COLLECT_CHUNK)],
                    write_sem.at[k]))
            for w in writes:
                w.wait()

    return run(y, dest3)


def _sc_collect_sum(y, dest3, gates16):
    C = SC_COLLECT_CHUNK
    n = dest3.shape[0] * C
    chunks_per_worker = n // C // SC_WORKERS
    lanes = 16
    mesh = plsc.VectorSubcoreMesh(core_axis_name="c", subcore_axis_name="s")

    @functools.partial(
        pl.kernel, mesh=mesh,
        out_type=jax.ShapeDtypeStruct((n, HALF_D), y.dtype),
        scratch_types=[pltpu.VMEM((TOP_K, C), jnp.int32),
                       pltpu.VMEM((TOP_K, C, lanes), jnp.uint32),
                       pltpu.VMEM((TOP_K, C, HALF_D), y.dtype),
                       pltpu.VMEM((C, HALF_D), y.dtype),
                       pltpu.SemaphoreType.DMA((TOP_K,))],
        compiler_params=pltpu.CompilerParams(needs_layout_passes=False),
        name="sc_collect_sum")
    def run(y_hbm, dest_hbm, gate_hbm, out_hbm, idx_v, gate_v, rows_v, sum_v, gather_sem):
        first = _sc_worker_id() * chunks_per_worker

        @pl.loop(0, chunks_per_worker)
        def _(i):
            ch = first + i
            pltpu.sync_copy(dest_hbm.at[ch], idx_v)
            gathers = [pltpu.async_copy(y_hbm.at[idx_v.at[k]], rows_v.at[k], gather_sem.at[k])
                       for k in range(TOP_K)]
            pltpu.sync_copy(gate_hbm.at[ch], gate_v)
            for cp in gathers:
                cp.wait()

            @pl.loop(0, C)
            def _(j):
                g = [plsc.bitcast(gate_v[k, j, :], jnp.bfloat16) for k in range(TOP_K)]
                for c in range(HALF_D // lanes):
                    sl = pl.ds(c * lanes, lanes)
                    acc = None
                    for k in range(TOP_K):
                        term = plsc.bitcast(rows_v[k, j, sl], jnp.bfloat16) * g[k]
                        acc = term if acc is None else acc + term
                    sum_v[j, sl] = plsc.bitcast(acc, y.dtype)

            pltpu.sync_copy(sum_v, out_hbm.at[pl.ds(ch * C, C)])

    return run(y, dest3, gates16)


def _combine_sum_kernel(x1_ref, ys_ref, o_ref):
    lo, hi = _unpack_halves(ys_ref[...])
    o_ref[:, :HALF_D] = x1_ref[:, :HALF_D] + lo
    o_ref[:, HALF_D:] = x1_ref[:, HALF_D:] + hi


def _combine_sum(acc, ysum, split):
    n = acc.shape[0]
    rows = PROJ_ROWS
    steps = ysum.shape[0] // rows
    first = split * steps
    return pl.pallas_call(
        _combine_sum_kernel,
        out_shape=jax.ShapeDtypeStruct((n, D_MODEL), jnp.float32),
        grid=(steps,),
        in_specs=[pl.BlockSpec((rows, D_MODEL), lambda i: (first + i, 0)),
                  pl.BlockSpec((rows, HALF_D), lambda i: (i, 0))],
        out_specs=pl.BlockSpec((rows, D_MODEL), lambda i: (first + i, 0)),
        input_output_aliases={0: 0},
        compiler_params=pltpu.CompilerParams(
            dimension_semantics=("arbitrary",), vmem_limit_bytes=VMEM_LIMIT),
        name="moe_combine",
    )(acc, ysum)


def _layer(x2, batch, seq, attn_norm_g, w_in, a_q_g, a_k_g, a_sinks, b_q_g, b_k_g, w_out,
           ffn_norm_g, w_router, b_router, w1, b1, w2, b2):
    n = x2.shape[0]
    slopes = _alibi_slopes()
    q_scale = HEAD_DIM ** -0.5 * LOG2E
    reps = MXU_DIM // HEAD_DIM
    gains = jnp.stack([jnp.tile(a_q_g, reps) * q_scale, jnp.tile(a_k_g, reps),
                       jnp.tile(b_q_g, reps) * q_scale, jnp.tile(b_k_g, reps)]).astype(jnp.float32)

    proj = _in_proj(x2, attn_norm_g.reshape(1, -1), w_in, gains, batch, seq)
    qa, ka, va = proj[:3]
    nb = len(B_DILS)
    qbs, kbs, vbs = proj[3:3 + nb], proj[3 + nb:3 + 2 * nb], proj[3 + 2 * nb:]

    bias_a = _bias_tables(slopes[:A_Q_HEADS], A_STACK_HEADS, A_HALF_WINDOW, 1, Q_TILE + 2 * A_HALF_WINDOW)
    sink_col = jnp.repeat(a_sinks.astype(jnp.float32) * LOG2E, Q_TILE).reshape(
        A_Q_HEADS // A_STACK_HEADS, A_STACK_HEADS * Q_TILE, 1)
    as_seqs = lambda a: a.reshape(batch, seq, a.shape[-1])
    out_a = _banded_attention(as_seqs(qa), as_seqs(ka), as_seqs(va), bias_a, half_w=A_HALF_WINDOW,
                              sink=sink_col, name="attn_a")[0].reshape(n, A_Q_W)

    outs_b, lses_b = [], []
    for bi, (window, dil) in enumerate(B_BRANCHES):
        half_w = window // (2 * dil)
        bias_b = _bias_tables(slopes[A_Q_HEADS:], B_STACK_HEADS, half_w, dil, Q_TILE + 2 * half_w)
        L = seq // dil
        to_seqs = lambda a: a.reshape(batch * dil, L, a.shape[-1])
        o, lse = _banded_attention(to_seqs(qbs[bi]), to_seqs(kbs[bi]), to_seqs(vbs[bi]), bias_b,
                                   half_w=half_w, want_lse=True, name=f"attn_b_d{dil}")
        if dil == 1:
            outs_b.append(o.reshape(n, B_W))
            lses_b.append(lse.reshape(n, LANES))
        else:
            outs_b.append(o.reshape(batch, dil, L, B_W))
            lses_b.append(lse.reshape(batch, dil, L, LANES))

    wr_t = w_router.T.astype(jnp.float32)
    wr_hi = wr_t.astype(jnp.bfloat16)
    wr_lo = (wr_t - wr_hi.astype(jnp.float32)).astype(jnp.bfloat16)
    wr = jnp.concatenate([wr_hi, wr_lo], axis=0)
    br = jnp.broadcast_to(b_router.astype(jnp.float32)[:, None], (N_EXPERTS, LANES))
    x1, hf, topi, gates, ranks, counts = _out_proj_router(
        out_a, outs_b, lses_b, x2, w_out, ffn_norm_g.reshape(1, -1), wr, br, seq)

    g = MOE_ROWS
    nk = n * TOP_K
    step_rows = MOE_STEP_BLOCKS * g
    n_rows = -(-(nk + N_EXPERTS * g) // step_rows) * step_rows
    cnt = counts[:, 0]
    pcnt = (cnt + g - 1) // g * g
    pends = jnp.cumsum(pcnt)
    pstarts = pends - pcnt
    experts = jnp.arange(N_EXPERTS, dtype=jnp.int32)
    start_of = jnp.sum(jnp.where(topi[:, :, None] == experts, pstarts, 0), axis=-1)
    dest = (start_of + ranks).astype(jnp.int32)
    plan = _moe_plan(pends, n_rows // g)
    dest3 = dest.reshape(TOP_K, n // SC_CHUNK, SC_CHUNK).transpose(1, 0, 2)

    xb = _sc_dispatch(hf, dest3, n_rows)
    y = _moe_experts(plan, xb, w1, b1[:, None, :], w2, b2[:, None, :])
    smallest_range = max(PROJ_ROWS, SC_COLLECT_CHUNK * SC_WORKERS)
    n_splits = max(1, min(COMBINE_SPLITS, n // smallest_range))
    per_split = n // n_splits
    chunked = lambda a: a.reshape(TOP_K, n // SC_COLLECT_CHUNK, SC_COLLECT_CHUNK).transpose(1, 0, 2)
    dest_c = chunked(dest)
    gate_bits = lax.bitcast_convert_type(gates.astype(jnp.bfloat16), jnp.uint16).astype(jnp.uint32)
    gate_words = gate_bits | (gate_bits << 16)
    gates_c = jnp.broadcast_to(chunked(gate_words)[..., None], dest_c.shape + (16,))
    chunks_per_split = per_split // SC_COLLECT_CHUNK
    out = x1
    for s in range(n_splits):
        rng = slice(s * chunks_per_split, (s + 1) * chunks_per_split)
        ysum = _sc_collect_sum(y, dest_c[rng], gates_c[rng])
        out = _combine_sum(out, ysum, s)
    return out


def kernel(x, attn_norm_g, w_in, a_q_norm_g, a_k_norm_g, a_sinks, b_q_norm_g, b_k_norm_g, w_out,
           ffn_norm_g, w_router, b_router, w1, b1, w2, b2):
    batch, seq, d = x.shape
    x2 = x.reshape(batch * seq, d)
    for i in range(attn_norm_g.shape[0]):
        x2 = _layer(x2, batch, seq, attn_norm_g[i], w_in[i], a_q_norm_g[i], a_k_norm_g[i],
                    a_sinks[i], b_q_norm_g[i], b_k_norm_g[i], w_out[i], ffn_norm_g[i],
                    w_router[i], b_router[i], w1[i], b1[i], w2[i], b2[i])
    return x2.reshape(batch, seq, d)
```

```python
import functools

import jax
import jax.numpy as jnp
import numpy as np
from jax import lax
from jax.experimental import pallas as pl
from jax.experimental.pallas import tpu as pltpu
from jax.experimental.pallas import tpu_sc as plsc

D_MODEL = 1024
HALF_D = D_MODEL // 2
HEAD_DIM = 64
LANES = 128
MXU_DIM = 256
A_Q_HEADS = 8
A_KV_HEADS = 2
B_HEADS = 8
A_HALF_WINDOW = 128
B_BRANCHES = ((128, 1), (512, 4), (2048, 16))
B_DILS = tuple(d for _, d in B_BRANCHES)
RESIDUE_STRIDE = 4
N_ALIBI_HEADS = 16
A_Q_W = A_Q_HEADS * HEAD_DIM
A_KV_W = A_KV_HEADS * HEAD_DIM
B_W = B_HEADS * HEAD_DIM
N_EXPERTS = 32
TOP_K = 4
D_FF = 1024
SWIGLU_ALPHA = 1.702
SWIGLU_LIMIT = 7.0
NORM_EPS = 1e-5
MASK_VALUE = -1e30
LOG2E = 1.4426950408889634

Q_TILE = 128
ATTN_STEP_ROWS = 2048
A_STACK_HEADS = 4
B_STACK_HEADS = 2
PROJ_ROWS = 1024
IN_PROJ_ROWS = 1024
MOE_ROWS = 512
FF_CHUNK = 512
MOE_STEP_BLOCKS = 2
MOE_WEIGHT_BUFS = 3
VMEM_LIMIT = 48 * 1024 * 1024
LARGE_VMEM_LIMIT = 58 * 1024 * 1024
SC_CORES = 2
SC_SUBCORES = 16
SC_WORKERS = SC_CORES * SC_SUBCORES
SC_CHUNK = 64
SC_COLLECT_CHUNK = 16
COMBINE_SPLITS = 8


def _pack_halves(v):
    lo = v[:, :HALF_D].astype(jnp.bfloat16).astype(jnp.float32)
    hi = v[:, HALF_D:].astype(jnp.bfloat16).astype(jnp.float32)
    return (pltpu.bitcast(lo, jnp.uint32) >> 16) | pltpu.bitcast(hi, jnp.uint32)


def _unpack_halves(w):
    lo = pltpu.bitcast(w << 16, jnp.float32)
    hi = pltpu.bitcast(w & jnp.uint32(0xFFFF0000), jnp.float32)
    return lo, hi


def _alibi_slopes():
    return np.exp2(-8.0 * np.arange(1, N_ALIBI_HEADS + 1, dtype=np.float32) / N_ALIBI_HEADS).astype(np.float32)


def _bias_tables(head_slopes, heads_per_group, half_w, dist_scale, tk):
    i = np.arange(Q_TILE)[:, None]
    j = np.arange(tk)[None, :]
    tabs = []
    for shift in (0, half_w, tk - Q_TILE):
        dist = np.abs(j - shift - i)
        valid = dist <= half_w
        per_head = []
        for sl in head_slopes:
            b = (-np.float64(sl) * LOG2E * (dist * dist_scale)).astype(np.float32)
            per_head.append(np.where(valid, b, np.float32(MASK_VALUE)).astype(np.float32))
        t = np.stack(per_head).reshape(-1, heads_per_group * Q_TILE, tk)
        tabs.append(t)
    return jnp.asarray(np.stack(tabs))


def _in_proj_kernel(x_ref, g_ref, w_hbm, gains_ref, qa_ref, ka_ref, va_ref, *rest):
    b_refs, (scr_ref, scr2_ref, w_ref, stage_ref, sem) = rest[:-5], rest[-5:]

    @pl.when(pl.program_id(0) == 0)
    def _():
        width = stage_ref.shape[1]
        for c0 in range(0, w_ref.shape[1], width):
            cp = pltpu.make_async_copy(w_hbm.at[:, c0:c0 + width], stage_ref, sem)
            cp.start()
            cp.wait()
            w_ref[:, c0:c0 + width] = stage_ref[...].astype(jnp.bfloat16)

    x = x_ref[...]
    xn = x * lax.rsqrt(jnp.mean(x * x, axis=-1, keepdims=True) + NORM_EPS) * g_ref[...]
    xn = xn.astype(jnp.bfloat16)
    r = lax.broadcasted_iota(jnp.int32, (MXU_DIM, MXU_DIM), 0) // HEAD_DIM
    c = lax.broadcasted_iota(jnp.int32, (MXU_DIM, MXU_DIM), 1) // HEAD_DIM
    blockdiag = jnp.where(r == c, 1.0, 0.0).astype(jnp.bfloat16)

    def head_rms(sec, gain_row):
        width = sec.shape[1]
        parts = []
        step = min(width, MXU_DIM)
        for j in range(width // step):
            p = sec[:, j * step:(j + 1) * step]
            ss = jnp.dot((p * p).astype(jnp.bfloat16), blockdiag[:step, :step],
                         preferred_element_type=jnp.float32)
            parts.append(p * lax.rsqrt(ss * (1.0 / HEAD_DIM) + NORM_EPS)
                         * gains_ref[gain_row:gain_row + 1, :step])
        return parts

    def project(col0, width, gain_row):
        sec = jnp.dot(xn, w_ref[:, col0:col0 + width], preferred_element_type=jnp.float32)
        return [sec] if gain_row is None else head_rms(sec, gain_row)

    def store(out_ref, parts):
        w = parts[0].shape[1]
        for j, p in enumerate(parts):
            out_ref[:, j * w:(j + 1) * w] = p.astype(out_ref.dtype)

    def per_kv_head(p):
        lane = lax.broadcasted_iota(jnp.int32, p.shape, 1)
        swapped = pltpu.roll(p, HEAD_DIM, axis=1)
        low = lane < HEAD_DIM
        return [jnp.where(low, p, swapped), jnp.where(low, swapped, p)]

    store(qa_ref, project(0, A_Q_W, 0))
    kva = project(A_Q_W, 2 * A_KV_W, None)[0]
    store(ka_ref, per_kv_head(head_rms(kva[:, :A_KV_W], 1)[0]))
    store(va_ref, per_kv_head(kva[:, A_KV_W:]))

    rows = x_ref.shape[0]
    col0 = A_Q_W + 2 * A_KV_W
    for t, gain_row in enumerate((2, 3, None)):
        parts = project(col0 + t * B_W, B_W, gain_row)
        sec = jnp.concatenate(parts, axis=-1) if len(parts) > 1 else parts[0]
        for j in range(B_W // LANES):
            scr_ref[j] = sec[:, j * LANES:(j + 1) * LANES]
        n_lane_chunks = B_W // LANES
        prev_dil = 1
        for bi, dil in enumerate(B_DILS):
            out_ref = b_refs[t * len(B_DILS) + bi]
            if dil == 1:
                out_ref[...] = sec.astype(out_ref.dtype)
                continue
            assert dil == prev_dil * RESIDUE_STRIDE
            last = dil == B_DILS[-1]
            for res in range(dil):
                r_prev, r_sub = res % prev_dil, res // prev_dil
                for j in range(n_lane_chunks):
                    if prev_dil == 1:
                        v = scr_ref[j, pl.ds(r_sub, rows // dil, stride=RESIDUE_STRIDE), :]
                    else:
                        v = scr2_ref[r_prev * n_lane_chunks + j,
                                     pl.ds(r_sub, rows // dil, stride=RESIDUE_STRIDE), :]
                    out_ref[0, res, :, j * LANES:(j + 1) * LANES] = v.astype(out_ref.dtype)
                    if not last:
                        scr2_ref[res * n_lane_chunks + j] = v
            prev_dil = dil


def _in_proj(x2, g, w_in, gains, batch, seq):
    n = x2.shape[0]
    rows = IN_PROJ_ROWS
    steps = seq // rows
    a_widths = (A_Q_W, 2 * A_KV_W, 2 * A_KV_W)
    out_shape = [jax.ShapeDtypeStruct((n, w), jnp.bfloat16) for w in a_widths]
    out_specs = [pl.BlockSpec((rows, w), lambda i: (i, 0)) for w in a_widths]
    for _ in range(3):
        for dil in B_DILS:
            if dil == 1:
                out_shape.append(jax.ShapeDtypeStruct((n, B_W), jnp.bfloat16))
                out_specs.append(pl.BlockSpec((rows, B_W), lambda i: (i, 0)))
            else:
                out_shape.append(jax.ShapeDtypeStruct((batch, dil, seq // dil, B_W), jnp.bfloat16))
                out_specs.append(pl.BlockSpec((1, dil, rows // dil, B_W),
                                              lambda i: (i // steps, 0, i % steps, 0)))
    return pl.pallas_call(
        _in_proj_kernel,
        out_shape=out_shape,
        grid=(n // rows,),
        in_specs=[
            pl.BlockSpec((rows, D_MODEL), lambda i: (i, 0)),
            pl.BlockSpec((1, D_MODEL), lambda i: (0, 0)),
            pl.BlockSpec(memory_space=pl.ANY),
            pl.BlockSpec(gains.shape, lambda i: (0, 0)),
        ],
        out_specs=out_specs,
        scratch_shapes=[pltpu.VMEM((B_W // LANES, rows, LANES), jnp.float32),
                        pltpu.VMEM((RESIDUE_STRIDE * B_W // LANES, rows // RESIDUE_STRIDE, LANES),
                                   jnp.float32),
                        pltpu.VMEM(w_in.shape, jnp.bfloat16),
                        pltpu.VMEM((w_in.shape[0], w_in.shape[1] // 3), jnp.float32),
                        pltpu.SemaphoreType.DMA],
        compiler_params=pltpu.CompilerParams(
            dimension_semantics=("arbitrary",), vmem_limit_bytes=LARGE_VMEM_LIMIT),
        name="in_proj",
    )(x2, g, w_in, gains)


def _attn_kernel(*refs, n_chunks, kv_chunks, heads_per_stack, tk, half_w, seq_len, rows, has_sink,
                 want_lse):
    it = iter(refs)
    q_ref, k_ref, v_ref, bias_ref = next(it), next(it), next(it), next(it)
    sink_ref = next(it) if has_sink else None
    o_ref = next(it)
    lse_ref = next(it) if want_lse else None

    n_tiles = seq_len // Q_TILE
    tiles_per_step = rows // Q_TILE
    chunks_per_group = n_chunks // kv_chunks
    assert (2 * chunks_per_group) % heads_per_stack == 0
    step = pl.program_id(1)
    lane = lax.broadcasted_iota(jnp.int32, (Q_TILE, LANES), 1)
    low_half = lane < HEAD_DIM
    ones = jnp.ones((tk, LANES), jnp.bfloat16)

    for sq, t in [(a, b) for a in range(q_ref.shape[0]) for b in range(tiles_per_step)]:
        tile = step * tiles_per_step + t
        q0 = tile * Q_TILE
        kv_rows = k_ref.shape[1]
        kv_row0 = jnp.clip(step * rows - half_w, 0, seq_len - kv_rows)
        start = pl.multiple_of(jnp.clip(q0 - half_w, 0, seq_len - tk) - kv_row0, HEAD_DIM)
        variant = jnp.where(tile == 0, 0, jnp.where(tile == n_tiles - 1, 2, 1))
        r0 = t * Q_TILE
        lse_tile = jnp.zeros((Q_TILE, LANES), jnp.float32)
        head_o, head_lse = {}, {}
        for g in range(2 * n_chunks // heads_per_stack):
            heads = range(g * heads_per_stack, (g + 1) * heads_per_stack)
            kv = (heads[0] // 2) // chunks_per_group
            kc = k_ref[sq, pl.ds(start, tk), kv * LANES:(kv + 1) * LANES]
            vc = v_ref[sq, pl.ds(start, tk), kv * LANES:(kv + 1) * LANES]
            v_aug = jnp.concatenate([vc, ones], axis=1)
            q_parts = []
            for h in heads:
                c = h // 2
                q2 = q_ref[sq, r0:r0 + Q_TILE, c * LANES:(c + 1) * LANES]
                keep = low_half if h % 2 == 0 else ~low_half
                q_parts.append(jnp.where(keep, q2, jnp.zeros_like(q2)))
            qs = q_parts[0] if len(q_parts) == 1 else jnp.concatenate(q_parts, axis=0)
            s = lax.dot_general(qs, kc, (((1,), (1,)), ((), ())),
                                preferred_element_type=jnp.float32)
            s = s + bias_ref[variant, g]
            m = jnp.max(s, axis=-1, keepdims=True)
            if has_sink:
                m = jnp.maximum(m, sink_ref[g])
            p = jnp.exp2(s - m)
            ov = jnp.dot(p.astype(jnp.bfloat16), v_aug, preferred_element_type=jnp.float32)
            o, l = ov[:, :LANES], ov[:, LANES:]
            if has_sink:
                l = l + jnp.exp2(sink_ref[g] - m)
            o = o * (1.0 / l)
            if want_lse:
                lse = m + jnp.log(l) * LOG2E
            for idx, h in enumerate(heads):
                head_o[h] = o[idx * Q_TILE:(idx + 1) * Q_TILE]
                if want_lse:
                    head_lse[h] = lse[idx * Q_TILE:(idx + 1) * Q_TILE]
                if h % 2 == 1:
                    c = h // 2
                    o2 = jnp.where(low_half, head_o.pop(h - 1), head_o.pop(h))
                    o_ref[sq, r0:r0 + Q_TILE, c * LANES:(c + 1) * LANES] = o2.astype(o_ref.dtype)
                    if want_lse:
                        lse_tile = jnp.where(lane == h - 1, head_lse.pop(h - 1),
                                             jnp.where(lane == h, head_lse.pop(h), lse_tile))
        if want_lse:
            lse_ref[sq, r0:r0 + Q_TILE, :] = lse_tile


def _banded_attention(q, k, v, bias, *, half_w, sink=None, want_lse=False, name):
    n_seq, L, qw = q.shape
    kw = k.shape[2]
    tk = Q_TILE + 2 * half_w
    rows = min(ATTN_STEP_ROWS, L)
    seqs = ATTN_STEP_ROWS // rows
    kv_rows = min(L, rows + 2 * half_w)

    def kv_index(s, i):
        row0 = pl.multiple_of(jnp.clip(i * rows - half_w, 0, L - kv_rows), HEAD_DIM)
        return s * seqs, row0, 0

    args = [q, k, v, bias]
    in_specs = [
        pl.BlockSpec((seqs, rows, qw), lambda s, i: (s, i, 0)),
        pl.BlockSpec((pl.Element(seqs), pl.Element(kv_rows), pl.Element(kw)), kv_index),
        pl.BlockSpec((pl.Element(seqs), pl.Element(kv_rows), pl.Element(kw)), kv_index),
        pl.BlockSpec(bias.shape, lambda s, i: (0, 0, 0, 0)),
    ]
    if sink is not None:
        args.append(sink)
        in_specs.append(pl.BlockSpec(sink.shape, lambda s, i: (0, 0, 0)))
    out_shape = [jax.ShapeDtypeStruct((n_seq, L, qw), jnp.bfloat16)]
    out_specs = [pl.BlockSpec((seqs, rows, qw), lambda s, i: (s, i, 0))]
    if want_lse:
        out_shape.append(jax.ShapeDtypeStruct((n_seq, L, LANES), jnp.float32))
        out_specs.append(pl.BlockSpec((seqs, rows, LANES), lambda s, i: (s, i, 0)))

    kern = functools.partial(
        _attn_kernel, n_chunks=qw // LANES, kv_chunks=kw // LANES,
        heads_per_stack=bias.shape[2] // Q_TILE, tk=tk, half_w=half_w, seq_len=L,
        rows=rows, has_sink=sink is not None, want_lse=want_lse)
    return pl.pallas_call(
        kern,
        out_shape=out_shape,
        grid=(n_seq // seqs, L // rows),
        in_specs=in_specs,
        out_specs=out_specs,
        compiler_params=pltpu.CompilerParams(
            dimension_semantics=("arbitrary", "arbitrary"), vmem_limit_bytes=VMEM_LIMIT),
        name=name,
    )(*args)


def _out_proj_router_kernel(*refs):
    nb = len(B_DILS)
    oa_ref = refs[0]
    o_refs = refs[1:1 + nb]
    lse_refs = refs[1 + nb:1 + 2 * nb]
    (x_ref, wo_ref, g_ref, wr_ref, br_ref,
     x1_ref, hf_ref, topi_ref, gate_ref, rank_ref, cnt_ref,
     tri_ref, carry_ref, so_ref, sl_ref, wo_bf_ref) = refs[1 + 2 * nb:]
    i = pl.program_id(0)
    rows = x_ref.shape[0]

    @pl.when(i == 0)
    def _():
        a = lax.broadcasted_iota(jnp.int32, (rows, rows), 0)
        b = lax.broadcasted_iota(jnp.int32, (rows, rows), 1)
        tri_ref[...] = jnp.where(a <= b, 1.0, 0.0).astype(jnp.bfloat16)
        carry_ref[...] = jnp.zeros_like(carry_ref)
        wo_bf_ref[...] = wo_ref[...].astype(jnp.bfloat16)

    outs, lses = [], []
    for bi, dil in enumerate(B_DILS):
        if dil == 1:
            outs.append(o_refs[bi][...].astype(jnp.float32))
            lses.append(lse_refs[bi][...])
        else:
            for res in range(dil):
                for j in range(B_W // LANES):
                    so_ref[bi, j, pl.ds(res, rows // dil, stride=dil), :] = (
                        o_refs[bi][0, res, :, j * LANES:(j + 1) * LANES].astype(jnp.float32))
                sl_ref[bi, pl.ds(res, rows // dil, stride=dil), :] = lse_refs[bi][0, res]
            outs.append(jnp.concatenate([so_ref[bi, j] for j in range(B_W // LANES)], axis=-1))
            lses.append(sl_ref[bi])

    mx = functools.reduce(jnp.maximum, lses)
    es = [jnp.exp2(l - mx) for l in lses]
    inv = 1.0 / functools.reduce(lambda a, b: a + b, es)
    eh = lax.broadcasted_iota(jnp.int32, (LANES, B_W), 0)
    ej = lax.broadcasted_iota(jnp.int32, (LANES, B_W), 1) // HEAD_DIM
    expand = jnp.where(eh == ej, 1.0, 0.0).astype(jnp.bfloat16)
    ob = jnp.zeros((rows, B_W), jnp.float32)
    for e, o in zip(es, outs):
        w = e * inv
        wide = jnp.dot(w.astype(jnp.bfloat16), expand, preferred_element_type=jnp.float32)
        ob = ob + wide * o

    attn = jnp.concatenate([oa_ref[...], ob.astype(jnp.bfloat16)], axis=-1)
    x1 = x_ref[...] + jnp.dot(attn, wo_bf_ref[...], preferred_element_type=jnp.float32)
    x1_ref[...] = x1
    hf = x1 * lax.rsqrt(jnp.mean(x1 * x1, axis=-1, keepdims=True) + NORM_EPS) * g_ref[...]
    hf_hi = hf.astype(jnp.bfloat16)
    hf_ref[...] = _pack_halves(hf)
    hf_lo = (hf - hf_hi.astype(jnp.float32)).astype(jnp.bfloat16)

    nt = (((1,), (1,)), ((), ()))
    lg_hi = lax.dot_general(wr_ref[...], hf_hi, nt, preferred_element_type=jnp.float32)
    lg_lo = lax.dot_general(wr_ref[0:N_EXPERTS, :], hf_lo, nt, preferred_element_type=jnp.float32)
    logits = lg_hi[0:N_EXPERTS] + lg_hi[N_EXPERTS:] + lg_lo + br_ref[:, 0:1]


    eidx = lax.broadcasted_iota(jnp.int32, (N_EXPERTS, rows), 0)
    work = logits
    vals, sels = [], []
    for k in range(TOP_K):
        mk = jnp.max(work, axis=0, keepdims=True)
        ik = jnp.min(jnp.where(work == mk, eidx, N_EXPERTS), axis=0, keepdims=True)
        sel = eidx == ik
        work = jnp.where(sel, -jnp.inf, work)
        vals.append(mk)
        sels.append(sel)
        topi_ref[k:k + 1, :] = ik
    exps = [jnp.exp(vk - vals[0]) for vk in vals]
    denom = exps[0] + exps[1] + exps[2] + exps[3]
    ginv = 1.0 / denom
    for k in range(TOP_K):
        gate_ref[k:k + 1, :] = exps[k] * ginv

    onehot = jnp.zeros((N_EXPERTS, rows), jnp.float32)
    for sel in sels:
        onehot = onehot + jnp.where(sel, 1.0, 0.0)
    incl = jnp.dot(onehot.astype(jnp.bfloat16), tri_ref[...], preferred_element_type=jnp.float32)
    before = incl - onehot + carry_ref[:, 0:1]
    for k in range(TOP_K):
        rk = jnp.sum(jnp.where(sels[k], before, 0.0), axis=0, keepdims=True)
        rank_ref[k:k + 1, :] = rk.astype(jnp.int32)
    carry = carry_ref[...] + jnp.sum(onehot, axis=1, keepdims=True)
    carry_ref[...] = carry
    cnt_ref[...] = carry.astype(jnp.int32)


def _out_proj_router(oa, outs_b, lses_b, x2, wo_bf, g, wr, br, seq):
    n = x2.shape[0]
    rows = PROJ_ROWS
    steps = seq // rows
    row_spec = lambda w: pl.BlockSpec((rows, w), lambda i: (i, 0))
    full = lambda a: pl.BlockSpec(a.shape, lambda i: (0,) * a.ndim)
    col_spec = pl.BlockSpec((TOP_K, rows), lambda i: (0, i))

    def branch_spec(dil, w):
        if dil == 1:
            return row_spec(w)
        return pl.BlockSpec((1, dil, rows // dil, w), lambda i: (i // steps, 0, i % steps, 0))

    in_specs = ([row_spec(A_Q_W)]
                + [branch_spec(d, B_W) for d in B_DILS]
                + [branch_spec(d, LANES) for d in B_DILS]
                + [row_spec(D_MODEL), full(wo_bf), full(g), full(wr), full(br)])
    return pl.pallas_call(
        _out_proj_router_kernel,
        out_shape=[
            jax.ShapeDtypeStruct((n, D_MODEL), jnp.float32),
            jax.ShapeDtypeStruct((n, HALF_D), jnp.uint32),
            jax.ShapeDtypeStruct((TOP_K, n), jnp.int32),
            jax.ShapeDtypeStruct((TOP_K, n), jnp.float32),
            jax.ShapeDtypeStruct((TOP_K, n), jnp.int32),
            jax.ShapeDtypeStruct((N_EXPERTS, LANES), jnp.int32),
        ],
        grid=(n // rows,),
        in_specs=in_specs,
        out_specs=[row_spec(D_MODEL), row_spec(HALF_D), col_spec, col_spec, col_spec,
                   pl.BlockSpec((N_EXPERTS, LANES), lambda i: (0, 0))],
        scratch_shapes=[pltpu.VMEM((rows, rows), jnp.bfloat16),
                        pltpu.VMEM((N_EXPERTS, LANES), jnp.float32),
                        pltpu.VMEM((len(B_DILS), B_W // LANES, rows, LANES), jnp.float32),
                        pltpu.VMEM((len(B_DILS), rows, LANES), jnp.float32),
                        pltpu.VMEM(wo_bf.shape, jnp.bfloat16)],
        compiler_params=pltpu.CompilerParams(
            dimension_semantics=("arbitrary",), vmem_limit_bytes=LARGE_VMEM_LIMIT),
        name="out_proj_router",
    )(oa, *outs_b, *lses_b, x2, wo_bf, g, wr, br)


def _mxu_dot(a_bf, w_f32):
    return lax.dot_general(a_bf, w_f32, (((1,), (0,)), ((), ())), preferred_element_type=jnp.float32)


def _moe_kernel(blk_exp_ref, first_ref, slot_ref, next_exp_ref, n_used_ref,
                x_ref, w1_hbm, b1_ref, w2_hbm, b2_ref, y_ref, w1_buf, w2_buf, sem):
    step = pl.program_id(0)

    def weight_copies(expert, slot):
        return (pltpu.make_async_copy(w1_hbm.at[expert], w1_buf.at[slot], sem.at[slot, 0]),
                pltpu.make_async_copy(w2_hbm.at[expert], w2_buf.at[slot], sem.at[slot, 1]))

    @pl.when(step * MOE_STEP_BLOCKS < n_used_ref[0])
    def _():
        @pl.when(step == 0)
        def _():
            for cp in weight_copies(blk_exp_ref[0], slot_ref[0]):
                cp.start()

        for j in range(MOE_STEP_BLOCKS):
            i = step * MOE_STEP_BLOCKS + j

            @pl.when(first_ref[i] == 1)
            def _():
                slot = slot_ref[i]
                for cp in weight_copies(blk_exp_ref[i], slot):
                    cp.wait()

                @pl.when(next_exp_ref[i] >= 0)
                def _():
                    for cp in weight_copies(next_exp_ref[i], (slot + 1) % MOE_WEIGHT_BUFS):
                        cp.start()

        for j in range(MOE_STEP_BLOCKS):
            i = step * MOE_STEP_BLOCKS + j
            slot = slot_ref[i]
            expert = blk_exp_ref[i]
            rs = slice(j * MOE_ROWS, (j + 1) * MOE_ROWS)
            x = jnp.concatenate(_unpack_halves(x_ref[rs, :]), axis=-1).astype(jnp.bfloat16)
            acc = jnp.zeros((MOE_ROWS, D_MODEL), jnp.float32)
            for c in range(D_FF // FF_CHUNK):
                lo = c * FF_CHUNK
                glu = _mxu_dot(x, w1_buf[slot, :, lo:lo + FF_CHUNK]) + b1_ref[expert, :, lo:lo + FF_CHUNK]
                lin = (_mxu_dot(x, w1_buf[slot, :, D_FF + lo:D_FF + lo + FF_CHUNK])
                       + b1_ref[expert, :, D_FF + lo:D_FF + lo + FF_CHUNK])
                glu = jnp.minimum(glu, SWIGLU_LIMIT)
                lin = jnp.clip(lin, -SWIGLU_LIMIT, SWIGLU_LIMIT)
                act = glu * (1.0 / (1.0 + jnp.exp(-SWIGLU_ALPHA * glu))) * (lin + 1.0)
                acc = acc + _mxu_dot(act.astype(jnp.bfloat16), w2_buf[slot, lo:lo + FF_CHUNK, :])
            y_ref[rs, :] = _pack_halves(acc + b2_ref[expert])


def _moe_plan(pends, n_blk):
    g = MOE_ROWS
    blk_row0 = jnp.arange(n_blk, dtype=jnp.int32) * g
    blk_exp = jnp.minimum(jnp.sum(pends[None, :] <= blk_row0[:, None], axis=-1),
                          N_EXPERTS - 1).astype(jnp.int32)
    n_used = (pends[-1] // g).astype(jnp.int32)
    used = blk_row0 < pends[-1]
    prev_exp = jnp.concatenate([jnp.full((1,), -1, jnp.int32), blk_exp[:-1]])
    first = (used & (blk_exp != prev_exp)).astype(jnp.int32)
    slot = ((jnp.cumsum(first) - 1) % MOE_WEIGHT_BUFS).astype(jnp.int32)
    pstarts = jnp.concatenate([jnp.zeros((1,), pends.dtype), pends[:-1]])
    nonempty = pends > pstarts
    experts = jnp.arange(N_EXPERTS, dtype=jnp.int32)
    later = nonempty[None, :] & (experts[None, :] > experts[:, None])
    next_nonempty = jnp.min(jnp.where(later, experts[None, :], N_EXPERTS), axis=-1)
    next_nonempty = jnp.where(next_nonempty == N_EXPERTS, -1, next_nonempty).astype(jnp.int32)
    next_exp = jnp.sum(jnp.where(blk_exp[:, None] == experts[None, :], next_nonempty[None, :], 0),
                       axis=-1).astype(jnp.int32)
    return blk_exp, first, slot, next_exp, n_used.reshape(1)


def _moe_experts(plan, xb, w1, b1, w2, b2):
    n_rows = xb.shape[0]
    n_blk = n_rows // MOE_ROWS

    step_rows = MOE_STEP_BLOCKS * MOE_ROWS

    def blk(i, *p):
        return jnp.minimum(i, (p[-1][0] - 1) // MOE_STEP_BLOCKS)

    grid_spec = pltpu.PrefetchScalarGridSpec(
        num_scalar_prefetch=len(plan),
        grid=(n_blk // MOE_STEP_BLOCKS,),
        in_specs=[
            pl.BlockSpec((step_rows, HALF_D), lambda i, *p: (blk(i, *p), 0)),
            pl.BlockSpec(memory_space=pl.ANY),
            pl.BlockSpec(b1.shape, lambda i, *p: (0, 0, 0)),
            pl.BlockSpec(memory_space=pl.ANY),
            pl.BlockSpec(b2.shape, lambda i, *p: (0, 0, 0)),
        ],
        out_specs=pl.BlockSpec((step_rows, HALF_D), lambda i, *p: (blk(i, *p), 0)),
        scratch_shapes=[pltpu.VMEM((MOE_WEIGHT_BUFS, D_MODEL, 2 * D_FF), jnp.float32),
                        pltpu.VMEM((MOE_WEIGHT_BUFS, D_FF, D_MODEL), jnp.float32),
                        pltpu.SemaphoreType.DMA((MOE_WEIGHT_BUFS, 2))],
    )
    return pl.pallas_call(
        _moe_kernel,
        out_shape=jax.ShapeDtypeStruct((n_rows, HALF_D), jnp.uint32),
        grid_spec=grid_spec,
        compiler_params=pltpu.CompilerParams(
            dimension_semantics=("arbitrary",), vmem_limit_bytes=LARGE_VMEM_LIMIT),
        name="moe_experts",
    )(*plan, xb, w1, b1, w2, b2)


def _combine_kernel(x1_ref, yg_ref, gate_ref, o_ref):
    acc_lo = x1_ref[:, :HALF_D]
    acc_hi = x1_ref[:, HALF_D:]
    rows = x1_ref.shape[0]
    gates = jnp.concatenate([gate_ref[...], jnp.zeros((8 - TOP_K, rows), jnp.float32)], axis=0).T
    for k in range(TOP_K):
        lo, hi = _unpack_halves(yg_ref[k])
        gk = gates[:, k:k + 1]
        acc_lo = acc_lo + lo * gk
        acc_hi = acc_hi + hi * gk
    o_ref[:, :HALF_D] = acc_lo
    o_ref[:, HALF_D:] = acc_hi


def _combine(acc, yg, gates_nk, split):
    n = acc.shape[0]
    rows = PROJ_ROWS
    steps = yg.shape[1] // rows
    first = split * steps
    return pl.pallas_call(
        _combine_kernel,
        out_shape=jax.ShapeDtypeStruct((n, D_MODEL), jnp.float32),
        grid=(steps,),
        in_specs=[pl.BlockSpec((rows, D_MODEL), lambda i: (first + i, 0)),
                  pl.BlockSpec((TOP_K, rows, HALF_D), lambda i: (0, i, 0)),
                  pl.BlockSpec((TOP_K, rows), lambda i: (0, first + i))],
        out_specs=pl.BlockSpec((rows, D_MODEL), lambda i: (first + i, 0)),
        input_output_aliases={0: 0},
        compiler_params=pltpu.CompilerParams(
            dimension_semantics=("arbitrary",), vmem_limit_bytes=VMEM_LIMIT),
        name="moe_combine",
    )(acc, yg, gates_nk)


def _sc_worker_id():
    return lax.axis_index("s") * SC_CORES + lax.axis_index("c")


def _sc_dispatch(hf, dest3, n_rows):
    n = hf.shape[0]
    chunks_per_worker = n // SC_CHUNK // SC_WORKERS
    mesh = plsc.VectorSubcoreMesh(core_axis_name="c", subcore_axis_name="s")

    @functools.partial(
        pl.kernel, mesh=mesh,
        out_type=jax.ShapeDtypeStruct((n_rows, HALF_D), hf.dtype),
        scratch_types=[pltpu.VMEM((2, TOP_K, SC_CHUNK), jnp.int32),
                       pltpu.VMEM((2, SC_CHUNK, HALF_D), hf.dtype),
                       pltpu.SemaphoreType.DMA((2,)),
                       pltpu.SemaphoreType.DMA((2,))],
        name="sc_dispatch")
    def run(hf_hbm, dest_hbm, xb_hbm, idx_v, rows_v, load_sem, scatter_sem):
        first = _sc_worker_id() * chunks_per_worker

        def load(j):
            slot = j % 2
            pltpu.sync_copy(dest_hbm.at[first + j], idx_v.at[slot])
            return pltpu.async_copy(hf_hbm.at[pl.ds((first + j) * SC_CHUNK, SC_CHUNK)],
                                    rows_v.at[slot], load_sem.at[slot])

        loads = {0: load(0)}
        scatters = {}
        for j in range(chunks_per_worker):
            slot = j % 2
            loads.pop(j).wait()
            scatters[j] = [pltpu.async_copy(rows_v.at[slot], xb_hbm.at[idx_v.at[slot, k]],
                                            scatter_sem.at[slot]) for k in range(TOP_K)]
            if j >= 1:
                for cp in scatters.pop(j - 1):
                    cp.wait()
            if j + 1 < chunks_per_worker:
                loads[j + 1] = load(j + 1)
        for cp in scatters.pop(chunks_per_worker - 1):
            cp.wait()

    return run(hf, dest3)


def _sc_collect(y, dest3):
    n = dest3.shape[0] * SC_COLLECT_CHUNK
    chunks_per_worker = n // SC_COLLECT_CHUNK // SC_WORKERS
    mesh = plsc.VectorSubcoreMesh(core_axis_name="c", subcore_axis_name="s")

    @functools.partial(
        pl.kernel, mesh=mesh,
        out_type=jax.ShapeDtypeStruct((TOP_K, n, HALF_D), y.dtype),
        scratch_types=[pltpu.VMEM((TOP_K, SC_COLLECT_CHUNK), jnp.int32),
                       pltpu.VMEM((TOP_K, SC_COLLECT_CHUNK, HALF_D), y.dtype),
                       pltpu.SemaphoreType.DMA((TOP_K,)),
                       pltpu.SemaphoreType.DMA((TOP_K,))],
        name="sc_collect")
    def run(y_hbm, dest_hbm, yg_hbm, idx_v, rows_v, gather_sem, write_sem):
        first = _sc_worker_id() * chunks_per_worker

        @pl.loop(0, chunks_per_worker)
        def _(j):
            ch = first + j
            pltpu.sync_copy(dest_hbm.at[ch], idx_v)
            gathers = [pltpu.async_copy(y_hbm.at[idx_v.at[k]], rows_v.at[k], gather_sem.at[k])
                       for k in range(TOP_K)]
            writes = []
            for k in range(TOP_K):
                gathers[k].wait()
                writes.append(pltpu.async_copy(
                    rows_v.at[k], yg_hbm.at[k, pl.ds(ch * SC_COLLECT_CHUNK, SC_COLLECT_CHUNK)],
                    write_sem.at[k]))
            for w in writes:
                w.wait()

    return run(y, dest3)


def _sc_collect_sum(y, dest3, gates16, first_chunk, n_chunks):
    C = SC_COLLECT_CHUNK
    n = n_chunks * C
    chunks_per_worker = n_chunks // SC_WORKERS
    lanes = 16
    mesh = plsc.VectorSubcoreMesh(core_axis_name="c", subcore_axis_name="s")

    @functools.partial(
        pl.kernel, mesh=mesh,
        out_type=jax.ShapeDtypeStruct((n, HALF_D), y.dtype),
        scratch_types=[pltpu.VMEM((2, TOP_K, C), jnp.int32),
                       pltpu.VMEM((2, TOP_K, C, lanes), jnp.uint32),
                       pltpu.VMEM((2, TOP_K, C, HALF_D), y.dtype),
                       pltpu.VMEM((2, C, HALF_D), y.dtype),
                       pltpu.SemaphoreType.DMA((2, TOP_K)),
                       pltpu.SemaphoreType.DMA((2,))],
        compiler_params=pltpu.CompilerParams(needs_layout_passes=False),
        name="sc_collect_sum")
    def run(y_hbm, dest_hbm, gate_hbm, out_hbm, idx_v, gate_v, rows_v, sum_v, gather_sem, write_sem):
        local0 = _sc_worker_id() * chunks_per_worker

        def fetch(i):
            slot = i % 2
            ch = first_chunk + local0 + i
            pltpu.sync_copy(dest_hbm.at[ch], idx_v.at[slot])
            gathers = [pltpu.async_copy(y_hbm.at[idx_v.at[slot, k]], rows_v.at[slot, k],
                                        gather_sem.at[slot, k]) for k in range(TOP_K)]
            pltpu.sync_copy(gate_hbm.at[ch], gate_v.at[slot])
            return gathers

        pending = {0: fetch(0)}
        writes = {}
        for i in range(chunks_per_worker):
            slot = i % 2
            if i + 1 < chunks_per_worker:
                pending[i + 1] = fetch(i + 1)
            for cp in pending.pop(i):
                cp.wait()
            if i >= 2:
                writes.pop(i - 2).wait()

            @plsc.parallel_loop(0, C, unroll=2)
            def _(j):
                g = [plsc.bitcast(gate_v[slot, k, j, :], jnp.bfloat16) for k in range(TOP_K)]
                for c in range(HALF_D // lanes):
                    sl = pl.ds(c * lanes, lanes)
                    acc = None
                    for k in range(TOP_K):
                        term = plsc.bitcast(rows_v[slot, k, j, sl], jnp.bfloat16) * g[k]
                        acc = term if acc is None else acc + term
                    sum_v[slot, j, sl] = plsc.bitcast(acc, y.dtype)

            writes[i] = pltpu.async_copy(sum_v.at[slot], out_hbm.at[pl.ds((local0 + i) * C, C)],
                                         write_sem.at[slot])
        for w in writes.values():
            w.wait()

    return run(y, dest3, gates16)


def _combine_sum_kernel(x1_ref, ys_ref, o_ref):
    lo, hi = _unpack_halves(ys_ref[...])
    o_ref[:, :HALF_D] = x1_ref[:, :HALF_D] + lo
    o_ref[:, HALF_D:] = x1_ref[:, HALF_D:] + hi


def _combine_sum(acc, ysum, split):
    n = acc.shape[0]
    rows = PROJ_ROWS
    steps = ysum.shape[0] // rows
    first = split * steps
    return pl.pallas_call(
        _combine_sum_kernel,
        out_shape=jax.ShapeDtypeStruct((n, D_MODEL), jnp.float32),
        grid=(steps,),
        in_specs=[pl.BlockSpec((rows, D_MODEL), lambda i: (first + i, 0)),
                  pl.BlockSpec((rows, HALF_D), lambda i: (i, 0))],
        out_specs=pl.BlockSpec((rows, D_MODEL), lambda i: (first + i, 0)),
        input_output_aliases={0: 0},
        compiler_params=pltpu.CompilerParams(
            dimension_semantics=("arbitrary",), vmem_limit_bytes=VMEM_LIMIT),
        name="moe_combine",
    )(acc, ysum)


def _layer(x2, batch, seq, attn_norm_g, w_in, a_q_g, a_k_g, a_sinks, b_q_g, b_k_g, w_out,
           ffn_norm_g, w_router, b_router, w1, b1, w2, b2):
    n = x2.shape[0]
    slopes = _alibi_slopes()
    q_scale = HEAD_DIM ** -0.5 * LOG2E
    reps = MXU_DIM // HEAD_DIM
    gains = jnp.stack([jnp.tile(a_q_g, reps) * q_scale, jnp.tile(a_k_g, reps),
                       jnp.tile(b_q_g, reps) * q_scale, jnp.tile(b_k_g, reps)]).astype(jnp.float32)

    proj = _in_proj(x2, attn_norm_g.reshape(1, -1), w_in, gains, batch, seq)
    qa, ka, va = proj[:3]
    nb = len(B_DILS)
    qbs, kbs, vbs = proj[3:3 + nb], proj[3 + nb:3 + 2 * nb], proj[3 + 2 * nb:]

    bias_a = _bias_tables(slopes[:A_Q_HEADS], A_STACK_HEADS, A_HALF_WINDOW, 1, Q_TILE + 2 * A_HALF_WINDOW)
    sink_col = jnp.repeat(a_sinks.astype(jnp.float32) * LOG2E, Q_TILE).reshape(
        A_Q_HEADS // A_STACK_HEADS, A_STACK_HEADS * Q_TILE, 1)
    as_seqs = lambda a: a.reshape(batch, seq, a.shape[-1])
    out_a = _banded_attention(as_seqs(qa), as_seqs(ka), as_seqs(va), bias_a, half_w=A_HALF_WINDOW,
                              sink=sink_col, name="attn_a")[0].reshape(n, A_Q_W)

    outs_b, lses_b = [], []
    for bi, (window, dil) in enumerate(B_BRANCHES):
        half_w = window // (2 * dil)
        bias_b = _bias_tables(slopes[A_Q_HEADS:], B_STACK_HEADS, half_w, dil, Q_TILE + 2 * half_w)
        L = seq // dil
        to_seqs = lambda a: a.reshape(batch * dil, L, a.shape[-1])
        o, lse = _banded_attention(to_seqs(qbs[bi]), to_seqs(kbs[bi]), to_seqs(vbs[bi]), bias_b,
                                   half_w=half_w, want_lse=True, name=f"attn_b_d{dil}")
        if dil == 1:
            outs_b.append(o.reshape(n, B_W))
            lses_b.append(lse.reshape(n, LANES))
        else:
            outs_b.append(o.reshape(batch, dil, L, B_W))
            lses_b.append(lse.reshape(batch, dil, L, LANES))

    wr_t = w_router.T.astype(jnp.float32)
    wr_hi = wr_t.astype(jnp.bfloat16)
    wr_lo = (wr_t - wr_hi.astype(jnp.float32)).astype(jnp.bfloat16)
    wr = jnp.concatenate([wr_hi, wr_lo], axis=0)
    br = jnp.broadcast_to(b_router.astype(jnp.float32)[:, None], (N_EXPERTS, LANES))
    x1, hf, topi, gates, ranks, counts = _out_proj_router(
        out_a, outs_b, lses_b, x2, w_out, ffn_norm_g.reshape(1, -1), wr, br, seq)

    g = MOE_ROWS
    nk = n * TOP_K
    step_rows = MOE_STEP_BLOCKS * g
    n_rows = -(-(nk + N_EXPERTS * g) // step_rows) * step_rows
    cnt = counts[:, 0]
    pcnt = (cnt + g - 1) // g * g
    pends = jnp.cumsum(pcnt)
    pstarts = pends - pcnt
    experts = jnp.arange(N_EXPERTS, dtype=jnp.int32)
    start_of = jnp.sum(jnp.where(topi[:, :, None] == experts, pstarts, 0), axis=-1)
    dest = (start_of + ranks).astype(jnp.int32)
    plan = _moe_plan(pends, n_rows // g)
    dest3 = dest.reshape(TOP_K, n // SC_CHUNK, SC_CHUNK).transpose(1, 0, 2)

    xb = _sc_dispatch(hf, dest3, n_rows)
    y = _moe_experts(plan, xb, w1, b1[:, None, :], w2, b2[:, None, :])
    smallest_range = max(PROJ_ROWS, SC_COLLECT_CHUNK * SC_WORKERS)
    n_splits = max(1, min(COMBINE_SPLITS, n // smallest_range))
    per_split = n // n_splits
    chunked = lambda a: a.reshape(TOP_K, n // SC_COLLECT_CHUNK, SC_COLLECT_CHUNK).transpose(1, 0, 2)
    dest_c = chunked(dest)
    gate_bits = lax.bitcast_convert_type(gates.astype(jnp.bfloat16), jnp.uint16).astype(jnp.uint32)
    gate_words = gate_bits | (gate_bits << 16)
    gates_c = jnp.broadcast_to(chunked(gate_words)[..., None], dest_c.shape + (16,))
    chunks_per_split = per_split // SC_COLLECT_CHUNK
    out = x1
    for s in range(n_splits):
        ysum = _sc_collect_sum(y, dest_c, gates_c, s * chunks_per_split, chunks_per_split)
        out = _combine_sum(out, ysum, s)
    return out


def kernel(x, attn_norm_g, w_in, a_q_norm_g, a_k_norm_g, a_sinks, b_q_norm_g, b_k_norm_g, w_out,
           ffn_norm_g, w_router, b_router, w1, b1, w2, b2):
    batch, seq, d = x.shape
    x2 = x.reshape(batch * seq, d)
    for i in range(attn_norm_g.shape[0]):
        x2 = _layer(x2, batch, seq, attn_norm_g[i], w_in[i], a_q_norm_g[i], a_k_norm_g[i],
                    a_sinks[i], b_q_norm_g[i], b_k_norm_g[i], w_out[i], ffn_norm_g[i],
                    w_router[i], b_router[i], w1[i], b1[i], w2[i], b2[i])
    return x2.reshape(batch, seq, d)
```

```python
import functools

import jax
import jax.numpy as jnp
import numpy as np
from jax import lax
from jax.experimental import pallas as pl
from jax.experimental.pallas import tpu as pltpu
from jax.experimental.pallas import tpu_sc as plsc

D_MODEL = 1024
HALF_D = D_MODEL // 2
HEAD_DIM = 64
LANES = 128
MXU_DIM = 256
A_Q_HEADS = 8
A_KV_HEADS = 2
B_HEADS = 8
A_HALF_WINDOW = 128
B_BRANCHES = ((128, 1), (512, 4), (2048, 16))
B_DILS = tuple(d for _, d in B_BRANCHES)
RESIDUE_STRIDE = 4
N_ALIBI_HEADS = 16
A_Q_W = A_Q_HEADS * HEAD_DIM
A_KV_W = A_KV_HEADS * HEAD_DIM
B_W = B_HEADS * HEAD_DIM
N_EXPERTS = 32
TOP_K = 4
D_FF = 1024
SWIGLU_ALPHA = 1.702
SWIGLU_LIMIT = 7.0
NORM_EPS = 1e-5
MASK_VALUE = -1e30
LOG2E = 1.4426950408889634

Q_TILE = 128
ATTN_STEP_ROWS = 2048
A_STACK_HEADS = 4
B_STACK_HEADS = 2
PROJ_ROWS = 1024
IN_PROJ_ROWS = 1024
MOE_ROWS = 512
FF_CHUNK = 512
MOE_STEP_BLOCKS = 2
MOE_WEIGHT_BUFS = 3
VMEM_LIMIT = 48 * 1024 * 1024
LARGE_VMEM_LIMIT = 58 * 1024 * 1024
SC_CORES = 2
SC_SUBCORES = 16
SC_WORKERS = SC_CORES * SC_SUBCORES
SC_CHUNK = 64
SC_COLLECT_CHUNK = 32
COMBINE_SPLITS = 8


def _pack_halves(v):
    lo = v[:, :HALF_D].astype(jnp.bfloat16).astype(jnp.float32)
    hi = v[:, HALF_D:].astype(jnp.bfloat16).astype(jnp.float32)
    return (pltpu.bitcast(lo, jnp.uint32) >> 16) | pltpu.bitcast(hi, jnp.uint32)


def _unpack_halves(w):
    lo = pltpu.bitcast(w << 16, jnp.float32)
    hi = pltpu.bitcast(w & jnp.uint32(0xFFFF0000), jnp.float32)
    return lo, hi


def _alibi_slopes():
    return np.exp2(-8.0 * np.arange(1, N_ALIBI_HEADS + 1, dtype=np.float32) / N_ALIBI_HEADS).astype(np.float32)


def _bias_tables(head_slopes, heads_per_group, half_w, dist_scale, tk):
    i = np.arange(Q_TILE)[:, None]
    j = np.arange(tk)[None, :]
    tabs = []
    for shift in (0, half_w, tk - Q_TILE):
        dist = np.abs(j - shift - i)
        valid = dist <= half_w
        per_head = []
        for sl in head_slopes:
            b = (-np.float64(sl) * LOG2E * (dist * dist_scale)).astype(np.float32)
            per_head.append(np.where(valid, b, np.float32(MASK_VALUE)).astype(np.float32))
        t = np.stack(per_head).reshape(-1, heads_per_group * Q_TILE, tk)
        tabs.append(t)
    return jnp.asarray(np.stack(tabs))


def _in_proj_kernel(x_ref, g_ref, w_hbm, gains_ref, qa_ref, ka_ref, va_ref, *rest):
    b_refs, (scr_ref, scr2_ref, w_ref, stage_ref, sem) = rest[:-5], rest[-5:]

    @pl.when(pl.program_id(0) == 0)
    def _():
        width = stage_ref.shape[1]
        for c0 in range(0, w_ref.shape[1], width):
            cp = pltpu.make_async_copy(w_hbm.at[:, c0:c0 + width], stage_ref, sem)
            cp.start()
            cp.wait()
            w_ref[:, c0:c0 + width] = stage_ref[...].astype(jnp.bfloat16)

    x = x_ref[...]
    xn = x * lax.rsqrt(jnp.mean(x * x, axis=-1, keepdims=True) + NORM_EPS) * g_ref[...]
    xn = xn.astype(jnp.bfloat16)
    r = lax.broadcasted_iota(jnp.int32, (MXU_DIM, MXU_DIM), 0) // HEAD_DIM
    c = lax.broadcasted_iota(jnp.int32, (MXU_DIM, MXU_DIM), 1) // HEAD_DIM
    blockdiag = jnp.where(r == c, 1.0, 0.0).astype(jnp.bfloat16)

    def head_rms(sec, gain_row):
        width = sec.shape[1]
        parts = []
        step = min(width, MXU_DIM)
        for j in range(width // step):
            p = sec[:, j * step:(j + 1) * step]
            ss = jnp.dot((p * p).astype(jnp.bfloat16), blockdiag[:step, :step],
                         preferred_element_type=jnp.float32)
            parts.append(p * lax.rsqrt(ss * (1.0 / HEAD_DIM) + NORM_EPS)
                         * gains_ref[gain_row:gain_row + 1, :step])
        return parts

    def project(col0, width, gain_row):
        sec = jnp.dot(xn, w_ref[:, col0:col0 + width], preferred_element_type=jnp.float32)
        return [sec] if gain_row is None else head_rms(sec, gain_row)

    def store(out_ref, parts):
        w = parts[0].shape[1]
        for j, p in enumerate(parts):
            out_ref[:, j * w:(j + 1) * w] = p.astype(out_ref.dtype)

    def per_kv_head(p):
        lane = lax.broadcasted_iota(jnp.int32, p.shape, 1)
        swapped = pltpu.roll(p, HEAD_DIM, axis=1)
        low = lane < HEAD_DIM
        return [jnp.where(low, p, swapped), jnp.where(low, swapped, p)]

    store(qa_ref, project(0, A_Q_W, 0))
    kva = project(A_Q_W, 2 * A_KV_W, None)[0]
    store(ka_ref, per_kv_head(head_rms(kva[:, :A_KV_W], 1)[0]))
    store(va_ref, per_kv_head(kva[:, A_KV_W:]))

    rows = x_ref.shape[0]
    col0 = A_Q_W + 2 * A_KV_W
    for t, gain_row in enumerate((2, 3, None)):
        parts = project(col0 + t * B_W, B_W, gain_row)
        sec = jnp.concatenate(parts, axis=-1) if len(parts) > 1 else parts[0]
        for j in range(B_W // LANES):
            scr_ref[j] = sec[:, j * LANES:(j + 1) * LANES]
        n_lane_chunks = B_W // LANES
        prev_dil = 1
        for bi, dil in enumerate(B_DILS):
            out_ref = b_refs[t * len(B_DILS) + bi]
            if dil == 1:
                out_ref[...] = sec.astype(out_ref.dtype)
                continue
            assert dil == prev_dil * RESIDUE_STRIDE
            last = dil == B_DILS[-1]
            for res in range(dil):
                r_prev, r_sub = res % prev_dil, res // prev_dil
                for j in range(n_lane_chunks):
                    if prev_dil == 1:
                        v = scr_ref[j, pl.ds(r_sub, rows // dil, stride=RESIDUE_STRIDE), :]
                    else:
                        v = scr2_ref[r_prev * n_lane_chunks + j,
                                     pl.ds(r_sub, rows // dil, stride=RESIDUE_STRIDE), :]
                    out_ref[0, res, :, j * LANES:(j + 1) * LANES] = v.astype(out_ref.dtype)
                    if not last:
                        scr2_ref[res * n_lane_chunks + j] = v
            prev_dil = dil


def _in_proj(x2, g, w_in, gains, batch, seq):
    n = x2.shape[0]
    rows = IN_PROJ_ROWS
    steps = seq // rows
    a_widths = (A_Q_W, 2 * A_KV_W, 2 * A_KV_W)
    out_shape = [jax.ShapeDtypeStruct((n, w), jnp.bfloat16) for w in a_widths]
    out_specs = [pl.BlockSpec((rows, w), lambda i: (i, 0)) for w in a_widths]
    for _ in range(3):
        for dil in B_DILS:
            if dil == 1:
                out_shape.append(jax.ShapeDtypeStruct((n, B_W), jnp.bfloat16))
                out_specs.append(pl.BlockSpec((rows, B_W), lambda i: (i, 0)))
            else:
                out_shape.append(jax.ShapeDtypeStruct((batch, dil, seq // dil, B_W), jnp.bfloat16))
                out_specs.append(pl.BlockSpec((1, dil, rows // dil, B_W),
                                              lambda i: (i // steps, 0, i % steps, 0)))
    return pl.pallas_call(
        _in_proj_kernel,
        out_shape=out_shape,
        grid=(n // rows,),
        in_specs=[
            pl.BlockSpec((rows, D_MODEL), lambda i: (i, 0)),
            pl.BlockSpec((1, D_MODEL), lambda i: (0, 0)),
            pl.BlockSpec(memory_space=pl.ANY),
            pl.BlockSpec(gains.shape, lambda i: (0, 0)),
        ],
        out_specs=out_specs,
        scratch_shapes=[pltpu.VMEM((B_W // LANES, rows, LANES), jnp.float32),
                        pltpu.VMEM((RESIDUE_STRIDE * B_W // LANES, rows // RESIDUE_STRIDE, LANES),
                                   jnp.float32),
                        pltpu.VMEM(w_in.shape, jnp.bfloat16),
                        pltpu.VMEM((w_in.shape[0], w_in.shape[1] // 3), jnp.float32),
                        pltpu.SemaphoreType.DMA],
        compiler_params=pltpu.CompilerParams(
            dimension_semantics=("arbitrary",), vmem_limit_bytes=LARGE_VMEM_LIMIT),
        name="in_proj",
    )(x2, g, w_in, gains)


def _attn_kernel(*refs, n_chunks, kv_chunks, heads_per_stack, tk, half_w, seq_len, rows, has_sink,
                 want_lse):
    it = iter(refs)
    q_ref, k_ref, v_ref, bias_ref = next(it), next(it), next(it), next(it)
    sink_ref = next(it) if has_sink else None
    o_ref = next(it)
    lse_ref = next(it) if want_lse else None

    n_tiles = seq_len // Q_TILE
    tiles_per_step = rows // Q_TILE
    chunks_per_group = n_chunks // kv_chunks
    assert (2 * chunks_per_group) % heads_per_stack == 0
    step = pl.program_id(1)
    lane = lax.broadcasted_iota(jnp.int32, (Q_TILE, LANES), 1)
    low_half = lane < HEAD_DIM
    ones = jnp.ones((tk, LANES), jnp.bfloat16)

    for sq, t in [(a, b) for a in range(q_ref.shape[0]) for b in range(tiles_per_step)]:
        tile = step * tiles_per_step + t
        q0 = tile * Q_TILE
        kv_rows = k_ref.shape[1]
        kv_row0 = jnp.clip(step * rows - half_w, 0, seq_len - kv_rows)
        start = pl.multiple_of(jnp.clip(q0 - half_w, 0, seq_len - tk) - kv_row0, HEAD_DIM)
        variant = jnp.where(tile == 0, 0, jnp.where(tile == n_tiles - 1, 2, 1))
        r0 = t * Q_TILE
        lse_tile = jnp.zeros((Q_TILE, LANES), jnp.float32)
        head_o, head_lse = {}, {}
        for g in range(2 * n_chunks // heads_per_stack):
            heads = range(g * heads_per_stack, (g + 1) * heads_per_stack)
            kv = (heads[0] // 2) // chunks_per_group
            kc = k_ref[sq, pl.ds(start, tk), kv * LANES:(kv + 1) * LANES]
            vc = v_ref[sq, pl.ds(start, tk), kv * LANES:(kv + 1) * LANES]
            v_aug = jnp.concatenate([vc, ones], axis=1)
            q_parts = []
            for h in heads:
                c = h // 2
                q2 = q_ref[sq, r0:r0 + Q_TILE, c * LANES:(c + 1) * LANES]
                keep = low_half if h % 2 == 0 else ~low_half
                q_parts.append(jnp.where(keep, q2, jnp.zeros_like(q2)))
            qs = q_parts[0] if len(q_parts) == 1 else jnp.concatenate(q_parts, axis=0)
            s = lax.dot_general(qs, kc, (((1,), (1,)), ((), ())),
                                preferred_element_type=jnp.float32)
            s = s + bias_ref[variant, g]
            m = jnp.max(s, axis=-1, keepdims=True)
            if has_sink:
                m = jnp.maximum(m, sink_ref[g])
            p = jnp.exp2(s - m)
            ov = jnp.dot(p.astype(jnp.bfloat16), v_aug, preferred_element_type=jnp.float32)
            o, l = ov[:, :LANES], ov[:, LANES:]
            if has_sink:
                l = l + jnp.exp2(sink_ref[g] - m)
            o = o * (1.0 / l)
            if want_lse:
                lse = m + jnp.log(l) * LOG2E
            for idx, h in enumerate(heads):
                head_o[h] = o[idx * Q_TILE:(idx + 1) * Q_TILE]
                if want_lse:
                    head_lse[h] = lse[idx * Q_TILE:(idx + 1) * Q_TILE]
                if h % 2 == 1:
                    c = h // 2
                    o2 = jnp.where(low_half, head_o.pop(h - 1), head_o.pop(h))
                    o_ref[sq, r0:r0 + Q_TILE, c * LANES:(c + 1) * LANES] = o2.astype(o_ref.dtype)
                    if want_lse:
                        lse_tile = jnp.where(lane == h - 1, head_lse.pop(h - 1),
                                             jnp.where(lane == h, head_lse.pop(h), lse_tile))
        if want_lse:
            lse_ref[sq, r0:r0 + Q_TILE, :] = lse_tile


def _banded_attention(q, k, v, bias, *, half_w, sink=None, want_lse=False, name):
    n_seq, L, qw = q.shape
    kw = k.shape[2]
    tk = Q_TILE + 2 * half_w
    rows = min(ATTN_STEP_ROWS, L)
    seqs = ATTN_STEP_ROWS // rows
    kv_rows = min(L, rows + 2 * half_w)

    def kv_index(s, i):
        row0 = pl.multiple_of(jnp.clip(i * rows - half_w, 0, L - kv_rows), HEAD_DIM)
        return s * seqs, row0, 0

    args = [q, k, v, bias]
    in_specs = [
        pl.BlockSpec((seqs, rows, qw), lambda s, i: (s, i, 0)),
        pl.BlockSpec((pl.Element(seqs), pl.Element(kv_rows), pl.Element(kw)), kv_index),
        pl.BlockSpec((pl.Element(seqs), pl.Element(kv_rows), pl.Element(kw)), kv_index),
        pl.BlockSpec(bias.shape, lambda s, i: (0, 0, 0, 0)),
    ]
    if sink is not None:
        args.append(sink)
        in_specs.append(pl.BlockSpec(sink.shape, lambda s, i: (0, 0, 0)))
    out_shape = [jax.ShapeDtypeStruct((n_seq, L, qw), jnp.bfloat16)]
    out_specs = [pl.BlockSpec((seqs, rows, qw), lambda s, i: (s, i, 0))]
    if want_lse:
        out_shape.append(jax.ShapeDtypeStruct((n_seq, L, LANES), jnp.float32))
        out_specs.append(pl.BlockSpec((seqs, rows, LANES), lambda s, i: (s, i, 0)))

    kern = functools.partial(
        _attn_kernel, n_chunks=qw // LANES, kv_chunks=kw // LANES,
        heads_per_stack=bias.shape[2] // Q_TILE, tk=tk, half_w=half_w, seq_len=L,
        rows=rows, has_sink=sink is not None, want_lse=want_lse)
    return pl.pallas_call(
        kern,
        out_shape=out_shape,
        grid=(n_seq // seqs, L // rows),
        in_specs=in_specs,
        out_specs=out_specs,
        compiler_params=pltpu.CompilerParams(
            dimension_semantics=("arbitrary", "arbitrary"), vmem_limit_bytes=VMEM_LIMIT),
        name=name,
    )(*args)


def _out_proj_router_kernel(*refs):
    nb = len(B_DILS)
    oa_ref = refs[0]
    o_refs = refs[1:1 + nb]
    lse_refs = refs[1 + nb:1 + 2 * nb]
    (x_ref, wo_ref, g_ref, wr_ref, br_ref,
     x1_ref, hf_ref, topi_ref, gate_ref, rank_ref, cnt_ref,
     tri_ref, carry_ref, so_ref, sl_ref, wo_bf_ref) = refs[1 + 2 * nb:]
    i = pl.program_id(0)
    rows = x_ref.shape[0]

    @pl.when(i == 0)
    def _():
        a = lax.broadcasted_iota(jnp.int32, (rows, rows), 0)
        b = lax.broadcasted_iota(jnp.int32, (rows, rows), 1)
        tri_ref[...] = jnp.where(a <= b, 1.0, 0.0).astype(jnp.bfloat16)
        carry_ref[...] = jnp.zeros_like(carry_ref)
        wo_bf_ref[...] = wo_ref[...].astype(jnp.bfloat16)

    outs, lses = [], []
    for bi, dil in enumerate(B_DILS):
        if dil == 1:
            outs.append(o_refs[bi][...].astype(jnp.float32))
            lses.append(lse_refs[bi][...])
        else:
            for res in range(dil):
                for j in range(B_W // LANES):
                    so_ref[bi, j, pl.ds(res, rows // dil, stride=dil), :] = (
                        o_refs[bi][0, res, :, j * LANES:(j + 1) * LANES].astype(jnp.float32))
                sl_ref[bi, pl.ds(res, rows // dil, stride=dil), :] = lse_refs[bi][0, res]
            outs.append(jnp.concatenate([so_ref[bi, j] for j in range(B_W // LANES)], axis=-1))
            lses.append(sl_ref[bi])

    mx = functools.reduce(jnp.maximum, lses)
    es = [jnp.exp2(l - mx) for l in lses]
    inv = 1.0 / functools.reduce(lambda a, b: a + b, es)
    eh = lax.broadcasted_iota(jnp.int32, (LANES, B_W), 0)
    ej = lax.broadcasted_iota(jnp.int32, (LANES, B_W), 1) // HEAD_DIM
    expand = jnp.where(eh == ej, 1.0, 0.0).astype(jnp.bfloat16)
    ob = jnp.zeros((rows, B_W), jnp.float32)
    for e, o in zip(es, outs):
        w = e * inv
        wide = jnp.dot(w.astype(jnp.bfloat16), expand, preferred_element_type=jnp.float32)
        ob = ob + wide * o

    attn = jnp.concatenate([oa_ref[...], ob.astype(jnp.bfloat16)], axis=-1)
    x1 = x_ref[...] + jnp.dot(attn, wo_bf_ref[...], preferred_element_type=jnp.float32)
    x1_ref[...] = x1
    hf = x1 * lax.rsqrt(jnp.mean(x1 * x1, axis=-1, keepdims=True) + NORM_EPS) * g_ref[...]
    hf_hi = hf.astype(jnp.bfloat16)
    hf_ref[...] = _pack_halves(hf)
    hf_lo = (hf - hf_hi.astype(jnp.float32)).astype(jnp.bfloat16)

    nt = (((1,), (1,)), ((), ()))
    lg_hi = lax.dot_general(wr_ref[...], hf_hi, nt, preferred_element_type=jnp.float32)
    lg_lo = lax.dot_general(wr_ref[0:N_EXPERTS, :], hf_lo, nt, preferred_element_type=jnp.float32)
    logits = lg_hi[0:N_EXPERTS] + lg_hi[N_EXPERTS:] + lg_lo + br_ref[:, 0:1]


    eidx = lax.broadcasted_iota(jnp.int32, (N_EXPERTS, rows), 0)
    work = logits
    vals, sels = [], []
    for k in range(TOP_K):
        mk = jnp.max(work, axis=0, keepdims=True)
        ik = jnp.min(jnp.where(work == mk, eidx, N_EXPERTS), axis=0, keepdims=True)
        sel = eidx == ik
        work = jnp.where(sel, -jnp.inf, work)
        vals.append(mk)
        sels.append(sel)
        topi_ref[k:k + 1, :] = ik
    exps = [jnp.exp(vk - vals[0]) for vk in vals]
    denom = exps[0] + exps[1] + exps[2] + exps[3]
    ginv = 1.0 / denom
    for k in range(TOP_K):
        gate_ref[k:k + 1, :] = exps[k] * ginv

    onehot = jnp.zeros((N_EXPERTS, rows), jnp.float32)
    for sel in sels:
        onehot = onehot + jnp.where(sel, 1.0, 0.0)
    incl = jnp.dot(onehot.astype(jnp.bfloat16), tri_ref[...], preferred_element_type=jnp.float32)
    before = incl - onehot + carry_ref[:, 0:1]
    for k in range(TOP_K):
        rk = jnp.sum(jnp.where(sels[k], before, 0.0), axis=0, keepdims=True)
        rank_ref[k:k + 1, :] = rk.astype(jnp.int32)
    carry = carry_ref[...] + jnp.sum(onehot, axis=1, keepdims=True)
    carry_ref[...] = carry
    cnt_ref[...] = carry.astype(jnp.int32)


def _out_proj_router(oa, outs_b, lses_b, x2, wo_bf, g, wr, br, seq):
    n = x2.shape[0]
    rows = PROJ_ROWS
    steps = seq // rows
    row_spec = lambda w: pl.BlockSpec((rows, w), lambda i: (i, 0))
    full = lambda a: pl.BlockSpec(a.shape, lambda i: (0,) * a.ndim)
    col_spec = pl.BlockSpec((TOP_K, rows), lambda i: (0, i))

    def branch_spec(dil, w):
        if dil == 1:
            return row_spec(w)
        return pl.BlockSpec((1, dil, rows // dil, w), lambda i: (i // steps, 0, i % steps, 0))

    in_specs = ([row_spec(A_Q_W)]
                + [branch_spec(d, B_W) for d in B_DILS]
                + [branch_spec(d, LANES) for d in B_DILS]
                + [row_spec(D_MODEL), full(wo_bf), full(g), full(wr), full(br)])
    return pl.pallas_call(
        _out_proj_router_kernel,
        out_shape=[
            jax.ShapeDtypeStruct((n, D_MODEL), jnp.float32),
            jax.ShapeDtypeStruct((n, HALF_D), jnp.uint32),
            jax.ShapeDtypeStruct((TOP_K, n), jnp.int32),
            jax.ShapeDtypeStruct((TOP_K, n), jnp.float32),
            jax.ShapeDtypeStruct((TOP_K, n), jnp.int32),
            jax.ShapeDtypeStruct((N_EXPERTS, LANES), jnp.int32),
        ],
        grid=(n // rows,),
        in_specs=in_specs,
        out_specs=[row_spec(D_MODEL), row_spec(HALF_D), col_spec, col_spec, col_spec,
                   pl.BlockSpec((N_EXPERTS, LANES), lambda i: (0, 0))],
        scratch_shapes=[pltpu.VMEM((rows, rows), jnp.bfloat16),
                        pltpu.VMEM((N_EXPERTS, LANES), jnp.float32),
                        pltpu.VMEM((len(B_DILS), B_W // LANES, rows, LANES), jnp.float32),
                        pltpu.VMEM((len(B_DILS), rows, LANES), jnp.float32),
                        pltpu.VMEM(wo_bf.shape, jnp.bfloat16)],
        compiler_params=pltpu.CompilerParams(
            dimension_semantics=("arbitrary",), vmem_limit_bytes=LARGE_VMEM_LIMIT),
        name="out_proj_router",
    )(oa, *outs_b, *lses_b, x2, wo_bf, g, wr, br)


def _mxu_dot(a_bf, w_f32):
    return lax.dot_general(a_bf, w_f32, (((1,), (0,)), ((), ())), preferred_element_type=jnp.float32)


def _moe_kernel(blk_exp_ref, first_ref, slot_ref, next_exp_ref, n_used_ref,
                x_ref, w1_hbm, b1_ref, w2_hbm, b2_ref, y_ref, w1_buf, w2_buf, sem):
    step = pl.program_id(0)

    def weight_copies(expert, slot):
        return (pltpu.make_async_copy(w1_hbm.at[expert], w1_buf.at[slot], sem.at[slot, 0]),
                pltpu.make_async_copy(w2_hbm.at[expert], w2_buf.at[slot], sem.at[slot, 1]))

    @pl.when(step * MOE_STEP_BLOCKS < n_used_ref[0])
    def _():
        @pl.when(step == 0)
        def _():
            for cp in weight_copies(blk_exp_ref[0], slot_ref[0]):
                cp.start()

        for j in range(MOE_STEP_BLOCKS):
            i = step * MOE_STEP_BLOCKS + j

            @pl.when(first_ref[i] == 1)
            def _():
                slot = slot_ref[i]
                for cp in weight_copies(blk_exp_ref[i], slot):
                    cp.wait()

                @pl.when(next_exp_ref[i] >= 0)
                def _():
                    for cp in weight_copies(next_exp_ref[i], (slot + 1) % MOE_WEIGHT_BUFS):
                        cp.start()

        for j in range(MOE_STEP_BLOCKS):
            i = step * MOE_STEP_BLOCKS + j
            slot = slot_ref[i]
            expert = blk_exp_ref[i]
            rs = slice(j * MOE_ROWS, (j + 1) * MOE_ROWS)
            x = jnp.concatenate(_unpack_halves(x_ref[rs, :]), axis=-1).astype(jnp.bfloat16)
            acc = jnp.zeros((MOE_ROWS, D_MODEL), jnp.float32)
            for c in range(D_FF // FF_CHUNK):
                lo = c * FF_CHUNK
                glu = _mxu_dot(x, w1_buf[slot, :, lo:lo + FF_CHUNK]) + b1_ref[expert, :, lo:lo + FF_CHUNK]
                lin = (_mxu_dot(x, w1_buf[slot, :, D_FF + lo:D_FF + lo + FF_CHUNK])
                       + b1_ref[expert, :, D_FF + lo:D_FF + lo + FF_CHUNK])
                glu = jnp.minimum(glu, SWIGLU_LIMIT)
                lin = jnp.clip(lin, -SWIGLU_LIMIT, SWIGLU_LIMIT)
                act = glu * (1.0 / (1.0 + jnp.exp(-SWIGLU_ALPHA * glu))) * (lin + 1.0)
                acc = acc + _mxu_dot(act.astype(jnp.bfloat16), w2_buf[slot, lo:lo + FF_CHUNK, :])
            y_ref[rs, :] = _pack_halves(acc + b2_ref[expert])


def _moe_plan(pends, n_blk):
    g = MOE_ROWS
    blk_row0 = jnp.arange(n_blk, dtype=jnp.int32) * g
    blk_exp = jnp.minimum(jnp.sum(pends[None, :] <= blk_row0[:, None], axis=-1),
                          N_EXPERTS - 1).astype(jnp.int32)
    n_used = (pends[-1] // g).astype(jnp.int32)
    used = blk_row0 < pends[-1]
    prev_exp = jnp.concatenate([jnp.full((1,), -1, jnp.int32), blk_exp[:-1]])
    first = (used & (blk_exp != prev_exp)).astype(jnp.int32)
    slot = ((jnp.cumsum(first) - 1) % MOE_WEIGHT_BUFS).astype(jnp.int32)
    pstarts = jnp.concatenate([jnp.zeros((1,), pends.dtype), pends[:-1]])
    nonempty = pends > pstarts
    experts = jnp.arange(N_EXPERTS, dtype=jnp.int32)
    later = nonempty[None, :] & (experts[None, :] > experts[:, None])
    next_nonempty = jnp.min(jnp.where(later, experts[None, :], N_EXPERTS), axis=-1)
    next_nonempty = jnp.where(next_nonempty == N_EXPERTS, -1, next_nonempty).astype(jnp.int32)
    next_exp = jnp.sum(jnp.where(blk_exp[:, None] == experts[None, :], next_nonempty[None, :], 0),
                       axis=-1).astype(jnp.int32)
    return blk_exp, first, slot, next_exp, n_used.reshape(1)


def _moe_experts(plan, xb, w1, b1, w2, b2):
    n_rows = xb.shape[0]
    n_blk = n_rows // MOE_ROWS

    step_rows = MOE_STEP_BLOCKS * MOE_ROWS

    def blk(i, *p):
        return jnp.minimum(i, (p[-1][0] - 1) // MOE_STEP_BLOCKS)

    grid_spec = pltpu.PrefetchScalarGridSpec(
        num_scalar_prefetch=len(plan),
        grid=(n_blk // MOE_STEP_BLOCKS,),
        in_specs=[
            pl.BlockSpec((step_rows, HALF_D), lambda i, *p: (blk(i, *p), 0)),
            pl.BlockSpec(memory_space=pl.ANY),
            pl.BlockSpec(b1.shape, lambda i, *p: (0, 0, 0)),
            pl.BlockSpec(memory_space=pl.ANY),
            pl.BlockSpec(b2.shape, lambda i, *p: (0, 0, 0)),
        ],
        out_specs=pl.BlockSpec((step_rows, HALF_D), lambda i, *p: (blk(i, *p), 0)),
        scratch_shapes=[pltpu.VMEM((MOE_WEIGHT_BUFS, D_MODEL, 2 * D_FF), jnp.float32),
                        pltpu.VMEM((MOE_WEIGHT_BUFS, D_FF, D_MODEL), jnp.float32),
                        pltpu.SemaphoreType.DMA((MOE_WEIGHT_BUFS, 2))],
    )
    return pl.pallas_call(
        _moe_kernel,
        out_shape=jax.ShapeDtypeStruct((n_rows, HALF_D), jnp.uint32),
        grid_spec=grid_spec,
        compiler_params=pltpu.CompilerParams(
            dimension_semantics=("arbitrary",), vmem_limit_bytes=LARGE_VMEM_LIMIT),
        name="moe_experts",
    )(*plan, xb, w1, b1, w2, b2)


def _combine_kernel(x1_ref, yg_ref, gate_ref, o_ref):
    acc_lo = x1_ref[:, :HALF_D]
    acc_hi = x1_ref[:, HALF_D:]
    rows = x1_ref.shape[0]
    gates = jnp.concatenate([gate_ref[...], jnp.zeros((8 - TOP_K, rows), jnp.float32)], axis=0).T
    for k in range(TOP_K):
        lo, hi = _unpack_halves(yg_ref[k])
        gk = gates[:, k:k + 1]
        acc_lo = acc_lo + lo * gk
        acc_hi = acc_hi + hi * gk
    o_ref[:, :HALF_D] = acc_lo
    o_ref[:, HALF_D:] = acc_hi


def _combine(acc, yg, gates_nk, split):
    n = acc.shape[0]
    rows = PROJ_ROWS
    steps = yg.shape[1] // rows
    first = split * steps
    return pl.pallas_call(
        _combine_kernel,
        out_shape=jax.ShapeDtypeStruct((n, D_MODEL), jnp.float32),
        grid=(steps,),
        in_specs=[pl.BlockSpec((rows, D_MODEL), lambda i: (first + i, 0)),
                  pl.BlockSpec((TOP_K, rows, HALF_D), lambda i: (0, i, 0)),
                  pl.BlockSpec((TOP_K, rows), lambda i: (0, first + i))],
        out_specs=pl.BlockSpec((rows, D_MODEL), lambda i: (first + i, 0)),
        input_output_aliases={0: 0},
        compiler_params=pltpu.CompilerParams(
            dimension_semantics=("arbitrary",), vmem_limit_bytes=VMEM_LIMIT),
        name="moe_combine",
    )(acc, yg, gates_nk)


def _sc_worker_id():
    return lax.axis_index("s") * SC_CORES + lax.axis_index("c")


def _sc_dispatch(hf, dest3, n_rows):
    n = hf.shape[0]
    chunks_per_worker = n // SC_CHUNK // SC_WORKERS
    mesh = plsc.VectorSubcoreMesh(core_axis_name="c", subcore_axis_name="s")

    @functools.partial(
        pl.kernel, mesh=mesh,
        out_type=jax.ShapeDtypeStruct((n_rows, HALF_D), hf.dtype),
        scratch_types=[pltpu.VMEM((2, TOP_K, SC_CHUNK), jnp.int32),
                       pltpu.VMEM((2, SC_CHUNK, HALF_D), hf.dtype),
                       pltpu.SemaphoreType.DMA((2,)),
                       pltpu.SemaphoreType.DMA((2,))],
        name="sc_dispatch")
    def run(hf_hbm, dest_hbm, xb_hbm, idx_v, rows_v, load_sem, scatter_sem):
        first = _sc_worker_id() * chunks_per_worker

        def load(j):
            slot = j % 2
            pltpu.sync_copy(dest_hbm.at[first + j], idx_v.at[slot])
            return pltpu.async_copy(hf_hbm.at[pl.ds((first + j) * SC_CHUNK, SC_CHUNK)],
                                    rows_v.at[slot], load_sem.at[slot])

        loads = {0: load(0)}
        scatters = {}
        for j in range(chunks_per_worker):
            slot = j % 2
            loads.pop(j).wait()
            scatters[j] = [pltpu.async_copy(rows_v.at[slot], xb_hbm.at[idx_v.at[slot, k]],
                                            scatter_sem.at[slot]) for k in range(TOP_K)]
            if j >= 1:
                for cp in scatters.pop(j - 1):
                    cp.wait()
            if j + 1 < chunks_per_worker:
                loads[j + 1] = load(j + 1)
        for cp in scatters.pop(chunks_per_worker - 1):
            cp.wait()

    return run(hf, dest3)


def _sc_collect(y, dest3):
    n = dest3.shape[0] * SC_COLLECT_CHUNK
    chunks_per_worker = n // SC_COLLECT_CHUNK // SC_WORKERS
    mesh = plsc.VectorSubcoreMesh(core_axis_name="c", subcore_axis_name="s")

    @functools.partial(
        pl.kernel, mesh=mesh,
        out_type=jax.ShapeDtypeStruct((TOP_K, n, HALF_D), y.dtype),
        scratch_types=[pltpu.VMEM((TOP_K, SC_COLLECT_CHUNK), jnp.int32),
                       pltpu.VMEM((TOP_K, SC_COLLECT_CHUNK, HALF_D), y.dtype),
                       pltpu.SemaphoreType.DMA((TOP_K,)),
                       pltpu.SemaphoreType.DMA((TOP_K,))],
        name="sc_collect")
    def run(y_hbm, dest_hbm, yg_hbm, idx_v, rows_v, gather_sem, write_sem):
        first = _sc_worker_id() * chunks_per_worker

        @pl.loop(0, chunks_per_worker)
        def _(j):
            ch = first + j
            pltpu.sync_copy(dest_hbm.at[ch], idx_v)
            gathers = [pltpu.async_copy(y_hbm.at[idx_v.at[k]], rows_v.at[k], gather_sem.at[k])
                       for k in range(TOP_K)]
            writes = []
            for k in range(TOP_K):
                gathers[k].wait()
                writes.append(pltpu.async_copy(
                    rows_v.at[k], yg_hbm.at[k, pl.ds(ch * SC_COLLECT_CHUNK, SC_COLLECT_CHUNK)],
                    write_sem.at[k]))
            for w in writes:
                w.wait()

    return run(y, dest3)


def _sc_collect_sum(y, dest3, gates16, first_chunk, n_chunks):
    C = SC_COLLECT_CHUNK
    n = n_chunks * C
    chunks_per_worker = n_chunks // SC_WORKERS
    lanes = 16
    mesh = plsc.VectorSubcoreMesh(core_axis_name="c", subcore_axis_name="s")

    @functools.partial(
        pl.kernel, mesh=mesh,
        out_type=jax.ShapeDtypeStruct((n, HALF_D), y.dtype),
        scratch_types=[pltpu.VMEM((TOP_K, C), jnp.int32),
                       pltpu.VMEM((TOP_K, C, lanes), jnp.uint32),
                       pltpu.VMEM((TOP_K, C, HALF_D), y.dtype),
                       pltpu.VMEM((C, HALF_D), y.dtype),
                       pltpu.SemaphoreType.DMA((TOP_K,))],
        compiler_params=pltpu.CompilerParams(needs_layout_passes=False),
        name="sc_collect_sum")
    def run(y_hbm, dest_hbm, gate_hbm, out_hbm, idx_v, gate_v, rows_v, sum_v, gather_sem):
        local0 = _sc_worker_id() * chunks_per_worker

        @pl.loop(0, chunks_per_worker)
        def _(i):
            ch = first_chunk + local0 + i
            pltpu.sync_copy(dest_hbm.at[ch], idx_v)
            gathers = [pltpu.async_copy(y_hbm.at[idx_v.at[k]], rows_v.at[k], gather_sem.at[k])
                       for k in range(TOP_K)]
            pltpu.sync_copy(gate_hbm.at[ch], gate_v)
            for cp in gathers:
                cp.wait()

            @plsc.parallel_loop(0, C)
            def _(j):
                g = [plsc.bitcast(gate_v[k, j, :], jnp.bfloat16) for k in range(TOP_K)]
                for c in range(HALF_D // lanes):
                    sl = pl.ds(c * lanes, lanes)
                    acc = None
                    for k in range(TOP_K):
                        term = plsc.bitcast(rows_v[k, j, sl], jnp.bfloat16) * g[k]
                        acc = term if acc is None else acc + term
                    sum_v[j, sl] = plsc.bitcast(acc, y.dtype)

            pltpu.sync_copy(sum_v, out_hbm.at[pl.ds((local0 + i) * C, C)])

    return run(y, dest3, gates16)


def _combine_sum_kernel(x1_ref, ys_ref, o_ref):
    lo, hi = _unpack_halves(ys_ref[...])
    o_ref[:, :HALF_D] = x1_ref[:, :HALF_D] + lo
    o_ref[:, HALF_D:] = x1_ref[:, HALF_D:] + hi


def _combine_sum(acc, ysum, split):
    n = acc.shape[0]
    rows = PROJ_ROWS
    steps = ysum.shape[0] // rows
    first = split * steps
    return pl.pallas_call(
        _combine_sum_kernel,
        out_shape=jax.ShapeDtypeStruct((n, D_MODEL), jnp.float32),
        grid=(steps,),
        in_specs=[pl.BlockSpec((rows, D_MODEL), lambda i: (first + i, 0)),
                  pl.BlockSpec((rows, HALF_D), lambda i: (i, 0))],
        out_specs=pl.BlockSpec((rows, D_MODEL), lambda i: (first + i, 0)),
        input_output_aliases={0: 0},
        compiler_params=pltpu.CompilerParams(
            dimension_semantics=("arbitrary",), vmem_limit_bytes=VMEM_LIMIT),
        name="moe_combine",
    )(acc, ysum)


def _layer(x2, batch, seq, attn_norm_g, w_in, a_q_g, a_k_g, a_sinks, b_q_g, b_k_g, w_out,
           ffn_norm_g, w_router, b_router, w1, b1, w2, b2):
    n = x2.shape[0]
    slopes = _alibi_slopes()
    q_scale = HEAD_DIM ** -0.5 * LOG2E
    reps = MXU_DIM // HEAD_DIM
    gains = jnp.stack([jnp.tile(a_q_g, reps) * q_scale, jnp.tile(a_k_g, reps),
                       jnp.tile(b_q_g, reps) * q_scale, jnp.tile(b_k_g, reps)]).astype(jnp.float32)

    proj = _in_proj(x2, attn_norm_g.reshape(1, -1), w_in, gains, batch, seq)
    qa, ka, va = proj[:3]
    nb = len(B_DILS)
    qbs, kbs, vbs = proj[3:3 + nb], proj[3 + nb:3 + 2 * nb], proj[3 + 2 * nb:]

    bias_a = _bias_tables(slopes[:A_Q_HEADS], A_STACK_HEADS, A_HALF_WINDOW, 1, Q_TILE + 2 * A_HALF_WINDOW)
    sink_col = jnp.repeat(a_sinks.astype(jnp.float32) * LOG2E, Q_TILE).reshape(
        A_Q_HEADS // A_STACK_HEADS, A_STACK_HEADS * Q_TILE, 1)
    as_seqs = lambda a: a.reshape(batch, seq, a.shape[-1])
    out_a = _banded_attention(as_seqs(qa), as_seqs(ka), as_seqs(va), bias_a, half_w=A_HALF_WINDOW,
                              sink=sink_col, name="attn_a")[0].reshape(n, A_Q_W)

    outs_b, lses_b = [], []
    for bi, (window, dil) in enumerate(B_BRANCHES):
        half_w = window // (2 * dil)
        bias_b = _bias_tables(slopes[A_Q_HEADS:], B_STACK_HEADS, half_w, dil, Q_TILE + 2 * half_w)
        L = seq // dil
        to_seqs = lambda a: a.reshape(batch * dil, L, a.shape[-1])
        o, lse = _banded_attention(to_seqs(qbs[bi]), to_seqs(kbs[bi]), to_seqs(vbs[bi]), bias_b,
                                   half_w=half_w, want_lse=True, name=f"attn_b_d{dil}")
        if dil == 1:
            outs_b.append(o.reshape(n, B_W))
            lses_b.append(lse.reshape(n, LANES))
        else:
            outs_b.append(o.reshape(batch, dil, L, B_W))
            lses_b.append(lse.reshape(batch, dil, L, LANES))

    wr_t = w_router.T.astype(jnp.float32)
    wr_hi = wr_t.astype(jnp.bfloat16)
    wr_lo = (wr_t - wr_hi.astype(jnp.float32)).astype(jnp.bfloat16)
    wr = jnp.concatenate([wr_hi, wr_lo], axis=0)
    br = jnp.broadcast_to(b_router.astype(jnp.float32)[:, None], (N_EXPERTS, LANES))
    x1, hf, topi, gates, ranks, counts = _out_proj_router(
        out_a, outs_b, lses_b, x2, w_out, ffn_norm_g.reshape(1, -1), wr, br, seq)

    g = MOE_ROWS
    nk = n * TOP_K
    step_rows = MOE_STEP_BLOCKS * g
    n_rows = -(-(nk + N_EXPERTS * g) // step_rows) * step_rows
    cnt = counts[:, 0]
    pcnt = (cnt + g - 1) // g * g
    pends = jnp.cumsum(pcnt)
    pstarts = pends - pcnt
    experts = jnp.arange(N_EXPERTS, dtype=jnp.int32)
    start_of = jnp.sum(jnp.where(topi[:, :, None] == experts, pstarts, 0), axis=-1)
    dest = (start_of + ranks).astype(jnp.int32)
    plan = _moe_plan(pends, n_rows // g)
    dest3 = dest.reshape(TOP_K, n // SC_CHUNK, SC_CHUNK).transpose(1, 0, 2)

    xb = _sc_dispatch(hf, dest3, n_rows)
    y = _moe_experts(plan, xb, w1, b1[:, None, :], w2, b2[:, None, :])
    smallest_range = max(PROJ_ROWS, SC_COLLECT_CHUNK * SC_WORKERS)
    n_splits = max(1, min(COMBINE_SPLITS, n // smallest_range))
    per_split = n // n_splits
    chunked = lambda a: a.reshape(TOP_K, n // SC_COLLECT_CHUNK, SC_COLLECT_CHUNK).transpose(1, 0, 2)
    dest_c = chunked(dest)
    gate_bits = lax.bitcast_convert_type(gates.astype(jnp.bfloat16), jnp.uint16).astype(jnp.uint32)
    gate_words = gate_bits | (gate_bits << 16)
    gates_c = jnp.broadcast_to(chunked(gate_words)[..., None], dest_c.shape + (16,))
    chunks_per_split = per_split // SC_COLLECT_CHUNK
    out = x1
    for s in range(n_splits):
        ysum = _sc_collect_sum(y, dest_c, gates_c, s * chunks_per_split, chunks_per_split)
        out = _combine_sum(out, ysum, s)
    return out


def kernel(x, attn_norm_g, w_in, a_q_norm_g, a_k_norm_g, a_sinks, b_q_norm_g, b_k_norm_g, w_out,
           ffn_norm_g, w_router, b_router, w1, b1, w2, b2):
    batch, seq, d = x.shape
    x2 = x.reshape(batch * seq, d)
    for i in range(attn_norm_g.shape[0]):
        x2 = _layer(x2, batch, seq, attn_norm_g[i], w_in[i], a_q_norm_g[i], a_k_norm_g[i],
                    a_sinks[i], b_q_norm_g[i], b_k_norm_g[i], w_out[i], ffn_norm_g[i],
                    w_router[i], b_router[i], w1[i], b1[i], w2[i], b2[i])
    return x2.reshape(batch, seq, d)
```

```python
import functools

import jax
import jax.numpy as jnp
import numpy as np
from jax import lax
from jax.experimental import pallas as pl
from jax.experimental.pallas import tpu as pltpu
from jax.experimental.pallas import tpu_sc as plsc

D_MODEL = 1024
HALF_D = D_MODEL // 2
HEAD_DIM = 64
LANES = 128
MXU_DIM = 256
A_Q_HEADS = 8
A_KV_HEADS = 2
B_HEADS = 8
A_HALF_WINDOW = 128
B_BRANCHES = ((128, 1), (512, 4), (2048, 16))
B_DILS = tuple(d for _, d in B_BRANCHES)
RESIDUE_STRIDE = 4
N_ALIBI_HEADS = 16
A_Q_W = A_Q_HEADS * HEAD_DIM
A_KV_W = A_KV_HEADS * HEAD_DIM
B_W = B_HEADS * HEAD_DIM
N_EXPERTS = 32
TOP_K = 4
D_FF = 1024
SWIGLU_ALPHA = 1.702
SWIGLU_LIMIT = 7.0
NORM_EPS = 1e-5
MASK_VALUE = -1e30
LOG2E = 1.4426950408889634

Q_TILE = 128
ATTN_STEP_ROWS = 2048
A_STACK_HEADS = 4
B_STACK_HEADS = 2
PROJ_ROWS = 1024
IN_PROJ_ROWS = 1024
MOE_ROWS = 512
FF_CHUNK = 512
MOE_STEP_BLOCKS = 2
MOE_WEIGHT_BUFS = 3
VMEM_LIMIT = 48 * 1024 * 1024
LARGE_VMEM_LIMIT = 58 * 1024 * 1024
SC_CORES = 2
SC_SUBCORES = 16
SC_WORKERS = SC_CORES * SC_SUBCORES
SC_CHUNK = 64
SC_COLLECT_CHUNK = 16
COMBINE_SPLITS = 2


def _pack_halves(v):
    lo = v[:, :HALF_D].astype(jnp.bfloat16).astype(jnp.float32)
    hi = v[:, HALF_D:].astype(jnp.bfloat16).astype(jnp.float32)
    return (pltpu.bitcast(lo, jnp.uint32) >> 16) | pltpu.bitcast(hi, jnp.uint32)


def _unpack_halves(w):
    lo = pltpu.bitcast(w << 16, jnp.float32)
    hi = pltpu.bitcast(w & jnp.uint32(0xFFFF0000), jnp.float32)
    return lo, hi


def _alibi_slopes():
    return np.exp2(-8.0 * np.arange(1, N_ALIBI_HEADS + 1, dtype=np.float32) / N_ALIBI_HEADS).astype(np.float32)


def _bias_tables(head_slopes, heads_per_group, half_w, dist_scale, tk):
    i = np.arange(Q_TILE)[:, None]
    j = np.arange(tk)[None, :]
    tabs = []
    for shift in (0, half_w, tk - Q_TILE):
        dist = np.abs(j - shift - i)
        valid = dist <= half_w
        per_head = []
        for sl in head_slopes:
            b = (-np.float64(sl) * LOG2E * (dist * dist_scale)).astype(np.float32)
            per_head.append(np.where(valid, b, np.float32(MASK_VALUE)).astype(np.float32))
        t = np.stack(per_head).reshape(-1, heads_per_group * Q_TILE, tk)
        tabs.append(t)
    return jnp.asarray(np.stack(tabs))


def _in_proj_kernel(x_ref, g_ref, w_hbm, gains_ref, qa_ref, ka_ref, va_ref, *rest):
    b_refs, (scr_ref, scr2_ref, w_ref, stage_ref, sem) = rest[:-5], rest[-5:]

    @pl.when(pl.program_id(0) == 0)
    def _():
        width = stage_ref.shape[1]
        for c0 in range(0, w_ref.shape[1], width):
            cp = pltpu.make_async_copy(w_hbm.at[:, c0:c0 + width], stage_ref, sem)
            cp.start()
            cp.wait()
            w_ref[:, c0:c0 + width] = stage_ref[...].astype(jnp.bfloat16)

    x = x_ref[...]
    xn = x * lax.rsqrt(jnp.mean(x * x, axis=-1, keepdims=True) + NORM_EPS) * g_ref[...]
    xn = xn.astype(jnp.bfloat16)
    r = lax.broadcasted_iota(jnp.int32, (MXU_DIM, MXU_DIM), 0) // HEAD_DIM
    c = lax.broadcasted_iota(jnp.int32, (MXU_DIM, MXU_DIM), 1) // HEAD_DIM
    blockdiag = jnp.where(r == c, 1.0, 0.0).astype(jnp.bfloat16)

    def head_rms(sec, gain_row):
        width = sec.shape[1]
        parts = []
        step = min(width, MXU_DIM)
        for j in range(width // step):
            p = sec[:, j * step:(j + 1) * step]
            ss = jnp.dot((p * p).astype(jnp.bfloat16), blockdiag[:step, :step],
                         preferred_element_type=jnp.float32)
            parts.append(p * lax.rsqrt(ss * (1.0 / HEAD_DIM) + NORM_EPS)
                         * gains_ref[gain_row:gain_row + 1, :step])
        return parts

    def project(col0, width, gain_row):
        sec = jnp.dot(xn, w_ref[:, col0:col0 + width], preferred_element_type=jnp.float32)
        return [sec] if gain_row is None else head_rms(sec, gain_row)

    def store(out_ref, parts):
        w = parts[0].shape[1]
        for j, p in enumerate(parts):
            out_ref[:, j * w:(j + 1) * w] = p.astype(out_ref.dtype)

    def per_kv_head(p):
        lane = lax.broadcasted_iota(jnp.int32, p.shape, 1)
        swapped = pltpu.roll(p, HEAD_DIM, axis=1)
        low = lane < HEAD_DIM
        return [jnp.where(low, p, swapped), jnp.where(low, swapped, p)]

    store(qa_ref, project(0, A_Q_W, 0))
    kva = project(A_Q_W, 2 * A_KV_W, None)[0]
    store(ka_ref, per_kv_head(head_rms(kva[:, :A_KV_W], 1)[0]))
    store(va_ref, per_kv_head(kva[:, A_KV_W:]))

    rows = x_ref.shape[0]
    col0 = A_Q_W + 2 * A_KV_W
    for t, gain_row in enumerate((2, 3, None)):
        parts = project(col0 + t * B_W, B_W, gain_row)
        sec = jnp.concatenate(parts, axis=-1) if len(parts) > 1 else parts[0]
        for j in range(B_W // LANES):
            scr_ref[j] = sec[:, j * LANES:(j + 1) * LANES]
        n_lane_chunks = B_W // LANES
        prev_dil = 1
        for bi, dil in enumerate(B_DILS):
            out_ref = b_refs[t * len(B_DILS) + bi]
            if dil == 1:
                out_ref[...] = sec.astype(out_ref.dtype)
                continue
            assert dil == prev_dil * RESIDUE_STRIDE
            last = dil == B_DILS[-1]
            for res in range(dil):
                r_prev, r_sub = res % prev_dil, res // prev_dil
                for j in range(n_lane_chunks):
                    if prev_dil == 1:
                        v = scr_ref[j, pl.ds(r_sub, rows // dil, stride=RESIDUE_STRIDE), :]
                    else:
                        v = scr2_ref[r_prev * n_lane_chunks + j,
                                     pl.ds(r_sub, rows // dil, stride=RESIDUE_STRIDE), :]
                    out_ref[0, res, :, j * LANES:(j + 1) * LANES] = v.astype(out_ref.dtype)
                    if not last:
                        scr2_ref[res * n_lane_chunks + j] = v
            prev_dil = dil


def _in_proj(x2, g, w_in, gains, batch, seq):
    n = x2.shape[0]
    rows = IN_PROJ_ROWS
    steps = seq // rows
    a_widths = (A_Q_W, 2 * A_KV_W, 2 * A_KV_W)
    out_shape = [jax.ShapeDtypeStruct((n, w), jnp.bfloat16) for w in a_widths]
    out_specs = [pl.BlockSpec((rows, w), lambda i: (i, 0)) for w in a_widths]
    for _ in range(3):
        for dil in B_DILS:
            if dil == 1:
                out_shape.append(jax.ShapeDtypeStruct((n, B_W), jnp.bfloat16))
                out_specs.append(pl.BlockSpec((rows, B_W), lambda i: (i, 0)))
            else:
                out_shape.append(jax.ShapeDtypeStruct((batch, dil, seq // dil, B_W), jnp.bfloat16))
                out_specs.append(pl.BlockSpec((1, dil, rows // dil, B_W),
                                              lambda i: (i // steps, 0, i % steps, 0)))
    return pl.pallas_call(
        _in_proj_kernel,
        out_shape=out_shape,
        grid=(n // rows,),
        in_specs=[
            pl.BlockSpec((rows, D_MODEL), lambda i: (i, 0)),
            pl.BlockSpec((1, D_MODEL), lambda i: (0, 0)),
            pl.BlockSpec(memory_space=pl.ANY),
            pl.BlockSpec(gains.shape, lambda i: (0, 0)),
        ],
        out_specs=out_specs,
        scratch_shapes=[pltpu.VMEM((B_W // LANES, rows, LANES), jnp.float32),
                        pltpu.VMEM((RESIDUE_STRIDE * B_W // LANES, rows // RESIDUE_STRIDE, LANES),
                                   jnp.float32),
                        pltpu.VMEM(w_in.shape, jnp.bfloat16),
                        pltpu.VMEM((w_in.shape[0], w_in.shape[1] // 3), jnp.float32),
                        pltpu.SemaphoreType.DMA],
        compiler_params=pltpu.CompilerParams(
            dimension_semantics=("arbitrary",), vmem_limit_bytes=LARGE_VMEM_LIMIT),
        name="in_proj",
    )(x2, g, w_in, gains)


def _attn_kernel(*refs, n_chunks, kv_chunks, heads_per_stack, tk, half_w, seq_len, rows, has_sink,
                 want_lse):
    it = iter(refs)
    q_ref, k_ref, v_ref, bias_ref = next(it), next(it), next(it), next(it)
    sink_ref = next(it) if has_sink else None
    o_ref = next(it)
    lse_ref = next(it) if want_lse else None

    n_tiles = seq_len // Q_TILE
    tiles_per_step = rows // Q_TILE
    chunks_per_group = n_chunks // kv_chunks
    assert (2 * chunks_per_group) % heads_per_stack == 0
    step = pl.program_id(1)
    lane = lax.broadcasted_iota(jnp.int32, (Q_TILE, LANES), 1)
    low_half = lane < HEAD_DIM
    ones = jnp.ones((tk, LANES), jnp.bfloat16)

    for sq, t in [(a, b) for a in range(q_ref.shape[0]) for b in range(tiles_per_step)]:
        tile = step * tiles_per_step + t
        q0 = tile * Q_TILE
        kv_rows = k_ref.shape[1]
        kv_row0 = jnp.clip(step * rows - half_w, 0, seq_len - kv_rows)
        start = pl.multiple_of(jnp.clip(q0 - half_w, 0, seq_len - tk) - kv_row0, HEAD_DIM)
        variant = jnp.where(tile == 0, 0, jnp.where(tile == n_tiles - 1, 2, 1))
        r0 = t * Q_TILE
        lse_tile = jnp.zeros((Q_TILE, LANES), jnp.float32)
        head_o, head_lse = {}, {}
        for g in range(2 * n_chunks // heads_per_stack):
            heads = range(g * heads_per_stack, (g + 1) * heads_per_stack)
            kv = (heads[0] // 2) // chunks_per_group
            kc = k_ref[sq, pl.ds(start, tk), kv * LANES:(kv + 1) * LANES]
            vc = v_ref[sq, pl.ds(start, tk), kv * LANES:(kv + 1) * LANES]
            v_aug = jnp.concatenate([vc, ones], axis=1)
            q_parts = []
            for h in heads:
                c = h // 2
                q2 = q_ref[sq, r0:r0 + Q_TILE, c * LANES:(c + 1) * LANES]
                keep = low_half if h % 2 == 0 else ~low_half
                q_parts.append(jnp.where(keep, q2, jnp.zeros_like(q2)))
            qs = q_parts[0] if len(q_parts) == 1 else jnp.concatenate(q_parts, axis=0)
            s = lax.dot_general(qs, kc, (((1,), (1,)), ((), ())),
                                preferred_element_type=jnp.float32)
            s = s + bias_ref[variant, g]
            m = jnp.max(s, axis=-1, keepdims=True)
            if has_sink:
                m = jnp.maximum(m, sink_ref[g])
            p = jnp.exp2(s - m)
            ov = jnp.dot(p.astype(jnp.bfloat16), v_aug, preferred_element_type=jnp.float32)
            o, l = ov[:, :LANES], ov[:, LANES:]
            if has_sink:
                l = l + jnp.exp2(sink_ref[g] - m)
            o = o * (1.0 / l)
            if want_lse:
                lse = m + jnp.log(l) * LOG2E
            for idx, h in enumerate(heads):
                head_o[h] = o[idx * Q_TILE:(idx + 1) * Q_TILE]
                if want_lse:
                    head_lse[h] = lse[idx * Q_TILE:(idx + 1) * Q_TILE]
                if h % 2 == 1:
                    c = h // 2
                    o2 = jnp.where(low_half, head_o.pop(h - 1), head_o.pop(h))
                    o_ref[sq, r0:r0 + Q_TILE, c * LANES:(c + 1) * LANES] = o2.astype(o_ref.dtype)
                    if want_lse:
                        lse_tile = jnp.where(lane == h - 1, head_lse.pop(h - 1),
                                             jnp.where(lane == h, head_lse.pop(h), lse_tile))
        if want_lse:
            lse_ref[sq, r0:r0 + Q_TILE, :] = lse_tile


def _banded_attention(q, k, v, bias, *, half_w, sink=None, want_lse=False, name):
    n_seq, L, qw = q.shape
    kw = k.shape[2]
    tk = Q_TILE + 2 * half_w
    rows = min(ATTN_STEP_ROWS, L)
    seqs = ATTN_STEP_ROWS // rows
    kv_rows = min(L, rows + 2 * half_w)

    def kv_index(s, i):
        row0 = pl.multiple_of(jnp.clip(i * rows - half_w, 0, L - kv_rows), HEAD_DIM)
        return s * seqs, row0, 0

    args = [q, k, v, bias]
    in_specs = [
        pl.BlockSpec((seqs, rows, qw), lambda s, i: (s, i, 0)),
        pl.BlockSpec((pl.Element(seqs), pl.Element(kv_rows), pl.Element(kw)), kv_index),
        pl.BlockSpec((pl.Element(seqs), pl.Element(kv_rows), pl.Element(kw)), kv_index),
        pl.BlockSpec(bias.shape, lambda s, i: (0, 0, 0, 0)),
    ]
    if sink is not None:
        args.append(sink)
        in_specs.append(pl.BlockSpec(sink.shape, lambda s, i: (0, 0, 0)))
    out_shape = [jax.ShapeDtypeStruct((n_seq, L, qw), jnp.bfloat16)]
    out_specs = [pl.BlockSpec((seqs, rows, qw), lambda s, i: (s, i, 0))]
    if want_lse:
        out_shape.append(jax.ShapeDtypeStruct((n_seq, L, LANES), jnp.float32))
        out_specs.append(pl.BlockSpec((seqs, rows, LANES), lambda s, i: (s, i, 0)))

    kern = functools.partial(
        _attn_kernel, n_chunks=qw // LANES, kv_chunks=kw // LANES,
        heads_per_stack=bias.shape[2] // Q_TILE, tk=tk, half_w=half_w, seq_len=L,
        rows=rows, has_sink=sink is not None, want_lse=want_lse)
    return pl.pallas_call(
        kern,
        out_shape=out_shape,
        grid=(n_seq // seqs, L // rows),
        in_specs=in_specs,
        out_specs=out_specs,
        compiler_params=pltpu.CompilerParams(
            dimension_semantics=("arbitrary", "arbitrary"), vmem_limit_bytes=VMEM_LIMIT),
        name=name,
    )(*args)


def _out_proj_router_kernel(*refs):
    nb = len(B_DILS)
    oa_ref = refs[0]
    o_refs = refs[1:1 + nb]
    lse_refs = refs[1 + nb:1 + 2 * nb]
    (x_ref, wo_ref, g_ref, wr_ref, br_ref,
     x1_ref, hf_ref, topi_ref, gate_ref, rank_ref, cnt_ref,
     tri_ref, carry_ref, so_ref, sl_ref, wo_bf_ref) = refs[1 + 2 * nb:]
    i = pl.program_id(0)
    rows = x_ref.shape[0]

    @pl.when(i == 0)
    def _():
        a = lax.broadcasted_iota(jnp.int32, (rows, rows), 0)
        b = lax.broadcasted_iota(jnp.int32, (rows, rows), 1)
        tri_ref[...] = jnp.where(a <= b, 1.0, 0.0).astype(jnp.bfloat16)
        carry_ref[...] = jnp.zeros_like(carry_ref)
        wo_bf_ref[...] = wo_ref[...].astype(jnp.bfloat16)

    outs, lses = [], []
    for bi, dil in enumerate(B_DILS):
        if dil == 1:
            outs.append(o_refs[bi][...].astype(jnp.float32))
            lses.append(lse_refs[bi][...])
        else:
            for res in range(dil):
                for j in range(B_W // LANES):
                    so_ref[bi, j, pl.ds(res, rows // dil, stride=dil), :] = (
                        o_refs[bi][0, res, :, j * LANES:(j + 1) * LANES].astype(jnp.float32))
                sl_ref[bi, pl.ds(res, rows // dil, stride=dil), :] = lse_refs[bi][0, res]
            outs.append(jnp.concatenate([so_ref[bi, j] for j in range(B_W // LANES)], axis=-1))
            lses.append(sl_ref[bi])

    mx = functools.reduce(jnp.maximum, lses)
    es = [jnp.exp2(l - mx) for l in lses]
    inv = 1.0 / functools.reduce(lambda a, b: a + b, es)
    eh = lax.broadcasted_iota(jnp.int32, (LANES, B_W), 0)
    ej = lax.broadcasted_iota(jnp.int32, (LANES, B_W), 1) // HEAD_DIM
    expand = jnp.where(eh == ej, 1.0, 0.0).astype(jnp.bfloat16)
    ob = jnp.zeros((rows, B_W), jnp.float32)
    for e, o in zip(es, outs):
        w = e * inv
        wide = jnp.dot(w.astype(jnp.bfloat16), expand, preferred_element_type=jnp.float32)
        ob = ob + wide * o

    attn = jnp.concatenate([oa_ref[...], ob.astype(jnp.bfloat16)], axis=-1)
    x1 = x_ref[...] + jnp.dot(attn, wo_bf_ref[...], preferred_element_type=jnp.float32)
    x1_ref[...] = x1
    hf = x1 * lax.rsqrt(jnp.mean(x1 * x1, axis=-1, keepdims=True) + NORM_EPS) * g_ref[...]
    hf_hi = hf.astype(jnp.bfloat16)
    hf_ref[...] = _pack_halves(hf)
    hf_lo = (hf - hf_hi.astype(jnp.float32)).astype(jnp.bfloat16)

    nt = (((1,), (1,)), ((), ()))
    lg_hi = lax.dot_general(wr_ref[...], hf_hi, nt, preferred_element_type=jnp.float32)
    lg_lo = lax.dot_general(wr_ref[0:N_EXPERTS, :], hf_lo, nt, preferred_element_type=jnp.float32)
    logits = lg_hi[0:N_EXPERTS] + lg_hi[N_EXPERTS:] + lg_lo + br_ref[:, 0:1]


    eidx = lax.broadcasted_iota(jnp.int32, (N_EXPERTS, rows), 0)
    work = logits
    vals, sels = [], []
    for k in range(TOP_K):
        mk = jnp.max(work, axis=0, keepdims=True)
        ik = jnp.min(jnp.where(work == mk, eidx, N_EXPERTS), axis=0, keepdims=True)
        sel = eidx == ik
        work = jnp.where(sel, -jnp.inf, work)
        vals.append(mk)
        sels.append(sel)
        topi_ref[k:k + 1, :] = ik
    exps = [jnp.exp(vk - vals[0]) for vk in vals]
    denom = exps[0] + exps[1] + exps[2] + exps[3]
    ginv = 1.0 / denom
    for k in range(TOP_K):
        gate_ref[k:k + 1, :] = exps[k] * ginv

    onehot = jnp.zeros((N_EXPERTS, rows), jnp.float32)
    for sel in sels:
        onehot = onehot + jnp.where(sel, 1.0, 0.0)
    incl = jnp.dot(onehot.astype(jnp.bfloat16), tri_ref[...], preferred_element_type=jnp.float32)
    before = incl - onehot + carry_ref[:, 0:1]
    for k in range(TOP_K):
        rk = jnp.sum(jnp.where(sels[k], before, 0.0), axis=0, keepdims=True)
        rank_ref[k:k + 1, :] = rk.astype(jnp.int32)
    carry = carry_ref[...] + jnp.sum(onehot, axis=1, keepdims=True)
    carry_ref[...] = carry
    cnt_ref[...] = carry.astype(jnp.int32)


def _out_proj_router(oa, outs_b, lses_b, x2, wo_bf, g, wr, br, seq):
    n = x2.shape[0]
    rows = PROJ_ROWS
    steps = seq // rows
    row_spec = lambda w: pl.BlockSpec((rows, w), lambda i: (i, 0))
    full = lambda a: pl.BlockSpec(a.shape, lambda i: (0,) * a.ndim)
    col_spec = pl.BlockSpec((TOP_K, rows), lambda i: (0, i))

    def branch_spec(dil, w):
        if dil == 1:
            return row_spec(w)
        return pl.BlockSpec((1, dil, rows // dil, w), lambda i: (i // steps, 0, i % steps, 0))

    in_specs = ([row_spec(A_Q_W)]
                + [branch_spec(d, B_W) for d in B_DILS]
                + [branch_spec(d, LANES) for d in B_DILS]
                + [row_spec(D_MODEL), full(wo_bf), full(g), full(wr), full(br)])
    return pl.pallas_call(
        _out_proj_router_kernel,
        out_shape=[
            jax.ShapeDtypeStruct((n, D_MODEL), jnp.float32),
            jax.ShapeDtypeStruct((n, HALF_D), jnp.uint32),
            jax.ShapeDtypeStruct((TOP_K, n), jnp.int32),
            jax.ShapeDtypeStruct((TOP_K, n), jnp.float32),
            jax.ShapeDtypeStruct((TOP_K, n), jnp.int32),
            jax.ShapeDtypeStruct((N_EXPERTS, LANES), jnp.int32),
        ],
        grid=(n // rows,),
        in_specs=in_specs,
        out_specs=[row_spec(D_MODEL), row_spec(HALF_D), col_spec, col_spec, col_spec,
                   pl.BlockSpec((N_EXPERTS, LANES), lambda i: (0, 0))],
        scratch_shapes=[pltpu.VMEM((rows, rows), jnp.bfloat16),
                        pltpu.VMEM((N_EXPERTS, LANES), jnp.float32),
                        pltpu.VMEM((len(B_DILS), B_W // LANES, rows, LANES), jnp.float32),
                        pltpu.VMEM((len(B_DILS), rows, LANES), jnp.float32),
                        pltpu.VMEM(wo_bf.shape, jnp.bfloat16)],
        compiler_params=pltpu.CompilerParams(
            dimension_semantics=("arbitrary",), vmem_limit_bytes=LARGE_VMEM_LIMIT),
        name="out_proj_router",
    )(oa, *outs_b, *lses_b, x2, wo_bf, g, wr, br)


def _mxu_dot(a_bf, w_f32):
    return lax.dot_general(a_bf, w_f32, (((1,), (0,)), ((), ())), preferred_element_type=jnp.float32)


def _moe_kernel(blk_exp_ref, first_ref, slot_ref, next_exp_ref, n_used_ref,
                x_ref, w1_hbm, b1_ref, w2_hbm, b2_ref, y_ref, w1_buf, w2_buf, sem):
    step = pl.program_id(0)

    def weight_copies(expert, slot):
        return (pltpu.make_async_copy(w1_hbm.at[expert], w1_buf.at[slot], sem.at[slot, 0]),
                pltpu.make_async_copy(w2_hbm.at[expert], w2_buf.at[slot], sem.at[slot, 1]))

    @pl.when(step * MOE_STEP_BLOCKS < n_used_ref[0])
    def _():
        @pl.when(step == 0)
        def _():
            for cp in weight_copies(blk_exp_ref[0], slot_ref[0]):
                cp.start()

        for j in range(MOE_STEP_BLOCKS):
            i = step * MOE_STEP_BLOCKS + j

            @pl.when(first_ref[i] == 1)
            def _():
                slot = slot_ref[i]
                for cp in weight_copies(blk_exp_ref[i], slot):
                    cp.wait()

                @pl.when(next_exp_ref[i] >= 0)
                def _():
                    for cp in weight_copies(next_exp_ref[i], (slot + 1) % MOE_WEIGHT_BUFS):
                        cp.start()

        for j in range(MOE_STEP_BLOCKS):
            i = step * MOE_STEP_BLOCKS + j
            slot = slot_ref[i]
            expert = blk_exp_ref[i]
            rs = slice(j * MOE_ROWS, (j + 1) * MOE_ROWS)
            x = jnp.concatenate(_unpack_halves(x_ref[rs, :]), axis=-1).astype(jnp.bfloat16)
            acc = jnp.zeros((MOE_ROWS, D_MODEL), jnp.float32)
            for c in range(D_FF // FF_CHUNK):
                lo = c * FF_CHUNK
                glu = _mxu_dot(x, w1_buf[slot, :, lo:lo + FF_CHUNK]) + b1_ref[expert, :, lo:lo + FF_CHUNK]
                lin = (_mxu_dot(x, w1_buf[slot, :, D_FF + lo:D_FF + lo + FF_CHUNK])
                       + b1_ref[expert, :, D_FF + lo:D_FF + lo + FF_CHUNK])
                glu = jnp.minimum(glu, SWIGLU_LIMIT)
                lin = jnp.clip(lin, -SWIGLU_LIMIT, SWIGLU_LIMIT)
                act = glu * (1.0 / (1.0 + jnp.exp(-SWIGLU_ALPHA * glu))) * (lin + 1.0)
                acc = acc + _mxu_dot(act.astype(jnp.bfloat16), w2_buf[slot, lo:lo + FF_CHUNK, :])
            y_ref[rs, :] = _pack_halves(acc + b2_ref[expert])


def _moe_plan(pends, n_blk):
    g = MOE_ROWS
    blk_row0 = jnp.arange(n_blk, dtype=jnp.int32) * g
    blk_exp = jnp.minimum(jnp.sum(pends[None, :] <= blk_row0[:, None], axis=-1),
                          N_EXPERTS - 1).astype(jnp.int32)
    n_used = (pends[-1] // g).astype(jnp.int32)
    used = blk_row0 < pends[-1]
    prev_exp = jnp.concatenate([jnp.full((1,), -1, jnp.int32), blk_exp[:-1]])
    first = (used & (blk_exp != prev_exp)).astype(jnp.int32)
    slot = ((jnp.cumsum(first) - 1) % MOE_WEIGHT_BUFS).astype(jnp.int32)
    pstarts = jnp.concatenate([jnp.zeros((1,), pends.dtype), pends[:-1]])
    nonempty = pends > pstarts
    experts = jnp.arange(N_EXPERTS, dtype=jnp.int32)
    later = nonempty[None, :] & (experts[None, :] > experts[:, None])
    next_nonempty = jnp.min(jnp.where(later, experts[None, :], N_EXPERTS), axis=-1)
    next_nonempty = jnp.where(next_nonempty == N_EXPERTS, -1, next_nonempty).astype(jnp.int32)
    next_exp = jnp.sum(jnp.where(blk_exp[:, None] == experts[None, :], next_nonempty[None, :], 0),
                       axis=-1).astype(jnp.int32)
    return blk_exp, first, slot, next_exp, n_used.reshape(1)


def _moe_experts(plan, xb, w1, b1, w2, b2):
    n_rows = xb.shape[0]
    n_blk = n_rows // MOE_ROWS

    step_rows = MOE_STEP_BLOCKS * MOE_ROWS

    def blk(i, *p):
        return jnp.minimum(i, (p[-1][0] - 1) // MOE_STEP_BLOCKS)

    grid_spec = pltpu.PrefetchScalarGridSpec(
        num_scalar_prefetch=len(plan),
        grid=(n_blk // MOE_STEP_BLOCKS,),
        in_specs=[
            pl.BlockSpec((step_rows, HALF_D), lambda i, *p: (blk(i, *p), 0)),
            pl.BlockSpec(memory_space=pl.ANY),
            pl.BlockSpec(b1.shape, lambda i, *p: (0, 0, 0)),
            pl.BlockSpec(memory_space=pl.ANY),
            pl.BlockSpec(b2.shape, lambda i, *p: (0, 0, 0)),
        ],
        out_specs=pl.BlockSpec((step_rows, HALF_D), lambda i, *p: (blk(i, *p), 0)),
        scratch_shapes=[pltpu.VMEM((MOE_WEIGHT_BUFS, D_MODEL, 2 * D_FF), jnp.float32),
                        pltpu.VMEM((MOE_WEIGHT_BUFS, D_FF, D_MODEL), jnp.float32),
                        pltpu.SemaphoreType.DMA((MOE_WEIGHT_BUFS, 2))],
    )
    return pl.pallas_call(
        _moe_kernel,
        out_shape=jax.ShapeDtypeStruct((n_rows, HALF_D), jnp.uint32),
        grid_spec=grid_spec,
        compiler_params=pltpu.CompilerParams(
            dimension_semantics=("arbitrary",), vmem_limit_bytes=LARGE_VMEM_LIMIT),
        name="moe_experts",
    )(*plan, xb, w1, b1, w2, b2)


def _combine_kernel(x1_ref, yg_ref, gate_ref, o_ref):
    acc_lo = x1_ref[:, :HALF_D]
    acc_hi = x1_ref[:, HALF_D:]
    rows = x1_ref.shape[0]
    gates = jnp.concatenate([gate_ref[...], jnp.zeros((8 - TOP_K, rows), jnp.float32)], axis=0).T
    for k in range(TOP_K):
        lo, hi = _unpack_halves(yg_ref[k])
        gk = gates[:, k:k + 1]
        acc_lo = acc_lo + lo * gk
        acc_hi = acc_hi + hi * gk
    o_ref[:, :HALF_D] = acc_lo
    o_ref[:, HALF_D:] = acc_hi


def _combine(acc, yg, gates_nk, split):
    n = acc.shape[0]
    rows = PROJ_ROWS
    steps = yg.shape[1] // rows
    first = split * steps
    return pl.pallas_call(
        _combine_kernel,
        out_shape=jax.ShapeDtypeStruct((n, D_MODEL), jnp.float32),
        grid=(steps,),
        in_specs=[pl.BlockSpec((rows, D_MODEL), lambda i: (first + i, 0)),
                  pl.BlockSpec((TOP_K, rows, HALF_D), lambda i: (0, i, 0)),
                  pl.BlockSpec((TOP_K, rows), lambda i: (0, first + i))],
        out_specs=pl.BlockSpec((rows, D_MODEL), lambda i: (first + i, 0)),
        input_output_aliases={0: 0},
        compiler_params=pltpu.CompilerParams(
            dimension_semantics=("arbitrary",), vmem_limit_bytes=VMEM_LIMIT),
        name="moe_combine",
    )(acc, yg, gates_nk)


def _sc_worker_id():
    return lax.axis_index("s") * SC_CORES + lax.axis_index("c")


def _sc_dispatch(hf, dest3, n_rows):
    n = hf.shape[0]
    chunks_per_worker = n // SC_CHUNK // SC_WORKERS
    mesh = plsc.VectorSubcoreMesh(core_axis_name="c", subcore_axis_name="s")

    @functools.partial(
        pl.kernel, mesh=mesh,
        out_type=jax.ShapeDtypeStruct((n_rows, HALF_D), hf.dtype),
        scratch_types=[pltpu.VMEM((2, TOP_K, SC_CHUNK), jnp.int32),
                       pltpu.VMEM((2, SC_CHUNK, HALF_D), hf.dtype),
                       pltpu.SemaphoreType.DMA((2,)),
                       pltpu.SemaphoreType.DMA((2,))],
        name="sc_dispatch")
    def run(hf_hbm, dest_hbm, xb_hbm, idx_v, rows_v, load_sem, scatter_sem):
        first = _sc_worker_id() * chunks_per_worker

        def load(j):
            slot = j % 2
            pltpu.sync_copy(dest_hbm.at[first + j], idx_v.at[slot])
            return pltpu.async_copy(hf_hbm.at[pl.ds((first + j) * SC_CHUNK, SC_CHUNK)],
                                    rows_v.at[slot], load_sem.at[slot])

        loads = {0: load(0)}
        scatters = {}
        for j in range(chunks_per_worker):
            slot = j % 2
            loads.pop(j).wait()
            scatters[j] = [pltpu.async_copy(rows_v.at[slot], xb_hbm.at[idx_v.at[slot, k]],
                                            scatter_sem.at[slot]) for k in range(TOP_K)]
            if j >= 1:
                for cp in scatters.pop(j - 1):
                    cp.wait()
            if j + 1 < chunks_per_worker:
                loads[j + 1] = load(j + 1)
        for cp in scatters.pop(chunks_per_worker - 1):
            cp.wait()

    return run(hf, dest3)


def _sc_collect(y, dest3):
    n = dest3.shape[0] * SC_COLLECT_CHUNK
    chunks_per_worker = n // SC_COLLECT_CHUNK // SC_WORKERS
    mesh = plsc.VectorSubcoreMesh(core_axis_name="c", subcore_axis_name="s")

    @functools.partial(
        pl.kernel, mesh=mesh,
        out_type=jax.ShapeDtypeStruct((TOP_K, n, HALF_D), y.dtype),
        scratch_types=[pltpu.VMEM((TOP_K, SC_COLLECT_CHUNK), jnp.int32),
                       pltpu.VMEM((TOP_K, SC_COLLECT_CHUNK, HALF_D), y.dtype),
                       pltpu.SemaphoreType.DMA((TOP_K,)),
                       pltpu.SemaphoreType.DMA((TOP_K,))],
        name="sc_collect")
    def run(y_hbm, dest_hbm, yg_hbm, idx_v, rows_v, gather_sem, write_sem):
        first = _sc_worker_id() * chunks_per_worker

        @pl.loop(0, chunks_per_worker)
        def _(j):
            ch = first + j
            pltpu.sync_copy(dest_hbm.at[ch], idx_v)
            gathers = [pltpu.async_copy(y_hbm.at[idx_v.at[k]], rows_v.at[k], gather_sem.at[k])
                       for k in range(TOP_K)]
            writes = []
            for k in range(TOP_K):
                gathers[k].wait()
                writes.append(pltpu.async_copy(
                    rows_v.at[k], yg_hbm.at[k, pl.ds(ch * SC_COLLECT_CHUNK, SC_COLLECT_CHUNK)],
                    write_sem.at[k]))
            for w in writes:
                w.wait()

    return run(y, dest3)


def _sc_collect_sum(y, dest3, gates16, first_chunk, n_chunks):
    C = SC_COLLECT_CHUNK
    n = n_chunks * C
    chunks_per_worker = n_chunks // SC_WORKERS
    lanes = 16
    mesh = plsc.VectorSubcoreMesh(core_axis_name="c", subcore_axis_name="s")

    @functools.partial(
        pl.kernel, mesh=mesh,
        out_type=jax.ShapeDtypeStruct((n, HALF_D), y.dtype),
        scratch_types=[pltpu.VMEM((2, TOP_K, C), jnp.int32),
                       pltpu.VMEM((2, TOP_K, C, lanes), jnp.uint32),
                       pltpu.VMEM((2, TOP_K, C, HALF_D), y.dtype),
                       pltpu.VMEM((2, C, HALF_D), y.dtype),
                       pltpu.SemaphoreType.DMA((2, TOP_K)),
                       pltpu.SemaphoreType.DMA((2,))],
        compiler_params=pltpu.CompilerParams(needs_layout_passes=False),
        name="sc_collect_sum")
    def run(y_hbm, dest_hbm, gate_hbm, out_hbm, idx_v, gate_v, rows_v, sum_v, gather_sem, write_sem):
        local0 = _sc_worker_id() * chunks_per_worker

        def gather_copies(slot):
            return [pltpu.make_async_copy(y_hbm.at[idx_v.at[slot, k]], rows_v.at[slot, k],
                                          gather_sem.at[slot, k]) for k in range(TOP_K)]

        def write_copy(i, slot):
            return pltpu.make_async_copy(sum_v.at[slot], out_hbm.at[pl.ds((local0 + i) * C, C)],
                                         write_sem.at[slot])

        def fetch(i, slot):
            ch = first_chunk + local0 + i
            pltpu.sync_copy(dest_hbm.at[ch], idx_v.at[slot])
            pltpu.sync_copy(gate_hbm.at[ch], gate_v.at[slot])
            for cp in gather_copies(slot):
                cp.start()

        def reduce_chunk(i, slot):
            for cp in gather_copies(slot):
                cp.wait()

            @pl.when(i >= 2)
            def _():
                write_copy(i - 2, slot).wait()

            @plsc.parallel_loop(0, C)
            def _(j):
                g = [plsc.bitcast(gate_v[slot, k, j, :], jnp.bfloat16) for k in range(TOP_K)]
                for c in range(HALF_D // lanes):
                    sl = pl.ds(c * lanes, lanes)
                    acc = None
                    for k in range(TOP_K):
                        term = plsc.bitcast(rows_v[slot, k, j, sl], jnp.bfloat16) * g[k]
                        acc = term if acc is None else acc + term
                    sum_v[slot, j, sl] = plsc.bitcast(acc, y.dtype)

            write_copy(i, slot).start()

        fetch(0, 0)

        @pl.loop(0, chunks_per_worker, step=2)
        def _(i):
            fetch(i + 1, 1)
            reduce_chunk(i, 0)

            @pl.when(i + 2 < chunks_per_worker)
            def _():
                fetch(i + 2, 0)

            reduce_chunk(i + 1, 1)

        write_copy(chunks_per_worker - 2, 0).wait()
        write_copy(chunks_per_worker - 1, 1).wait()

    return run(y, dest3, gates16)


def _combine_sum_kernel(x1_ref, ys_ref, o_ref):
    lo, hi = _unpack_halves(ys_ref[...])
    o_ref[:, :HALF_D] = x1_ref[:, :HALF_D] + lo
    o_ref[:, HALF_D:] = x1_ref[:, HALF_D:] + hi


def _combine_sum(acc, ysum, split):
    n = acc.shape[0]
    rows = PROJ_ROWS
    steps = ysum.shape[0] // rows
    first = split * steps
    return pl.pallas_call(
        _combine_sum_kernel,
        out_shape=jax.ShapeDtypeStruct((n, D_MODEL), jnp.float32),
        grid=(steps,),
        in_specs=[pl.BlockSpec((rows, D_MODEL), lambda i: (first + i, 0)),
                  pl.BlockSpec((rows, HALF_D), lambda i: (i, 0))],
        out_specs=pl.BlockSpec((rows, D_MODEL), lambda i: (first + i, 0)),
        input_output_aliases={0: 0},
        compiler_params=pltpu.CompilerParams(
            dimension_semantics=("arbitrary",), vmem_limit_bytes=VMEM_LIMIT),
        name="moe_combine",
    )(acc, ysum)


def _layer(x2, batch, seq, attn_norm_g, w_in, a_q_g, a_k_g, a_sinks, b_q_g, b_k_g, w_out,
           ffn_norm_g, w_router, b_router, w1, b1, w2, b2):
    n = x2.shape[0]
    slopes = _alibi_slopes()
    q_scale = HEAD_DIM ** -0.5 * LOG2E
    reps = MXU_DIM // HEAD_DIM
    gains = jnp.stack([jnp.tile(a_q_g, reps) * q_scale, jnp.tile(a_k_g, reps),
                       jnp.tile(b_q_g, reps) * q_scale, jnp.tile(b_k_g, reps)]).astype(jnp.float32)

    proj = _in_proj(x2, attn_norm_g.reshape(1, -1), w_in, gains, batch, seq)
    qa, ka, va = proj[:3]
    nb = len(B_DILS)
    qbs, kbs, vbs = proj[3:3 + nb], proj[3 + nb:3 + 2 * nb], proj[3 + 2 * nb:]

    bias_a = _bias_tables(slopes[:A_Q_HEADS], A_STACK_HEADS, A_HALF_WINDOW, 1, Q_TILE + 2 * A_HALF_WINDOW)
    sink_col = jnp.repeat(a_sinks.astype(jnp.float32) * LOG2E, Q_TILE).reshape(
        A_Q_HEADS // A_STACK_HEADS, A_STACK_HEADS * Q_TILE, 1)
    as_seqs = lambda a: a.reshape(batch, seq, a.shape[-1])
    out_a = _banded_attention(as_seqs(qa), as_seqs(ka), as_seqs(va), bias_a, half_w=A_HALF_WINDOW,
                              sink=sink_col, name="attn_a")[0].reshape(n, A_Q_W)

    outs_b, lses_b = [], []
    for bi, (window, dil) in enumerate(B_BRANCHES):
        half_w = window // (2 * dil)
        bias_b = _bias_tables(slopes[A_Q_HEADS:], B_STACK_HEADS, half_w, dil, Q_TILE + 2 * half_w)
        L = seq // dil
        to_seqs = lambda a: a.reshape(batch * dil, L, a.shape[-1])
        o, lse = _banded_attention(to_seqs(qbs[bi]), to_seqs(kbs[bi]), to_seqs(vbs[bi]), bias_b,
                                   half_w=half_w, want_lse=True, name=f"attn_b_d{dil}")
        if dil == 1:
            outs_b.append(o.reshape(n, B_W))
            lses_b.append(lse.reshape(n, LANES))
        else:
            outs_b.append(o.reshape(batch, dil, L, B_W))
            lses_b.append(lse.reshape(batch, dil, L, LANES))

    wr_t = w_router.T.astype(jnp.float32)
    wr_hi = wr_t.astype(jnp.bfloat16)
    wr_lo = (wr_t - wr_hi.astype(jnp.float32)).astype(jnp.bfloat16)
    wr = jnp.concatenate([wr_hi, wr_lo], axis=0)
    br = jnp.broadcast_to(b_router.astype(jnp.float32)[:, None], (N_EXPERTS, LANES))
    x1, hf, topi, gates, ranks, counts = _out_proj_router(
        out_a, outs_b, lses_b, x2, w_out, ffn_norm_g.reshape(1, -1), wr, br, seq)

    g = MOE_ROWS
    nk = n * TOP_K
    step_rows = MOE_STEP_BLOCKS * g
    n_rows = -(-(nk + N_EXPERTS * g) // step_rows) * step_rows
    cnt = counts[:, 0]
    pcnt = (cnt + g - 1) // g * g
    pends = jnp.cumsum(pcnt)
    pstarts = pends - pcnt
    experts = jnp.arange(N_EXPERTS, dtype=jnp.int32)
    start_of = jnp.sum(jnp.where(topi[:, :, None] == experts, pstarts, 0), axis=-1)
    dest = (start_of + ranks).astype(jnp.int32)
    plan = _moe_plan(pends, n_rows // g)
    dest3 = dest.reshape(TOP_K, n // SC_CHUNK, SC_CHUNK).transpose(1, 0, 2)

    xb = _sc_dispatch(hf, dest3, n_rows)
    y = _moe_experts(plan, xb, w1, b1[:, None, :], w2, b2[:, None, :])
    smallest_range = max(PROJ_ROWS, SC_COLLECT_CHUNK * SC_WORKERS)
    n_splits = max(1, min(COMBINE_SPLITS, n // smallest_range))
    per_split = n // n_splits
    chunked = lambda a: a.reshape(TOP_K, n // SC_COLLECT_CHUNK, SC_COLLECT_CHUNK).transpose(1, 0, 2)
    dest_c = chunked(dest)
    gate_bits = lax.bitcast_convert_type(gates.astype(jnp.bfloat16), jnp.uint16).astype(jnp.uint32)
    gate_words = gate_bits | (gate_bits << 16)
    gates_c = jnp.broadcast_to(chunked(gate_words)[..., None], dest_c.shape + (16,))
    chunks_per_split = per_split // SC_COLLECT_CHUNK
    out = x1
    for s in range(n_splits):
        ysum = _sc_collect_sum(y, dest_c, gates_c, s * chunks_per_split, chunks_per_split)
        out = _combine_sum(out, ysum, s)
    return out


def kernel(x, attn_norm_g, w_in, a_q_norm_g, a_k_norm_g, a_sinks, b_q_norm_g, b_k_norm_g, w_out,
           ffn_norm_g, w_router, b_router, w1, b1, w2, b2):
    batch, seq, d = x.shape
    x2 = x.reshape(batch * seq, d)
    for i in range(attn_norm_g.shape[0]):
        x2 = _layer(x2, batch, seq, attn_norm_g[i], w_in[i], a_q_norm_g[i], a_k_norm_g[i],
                    a_sinks[i], b_q_norm_g[i], b_k_norm_g[i], w_out[i], ffn_norm_g[i],
                    w_router[i], b_router[i], w1[i], b1[i], w2[i], b2[i])
    return x2.reshape(batch, seq, d)
```

```python
import functools

import jax
import jax.numpy as jnp
import numpy as np
from jax import lax
from jax.experimental import pallas as pl
from jax.experimental.pallas import tpu as pltpu
from jax.experimental.pallas import tpu_sc as plsc

D_MODEL = 1024
HALF_D = D_MODEL // 2
HEAD_DIM = 64
LANES = 128
MXU_DIM = 256
A_Q_HEADS = 8
A_KV_HEADS = 2
B_HEADS = 8
A_HALF_WINDOW = 128
B_BRANCHES = ((128, 1), (512, 4), (2048, 16))
B_DILS = tuple(d for _, d in B_BRANCHES)
RESIDUE_STRIDE = 4
N_ALIBI_HEADS = 16
A_Q_W = A_Q_HEADS * HEAD_DIM
A_KV_W = A_KV_HEADS * HEAD_DIM
B_W = B_HEADS * HEAD_DIM
N_EXPERTS = 32
TOP_K = 4
D_FF = 1024
SWIGLU_ALPHA = 1.702
SWIGLU_LIMIT = 7.0
NORM_EPS = 1e-5
MASK_VALUE = -1e30
LOG2E = 1.4426950408889634

Q_TILE = 128
ATTN_STEP_ROWS = 2048
A_STACK_HEADS = 4
B_STACK_HEADS = 2
PROJ_ROWS = 1024
IN_PROJ_ROWS = 1024
MOE_ROWS = 512
FF_CHUNK = 512
MOE_STEP_BLOCKS = 2
MOE_WEIGHT_BUFS = 3
VMEM_LIMIT = 48 * 1024 * 1024
LARGE_VMEM_LIMIT = 58 * 1024 * 1024
SC_CORES = 2
SC_SUBCORES = 16
SC_WORKERS = SC_CORES * SC_SUBCORES
SC_CHUNK = 64
SC_COLLECT_CHUNK = 16
COMBINE_SPLITS = 1


def _pack_halves(v):
    lo = v[:, :HALF_D].astype(jnp.bfloat16).astype(jnp.float32)
    hi = v[:, HALF_D:].astype(jnp.bfloat16).astype(jnp.float32)
    return (pltpu.bitcast(lo, jnp.uint32) >> 16) | pltpu.bitcast(hi, jnp.uint32)


def _unpack_halves(w):
    lo = pltpu.bitcast(w << 16, jnp.float32)
    hi = pltpu.bitcast(w & jnp.uint32(0xFFFF0000), jnp.float32)
    return lo, hi


def _alibi_slopes():
    return np.exp2(-8.0 * np.arange(1, N_ALIBI_HEADS + 1, dtype=np.float32) / N_ALIBI_HEADS).astype(np.float32)


def _bias_tables(head_slopes, heads_per_group, half_w, dist_scale, tk):
    i = np.arange(Q_TILE)[:, None]
    j = np.arange(tk)[None, :]
    tabs = []
    for shift in (0, half_w, tk - Q_TILE):
        dist = np.abs(j - shift - i)
        valid = dist <= half_w
        per_head = []
        for sl in head_slopes:
            b = (-np.float64(sl) * LOG2E * (dist * dist_scale)).astype(np.float32)
            per_head.append(np.where(valid, b, np.float32(MASK_VALUE)).astype(np.float32))
        t = np.stack(per_head).reshape(-1, heads_per_group * Q_TILE, tk)
        tabs.append(t)
    return jnp.asarray(np.stack(tabs))


def _in_proj_kernel(x_ref, g_ref, w_hbm, gains_ref, qa_ref, ka_ref, va_ref, *rest):
    b_refs, (scr_ref, scr2_ref, w_ref, stage_ref, sem) = rest[:-5], rest[-5:]

    @pl.when(pl.program_id(0) == 0)
    def _():
        width = stage_ref.shape[1]
        for c0 in range(0, w_ref.shape[1], width):
            cp = pltpu.make_async_copy(w_hbm.at[:, c0:c0 + width], stage_ref, sem)
            cp.start()
            cp.wait()
            w_ref[:, c0:c0 + width] = stage_ref[...].astype(jnp.bfloat16)

    x = x_ref[...]
    xn = x * lax.rsqrt(jnp.mean(x * x, axis=-1, keepdims=True) + NORM_EPS) * g_ref[...]
    xn = xn.astype(jnp.bfloat16)
    r = lax.broadcasted_iota(jnp.int32, (MXU_DIM, MXU_DIM), 0) // HEAD_DIM
    c = lax.broadcasted_iota(jnp.int32, (MXU_DIM, MXU_DIM), 1) // HEAD_DIM
    blockdiag = jnp.where(r == c, 1.0, 0.0).astype(jnp.bfloat16)

    def head_rms(sec, gain_row):
        width = sec.shape[1]
        parts = []
        step = min(width, MXU_DIM)
        for j in range(width // step):
            p = sec[:, j * step:(j + 1) * step]
            ss = jnp.dot((p * p).astype(jnp.bfloat16), blockdiag[:step, :step],
                         preferred_element_type=jnp.float32)
            parts.append(p * lax.rsqrt(ss * (1.0 / HEAD_DIM) + NORM_EPS)
                         * gains_ref[gain_row:gain_row + 1, :step])
        return parts

    def project(col0, width, gain_row):
        sec = jnp.dot(xn, w_ref[:, col0:col0 + width], preferred_element_type=jnp.float32)
        return [sec] if gain_row is None else head_rms(sec, gain_row)

    def store(out_ref, parts):
        w = parts[0].shape[1]
        for j, p in enumerate(parts):
            out_ref[:, j * w:(j + 1) * w] = p.astype(out_ref.dtype)

    def per_kv_head(p):
        lane = lax.broadcasted_iota(jnp.int32, p.shape, 1)
        swapped = pltpu.roll(p, HEAD_DIM, axis=1)
        low = lane < HEAD_DIM
        return [jnp.where(low, p, swapped), jnp.where(low, swapped, p)]

    store(qa_ref, project(0, A_Q_W, 0))
    kva = project(A_Q_W, 2 * A_KV_W, None)[0]
    store(ka_ref, per_kv_head(head_rms(kva[:, :A_KV_W], 1)[0]))
    store(va_ref, per_kv_head(kva[:, A_KV_W:]))

    rows = x_ref.shape[0]
    col0 = A_Q_W + 2 * A_KV_W
    for t, gain_row in enumerate((2, 3, None)):
        parts = project(col0 + t * B_W, B_W, gain_row)
        sec = jnp.concatenate(parts, axis=-1) if len(parts) > 1 else parts[0]
        for j in range(B_W // LANES):
            scr_ref[j] = sec[:, j * LANES:(j + 1) * LANES]
        n_lane_chunks = B_W // LANES
        prev_dil = 1
        for bi, dil in enumerate(B_DILS):
            out_ref = b_refs[t * len(B_DILS) + bi]
            if dil == 1:
                out_ref[...] = sec.astype(out_ref.dtype)
                continue
            assert dil == prev_dil * RESIDUE_STRIDE
            last = dil == B_DILS[-1]
            for res in range(dil):
                r_prev, r_sub = res % prev_dil, res // prev_dil
                for j in range(n_lane_chunks):
                    if prev_dil == 1:
                        v = scr_ref[j, pl.ds(r_sub, rows // dil, stride=RESIDUE_STRIDE), :]
                    else:
                        v = scr2_ref[r_prev * n_lane_chunks + j,
                                     pl.ds(r_sub, rows // dil, stride=RESIDUE_STRIDE), :]
                    out_ref[0, res, :, j * LANES:(j + 1) * LANES] = v.astype(out_ref.dtype)
                    if not last:
                        scr2_ref[res * n_lane_chunks + j] = v
            prev_dil = dil


def _in_proj(x2, g, w_in, gains, batch, seq):
    n = x2.shape[0]
    rows = IN_PROJ_ROWS
    steps = seq // rows
    a_widths = (A_Q_W, 2 * A_KV_W, 2 * A_KV_W)
    out_shape = [jax.ShapeDtypeStruct((n, w), jnp.bfloat16) for w in a_widths]
    out_specs = [pl.BlockSpec((rows, w), lambda i: (i, 0)) for w in a_widths]
    for _ in range(3):
        for dil in B_DILS:
            if dil == 1:
                out_shape.append(jax.ShapeDtypeStruct((n, B_W), jnp.bfloat16))
                out_specs.append(pl.BlockSpec((rows, B_W), lambda i: (i, 0)))
            else:
                out_shape.append(jax.ShapeDtypeStruct((batch, dil, seq // dil, B_W), jnp.bfloat16))
                out_specs.append(pl.BlockSpec((1, dil, rows // dil, B_W),
                                              lambda i: (i // steps, 0, i % steps, 0)))
    return pl.pallas_call(
        _in_proj_kernel,
        out_shape=out_shape,
        grid=(n // rows,),
        in_specs=[
            pl.BlockSpec((rows, D_MODEL), lambda i: (i, 0)),
            pl.BlockSpec((1, D_MODEL), lambda i: (0, 0)),
            pl.BlockSpec(memory_space=pl.ANY),
            pl.BlockSpec(gains.shape, lambda i: (0, 0)),
        ],
        out_specs=out_specs,
        scratch_shapes=[pltpu.VMEM((B_W // LANES, rows, LANES), jnp.float32),
                        pltpu.VMEM((RESIDUE_STRIDE * B_W // LANES, rows // RESIDUE_STRIDE, LANES),
                                   jnp.float32),
                        pltpu.VMEM(w_in.shape, jnp.bfloat16),
                        pltpu.VMEM((w_in.shape[0], w_in.shape[1] // 3), jnp.float32),
                        pltpu.SemaphoreType.DMA],
        compiler_params=pltpu.CompilerParams(
            dimension_semantics=("arbitrary",), vmem_limit_bytes=LARGE_VMEM_LIMIT),
        name="in_proj",
    )(x2, g, w_in, gains)


def _attn_kernel(*refs, n_chunks, kv_chunks, heads_per_stack, tk, half_w, seq_len, rows, has_sink,
                 want_lse):
    it = iter(refs)
    q_ref, k_ref, v_ref, bias_ref = next(it), next(it), next(it), next(it)
    sink_ref = next(it) if has_sink else None
    o_ref = next(it)
    lse_ref = next(it) if want_lse else None

    n_tiles = seq_len // Q_TILE
    tiles_per_step = rows // Q_TILE
    chunks_per_group = n_chunks // kv_chunks
    assert (2 * chunks_per_group) % heads_per_stack == 0
    step = pl.program_id(1)
    lane = lax.broadcasted_iota(jnp.int32, (Q_TILE, LANES), 1)
    low_half = lane < HEAD_DIM
    ones = jnp.ones((tk, LANES), jnp.bfloat16)

    for sq, t in [(a, b) for a in range(q_ref.shape[0]) for b in range(tiles_per_step)]:
        tile = step * tiles_per_step + t
        q0 = tile * Q_TILE
        kv_rows = k_ref.shape[1]
        kv_row0 = jnp.clip(step * rows - half_w, 0, seq_len - kv_rows)
        start = pl.multiple_of(jnp.clip(q0 - half_w, 0, seq_len - tk) - kv_row0, HEAD_DIM)
        variant = jnp.where(tile == 0, 0, jnp.where(tile == n_tiles - 1, 2, 1))
        r0 = t * Q_TILE
        lse_tile = jnp.zeros((Q_TILE, LANES), jnp.float32)
        head_o, head_lse = {}, {}
        for g in range(2 * n_chunks // heads_per_stack):
            heads = range(g * heads_per_stack, (g + 1) * heads_per_stack)
            kv = (heads[0] // 2) // chunks_per_group
            kc = k_ref[sq, pl.ds(start, tk), kv * LANES:(kv + 1) * LANES]
            vc = v_ref[sq, pl.ds(start, tk), kv * LANES:(kv + 1) * LANES]
            v_aug = jnp.concatenate([vc, ones], axis=1)
            q_parts = []
            for h in heads:
                c = h // 2
                q2 = q_ref[sq, r0:r0 + Q_TILE, c * LANES:(c + 1) * LANES]
                keep = low_half if h % 2 == 0 else ~low_half
                q_parts.append(jnp.where(keep, q2, jnp.zeros_like(q2)))
            qs = q_parts[0] if len(q_parts) == 1 else jnp.concatenate(q_parts, axis=0)
            s = lax.dot_general(qs, kc, (((1,), (1,)), ((), ())),
                                preferred_element_type=jnp.float32)
            s = s + bias_ref[variant, g]
            m = jnp.max(s, axis=-1, keepdims=True)
            if has_sink:
                m = jnp.maximum(m, sink_ref[g])
            p = jnp.exp2(s - m)
            ov = jnp.dot(p.astype(jnp.bfloat16), v_aug, preferred_element_type=jnp.float32)
            o, l = ov[:, :LANES], ov[:, LANES:]
            if has_sink:
                l = l + jnp.exp2(sink_ref[g] - m)
            o = o * (1.0 / l)
            if want_lse:
                lse = m + jnp.log(l) * LOG2E
            for idx, h in enumerate(heads):
                head_o[h] = o[idx * Q_TILE:(idx + 1) * Q_TILE]
                if want_lse:
                    head_lse[h] = lse[idx * Q_TILE:(idx + 1) * Q_TILE]
                if h % 2 == 1:
                    c = h // 2
                    o2 = jnp.where(low_half, head_o.pop(h - 1), head_o.pop(h))
                    o_ref[sq, r0:r0 + Q_TILE, c * LANES:(c + 1) * LANES] = o2.astype(o_ref.dtype)
                    if want_lse:
                        lse_tile = jnp.where(lane == h - 1, head_lse.pop(h - 1),
                                             jnp.where(lane == h, head_lse.pop(h), lse_tile))
        if want_lse:
            lse_ref[sq, r0:r0 + Q_TILE, :] = lse_tile


def _banded_attention(q, k, v, bias, *, half_w, sink=None, want_lse=False, name):
    n_seq, L, qw = q.shape
    kw = k.shape[2]
    tk = Q_TILE + 2 * half_w
    rows = min(ATTN_STEP_ROWS, L)
    seqs = ATTN_STEP_ROWS // rows
    kv_rows = min(L, rows + 2 * half_w)

    def kv_index(s, i):
        row0 = pl.multiple_of(jnp.clip(i * rows - half_w, 0, L - kv_rows), HEAD_DIM)
        return s * seqs, row0, 0

    args = [q, k, v, bias]
    in_specs = [
        pl.BlockSpec((seqs, rows, qw), lambda s, i: (s, i, 0)),
        pl.BlockSpec((pl.Element(seqs), pl.Element(kv_rows), pl.Element(kw)), kv_index),
        pl.BlockSpec((pl.Element(seqs), pl.Element(kv_rows), pl.Element(kw)), kv_index),
        pl.BlockSpec(bias.shape, lambda s, i: (0, 0, 0, 0)),
    ]
    if sink is not None:
        args.append(sink)
        in_specs.append(pl.BlockSpec(sink.shape, lambda s, i: (0, 0, 0)))
    out_shape = [jax.ShapeDtypeStruct((n_seq, L, qw), jnp.bfloat16)]
    out_specs = [pl.BlockSpec((seqs, rows, qw), lambda s, i: (s, i, 0))]
    if want_lse:
        out_shape.append(jax.ShapeDtypeStruct((n_seq, L, LANES), jnp.float32))
        out_specs.append(pl.BlockSpec((seqs, rows, LANES), lambda s, i: (s, i, 0)))

    kern = functools.partial(
        _attn_kernel, n_chunks=qw // LANES, kv_chunks=kw // LANES,
        heads_per_stack=bias.shape[2] // Q_TILE, tk=tk, half_w=half_w, seq_len=L,
        rows=rows, has_sink=sink is not None, want_lse=want_lse)
    return pl.pallas_call(
        kern,
        out_shape=out_shape,
        grid=(n_seq // seqs, L // rows),
        in_specs=in_specs,
        out_specs=out_specs,
        compiler_params=pltpu.CompilerParams(
            dimension_semantics=("arbitrary", "arbitrary"), vmem_limit_bytes=VMEM_LIMIT),
        name=name,
    )(*args)


def _out_proj_router_kernel(*refs):
    nb = len(B_DILS)
    oa_ref = refs[0]
    o_refs = refs[1:1 + nb]
    lse_refs = refs[1 + nb:1 + 2 * nb]
    (x_ref, wo_ref, g_ref, wr_ref, br_ref,
     x1_ref, hf_ref, topi_ref, gate_ref, rank_ref, cnt_ref,
     tri_ref, carry_ref, so_ref, sl_ref, wo_bf_ref) = refs[1 + 2 * nb:]
    i = pl.program_id(0)
    rows = x_ref.shape[0]

    @pl.when(i == 0)
    def _():
        a = lax.broadcasted_iota(jnp.int32, (rows, rows), 0)
        b = lax.broadcasted_iota(jnp.int32, (rows, rows), 1)
        tri_ref[...] = jnp.where(a <= b, 1.0, 0.0).astype(jnp.bfloat16)
        carry_ref[...] = jnp.zeros_like(carry_ref)
        wo_bf_ref[...] = wo_ref[...].astype(jnp.bfloat16)

    outs, lses = [], []
    for bi, dil in enumerate(B_DILS):
        if dil == 1:
            outs.append(o_refs[bi][...].astype(jnp.float32))
            lses.append(lse_refs[bi][...])
        else:
            for res in range(dil):
                for j in range(B_W // LANES):
                    so_ref[bi, j, pl.ds(res, rows // dil, stride=dil), :] = (
                        o_refs[bi][0, res, :, j * LANES:(j + 1) * LANES].astype(jnp.float32))
                sl_ref[bi, pl.ds(res, rows // dil, stride=dil), :] = lse_refs[bi][0, res]
            outs.append(jnp.concatenate([so_ref[bi, j] for j in range(B_W // LANES)], axis=-1))
            lses.append(sl_ref[bi])

    mx = functools.reduce(jnp.maximum, lses)
    es = [jnp.exp2(l - mx) for l in lses]
    inv = 1.0 / functools.reduce(lambda a, b: a + b, es)
    eh = lax.broadcasted_iota(jnp.int32, (LANES, B_W), 0)
    ej = lax.broadcasted_iota(jnp.int32, (LANES, B_W), 1) // HEAD_DIM
    expand = jnp.where(eh == ej, 1.0, 0.0).astype(jnp.bfloat16)
    ob = jnp.zeros((rows, B_W), jnp.float32)
    for e, o in zip(es, outs):
        w = e * inv
        wide = jnp.dot(w.astype(jnp.bfloat16), expand, preferred_element_type=jnp.float32)
        ob = ob + wide * o

    attn = jnp.concatenate([oa_ref[...], ob.astype(jnp.bfloat16)], axis=-1)
    x1 = x_ref[...] + jnp.dot(attn, wo_bf_ref[...], preferred_element_type=jnp.float32)
    x1_ref[...] = x1
    hf = x1 * lax.rsqrt(jnp.mean(x1 * x1, axis=-1, keepdims=True) + NORM_EPS) * g_ref[...]
    hf_hi = hf.astype(jnp.bfloat16)
    hf_ref[...] = _pack_halves(hf)
    hf_lo = (hf - hf_hi.astype(jnp.float32)).astype(jnp.bfloat16)

    nt = (((1,), (1,)), ((), ()))
    lg_hi = lax.dot_general(wr_ref[...], hf_hi, nt, preferred_element_type=jnp.float32)
    lg_lo = lax.dot_general(wr_ref[0:N_EXPERTS, :], hf_lo, nt, preferred_element_type=jnp.float32)
    logits = lg_hi[0:N_EXPERTS] + lg_hi[N_EXPERTS:] + lg_lo + br_ref[:, 0:1]


    eidx = lax.broadcasted_iota(jnp.int32, (N_EXPERTS, rows), 0)
    work = logits
    vals, sels = [], []
    for k in range(TOP_K):
        mk = jnp.max(work, axis=0, keepdims=True)
        ik = jnp.min(jnp.where(work == mk, eidx, N_EXPERTS), axis=0, keepdims=True)
        sel = eidx == ik
        work = jnp.where(sel, -jnp.inf, work)
        vals.append(mk)
        sels.append(sel)
        topi_ref[k:k + 1, :] = ik
    exps = [jnp.exp(vk - vals[0]) for vk in vals]
    denom = exps[0] + exps[1] + exps[2] + exps[3]
    ginv = 1.0 / denom
    for k in range(TOP_K):
        gate_ref[k:k + 1, :] = exps[k] * ginv

    onehot = jnp.zeros((N_EXPERTS, rows), jnp.float32)
    for sel in sels:
        onehot = onehot + jnp.where(sel, 1.0, 0.0)
    incl = jnp.dot(onehot.astype(jnp.bfloat16), tri_ref[...], preferred_element_type=jnp.float32)
    before = incl - onehot + carry_ref[:, 0:1]
    for k in range(TOP_K):
        rk = jnp.sum(jnp.where(sels[k], before, 0.0), axis=0, keepdims=True)
        rank_ref[k:k + 1, :] = rk.astype(jnp.int32)
    carry = carry_ref[...] + jnp.sum(onehot, axis=1, keepdims=True)
    carry_ref[...] = carry
    cnt_ref[...] = carry.astype(jnp.int32)


def _out_proj_router(oa, outs_b, lses_b, x2, wo_bf, g, wr, br, seq):
    n = x2.shape[0]
    rows = PROJ_ROWS
    steps = seq // rows
    row_spec = lambda w: pl.BlockSpec((rows, w), lambda i: (i, 0))
    full = lambda a: pl.BlockSpec(a.shape, lambda i: (0,) * a.ndim)
    col_spec = pl.BlockSpec((TOP_K, rows), lambda i: (0, i))

    def branch_spec(dil, w):
        if dil == 1:
            return row_spec(w)
        return pl.BlockSpec((1, dil, rows // dil, w), lambda i: (i // steps, 0, i % steps, 0))

    in_specs = ([row_spec(A_Q_W)]
                + [branch_spec(d, B_W) for d in B_DILS]
                + [branch_spec(d, LANES) for d in B_DILS]
                + [row_spec(D_MODEL), full(wo_bf), full(g), full(wr), full(br)])
    return pl.pallas_call(
        _out_proj_router_kernel,
        out_shape=[
            jax.ShapeDtypeStruct((n, D_MODEL), jnp.float32),
            jax.ShapeDtypeStruct((n, HALF_D), jnp.uint32),
            jax.ShapeDtypeStruct((TOP_K, n), jnp.int32),
            jax.ShapeDtypeStruct((TOP_K, n), jnp.float32),
            jax.ShapeDtypeStruct((TOP_K, n), jnp.int32),
            jax.ShapeDtypeStruct((N_EXPERTS, LANES), jnp.int32),
        ],
        grid=(n // rows,),
        in_specs=in_specs,
        out_specs=[row_spec(D_MODEL), row_spec(HALF_D), col_spec, col_spec, col_spec,
                   pl.BlockSpec((N_EXPERTS, LANES), lambda i: (0, 0))],
        scratch_shapes=[pltpu.VMEM((rows, rows), jnp.bfloat16),
                        pltpu.VMEM((N_EXPERTS, LANES), jnp.float32),
                        pltpu.VMEM((len(B_DILS), B_W // LANES, rows, LANES), jnp.float32),
                        pltpu.VMEM((len(B_DILS), rows, LANES), jnp.float32),
                        pltpu.VMEM(wo_bf.shape, jnp.bfloat16)],
        compiler_params=pltpu.CompilerParams(
            dimension_semantics=("arbitrary",), vmem_limit_bytes=LARGE_VMEM_LIMIT),
        name="out_proj_router",
    )(oa, *outs_b, *lses_b, x2, wo_bf, g, wr, br)


def _mxu_dot(a_bf, w_f32):
    return lax.dot_general(a_bf, w_f32, (((1,), (0,)), ((), ())), preferred_element_type=jnp.float32)


def _moe_kernel(blk_exp_ref, first_ref, slot_ref, next_exp_ref, n_used_ref,
                x_ref, w1_hbm, b1_ref, w2_hbm, b2_ref, y_ref, w1_buf, w2_buf, sem):
    step = pl.program_id(0)

    def weight_copies(expert, slot):
        return (pltpu.make_async_copy(w1_hbm.at[expert], w1_buf.at[slot], sem.at[slot, 0]),
                pltpu.make_async_copy(w2_hbm.at[expert], w2_buf.at[slot], sem.at[slot, 1]))

    @pl.when(step * MOE_STEP_BLOCKS < n_used_ref[0])
    def _():
        @pl.when(step == 0)
        def _():
            for cp in weight_copies(blk_exp_ref[0], slot_ref[0]):
                cp.start()

        for j in range(MOE_STEP_BLOCKS):
            i = step * MOE_STEP_BLOCKS + j

            @pl.when(first_ref[i] == 1)
            def _():
                slot = slot_ref[i]
                for cp in weight_copies(blk_exp_ref[i], slot):
                    cp.wait()

                @pl.when(next_exp_ref[i] >= 0)
                def _():
                    for cp in weight_copies(next_exp_ref[i], (slot + 1) % MOE_WEIGHT_BUFS):
                        cp.start()

        for j in range(MOE_STEP_BLOCKS):
            i = step * MOE_STEP_BLOCKS + j
            slot = slot_ref[i]
            expert = blk_exp_ref[i]
            rs = slice(j * MOE_ROWS, (j + 1) * MOE_ROWS)
            x = jnp.concatenate(_unpack_halves(x_ref[rs, :]), axis=-1).astype(jnp.bfloat16)
            acc = jnp.zeros((MOE_ROWS, D_MODEL), jnp.float32)
            for c in range(D_FF // FF_CHUNK):
                lo = c * FF_CHUNK
                glu = _mxu_dot(x, w1_buf[slot, :, lo:lo + FF_CHUNK]) + b1_ref[expert, :, lo:lo + FF_CHUNK]
                lin = (_mxu_dot(x, w1_buf[slot, :, D_FF + lo:D_FF + lo + FF_CHUNK])
                       + b1_ref[expert, :, D_FF + lo:D_FF + lo + FF_CHUNK])
                glu = jnp.minimum(glu, SWIGLU_LIMIT)
                lin = jnp.clip(lin, -SWIGLU_LIMIT, SWIGLU_LIMIT)
                act = glu * (1.0 / (1.0 + jnp.exp(-SWIGLU_ALPHA * glu))) * (lin + 1.0)
                acc = acc + _mxu_dot(act.astype(jnp.bfloat16), w2_buf[slot, lo:lo + FF_CHUNK, :])
            y_ref[rs, :] = _pack_halves(acc + b2_ref[expert])


def _moe_plan(pends, n_blk):
    g = MOE_ROWS
    blk_row0 = jnp.arange(n_blk, dtype=jnp.int32) * g
    blk_exp = jnp.minimum(jnp.sum(pends[None, :] <= blk_row0[:, None], axis=-1),
                          N_EXPERTS - 1).astype(jnp.int32)
    n_used = (pends[-1] // g).astype(jnp.int32)
    used = blk_row0 < pends[-1]
    prev_exp = jnp.concatenate([jnp.full((1,), -1, jnp.int32), blk_exp[:-1]])
    first = (used & (blk_exp != prev_exp)).astype(jnp.int32)
    slot = ((jnp.cumsum(first) - 1) % MOE_WEIGHT_BUFS).astype(jnp.int32)
    pstarts = jnp.concatenate([jnp.zeros((1,), pends.dtype), pends[:-1]])
    nonempty = pends > pstarts
    experts = jnp.arange(N_EXPERTS, dtype=jnp.int32)
    later = nonempty[None, :] & (experts[None, :] > experts[:, None])
    next_nonempty = jnp.min(jnp.where(later, experts[None, :], N_EXPERTS), axis=-1)
    next_nonempty = jnp.where(next_nonempty == N_EXPERTS, -1, next_nonempty).astype(jnp.int32)
    next_exp = jnp.sum(jnp.where(blk_exp[:, None] == experts[None, :], next_nonempty[None, :], 0),
                       axis=-1).astype(jnp.int32)
    return blk_exp, first, slot, next_exp, n_used.reshape(1)


def _moe_experts(plan, xb, w1, b1, w2, b2):
    n_rows = xb.shape[0]
    n_blk = n_rows // MOE_ROWS

    step_rows = MOE_STEP_BLOCKS * MOE_ROWS

    def blk(i, *p):
        return jnp.minimum(i, (p[-1][0] - 1) // MOE_STEP_BLOCKS)

    grid_spec = pltpu.PrefetchScalarGridSpec(
        num_scalar_prefetch=len(plan),
        grid=(n_blk // MOE_STEP_BLOCKS,),
        in_specs=[
            pl.BlockSpec((step_rows, HALF_D), lambda i, *p: (blk(i, *p), 0)),
            pl.BlockSpec(memory_space=pl.ANY),
            pl.BlockSpec(b1.shape, lambda i, *p: (0, 0, 0)),
            pl.BlockSpec(memory_space=pl.ANY),
            pl.BlockSpec(b2.shape, lambda i, *p: (0, 0, 0)),
        ],
        out_specs=pl.BlockSpec((step_rows, HALF_D), lambda i, *p: (blk(i, *p), 0)),
        scratch_shapes=[pltpu.VMEM((MOE_WEIGHT_BUFS, D_MODEL, 2 * D_FF), jnp.float32),
                        pltpu.VMEM((MOE_WEIGHT_BUFS, D_FF, D_MODEL), jnp.float32),
                        pltpu.SemaphoreType.DMA((MOE_WEIGHT_BUFS, 2))],
    )
    return pl.pallas_call(
        _moe_kernel,
        out_shape=jax.ShapeDtypeStruct((n_rows, HALF_D), jnp.uint32),
        grid_spec=grid_spec,
        compiler_params=pltpu.CompilerParams(
            dimension_semantics=("arbitrary",), vmem_limit_bytes=LARGE_VMEM_LIMIT),
        name="moe_experts",
    )(*plan, xb, w1, b1, w2, b2)


def _combine_kernel(x1_ref, yg_ref, gate_ref, o_ref):
    acc_lo = x1_ref[:, :HALF_D]
    acc_hi = x1_ref[:, HALF_D:]
    rows = x1_ref.shape[0]
    gates = jnp.concatenate([gate_ref[...], jnp.zeros((8 - TOP_K, rows), jnp.float32)], axis=0).T
    for k in range(TOP_K):
        lo, hi = _unpack_halves(yg_ref[k])
        gk = gates[:, k:k + 1]
        acc_lo = acc_lo + lo * gk
        acc_hi = acc_hi + hi * gk
    o_ref[:, :HALF_D] = acc_lo
    o_ref[:, HALF_D:] = acc_hi


def _combine(acc, yg, gates_nk, split):
    n = acc.shape[0]
    rows = PROJ_ROWS
    steps = yg.shape[1] // rows
    first = split * steps
    return pl.pallas_call(
        _combine_kernel,
        out_shape=jax.ShapeDtypeStruct((n, D_MODEL), jnp.float32),
        grid=(steps,),
        in_specs=[pl.BlockSpec((rows, D_MODEL), lambda i: (first + i, 0)),
                  pl.BlockSpec((TOP_K, rows, HALF_D), lambda i: (0, i, 0)),
                  pl.BlockSpec((TOP_K, rows), lambda i: (0, first + i))],
        out_specs=pl.BlockSpec((rows, D_MODEL), lambda i: (first + i, 0)),
        input_output_aliases={0: 0},
        compiler_params=pltpu.CompilerParams(
            dimension_semantics=("arbitrary",), vmem_limit_bytes=VMEM_LIMIT),
        name="moe_combine",
    )(acc, yg, gates_nk)


def _sc_worker_id():
    return lax.axis_index("s") * SC_CORES + lax.axis_index("c")


def _sc_dispatch(hf, dest3, n_rows):
    n = hf.shape[0]
    chunks_per_worker = n // SC_CHUNK // SC_WORKERS
    mesh = plsc.VectorSubcoreMesh(core_axis_name="c", subcore_axis_name="s")

    @functools.partial(
        pl.kernel, mesh=mesh,
        out_type=jax.ShapeDtypeStruct((n_rows, HALF_D), hf.dtype),
        scratch_types=[pltpu.VMEM((2, TOP_K, SC_CHUNK), jnp.int32),
                       pltpu.VMEM((2, SC_CHUNK, HALF_D), hf.dtype),
                       pltpu.SemaphoreType.DMA((2,)),
                       pltpu.SemaphoreType.DMA((2,))],
        name="sc_dispatch")
    def run(hf_hbm, dest_hbm, xb_hbm, idx_v, rows_v, load_sem, scatter_sem):
        first = _sc_worker_id() * chunks_per_worker

        def load(j):
            slot = j % 2
            pltpu.sync_copy(dest_hbm.at[first + j], idx_v.at[slot])
            return pltpu.async_copy(hf_hbm.at[pl.ds((first + j) * SC_CHUNK, SC_CHUNK)],
                                    rows_v.at[slot], load_sem.at[slot])

        loads = {0: load(0)}
        scatters = {}
        for j in range(chunks_per_worker):
            slot = j % 2
            loads.pop(j).wait()
            scatters[j] = [pltpu.async_copy(rows_v.at[slot], xb_hbm.at[idx_v.at[slot, k]],
                                            scatter_sem.at[slot]) for k in range(TOP_K)]
            if j >= 1:
                for cp in scatters.pop(j - 1):
                    cp.wait()
            if j + 1 < chunks_per_worker:
                loads[j + 1] = load(j + 1)
        for cp in scatters.pop(chunks_per_worker - 1):
            cp.wait()

    return run(hf, dest3)


def _sc_collect(y, dest3):
    n = dest3.shape[0] * SC_COLLECT_CHUNK
    chunks_per_worker = n // SC_COLLECT_CHUNK // SC_WORKERS
    mesh = plsc.VectorSubcoreMesh(core_axis_name="c", subcore_axis_name="s")

    @functools.partial(
        pl.kernel, mesh=mesh,
        out_type=jax.ShapeDtypeStruct((TOP_K, n, HALF_D), y.dtype),
        scratch_types=[pltpu.VMEM((TOP_K, SC_COLLECT_CHUNK), jnp.int32),
                       pltpu.VMEM((TOP_K, SC_COLLECT_CHUNK, HALF_D), y.dtype),
                       pltpu.SemaphoreType.DMA((TOP_K,)),
                       pltpu.SemaphoreType.DMA((TOP_K,))],
        name="sc_collect")
    def run(y_hbm, dest_hbm, yg_hbm, idx_v, rows_v, gather_sem, write_sem):
        first = _sc_worker_id() * chunks_per_worker

        @pl.loop(0, chunks_per_worker)
        def _(j):
            ch = first + j
            pltpu.sync_copy(dest_hbm.at[ch], idx_v)
            gathers = [pltpu.async_copy(y_hbm.at[idx_v.at[k]], rows_v.at[k], gather_sem.at[k])
                       for k in range(TOP_K)]
            writes = []
            for k in range(TOP_K):
                gathers[k].wait()
                writes.append(pltpu.async_copy(
                    rows_v.at[k], yg_hbm.at[k, pl.ds(ch * SC_COLLECT_CHUNK, SC_COLLECT_CHUNK)],
                    write_sem.at[k]))
            for w in writes:
                w.wait()

    return run(y, dest3)


def _sc_collect_sum(y, dest3, gates16, first_chunk, n_chunks):
    C = SC_COLLECT_CHUNK
    n = n_chunks * C
    chunks_per_worker = n_chunks // SC_WORKERS
    lanes = 16
    mesh = plsc.VectorSubcoreMesh(core_axis_name="c", subcore_axis_name="s")

    @functools.partial(
        pl.kernel, mesh=mesh,
        out_type=jax.ShapeDtypeStruct((n, HALF_D), y.dtype),
        scratch_types=[pltpu.VMEM((2, TOP_K, C), jnp.int32),
                       pltpu.VMEM((2, TOP_K, C), jnp.uint32),
                       pltpu.VMEM((2, TOP_K, C, HALF_D), y.dtype),
                       pltpu.VMEM((2, C, HALF_D), y.dtype),
                       pltpu.SemaphoreType.DMA((2, TOP_K)),
                       pltpu.SemaphoreType.DMA((2,))],
        compiler_params=pltpu.CompilerParams(needs_layout_passes=False),
        name="sc_collect_sum")
    def run(y_hbm, dest_hbm, gate_hbm, out_hbm, idx_v, gate_v, rows_v, sum_v, gather_sem, write_sem):
        local0 = _sc_worker_id() * chunks_per_worker

        def gather_copies(slot):
            return [pltpu.make_async_copy(y_hbm.at[idx_v.at[slot, k]], rows_v.at[slot, k],
                                          gather_sem.at[slot, k]) for k in range(TOP_K)]

        def write_copy(i, slot):
            return pltpu.make_async_copy(sum_v.at[slot], out_hbm.at[pl.ds((local0 + i) * C, C)],
                                         write_sem.at[slot])

        def fetch(i, slot):
            ch = first_chunk + local0 + i
            pltpu.sync_copy(dest_hbm.at[ch], idx_v.at[slot])
            pltpu.sync_copy(gate_hbm.at[ch], gate_v.at[slot])
            for cp in gather_copies(slot):
                cp.start()

        def reduce_chunk(i, slot):
            for cp in gather_copies(slot):
                cp.wait()

            @pl.when(i >= 2)
            def _():
                write_copy(i - 2, slot).wait()

            @plsc.parallel_loop(0, C)
            def _(j):
                lane_j = jnp.full((lanes,), j, jnp.int32)
                g = [plsc.bitcast(gate_v[slot, k, :].at[lane_j].get(mode="promise_in_bounds"),
                                  jnp.bfloat16) for k in range(TOP_K)]
                for c in range(HALF_D // lanes):
                    sl = pl.ds(c * lanes, lanes)
                    acc = None
                    for k in range(TOP_K):
                        term = plsc.bitcast(rows_v[slot, k, j, sl], jnp.bfloat16) * g[k]
                        acc = term if acc is None else acc + term
                    sum_v[slot, j, sl] = plsc.bitcast(acc, y.dtype)

            write_copy(i, slot).start()

        fetch(0, 0)

        @pl.loop(0, chunks_per_worker, step=2)
        def _(i):
            fetch(i + 1, 1)
            reduce_chunk(i, 0)

            @pl.when(i + 2 < chunks_per_worker)
            def _():
                fetch(i + 2, 0)

            reduce_chunk(i + 1, 1)

        write_copy(chunks_per_worker - 2, 0).wait()
        write_copy(chunks_per_worker - 1, 1).wait()

    return run(y, dest3, gates16)


def _combine_sum_kernel(x1_ref, ys_ref, o_ref):
    lo, hi = _unpack_halves(ys_ref[...])
    o_ref[:, :HALF_D] = x1_ref[:, :HALF_D] + lo
    o_ref[:, HALF_D:] = x1_ref[:, HALF_D:] + hi


def _combine_sum(acc, ysum, split):
    n = acc.shape[0]
    rows = PROJ_ROWS
    steps = ysum.shape[0] // rows
    first = split * steps
    return pl.pallas_call(
        _combine_sum_kernel,
        out_shape=jax.ShapeDtypeStruct((n, D_MODEL), jnp.float32),
        grid=(steps,),
        in_specs=[pl.BlockSpec((rows, D_MODEL), lambda i: (first + i, 0)),
                  pl.BlockSpec((rows, HALF_D), lambda i: (i, 0))],
        out_specs=pl.BlockSpec((rows, D_MODEL), lambda i: (first + i, 0)),
        input_output_aliases={0: 0},
        compiler_params=pltpu.CompilerParams(
            dimension_semantics=("arbitrary",), vmem_limit_bytes=VMEM_LIMIT),
        name="moe_combine",
    )(acc, ysum)


def _layer(x2, batch, seq, attn_norm_g, w_in, a_q_g, a_k_g, a_sinks, b_q_g, b_k_g, w_out,
           ffn_norm_g, w_router, b_router, w1, b1, w2, b2):
    n = x2.shape[0]
    slopes = _alibi_slopes()
    q_scale = HEAD_DIM ** -0.5 * LOG2E
    reps = MXU_DIM // HEAD_DIM
    gains = jnp.stack([jnp.tile(a_q_g, reps) * q_scale, jnp.tile(a_k_g, reps),
                       jnp.tile(b_q_g, reps) * q_scale, jnp.tile(b_k_g, reps)]).astype(jnp.float32)

    proj = _in_proj(x2, attn_norm_g.reshape(1, -1), w_in, gains, batch, seq)
    qa, ka, va = proj[:3]
    nb = len(B_DILS)
    qbs, kbs, vbs = proj[3:3 + nb], proj[3 + nb:3 + 2 * nb], proj[3 + 2 * nb:]

    bias_a = _bias_tables(slopes[:A_Q_HEADS], A_STACK_HEADS, A_HALF_WINDOW, 1, Q_TILE + 2 * A_HALF_WINDOW)
    sink_col = jnp.repeat(a_sinks.astype(jnp.float32) * LOG2E, Q_TILE).reshape(
        A_Q_HEADS // A_STACK_HEADS, A_STACK_HEADS * Q_TILE, 1)
    as_seqs = lambda a: a.reshape(batch, seq, a.shape[-1])
    out_a = _banded_attention(as_seqs(qa), as_seqs(ka), as_seqs(va), bias_a, half_w=A_HALF_WINDOW,
                              sink=sink_col, name="attn_a")[0].reshape(n, A_Q_W)

    outs_b, lses_b = [], []
    for bi, (window, dil) in enumerate(B_BRANCHES):
        half_w = window // (2 * dil)
        bias_b = _bias_tables(slopes[A_Q_HEADS:], B_STACK_HEADS, half_w, dil, Q_TILE + 2 * half_w)
        L = seq // dil
        to_seqs = lambda a: a.reshape(batch * dil, L, a.shape[-1])
        o, lse = _banded_attention(to_seqs(qbs[bi]), to_seqs(kbs[bi]), to_seqs(vbs[bi]), bias_b,
                                   half_w=half_w, want_lse=True, name=f"attn_b_d{dil}")
        if dil == 1:
            outs_b.append(o.reshape(n, B_W))
            lses_b.append(lse.reshape(n, LANES))
        else:
            outs_b.append(o.reshape(batch, dil, L, B_W))
            lses_b.append(lse.reshape(batch, dil, L, LANES))

    wr_t = w_router.T.astype(jnp.float32)
    wr_hi = wr_t.astype(jnp.bfloat16)
    wr_lo = (wr_t - wr_hi.astype(jnp.float32)).astype(jnp.bfloat16)
    wr = jnp.concatenate([wr_hi, wr_lo], axis=0)
    br = jnp.broadcast_to(b_router.astype(jnp.float32)[:, None], (N_EXPERTS, LANES))
    x1, hf, topi, gates, ranks, counts = _out_proj_router(
        out_a, outs_b, lses_b, x2, w_out, ffn_norm_g.reshape(1, -1), wr, br, seq)

    g = MOE_ROWS
    nk = n * TOP_K
    step_rows = MOE_STEP_BLOCKS * g
    n_rows = -(-(nk + N_EXPERTS * g) // step_rows) * step_rows
    cnt = counts[:, 0]
    pcnt = (cnt + g - 1) // g * g
    pends = jnp.cumsum(pcnt)
    pstarts = pends - pcnt
    experts = jnp.arange(N_EXPERTS, dtype=jnp.int32)
    start_of = jnp.sum(jnp.where(topi[:, :, None] == experts, pstarts, 0), axis=-1)
    dest = (start_of + ranks).astype(jnp.int32)
    plan = _moe_plan(pends, n_rows // g)
    dest3 = dest.reshape(TOP_K, n // SC_CHUNK, SC_CHUNK).transpose(1, 0, 2)

    xb = _sc_dispatch(hf, dest3, n_rows)
    y = _moe_experts(plan, xb, w1, b1[:, None, :], w2, b2[:, None, :])
    smallest_range = max(PROJ_ROWS, SC_COLLECT_CHUNK * SC_WORKERS)
    n_splits = max(1, min(COMBINE_SPLITS, n // smallest_range))
    per_split = n // n_splits
    chunked = lambda a: a.reshape(TOP_K, n // SC_COLLECT_CHUNK, SC_COLLECT_CHUNK).transpose(1, 0, 2)
    dest_c = chunked(dest)
    gate_bits = lax.bitcast_convert_type(gates.astype(jnp.bfloat16), jnp.uint16).astype(jnp.uint32)
    gate_words = gate_bits | (gate_bits << 16)
    gates_c = chunked(gate_words)
    chunks_per_split = per_split // SC_COLLECT_CHUNK
    out = x1
    for s in range(n_splits):
        ysum = _sc_collect_sum(y, dest_c, gates_c, s * chunks_per_split, chunks_per_split)
        out = _combine_sum(out, ysum, s)
    return out


def kernel(x, attn_norm_g, w_in, a_q_norm_g, a_k_norm_g, a_sinks, b_q_norm_g, b_k_norm_g, w_out,
           ffn_norm_g, w_router, b_router, w1, b1, w2, b2):
    batch, seq, d = x.shape
    x2 = x.reshape(batch * seq, d)
    for i in range(attn_norm_g.shape[0]):
        x2 = _layer(x2, batch, seq, attn_norm_g[i], w_in[i], a_q_norm_g[i], a_k_norm_g[i],
                    a_sinks[i], b_q_norm_g[i], b_k_norm_g[i], w_out[i], ffn_norm_g[i],
                    w_router[i], b_router[i], w1[i], b1[i], w2[i], b2[i])
    return x2.reshape(batch, seq, d)
```

```python
import functools

import jax
import jax.numpy as jnp
import numpy as np
from jax import lax
from jax.experimental import pallas as pl
from jax.experimental.pallas import tpu as pltpu
from jax.experimental.pallas import tpu_sc as plsc

D_MODEL = 1024
HALF_D = D_MODEL // 2
HEAD_DIM = 64
LANES = 128
MXU_DIM = 256
A_Q_HEADS = 8
A_KV_HEADS = 2
B_HEADS = 8
A_HALF_WINDOW = 128
B_BRANCHES = ((128, 1), (512, 4), (2048, 16))
B_DILS = tuple(d for _, d in B_BRANCHES)
RESIDUE_STRIDE = 4
N_ALIBI_HEADS = 16
A_Q_W = A_Q_HEADS * HEAD_DIM
A_KV_W = A_KV_HEADS * HEAD_DIM
B_W = B_HEADS * HEAD_DIM
N_EXPERTS = 32
TOP_K = 4
D_FF = 1024
SWIGLU_ALPHA = 1.702
SWIGLU_LIMIT = 7.0
NORM_EPS = 1e-5
MASK_VALUE = -1e30
LOG2E = 1.4426950408889634

Q_TILE = 128
ATTN_STEP_ROWS = 2048
A_STACK_HEADS = 4
B_STACK_HEADS = 2
PROJ_ROWS = 1024
IN_PROJ_ROWS = 1024
MOE_ROWS = 512
FF_CHUNK = 512
MOE_STEP_BLOCKS = 2
MOE_WEIGHT_BUFS = 3
VMEM_LIMIT = 48 * 1024 * 1024
LARGE_VMEM_LIMIT = 58 * 1024 * 1024
SC_CORES = 2
SC_SUBCORES = 16
SC_WORKERS = SC_CORES * SC_SUBCORES
SC_CHUNK = 64
SC_COLLECT_CHUNK = 16
COMBINE_SPLITS = 2


def _pack_halves(v):
    lo = v[:, :HALF_D].astype(jnp.bfloat16).astype(jnp.float32)
    hi = v[:, HALF_D:].astype(jnp.bfloat16).astype(jnp.float32)
    return (pltpu.bitcast(lo, jnp.uint32) >> 16) | pltpu.bitcast(hi, jnp.uint32)


def _unpack_halves(w):
    lo = pltpu.bitcast(w << 16, jnp.float32)
    hi = pltpu.bitcast(w & jnp.uint32(0xFFFF0000), jnp.float32)
    return lo, hi


def _alibi_slopes():
    return np.exp2(-8.0 * np.arange(1, N_ALIBI_HEADS + 1, dtype=np.float32) / N_ALIBI_HEADS).astype(np.float32)


def _bias_tables(head_slopes, heads_per_group, half_w, dist_scale, tk):
    i = np.arange(Q_TILE)[:, None]
    j = np.arange(tk)[None, :]
    tabs = []
    for shift in (0, half_w, tk - Q_TILE):
        dist = np.abs(j - shift - i)
        valid = dist <= half_w
        per_head = []
        for sl in head_slopes:
            b = (-np.float64(sl) * LOG2E * (dist * dist_scale)).astype(np.float32)
            per_head.append(np.where(valid, b, np.float32(MASK_VALUE)).astype(np.float32))
        t = np.stack(per_head).reshape(-1, heads_per_group * Q_TILE, tk)
        tabs.append(t)
    return jnp.asarray(np.stack(tabs))


def _in_proj_kernel(x_ref, g_ref, w_hbm, gains_ref, qa_ref, ka_ref, va_ref, *rest):
    b_refs, (scr_ref, scr2_ref, w_ref, stage_ref, sem) = rest[:-5], rest[-5:]

    @pl.when(pl.program_id(0) == 0)
    def _():
        width = stage_ref.shape[1]
        for c0 in range(0, w_ref.shape[1], width):
            cp = pltpu.make_async_copy(w_hbm.at[:, c0:c0 + width], stage_ref, sem)
            cp.start()
            cp.wait()
            w_ref[:, c0:c0 + width] = stage_ref[...].astype(jnp.bfloat16)

    x = x_ref[...]
    xn = x * lax.rsqrt(jnp.mean(x * x, axis=-1, keepdims=True) + NORM_EPS) * g_ref[...]
    xn = xn.astype(jnp.bfloat16)
    r = lax.broadcasted_iota(jnp.int32, (MXU_DIM, MXU_DIM), 0) // HEAD_DIM
    c = lax.broadcasted_iota(jnp.int32, (MXU_DIM, MXU_DIM), 1) // HEAD_DIM
    blockdiag = jnp.where(r == c, 1.0, 0.0).astype(jnp.bfloat16)

    def head_rms(sec, gain_row):
        width = sec.shape[1]
        parts = []
        step = min(width, MXU_DIM)
        for j in range(width // step):
            p = sec[:, j * step:(j + 1) * step]
            ss = jnp.dot((p * p).astype(jnp.bfloat16), blockdiag[:step, :step],
                         preferred_element_type=jnp.float32)
            parts.append(p * lax.rsqrt(ss * (1.0 / HEAD_DIM) + NORM_EPS)
                         * gains_ref[gain_row:gain_row + 1, :step])
        return parts

    def project(col0, width, gain_row):
        sec = jnp.dot(xn, w_ref[:, col0:col0 + width], preferred_element_type=jnp.float32)
        return [sec] if gain_row is None else head_rms(sec, gain_row)

    def store(out_ref, parts):
        w = parts[0].shape[1]
        for j, p in enumerate(parts):
            out_ref[:, j * w:(j + 1) * w] = p.astype(out_ref.dtype)

    def per_kv_head(p):
        lane = lax.broadcasted_iota(jnp.int32, p.shape, 1)
        swapped = pltpu.roll(p, HEAD_DIM, axis=1)
        low = lane < HEAD_DIM
        return [jnp.where(low, p, swapped), jnp.where(low, swapped, p)]

    store(qa_ref, project(0, A_Q_W, 0))
    kva = project(A_Q_W, 2 * A_KV_W, None)[0]
    store(ka_ref, per_kv_head(head_rms(kva[:, :A_KV_W], 1)[0]))
    store(va_ref, per_kv_head(kva[:, A_KV_W:]))

    rows = x_ref.shape[0]
    col0 = A_Q_W + 2 * A_KV_W
    for t, gain_row in enumerate((2, 3, None)):
        parts = project(col0 + t * B_W, B_W, gain_row)
        sec = jnp.concatenate(parts, axis=-1) if len(parts) > 1 else parts[0]
        for j in range(B_W // LANES):
            scr_ref[j] = sec[:, j * LANES:(j + 1) * LANES]
        n_lane_chunks = B_W // LANES
        prev_dil = 1
        for bi, dil in enumerate(B_DILS):
            out_ref = b_refs[t * len(B_DILS) + bi]
            if dil == 1:
                out_ref[...] = sec.astype(out_ref.dtype)
                continue
            assert dil == prev_dil * RESIDUE_STRIDE
            last = dil == B_DILS[-1]
            for res in range(dil):
                r_prev, r_sub = res % prev_dil, res // prev_dil
                for j in range(n_lane_chunks):
                    if prev_dil == 1:
                        v = scr_ref[j, pl.ds(r_sub, rows // dil, stride=RESIDUE_STRIDE), :]
                    else:
                        v = scr2_ref[r_prev * n_lane_chunks + j,
                                     pl.ds(r_sub, rows // dil, stride=RESIDUE_STRIDE), :]
                    out_ref[0, res, :, j * LANES:(j + 1) * LANES] = v.astype(out_ref.dtype)
                    if not last:
                        scr2_ref[res * n_lane_chunks + j] = v
            prev_dil = dil


def _in_proj(x2, g, w_in, gains, batch, seq):
    n = x2.shape[0]
    rows = IN_PROJ_ROWS
    steps = seq // rows
    a_widths = (A_Q_W, 2 * A_KV_W, 2 * A_KV_W)
    out_shape = [jax.ShapeDtypeStruct((n, w), jnp.bfloat16) for w in a_widths]
    out_specs = [pl.BlockSpec((rows, w), lambda i: (i, 0)) for w in a_widths]
    for _ in range(3):
        for dil in B_DILS:
            if dil == 1:
                out_shape.append(jax.ShapeDtypeStruct((n, B_W), jnp.bfloat16))
                out_specs.append(pl.BlockSpec((rows, B_W), lambda i: (i, 0)))
            else:
                out_shape.append(jax.ShapeDtypeStruct((batch, dil, seq // dil, B_W), jnp.bfloat16))
                out_specs.append(pl.BlockSpec((1, dil, rows // dil, B_W),
                                              lambda i: (i // steps, 0, i % steps, 0)))
    return pl.pallas_call(
        _in_proj_kernel,
        out_shape=out_shape,
        grid=(n // rows,),
        in_specs=[
            pl.BlockSpec((rows, D_MODEL), lambda i: (i, 0)),
            pl.BlockSpec((1, D_MODEL), lambda i: (0, 0)),
            pl.BlockSpec(memory_space=pl.ANY),
            pl.BlockSpec(gains.shape, lambda i: (0, 0)),
        ],
        out_specs=out_specs,
        scratch_shapes=[pltpu.VMEM((B_W // LANES, rows, LANES), jnp.float32),
                        pltpu.VMEM((RESIDUE_STRIDE * B_W // LANES, rows // RESIDUE_STRIDE, LANES),
                                   jnp.float32),
                        pltpu.VMEM(w_in.shape, jnp.bfloat16),
                        pltpu.VMEM((w_in.shape[0], w_in.shape[1] // 3), jnp.float32),
                        pltpu.SemaphoreType.DMA],
        compiler_params=pltpu.CompilerParams(
            dimension_semantics=("arbitrary",), vmem_limit_bytes=LARGE_VMEM_LIMIT),
        name="in_proj",
    )(x2, g, w_in, gains)


def _attn_kernel(*refs, n_chunks, kv_chunks, heads_per_stack, tk, half_w, seq_len, rows, has_sink,
                 want_lse):
    it = iter(refs)
    q_ref, k_ref, v_ref, bias_ref = next(it), next(it), next(it), next(it)
    sink_ref = next(it) if has_sink else None
    o_ref = next(it)
    lse_ref = next(it) if want_lse else None

    n_tiles = seq_len // Q_TILE
    tiles_per_step = rows // Q_TILE
    chunks_per_group = n_chunks // kv_chunks
    assert (2 * chunks_per_group) % heads_per_stack == 0
    step = pl.program_id(1)
    lane = lax.broadcasted_iota(jnp.int32, (Q_TILE, LANES), 1)
    low_half = lane < HEAD_DIM
    ones = jnp.ones((tk, LANES), jnp.bfloat16)

    for sq, t in [(a, b) for a in range(q_ref.shape[0]) for b in range(tiles_per_step)]:
        tile = step * tiles_per_step + t
        q0 = tile * Q_TILE
        kv_rows = k_ref.shape[1]
        kv_row0 = jnp.clip(step * rows - half_w, 0, seq_len - kv_rows)
        start = pl.multiple_of(jnp.clip(q0 - half_w, 0, seq_len - tk) - kv_row0, HEAD_DIM)
        variant = jnp.where(tile == 0, 0, jnp.where(tile == n_tiles - 1, 2, 1))
        r0 = t * Q_TILE
        lse_tile = jnp.zeros((Q_TILE, LANES), jnp.float32)
        head_o, head_lse = {}, {}
        for g in range(2 * n_chunks // heads_per_stack):
            heads = range(g * heads_per_stack, (g + 1) * heads_per_stack)
            kv = (heads[0] // 2) // chunks_per_group
            kc = k_ref[sq, pl.ds(start, tk), kv * LANES:(kv + 1) * LANES]
            vc = v_ref[sq, pl.ds(start, tk), kv * LANES:(kv + 1) * LANES]
            v_aug = jnp.concatenate([vc, ones], axis=1)
            q_parts = []
            for h in heads:
                c = h // 2
                q2 = q_ref[sq, r0:r0 + Q_TILE, c * LANES:(c + 1) * LANES]
                keep = low_half if h % 2 == 0 else ~low_half
                q_parts.append(jnp.where(keep, q2, jnp.zeros_like(q2)))
            qs = q_parts[0] if len(q_parts) == 1 else jnp.concatenate(q_parts, axis=0)
            s = lax.dot_general(qs, kc, (((1,), (1,)), ((), ())),
                                preferred_element_type=jnp.float32)
            s = s + bias_ref[variant, g]
            m = jnp.max(s, axis=-1, keepdims=True)
            if has_sink:
                m = jnp.maximum(m, sink_ref[g])
            p = jnp.exp2(s - m)
            ov = jnp.dot(p.astype(jnp.bfloat16), v_aug, preferred_element_type=jnp.float32)
            o, l = ov[:, :LANES], ov[:, LANES:]
            if has_sink:
                l = l + jnp.exp2(sink_ref[g] - m)
            o = o * (1.0 / l)
            if want_lse:
                lse = m + jnp.log(l) * LOG2E
            for idx, h in enumerate(heads):
                head_o[h] = o[idx * Q_TILE:(idx + 1) * Q_TILE]
                if want_lse:
                    head_lse[h] = lse[idx * Q_TILE:(idx + 1) * Q_TILE]
                if h % 2 == 1:
                    c = h // 2
                    o2 = jnp.where(low_half, head_o.pop(h - 1), head_o.pop(h))
                    o_ref[sq, r0:r0 + Q_TILE, c * LANES:(c + 1) * LANES] = o2.astype(o_ref.dtype)
                    if want_lse:
                        lse_tile = jnp.where(lane == h - 1, head_lse.pop(h - 1),
                                             jnp.where(lane == h, head_lse.pop(h), lse_tile))
        if want_lse:
            lse_ref[sq, r0:r0 + Q_TILE, :] = lse_tile


def _banded_attention(q, k, v, bias, *, half_w, sink=None, want_lse=False, name):
    n_seq, L, qw = q.shape
    kw = k.shape[2]
    tk = Q_TILE + 2 * half_w
    rows = min(ATTN_STEP_ROWS, L)
    seqs = ATTN_STEP_ROWS // rows
    kv_rows = min(L, rows + 2 * half_w)

    def kv_index(s, i):
        row0 = pl.multiple_of(jnp.clip(i * rows - half_w, 0, L - kv_rows), HEAD_DIM)
        return s * seqs, row0, 0

    args = [q, k, v, bias]
    in_specs = [
        pl.BlockSpec((seqs, rows, qw), lambda s, i: (s, i, 0)),
        pl.BlockSpec((pl.Element(seqs), pl.Element(kv_rows), pl.Element(kw)), kv_index),
        pl.BlockSpec((pl.Element(seqs), pl.Element(kv_rows), pl.Element(kw)), kv_index),
        pl.BlockSpec(bias.shape, lambda s, i: (0, 0, 0, 0)),
    ]
    if sink is not None:
        args.append(sink)
        in_specs.append(pl.BlockSpec(sink.shape, lambda s, i: (0, 0, 0)))
    out_shape = [jax.ShapeDtypeStruct((n_seq, L, qw), jnp.bfloat16)]
    out_specs = [pl.BlockSpec((seqs, rows, qw), lambda s, i: (s, i, 0))]
    if want_lse:
        out_shape.append(jax.ShapeDtypeStruct((n_seq, L, LANES), jnp.float32))
        out_specs.append(pl.BlockSpec((seqs, rows, LANES), lambda s, i: (s, i, 0)))

    kern = functools.partial(
        _attn_kernel, n_chunks=qw // LANES, kv_chunks=kw // LANES,
        heads_per_stack=bias.shape[2] // Q_TILE, tk=tk, half_w=half_w, seq_len=L,
        rows=rows, has_sink=sink is not None, want_lse=want_lse)
    return pl.pallas_call(
        kern,
        out_shape=out_shape,
        grid=(n_seq // seqs, L // rows),
        in_specs=in_specs,
        out_specs=out_specs,
        compiler_params=pltpu.CompilerParams(
            dimension_semantics=("arbitrary", "arbitrary"), vmem_limit_bytes=VMEM_LIMIT),
        name=name,
    )(*args)


def _out_proj_router_kernel(*refs):
    nb = len(B_DILS)
    oa_ref = refs[0]
    o_refs = refs[1:1 + nb]
    lse_refs = refs[1 + nb:1 + 2 * nb]
    (x_ref, wo_ref, g_ref, wr_ref, br_ref,
     x1_ref, hf_ref, topi_ref, gate_ref, rank_ref, cnt_ref,
     tri_ref, carry_ref, so_ref, sl_ref, wo_bf_ref) = refs[1 + 2 * nb:]
    i = pl.program_id(0)
    rows = x_ref.shape[0]

    @pl.when(i == 0)
    def _():
        a = lax.broadcasted_iota(jnp.int32, (rows, rows), 0)
        b = lax.broadcasted_iota(jnp.int32, (rows, rows), 1)
        tri_ref[...] = jnp.where(a <= b, 1.0, 0.0).astype(jnp.bfloat16)
        carry_ref[...] = jnp.zeros_like(carry_ref)
        wo_bf_ref[...] = wo_ref[...].astype(jnp.bfloat16)

    outs, lses = [], []
    for bi, dil in enumerate(B_DILS):
        if dil == 1:
            outs.append(o_refs[bi][...].astype(jnp.float32))
            lses.append(lse_refs[bi][...])
        else:
            for res in range(dil):
                for j in range(B_W // LANES):
                    so_ref[bi, j, pl.ds(res, rows // dil, stride=dil), :] = (
                        o_refs[bi][0, res, :, j * LANES:(j + 1) * LANES].astype(jnp.float32))
                sl_ref[bi, pl.ds(res, rows // dil, stride=dil), :] = lse_refs[bi][0, res]
            outs.append(jnp.concatenate([so_ref[bi, j] for j in range(B_W // LANES)], axis=-1))
            lses.append(sl_ref[bi])

    mx = functools.reduce(jnp.maximum, lses)
    es = [jnp.exp2(l - mx) for l in lses]
    inv = 1.0 / functools.reduce(lambda a, b: a + b, es)
    eh = lax.broadcasted_iota(jnp.int32, (LANES, B_W), 0)
    ej = lax.broadcasted_iota(jnp.int32, (LANES, B_W), 1) // HEAD_DIM
    expand = jnp.where(eh == ej, 1.0, 0.0).astype(jnp.bfloat16)
    ob = jnp.zeros((rows, B_W), jnp.float32)
    for e, o in zip(es, outs):
        w = e * inv
        wide = jnp.dot(w.astype(jnp.bfloat16), expand, preferred_element_type=jnp.float32)
        ob = ob + wide * o

    attn = jnp.concatenate([oa_ref[...], ob.astype(jnp.bfloat16)], axis=-1)
    x1 = x_ref[...] + jnp.dot(attn, wo_bf_ref[...], preferred_element_type=jnp.float32)
    x1_ref[...] = x1
    hf = x1 * lax.rsqrt(jnp.mean(x1 * x1, axis=-1, keepdims=True) + NORM_EPS) * g_ref[...]
    hf_hi = hf.astype(jnp.bfloat16)
    hf_ref[...] = _pack_halves(hf)
    hf_lo = (hf - hf_hi.astype(jnp.float32)).astype(jnp.bfloat16)

    nt = (((1,), (1,)), ((), ()))
    lg_hi = lax.dot_general(wr_ref[...], hf_hi, nt, preferred_element_type=jnp.float32)
    lg_lo = lax.dot_general(wr_ref[0:N_EXPERTS, :], hf_lo, nt, preferred_element_type=jnp.float32)
    logits = lg_hi[0:N_EXPERTS] + lg_hi[N_EXPERTS:] + lg_lo + br_ref[:, 0:1]


    eidx = lax.broadcasted_iota(jnp.int32, (N_EXPERTS, rows), 0)
    work = logits
    vals, sels = [], []
    for k in range(TOP_K):
        mk = jnp.max(work, axis=0, keepdims=True)
        ik = jnp.min(jnp.where(work == mk, eidx, N_EXPERTS), axis=0, keepdims=True)
        sel = eidx == ik
        work = jnp.where(sel, -jnp.inf, work)
        vals.append(mk)
        sels.append(sel)
        topi_ref[k:k + 1, :] = ik
    exps = [jnp.exp(vk - vals[0]) for vk in vals]
    denom = exps[0] + exps[1] + exps[2] + exps[3]
    ginv = 1.0 / denom
    for k in range(TOP_K):
        gate_ref[k:k + 1, :] = exps[k] * ginv

    onehot = jnp.zeros((N_EXPERTS, rows), jnp.float32)
    for sel in sels:
        onehot = onehot + jnp.where(sel, 1.0, 0.0)
    incl = jnp.dot(onehot.astype(jnp.bfloat16), tri_ref[...], preferred_element_type=jnp.float32)
    before = incl - onehot + carry_ref[:, 0:1]
    for k in range(TOP_K):
        rk = jnp.sum(jnp.where(sels[k], before, 0.0), axis=0, keepdims=True)
        rank_ref[k:k + 1, :] = rk.astype(jnp.int32)
    carry = carry_ref[...] + jnp.sum(onehot, axis=1, keepdims=True)
    carry_ref[...] = carry
    cnt_ref[...] = carry.astype(jnp.int32)


def _out_proj_router(oa, outs_b, lses_b, x2, wo_bf, g, wr, br, seq):
    n = x2.shape[0]
    rows = PROJ_ROWS
    steps = seq // rows
    row_spec = lambda w: pl.BlockSpec((rows, w), lambda i: (i, 0))
    full = lambda a: pl.BlockSpec(a.shape, lambda i: (0,) * a.ndim)
    col_spec = pl.BlockSpec((TOP_K, rows), lambda i: (0, i))

    def branch_spec(dil, w):
        if dil == 1:
            return row_spec(w)
        return pl.BlockSpec((1, dil, rows // dil, w), lambda i: (i // steps, 0, i % steps, 0))

    in_specs = ([row_spec(A_Q_W)]
                + [branch_spec(d, B_W) for d in B_DILS]
                + [branch_spec(d, LANES) for d in B_DILS]
                + [row_spec(D_MODEL), full(wo_bf), full(g), full(wr), full(br)])
    return pl.pallas_call(
        _out_proj_router_kernel,
        out_shape=[
            jax.ShapeDtypeStruct((n, D_MODEL), jnp.float32),
            jax.ShapeDtypeStruct((n, HALF_D), jnp.uint32),
            jax.ShapeDtypeStruct((TOP_K, n), jnp.int32),
            jax.ShapeDtypeStruct((TOP_K, n), jnp.float32),
            jax.ShapeDtypeStruct((TOP_K, n), jnp.int32),
            jax.ShapeDtypeStruct((N_EXPERTS, LANES), jnp.int32),
        ],
        grid=(n // rows,),
        in_specs=in_specs,
        out_specs=[row_spec(D_MODEL), row_spec(HALF_D), col_spec, col_spec, col_spec,
                   pl.BlockSpec((N_EXPERTS, LANES), lambda i: (0, 0))],
        scratch_shapes=[pltpu.VMEM((rows, rows), jnp.bfloat16),
                        pltpu.VMEM((N_EXPERTS, LANES), jnp.float32),
                        pltpu.VMEM((len(B_DILS), B_W // LANES, rows, LANES), jnp.float32),
                        pltpu.VMEM((len(B_DILS), rows, LANES), jnp.float32),
                        pltpu.VMEM(wo_bf.shape, jnp.bfloat16)],
        compiler_params=pltpu.CompilerParams(
            dimension_semantics=("arbitrary",), vmem_limit_bytes=LARGE_VMEM_LIMIT),
        name="out_proj_router",
    )(oa, *outs_b, *lses_b, x2, wo_bf, g, wr, br)


def _mxu_dot(a_bf, w_f32):
    return lax.dot_general(a_bf, w_f32, (((1,), (0,)), ((), ())), preferred_element_type=jnp.float32)


def _moe_kernel(blk_exp_ref, first_ref, slot_ref, next_exp_ref, n_used_ref,
                x_ref, w1_hbm, b1_ref, w2_hbm, b2_ref, y_ref, w1_buf, w2_buf, sem):
    step = pl.program_id(0)

    def weight_copies(expert, slot):
        return (pltpu.make_async_copy(w1_hbm.at[expert], w1_buf.at[slot], sem.at[slot, 0]),
                pltpu.make_async_copy(w2_hbm.at[expert], w2_buf.at[slot], sem.at[slot, 1]))

    @pl.when(step * MOE_STEP_BLOCKS < n_used_ref[0])
    def _():
        @pl.when(step == 0)
        def _():
            for cp in weight_copies(blk_exp_ref[0], slot_ref[0]):
                cp.start()

        for j in range(MOE_STEP_BLOCKS):
            i = step * MOE_STEP_BLOCKS + j

            @pl.when(first_ref[i] == 1)
            def _():
                slot = slot_ref[i]
                for cp in weight_copies(blk_exp_ref[i], slot):
                    cp.wait()

                @pl.when(next_exp_ref[i] >= 0)
                def _():
                    for cp in weight_copies(next_exp_ref[i], (slot + 1) % MOE_WEIGHT_BUFS):
                        cp.start()

        for j in range(MOE_STEP_BLOCKS):
            i = step * MOE_STEP_BLOCKS + j
            slot = slot_ref[i]
            expert = blk_exp_ref[i]
            rs = slice(j * MOE_ROWS, (j + 1) * MOE_ROWS)
            x = jnp.concatenate(_unpack_halves(x_ref[rs, :]), axis=-1).astype(jnp.bfloat16)
            acc = jnp.zeros((MOE_ROWS, D_MODEL), jnp.float32)
            for c in range(D_FF // FF_CHUNK):
                lo = c * FF_CHUNK
                glu = _mxu_dot(x, w1_buf[slot, :, lo:lo + FF_CHUNK]) + b1_ref[expert, :, lo:lo + FF_CHUNK]
                lin = (_mxu_dot(x, w1_buf[slot, :, D_FF + lo:D_FF + lo + FF_CHUNK])
                       + b1_ref[expert, :, D_FF + lo:D_FF + lo + FF_CHUNK])
                glu = jnp.minimum(glu, SWIGLU_LIMIT)
                lin = jnp.clip(lin, -SWIGLU_LIMIT, SWIGLU_LIMIT)
                act = glu * (1.0 / (1.0 + jnp.exp(-SWIGLU_ALPHA * glu))) * (lin + 1.0)
                acc = acc + _mxu_dot(act.astype(jnp.bfloat16), w2_buf[slot, lo:lo + FF_CHUNK, :])
            y_ref[rs, :] = _pack_halves(acc + b2_ref[expert])


def _moe_plan(pends, n_blk):
    g = MOE_ROWS
    blk_row0 = jnp.arange(n_blk, dtype=jnp.int32) * g
    blk_exp = jnp.minimum(jnp.sum(pends[None, :] <= blk_row0[:, None], axis=-1),
                          N_EXPERTS - 1).astype(jnp.int32)
    n_used = (pends[-1] // g).astype(jnp.int32)
    used = blk_row0 < pends[-1]
    prev_exp = jnp.concatenate([jnp.full((1,), -1, jnp.int32), blk_exp[:-1]])
    first = (used & (blk_exp != prev_exp)).astype(jnp.int32)
    slot = ((jnp.cumsum(first) - 1) % MOE_WEIGHT_BUFS).astype(jnp.int32)
    pstarts = jnp.concatenate([jnp.zeros((1,), pends.dtype), pends[:-1]])
    nonempty = pends > pstarts
    experts = jnp.arange(N_EXPERTS, dtype=jnp.int32)
    later = nonempty[None, :] & (experts[None, :] > experts[:, None])
    next_nonempty = jnp.min(jnp.where(later, experts[None, :], N_EXPERTS), axis=-1)
    next_nonempty = jnp.where(next_nonempty == N_EXPERTS, -1, next_nonempty).astype(jnp.int32)
    next_exp = jnp.sum(jnp.where(blk_exp[:, None] == experts[None, :], next_nonempty[None, :], 0),
                       axis=-1).astype(jnp.int32)
    return blk_exp, first, slot, next_exp, n_used.reshape(1)


def _moe_experts(plan, xb, w1, b1, w2, b2):
    n_rows = xb.shape[0]
    n_blk = n_rows // MOE_ROWS

    step_rows = MOE_STEP_BLOCKS * MOE_ROWS

    def blk(i, *p):
        return jnp.minimum(i, (p[-1][0] - 1) // MOE_STEP_BLOCKS)

    grid_spec = pltpu.PrefetchScalarGridSpec(
        num_scalar_prefetch=len(plan),
        grid=(n_blk // MOE_STEP_BLOCKS,),
        in_specs=[
            pl.BlockSpec((step_rows, HALF_D), lambda i, *p: (blk(i, *p), 0)),
            pl.BlockSpec(memory_space=pl.ANY),
            pl.BlockSpec(b1.shape, lambda i, *p: (0, 0, 0)),
            pl.BlockSpec(memory_space=pl.ANY),
            pl.BlockSpec(b2.shape, lambda i, *p: (0, 0, 0)),
        ],
        out_specs=pl.BlockSpec((step_rows, HALF_D), lambda i, *p: (blk(i, *p), 0)),
        scratch_shapes=[pltpu.VMEM((MOE_WEIGHT_BUFS, D_MODEL, 2 * D_FF), jnp.float32),
                        pltpu.VMEM((MOE_WEIGHT_BUFS, D_FF, D_MODEL), jnp.float32),
                        pltpu.SemaphoreType.DMA((MOE_WEIGHT_BUFS, 2))],
    )
    return pl.pallas_call(
        _moe_kernel,
        out_shape=jax.ShapeDtypeStruct((n_rows, HALF_D), jnp.uint32),
        grid_spec=grid_spec,
        compiler_params=pltpu.CompilerParams(
            dimension_semantics=("arbitrary",), vmem_limit_bytes=LARGE_VMEM_LIMIT),
        name="moe_experts",
    )(*plan, xb, w1, b1, w2, b2)


def _combine_kernel(x1_ref, yg_ref, gate_ref, o_ref):
    acc_lo = x1_ref[:, :HALF_D]
    acc_hi = x1_ref[:, HALF_D:]
    rows = x1_ref.shape[0]
    gates = jnp.concatenate([gate_ref[...], jnp.zeros((8 - TOP_K, rows), jnp.float32)], axis=0).T
    for k in range(TOP_K):
        lo, hi = _unpack_halves(yg_ref[k])
        gk = gates[:, k:k + 1]
        acc_lo = acc_lo + lo * gk
        acc_hi = acc_hi + hi * gk
    o_ref[:, :HALF_D] = acc_lo
    o_ref[:, HALF_D:] = acc_hi


def _combine(acc, yg, gates_nk, split):
    n = acc.shape[0]
    rows = PROJ_ROWS
    steps = yg.shape[1] // rows
    first = split * steps
    return pl.pallas_call(
        _combine_kernel,
        out_shape=jax.ShapeDtypeStruct((n, D_MODEL), jnp.float32),
        grid=(steps,),
        in_specs=[pl.BlockSpec((rows, D_MODEL), lambda i: (first + i, 0)),
                  pl.BlockSpec((TOP_K, rows, HALF_D), lambda i: (0, i, 0)),
                  pl.BlockSpec((TOP_K, rows), lambda i: (0, first + i))],
        out_specs=pl.BlockSpec((rows, D_MODEL), lambda i: (first + i, 0)),
        input_output_aliases={0: 0},
        compiler_params=pltpu.CompilerParams(
            dimension_semantics=("arbitrary",), vmem_limit_bytes=VMEM_LIMIT),
        name="moe_combine",
    )(acc, yg, gates_nk)


def _sc_worker_id():
    return lax.axis_index("s") * SC_CORES + lax.axis_index("c")


def _sc_dispatch(hf, dest3, n_rows):
    n = hf.shape[0]
    chunks_per_worker = n // SC_CHUNK // SC_WORKERS
    mesh = plsc.VectorSubcoreMesh(core_axis_name="c", subcore_axis_name="s")

    @functools.partial(
        pl.kernel, mesh=mesh,
        out_type=jax.ShapeDtypeStruct((n_rows, HALF_D), hf.dtype),
        scratch_types=[pltpu.VMEM((2, TOP_K, SC_CHUNK), jnp.int32),
                       pltpu.VMEM((2, SC_CHUNK, HALF_D), hf.dtype),
                       pltpu.SemaphoreType.DMA((2,)),
                       pltpu.SemaphoreType.DMA((2,))],
        name="sc_dispatch")
    def run(hf_hbm, dest_hbm, xb_hbm, idx_v, rows_v, load_sem, scatter_sem):
        first = _sc_worker_id() * chunks_per_worker

        def load(j):
            slot = j % 2
            pltpu.sync_copy(dest_hbm.at[first + j], idx_v.at[slot])
            return pltpu.async_copy(hf_hbm.at[pl.ds((first + j) * SC_CHUNK, SC_CHUNK)],
                                    rows_v.at[slot], load_sem.at[slot])

        loads = {0: load(0)}
        scatters = {}
        for j in range(chunks_per_worker):
            slot = j % 2
            loads.pop(j).wait()
            scatters[j] = [pltpu.async_copy(rows_v.at[slot], xb_hbm.at[idx_v.at[slot, k]],
                                            scatter_sem.at[slot]) for k in range(TOP_K)]
            if j >= 1:
                for cp in scatters.pop(j - 1):
                    cp.wait()
            if j + 1 < chunks_per_worker:
                loads[j + 1] = load(j + 1)
        for cp in scatters.pop(chunks_per_worker - 1):
            cp.wait()

    return run(hf, dest3)


def _sc_collect(y, dest3):
    n = dest3.shape[0] * SC_COLLECT_CHUNK
    chunks_per_worker = n // SC_COLLECT_CHUNK // SC_WORKERS
    mesh = plsc.VectorSubcoreMesh(core_axis_name="c", subcore_axis_name="s")

    @functools.partial(
        pl.kernel, mesh=mesh,
        out_type=jax.ShapeDtypeStruct((TOP_K, n, HALF_D), y.dtype),
        scratch_types=[pltpu.VMEM((TOP_K, SC_COLLECT_CHUNK), jnp.int32),
                       pltpu.VMEM((TOP_K, SC_COLLECT_CHUNK, HALF_D), y.dtype),
                       pltpu.SemaphoreType.DMA((TOP_K,)),
                       pltpu.SemaphoreType.DMA((TOP_K,))],
        name="sc_collect")
    def run(y_hbm, dest_hbm, yg_hbm, idx_v, rows_v, gather_sem, write_sem):
        first = _sc_worker_id() * chunks_per_worker

        @pl.loop(0, chunks_per_worker)
        def _(j):
            ch = first + j
            pltpu.sync_copy(dest_hbm.at[ch], idx_v)
            gathers = [pltpu.async_copy(y_hbm.at[idx_v.at[k]], rows_v.at[k], gather_sem.at[k])
                       for k in range(TOP_K)]
            writes = []
            for k in range(TOP_K):
                gathers[k].wait()
                writes.append(pltpu.async_copy(
                    rows_v.at[k], yg_hbm.at[k, pl.ds(ch * SC_COLLECT_CHUNK, SC_COLLECT_CHUNK)],
                    write_sem.at[k]))
            for w in writes:
                w.wait()

    return run(y, dest3)


def _sc_collect_sum(y, dest3, gates16, first_chunk, n_chunks):
    C = SC_COLLECT_CHUNK
    n = n_chunks * C
    chunks_per_worker = n_chunks // SC_WORKERS
    lanes = 16
    mesh = plsc.VectorSubcoreMesh(core_axis_name="c", subcore_axis_name="s")

    @functools.partial(
        pl.kernel, mesh=mesh,
        out_type=jax.ShapeDtypeStruct((n, HALF_D), y.dtype),
        scratch_types=[pltpu.VMEM((2, TOP_K, C), jnp.int32),
                       pltpu.VMEM((2, TOP_K, C), jnp.uint32),
                       pltpu.VMEM((2, TOP_K, C, HALF_D), y.dtype),
                       pltpu.VMEM((2, C, HALF_D), y.dtype),
                       pltpu.SemaphoreType.DMA((2, TOP_K)),
                       pltpu.SemaphoreType.DMA((2,))],
        compiler_params=pltpu.CompilerParams(needs_layout_passes=False),
        name="sc_collect_sum")
    def run(y_hbm, dest_hbm, gate_hbm, out_hbm, idx_v, gate_v, rows_v, sum_v, gather_sem, write_sem):
        local0 = _sc_worker_id() * chunks_per_worker

        def gather_copies(slot):
            return [pltpu.make_async_copy(y_hbm.at[idx_v.at[slot, k]], rows_v.at[slot, k],
                                          gather_sem.at[slot, k]) for k in range(TOP_K)]

        def write_copy(i, slot):
            return pltpu.make_async_copy(sum_v.at[slot], out_hbm.at[pl.ds((local0 + i) * C, C)],
                                         write_sem.at[slot])

        def fetch(i, slot):
            ch = first_chunk + local0 + i
            pltpu.sync_copy(dest_hbm.at[ch], idx_v.at[slot])
            pltpu.sync_copy(gate_hbm.at[ch], gate_v.at[slot])
            for cp in gather_copies(slot):
                cp.start()

        def reduce_chunk(i, slot):
            for cp in gather_copies(slot):
                cp.wait()

            @pl.when(i >= 2)
            def _():
                write_copy(i - 2, slot).wait()

            @plsc.parallel_loop(0, C)
            def _(j):
                lane_j = jnp.full((lanes,), j, jnp.int32)
                g = [plsc.bitcast(gate_v[slot, k, :].at[lane_j].get(mode="promise_in_bounds"),
                                  jnp.bfloat16) for k in range(TOP_K)]
                for c in range(HALF_D // lanes):
                    sl = pl.ds(c * lanes, lanes)
                    acc = None
                    for k in range(TOP_K):
                        term = plsc.bitcast(rows_v[slot, k, j, sl], jnp.bfloat16) * g[k]
                        acc = term if acc is None else acc + term
                    sum_v[slot, j, sl] = plsc.bitcast(acc, y.dtype)

            write_copy(i, slot).start()

        fetch(0, 0)

        @pl.loop(0, chunks_per_worker, step=2)
        def _(i):
            fetch(i + 1, 1)
            reduce_chunk(i, 0)

            @pl.when(i + 2 < chunks_per_worker)
            def _():
                fetch(i + 2, 0)

            reduce_chunk(i + 1, 1)

        write_copy(chunks_per_worker - 2, 0).wait()
        write_copy(chunks_per_worker - 1, 1).wait()

    return run(y, dest3, gates16)


def _combine_sum_kernel(x1_ref, ys_ref, o_ref):
    lo, hi = _unpack_halves(ys_ref[...])
    o_ref[:, :HALF_D] = x1_ref[:, :HALF_D] + lo
    o_ref[:, HALF_D:] = x1_ref[:, HALF_D:] + hi


def _combine_sum(acc, ysum, split):
    n = acc.shape[0]
    rows = PROJ_ROWS
    steps = ysum.shape[0] // rows
    first = split * steps
    return pl.pallas_call(
        _combine_sum_kernel,
        out_shape=jax.ShapeDtypeStruct((n, D_MODEL), jnp.float32),
        grid=(steps,),
        in_specs=[pl.BlockSpec((rows, D_MODEL), lambda i: (first + i, 0)),
                  pl.BlockSpec((rows, HALF_D), lambda i: (i, 0))],
        out_specs=pl.BlockSpec((rows, D_MODEL), lambda i: (first + i, 0)),
        input_output_aliases={0: 0},
        compiler_params=pltpu.CompilerParams(
            dimension_semantics=("arbitrary",), vmem_limit_bytes=VMEM_LIMIT),
        name="moe_combine",
    )(acc, ysum)


def _layer(x2, batch, seq, attn_norm_g, w_in, a_q_g, a_k_g, a_sinks, b_q_g, b_k_g, w_out,
           ffn_norm_g, w_router, b_router, w1, b1, w2, b2):
    n = x2.shape[0]
    slopes = _alibi_slopes()
    q_scale = HEAD_DIM ** -0.5 * LOG2E
    reps = MXU_DIM // HEAD_DIM
    gains = jnp.stack([jnp.tile(a_q_g, reps) * q_scale, jnp.tile(a_k_g, reps),
                       jnp.tile(b_q_g, reps) * q_scale, jnp.tile(b_k_g, reps)]).astype(jnp.float32)

    proj = _in_proj(x2, attn_norm_g.reshape(1, -1), w_in, gains, batch, seq)
    qa, ka, va = proj[:3]
    nb = len(B_DILS)
    qbs, kbs, vbs = proj[3:3 + nb], proj[3 + nb:3 + 2 * nb], proj[3 + 2 * nb:]

    bias_a = _bias_tables(slopes[:A_Q_HEADS], A_STACK_HEADS, A_HALF_WINDOW, 1, Q_TILE + 2 * A_HALF_WINDOW)
    sink_col = jnp.repeat(a_sinks.astype(jnp.float32) * LOG2E, Q_TILE).reshape(
        A_Q_HEADS // A_STACK_HEADS, A_STACK_HEADS * Q_TILE, 1)
    as_seqs = lambda a: a.reshape(batch, seq, a.shape[-1])
    out_a = _banded_attention(as_seqs(qa), as_seqs(ka), as_seqs(va), bias_a, half_w=A_HALF_WINDOW,
                              sink=sink_col, name="attn_a")[0].reshape(n, A_Q_W)

    outs_b, lses_b = [], []
    for bi, (window, dil) in enumerate(B_BRANCHES):
        half_w = window // (2 * dil)
        bias_b = _bias_tables(slopes[A_Q_HEADS:], B_STACK_HEADS, half_w, dil, Q_TILE + 2 * half_w)
        L = seq // dil
        to_seqs = lambda a: a.reshape(batch * dil, L, a.shape[-1])
        o, lse = _banded_attention(to_seqs(qbs[bi]), to_seqs(kbs[bi]), to_seqs(vbs[bi]), bias_b,
                                   half_w=half_w, want_lse=True, name=f"attn_b_d{dil}")
        if dil == 1:
            outs_b.append(o.reshape(n, B_W))
            lses_b.append(lse.reshape(n, LANES))
        else:
            outs_b.append(o.reshape(batch, dil, L, B_W))
            lses_b.append(lse.reshape(batch, dil, L, LANES))

    wr_t = w_router.T.astype(jnp.float32)
    wr_hi = wr_t.astype(jnp.bfloat16)
    wr_lo = (wr_t - wr_hi.astype(jnp.float32)).astype(jnp.bfloat16)
    wr = jnp.concatenate([wr_hi, wr_lo], axis=0)
    br = jnp.broadcast_to(b_router.astype(jnp.float32)[:, None], (N_EXPERTS, LANES))
    x1, hf, topi, gates, ranks, counts = _out_proj_router(
        out_a, outs_b, lses_b, x2, w_out, ffn_norm_g.reshape(1, -1), wr, br, seq)

    g = MOE_ROWS
    nk = n * TOP_K
    step_rows = MOE_STEP_BLOCKS * g
    n_rows = -(-(nk + N_EXPERTS * g) // step_rows) * step_rows
    cnt = counts[:, 0]
    pcnt = (cnt + g - 1) // g * g
    pends = jnp.cumsum(pcnt)
    pstarts = pends - pcnt
    experts = jnp.arange(N_EXPERTS, dtype=jnp.int32)
    start_of = jnp.sum(jnp.where(topi[:, :, None] == experts, pstarts, 0), axis=-1)
    dest = (start_of + ranks).astype(jnp.int32)
    plan = _moe_plan(pends, n_rows // g)
    dest3 = dest.reshape(TOP_K, n // SC_CHUNK, SC_CHUNK).transpose(1, 0, 2)

    xb = _sc_dispatch(hf, dest3, n_rows)
    y = _moe_experts(plan, xb, w1, b1[:, None, :], w2, b2[:, None, :])
    smallest_range = max(PROJ_ROWS, SC_COLLECT_CHUNK * SC_WORKERS)
    n_splits = max(1, min(COMBINE_SPLITS, n // smallest_range))
    per_split = n // n_splits
    chunked = lambda a: a.reshape(TOP_K, n // SC_COLLECT_CHUNK, SC_COLLECT_CHUNK).transpose(1, 0, 2)
    dest_c = chunked(dest)
    gate_bits = lax.bitcast_convert_type(gates.astype(jnp.bfloat16), jnp.uint16).astype(jnp.uint32)
    gate_words = gate_bits | (gate_bits << 16)
    gates_c = chunked(gate_words)
    chunks_per_split = per_split // SC_COLLECT_CHUNK
    out = x1
    for s in range(n_splits):
        ysum = _sc_collect_sum(y, dest_c, gates_c, s * chunks_per_split, chunks_per_split)
        out = _combine_sum(out, ysum, s)
    return out


def kernel(x, attn_norm_g, w_in, a_q_norm_g, a_k_norm_g, a_sinks, b_q_norm_g, b_k_norm_g, w_out,
           ffn_norm_g, w_router, b_router, w1, b1, w2, b2):
    batch, seq, d = x.shape
    x2 = x.reshape(batch * seq, d)
    for i in range(attn_norm_g.shape[0]):
        x2 = _layer(x2, batch, seq, attn_norm_g[i], w_in[i], a_q_norm_g[i], a_k_norm_g[i],
                    a_sinks[i], b_q_norm_g[i], b_k_norm_g[i], w_out[i], ffn_norm_g[i],
                    w_router[i], b_router[i], w1[i], b1[i], w2[i], b2[i])
    return x2.reshape(batch, seq, d)
```

```python
import functools

import jax
import jax.numpy as jnp
import numpy as np
from jax import lax
from jax.experimental import pallas as pl
from jax.experimental.pallas import tpu as pltpu
from jax.experimental.pallas import tpu_sc as plsc

D_MODEL = 1024
HALF_D = D_MODEL // 2
HEAD_DIM = 64
LANES = 128
MXU_DIM = 256
A_Q_HEADS = 8
A_KV_HEADS = 2
B_HEADS = 8
A_HALF_WINDOW = 128
B_BRANCHES = ((128, 1), (512, 4), (2048, 16))
B_DILS = tuple(d for _, d in B_BRANCHES)
RESIDUE_STRIDE = 4
N_ALIBI_HEADS = 16
A_Q_W = A_Q_HEADS * HEAD_DIM
A_KV_W = A_KV_HEADS * HEAD_DIM
B_W = B_HEADS * HEAD_DIM
N_EXPERTS = 32
TOP_K = 4
D_FF = 1024
SWIGLU_ALPHA = 1.702
SWIGLU_LIMIT = 7.0
NORM_EPS = 1e-5
MASK_VALUE = -1e30
LOG2E = 1.4426950408889634

Q_TILE = 128
ATTN_STEP_ROWS = 2048
A_STACK_HEADS = 4
B_STACK_HEADS = 2
PROJ_ROWS = 1024
IN_PROJ_ROWS = 1024
MOE_ROWS = 512
FF_CHUNK = 512
MOE_STEP_BLOCKS = 2
MOE_WEIGHT_BUFS = 3
VMEM_LIMIT = 48 * 1024 * 1024
LARGE_VMEM_LIMIT = 58 * 1024 * 1024
SC_CORES = 2
SC_SUBCORES = 16
SC_WORKERS = SC_CORES * SC_SUBCORES
SC_CHUNK = 64
SC_COLLECT_CHUNK = 16
COMBINE_SPLITS = 2


def _pack_halves(v):
    lo = v[:, :HALF_D].astype(jnp.bfloat16).astype(jnp.float32)
    hi = v[:, HALF_D:].astype(jnp.bfloat16).astype(jnp.float32)
    return (pltpu.bitcast(lo, jnp.uint32) >> 16) | pltpu.bitcast(hi, jnp.uint32)


def _unpack_halves(w):
    lo = pltpu.bitcast(w << 16, jnp.float32)
    hi = pltpu.bitcast(w & jnp.uint32(0xFFFF0000), jnp.float32)
    return lo, hi


def _alibi_slopes():
    return np.exp2(-8.0 * np.arange(1, N_ALIBI_HEADS + 1, dtype=np.float32) / N_ALIBI_HEADS).astype(np.float32)


def _bias_tables(head_slopes, heads_per_group, half_w, dist_scale, tk):
    i = np.arange(Q_TILE)[:, None]
    j = np.arange(tk)[None, :]
    tabs = []
    for shift in (0, half_w, tk - Q_TILE):
        dist = np.abs(j - shift - i)
        valid = dist <= half_w
        per_head = []
        for sl in head_slopes:
            b = (-np.float64(sl) * LOG2E * (dist * dist_scale)).astype(np.float32)
            per_head.append(np.where(valid, b, np.float32(MASK_VALUE)).astype(np.float32))
        t = np.stack(per_head).reshape(-1, heads_per_group * Q_TILE, tk)
        tabs.append(t)
    return jnp.asarray(np.stack(tabs))


def _in_proj_kernel(x_ref, g_ref, w_hbm, gains_ref, qa_ref, ka_ref, va_ref, *rest):
    b_refs, (scr_ref, scr2_ref, w_ref, stage_ref, sem) = rest[:-5], rest[-5:]

    @pl.when(pl.program_id(0) == 0)
    def _():
        width = stage_ref.shape[1]
        for c0 in range(0, w_ref.shape[1], width):
            cp = pltpu.make_async_copy(w_hbm.at[:, c0:c0 + width], stage_ref, sem)
            cp.start()
            cp.wait()
            w_ref[:, c0:c0 + width] = stage_ref[...].astype(jnp.bfloat16)

    x = x_ref[...]
    xn = x * lax.rsqrt(jnp.mean(x * x, axis=-1, keepdims=True) + NORM_EPS) * g_ref[...]
    xn = xn.astype(jnp.bfloat16)
    r = lax.broadcasted_iota(jnp.int32, (MXU_DIM, MXU_DIM), 0) // HEAD_DIM
    c = lax.broadcasted_iota(jnp.int32, (MXU_DIM, MXU_DIM), 1) // HEAD_DIM
    blockdiag = jnp.where(r == c, 1.0, 0.0).astype(jnp.bfloat16)

    def head_rms(sec, gain_row):
        width = sec.shape[1]
        parts = []
        step = min(width, MXU_DIM)
        for j in range(width // step):
            p = sec[:, j * step:(j + 1) * step]
            ss = jnp.dot((p * p).astype(jnp.bfloat16), blockdiag[:step, :step],
                         preferred_element_type=jnp.float32)
            parts.append(p * lax.rsqrt(ss * (1.0 / HEAD_DIM) + NORM_EPS)
                         * gains_ref[gain_row:gain_row + 1, :step])
        return parts

    def project(col0, width, gain_row):
        sec = jnp.dot(xn, w_ref[:, col0:col0 + width], preferred_element_type=jnp.float32)
        return [sec] if gain_row is None else head_rms(sec, gain_row)

    def store(out_ref, parts):
        w = parts[0].shape[1]
        for j, p in enumerate(parts):
            out_ref[:, j * w:(j + 1) * w] = p.astype(out_ref.dtype)

    def per_kv_head(p):
        lane = lax.broadcasted_iota(jnp.int32, p.shape, 1)
        swapped = pltpu.roll(p, HEAD_DIM, axis=1)
        low = lane < HEAD_DIM
        return [jnp.where(low, p, swapped), jnp.where(low, swapped, p)]

    store(qa_ref, project(0, A_Q_W, 0))
    kva = project(A_Q_W, 2 * A_KV_W, None)[0]
    store(ka_ref, per_kv_head(head_rms(kva[:, :A_KV_W], 1)[0]))
    store(va_ref, per_kv_head(kva[:, A_KV_W:]))

    rows = x_ref.shape[0]
    col0 = A_Q_W + 2 * A_KV_W
    for t, gain_row in enumerate((2, 3, None)):
        parts = project(col0 + t * B_W, B_W, gain_row)
        sec = jnp.concatenate(parts, axis=-1) if len(parts) > 1 else parts[0]
        for j in range(B_W // LANES):
            scr_ref[j] = sec[:, j * LANES:(j + 1) * LANES]
        n_lane_chunks = B_W // LANES
        prev_dil = 1
        for bi, dil in enumerate(B_DILS):
            out_ref = b_refs[t * len(B_DILS) + bi]
            if dil == 1:
                out_ref[...] = sec.astype(out_ref.dtype)
                continue
            assert dil == prev_dil * RESIDUE_STRIDE
            last = dil == B_DILS[-1]
            for res in range(dil):
                r_prev, r_sub = res % prev_dil, res // prev_dil
                for j in range(n_lane_chunks):
                    if prev_dil == 1:
                        v = scr_ref[j, pl.ds(r_sub, rows // dil, stride=RESIDUE_STRIDE), :]
                    else:
                        v = scr2_ref[r_prev * n_lane_chunks + j,
                                     pl.ds(r_sub, rows // dil, stride=RESIDUE_STRIDE), :]
                    out_ref[0, res, :, j * LANES:(j + 1) * LANES] = v.astype(out_ref.dtype)
                    if not last:
                        scr2_ref[res * n_lane_chunks + j] = v
            prev_dil = dil


def _in_proj(x2, g, w_in, gains, batch, seq):
    n = x2.shape[0]
    rows = IN_PROJ_ROWS
    steps = seq // rows
    a_widths = (A_Q_W, 2 * A_KV_W, 2 * A_KV_W)
    out_shape = [jax.ShapeDtypeStruct((n, w), jnp.bfloat16) for w in a_widths]
    out_specs = [pl.BlockSpec((rows, w), lambda i: (i, 0)) for w in a_widths]
    for _ in range(3):
        for dil in B_DILS:
            if dil == 1:
                out_shape.append(jax.ShapeDtypeStruct((n, B_W), jnp.bfloat16))
                out_specs.append(pl.BlockSpec((rows, B_W), lambda i: (i, 0)))
            else:
                out_shape.append(jax.ShapeDtypeStruct((batch, dil, seq // dil, B_W), jnp.bfloat16))
                out_specs.append(pl.BlockSpec((1, dil, rows // dil, B_W),
                                              lambda i: (i // steps, 0, i % steps, 0)))
    return pl.pallas_call(
        _in_proj_kernel,
        out_shape=out_shape,
        grid=(n // rows,),
        in_specs=[
            pl.BlockSpec((rows, D_MODEL), lambda i: (i, 0)),
            pl.BlockSpec((1, D_MODEL), lambda i: (0, 0)),
            pl.BlockSpec(memory_space=pl.ANY),
            pl.BlockSpec(gains.shape, lambda i: (0, 0)),
        ],
        out_specs=out_specs,
        scratch_shapes=[pltpu.VMEM((B_W // LANES, rows, LANES), jnp.float32),
                        pltpu.VMEM((RESIDUE_STRIDE * B_W // LANES, rows // RESIDUE_STRIDE, LANES),
                                   jnp.float32),
                        pltpu.VMEM(w_in.shape, jnp.bfloat16),
                        pltpu.VMEM((w_in.shape[0], w_in.shape[1] // 3), jnp.float32),
                        pltpu.SemaphoreType.DMA],
        compiler_params=pltpu.CompilerParams(
            dimension_semantics=("arbitrary",), vmem_limit_bytes=LARGE_VMEM_LIMIT),
        name="in_proj",
    )(x2, g, w_in, gains)


def _attn_kernel(*refs, n_chunks, kv_chunks, heads_per_stack, tk, half_w, seq_len, rows, has_sink,
                 want_lse):
    it = iter(refs)
    q_ref, k_ref, v_ref, bias_ref = next(it), next(it), next(it), next(it)
    sink_ref = next(it) if has_sink else None
    o_ref = next(it)
    lse_ref = next(it) if want_lse else None

    n_tiles = seq_len // Q_TILE
    tiles_per_step = rows // Q_TILE
    chunks_per_group = n_chunks // kv_chunks
    assert (2 * chunks_per_group) % heads_per_stack == 0
    step = pl.program_id(1)
    lane = lax.broadcasted_iota(jnp.int32, (Q_TILE, LANES), 1)
    low_half = lane < HEAD_DIM
    ones = jnp.ones((tk, LANES), jnp.bfloat16)

    for sq, t in [(a, b) for a in range(q_ref.shape[0]) for b in range(tiles_per_step)]:
        tile = step * tiles_per_step + t
        q0 = tile * Q_TILE
        kv_rows = k_ref.shape[1]
        kv_row0 = jnp.clip(step * rows - half_w, 0, seq_len - kv_rows)
        start = pl.multiple_of(jnp.clip(q0 - half_w, 0, seq_len - tk) - kv_row0, HEAD_DIM)
        variant = jnp.where(tile == 0, 0, jnp.where(tile == n_tiles - 1, 2, 1))
        r0 = t * Q_TILE
        lse_tile = jnp.zeros((Q_TILE, LANES), jnp.float32)
        head_o, head_lse = {}, {}
        for g in range(2 * n_chunks // heads_per_stack):
            heads = range(g * heads_per_stack, (g + 1) * heads_per_stack)
            kv = (heads[0] // 2) // chunks_per_group
            kc = k_ref[sq, pl.ds(start, tk), kv * LANES:(kv + 1) * LANES]
            vc = v_ref[sq, pl.ds(start, tk), kv * LANES:(kv + 1) * LANES]
            v_aug = jnp.concatenate([vc, ones], axis=1)
            q_parts = []
            for h in heads:
                c = h // 2
                q2 = q_ref[sq, r0:r0 + Q_TILE, c * LANES:(c + 1) * LANES]
                keep = low_half if h % 2 == 0 else ~low_half
                q_parts.append(jnp.where(keep, q2, jnp.zeros_like(q2)))
            qs = q_parts[0] if len(q_parts) == 1 else jnp.concatenate(q_parts, axis=0)
            s = lax.dot_general(qs, kc, (((1,), (1,)), ((), ())),
                                preferred_element_type=jnp.float32)
            s = s + bias_ref[variant, g]
            m = jnp.max(s, axis=-1, keepdims=True)
            if has_sink:
                m = jnp.maximum(m, sink_ref[g])
            p = jnp.exp2(s - m)
            ov = jnp.dot(p.astype(jnp.bfloat16), v_aug, preferred_element_type=jnp.float32)
            o, l = ov[:, :LANES], ov[:, LANES:]
            if has_sink:
                l = l + jnp.exp2(sink_ref[g] - m)
            o = o * (1.0 / l)
            if want_lse:
                lse = m + jnp.log(l) * LOG2E
            for idx, h in enumerate(heads):
                head_o[h] = o[idx * Q_TILE:(idx + 1) * Q_TILE]
                if want_lse:
                    head_lse[h] = lse[idx * Q_TILE:(idx + 1) * Q_TILE]
                if h % 2 == 1:
                    c = h // 2
                    o2 = jnp.where(low_half, head_o.pop(h - 1), head_o.pop(h))
                    o_ref[sq, r0:r0 + Q_TILE, c * LANES:(c + 1) * LANES] = o2.astype(o_ref.dtype)
                    if want_lse:
                        lse_tile = jnp.where(lane == h - 1, head_lse.pop(h - 1),
                                             jnp.where(lane == h, head_lse.pop(h), lse_tile))
        if want_lse:
            lse_ref[sq, r0:r0 + Q_TILE, :] = lse_tile


def _banded_attention(q, k, v, bias, *, half_w, sink=None, want_lse=False, name):
    n_seq, L, qw = q.shape
    kw = k.shape[2]
    tk = Q_TILE + 2 * half_w
    rows = min(ATTN_STEP_ROWS, L)
    seqs = ATTN_STEP_ROWS // rows
    kv_rows = min(L, rows + 2 * half_w)

    def kv_index(s, i):
        row0 = pl.multiple_of(jnp.clip(i * rows - half_w, 0, L - kv_rows), HEAD_DIM)
        return s * seqs, row0, 0

    args = [q, k, v, bias]
    in_specs = [
        pl.BlockSpec((seqs, rows, qw), lambda s, i: (s, i, 0)),
        pl.BlockSpec((pl.Element(seqs), pl.Element(kv_rows), pl.Element(kw)), kv_index),
        pl.BlockSpec((pl.Element(seqs), pl.Element(kv_rows), pl.Element(kw)), kv_index),
        pl.BlockSpec(bias.shape, lambda s, i: (0, 0, 0, 0)),
    ]
    if sink is not None:
        args.append(sink)
        in_specs.append(pl.BlockSpec(sink.shape, lambda s, i: (0, 0, 0)))
    out_shape = [jax.ShapeDtypeStruct((n_seq, L, qw), jnp.bfloat16)]
    out_specs = [pl.BlockSpec((seqs, rows, qw), lambda s, i: (s, i, 0))]
    if want_lse:
        out_shape.append(jax.ShapeDtypeStruct((n_seq, L, LANES), jnp.float32))
        out_specs.append(pl.BlockSpec((seqs, rows, LANES), lambda s, i: (s, i, 0)))

    kern = functools.partial(
        _attn_kernel, n_chunks=qw // LANES, kv_chunks=kw // LANES,
        heads_per_stack=bias.shape[2] // Q_TILE, tk=tk, half_w=half_w, seq_len=L,
        rows=rows, has_sink=sink is not None, want_lse=want_lse)
    return pl.pallas_call(
        kern,
        out_shape=out_shape,
        grid=(n_seq // seqs, L // rows),
        in_specs=in_specs,
        out_specs=out_specs,
        compiler_params=pltpu.CompilerParams(
            dimension_semantics=("arbitrary", "arbitrary"), vmem_limit_bytes=VMEM_LIMIT),
        name=name,
    )(*args)


def _out_proj_router_kernel(*refs):
    nb = len(B_DILS)
    oa_ref = refs[0]
    o_refs = refs[1:1 + nb]
    lse_refs = refs[1 + nb:1 + 2 * nb]
    (x_ref, wo_ref, g_ref, wr_ref, br_ref,
     x1_ref, hf_ref, topi_ref, gate_ref, rank_ref, cnt_ref,
     tri_ref, carry_ref, so_ref, sl_ref, wo_bf_ref) = refs[1 + 2 * nb:]
    i = pl.program_id(0)
    rows = x_ref.shape[0]

    @pl.when(i == 0)
    def _():
        a = lax.broadcasted_iota(jnp.int32, (rows, rows), 0)
        b = lax.broadcasted_iota(jnp.int32, (rows, rows), 1)
        tri_ref[...] = jnp.where(a <= b, 1.0, 0.0).astype(jnp.bfloat16)
        carry_ref[...] = jnp.zeros_like(carry_ref)
        wo_bf_ref[...] = wo_ref[...].astype(jnp.bfloat16)

    outs, lses = [], []
    for bi, dil in enumerate(B_DILS):
        if dil == 1:
            outs.append(o_refs[bi][...].astype(jnp.float32))
            lses.append(lse_refs[bi][...])
        else:
            for res in range(dil):
                for j in range(B_W // LANES):
                    so_ref[bi, j, pl.ds(res, rows // dil, stride=dil), :] = (
                        o_refs[bi][0, res, :, j * LANES:(j + 1) * LANES].astype(jnp.float32))
                sl_ref[bi, pl.ds(res, rows // dil, stride=dil), :] = lse_refs[bi][0, res]
            outs.append(jnp.concatenate([so_ref[bi, j] for j in range(B_W // LANES)], axis=-1))
            lses.append(sl_ref[bi])

    mx = functools.reduce(jnp.maximum, lses)
    es = [jnp.exp2(l - mx) for l in lses]
    inv = 1.0 / functools.reduce(lambda a, b: a + b, es)
    eh = lax.broadcasted_iota(jnp.int32, (LANES, B_W), 0)
    ej = lax.broadcasted_iota(jnp.int32, (LANES, B_W), 1) // HEAD_DIM
    expand = jnp.where(eh == ej, 1.0, 0.0).astype(jnp.bfloat16)
    ob = jnp.zeros((rows, B_W), jnp.float32)
    for e, o in zip(es, outs):
        w = e * inv
        wide = jnp.dot(w.astype(jnp.bfloat16), expand, preferred_element_type=jnp.float32)
        ob = ob + wide * o

    attn = jnp.concatenate([oa_ref[...], ob.astype(jnp.bfloat16)], axis=-1)
    x1 = x_ref[...] + jnp.dot(attn, wo_bf_ref[...], preferred_element_type=jnp.float32)
    x1_ref[...] = x1
    hf = x1 * lax.rsqrt(jnp.mean(x1 * x1, axis=-1, keepdims=True) + NORM_EPS) * g_ref[...]
    hf_hi = hf.astype(jnp.bfloat16)
    hf_ref[...] = _pack_halves(hf)
    hf_lo = (hf - hf_hi.astype(jnp.float32)).astype(jnp.bfloat16)

    nt = (((1,), (1,)), ((), ()))
    lg_hi = lax.dot_general(wr_ref[...], hf_hi, nt, preferred_element_type=jnp.float32)
    lg_lo = lax.dot_general(wr_ref[0:N_EXPERTS, :], hf_lo, nt, preferred_element_type=jnp.float32)
    logits = lg_hi[0:N_EXPERTS] + lg_hi[N_EXPERTS:] + lg_lo + br_ref[:, 0:1]


    eidx = lax.broadcasted_iota(jnp.int32, (N_EXPERTS, rows), 0)
    work = logits
    vals, sels = [], []
    for k in range(TOP_K):
        mk = jnp.max(work, axis=0, keepdims=True)
        ik = jnp.min(jnp.where(work == mk, eidx, N_EXPERTS), axis=0, keepdims=True)
        sel = eidx == ik
        work = jnp.where(sel, -jnp.inf, work)
        vals.append(mk)
        sels.append(sel)
        topi_ref[k:k + 1, :] = ik
    exps = [jnp.exp(vk - vals[0]) for vk in vals]
    denom = exps[0] + exps[1] + exps[2] + exps[3]
    ginv = 1.0 / denom
    for k in range(TOP_K):
        gate_ref[k:k + 1, :] = exps[k] * ginv

    onehot = jnp.zeros((N_EXPERTS, rows), jnp.float32)
    for sel in sels:
        onehot = onehot + jnp.where(sel, 1.0, 0.0)
    incl = jnp.dot(onehot.astype(jnp.bfloat16), tri_ref[...], preferred_element_type=jnp.float32)
    before = incl - onehot + carry_ref[:, 0:1]
    for k in range(TOP_K):
        rk = jnp.sum(jnp.where(sels[k], before, 0.0), axis=0, keepdims=True)
        rank_ref[k:k + 1, :] = rk.astype(jnp.int32)
    carry = carry_ref[...] + jnp.sum(onehot, axis=1, keepdims=True)
    carry_ref[...] = carry
    cnt_ref[...] = carry.astype(jnp.int32)


def _out_proj_router(oa, outs_b, lses_b, x2, wo_bf, g, wr, br, seq):
    n = x2.shape[0]
    rows = PROJ_ROWS
    steps = seq // rows
    row_spec = lambda w: pl.BlockSpec((rows, w), lambda i: (i, 0))
    full = lambda a: pl.BlockSpec(a.shape, lambda i: (0,) * a.ndim)
    col_spec = pl.BlockSpec((TOP_K, rows), lambda i: (0, i))

    def branch_spec(dil, w):
        if dil == 1:
            return row_spec(w)
        return pl.BlockSpec((1, dil, rows // dil, w), lambda i: (i // steps, 0, i % steps, 0))

    in_specs = ([row_spec(A_Q_W)]
                + [branch_spec(d, B_W) for d in B_DILS]
                + [branch_spec(d, LANES) for d in B_DILS]
                + [row_spec(D_MODEL), full(wo_bf), full(g), full(wr), full(br)])
    return pl.pallas_call(
        _out_proj_router_kernel,
        out_shape=[
            jax.ShapeDtypeStruct((n, D_MODEL), jnp.float32),
            jax.ShapeDtypeStruct((n, HALF_D), jnp.uint32),
            jax.ShapeDtypeStruct((TOP_K, n), jnp.int32),
            jax.ShapeDtypeStruct((TOP_K, n), jnp.float32),
            jax.ShapeDtypeStruct((TOP_K, n), jnp.int32),
            jax.ShapeDtypeStruct((N_EXPERTS, LANES), jnp.int32),
        ],
        grid=(n // rows,),
        in_specs=in_specs,
        out_specs=[row_spec(D_MODEL), row_spec(HALF_D), col_spec, col_spec, col_spec,
                   pl.BlockSpec((N_EXPERTS, LANES), lambda i: (0, 0))],
        scratch_shapes=[pltpu.VMEM((rows, rows), jnp.bfloat16),
                        pltpu.VMEM((N_EXPERTS, LANES), jnp.float32),
                        pltpu.VMEM((len(B_DILS), B_W // LANES, rows, LANES), jnp.float32),
                        pltpu.VMEM((len(B_DILS), rows, LANES), jnp.float32),
                        pltpu.VMEM(wo_bf.shape, jnp.bfloat16)],
        compiler_params=pltpu.CompilerParams(
            dimension_semantics=("arbitrary",), vmem_limit_bytes=LARGE_VMEM_LIMIT),
        name="out_proj_router",
    )(oa, *outs_b, *lses_b, x2, wo_bf, g, wr, br)


def _mxu_dot(a_bf, w_f32):
    return lax.dot_general(a_bf, w_f32, (((1,), (0,)), ((), ())), preferred_element_type=jnp.float32)


def _moe_kernel(blk_exp_ref, first_ref, slot_ref, next_exp_ref, n_used_ref,
                x_ref, w1_hbm, b1_ref, w2_hbm, b2_ref, y_ref, w1_buf, w2_buf, sem):
    step = pl.program_id(0)

    def weight_copies(expert, slot):
        return (pltpu.make_async_copy(w1_hbm.at[expert], w1_buf.at[slot], sem.at[slot, 0]),
                pltpu.make_async_copy(w2_hbm.at[expert], w2_buf.at[slot], sem.at[slot, 1]))

    @pl.when(step * MOE_STEP_BLOCKS < n_used_ref[0])
    def _():
        @pl.when(step == 0)
        def _():
            for cp in weight_copies(blk_exp_ref[0], slot_ref[0]):
                cp.start()

        for j in range(MOE_STEP_BLOCKS):
            i = step * MOE_STEP_BLOCKS + j

            @pl.when(first_ref[i] == 1)
            def _():
                slot = slot_ref[i]
                for cp in weight_copies(blk_exp_ref[i], slot):
                    cp.wait()

                @pl.when(next_exp_ref[i] >= 0)
                def _():
                    for cp in weight_copies(next_exp_ref[i], (slot + 1) % MOE_WEIGHT_BUFS):
                        cp.start()

        for j in range(MOE_STEP_BLOCKS):
            i = step * MOE_STEP_BLOCKS + j
            slot = slot_ref[i]
            expert = blk_exp_ref[i]
            rs = slice(j * MOE_ROWS, (j + 1) * MOE_ROWS)
            x = jnp.concatenate(_unpack_halves(x_ref[rs, :]), axis=-1).astype(jnp.bfloat16)
            acc = jnp.zeros((MOE_ROWS, D_MODEL), jnp.float32)
            for c in range(D_FF // FF_CHUNK):
                lo = c * FF_CHUNK
                glu = _mxu_dot(x, w1_buf[slot, :, lo:lo + FF_CHUNK]) + b1_ref[expert, :, lo:lo + FF_CHUNK]
                lin = (_mxu_dot(x, w1_buf[slot, :, D_FF + lo:D_FF + lo + FF_CHUNK])
                       + b1_ref[expert, :, D_FF + lo:D_FF + lo + FF_CHUNK])
                glu = jnp.minimum(glu, SWIGLU_LIMIT)
                lin = jnp.clip(lin, -SWIGLU_LIMIT, SWIGLU_LIMIT)
                act = glu * (1.0 / (1.0 + jnp.exp(-SWIGLU_ALPHA * glu))) * (lin + 1.0)
                acc = acc + _mxu_dot(act.astype(jnp.bfloat16), w2_buf[slot, lo:lo + FF_CHUNK, :])
            y_ref[rs, :] = _pack_halves(acc + b2_ref[expert])


def _moe_plan(pends, n_blk):
    g = MOE_ROWS
    blk_row0 = jnp.arange(n_blk, dtype=jnp.int32) * g
    blk_exp = jnp.minimum(jnp.sum(pends[None, :] <= blk_row0[:, None], axis=-1),
                          N_EXPERTS - 1).astype(jnp.int32)
    n_used = (pends[-1] // g).astype(jnp.int32)
    used = blk_row0 < pends[-1]
    prev_exp = jnp.concatenate([jnp.full((1,), -1, jnp.int32), blk_exp[:-1]])
    first = (used & (blk_exp != prev_exp)).astype(jnp.int32)
    slot = ((jnp.cumsum(first) - 1) % MOE_WEIGHT_BUFS).astype(jnp.int32)
    pstarts = jnp.concatenate([jnp.zeros((1,), pends.dtype), pends[:-1]])
    nonempty = pends > pstarts
    experts = jnp.arange(N_EXPERTS, dtype=jnp.int32)
    later = nonempty[None, :] & (experts[None, :] > experts[:, None])
    next_nonempty = jnp.min(jnp.where(later, experts[None, :], N_EXPERTS), axis=-1)
    next_nonempty = jnp.where(next_nonempty == N_EXPERTS, -1, next_nonempty).astype(jnp.int32)
    next_exp = jnp.sum(jnp.where(blk_exp[:, None] == experts[None, :], next_nonempty[None, :], 0),
                       axis=-1).astype(jnp.int32)
    return blk_exp, first, slot, next_exp, n_used.reshape(1)


def _moe_experts(plan, xb, w1, b1, w2, b2):
    n_rows = xb.shape[0]
    n_blk = n_rows // MOE_ROWS

    step_rows = MOE_STEP_BLOCKS * MOE_ROWS

    def blk(i, *p):
        return jnp.minimum(i, (p[-1][0] - 1) // MOE_STEP_BLOCKS)

    grid_spec = pltpu.PrefetchScalarGridSpec(
        num_scalar_prefetch=len(plan),
        grid=(n_blk // MOE_STEP_BLOCKS,),
        in_specs=[
            pl.BlockSpec((step_rows, HALF_D), lambda i, *p: (blk(i, *p), 0)),
            pl.BlockSpec(memory_space=pl.ANY),
            pl.BlockSpec(b1.shape, lambda i, *p: (0, 0, 0)),
            pl.BlockSpec(memory_space=pl.ANY),
            pl.BlockSpec(b2.shape, lambda i, *p: (0, 0, 0)),
        ],
        out_specs=pl.BlockSpec((step_rows, HALF_D), lambda i, *p: (blk(i, *p), 0)),
        scratch_shapes=[pltpu.VMEM((MOE_WEIGHT_BUFS, D_MODEL, 2 * D_FF), jnp.float32),
                        pltpu.VMEM((MOE_WEIGHT_BUFS, D_FF, D_MODEL), jnp.float32),
                        pltpu.SemaphoreType.DMA((MOE_WEIGHT_BUFS, 2))],
    )
    return pl.pallas_call(
        _moe_kernel,
        out_shape=jax.ShapeDtypeStruct((n_rows, HALF_D), jnp.uint32),
        grid_spec=grid_spec,
        compiler_params=pltpu.CompilerParams(
            dimension_semantics=("arbitrary",), vmem_limit_bytes=LARGE_VMEM_LIMIT),
        name="moe_experts",
    )(*plan, xb, w1, b1, w2, b2)


def _sc_worker_id():
    return lax.axis_index("s") * SC_CORES + lax.axis_index("c")


def _sc_dispatch(hf, dest3, n_rows):
    n = hf.shape[0]
    chunks_per_worker = n // SC_CHUNK // SC_WORKERS
    mesh = plsc.VectorSubcoreMesh(core_axis_name="c", subcore_axis_name="s")

    @functools.partial(
        pl.kernel, mesh=mesh,
        out_type=jax.ShapeDtypeStruct((n_rows, HALF_D), hf.dtype),
        scratch_types=[pltpu.VMEM((2, TOP_K, SC_CHUNK), jnp.int32),
                       pltpu.VMEM((2, SC_CHUNK, HALF_D), hf.dtype),
                       pltpu.SemaphoreType.DMA((2,)),
                       pltpu.SemaphoreType.DMA((2,))],
        name="sc_dispatch")
    def run(hf_hbm, dest_hbm, xb_hbm, idx_v, rows_v, load_sem, scatter_sem):
        first = _sc_worker_id() * chunks_per_worker

        def load(j):
            slot = j % 2
            pltpu.sync_copy(dest_hbm.at[first + j], idx_v.at[slot])
            return pltpu.async_copy(hf_hbm.at[pl.ds((first + j) * SC_CHUNK, SC_CHUNK)],
                                    rows_v.at[slot], load_sem.at[slot])

        loads = {0: load(0)}
        scatters = {}
        for j in range(chunks_per_worker):
            slot = j % 2
            loads.pop(j).wait()
            scatters[j] = [pltpu.async_copy(rows_v.at[slot], xb_hbm.at[idx_v.at[slot, k]],
                                            scatter_sem.at[slot]) for k in range(TOP_K)]
            if j >= 1:
                for cp in scatters.pop(j - 1):
                    cp.wait()
            if j + 1 < chunks_per_worker:
                loads[j + 1] = load(j + 1)
        for cp in scatters.pop(chunks_per_worker - 1):
            cp.wait()

    return run(hf, dest3)


def _sc_collect_sum(y, dest3, gates16, first_chunk, n_chunks):
    C = SC_COLLECT_CHUNK
    n = n_chunks * C
    chunks_per_worker = n_chunks // SC_WORKERS
    lanes = 16
    mesh = plsc.VectorSubcoreMesh(core_axis_name="c", subcore_axis_name="s")

    @functools.partial(
        pl.kernel, mesh=mesh,
        out_type=jax.ShapeDtypeStruct((n, HALF_D), y.dtype),
        scratch_types=[pltpu.VMEM((chunks_per_worker, TOP_K, C), jnp.int32),
                       pltpu.VMEM((chunks_per_worker, TOP_K, C), jnp.uint32),
                       pltpu.VMEM((2, TOP_K, C, HALF_D), y.dtype),
                       pltpu.VMEM((2, C, HALF_D), y.dtype),
                       pltpu.SemaphoreType.DMA((2, TOP_K)),
                       pltpu.SemaphoreType.DMA((2,))],
        compiler_params=pltpu.CompilerParams(needs_layout_passes=False),
        name="sc_collect_sum")
    def run(y_hbm, dest_hbm, gate_hbm, out_hbm, idx_v, gate_v, rows_v, sum_v, gather_sem, write_sem):
        local0 = _sc_worker_id() * chunks_per_worker

        pltpu.sync_copy(dest_hbm.at[pl.ds(first_chunk + local0, chunks_per_worker)], idx_v)
        pltpu.sync_copy(gate_hbm.at[pl.ds(first_chunk + local0, chunks_per_worker)], gate_v)

        def gather_copies(i, slot):
            return [pltpu.make_async_copy(y_hbm.at[idx_v.at[i, k]], rows_v.at[slot, k],
                                          gather_sem.at[slot, k]) for k in range(TOP_K)]

        def write_copy(i, slot):
            return pltpu.make_async_copy(sum_v.at[slot], out_hbm.at[pl.ds((local0 + i) * C, C)],
                                         write_sem.at[slot])

        def fetch(i, slot):
            for cp in gather_copies(i, slot):
                cp.start()

        def reduce_chunk(i, slot):
            for cp in gather_copies(i, slot):
                cp.wait()

            @pl.when(i >= 2)
            def _():
                write_copy(i - 2, slot).wait()

            @plsc.parallel_loop(0, C)
            def _(j):
                lane_j = jnp.full((lanes,), j, jnp.int32)
                g = [plsc.bitcast(gate_v[i, k, :].at[lane_j].get(mode="promise_in_bounds"),
                                  jnp.bfloat16) for k in range(TOP_K)]
                for c in range(HALF_D // lanes):
                    sl = pl.ds(c * lanes, lanes)
                    acc = None
                    for k in range(TOP_K):
                        term = plsc.bitcast(rows_v[slot, k, j, sl], jnp.bfloat16) * g[k]
                        acc = term if acc is None else acc + term
                    sum_v[slot, j, sl] = plsc.bitcast(acc, y.dtype)

            write_copy(i, slot).start()

        fetch(0, 0)

        @pl.loop(0, chunks_per_worker, step=2)
        def _(i):
            fetch(i + 1, 1)
            reduce_chunk(i, 0)

            @pl.when(i + 2 < chunks_per_worker)
            def _():
                fetch(i + 2, 0)

            reduce_chunk(i + 1, 1)

        write_copy(chunks_per_worker - 2, 0).wait()
        write_copy(chunks_per_worker - 1, 1).wait()

    return run(y, dest3, gates16)


def _combine_sum_kernel(x1_ref, ys_ref, o_ref):
    lo, hi = _unpack_halves(ys_ref[...])
    o_ref[:, :HALF_D] = x1_ref[:, :HALF_D] + lo
    o_ref[:, HALF_D:] = x1_ref[:, HALF_D:] + hi


def _combine_sum(acc, ysum, split):
    n = acc.shape[0]
    rows = PROJ_ROWS
    steps = ysum.shape[0] // rows
    first = split * steps
    return pl.pallas_call(
        _combine_sum_kernel,
        out_shape=jax.ShapeDtypeStruct((n, D_MODEL), jnp.float32),
        grid=(steps,),
        in_specs=[pl.BlockSpec((rows, D_MODEL), lambda i: (first + i, 0)),
                  pl.BlockSpec((rows, HALF_D), lambda i: (i, 0))],
        out_specs=pl.BlockSpec((rows, D_MODEL), lambda i: (first + i, 0)),
        input_output_aliases={0: 0},
        compiler_params=pltpu.CompilerParams(
            dimension_semantics=("arbitrary",), vmem_limit_bytes=VMEM_LIMIT),
        name="moe_combine",
    )(acc, ysum)


def _layer(x2, batch, seq, attn_norm_g, w_in, a_q_g, a_k_g, a_sinks, b_q_g, b_k_g, w_out,
           ffn_norm_g, w_router, b_router, w1, b1, w2, b2):
    n = x2.shape[0]
    slopes = _alibi_slopes()
    q_scale = HEAD_DIM ** -0.5 * LOG2E
    reps = MXU_DIM // HEAD_DIM
    gains = jnp.stack([jnp.tile(a_q_g, reps) * q_scale, jnp.tile(a_k_g, reps),
                       jnp.tile(b_q_g, reps) * q_scale, jnp.tile(b_k_g, reps)]).astype(jnp.float32)

    proj = _in_proj(x2, attn_norm_g.reshape(1, -1), w_in, gains, batch, seq)
    qa, ka, va = proj[:3]
    nb = len(B_DILS)
    qbs, kbs, vbs = proj[3:3 + nb], proj[3 + nb:3 + 2 * nb], proj[3 + 2 * nb:]

    bias_a = _bias_tables(slopes[:A_Q_HEADS], A_STACK_HEADS, A_HALF_WINDOW, 1, Q_TILE + 2 * A_HALF_WINDOW)
    sink_col = jnp.repeat(a_sinks.astype(jnp.float32) * LOG2E, Q_TILE).reshape(
        A_Q_HEADS // A_STACK_HEADS, A_STACK_HEADS * Q_TILE, 1)
    as_seqs = lambda a: a.reshape(batch, seq, a.shape[-1])
    out_a = _banded_attention(as_seqs(qa), as_seqs(ka), as_seqs(va), bias_a, half_w=A_HALF_WINDOW,
                              sink=sink_col, name="attn_a")[0].reshape(n, A_Q_W)

    outs_b, lses_b = [], []
    for bi, (window, dil) in enumerate(B_BRANCHES):
        half_w = window // (2 * dil)
        bias_b = _bias_tables(slopes[A_Q_HEADS:], B_STACK_HEADS, half_w, dil, Q_TILE + 2 * half_w)
        L = seq // dil
        to_seqs = lambda a: a.reshape(batch * dil, L, a.shape[-1])
        o, lse = _banded_attention(to_seqs(qbs[bi]), to_seqs(kbs[bi]), to_seqs(vbs[bi]), bias_b,
                                   half_w=half_w, want_lse=True, name=f"attn_b_d{dil}")
        if dil == 1:
            outs_b.append(o.reshape(n, B_W))
            lses_b.append(lse.reshape(n, LANES))
        else:
            outs_b.append(o.reshape(batch, dil, L, B_W))
            lses_b.append(lse.reshape(batch, dil, L, LANES))

    wr_t = w_router.T.astype(jnp.float32)
    wr_hi = wr_t.astype(jnp.bfloat16)
    wr_lo = (wr_t - wr_hi.astype(jnp.float32)).astype(jnp.bfloat16)
    wr = jnp.concatenate([wr_hi, wr_lo], axis=0)
    br = jnp.broadcast_to(b_router.astype(jnp.float32)[:, None], (N_EXPERTS, LANES))
    x1, hf, topi, gates, ranks, counts = _out_proj_router(
        out_a, outs_b, lses_b, x2, w_out, ffn_norm_g.reshape(1, -1), wr, br, seq)

    g = MOE_ROWS
    nk = n * TOP_K
    step_rows = MOE_STEP_BLOCKS * g
    n_rows = -(-(nk + N_EXPERTS * g) // step_rows) * step_rows
    cnt = counts[:, 0]
    pcnt = (cnt + g - 1) // g * g
    pends = jnp.cumsum(pcnt)
    pstarts = pends - pcnt
    experts = jnp.arange(N_EXPERTS, dtype=jnp.int32)
    start_of = jnp.sum(jnp.where(topi[:, :, None] == experts, pstarts, 0), axis=-1)
    dest = (start_of + ranks).astype(jnp.int32)
    plan = _moe_plan(pends, n_rows // g)
    dest3 = dest.reshape(TOP_K, n // SC_CHUNK, SC_CHUNK).transpose(1, 0, 2)

    xb = _sc_dispatch(hf, dest3, n_rows)
    y = _moe_experts(plan, xb, w1, b1[:, None, :], w2, b2[:, None, :])
    smallest_range = max(PROJ_ROWS, SC_COLLECT_CHUNK * SC_WORKERS)
    n_splits = max(1, min(COMBINE_SPLITS, n // smallest_range))
    per_split = n // n_splits
    chunked = lambda a: a.reshape(TOP_K, n // SC_COLLECT_CHUNK, SC_COLLECT_CHUNK).transpose(1, 0, 2)
    dest_c = chunked(dest)
    gate_bits = lax.bitcast_convert_type(gates.astype(jnp.bfloat16), jnp.uint16).astype(jnp.uint32)
    gate_words = gate_bits | (gate_bits << 16)
    gates_c = chunked(gate_words)
    chunks_per_split = per_split // SC_COLLECT_CHUNK
    out = x1
    for s in range(n_splits):
        ysum = _sc_collect_sum(y, dest_c, gates_c, s * chunks_per_split, chunks_per_split)
        out = _combine_sum(out, ysum, s)
    return out


def kernel(x, attn_norm_g, w_in, a_q_norm_g, a_k_norm_g, a_sinks, b_q_norm_g, b_k_norm_g, w_out,
           ffn_norm_g, w_router, b_router, w1, b1, w2, b2):
    batch, seq, d = x.shape
    x2 = x.reshape(batch * seq, d)
    for i in range(attn_norm_g.shape[0]):
        x2 = _layer(x2, batch, seq, attn_norm_g[i], w_in[i], a_q_norm_g[i], a_k_norm_g[i],
                    a_sinks[i], b_q_norm_g[i], b_k_norm_g[i], w_out[i], ffn_norm_g[i],
                    w_router[i], b_router[i], w1[i], b1[i], w2[i], b2[i])
    return x2.reshape(batch, seq, d)
```

```python
import functools

import jax
import jax.numpy as jnp
import numpy as np
from jax import lax
from jax.experimental import pallas as pl
from jax.experimental.pallas import tpu as pltpu
from jax.experimental.pallas import tpu_sc as plsc

D_MODEL = 1024
HALF_D = D_MODEL // 2
HEAD_DIM = 64
LANES = 128
MXU_DIM = 256
A_Q_HEADS = 8
A_KV_HEADS = 2
B_HEADS = 8
A_HALF_WINDOW = 128
B_BRANCHES = ((128, 1), (512, 4), (2048, 16))
B_DILS = tuple(d for _, d in B_BRANCHES)
RESIDUE_STRIDE = 4
N_ALIBI_HEADS = 16
A_Q_W = A_Q_HEADS * HEAD_DIM
A_KV_W = A_KV_HEADS * HEAD_DIM
B_W = B_HEADS * HEAD_DIM
N_EXPERTS = 32
TOP_K = 4
D_FF = 1024
SWIGLU_ALPHA = 1.702
SWIGLU_LIMIT = 7.0
NORM_EPS = 1e-5
MASK_VALUE = -1e30
LOG2E = 1.4426950408889634

Q_TILE = 128
ATTN_STEP_ROWS = 2048
A_STACK_HEADS = 4
B_STACK_HEADS = 2
PROJ_ROWS = 1024
IN_PROJ_ROWS = 1024
MOE_ROWS = 512
FF_CHUNK = 512
MOE_STEP_BLOCKS = 2
MOE_WEIGHT_BUFS = 3
VMEM_LIMIT = 48 * 1024 * 1024
LARGE_VMEM_LIMIT = 58 * 1024 * 1024
SC_CORES = 2
SC_SUBCORES = 16
SC_WORKERS = SC_CORES * SC_SUBCORES
SC_CHUNK = 64
SC_COLLECT_CHUNK = 16
COMBINE_SPLITS = 2


def _pack_halves(v):
    lo = v[:, :HALF_D].astype(jnp.bfloat16).astype(jnp.float32)
    hi = v[:, HALF_D:].astype(jnp.bfloat16).astype(jnp.float32)
    return (pltpu.bitcast(lo, jnp.uint32) >> 16) | pltpu.bitcast(hi, jnp.uint32)


def _unpack_halves(w):
    lo = pltpu.bitcast(w << 16, jnp.float32)
    hi = pltpu.bitcast(w & jnp.uint32(0xFFFF0000), jnp.float32)
    return lo, hi


def _alibi_slopes():
    return np.exp2(-8.0 * np.arange(1, N_ALIBI_HEADS + 1, dtype=np.float32) / N_ALIBI_HEADS).astype(np.float32)


def _bias_tables(head_slopes, heads_per_group, half_w, dist_scale, tk):
    i = np.arange(Q_TILE)[:, None]
    j = np.arange(tk)[None, :]
    tabs = []
    for shift in (0, half_w, tk - Q_TILE):
        dist = np.abs(j - shift - i)
        valid = dist <= half_w
        per_head = []
        for sl in head_slopes:
            b = (-np.float64(sl) * LOG2E * (dist * dist_scale)).astype(np.float32)
            per_head.append(np.where(valid, b, np.float32(MASK_VALUE)).astype(np.float32))
        t = np.stack(per_head).reshape(-1, heads_per_group * Q_TILE, tk)
        tabs.append(t)
    return jnp.asarray(np.stack(tabs))


def _in_proj_kernel(x_ref, g_ref, w_hbm, gains_ref, qa_ref, ka_ref, va_ref, *rest):
    b_refs, (scr_ref, scr2_ref, w_ref, stage_ref, sem) = rest[:-5], rest[-5:]

    @pl.when(pl.program_id(0) == 0)
    def _():
        width = stage_ref.shape[1]
        for c0 in range(0, w_ref.shape[1], width):
            cp = pltpu.make_async_copy(w_hbm.at[:, c0:c0 + width], stage_ref, sem)
            cp.start()
            cp.wait()
            w_ref[:, c0:c0 + width] = stage_ref[...].astype(jnp.bfloat16)

    x = x_ref[...]
    xn = x * lax.rsqrt(jnp.mean(x * x, axis=-1, keepdims=True) + NORM_EPS) * g_ref[...]
    xn = xn.astype(jnp.bfloat16)
    r = lax.broadcasted_iota(jnp.int32, (MXU_DIM, MXU_DIM), 0) // HEAD_DIM
    c = lax.broadcasted_iota(jnp.int32, (MXU_DIM, MXU_DIM), 1) // HEAD_DIM
    blockdiag = jnp.where(r == c, 1.0, 0.0).astype(jnp.bfloat16)

    def head_rms(sec, gain_row):
        width = sec.shape[1]
        parts = []
        step = min(width, MXU_DIM)
        for j in range(width // step):
            p = sec[:, j * step:(j + 1) * step]
            ss = jnp.dot((p * p).astype(jnp.bfloat16), blockdiag[:step, :step],
                         preferred_element_type=jnp.float32)
            parts.append(p * lax.rsqrt(ss * (1.0 / HEAD_DIM) + NORM_EPS)
                         * gains_ref[gain_row:gain_row + 1, :step])
        return parts

    def project(col0, width, gain_row):
        sec = jnp.dot(xn, w_ref[:, col0:col0 + width], preferred_element_type=jnp.float32)
        return [sec] if gain_row is None else head_rms(sec, gain_row)

    def store(out_ref, parts):
        w = parts[0].shape[1]
        for j, p in enumerate(parts):
            out_ref[:, j * w:(j + 1) * w] = p.astype(out_ref.dtype)

    def per_kv_head(p):
        lane = lax.broadcasted_iota(jnp.int32, p.shape, 1)
        swapped = pltpu.roll(p, HEAD_DIM, axis=1)
        low = lane < HEAD_DIM
        return [jnp.where(low, p, swapped), jnp.where(low, swapped, p)]

    store(qa_ref, project(0, A_Q_W, 0))
    kva = project(A_Q_W, 2 * A_KV_W, None)[0]
    store(ka_ref, per_kv_head(head_rms(kva[:, :A_KV_W], 1)[0]))
    store(va_ref, per_kv_head(kva[:, A_KV_W:]))

    rows = x_ref.shape[0]
    col0 = A_Q_W + 2 * A_KV_W
    for t, gain_row in enumerate((2, 3, None)):
        parts = project(col0 + t * B_W, B_W, gain_row)
        sec = jnp.concatenate(parts, axis=-1) if len(parts) > 1 else parts[0]
        for j in range(B_W // LANES):
            scr_ref[j] = sec[:, j * LANES:(j + 1) * LANES]
        n_lane_chunks = B_W // LANES
        prev_dil = 1
        for bi, dil in enumerate(B_DILS):
            out_ref = b_refs[t * len(B_DILS) + bi]
            if dil == 1:
                out_ref[...] = sec.astype(out_ref.dtype)
                continue
            assert dil == prev_dil * RESIDUE_STRIDE
            last = dil == B_DILS[-1]
            for res in range(dil):
                r_prev, r_sub = res % prev_dil, res // prev_dil
                for j in range(n_lane_chunks):
                    if prev_dil == 1:
                        v = scr_ref[j, pl.ds(r_sub, rows // dil, stride=RESIDUE_STRIDE), :]
                    else:
                        v = scr2_ref[r_prev * n_lane_chunks + j,
                                     pl.ds(r_sub, rows // dil, stride=RESIDUE_STRIDE), :]
                    out_ref[0, res, :, j * LANES:(j + 1) * LANES] = v.astype(out_ref.dtype)
                    if not last:
                        scr2_ref[res * n_lane_chunks + j] = v
            prev_dil = dil


def _in_proj(x2, g, w_in, gains, batch, seq):
    n = x2.shape[0]
    rows = IN_PROJ_ROWS
    steps = seq // rows
    a_widths = (A_Q_W, 2 * A_KV_W, 2 * A_KV_W)
    out_shape = [jax.ShapeDtypeStruct((n, w), jnp.bfloat16) for w in a_widths]
    out_specs = [pl.BlockSpec((rows, w), lambda i: (i, 0)) for w in a_widths]
    for _ in range(3):
        for dil in B_DILS:
            if dil == 1:
                out_shape.append(jax.ShapeDtypeStruct((n, B_W), jnp.bfloat16))
                out_specs.append(pl.BlockSpec((rows, B_W), lambda i: (i, 0)))
            else:
                out_shape.append(jax.ShapeDtypeStruct((batch, dil, seq // dil, B_W), jnp.bfloat16))
                out_specs.append(pl.BlockSpec((1, dil, rows // dil, B_W),
                                              lambda i: (i // steps, 0, i % steps, 0)))
    return pl.pallas_call(
        _in_proj_kernel,
        out_shape=out_shape,
        grid=(n // rows,),
        in_specs=[
            pl.BlockSpec((rows, D_MODEL), lambda i: (i, 0)),
            pl.BlockSpec((1, D_MODEL), lambda i: (0, 0)),
            pl.BlockSpec(memory_space=pl.ANY),
            pl.BlockSpec(gains.shape, lambda i: (0, 0)),
        ],
        out_specs=out_specs,
        scratch_shapes=[pltpu.VMEM((B_W // LANES, rows, LANES), jnp.float32),
                        pltpu.VMEM((RESIDUE_STRIDE * B_W // LANES, rows // RESIDUE_STRIDE, LANES),
                                   jnp.float32),
                        pltpu.VMEM(w_in.shape, jnp.bfloat16),
                        pltpu.VMEM((w_in.shape[0], w_in.shape[1] // 3), jnp.float32),
                        pltpu.SemaphoreType.DMA],
        compiler_params=pltpu.CompilerParams(
            dimension_semantics=("arbitrary",), vmem_limit_bytes=LARGE_VMEM_LIMIT),
        name="in_proj",
    )(x2, g, w_in, gains)


def _attn_kernel(*refs, n_chunks, kv_chunks, heads_per_stack, tk, half_w, seq_len, rows, has_sink,
                 want_lse):
    it = iter(refs)
    q_ref, k_ref, v_ref, bias_ref = next(it), next(it), next(it), next(it)
    sink_ref = next(it) if has_sink else None
    o_ref = next(it)
    lse_ref = next(it) if want_lse else None

    n_tiles = seq_len // Q_TILE
    tiles_per_step = rows // Q_TILE
    chunks_per_group = n_chunks // kv_chunks
    assert (2 * chunks_per_group) % heads_per_stack == 0
    step = pl.program_id(1)
    lane = lax.broadcasted_iota(jnp.int32, (Q_TILE, LANES), 1)
    low_half = lane < HEAD_DIM
    ones = jnp.ones((tk, LANES), jnp.bfloat16)

    for sq, t in [(a, b) for a in range(q_ref.shape[0]) for b in range(tiles_per_step)]:
        tile = step * tiles_per_step + t
        q0 = tile * Q_TILE
        kv_rows = k_ref.shape[1]
        kv_row0 = jnp.clip(step * rows - half_w, 0, seq_len - kv_rows)
        start = pl.multiple_of(jnp.clip(q0 - half_w, 0, seq_len - tk) - kv_row0, HEAD_DIM)
        variant = jnp.where(tile == 0, 0, jnp.where(tile == n_tiles - 1, 2, 1))
        r0 = t * Q_TILE
        lse_tile = jnp.zeros((Q_TILE, LANES), jnp.float32)
        head_o, head_lse = {}, {}
        for g in range(2 * n_chunks // heads_per_stack):
            heads = range(g * heads_per_stack, (g + 1) * heads_per_stack)
            kv = (heads[0] // 2) // chunks_per_group
            kc = k_ref[sq, pl.ds(start, tk), kv * LANES:(kv + 1) * LANES]
            vc = v_ref[sq, pl.ds(start, tk), kv * LANES:(kv + 1) * LANES]
            v_aug = jnp.concatenate([vc, ones], axis=1)
            q_parts = []
            for h in heads:
                c = h // 2
                q2 = q_ref[sq, r0:r0 + Q_TILE, c * LANES:(c + 1) * LANES]
                keep = low_half if h % 2 == 0 else ~low_half
                q_parts.append(jnp.where(keep, q2, jnp.zeros_like(q2)))
            qs = q_parts[0] if len(q_parts) == 1 else jnp.concatenate(q_parts, axis=0)
            s = lax.dot_general(qs, kc, (((1,), (1,)), ((), ())),
                                preferred_element_type=jnp.float32)
            s = s + bias_ref[variant, g]
            m = jnp.max(s, axis=-1, keepdims=True)
            if has_sink:
                m = jnp.maximum(m, sink_ref[g])
            p = jnp.exp2(s - m)
            ov = jnp.dot(p.astype(jnp.bfloat16), v_aug, preferred_element_type=jnp.float32)
            o, l = ov[:, :LANES], ov[:, LANES:]
            if has_sink:
                l = l + jnp.exp2(sink_ref[g] - m)
            o = o * (1.0 / l)
            if want_lse:
                lse = m + jnp.log(l) * LOG2E
            for idx, h in enumerate(heads):
                head_o[h] = o[idx * Q_TILE:(idx + 1) * Q_TILE]
                if want_lse:
                    head_lse[h] = lse[idx * Q_TILE:(idx + 1) * Q_TILE]
                if h % 2 == 1:
                    c = h // 2
                    o2 = jnp.where(low_half, head_o.pop(h - 1), head_o.pop(h))
                    o_ref[sq, r0:r0 + Q_TILE, c * LANES:(c + 1) * LANES] = o2.astype(o_ref.dtype)
                    if want_lse:
                        lse_tile = jnp.where(lane == h - 1, head_lse.pop(h - 1),
                                             jnp.where(lane == h, head_lse.pop(h), lse_tile))
        if want_lse:
            lse_ref[sq, r0:r0 + Q_TILE, :] = lse_tile


def _banded_attention(q, k, v, bias, *, half_w, sink=None, want_lse=False, name):
    n_seq, L, qw = q.shape
    kw = k.shape[2]
    tk = Q_TILE + 2 * half_w
    rows = min(ATTN_STEP_ROWS, L)
    seqs = ATTN_STEP_ROWS // rows
    kv_rows = min(L, rows + 2 * half_w)

    def kv_index(s, i):
        row0 = pl.multiple_of(jnp.clip(i * rows - half_w, 0, L - kv_rows), HEAD_DIM)
        return s * seqs, row0, 0

    args = [q, k, v, bias]
    in_specs = [
        pl.BlockSpec((seqs, rows, qw), lambda s, i: (s, i, 0)),
        pl.BlockSpec((pl.Element(seqs), pl.Element(kv_rows), pl.Element(kw)), kv_index),
        pl.BlockSpec((pl.Element(seqs), pl.Element(kv_rows), pl.Element(kw)), kv_index),
        pl.BlockSpec(bias.shape, lambda s, i: (0, 0, 0, 0)),
    ]
    if sink is not None:
        args.append(sink)
        in_specs.append(pl.BlockSpec(sink.shape, lambda s, i: (0, 0, 0)))
    out_shape = [jax.ShapeDtypeStruct((n_seq, L, qw), jnp.bfloat16)]
    out_specs = [pl.BlockSpec((seqs, rows, qw), lambda s, i: (s, i, 0))]
    if want_lse:
        out_shape.append(jax.ShapeDtypeStruct((n_seq, L, LANES), jnp.float32))
        out_specs.append(pl.BlockSpec((seqs, rows, LANES), lambda s, i: (s, i, 0)))

    kern = functools.partial(
        _attn_kernel, n_chunks=qw // LANES, kv_chunks=kw // LANES,
        heads_per_stack=bias.shape[2] // Q_TILE, tk=tk, half_w=half_w, seq_len=L,
        rows=rows, has_sink=sink is not None, want_lse=want_lse)
    return pl.pallas_call(
        kern,
        out_shape=out_shape,
        grid=(n_seq // seqs, L // rows),
        in_specs=in_specs,
        out_specs=out_specs,
        compiler_params=pltpu.CompilerParams(
            dimension_semantics=("arbitrary", "arbitrary"), vmem_limit_bytes=VMEM_LIMIT),
        name=name,
    )(*args)


def _out_proj_router_kernel(*refs):
    nb = len(B_DILS)
    oa_ref = refs[0]
    o_refs = refs[1:1 + nb]
    lse_refs = refs[1 + nb:1 + 2 * nb]
    (x_ref, wo_ref, g_ref, wr_ref, br_ref,
     x1_ref, hf_ref, topi_ref, gate_ref, rank_ref, cnt_ref,
     tri_ref, carry_ref, so_ref, sl_ref, wo_bf_ref) = refs[1 + 2 * nb:]
    i = pl.program_id(0)
    rows = x_ref.shape[0]

    @pl.when(i == 0)
    def _():
        a = lax.broadcasted_iota(jnp.int32, (rows, rows), 0)
        b = lax.broadcasted_iota(jnp.int32, (rows, rows), 1)
        tri_ref[...] = jnp.where(a <= b, 1.0, 0.0).astype(jnp.bfloat16)
        carry_ref[...] = jnp.zeros_like(carry_ref)
        wo_bf_ref[...] = wo_ref[...].astype(jnp.bfloat16)

    outs, lses = [], []
    for bi, dil in enumerate(B_DILS):
        if dil == 1:
            outs.append(o_refs[bi][...].astype(jnp.float32))
            lses.append(lse_refs[bi][...])
        else:
            for res in range(dil):
                for j in range(B_W // LANES):
                    so_ref[bi, j, pl.ds(res, rows // dil, stride=dil), :] = (
                        o_refs[bi][0, res, :, j * LANES:(j + 1) * LANES].astype(jnp.float32))
                sl_ref[bi, pl.ds(res, rows // dil, stride=dil), :] = lse_refs[bi][0, res]
            outs.append(jnp.concatenate([so_ref[bi, j] for j in range(B_W // LANES)], axis=-1))
            lses.append(sl_ref[bi])

    mx = functools.reduce(jnp.maximum, lses)
    es = [jnp.exp2(l - mx) for l in lses]
    inv = 1.0 / functools.reduce(lambda a, b: a + b, es)
    eh = lax.broadcasted_iota(jnp.int32, (LANES, B_W), 0)
    ej = lax.broadcasted_iota(jnp.int32, (LANES, B_W), 1) // HEAD_DIM
    expand = jnp.where(eh == ej, 1.0, 0.0).astype(jnp.bfloat16)
    ob = jnp.zeros((rows, B_W), jnp.float32)
    for e, o in zip(es, outs):
        w = e * inv
        wide = jnp.dot(w.astype(jnp.bfloat16), expand, preferred_element_type=jnp.float32)
        ob = ob + wide * o

    attn = jnp.concatenate([oa_ref[...], ob.astype(jnp.bfloat16)], axis=-1)
    x1 = x_ref[...] + jnp.dot(attn, wo_bf_ref[...], preferred_element_type=jnp.float32)
    x1_ref[...] = x1
    hf = x1 * lax.rsqrt(jnp.mean(x1 * x1, axis=-1, keepdims=True) + NORM_EPS) * g_ref[...]
    hf_hi = hf.astype(jnp.bfloat16)
    hf_ref[...] = _pack_halves(hf)
    hf_lo = (hf - hf_hi.astype(jnp.float32)).astype(jnp.bfloat16)

    nt = (((1,), (1,)), ((), ()))
    lg_hi = lax.dot_general(wr_ref[...], hf_hi, nt, preferred_element_type=jnp.float32)
    lg_lo = lax.dot_general(wr_ref[0:N_EXPERTS, :], hf_lo, nt, preferred_element_type=jnp.float32)
    logits = lg_hi[0:N_EXPERTS] + lg_hi[N_EXPERTS:] + lg_lo + br_ref[:, 0:1]


    eidx = lax.broadcasted_iota(jnp.int32, (N_EXPERTS, rows), 0)
    work = logits
    vals, sels = [], []
    for k in range(TOP_K):
        mk = jnp.max(work, axis=0, keepdims=True)
        ik = jnp.min(jnp.where(work == mk, eidx, N_EXPERTS), axis=0, keepdims=True)
        sel = eidx == ik
        work = jnp.where(sel, -jnp.inf, work)
        vals.append(mk)
        sels.append(sel)
        topi_ref[k:k + 1, :] = ik
    exps = [jnp.exp(vk - vals[0]) for vk in vals]
    denom = exps[0] + exps[1] + exps[2] + exps[3]
    ginv = 1.0 / denom
    for k in range(TOP_K):
        gate_ref[k:k + 1, :] = exps[k] * ginv

    onehot = jnp.zeros((N_EXPERTS, rows), jnp.float32)
    for sel in sels:
        onehot = onehot + jnp.where(sel, 1.0, 0.0)
    incl = jnp.dot(onehot.astype(jnp.bfloat16), tri_ref[...], preferred_element_type=jnp.float32)
    before = incl - onehot + carry_ref[:, 0:1]
    for k in range(TOP_K):
        rk = jnp.sum(jnp.where(sels[k], before, 0.0), axis=0, keepdims=True)
        rank_ref[k:k + 1, :] = rk.astype(jnp.int32)
    carry = carry_ref[...] + jnp.sum(onehot, axis=1, keepdims=True)
    carry_ref[...] = carry
    cnt_ref[...] = carry.astype(jnp.int32)


def _out_proj_router(oa, outs_b, lses_b, x2, wo_bf, g, wr, br, seq):
    n = x2.shape[0]
    rows = PROJ_ROWS
    steps = seq // rows
    row_spec = lambda w: pl.BlockSpec((rows, w), lambda i: (i, 0))
    full = lambda a: pl.BlockSpec(a.shape, lambda i: (0,) * a.ndim)
    col_spec = pl.BlockSpec((TOP_K, rows), lambda i: (0, i))

    def branch_spec(dil, w):
        if dil == 1:
            return row_spec(w)
        return pl.BlockSpec((1, dil, rows // dil, w), lambda i: (i // steps, 0, i % steps, 0))

    in_specs = ([row_spec(A_Q_W)]
                + [branch_spec(d, B_W) for d in B_DILS]
                + [branch_spec(d, LANES) for d in B_DILS]
                + [row_spec(D_MODEL), full(wo_bf), full(g), full(wr), full(br)])
    return pl.pallas_call(
        _out_proj_router_kernel,
        out_shape=[
            jax.ShapeDtypeStruct((n, D_MODEL), jnp.float32),
            jax.ShapeDtypeStruct((n, HALF_D), jnp.uint32),
            jax.ShapeDtypeStruct((TOP_K, n), jnp.int32),
            jax.ShapeDtypeStruct((TOP_K, n), jnp.float32),
            jax.ShapeDtypeStruct((TOP_K, n), jnp.int32),
            jax.ShapeDtypeStruct((N_EXPERTS, LANES), jnp.int32),
        ],
        grid=(n // rows,),
        in_specs=in_specs,
        out_specs=[row_spec(D_MODEL), row_spec(HALF_D), col_spec, col_spec, col_spec,
                   pl.BlockSpec((N_EXPERTS, LANES), lambda i: (0, 0))],
        scratch_shapes=[pltpu.VMEM((rows, rows), jnp.bfloat16),
                        pltpu.VMEM((N_EXPERTS, LANES), jnp.float32),
                        pltpu.VMEM((len(B_DILS), B_W // LANES, rows, LANES), jnp.float32),
                        pltpu.VMEM((len(B_DILS), rows, LANES), jnp.float32),
                        pltpu.VMEM(wo_bf.shape, jnp.bfloat16)],
        compiler_params=pltpu.CompilerParams(
            dimension_semantics=("arbitrary",), vmem_limit_bytes=LARGE_VMEM_LIMIT),
        name="out_proj_router",
    )(oa, *outs_b, *lses_b, x2, wo_bf, g, wr, br)


def _mxu_dot(a_bf, w_f32):
    return lax.dot_general(a_bf, w_f32, (((1,), (0,)), ((), ())), preferred_element_type=jnp.float32)


def _moe_kernel(blk_exp_ref, first_ref, slot_ref, next_exp_ref, n_used_ref,
                x_ref, w1_hbm, b1_ref, w2_hbm, b2_ref, y_ref, w1_buf, w2_buf, sem):
    step = pl.program_id(0)

    def weight_copies(expert, slot):
        return (pltpu.make_async_copy(w1_hbm.at[expert], w1_buf.at[slot], sem.at[slot, 0]),
                pltpu.make_async_copy(w2_hbm.at[expert], w2_buf.at[slot], sem.at[slot, 1]))

    @pl.when(step * MOE_STEP_BLOCKS < n_used_ref[0])
    def _():
        @pl.when(step == 0)
        def _():
            for cp in weight_copies(blk_exp_ref[0], slot_ref[0]):
                cp.start()

        for j in range(MOE_STEP_BLOCKS):
            i = step * MOE_STEP_BLOCKS + j

            @pl.when(first_ref[i] == 1)
            def _():
                slot = slot_ref[i]
                for cp in weight_copies(blk_exp_ref[i], slot):
                    cp.wait()

                @pl.when(next_exp_ref[i] >= 0)
                def _():
                    for cp in weight_copies(next_exp_ref[i], (slot + 1) % MOE_WEIGHT_BUFS):
                        cp.start()

        for j in range(MOE_STEP_BLOCKS):
            i = step * MOE_STEP_BLOCKS + j
            slot = slot_ref[i]
            expert = blk_exp_ref[i]
            rs = slice(j * MOE_ROWS, (j + 1) * MOE_ROWS)
            x = jnp.concatenate(_unpack_halves(x_ref[rs, :]), axis=-1).astype(jnp.bfloat16)
            acc = jnp.zeros((MOE_ROWS, D_MODEL), jnp.float32)
            for c in range(D_FF // FF_CHUNK):
                lo = c * FF_CHUNK
                glu = _mxu_dot(x, w1_buf[slot, :, lo:lo + FF_CHUNK]) + b1_ref[expert, :, lo:lo + FF_CHUNK]
                lin = (_mxu_dot(x, w1_buf[slot, :, D_FF + lo:D_FF + lo + FF_CHUNK])
                       + b1_ref[expert, :, D_FF + lo:D_FF + lo + FF_CHUNK])
                glu = jnp.minimum(glu, SWIGLU_LIMIT)
                lin = jnp.clip(lin, -SWIGLU_LIMIT, SWIGLU_LIMIT)
                act = glu * (1.0 / (1.0 + jnp.exp(-SWIGLU_ALPHA * glu))) * (lin + 1.0)
                acc = acc + _mxu_dot(act.astype(jnp.bfloat16), w2_buf[slot, lo:lo + FF_CHUNK, :])
            y_ref[rs, :] = _pack_halves(acc + b2_ref[expert])


def _moe_plan(pends, n_blk):
    g = MOE_ROWS
    blk_row0 = jnp.arange(n_blk, dtype=jnp.int32) * g
    blk_exp = jnp.minimum(jnp.sum(pends[None, :] <= blk_row0[:, None], axis=-1),
                          N_EXPERTS - 1).astype(jnp.int32)
    n_used = (pends[-1] // g).astype(jnp.int32)
    used = blk_row0 < pends[-1]
    prev_exp = jnp.concatenate([jnp.full((1,), -1, jnp.int32), blk_exp[:-1]])
    first = (used & (blk_exp != prev_exp)).astype(jnp.int32)
    slot = ((jnp.cumsum(first) - 1) % MOE_WEIGHT_BUFS).astype(jnp.int32)
    pstarts = jnp.concatenate([jnp.zeros((1,), pends.dtype), pends[:-1]])
    nonempty = pends > pstarts
    experts = jnp.arange(N_EXPERTS, dtype=jnp.int32)
    later = nonempty[None, :] & (experts[None, :] > experts[:, None])
    next_nonempty = jnp.min(jnp.where(later, experts[None, :], N_EXPERTS), axis=-1)
    next_nonempty = jnp.where(next_nonempty == N_EXPERTS, -1, next_nonempty).astype(jnp.int32)
    next_exp = jnp.sum(jnp.where(blk_exp[:, None] == experts[None, :], next_nonempty[None, :], 0),
                       axis=-1).astype(jnp.int32)
    return blk_exp, first, slot, next_exp, n_used.reshape(1)


def _moe_experts(plan, xb, w1, b1, w2, b2):
    n_rows = xb.shape[0]
    n_blk = n_rows // MOE_ROWS

    step_rows = MOE_STEP_BLOCKS * MOE_ROWS

    def blk(i, *p):
        return jnp.minimum(i, (p[-1][0] - 1) // MOE_STEP_BLOCKS)

    grid_spec = pltpu.PrefetchScalarGridSpec(
        num_scalar_prefetch=len(plan),
        grid=(n_blk // MOE_STEP_BLOCKS,),
        in_specs=[
            pl.BlockSpec((step_rows, HALF_D), lambda i, *p: (blk(i, *p), 0)),
            pl.BlockSpec(memory_space=pl.ANY),
            pl.BlockSpec(b1.shape, lambda i, *p: (0, 0, 0)),
            pl.BlockSpec(memory_space=pl.ANY),
            pl.BlockSpec(b2.shape, lambda i, *p: (0, 0, 0)),
        ],
        out_specs=pl.BlockSpec((step_rows, HALF_D), lambda i, *p: (blk(i, *p), 0)),
        scratch_shapes=[pltpu.VMEM((MOE_WEIGHT_BUFS, D_MODEL, 2 * D_FF), jnp.float32),
                        pltpu.VMEM((MOE_WEIGHT_BUFS, D_FF, D_MODEL), jnp.float32),
                        pltpu.SemaphoreType.DMA((MOE_WEIGHT_BUFS, 2))],
    )
    return pl.pallas_call(
        _moe_kernel,
        out_shape=jax.ShapeDtypeStruct((n_rows, HALF_D), jnp.uint32),
        grid_spec=grid_spec,
        compiler_params=pltpu.CompilerParams(
            dimension_semantics=("arbitrary",), vmem_limit_bytes=LARGE_VMEM_LIMIT),
        name="moe_experts",
    )(*plan, xb, w1, b1, w2, b2)


def _sc_worker_id():
    return lax.axis_index("s") * SC_CORES + lax.axis_index("c")


def _sc_dispatch(hf, dest3, n_rows):
    n = hf.shape[0]
    chunks_per_worker = n // SC_CHUNK // SC_WORKERS
    mesh = plsc.VectorSubcoreMesh(core_axis_name="c", subcore_axis_name="s")

    @functools.partial(
        pl.kernel, mesh=mesh,
        out_type=jax.ShapeDtypeStruct((n_rows, HALF_D), hf.dtype),
        scratch_types=[pltpu.VMEM((chunks_per_worker, TOP_K, SC_CHUNK), jnp.int32),
                       pltpu.VMEM((2, SC_CHUNK, HALF_D), hf.dtype),
                       pltpu.SemaphoreType.DMA((2,)),
                       pltpu.SemaphoreType.DMA((2,))],
        name="sc_dispatch")
    def run(hf_hbm, dest_hbm, xb_hbm, idx_v, rows_v, load_sem, scatter_sem):
        first = _sc_worker_id() * chunks_per_worker
        pltpu.sync_copy(dest_hbm.at[pl.ds(first, chunks_per_worker)], idx_v)

        def load(j):
            slot = j % 2
            return pltpu.async_copy(hf_hbm.at[pl.ds((first + j) * SC_CHUNK, SC_CHUNK)],
                                    rows_v.at[slot], load_sem.at[slot])

        loads = {0: load(0)}
        scatters = {}
        for j in range(chunks_per_worker):
            slot = j % 2
            loads.pop(j).wait()
            scatters[j] = [pltpu.async_copy(rows_v.at[slot], xb_hbm.at[idx_v.at[j, k]],
                                            scatter_sem.at[slot]) for k in range(TOP_K)]
            if j >= 1:
                for cp in scatters.pop(j - 1):
                    cp.wait()
            if j + 1 < chunks_per_worker:
                loads[j + 1] = load(j + 1)
        for cp in scatters.pop(chunks_per_worker - 1):
            cp.wait()

    return run(hf, dest3)


def _sc_collect_sum(y, dest3, gates16, first_chunk, n_chunks):
    C = SC_COLLECT_CHUNK
    n = n_chunks * C
    chunks_per_worker = n_chunks // SC_WORKERS
    lanes = 16
    mesh = plsc.VectorSubcoreMesh(core_axis_name="c", subcore_axis_name="s")

    @functools.partial(
        pl.kernel, mesh=mesh,
        out_type=jax.ShapeDtypeStruct((n, HALF_D), y.dtype),
        scratch_types=[pltpu.VMEM((chunks_per_worker, TOP_K, C), jnp.int32),
                       pltpu.VMEM((chunks_per_worker, TOP_K, C), jnp.uint32),
                       pltpu.VMEM((2, TOP_K, C, HALF_D), y.dtype),
                       pltpu.VMEM((2, C, HALF_D), y.dtype),
                       pltpu.SemaphoreType.DMA((2, TOP_K)),
                       pltpu.SemaphoreType.DMA((2,))],
        compiler_params=pltpu.CompilerParams(needs_layout_passes=False),
        name="sc_collect_sum")
    def run(y_hbm, dest_hbm, gate_hbm, out_hbm, idx_v, gate_v, rows_v, sum_v, gather_sem, write_sem):
        local0 = _sc_worker_id() * chunks_per_worker

        pltpu.sync_copy(dest_hbm.at[pl.ds(first_chunk + local0, chunks_per_worker)], idx_v)
        pltpu.sync_copy(gate_hbm.at[pl.ds(first_chunk + local0, chunks_per_worker)], gate_v)

        def gather_copies(i, slot):
            return [pltpu.make_async_copy(y_hbm.at[idx_v.at[i, k]], rows_v.at[slot, k],
                                          gather_sem.at[slot, k]) for k in range(TOP_K)]

        def write_copy(i, slot):
            return pltpu.make_async_copy(sum_v.at[slot], out_hbm.at[pl.ds((local0 + i) * C, C)],
                                         write_sem.at[slot])

        def fetch(i, slot):
            for cp in gather_copies(i, slot):
                cp.start()

        def reduce_chunk(i, slot):
            for cp in gather_copies(i, slot):
                cp.wait()

            @pl.when(i >= 2)
            def _():
                write_copy(i - 2, slot).wait()

            @plsc.parallel_loop(0, C)
            def _(j):
                lane_j = jnp.full((lanes,), j, jnp.int32)
                g = [plsc.bitcast(gate_v[i, k, :].at[lane_j].get(mode="promise_in_bounds"),
                                  jnp.bfloat16) for k in range(TOP_K)]
                for c in range(HALF_D // lanes):
                    sl = pl.ds(c * lanes, lanes)
                    acc = None
                    for k in range(TOP_K):
                        term = plsc.bitcast(rows_v[slot, k, j, sl], jnp.bfloat16) * g[k]
                        acc = term if acc is None else acc + term
                    sum_v[slot, j, sl] = plsc.bitcast(acc, y.dtype)

            write_copy(i, slot).start()

        fetch(0, 0)

        @pl.loop(0, chunks_per_worker, step=2)
        def _(i):
            fetch(i + 1, 1)
            reduce_chunk(i, 0)

            @pl.when(i + 2 < chunks_per_worker)
            def _():
                fetch(i + 2, 0)

            reduce_chunk(i + 1, 1)

        write_copy(chunks_per_worker - 2, 0).wait()
        write_copy(chunks_per_worker - 1, 1).wait()

    return run(y, dest3, gates16)


def _combine_sum_kernel(x1_ref, ys_ref, o_ref):
    lo, hi = _unpack_halves(ys_ref[...])
    o_ref[:, :HALF_D] = x1_ref[:, :HALF_D] + lo
    o_ref[:, HALF_D:] = x1_ref[:, HALF_D:] + hi


def _combine_sum(acc, ysum, split):
    n = acc.shape[0]
    rows = PROJ_ROWS
    steps = ysum.shape[0] // rows
    first = split * steps
    return pl.pallas_call(
        _combine_sum_kernel,
        out_shape=jax.ShapeDtypeStruct((n, D_MODEL), jnp.float32),
        grid=(steps,),
        in_specs=[pl.BlockSpec((rows, D_MODEL), lambda i: (first + i, 0)),
                  pl.BlockSpec((rows, HALF_D), lambda i: (i, 0))],
        out_specs=pl.BlockSpec((rows, D_MODEL), lambda i: (first + i, 0)),
        input_output_aliases={0: 0},
        compiler_params=pltpu.CompilerParams(
            dimension_semantics=("arbitrary",), vmem_limit_bytes=VMEM_LIMIT),
        name="moe_combine",
    )(acc, ysum)


def _layer(x2, batch, seq, attn_norm_g, w_in, a_q_g, a_k_g, a_sinks, b_q_g, b_k_g, w_out,
           ffn_norm_g, w_router, b_router, w1, b1, w2, b2):
    n = x2.shape[0]
    slopes = _alibi_slopes()
    q_scale = HEAD_DIM ** -0.5 * LOG2E
    reps = MXU_DIM // HEAD_DIM
    gains = jnp.stack([jnp.tile(a_q_g, reps) * q_scale, jnp.tile(a_k_g, reps),
                       jnp.tile(b_q_g, reps) * q_scale, jnp.tile(b_k_g, reps)]).astype(jnp.float32)

    proj = _in_proj(x2, attn_norm_g.reshape(1, -1), w_in, gains, batch, seq)
    qa, ka, va = proj[:3]
    nb = len(B_DILS)
    qbs, kbs, vbs = proj[3:3 + nb], proj[3 + nb:3 + 2 * nb], proj[3 + 2 * nb:]

    bias_a = _bias_tables(slopes[:A_Q_HEADS], A_STACK_HEADS, A_HALF_WINDOW, 1, Q_TILE + 2 * A_HALF_WINDOW)
    sink_col = jnp.repeat(a_sinks.astype(jnp.float32) * LOG2E, Q_TILE).reshape(
        A_Q_HEADS // A_STACK_HEADS, A_STACK_HEADS * Q_TILE, 1)
    as_seqs = lambda a: a.reshape(batch, seq, a.shape[-1])
    out_a = _banded_attention(as_seqs(qa), as_seqs(ka), as_seqs(va), bias_a, half_w=A_HALF_WINDOW,
                              sink=sink_col, name="attn_a")[0].reshape(n, A_Q_W)

    outs_b, lses_b = [], []
    for bi, (window, dil) in enumerate(B_BRANCHES):
        half_w = window // (2 * dil)
        bias_b = _bias_tables(slopes[A_Q_HEADS:], B_STACK_HEADS, half_w, dil, Q_TILE + 2 * half_w)
        L = seq // dil
        to_seqs = lambda a: a.reshape(batch * dil, L, a.shape[-1])
        o, lse = _banded_attention(to_seqs(qbs[bi]), to_seqs(kbs[bi]), to_seqs(vbs[bi]), bias_b,
                                   half_w=half_w, want_lse=True, name=f"attn_b_d{dil}")
        if dil == 1:
            outs_b.append(o.reshape(n, B_W))
            lses_b.append(lse.reshape(n, LANES))
        else:
            outs_b.append(o.reshape(batch, dil, L, B_W))
            lses_b.append(lse.reshape(batch, dil, L, LANES))

    wr_t = w_router.T.astype(jnp.float32)
    wr_hi = wr_t.astype(jnp.bfloat16)
    wr_lo = (wr_t - wr_hi.astype(jnp.float32)).astype(jnp.bfloat16)
    wr = jnp.concatenate([wr_hi, wr_lo], axis=0)
    br = jnp.broadcast_to(b_router.astype(jnp.float32)[:, None], (N_EXPERTS, LANES))
    x1, hf, topi, gates, ranks, counts = _out_proj_router(
        out_a, outs_b, lses_b, x2, w_out, ffn_norm_g.reshape(1, -1), wr, br, seq)

    g = MOE_ROWS
    nk = n * TOP_K
    step_rows = MOE_STEP_BLOCKS * g
    n_rows = -(-(nk + N_EXPERTS * g) // step_rows) * step_rows
    cnt = counts[:, 0]
    pcnt = (cnt + g - 1) // g * g
    pends = jnp.cumsum(pcnt)
    pstarts = pends - pcnt
    experts = jnp.arange(N_EXPERTS, dtype=jnp.int32)
    start_of = jnp.sum(jnp.where(topi[:, :, None] == experts, pstarts, 0), axis=-1)
    dest = (start_of + ranks).astype(jnp.int32)
    plan = _moe_plan(pends, n_rows // g)
    dest3 = dest.reshape(TOP_K, n // SC_CHUNK, SC_CHUNK).transpose(1, 0, 2)

    xb = _sc_dispatch(hf, dest3, n_rows)
    y = _moe_experts(plan, xb, w1, b1[:, None, :], w2, b2[:, None, :])
    smallest_range = max(PROJ_ROWS, SC_COLLECT_CHUNK * SC_WORKERS)
    n_splits = max(1, min(COMBINE_SPLITS, n // smallest_range))
    per_split = n // n_splits
    chunked = lambda a: a.reshape(TOP_K, n // SC_COLLECT_CHUNK, SC_COLLECT_CHUNK).transpose(1, 0, 2)
    dest_c = chunked(dest)
    gate_bits = lax.bitcast_convert_type(gates.astype(jnp.bfloat16), jnp.uint16).astype(jnp.uint32)
    gate_words = gate_bits | (gate_bits << 16)
    gates_c = chunked(gate_words)
    chunks_per_split = per_split // SC_COLLECT_CHUNK
    out = x1
    for s in range(n_splits):
        ysum = _sc_collect_sum(y, dest_c, gates_c, s * chunks_per_split, chunks_per_split)
        out = _combine_sum(out, ysum, s)
    return out


def kernel(x, attn_norm_g, w_in, a_q_norm_g, a_k_norm_g, a_sinks, b_q_norm_g, b_k_norm_g, w_out,
           ffn_norm_g, w_router, b_router, w1, b1, w2, b2):
    batch, seq, d = x.shape
    x2 = x.reshape(batch * seq, d)
    for i in range(attn_norm_g.shape[0]):
        x2 = _layer(x2, batch, seq, attn_norm_g[i], w_in[i], a_q_norm_g[i], a_k_norm_g[i],
                    a_sinks[i], b_q_norm_g[i], b_k_norm_g[i], w_out[i], ffn_norm_g[i],
                    w_router[i], b_router[i], w1[i], b1[i], w2[i], b2[i])
    return x2.reshape(batch, seq, d)
```

```python
import functools

import jax
import jax.numpy as jnp
import numpy as np
from jax import lax
from jax.experimental import pallas as pl
from jax.experimental.pallas import tpu as pltpu
from jax.experimental.pallas import tpu_sc as plsc

D_MODEL = 1024
HALF_D = D_MODEL // 2
HEAD_DIM = 64
LANES = 128
MXU_DIM = 256
A_Q_HEADS = 8
A_KV_HEADS = 2
B_HEADS = 8
A_HALF_WINDOW = 128
B_BRANCHES = ((128, 1), (512, 4), (2048, 16))
B_DILS = tuple(d for _, d in B_BRANCHES)
RESIDUE_STRIDE = 4
N_ALIBI_HEADS = 16
A_Q_W = A_Q_HEADS * HEAD_DIM
A_KV_W = A_KV_HEADS * HEAD_DIM
B_W = B_HEADS * HEAD_DIM
N_EXPERTS = 32
TOP_K = 4
D_FF = 1024
SWIGLU_ALPHA = 1.702
SWIGLU_LIMIT = 7.0
NORM_EPS = 1e-5
MASK_VALUE = -1e30
LOG2E = 1.4426950408889634

Q_TILE = 128
ATTN_STEP_ROWS = 2048
A_STACK_HEADS = 4
B_STACK_HEADS = 2
PROJ_ROWS = 1024
IN_PROJ_ROWS = 1024
MOE_ROWS = 512
FF_CHUNK = 512
MOE_STEP_BLOCKS = 2
MOE_WEIGHT_BUFS = 3
VMEM_LIMIT = 48 * 1024 * 1024
LARGE_VMEM_LIMIT = 58 * 1024 * 1024
SC_CORES = 2
SC_SUBCORES = 16
SC_WORKERS = SC_CORES * SC_SUBCORES
SC_CHUNK = 64
SC_COLLECT_CHUNK = 16
COMBINE_SPLITS = 4


def _pack_halves(v):
    lo = v[:, :HALF_D].astype(jnp.bfloat16).astype(jnp.float32)
    hi = v[:, HALF_D:].astype(jnp.bfloat16).astype(jnp.float32)
    return (pltpu.bitcast(lo, jnp.uint32) >> 16) | pltpu.bitcast(hi, jnp.uint32)


def _unpack_halves(w):
    lo = pltpu.bitcast(w << 16, jnp.float32)
    hi = pltpu.bitcast(w & jnp.uint32(0xFFFF0000), jnp.float32)
    return lo, hi


def _alibi_slopes():
    return np.exp2(-8.0 * np.arange(1, N_ALIBI_HEADS + 1, dtype=np.float32) / N_ALIBI_HEADS).astype(np.float32)


def _bias_tables(head_slopes, heads_per_group, half_w, dist_scale, tk):
    i = np.arange(Q_TILE)[:, None]
    j = np.arange(tk)[None, :]
    tabs = []
    for shift in (0, half_w, tk - Q_TILE):
        dist = np.abs(j - shift - i)
        valid = dist <= half_w
        per_head = []
        for sl in head_slopes:
            b = (-np.float64(sl) * LOG2E * (dist * dist_scale)).astype(np.float32)
            per_head.append(np.where(valid, b, np.float32(MASK_VALUE)).astype(np.float32))
        t = np.stack(per_head).reshape(-1, heads_per_group * Q_TILE, tk)
        tabs.append(t)
    return jnp.asarray(np.stack(tabs))


def _in_proj_kernel(x_ref, g_ref, w_hbm, gains_ref, qa_ref, ka_ref, va_ref, *rest):
    b_refs, (scr_ref, scr2_ref, w_ref, stage_ref, sem) = rest[:-5], rest[-5:]

    @pl.when(pl.program_id(0) == 0)
    def _():
        width = stage_ref.shape[1]
        for c0 in range(0, w_ref.shape[1], width):
            cp = pltpu.make_async_copy(w_hbm.at[:, c0:c0 + width], stage_ref, sem)
            cp.start()
            cp.wait()
            w_ref[:, c0:c0 + width] = stage_ref[...].astype(jnp.bfloat16)

    x = x_ref[...]
    xn = x * lax.rsqrt(jnp.mean(x * x, axis=-1, keepdims=True) + NORM_EPS) * g_ref[...]
    xn = xn.astype(jnp.bfloat16)
    r = lax.broadcasted_iota(jnp.int32, (MXU_DIM, MXU_DIM), 0) // HEAD_DIM
    c = lax.broadcasted_iota(jnp.int32, (MXU_DIM, MXU_DIM), 1) // HEAD_DIM
    blockdiag = jnp.where(r == c, 1.0, 0.0).astype(jnp.bfloat16)

    def head_rms(sec, gain_row):
        width = sec.shape[1]
        parts = []
        step = min(width, MXU_DIM)
        for j in range(width // step):
            p = sec[:, j * step:(j + 1) * step]
            ss = jnp.dot((p * p).astype(jnp.bfloat16), blockdiag[:step, :step],
                         preferred_element_type=jnp.float32)
            parts.append(p * lax.rsqrt(ss * (1.0 / HEAD_DIM) + NORM_EPS)
                         * gains_ref[gain_row:gain_row + 1, :step])
        return parts

    def project(col0, width, gain_row):
        sec = jnp.dot(xn, w_ref[:, col0:col0 + width], preferred_element_type=jnp.float32)
        return [sec] if gain_row is None else head_rms(sec, gain_row)

    def store(out_ref, parts):
        w = parts[0].shape[1]
        for j, p in enumerate(parts):
            out_ref[:, j * w:(j + 1) * w] = p.astype(out_ref.dtype)

    def per_kv_head(p):
        lane = lax.broadcasted_iota(jnp.int32, p.shape, 1)
        swapped = pltpu.roll(p, HEAD_DIM, axis=1)
        low = lane < HEAD_DIM
        return [jnp.where(low, p, swapped), jnp.where(low, swapped, p)]

    store(qa_ref, project(0, A_Q_W, 0))
    kva = project(A_Q_W, 2 * A_KV_W, None)[0]
    store(ka_ref, per_kv_head(head_rms(kva[:, :A_KV_W], 1)[0]))
    store(va_ref, per_kv_head(kva[:, A_KV_W:]))

    rows = x_ref.shape[0]
    col0 = A_Q_W + 2 * A_KV_W
    for t, gain_row in enumerate((2, 3, None)):
        parts = project(col0 + t * B_W, B_W, gain_row)
        sec = jnp.concatenate(parts, axis=-1) if len(parts) > 1 else parts[0]
        for j in range(B_W // LANES):
            scr_ref[j] = sec[:, j * LANES:(j + 1) * LANES]
        n_lane_chunks = B_W // LANES
        prev_dil = 1
        for bi, dil in enumerate(B_DILS):
            out_ref = b_refs[t * len(B_DILS) + bi]
            if dil == 1:
                out_ref[...] = sec.astype(out_ref.dtype)
                continue
            assert dil == prev_dil * RESIDUE_STRIDE
            last = dil == B_DILS[-1]
            for res in range(dil):
                r_prev, r_sub = res % prev_dil, res // prev_dil
                for j in range(n_lane_chunks):
                    if prev_dil == 1:
                        v = scr_ref[j, pl.ds(r_sub, rows // dil, stride=RESIDUE_STRIDE), :]
                    else:
                        v = scr2_ref[r_prev * n_lane_chunks + j,
                                     pl.ds(r_sub, rows // dil, stride=RESIDUE_STRIDE), :]
                    out_ref[0, res, :, j * LANES:(j + 1) * LANES] = v.astype(out_ref.dtype)
                    if not last:
                        scr2_ref[res * n_lane_chunks + j] = v
            prev_dil = dil


def _in_proj(x2, g, w_in, gains, batch, seq):
    n = x2.shape[0]
    rows = IN_PROJ_ROWS
    steps = seq // rows
    a_widths = (A_Q_W, 2 * A_KV_W, 2 * A_KV_W)
    out_shape = [jax.ShapeDtypeStruct((n, w), jnp.bfloat16) for w in a_widths]
    out_specs = [pl.BlockSpec((rows, w), lambda i: (i, 0)) for w in a_widths]
    for _ in range(3):
        for dil in B_DILS:
            if dil == 1:
                out_shape.append(jax.ShapeDtypeStruct((n, B_W), jnp.bfloat16))
                out_specs.append(pl.BlockSpec((rows, B_W), lambda i: (i, 0)))
            else:
                out_shape.append(jax.ShapeDtypeStruct((batch, dil, seq // dil, B_W), jnp.bfloat16))
                out_specs.append(pl.BlockSpec((1, dil, rows // dil, B_W),
                                              lambda i: (i // steps, 0, i % steps, 0)))
    return pl.pallas_call(
        _in_proj_kernel,
        out_shape=out_shape,
        grid=(n // rows,),
        in_specs=[
            pl.BlockSpec((rows, D_MODEL), lambda i: (i, 0)),
            pl.BlockSpec((1, D_MODEL), lambda i: (0, 0)),
            pl.BlockSpec(memory_space=pl.ANY),
            pl.BlockSpec(gains.shape, lambda i: (0, 0)),
        ],
        out_specs=out_specs,
        scratch_shapes=[pltpu.VMEM((B_W // LANES, rows, LANES), jnp.float32),
                        pltpu.VMEM((RESIDUE_STRIDE * B_W // LANES, rows // RESIDUE_STRIDE, LANES),
                                   jnp.float32),
                        pltpu.VMEM(w_in.shape, jnp.bfloat16),
                        pltpu.VMEM((w_in.shape[0], w_in.shape[1] // 3), jnp.float32),
                        pltpu.SemaphoreType.DMA],
        compiler_params=pltpu.CompilerParams(
            dimension_semantics=("arbitrary",), vmem_limit_bytes=LARGE_VMEM_LIMIT),
        name="in_proj",
    )(x2, g, w_in, gains)


def _attn_kernel(*refs, n_chunks, kv_chunks, heads_per_stack, tk, half_w, seq_len, rows, has_sink,
                 want_lse):
    it = iter(refs)
    q_ref, k_ref, v_ref, bias_ref = next(it), next(it), next(it), next(it)
    sink_ref = next(it) if has_sink else None
    o_ref = next(it)
    lse_ref = next(it) if want_lse else None

    n_tiles = seq_len // Q_TILE
    tiles_per_step = rows // Q_TILE
    chunks_per_group = n_chunks // kv_chunks
    assert (2 * chunks_per_group) % heads_per_stack == 0
    step = pl.program_id(1)
    lane = lax.broadcasted_iota(jnp.int32, (Q_TILE, LANES), 1)
    low_half = lane < HEAD_DIM
    ones = jnp.ones((tk, LANES), jnp.bfloat16)

    for sq, t in [(a, b) for a in range(q_ref.shape[0]) for b in range(tiles_per_step)]:
        tile = step * tiles_per_step + t
        q0 = tile * Q_TILE
        kv_rows = k_ref.shape[1]
        kv_row0 = jnp.clip(step * rows - half_w, 0, seq_len - kv_rows)
        start = pl.multiple_of(jnp.clip(q0 - half_w, 0, seq_len - tk) - kv_row0, HEAD_DIM)
        variant = jnp.where(tile == 0, 0, jnp.where(tile == n_tiles - 1, 2, 1))
        r0 = t * Q_TILE
        lse_tile = jnp.zeros((Q_TILE, LANES), jnp.float32)
        head_o, head_lse = {}, {}
        for g in range(2 * n_chunks // heads_per_stack):
            heads = range(g * heads_per_stack, (g + 1) * heads_per_stack)
            kv = (heads[0] // 2) // chunks_per_group
            kc = k_ref[sq, pl.ds(start, tk), kv * LANES:(kv + 1) * LANES]
            vc = v_ref[sq, pl.ds(start, tk), kv * LANES:(kv + 1) * LANES]
            v_aug = jnp.concatenate([vc, ones], axis=1)
            q_parts = []
            for h in heads:
                c = h // 2
                q2 = q_ref[sq, r0:r0 + Q_TILE, c * LANES:(c + 1) * LANES]
                keep = low_half if h % 2 == 0 else ~low_half
                q_parts.append(jnp.where(keep, q2, jnp.zeros_like(q2)))
            qs = q_parts[0] if len(q_parts) == 1 else jnp.concatenate(q_parts, axis=0)
            s = lax.dot_general(qs, kc, (((1,), (1,)), ((), ())),
                                preferred_element_type=jnp.float32)
            s = s + bias_ref[variant, g]
            m = jnp.max(s, axis=-1, keepdims=True)
            if has_sink:
                m = jnp.maximum(m, sink_ref[g])
            p = jnp.exp2(s - m)
            ov = jnp.dot(p.astype(jnp.bfloat16), v_aug, preferred_element_type=jnp.float32)
            o, l = ov[:, :LANES], ov[:, LANES:]
            if has_sink:
                l = l + jnp.exp2(sink_ref[g] - m)
            o = o * (1.0 / l)
            if want_lse:
                lse = m + jnp.log(l) * LOG2E
            for idx, h in enumerate(heads):
                head_o[h] = o[idx * Q_TILE:(idx + 1) * Q_TILE]
                if want_lse:
                    head_lse[h] = lse[idx * Q_TILE:(idx + 1) * Q_TILE]
                if h % 2 == 1:
                    c = h // 2
                    o2 = jnp.where(low_half, head_o.pop(h - 1), head_o.pop(h))
                    o_ref[sq, r0:r0 + Q_TILE, c * LANES:(c + 1) * LANES] = o2.astype(o_ref.dtype)
                    if want_lse:
                        lse_tile = jnp.where(lane == h - 1, head_lse.pop(h - 1),
                                             jnp.where(lane == h, head_lse.pop(h), lse_tile))
        if want_lse:
            lse_ref[sq, r0:r0 + Q_TILE, :] = lse_tile


def _banded_attention(q, k, v, bias, *, half_w, sink=None, want_lse=False, name):
    n_seq, L, qw = q.shape
    kw = k.shape[2]
    tk = Q_TILE + 2 * half_w
    rows = min(ATTN_STEP_ROWS, L)
    seqs = ATTN_STEP_ROWS // rows
    kv_rows = min(L, rows + 2 * half_w)

    def kv_index(s, i):
        row0 = pl.multiple_of(jnp.clip(i * rows - half_w, 0, L - kv_rows), HEAD_DIM)
        return s * seqs, row0, 0

    args = [q, k, v, bias]
    in_specs = [
        pl.BlockSpec((seqs, rows, qw), lambda s, i: (s, i, 0)),
        pl.BlockSpec((pl.Element(seqs), pl.Element(kv_rows), pl.Element(kw)), kv_index),
        pl.BlockSpec((pl.Element(seqs), pl.Element(kv_rows), pl.Element(kw)), kv_index),
        pl.BlockSpec(bias.shape, lambda s, i: (0, 0, 0, 0)),
    ]
    if sink is not None:
        args.append(sink)
        in_specs.append(pl.BlockSpec(sink.shape, lambda s, i: (0, 0, 0)))
    out_shape = [jax.ShapeDtypeStruct((n_seq, L, qw), jnp.bfloat16)]
    out_specs = [pl.BlockSpec((seqs, rows, qw), lambda s, i: (s, i, 0))]
    if want_lse:
        out_shape.append(jax.ShapeDtypeStruct((n_seq, L, LANES), jnp.float32))
        out_specs.append(pl.BlockSpec((seqs, rows, LANES), lambda s, i: (s, i, 0)))

    kern = functools.partial(
        _attn_kernel, n_chunks=qw // LANES, kv_chunks=kw // LANES,
        heads_per_stack=bias.shape[2] // Q_TILE, tk=tk, half_w=half_w, seq_len=L,
        rows=rows, has_sink=sink is not None, want_lse=want_lse)
    return pl.pallas_call(
        kern,
        out_shape=out_shape,
        grid=(n_seq // seqs, L // rows),
        in_specs=in_specs,
        out_specs=out_specs,
        compiler_params=pltpu.CompilerParams(
            dimension_semantics=("arbitrary", "arbitrary"), vmem_limit_bytes=VMEM_LIMIT),
        name=name,
    )(*args)


def _out_proj_router_kernel(*refs):
    nb = len(B_DILS)
    oa_ref = refs[0]
    o_refs = refs[1:1 + nb]
    lse_refs = refs[1 + nb:1 + 2 * nb]
    (x_ref, wo_ref, g_ref, wr_ref, br_ref,
     x1_ref, hf_ref, topi_ref, gate_ref, rank_ref, cnt_ref,
     tri_ref, carry_ref, so_ref, sl_ref, wo_bf_ref) = refs[1 + 2 * nb:]
    i = pl.program_id(0)
    rows = x_ref.shape[0]

    @pl.when(i == 0)
    def _():
        a = lax.broadcasted_iota(jnp.int32, (rows, rows), 0)
        b = lax.broadcasted_iota(jnp.int32, (rows, rows), 1)
        tri_ref[...] = jnp.where(a <= b, 1.0, 0.0).astype(jnp.bfloat16)
        carry_ref[...] = jnp.zeros_like(carry_ref)
        wo_bf_ref[...] = wo_ref[...].astype(jnp.bfloat16)

    outs, lses = [], []
    for bi, dil in enumerate(B_DILS):
        if dil == 1:
            outs.append(o_refs[bi][...].astype(jnp.float32))
            lses.append(lse_refs[bi][...])
        else:
            for res in range(dil):
                for j in range(B_W // LANES):
                    so_ref[bi, j, pl.ds(res, rows // dil, stride=dil), :] = (
                        o_refs[bi][0, res, :, j * LANES:(j + 1) * LANES].astype(jnp.float32))
                sl_ref[bi, pl.ds(res, rows // dil, stride=dil), :] = lse_refs[bi][0, res]
            outs.append(jnp.concatenate([so_ref[bi, j] for j in range(B_W // LANES)], axis=-1))
            lses.append(sl_ref[bi])

    mx = functools.reduce(jnp.maximum, lses)
    es = [jnp.exp2(l - mx) for l in lses]
    inv = 1.0 / functools.reduce(lambda a, b: a + b, es)
    eh = lax.broadcasted_iota(jnp.int32, (LANES, B_W), 0)
    ej = lax.broadcasted_iota(jnp.int32, (LANES, B_W), 1) // HEAD_DIM
    expand = jnp.where(eh == ej, 1.0, 0.0).astype(jnp.bfloat16)
    ob = jnp.zeros((rows, B_W), jnp.float32)
    for e, o in zip(es, outs):
        w = e * inv
        wide = jnp.dot(w.astype(jnp.bfloat16), expand, preferred_element_type=jnp.float32)
        ob = ob + wide * o

    attn = jnp.concatenate([oa_ref[...], ob.astype(jnp.bfloat16)], axis=-1)
    x1 = x_ref[...] + jnp.dot(attn, wo_bf_ref[...], preferred_element_type=jnp.float32)
    x1_ref[...] = x1
    hf = x1 * lax.rsqrt(jnp.mean(x1 * x1, axis=-1, keepdims=True) + NORM_EPS) * g_ref[...]
    hf_hi = hf.astype(jnp.bfloat16)
    hf_ref[...] = _pack_halves(hf)
    hf_lo = (hf - hf_hi.astype(jnp.float32)).astype(jnp.bfloat16)

    nt = (((1,), (1,)), ((), ()))
    lg_hi = lax.dot_general(wr_ref[...], hf_hi, nt, preferred_element_type=jnp.float32)
    lg_lo = lax.dot_general(wr_ref[0:N_EXPERTS, :], hf_lo, nt, preferred_element_type=jnp.float32)
    logits = lg_hi[0:N_EXPERTS] + lg_hi[N_EXPERTS:] + lg_lo + br_ref[:, 0:1]


    eidx = lax.broadcasted_iota(jnp.int32, (N_EXPERTS, rows), 0)
    work = logits
    vals, sels = [], []
    for k in range(TOP_K):
        mk = jnp.max(work, axis=0, keepdims=True)
        ik = jnp.min(jnp.where(work == mk, eidx, N_EXPERTS), axis=0, keepdims=True)
        sel = eidx == ik
        work = jnp.where(sel, -jnp.inf, work)
        vals.append(mk)
        sels.append(sel)
        topi_ref[k:k + 1, :] = ik
    exps = [jnp.exp(vk - vals[0]) for vk in vals]
    denom = exps[0] + exps[1] + exps[2] + exps[3]
    ginv = 1.0 / denom
    for k in range(TOP_K):
        gate_ref[k:k + 1, :] = exps[k] * ginv

    onehot = jnp.zeros((N_EXPERTS, rows), jnp.float32)
    for sel in sels:
        onehot = onehot + jnp.where(sel, 1.0, 0.0)
    incl = jnp.dot(onehot.astype(jnp.bfloat16), tri_ref[...], preferred_element_type=jnp.float32)
    before = incl - onehot + carry_ref[:, 0:1]
    for k in range(TOP_K):
        rk = jnp.sum(jnp.where(sels[k], before, 0.0), axis=0, keepdims=True)
        rank_ref[k:k + 1, :] = rk.astype(jnp.int32)
    carry = carry_ref[...] + jnp.sum(onehot, axis=1, keepdims=True)
    carry_ref[...] = carry
    cnt_ref[...] = carry.astype(jnp.int32)


def _out_proj_router(oa, outs_b, lses_b, x2, wo_bf, g, wr, br, seq):
    n = x2.shape[0]
    rows = PROJ_ROWS
    steps = seq // rows
    row_spec = lambda w: pl.BlockSpec((rows, w), lambda i: (i, 0))
    full = lambda a: pl.BlockSpec(a.shape, lambda i: (0,) * a.ndim)
    col_spec = pl.BlockSpec((TOP_K, rows), lambda i: (0, i))

    def branch_spec(dil, w):
        if dil == 1:
            return row_spec(w)
        return pl.BlockSpec((1, dil, rows // dil, w), lambda i: (i // steps, 0, i % steps, 0))

    in_specs = ([row_spec(A_Q_W)]
                + [branch_spec(d, B_W) for d in B_DILS]
                + [branch_spec(d, LANES) for d in B_DILS]
                + [row_spec(D_MODEL), full(wo_bf), full(g), full(wr), full(br)])
    return pl.pallas_call(
        _out_proj_router_kernel,
        out_shape=[
            jax.ShapeDtypeStruct((n, D_MODEL), jnp.float32),
            jax.ShapeDtypeStruct((n, HALF_D), jnp.uint32),
            jax.ShapeDtypeStruct((TOP_K, n), jnp.int32),
            jax.ShapeDtypeStruct((TOP_K, n), jnp.float32),
            jax.ShapeDtypeStruct((TOP_K, n), jnp.int32),
            jax.ShapeDtypeStruct((N_EXPERTS, LANES), jnp.int32),
        ],
        grid=(n // rows,),
        in_specs=in_specs,
        out_specs=[row_spec(D_MODEL), row_spec(HALF_D), col_spec, col_spec, col_spec,
                   pl.BlockSpec((N_EXPERTS, LANES), lambda i: (0, 0))],
        scratch_shapes=[pltpu.VMEM((rows, rows), jnp.bfloat16),
                        pltpu.VMEM((N_EXPERTS, LANES), jnp.float32),
                        pltpu.VMEM((len(B_DILS), B_W // LANES, rows, LANES), jnp.float32),
                        pltpu.VMEM((len(B_DILS), rows, LANES), jnp.float32),
                        pltpu.VMEM(wo_bf.shape, jnp.bfloat16)],
        compiler_params=pltpu.CompilerParams(
            dimension_semantics=("arbitrary",), vmem_limit_bytes=LARGE_VMEM_LIMIT),
        name="out_proj_router",
    )(oa, *outs_b, *lses_b, x2, wo_bf, g, wr, br)


def _mxu_dot(a_bf, w_f32):
    return lax.dot_general(a_bf, w_f32, (((1,), (0,)), ((), ())), preferred_element_type=jnp.float32)


def _moe_kernel(blk_exp_ref, first_ref, slot_ref, next_exp_ref, n_used_ref,
                x_ref, w1_hbm, b1_ref, w2_hbm, b2_ref, y_ref, w1_buf, w2_buf, sem):
    step = pl.program_id(0)

    def weight_copies(expert, slot):
        return (pltpu.make_async_copy(w1_hbm.at[expert], w1_buf.at[slot], sem.at[slot, 0]),
                pltpu.make_async_copy(w2_hbm.at[expert], w2_buf.at[slot], sem.at[slot, 1]))

    @pl.when(step * MOE_STEP_BLOCKS < n_used_ref[0])
    def _():
        @pl.when(step == 0)
        def _():
            for cp in weight_copies(blk_exp_ref[0], slot_ref[0]):
                cp.start()

        for j in range(MOE_STEP_BLOCKS):
            i = step * MOE_STEP_BLOCKS + j

            @pl.when(first_ref[i] == 1)
            def _():
                slot = slot_ref[i]
                for cp in weight_copies(blk_exp_ref[i], slot):
                    cp.wait()

                @pl.when(next_exp_ref[i] >= 0)
                def _():
                    for cp in weight_copies(next_exp_ref[i], (slot + 1) % MOE_WEIGHT_BUFS):
                        cp.start()

        for j in range(MOE_STEP_BLOCKS):
            i = step * MOE_STEP_BLOCKS + j
            slot = slot_ref[i]
            expert = blk_exp_ref[i]
            rs = slice(j * MOE_ROWS, (j + 1) * MOE_ROWS)
            x = jnp.concatenate(_unpack_halves(x_ref[rs, :]), axis=-1).astype(jnp.bfloat16)
            acc = jnp.zeros((MOE_ROWS, D_MODEL), jnp.float32)
            for c in range(D_FF // FF_CHUNK):
                lo = c * FF_CHUNK
                glu = _mxu_dot(x, w1_buf[slot, :, lo:lo + FF_CHUNK]) + b1_ref[expert, :, lo:lo + FF_CHUNK]
                lin = (_mxu_dot(x, w1_buf[slot, :, D_FF + lo:D_FF + lo + FF_CHUNK])
                       + b1_ref[expert, :, D_FF + lo:D_FF + lo + FF_CHUNK])
                glu = jnp.minimum(glu, SWIGLU_LIMIT)
                lin = jnp.clip(lin, -SWIGLU_LIMIT, SWIGLU_LIMIT)
                act = glu * (1.0 / (1.0 + jnp.exp(-SWIGLU_ALPHA * glu))) * (lin + 1.0)
                acc = acc + _mxu_dot(act.astype(jnp.bfloat16), w2_buf[slot, lo:lo + FF_CHUNK, :])
            y_ref[rs, :] = _pack_halves(acc + b2_ref[expert])


def _moe_plan(pends, n_blk):
    g = MOE_ROWS
    blk_row0 = jnp.arange(n_blk, dtype=jnp.int32) * g
    blk_exp = jnp.minimum(jnp.sum(pends[None, :] <= blk_row0[:, None], axis=-1),
                          N_EXPERTS - 1).astype(jnp.int32)
    n_used = (pends[-1] // g).astype(jnp.int32)
    used = blk_row0 < pends[-1]
    prev_exp = jnp.concatenate([jnp.full((1,), -1, jnp.int32), blk_exp[:-1]])
    first = (used & (blk_exp != prev_exp)).astype(jnp.int32)
    slot = ((jnp.cumsum(first) - 1) % MOE_WEIGHT_BUFS).astype(jnp.int32)
    pstarts = jnp.concatenate([jnp.zeros((1,), pends.dtype), pends[:-1]])
    nonempty = pends > pstarts
    experts = jnp.arange(N_EXPERTS, dtype=jnp.int32)
    later = nonempty[None, :] & (experts[None, :] > experts[:, None])
    next_nonempty = jnp.min(jnp.where(later, experts[None, :], N_EXPERTS), axis=-1)
    next_nonempty = jnp.where(next_nonempty == N_EXPERTS, -1, next_nonempty).astype(jnp.int32)
    next_exp = jnp.sum(jnp.where(blk_exp[:, None] == experts[None, :], next_nonempty[None, :], 0),
                       axis=-1).astype(jnp.int32)
    return blk_exp, first, slot, next_exp, n_used.reshape(1)


def _moe_experts(plan, xb, w1, b1, w2, b2):
    n_rows = xb.shape[0]
    n_blk = n_rows // MOE_ROWS

    step_rows = MOE_STEP_BLOCKS * MOE_ROWS

    def blk(i, *p):
        return jnp.minimum(i, (p[-1][0] - 1) // MOE_STEP_BLOCKS)

    grid_spec = pltpu.PrefetchScalarGridSpec(
        num_scalar_prefetch=len(plan),
        grid=(n_blk // MOE_STEP_BLOCKS,),
        in_specs=[
            pl.BlockSpec((step_rows, HALF_D), lambda i, *p: (blk(i, *p), 0)),
            pl.BlockSpec(memory_space=pl.ANY),
            pl.BlockSpec(b1.shape, lambda i, *p: (0, 0, 0)),
            pl.BlockSpec(memory_space=pl.ANY),
            pl.BlockSpec(b2.shape, lambda i, *p: (0, 0, 0)),
        ],
        out_specs=pl.BlockSpec((step_rows, HALF_D), lambda i, *p: (blk(i, *p), 0)),
        scratch_shapes=[pltpu.VMEM((MOE_WEIGHT_BUFS, D_MODEL, 2 * D_FF), jnp.float32),
                        pltpu.VMEM((MOE_WEIGHT_BUFS, D_FF, D_MODEL), jnp.float32),
                        pltpu.SemaphoreType.DMA((MOE_WEIGHT_BUFS, 2))],
    )
    return pl.pallas_call(
        _moe_kernel,
        out_shape=jax.ShapeDtypeStruct((n_rows, HALF_D), jnp.uint32),
        grid_spec=grid_spec,
        compiler_params=pltpu.CompilerParams(
            dimension_semantics=("arbitrary",), vmem_limit_bytes=LARGE_VMEM_LIMIT),
        name="moe_experts",
    )(*plan, xb, w1, b1, w2, b2)


def _sc_worker_id():
    return lax.axis_index("s") * SC_CORES + lax.axis_index("c")


def _sc_dispatch(hf, dest3, n_rows):
    n = hf.shape[0]
    chunks_per_worker = n // SC_CHUNK // SC_WORKERS
    mesh = plsc.VectorSubcoreMesh(core_axis_name="c", subcore_axis_name="s")

    @functools.partial(
        pl.kernel, mesh=mesh,
        out_type=jax.ShapeDtypeStruct((n_rows, HALF_D), hf.dtype),
        scratch_types=[pltpu.VMEM((chunks_per_worker, TOP_K, SC_CHUNK), jnp.int32),
                       pltpu.VMEM((2, SC_CHUNK, HALF_D), hf.dtype),
                       pltpu.SemaphoreType.DMA((2,)),
                       pltpu.SemaphoreType.DMA((2,))],
        name="sc_dispatch")
    def run(hf_hbm, dest_hbm, xb_hbm, idx_v, rows_v, load_sem, scatter_sem):
        first = _sc_worker_id() * chunks_per_worker
        pltpu.sync_copy(dest_hbm.at[pl.ds(first, chunks_per_worker)], idx_v)

        def load(j):
            slot = j % 2
            return pltpu.async_copy(hf_hbm.at[pl.ds((first + j) * SC_CHUNK, SC_CHUNK)],
                                    rows_v.at[slot], load_sem.at[slot])

        loads = {0: load(0)}
        scatters = {}
        for j in range(chunks_per_worker):
            slot = j % 2
            loads.pop(j).wait()
            scatters[j] = [pltpu.async_copy(rows_v.at[slot], xb_hbm.at[idx_v.at[j, k]],
                                            scatter_sem.at[slot]) for k in range(TOP_K)]
            if j >= 1:
                for cp in scatters.pop(j - 1):
                    cp.wait()
            if j + 1 < chunks_per_worker:
                loads[j + 1] = load(j + 1)
        for cp in scatters.pop(chunks_per_worker - 1):
            cp.wait()

    return run(hf, dest3)


def _sc_collect_sum(y, dest3, gates16, first_chunk, n_chunks):
    C = SC_COLLECT_CHUNK
    n = n_chunks * C
    chunks_per_worker = n_chunks // SC_WORKERS
    lanes = 16
    mesh = plsc.VectorSubcoreMesh(core_axis_name="c", subcore_axis_name="s")

    @functools.partial(
        pl.kernel, mesh=mesh,
        out_type=jax.ShapeDtypeStruct((n, HALF_D), y.dtype),
        scratch_types=[pltpu.VMEM((chunks_per_worker, TOP_K, C), jnp.int32),
                       pltpu.VMEM((chunks_per_worker, TOP_K, C), jnp.uint32),
                       pltpu.VMEM((2, TOP_K, C, HALF_D), y.dtype),
                       pltpu.VMEM((2, C, HALF_D), y.dtype),
                       pltpu.SemaphoreType.DMA((2, TOP_K)),
                       pltpu.SemaphoreType.DMA((2,))],
        compiler_params=pltpu.CompilerParams(needs_layout_passes=False),
        name="sc_collect_sum")
    def run(y_hbm, dest_hbm, gate_hbm, out_hbm, idx_v, gate_v, rows_v, sum_v, gather_sem, write_sem):
        local0 = _sc_worker_id() * chunks_per_worker

        pltpu.sync_copy(dest_hbm.at[pl.ds(first_chunk + local0, chunks_per_worker)], idx_v)
        pltpu.sync_copy(gate_hbm.at[pl.ds(first_chunk + local0, chunks_per_worker)], gate_v)

        def gather_copies(i, slot):
            return [pltpu.make_async_copy(y_hbm.at[idx_v.at[i, k]], rows_v.at[slot, k],
                                          gather_sem.at[slot, k]) for k in range(TOP_K)]

        def write_copy(i, slot):
            return pltpu.make_async_copy(sum_v.at[slot], out_hbm.at[pl.ds((local0 + i) * C, C)],
                                         write_sem.at[slot])

        def fetch(i, slot):
            for cp in gather_copies(i, slot):
                cp.start()

        def reduce_chunk(i, slot):
            for cp in gather_copies(i, slot):
                cp.wait()

            @pl.when(i >= 2)
            def _():
                write_copy(i - 2, slot).wait()

            @plsc.parallel_loop(0, C)
            def _(j):
                lane_j = jnp.full((lanes,), j, jnp.int32)
                g = [plsc.bitcast(gate_v[i, k, :].at[lane_j].get(mode="promise_in_bounds"),
                                  jnp.bfloat16) for k in range(TOP_K)]
                for c in range(HALF_D // lanes):
                    sl = pl.ds(c * lanes, lanes)
                    acc = None
                    for k in range(TOP_K):
                        term = plsc.bitcast(rows_v[slot, k, j, sl], jnp.bfloat16) * g[k]
                        acc = term if acc is None else acc + term
                    sum_v[slot, j, sl] = plsc.bitcast(acc, y.dtype)

            write_copy(i, slot).start()

        fetch(0, 0)

        @pl.loop(0, chunks_per_worker, step=2)
        def _(i):
            fetch(i + 1, 1)
            reduce_chunk(i, 0)

            @pl.when(i + 2 < chunks_per_worker)
            def _():
                fetch(i + 2, 0)

            reduce_chunk(i + 1, 1)

        write_copy(chunks_per_worker - 2, 0).wait()
        write_copy(chunks_per_worker - 1, 1).wait()

    return run(y, dest3, gates16)


def _combine_sum_kernel(x1_ref, ys_ref, o_ref):
    lo, hi = _unpack_halves(ys_ref[...])
    o_ref[:, :HALF_D] = x1_ref[:, :HALF_D] + lo
    o_ref[:, HALF_D:] = x1_ref[:, HALF_D:] + hi


def _combine_sum(acc, ysum, split):
    n = acc.shape[0]
    rows = PROJ_ROWS
    steps = ysum.shape[0] // rows
    first = split * steps
    return pl.pallas_call(
        _combine_sum_kernel,
        out_shape=jax.ShapeDtypeStruct((n, D_MODEL), jnp.float32),
        grid=(steps,),
        in_specs=[pl.BlockSpec((rows, D_MODEL), lambda i: (first + i, 0)),
                  pl.BlockSpec((rows, HALF_D), lambda i: (i, 0))],
        out_specs=pl.BlockSpec((rows, D_MODEL), lambda i: (first + i, 0)),
        input_output_aliases={0: 0},
        compiler_params=pltpu.CompilerParams(
            dimension_semantics=("arbitrary",), vmem_limit_bytes=VMEM_LIMIT),
        name="moe_combine",
    )(acc, ysum)


def _layer(x2, batch, seq, attn_norm_g, w_in, a_q_g, a_k_g, a_sinks, b_q_g, b_k_g, w_out,
           ffn_norm_g, w_router, b_router, w1, b1, w2, b2):
    n = x2.shape[0]
    slopes = _alibi_slopes()
    q_scale = HEAD_DIM ** -0.5 * LOG2E
    reps = MXU_DIM // HEAD_DIM
    gains = jnp.stack([jnp.tile(a_q_g, reps) * q_scale, jnp.tile(a_k_g, reps),
                       jnp.tile(b_q_g, reps) * q_scale, jnp.tile(b_k_g, reps)]).astype(jnp.float32)

    proj = _in_proj(x2, attn_norm_g.reshape(1, -1), w_in, gains, batch, seq)
    qa, ka, va = proj[:3]
    nb = len(B_DILS)
    qbs, kbs, vbs = proj[3:3 + nb], proj[3 + nb:3 + 2 * nb], proj[3 + 2 * nb:]

    bias_a = _bias_tables(slopes[:A_Q_HEADS], A_STACK_HEADS, A_HALF_WINDOW, 1, Q_TILE + 2 * A_HALF_WINDOW)
    sink_col = jnp.repeat(a_sinks.astype(jnp.float32) * LOG2E, Q_TILE).reshape(
        A_Q_HEADS // A_STACK_HEADS, A_STACK_HEADS * Q_TILE, 1)
    as_seqs = lambda a: a.reshape(batch, seq, a.shape[-1])
    out_a = _banded_attention(as_seqs(qa), as_seqs(ka), as_seqs(va), bias_a, half_w=A_HALF_WINDOW,
                              sink=sink_col, name="attn_a")[0].reshape(n, A_Q_W)

    outs_b, lses_b = [], []
    for bi, (window, dil) in enumerate(B_BRANCHES):
        half_w = window // (2 * dil)
        bias_b = _bias_tables(slopes[A_Q_HEADS:], B_STACK_HEADS, half_w, dil, Q_TILE + 2 * half_w)
        L = seq // dil
        to_seqs = lambda a: a.reshape(batch * dil, L, a.shape[-1])
        o, lse = _banded_attention(to_seqs(qbs[bi]), to_seqs(kbs[bi]), to_seqs(vbs[bi]), bias_b,
                                   half_w=half_w, want_lse=True, name=f"attn_b_d{dil}")
        if dil == 1:
            outs_b.append(o.reshape(n, B_W))
            lses_b.append(lse.reshape(n, LANES))
        else:
            outs_b.append(o.reshape(batch, dil, L, B_W))
            lses_b.append(lse.reshape(batch, dil, L, LANES))

    wr_t = w_router.T.astype(jnp.float32)
    wr_hi = wr_t.astype(jnp.bfloat16)
    wr_lo = (wr_t - wr_hi.astype(jnp.float32)).astype(jnp.bfloat16)
    wr = jnp.concatenate([wr_hi, wr_lo], axis=0)
    br = jnp.broadcast_to(b_router.astype(jnp.float32)[:, None], (N_EXPERTS, LANES))
    x1, hf, topi, gates, ranks, counts = _out_proj_router(
        out_a, outs_b, lses_b, x2, w_out, ffn_norm_g.reshape(1, -1), wr, br, seq)

    g = MOE_ROWS
    nk = n * TOP_K
    step_rows = MOE_STEP_BLOCKS * g
    n_rows = -(-(nk + N_EXPERTS * g) // step_rows) * step_rows
    cnt = counts[:, 0]
    pcnt = (cnt + g - 1) // g * g
    pends = jnp.cumsum(pcnt)
    pstarts = pends - pcnt
    experts = jnp.arange(N_EXPERTS, dtype=jnp.int32)
    start_of = jnp.sum(jnp.where(topi[:, :, None] == experts, pstarts, 0), axis=-1)
    dest = (start_of + ranks).astype(jnp.int32)
    plan = _moe_plan(pends, n_rows // g)
    dest3 = dest.reshape(TOP_K, n // SC_CHUNK, SC_CHUNK).transpose(1, 0, 2)

    xb = _sc_dispatch(hf, dest3, n_rows)
    y = _moe_experts(plan, xb, w1, b1[:, None, :], w2, b2[:, None, :])
    smallest_range = max(PROJ_ROWS, SC_COLLECT_CHUNK * SC_WORKERS)
    n_splits = max(1, min(COMBINE_SPLITS, n // smallest_range))
    per_split = n // n_splits
    chunked = lambda a: a.reshape(TOP_K, n // SC_COLLECT_CHUNK, SC_COLLECT_CHUNK).transpose(1, 0, 2)
    dest_c = chunked(dest)
    gate_bits = lax.bitcast_convert_type(gates.astype(jnp.bfloat16), jnp.uint16).astype(jnp.uint32)
    gate_words = gate_bits | (gate_bits << 16)
    gates_c = chunked(gate_words)
    chunks_per_split = per_split // SC_COLLECT_CHUNK
    out = x1
    for s in range(n_splits):
        ysum = _sc_collect_sum(y, dest_c, gates_c, s * chunks_per_split, chunks_per_split)
        out = _combine_sum(out, ysum, s)
    return out


def kernel(x, attn_norm_g, w_in, a_q_norm_g, a_k_norm_g, a_sinks, b_q_norm_g, b_k_norm_g, w_out,
           ffn_norm_g, w_router, b_router, w1, b1, w2, b2):
    batch, seq, d = x.shape
    x2 = x.reshape(batch * seq, d)
    for i in range(attn_norm_g.shape[0]):
        x2 = _layer(x2, batch, seq, attn_norm_g[i], w_in[i], a_q_norm_g[i], a_k_norm_g[i],
                    a_sinks[i], b_q_norm_g[i], b_k_norm_g[i], w_out[i], ffn_norm_g[i],
                    w_router[i], b_router[i], w1[i], b1[i], w2[i], b2[i])
    return x2.reshape(batch, seq, d)
```

```python
import functools

import jax
import jax.numpy as jnp
import numpy as np
from jax import lax
from jax.experimental import pallas as pl
from jax.experimental.pallas import tpu as pltpu
from jax.experimental.pallas import tpu_sc as plsc

D_MODEL = 1024
HALF_D = D_MODEL // 2
HEAD_DIM = 64
LANES = 128
MXU_DIM = 256
A_Q_HEADS = 8
A_KV_HEADS = 2
B_HEADS = 8
A_HALF_WINDOW = 128
B_BRANCHES = ((128, 1), (512, 4), (2048, 16))
B_DILS = tuple(d for _, d in B_BRANCHES)
RESIDUE_STRIDE = 4
N_ALIBI_HEADS = 16
A_Q_W = A_Q_HEADS * HEAD_DIM
A_KV_W = A_KV_HEADS * HEAD_DIM
B_W = B_HEADS * HEAD_DIM
N_EXPERTS = 32
TOP_K = 4
D_FF = 1024
SWIGLU_ALPHA = 1.702
SWIGLU_LIMIT = 7.0
NORM_EPS = 1e-5
MASK_VALUE = -1e30
LOG2E = 1.4426950408889634

Q_TILE = 128
ATTN_STEP_ROWS = 2048
A_STACK_HEADS = 4
B_STACK_HEADS = 2
PROJ_ROWS = 1024
IN_PROJ_ROWS = 1024
MOE_ROWS = 512
FF_CHUNK = 512
MOE_STEP_BLOCKS = 2
MOE_WEIGHT_BUFS = 3
VMEM_LIMIT = 48 * 1024 * 1024
LARGE_VMEM_LIMIT = 58 * 1024 * 1024
SC_CORES = 2
SC_SUBCORES = 16
SC_WORKERS = SC_CORES * SC_SUBCORES
SC_CHUNK = 64
SC_COLLECT_CHUNK = 16
COMBINE_SPLITS = 2


def _pack_halves(v):
    lo = v[:, :HALF_D].astype(jnp.bfloat16).astype(jnp.float32)
    hi = v[:, HALF_D:].astype(jnp.bfloat16).astype(jnp.float32)
    return (pltpu.bitcast(lo, jnp.uint32) >> 16) | pltpu.bitcast(hi, jnp.uint32)


def _unpack_halves(w):
    lo = pltpu.bitcast(w << 16, jnp.float32)
    hi = pltpu.bitcast(w & jnp.uint32(0xFFFF0000), jnp.float32)
    return lo, hi


def _alibi_slopes():
    return np.exp2(-8.0 * np.arange(1, N_ALIBI_HEADS + 1, dtype=np.float32) / N_ALIBI_HEADS).astype(np.float32)


def _bias_tables(head_slopes, heads_per_group, half_w, dist_scale, tk):
    i = np.arange(Q_TILE)[:, None]
    j = np.arange(tk)[None, :]
    tabs = []
    for shift in (0, half_w, tk - Q_TILE):
        dist = np.abs(j - shift - i)
        valid = dist <= half_w
        per_head = []
        for sl in head_slopes:
            b = (-np.float64(sl) * LOG2E * (dist * dist_scale)).astype(np.float32)
            per_head.append(np.where(valid, b, np.float32(MASK_VALUE)).astype(np.float32))
        t = np.stack(per_head).reshape(-1, heads_per_group * Q_TILE, tk)
        tabs.append(t)
    return jnp.asarray(np.stack(tabs))


def _in_proj_kernel(x_ref, g_ref, w_hbm, gains_ref, qa_ref, ka_ref, va_ref, *rest):
    b_refs, (scr_ref, scr2_ref, w_ref, stage_ref, sem) = rest[:-5], rest[-5:]

    @pl.when(pl.program_id(0) == 0)
    def _():
        width = stage_ref.shape[1]
        for c0 in range(0, w_ref.shape[1], width):
            cp = pltpu.make_async_copy(w_hbm.at[:, c0:c0 + width], stage_ref, sem)
            cp.start()
            cp.wait()
            w_ref[:, c0:c0 + width] = stage_ref[...].astype(jnp.bfloat16)

    x = x_ref[...]
    xn = x * lax.rsqrt(jnp.mean(x * x, axis=-1, keepdims=True) + NORM_EPS) * g_ref[...]
    xn = xn.astype(jnp.bfloat16)
    r = lax.broadcasted_iota(jnp.int32, (MXU_DIM, MXU_DIM), 0) // HEAD_DIM
    c = lax.broadcasted_iota(jnp.int32, (MXU_DIM, MXU_DIM), 1) // HEAD_DIM
    blockdiag = jnp.where(r == c, 1.0, 0.0).astype(jnp.bfloat16)

    def head_rms(sec, gain_row):
        width = sec.shape[1]
        parts = []
        step = min(width, MXU_DIM)
        for j in range(width // step):
            p = sec[:, j * step:(j + 1) * step]
            ss = jnp.dot((p * p).astype(jnp.bfloat16), blockdiag[:step, :step],
                         preferred_element_type=jnp.float32)
            parts.append(p * lax.rsqrt(ss * (1.0 / HEAD_DIM) + NORM_EPS)
                         * gains_ref[gain_row:gain_row + 1, :step])
        return parts

    def project(col0, width, gain_row):
        sec = jnp.dot(xn, w_ref[:, col0:col0 + width], preferred_element_type=jnp.float32)
        return [sec] if gain_row is None else head_rms(sec, gain_row)

    def store(out_ref, parts):
        w = parts[0].shape[1]
        for j, p in enumerate(parts):
            out_ref[:, j * w:(j + 1) * w] = p.astype(out_ref.dtype)

    def per_kv_head(p):
        lane = lax.broadcasted_iota(jnp.int32, p.shape, 1)
        swapped = pltpu.roll(p, HEAD_DIM, axis=1)
        low = lane < HEAD_DIM
        return [jnp.where(low, p, swapped), jnp.where(low, swapped, p)]

    store(qa_ref, project(0, A_Q_W, 0))
    kva = project(A_Q_W, 2 * A_KV_W, None)[0]
    store(ka_ref, per_kv_head(head_rms(kva[:, :A_KV_W], 1)[0]))
    store(va_ref, per_kv_head(kva[:, A_KV_W:]))

    rows = x_ref.shape[0]
    col0 = A_Q_W + 2 * A_KV_W
    for t, gain_row in enumerate((2, 3, None)):
        parts = project(col0 + t * B_W, B_W, gain_row)
        sec = jnp.concatenate(parts, axis=-1) if len(parts) > 1 else parts[0]
        for j in range(B_W // LANES):
            scr_ref[j] = sec[:, j * LANES:(j + 1) * LANES]
        n_lane_chunks = B_W // LANES
        prev_dil = 1
        for bi, dil in enumerate(B_DILS):
            out_ref = b_refs[t * len(B_DILS) + bi]
            if dil == 1:
                out_ref[...] = sec.astype(out_ref.dtype)
                continue
            assert dil == prev_dil * RESIDUE_STRIDE
            last = dil == B_DILS[-1]
            for res in range(dil):
                r_prev, r_sub = res % prev_dil, res // prev_dil
                for j in range(n_lane_chunks):
                    if prev_dil == 1:
                        v = scr_ref[j, pl.ds(r_sub, rows // dil, stride=RESIDUE_STRIDE), :]
                    else:
                        v = scr2_ref[r_prev * n_lane_chunks + j,
                                     pl.ds(r_sub, rows // dil, stride=RESIDUE_STRIDE), :]
                    out_ref[0, res, :, j * LANES:(j + 1) * LANES] = v.astype(out_ref.dtype)
                    if not last:
                        scr2_ref[res * n_lane_chunks + j] = v
            prev_dil = dil


def _in_proj(x2, g, w_in, gains, batch, seq):
    n = x2.shape[0]
    rows = IN_PROJ_ROWS
    steps = seq // rows
    a_widths = (A_Q_W, 2 * A_KV_W, 2 * A_KV_W)
    out_shape = [jax.ShapeDtypeStruct((n, w), jnp.bfloat16) for w in a_widths]
    out_specs = [pl.BlockSpec((rows, w), lambda i: (i, 0)) for w in a_widths]
    for _ in range(3):
        for dil in B_DILS:
            if dil == 1:
                out_shape.append(jax.ShapeDtypeStruct((n, B_W), jnp.bfloat16))
                out_specs.append(pl.BlockSpec((rows, B_W), lambda i: (i, 0)))
            else:
                out_shape.append(jax.ShapeDtypeStruct((batch, dil, seq // dil, B_W), jnp.bfloat16))
                out_specs.append(pl.BlockSpec((1, dil, rows // dil, B_W),
                                              lambda i: (i // steps, 0, i % steps, 0)))
    return pl.pallas_call(
        _in_proj_kernel,
        out_shape=out_shape,
        grid=(n // rows,),
        in_specs=[
            pl.BlockSpec((rows, D_MODEL), lambda i: (i, 0)),
            pl.BlockSpec((1, D_MODEL), lambda i: (0, 0)),
            pl.BlockSpec(memory_space=pl.ANY),
            pl.BlockSpec(gains.shape, lambda i: (0, 0)),
        ],
        out_specs=out_specs,
        scratch_shapes=[pltpu.VMEM((B_W // LANES, rows, LANES), jnp.float32),
                        pltpu.VMEM((RESIDUE_STRIDE * B_W // LANES, rows // RESIDUE_STRIDE, LANES),
                                   jnp.float32),
                        pltpu.VMEM(w_in.shape, jnp.bfloat16),
                        pltpu.VMEM((w_in.shape[0], w_in.shape[1] // 3), jnp.float32),
                        pltpu.SemaphoreType.DMA],
        compiler_params=pltpu.CompilerParams(
            dimension_semantics=("arbitrary",), vmem_limit_bytes=LARGE_VMEM_LIMIT),
        name="in_proj",
    )(x2, g, w_in, gains)


def _attn_kernel(*refs, n_chunks, kv_chunks, heads_per_stack, tk, half_w, seq_len, rows, has_sink,
                 want_lse):
    it = iter(refs)
    q_ref, k_ref, v_ref, bias_ref = next(it), next(it), next(it), next(it)
    sink_ref = next(it) if has_sink else None
    o_ref = next(it)
    lse_ref = next(it) if want_lse else None

    n_tiles = seq_len // Q_TILE
    tiles_per_step = rows // Q_TILE
    chunks_per_group = n_chunks // kv_chunks
    assert (2 * chunks_per_group) % heads_per_stack == 0
    step = pl.program_id(1)
    lane = lax.broadcasted_iota(jnp.int32, (Q_TILE, LANES), 1)
    low_half = lane < HEAD_DIM
    ones = jnp.ones((tk, LANES), jnp.bfloat16)

    for sq, t in [(a, b) for a in range(q_ref.shape[0]) for b in range(tiles_per_step)]:
        tile = step * tiles_per_step + t
        q0 = tile * Q_TILE
        kv_rows = k_ref.shape[1]
        kv_row0 = jnp.clip(step * rows - half_w, 0, seq_len - kv_rows)
        start = pl.multiple_of(jnp.clip(q0 - half_w, 0, seq_len - tk) - kv_row0, HEAD_DIM)
        variant = jnp.where(tile == 0, 0, jnp.where(tile == n_tiles - 1, 2, 1))
        r0 = t * Q_TILE
        lse_tile = jnp.zeros((Q_TILE, LANES), jnp.float32)
        head_o, head_lse = {}, {}
        for g in range(2 * n_chunks // heads_per_stack):
            heads = range(g * heads_per_stack, (g + 1) * heads_per_stack)
            kv = (heads[0] // 2) // chunks_per_group
            kc = k_ref[sq, pl.ds(start, tk), kv * LANES:(kv + 1) * LANES]
            vc = v_ref[sq, pl.ds(start, tk), kv * LANES:(kv + 1) * LANES]
            v_aug = jnp.concatenate([vc, ones], axis=1)
            q_parts = []
            for h in heads:
                c = h // 2
                q2 = q_ref[sq, r0:r0 + Q_TILE, c * LANES:(c + 1) * LANES]
                keep = low_half if h % 2 == 0 else ~low_half
                q_parts.append(jnp.where(keep, q2, jnp.zeros_like(q2)))
            qs = q_parts[0] if len(q_parts) == 1 else jnp.concatenate(q_parts, axis=0)
            s = lax.dot_general(qs, kc, (((1,), (1,)), ((), ())),
                                preferred_element_type=jnp.float32)
            s = s + bias_ref[variant, g]
            m = jnp.max(s, axis=-1, keepdims=True)
            if has_sink:
                m = jnp.maximum(m, sink_ref[g])
            p = jnp.exp2(s - m)
            ov = jnp.dot(p.astype(jnp.bfloat16), v_aug, preferred_element_type=jnp.float32)
            o, l = ov[:, :LANES], ov[:, LANES:]
            if has_sink:
                l = l + jnp.exp2(sink_ref[g] - m)
            o = o * (1.0 / l)
            if want_lse:
                lse = m + jnp.log(l) * LOG2E
            for idx, h in enumerate(heads):
                head_o[h] = o[idx * Q_TILE:(idx + 1) * Q_TILE]
                if want_lse:
                    head_lse[h] = lse[idx * Q_TILE:(idx + 1) * Q_TILE]
                if h % 2 == 1:
                    c = h // 2
                    o2 = jnp.where(low_half, head_o.pop(h - 1), head_o.pop(h))
                    o_ref[sq, r0:r0 + Q_TILE, c * LANES:(c + 1) * LANES] = o2.astype(o_ref.dtype)
                    if want_lse:
                        lse_tile = jnp.where(lane == h - 1, head_lse.pop(h - 1),
                                             jnp.where(lane == h, head_lse.pop(h), lse_tile))
        if want_lse:
            lse_ref[sq, r0:r0 + Q_TILE, :] = lse_tile


def _banded_attention(q, k, v, bias, *, half_w, sink=None, want_lse=False, name):
    n_seq, L, qw = q.shape
    kw = k.shape[2]
    tk = Q_TILE + 2 * half_w
    rows = min(ATTN_STEP_ROWS, L)
    seqs = ATTN_STEP_ROWS // rows
    kv_rows = min(L, rows + 2 * half_w)

    def kv_index(s, i):
        row0 = pl.multiple_of(jnp.clip(i * rows - half_w, 0, L - kv_rows), HEAD_DIM)
        return s * seqs, row0, 0

    args = [q, k, v, bias]
    in_specs = [
        pl.BlockSpec((seqs, rows, qw), lambda s, i: (s, i, 0)),
        pl.BlockSpec((pl.Element(seqs), pl.Element(kv_rows), pl.Element(kw)), kv_index),
        pl.BlockSpec((pl.Element(seqs), pl.Element(kv_rows), pl.Element(kw)), kv_index),
        pl.BlockSpec(bias.shape, lambda s, i: (0, 0, 0, 0)),
    ]
    if sink is not None:
        args.append(sink)
        in_specs.append(pl.BlockSpec(sink.shape, lambda s, i: (0, 0, 0)))
    out_shape = [jax.ShapeDtypeStruct((n_seq, L, qw), jnp.bfloat16)]
    out_specs = [pl.BlockSpec((seqs, rows, qw), lambda s, i: (s, i, 0))]
    if want_lse:
        out_shape.append(jax.ShapeDtypeStruct((n_seq, L, LANES), jnp.float32))
        out_specs.append(pl.BlockSpec((seqs, rows, LANES), lambda s, i: (s, i, 0)))

    kern = functools.partial(
        _attn_kernel, n_chunks=qw // LANES, kv_chunks=kw // LANES,
        heads_per_stack=bias.shape[2] // Q_TILE, tk=tk, half_w=half_w, seq_len=L,
        rows=rows, has_sink=sink is not None, want_lse=want_lse)
    return pl.pallas_call(
        kern,
        out_shape=out_shape,
        grid=(n_seq // seqs, L // rows),
        in_specs=in_specs,
        out_specs=out_specs,
        compiler_params=pltpu.CompilerParams(
            dimension_semantics=("arbitrary", "arbitrary"), vmem_limit_bytes=VMEM_LIMIT),
        name=name,
    )(*args)


def _out_proj_router_kernel(*refs):
    nb = len(B_DILS)
    oa_ref = refs[0]
    o_refs = refs[1:1 + nb]
    lse_refs = refs[1 + nb:1 + 2 * nb]
    (x_ref, wo_ref, g_ref, wr_ref, br_ref,
     x1_ref, hf_ref, topi_ref, gate_ref, rank_ref, cnt_ref,
     tri_ref, carry_ref, so_ref, sl_ref, wo_bf_ref) = refs[1 + 2 * nb:]
    i = pl.program_id(0)
    rows = x_ref.shape[0]

    @pl.when(i == 0)
    def _():
        a = lax.broadcasted_iota(jnp.int32, (rows, rows), 0)
        b = lax.broadcasted_iota(jnp.int32, (rows, rows), 1)
        tri_ref[...] = jnp.where(a <= b, 1.0, 0.0).astype(jnp.bfloat16)
        carry_ref[...] = jnp.zeros_like(carry_ref)
        wo_bf_ref[...] = wo_ref[...].astype(jnp.bfloat16)

    outs, lses = [], []
    for bi, dil in enumerate(B_DILS):
        if dil == 1:
            outs.append(o_refs[bi][...].astype(jnp.float32))
            lses.append(lse_refs[bi][...])
        else:
            for res in range(dil):
                for j in range(B_W // LANES):
                    so_ref[bi, j, pl.ds(res, rows // dil, stride=dil), :] = (
                        o_refs[bi][0, res, :, j * LANES:(j + 1) * LANES].astype(jnp.float32))
                sl_ref[bi, pl.ds(res, rows // dil, stride=dil), :] = lse_refs[bi][0, res]
            outs.append(jnp.concatenate([so_ref[bi, j] for j in range(B_W // LANES)], axis=-1))
            lses.append(sl_ref[bi])

    mx = functools.reduce(jnp.maximum, lses)
    es = [jnp.exp2(l - mx) for l in lses]
    inv = 1.0 / functools.reduce(lambda a, b: a + b, es)
    eh = lax.broadcasted_iota(jnp.int32, (LANES, B_W), 0)
    ej = lax.broadcasted_iota(jnp.int32, (LANES, B_W), 1) // HEAD_DIM
    expand = jnp.where(eh == ej, 1.0, 0.0).astype(jnp.bfloat16)
    ob = jnp.zeros((rows, B_W), jnp.float32)
    for e, o in zip(es, outs):
        w = e * inv
        wide = jnp.dot(w.astype(jnp.bfloat16), expand, preferred_element_type=jnp.float32)
        ob = ob + wide * o

    attn = jnp.concatenate([oa_ref[...], ob.astype(jnp.bfloat16)], axis=-1)
    x1 = x_ref[...] + jnp.dot(attn, wo_bf_ref[...], preferred_element_type=jnp.float32)
    x1_ref[...] = x1
    hf = x1 * lax.rsqrt(jnp.mean(x1 * x1, axis=-1, keepdims=True) + NORM_EPS) * g_ref[...]
    hf_hi = hf.astype(jnp.bfloat16)
    hf_ref[...] = _pack_halves(hf)
    hf_lo = (hf - hf_hi.astype(jnp.float32)).astype(jnp.bfloat16)

    nt = (((1,), (1,)), ((), ()))
    lg_hi = lax.dot_general(wr_ref[...], hf_hi, nt, preferred_element_type=jnp.float32)
    lg_lo = lax.dot_general(wr_ref[0:N_EXPERTS, :], hf_lo, nt, preferred_element_type=jnp.float32)
    logits = lg_hi[0:N_EXPERTS] + lg_hi[N_EXPERTS:] + lg_lo + br_ref[:, 0:1]


    eidx = lax.broadcasted_iota(jnp.int32, (N_EXPERTS, rows), 0)
    work = logits
    vals, sels = [], []
    for k in range(TOP_K):
        mk = jnp.max(work, axis=0, keepdims=True)
        ik = jnp.min(jnp.where(work == mk, eidx, N_EXPERTS), axis=0, keepdims=True)
        sel = eidx == ik
        work = jnp.where(sel, -jnp.inf, work)
        vals.append(mk)
        sels.append(sel)
        topi_ref[k:k + 1, :] = ik
    exps = [jnp.exp(vk - vals[0]) for vk in vals]
    denom = exps[0] + exps[1] + exps[2] + exps[3]
    ginv = 1.0 / denom
    for k in range(TOP_K):
        gate_ref[k:k + 1, :] = exps[k] * ginv

    onehot = jnp.zeros((N_EXPERTS, rows), jnp.float32)
    for sel in sels:
        onehot = onehot + jnp.where(sel, 1.0, 0.0)
    incl = jnp.dot(onehot.astype(jnp.bfloat16), tri_ref[...], preferred_element_type=jnp.float32)
    before = incl - onehot + carry_ref[:, 0:1]
    for k in range(TOP_K):
        rk = jnp.sum(jnp.where(sels[k], before, 0.0), axis=0, keepdims=True)
        rank_ref[k:k + 1, :] = rk.astype(jnp.int32)
    carry = carry_ref[...] + jnp.sum(onehot, axis=1, keepdims=True)
    carry_ref[...] = carry
    cnt_ref[...] = carry.astype(jnp.int32)


def _out_proj_router(oa, outs_b, lses_b, x2, wo_bf, g, wr, br, seq):
    n = x2.shape[0]
    rows = PROJ_ROWS
    steps = seq // rows
    row_spec = lambda w: pl.BlockSpec((rows, w), lambda i: (i, 0))
    full = lambda a: pl.BlockSpec(a.shape, lambda i: (0,) * a.ndim)
    col_spec = pl.BlockSpec((TOP_K, rows), lambda i: (0, i))

    def branch_spec(dil, w):
        if dil == 1:
            return row_spec(w)
        return pl.BlockSpec((1, dil, rows // dil, w), lambda i: (i // steps, 0, i % steps, 0))

    in_specs = ([row_spec(A_Q_W)]
                + [branch_spec(d, B_W) for d in B_DILS]
                + [branch_spec(d, LANES) for d in B_DILS]
                + [row_spec(D_MODEL), full(wo_bf), full(g), full(wr), full(br)])
    return pl.pallas_call(
        _out_proj_router_kernel,
        out_shape=[
            jax.ShapeDtypeStruct((n, D_MODEL), jnp.float32),
            jax.ShapeDtypeStruct((n, HALF_D), jnp.uint32),
            jax.ShapeDtypeStruct((TOP_K, n), jnp.int32),
            jax.ShapeDtypeStruct((TOP_K, n), jnp.float32),
            jax.ShapeDtypeStruct((TOP_K, n), jnp.int32),
            jax.ShapeDtypeStruct((N_EXPERTS, LANES), jnp.int32),
        ],
        grid=(n // rows,),
        in_specs=in_specs,
        out_specs=[row_spec(D_MODEL), row_spec(HALF_D), col_spec, col_spec, col_spec,
                   pl.BlockSpec((N_EXPERTS, LANES), lambda i: (0, 0))],
        scratch_shapes=[pltpu.VMEM((rows, rows), jnp.bfloat16),
                        pltpu.VMEM((N_EXPERTS, LANES), jnp.float32),
                        pltpu.VMEM((len(B_DILS), B_W // LANES, rows, LANES), jnp.float32),
                        pltpu.VMEM((len(B_DILS), rows, LANES), jnp.float32),
                        pltpu.VMEM(wo_bf.shape, jnp.bfloat16)],
        compiler_params=pltpu.CompilerParams(
            dimension_semantics=("arbitrary",), vmem_limit_bytes=LARGE_VMEM_LIMIT),
        name="out_proj_router",
    )(oa, *outs_b, *lses_b, x2, wo_bf, g, wr, br)


def _mxu_dot(a_bf, w_f32):
    return lax.dot_general(a_bf, w_f32, (((1,), (0,)), ((), ())), preferred_element_type=jnp.float32)


def _moe_kernel(blk_exp_ref, first_ref, slot_ref, next_exp_ref, n_used_ref,
                x_ref, w1_hbm, b1_ref, w2_hbm, b2_ref, y_ref, w1_buf, w2_buf, sem):
    step = pl.program_id(0)

    def weight_copies(expert, slot):
        return (pltpu.make_async_copy(w1_hbm.at[expert], w1_buf.at[slot], sem.at[slot, 0]),
                pltpu.make_async_copy(w2_hbm.at[expert], w2_buf.at[slot], sem.at[slot, 1]))

    @pl.when(step * MOE_STEP_BLOCKS < n_used_ref[0])
    def _():
        @pl.when(step == 0)
        def _():
            for cp in weight_copies(blk_exp_ref[0], slot_ref[0]):
                cp.start()

        for j in range(MOE_STEP_BLOCKS):
            i = step * MOE_STEP_BLOCKS + j

            @pl.when(first_ref[i] == 1)
            def _():
                slot = slot_ref[i]
                for cp in weight_copies(blk_exp_ref[i], slot):
                    cp.wait()

                @pl.when(next_exp_ref[i] >= 0)
                def _():
                    for cp in weight_copies(next_exp_ref[i], (slot + 1) % MOE_WEIGHT_BUFS):
                        cp.start()

        for j in range(MOE_STEP_BLOCKS):
            i = step * MOE_STEP_BLOCKS + j
            slot = slot_ref[i]
            expert = blk_exp_ref[i]
            rs = slice(j * MOE_ROWS, (j + 1) * MOE_ROWS)
            x = jnp.concatenate(_unpack_halves(x_ref[rs, :]), axis=-1).astype(jnp.bfloat16)
            acc = jnp.zeros((MOE_ROWS, D_MODEL), jnp.float32)
            for c in range(D_FF // FF_CHUNK):
                lo = c * FF_CHUNK
                glu = _mxu_dot(x, w1_buf[slot, :, lo:lo + FF_CHUNK]) + b1_ref[expert, :, lo:lo + FF_CHUNK]
                lin = (_mxu_dot(x, w1_buf[slot, :, D_FF + lo:D_FF + lo + FF_CHUNK])
                       + b1_ref[expert, :, D_FF + lo:D_FF + lo + FF_CHUNK])
                glu = jnp.minimum(glu, SWIGLU_LIMIT)
                lin = jnp.clip(lin, -SWIGLU_LIMIT, SWIGLU_LIMIT)
                act = glu * (1.0 / (1.0 + jnp.exp(-SWIGLU_ALPHA * glu))) * (lin + 1.0)
                acc = acc + _mxu_dot(act.astype(jnp.bfloat16), w2_buf[slot, lo:lo + FF_CHUNK, :])
            y_ref[rs, :] = _pack_halves(acc + b2_ref[expert])


def _moe_plan(pends, n_blk):
    g = MOE_ROWS
    blk_row0 = jnp.arange(n_blk, dtype=jnp.int32) * g
    blk_exp = jnp.minimum(jnp.sum(pends[None, :] <= blk_row0[:, None], axis=-1),
                          N_EXPERTS - 1).astype(jnp.int32)
    n_used = (pends[-1] // g).astype(jnp.int32)
    used = blk_row0 < pends[-1]
    prev_exp = jnp.concatenate([jnp.full((1,), -1, jnp.int32), blk_exp[:-1]])
    first = (used & (blk_exp != prev_exp)).astype(jnp.int32)
    slot = ((jnp.cumsum(first) - 1) % MOE_WEIGHT_BUFS).astype(jnp.int32)
    pstarts = jnp.concatenate([jnp.zeros((1,), pends.dtype), pends[:-1]])
    nonempty = pends > pstarts
    experts = jnp.arange(N_EXPERTS, dtype=jnp.int32)
    later = nonempty[None, :] & (experts[None, :] > experts[:, None])
    next_nonempty = jnp.min(jnp.where(later, experts[None, :], N_EXPERTS), axis=-1)
    next_nonempty = jnp.where(next_nonempty == N_EXPERTS, -1, next_nonempty).astype(jnp.int32)
    next_exp = jnp.sum(jnp.where(blk_exp[:, None] == experts[None, :], next_nonempty[None, :], 0),
                       axis=-1).astype(jnp.int32)
    return blk_exp, first, slot, next_exp, n_used.reshape(1)


def _moe_experts(plan, xb, w1, b1, w2, b2):
    n_rows = xb.shape[0]
    n_blk = n_rows // MOE_ROWS

    step_rows = MOE_STEP_BLOCKS * MOE_ROWS

    def blk(i, *p):
        return jnp.minimum(i, (p[-1][0] - 1) // MOE_STEP_BLOCKS)

    grid_spec = pltpu.PrefetchScalarGridSpec(
        num_scalar_prefetch=len(plan),
        grid=(n_blk // MOE_STEP_BLOCKS,),
        in_specs=[
            pl.BlockSpec((step_rows, HALF_D), lambda i, *p: (blk(i, *p), 0)),
            pl.BlockSpec(memory_space=pl.ANY),
            pl.BlockSpec(b1.shape, lambda i, *p: (0, 0, 0)),
            pl.BlockSpec(memory_space=pl.ANY),
            pl.BlockSpec(b2.shape, lambda i, *p: (0, 0, 0)),
        ],
        out_specs=pl.BlockSpec((step_rows, HALF_D), lambda i, *p: (blk(i, *p), 0)),
        scratch_shapes=[pltpu.VMEM((MOE_WEIGHT_BUFS, D_MODEL, 2 * D_FF), jnp.float32),
                        pltpu.VMEM((MOE_WEIGHT_BUFS, D_FF, D_MODEL), jnp.float32),
                        pltpu.SemaphoreType.DMA((MOE_WEIGHT_BUFS, 2))],
    )
    return pl.pallas_call(
        _moe_kernel,
        out_shape=jax.ShapeDtypeStruct((n_rows, HALF_D), jnp.uint32),
        grid_spec=grid_spec,
        compiler_params=pltpu.CompilerParams(
            dimension_semantics=("arbitrary",), vmem_limit_bytes=LARGE_VMEM_LIMIT),
        name="moe_experts",
    )(*plan, xb, w1, b1, w2, b2)


def _sc_worker_id():
    return lax.axis_index("s") * SC_CORES + lax.axis_index("c")


def _sc_dispatch(hf, dest3, n_rows):
    n = hf.shape[0]
    chunks_per_worker = n // SC_CHUNK // SC_WORKERS
    mesh = plsc.VectorSubcoreMesh(core_axis_name="c", subcore_axis_name="s")

    @functools.partial(
        pl.kernel, mesh=mesh,
        out_type=jax.ShapeDtypeStruct((n_rows, HALF_D), hf.dtype),
        scratch_types=[pltpu.VMEM((TOP_K, chunks_per_worker, SC_CHUNK), jnp.int32),
                       pltpu.VMEM((2, SC_CHUNK, HALF_D), hf.dtype),
                       pltpu.SemaphoreType.DMA((2,)),
                       pltpu.SemaphoreType.DMA((2,))],
        name="sc_dispatch")
    def run(hf_hbm, dest_hbm, xb_hbm, idx_v, rows_v, load_sem, scatter_sem):
        first = _sc_worker_id() * chunks_per_worker
        pltpu.sync_copy(dest_hbm.at[:, pl.ds(first, chunks_per_worker)], idx_v)

        def load(j):
            slot = j % 2
            return pltpu.async_copy(hf_hbm.at[pl.ds((first + j) * SC_CHUNK, SC_CHUNK)],
                                    rows_v.at[slot], load_sem.at[slot])

        loads = {0: load(0)}
        scatters = {}
        for j in range(chunks_per_worker):
            slot = j % 2
            loads.pop(j).wait()
            scatters[j] = [pltpu.async_copy(rows_v.at[slot], xb_hbm.at[idx_v.at[k, j]],
                                            scatter_sem.at[slot]) for k in range(TOP_K)]
            if j >= 1:
                for cp in scatters.pop(j - 1):
                    cp.wait()
            if j + 1 < chunks_per_worker:
                loads[j + 1] = load(j + 1)
        for cp in scatters.pop(chunks_per_worker - 1):
            cp.wait()

    return run(hf, dest3)


def _sc_collect_sum(y, dest3, gates16, first_chunk, n_chunks):
    C = SC_COLLECT_CHUNK
    n = n_chunks * C
    chunks_per_worker = n_chunks // SC_WORKERS
    lanes = 16
    mesh = plsc.VectorSubcoreMesh(core_axis_name="c", subcore_axis_name="s")

    @functools.partial(
        pl.kernel, mesh=mesh,
        out_type=jax.ShapeDtypeStruct((n, HALF_D), y.dtype),
        scratch_types=[pltpu.VMEM((TOP_K, chunks_per_worker, C), jnp.int32),
                       pltpu.VMEM((TOP_K, chunks_per_worker, C), jnp.uint32),
                       pltpu.VMEM((2, TOP_K, C, HALF_D), y.dtype),
                       pltpu.VMEM((2, C, HALF_D), y.dtype),
                       pltpu.SemaphoreType.DMA((2, TOP_K)),
                       pltpu.SemaphoreType.DMA((2,))],
        compiler_params=pltpu.CompilerParams(needs_layout_passes=False),
        name="sc_collect_sum")
    def run(y_hbm, dest_hbm, gate_hbm, out_hbm, idx_v, gate_v, rows_v, sum_v, gather_sem, write_sem):
        local0 = _sc_worker_id() * chunks_per_worker

        pltpu.sync_copy(dest_hbm.at[:, pl.ds(first_chunk + local0, chunks_per_worker)], idx_v)
        pltpu.sync_copy(gate_hbm.at[:, pl.ds(first_chunk + local0, chunks_per_worker)], gate_v)

        def gather_copies(i, slot):
            return [pltpu.make_async_copy(y_hbm.at[idx_v.at[k, i]], rows_v.at[slot, k],
                                          gather_sem.at[slot, k]) for k in range(TOP_K)]

        def write_copy(i, slot):
            return pltpu.make_async_copy(sum_v.at[slot], out_hbm.at[pl.ds((local0 + i) * C, C)],
                                         write_sem.at[slot])

        def fetch(i, slot):
            for cp in gather_copies(i, slot):
                cp.start()

        def reduce_chunk(i, slot):
            for cp in gather_copies(i, slot):
                cp.wait()

            @pl.when(i >= 2)
            def _():
                write_copy(i - 2, slot).wait()

            @plsc.parallel_loop(0, C)
            def _(j):
                lane_j = jnp.full((lanes,), j, jnp.int32)
                g = [plsc.bitcast(gate_v[k, i, :].at[lane_j].get(mode="promise_in_bounds"),
                                  jnp.bfloat16) for k in range(TOP_K)]
                for c in range(HALF_D // lanes):
                    sl = pl.ds(c * lanes, lanes)
                    acc = None
                    for k in range(TOP_K):
                        term = plsc.bitcast(rows_v[slot, k, j, sl], jnp.bfloat16) * g[k]
                        acc = term if acc is None else acc + term
                    sum_v[slot, j, sl] = plsc.bitcast(acc, y.dtype)

            write_copy(i, slot).start()

        fetch(0, 0)

        @pl.loop(0, chunks_per_worker, step=2)
        def _(i):
            fetch(i + 1, 1)
            reduce_chunk(i, 0)

            @pl.when(i + 2 < chunks_per_worker)
            def _():
                fetch(i + 2, 0)

            reduce_chunk(i + 1, 1)

        write_copy(chunks_per_worker - 2, 0).wait()
        write_copy(chunks_per_worker - 1, 1).wait()

    return run(y, dest3, gates16)


def _combine_sum_kernel(x1_ref, ys_ref, o_ref):
    lo, hi = _unpack_halves(ys_ref[...])
    o_ref[:, :HALF_D] = x1_ref[:, :HALF_D] + lo
    o_ref[:, HALF_D:] = x1_ref[:, HALF_D:] + hi


def _combine_sum(acc, ysum, split):
    n = acc.shape[0]
    rows = PROJ_ROWS
    steps = ysum.shape[0] // rows
    first = split * steps
    return pl.pallas_call(
        _combine_sum_kernel,
        out_shape=jax.ShapeDtypeStruct((n, D_MODEL), jnp.float32),
        grid=(steps,),
        in_specs=[pl.BlockSpec((rows, D_MODEL), lambda i: (first + i, 0)),
                  pl.BlockSpec((rows, HALF_D), lambda i: (i, 0))],
        out_specs=pl.BlockSpec((rows, D_MODEL), lambda i: (first + i, 0)),
        input_output_aliases={0: 0},
        compiler_params=pltpu.CompilerParams(
            dimension_semantics=("arbitrary",), vmem_limit_bytes=VMEM_LIMIT),
        name="moe_combine",
    )(acc, ysum)


def _layer(x2, batch, seq, attn_norm_g, w_in, a_q_g, a_k_g, a_sinks, b_q_g, b_k_g, w_out,
           ffn_norm_g, w_router, b_router, w1, b1, w2, b2):
    n = x2.shape[0]
    slopes = _alibi_slopes()
    q_scale = HEAD_DIM ** -0.5 * LOG2E
    reps = MXU_DIM // HEAD_DIM
    gains = jnp.stack([jnp.tile(a_q_g, reps) * q_scale, jnp.tile(a_k_g, reps),
                       jnp.tile(b_q_g, reps) * q_scale, jnp.tile(b_k_g, reps)]).astype(jnp.float32)

    proj = _in_proj(x2, attn_norm_g.reshape(1, -1), w_in, gains, batch, seq)
    qa, ka, va = proj[:3]
    nb = len(B_DILS)
    qbs, kbs, vbs = proj[3:3 + nb], proj[3 + nb:3 + 2 * nb], proj[3 + 2 * nb:]

    bias_a = _bias_tables(slopes[:A_Q_HEADS], A_STACK_HEADS, A_HALF_WINDOW, 1, Q_TILE + 2 * A_HALF_WINDOW)
    sink_col = jnp.repeat(a_sinks.astype(jnp.float32) * LOG2E, Q_TILE).reshape(
        A_Q_HEADS // A_STACK_HEADS, A_STACK_HEADS * Q_TILE, 1)
    as_seqs = lambda a: a.reshape(batch, seq, a.shape[-1])
    out_a = _banded_attention(as_seqs(qa), as_seqs(ka), as_seqs(va), bias_a, half_w=A_HALF_WINDOW,
                              sink=sink_col, name="attn_a")[0].reshape(n, A_Q_W)

    outs_b, lses_b = [], []
    for bi, (window, dil) in enumerate(B_BRANCHES):
        half_w = window // (2 * dil)
        bias_b = _bias_tables(slopes[A_Q_HEADS:], B_STACK_HEADS, half_w, dil, Q_TILE + 2 * half_w)
        L = seq // dil
        to_seqs = lambda a: a.reshape(batch * dil, L, a.shape[-1])
        o, lse = _banded_attention(to_seqs(qbs[bi]), to_seqs(kbs[bi]), to_seqs(vbs[bi]), bias_b,
                                   half_w=half_w, want_lse=True, name=f"attn_b_d{dil}")
        if dil == 1:
            outs_b.append(o.reshape(n, B_W))
            lses_b.append(lse.reshape(n, LANES))
        else:
            outs_b.append(o.reshape(batch, dil, L, B_W))
            lses_b.append(lse.reshape(batch, dil, L, LANES))

    wr_t = w_router.T.astype(jnp.float32)
    wr_hi = wr_t.astype(jnp.bfloat16)
    wr_lo = (wr_t - wr_hi.astype(jnp.float32)).astype(jnp.bfloat16)
    wr = jnp.concatenate([wr_hi, wr_lo], axis=0)
    br = jnp.broadcast_to(b_router.astype(jnp.float32)[:, None], (N_EXPERTS, LANES))
    x1, hf, topi, gates, ranks, counts = _out_proj_router(
        out_a, outs_b, lses_b, x2, w_out, ffn_norm_g.reshape(1, -1), wr, br, seq)

    g = MOE_ROWS
    nk = n * TOP_K
    step_rows = MOE_STEP_BLOCKS * g
    n_rows = -(-(nk + N_EXPERTS * g) // step_rows) * step_rows
    cnt = counts[:, 0]
    pcnt = (cnt + g - 1) // g * g
    pends = jnp.cumsum(pcnt)
    pstarts = pends - pcnt
    experts = jnp.arange(N_EXPERTS, dtype=jnp.int32)
    start_of = jnp.sum(jnp.where(topi[:, :, None] == experts, pstarts, 0), axis=-1)
    dest = (start_of + ranks).astype(jnp.int32)
    plan = _moe_plan(pends, n_rows // g)
    dest3 = dest.reshape(TOP_K, n // SC_CHUNK, SC_CHUNK)

    xb = _sc_dispatch(hf, dest3, n_rows)
    y = _moe_experts(plan, xb, w1, b1[:, None, :], w2, b2[:, None, :])
    smallest_range = max(PROJ_ROWS, SC_COLLECT_CHUNK * SC_WORKERS)
    n_splits = max(1, min(COMBINE_SPLITS, n // smallest_range))
    per_split = n // n_splits
    chunked = lambda a: a.reshape(TOP_K, n // SC_COLLECT_CHUNK, SC_COLLECT_CHUNK)
    dest_c = chunked(dest)
    gate_bits = lax.bitcast_convert_type(gates.astype(jnp.bfloat16), jnp.uint16).astype(jnp.uint32)
    gate_words = gate_bits | (gate_bits << 16)
    gates_c = chunked(gate_words)
    chunks_per_split = per_split // SC_COLLECT_CHUNK
    out = x1
    for s in range(n_splits):
        ysum = _sc_collect_sum(y, dest_c, gates_c, s * chunks_per_split, chunks_per_split)
        out = _combine_sum(out, ysum, s)
    return out


def kernel(x, attn_norm_g, w_in, a_q_norm_g, a_k_norm_g, a_sinks, b_q_norm_g, b_k_norm_g, w_out,
           ffn_norm_g, w_router, b_router, w1, b1, w2, b2):
    batch, seq, d = x.shape
    x2 = x.reshape(batch * seq, d)
    for i in range(attn_norm_g.shape[0]):
        x2 = _layer(x2, batch, seq, attn_norm_g[i], w_in[i], a_q_norm_g[i], a_k_norm_g[i],
                    a_sinks[i], b_q_norm_g[i], b_k_norm_g[i], w_out[i], ffn_norm_g[i],
                    w_router[i], b_router[i], w1[i], b1[i], w2[i], b2[i])
    return x2.reshape(batch, seq, d)
```

```python
import functools

import jax
import jax.numpy as jnp
import numpy as np
from jax import lax
from jax.experimental import pallas as pl
from jax.experimental.pallas import tpu as pltpu
from jax.experimental.pallas import tpu_sc as plsc

D_MODEL = 1024
HALF_D = D_MODEL // 2
HEAD_DIM = 64
LANES = 128
MXU_DIM = 256
A_Q_HEADS = 8
A_KV_HEADS = 2
B_HEADS = 8
A_HALF_WINDOW = 128
B_BRANCHES = ((128, 1), (512, 4), (2048, 16))
B_DILS = tuple(d for _, d in B_BRANCHES)
RESIDUE_STRIDE = 4
N_ALIBI_HEADS = 16
A_Q_W = A_Q_HEADS * HEAD_DIM
A_KV_W = A_KV_HEADS * HEAD_DIM
B_W = B_HEADS * HEAD_DIM
N_EXPERTS = 32
TOP_K = 4
D_FF = 1024
SWIGLU_ALPHA = 1.702
SWIGLU_LIMIT = 7.0
NORM_EPS = 1e-5
MASK_VALUE = -1e30
LOG2E = 1.4426950408889634

Q_TILE = 128
ATTN_STEP_ROWS = 2048
A_STACK_HEADS = 4
B_STACK_HEADS = 2
PROJ_ROWS = 1024
IN_PROJ_ROWS = 1024
MOE_ROWS = 512
FF_CHUNK = 512
MOE_STEP_BLOCKS = 2
MOE_WEIGHT_BUFS = 3
VMEM_LIMIT = 48 * 1024 * 1024
LARGE_VMEM_LIMIT = 58 * 1024 * 1024
SC_CORES = 2
SC_SUBCORES = 16
SC_WORKERS = SC_CORES * SC_SUBCORES
SC_CHUNK = 64
SC_COLLECT_CHUNK = 16
COMBINE_SPLITS = 2


def _pack_halves(v):
    lo = v[:, :HALF_D].astype(jnp.bfloat16).astype(jnp.float32)
    hi = v[:, HALF_D:].astype(jnp.bfloat16).astype(jnp.float32)
    return (pltpu.bitcast(lo, jnp.uint32) >> 16) | pltpu.bitcast(hi, jnp.uint32)


def _unpack_halves(w):
    lo = pltpu.bitcast(w << 16, jnp.float32)
    hi = pltpu.bitcast(w & jnp.uint32(0xFFFF0000), jnp.float32)
    return lo, hi


def _alibi_slopes():
    return np.exp2(-8.0 * np.arange(1, N_ALIBI_HEADS + 1, dtype=np.float32) / N_ALIBI_HEADS).astype(np.float32)


def _bias_tables(head_slopes, heads_per_group, half_w, dist_scale, tk):
    i = np.arange(Q_TILE)[:, None]
    j = np.arange(tk)[None, :]
    tabs = []
    for shift in (0, half_w, tk - Q_TILE):
        dist = np.abs(j - shift - i)
        valid = dist <= half_w
        per_head = []
        for sl in head_slopes:
            b = (-np.float64(sl) * LOG2E * (dist * dist_scale)).astype(np.float32)
            per_head.append(np.where(valid, b, np.float32(MASK_VALUE)).astype(np.float32))
        t = np.stack(per_head).reshape(-1, heads_per_group * Q_TILE, tk)
        tabs.append(t)
    return jnp.asarray(np.stack(tabs))


def _in_proj_kernel(x_ref, g_ref, w_hbm, gains_ref, qa_ref, ka_ref, va_ref, *rest):
    b_refs, (scr_ref, scr2_ref, w_ref, stage_ref, sem) = rest[:-5], rest[-5:]

    @pl.when(pl.program_id(0) == 0)
    def _():
        width = stage_ref.shape[1]
        for c0 in range(0, w_ref.shape[1], width):
            cp = pltpu.make_async_copy(w_hbm.at[:, c0:c0 + width], stage_ref, sem)
            cp.start()
            cp.wait()
            w_ref[:, c0:c0 + width] = stage_ref[...].astype(jnp.bfloat16)

    x = x_ref[...]
    xn = x * lax.rsqrt(jnp.mean(x * x, axis=-1, keepdims=True) + NORM_EPS) * g_ref[...]
    xn = xn.astype(jnp.bfloat16)
    r = lax.broadcasted_iota(jnp.int32, (MXU_DIM, MXU_DIM), 0) // HEAD_DIM
    c = lax.broadcasted_iota(jnp.int32, (MXU_DIM, MXU_DIM), 1) // HEAD_DIM
    blockdiag = jnp.where(r == c, 1.0, 0.0).astype(jnp.bfloat16)

    def head_rms(sec, gain_row):
        width = sec.shape[1]
        parts = []
        step = min(width, MXU_DIM)
        for j in range(width // step):
            p = sec[:, j * step:(j + 1) * step]
            ss = jnp.dot((p * p).astype(jnp.bfloat16), blockdiag[:step, :step],
                         preferred_element_type=jnp.float32)
            parts.append(p * lax.rsqrt(ss * (1.0 / HEAD_DIM) + NORM_EPS)
                         * gains_ref[gain_row:gain_row + 1, :step])
        return parts

    def project(col0, width, gain_row):
        sec = jnp.dot(xn, w_ref[:, col0:col0 + width], preferred_element_type=jnp.float32)
        return [sec] if gain_row is None else head_rms(sec, gain_row)

    def store(out_ref, parts):
        w = parts[0].shape[1]
        for j, p in enumerate(parts):
            out_ref[:, j * w:(j + 1) * w] = p.astype(out_ref.dtype)

    def per_kv_head(p):
        lane = lax.broadcasted_iota(jnp.int32, p.shape, 1)
        swapped = pltpu.roll(p, HEAD_DIM, axis=1)
        low = lane < HEAD_DIM
        return [jnp.where(low, p, swapped), jnp.where(low, swapped, p)]

    rows = x_ref.shape[0]
    col0 = A_Q_W + 2 * A_KV_W
    for t, gain_row in enumerate((2, 3, None)):
        parts = project(col0 + t * B_W, B_W, gain_row)
        sec = jnp.concatenate(parts, axis=-1) if len(parts) > 1 else parts[0]
        for j in range(B_W // LANES):
            scr_ref[j] = sec[:, j * LANES:(j + 1) * LANES]
        n_lane_chunks = B_W // LANES
        prev_dil = 1
        for bi, dil in enumerate(B_DILS):
            out_ref = b_refs[t * len(B_DILS) + bi]
            if dil == 1:
                out_ref[...] = sec.astype(out_ref.dtype)
                continue
            assert dil == prev_dil * RESIDUE_STRIDE
            last = dil == B_DILS[-1]
            for res in range(dil):
                r_prev, r_sub = res % prev_dil, res // prev_dil
                for j in range(n_lane_chunks):
                    if prev_dil == 1:
                        v = scr_ref[j, pl.ds(r_sub, rows // dil, stride=RESIDUE_STRIDE), :]
                    else:
                        v = scr2_ref[r_prev * n_lane_chunks + j,
                                     pl.ds(r_sub, rows // dil, stride=RESIDUE_STRIDE), :]
                    out_ref[0, res, :, j * LANES:(j + 1) * LANES] = v.astype(out_ref.dtype)
                    if not last:
                        scr2_ref[res * n_lane_chunks + j] = v
            prev_dil = dil

    store(qa_ref, project(0, A_Q_W, 0))
    kva = project(A_Q_W, 2 * A_KV_W, None)[0]
    store(ka_ref, per_kv_head(head_rms(kva[:, :A_KV_W], 1)[0]))
    store(va_ref, per_kv_head(kva[:, A_KV_W:]))


def _in_proj(x2, g, w_in, gains, batch, seq):
    n = x2.shape[0]
    rows = IN_PROJ_ROWS
    steps = seq // rows
    a_widths = (A_Q_W, 2 * A_KV_W, 2 * A_KV_W)
    out_shape = [jax.ShapeDtypeStruct((n, w), jnp.bfloat16) for w in a_widths]
    out_specs = [pl.BlockSpec((rows, w), lambda i: (i, 0)) for w in a_widths]
    for _ in range(3):
        for dil in B_DILS:
            if dil == 1:
                out_shape.append(jax.ShapeDtypeStruct((n, B_W), jnp.bfloat16))
                out_specs.append(pl.BlockSpec((rows, B_W), lambda i: (i, 0)))
            else:
                out_shape.append(jax.ShapeDtypeStruct((batch, dil, seq // dil, B_W), jnp.bfloat16))
                out_specs.append(pl.BlockSpec((1, dil, rows // dil, B_W),
                                              lambda i: (i // steps, 0, i % steps, 0)))
    return pl.pallas_call(
        _in_proj_kernel,
        out_shape=out_shape,
        grid=(n // rows,),
        in_specs=[
            pl.BlockSpec((rows, D_MODEL), lambda i: (i, 0)),
            pl.BlockSpec((1, D_MODEL), lambda i: (0, 0)),
            pl.BlockSpec(memory_space=pl.ANY),
            pl.BlockSpec(gains.shape, lambda i: (0, 0)),
        ],
        out_specs=out_specs,
        scratch_shapes=[pltpu.VMEM((B_W // LANES, rows, LANES), jnp.float32),
                        pltpu.VMEM((RESIDUE_STRIDE * B_W // LANES, rows // RESIDUE_STRIDE, LANES),
                                   jnp.float32),
                        pltpu.VMEM(w_in.shape, jnp.bfloat16),
                        pltpu.VMEM((w_in.shape[0], w_in.shape[1] // 3), jnp.float32),
                        pltpu.SemaphoreType.DMA],
        compiler_params=pltpu.CompilerParams(
            dimension_semantics=("arbitrary",), vmem_limit_bytes=LARGE_VMEM_LIMIT),
        name="in_proj",
    )(x2, g, w_in, gains)


def _attn_kernel(*refs, n_chunks, kv_chunks, heads_per_stack, tk, half_w, seq_len, rows, has_sink,
                 want_lse):
    it = iter(refs)
    q_ref, k_ref, v_ref, bias_ref = next(it), next(it), next(it), next(it)
    sink_ref = next(it) if has_sink else None
    o_ref = next(it)
    lse_ref = next(it) if want_lse else None

    n_tiles = seq_len // Q_TILE
    tiles_per_step = rows // Q_TILE
    chunks_per_group = n_chunks // kv_chunks
    assert (2 * chunks_per_group) % heads_per_stack == 0
    step = pl.program_id(1)
    lane = lax.broadcasted_iota(jnp.int32, (Q_TILE, LANES), 1)
    low_half = lane < HEAD_DIM
    ones = jnp.ones((tk, LANES), jnp.bfloat16)

    for sq, t in [(a, b) for a in range(q_ref.shape[0]) for b in range(tiles_per_step)]:
        tile = step * tiles_per_step + t
        q0 = tile * Q_TILE
        kv_rows = k_ref.shape[1]
        kv_row0 = jnp.clip(step * rows - half_w, 0, seq_len - kv_rows)
        start = pl.multiple_of(jnp.clip(q0 - half_w, 0, seq_len - tk) - kv_row0, HEAD_DIM)
        variant = jnp.where(tile == 0, 0, jnp.where(tile == n_tiles - 1, 2, 1))
        r0 = t * Q_TILE
        lse_tile = jnp.zeros((Q_TILE, LANES), jnp.float32)
        head_o, head_lse = {}, {}
        for g in range(2 * n_chunks // heads_per_stack):
            heads = range(g * heads_per_stack, (g + 1) * heads_per_stack)
            kv = (heads[0] // 2) // chunks_per_group
            kc = k_ref[sq, pl.ds(start, tk), kv * LANES:(kv + 1) * LANES]
            vc = v_ref[sq, pl.ds(start, tk), kv * LANES:(kv + 1) * LANES]
            v_aug = jnp.concatenate([vc, ones], axis=1)
            q_parts = []
            for h in heads:
                c = h // 2
                q2 = q_ref[sq, r0:r0 + Q_TILE, c * LANES:(c + 1) * LANES]
                keep = low_half if h % 2 == 0 else ~low_half
                q_parts.append(jnp.where(keep, q2, jnp.zeros_like(q2)))
            qs = q_parts[0] if len(q_parts) == 1 else jnp.concatenate(q_parts, axis=0)
            s = lax.dot_general(qs, kc, (((1,), (1,)), ((), ())),
                                preferred_element_type=jnp.float32)
            s = s + bias_ref[variant, g]
            m = jnp.max(s, axis=-1, keepdims=True)
            if has_sink:
                m = jnp.maximum(m, sink_ref[g])
            p = jnp.exp2(s - m)
            ov = jnp.dot(p.astype(jnp.bfloat16), v_aug, preferred_element_type=jnp.float32)
            o, l = ov[:, :LANES], ov[:, LANES:]
            if has_sink:
                l = l + jnp.exp2(sink_ref[g] - m)
            o = o * (1.0 / l)
            if want_lse:
                lse = m + jnp.log(l) * LOG2E
            for idx, h in enumerate(heads):
                head_o[h] = o[idx * Q_TILE:(idx + 1) * Q_TILE]
                if want_lse:
                    head_lse[h] = lse[idx * Q_TILE:(idx + 1) * Q_TILE]
                if h % 2 == 1:
                    c = h // 2
                    o2 = jnp.where(low_half, head_o.pop(h - 1), head_o.pop(h))
                    o_ref[sq, r0:r0 + Q_TILE, c * LANES:(c + 1) * LANES] = o2.astype(o_ref.dtype)
                    if want_lse:
                        lse_tile = jnp.where(lane == h - 1, head_lse.pop(h - 1),
                                             jnp.where(lane == h, head_lse.pop(h), lse_tile))
        if want_lse:
            lse_ref[sq, r0:r0 + Q_TILE, :] = lse_tile


def _banded_attention(q, k, v, bias, *, half_w, sink=None, want_lse=False, name):
    n_seq, L, qw = q.shape
    kw = k.shape[2]
    tk = Q_TILE + 2 * half_w
    rows = min(ATTN_STEP_ROWS, L)
    seqs = ATTN_STEP_ROWS // rows
    kv_rows = min(L, rows + 2 * half_w)

    def kv_index(s, i):
        row0 = pl.multiple_of(jnp.clip(i * rows - half_w, 0, L - kv_rows), HEAD_DIM)
        return s * seqs, row0, 0

    args = [q, k, v, bias]
    in_specs = [
        pl.BlockSpec((seqs, rows, qw), lambda s, i: (s, i, 0)),
        pl.BlockSpec((pl.Element(seqs), pl.Element(kv_rows), pl.Element(kw)), kv_index),
        pl.BlockSpec((pl.Element(seqs), pl.Element(kv_rows), pl.Element(kw)), kv_index),
        pl.BlockSpec(bias.shape, lambda s, i: (0, 0, 0, 0)),
    ]
    if sink is not None:
        args.append(sink)
        in_specs.append(pl.BlockSpec(sink.shape, lambda s, i: (0, 0, 0)))
    out_shape = [jax.ShapeDtypeStruct((n_seq, L, qw), jnp.bfloat16)]
    out_specs = [pl.BlockSpec((seqs, rows, qw), lambda s, i: (s, i, 0))]
    if want_lse:
        out_shape.append(jax.ShapeDtypeStruct((n_seq, L, LANES), jnp.float32))
        out_specs.append(pl.BlockSpec((seqs, rows, LANES), lambda s, i: (s, i, 0)))

    kern = functools.partial(
        _attn_kernel, n_chunks=qw // LANES, kv_chunks=kw // LANES,
        heads_per_stack=bias.shape[2] // Q_TILE, tk=tk, half_w=half_w, seq_len=L,
        rows=rows, has_sink=sink is not None, want_lse=want_lse)
    return pl.pallas_call(
        kern,
        out_shape=out_shape,
        grid=(n_seq // seqs, L // rows),
        in_specs=in_specs,
        out_specs=out_specs,
        compiler_params=pltpu.CompilerParams(
            dimension_semantics=("arbitrary", "arbitrary"), vmem_limit_bytes=VMEM_LIMIT),
        name=name,
    )(*args)


def _out_proj_router_kernel(*refs):
    nb = len(B_DILS)
    oa_ref = refs[0]
    o_refs = refs[1:1 + nb]
    lse_refs = refs[1 + nb:1 + 2 * nb]
    (x_ref, wo_ref, g_ref, wr_ref, br_ref,
     x1_ref, hf_ref, topi_ref, gate_ref, rank_ref, cnt_ref,
     tri_ref, carry_ref, so_ref, sl_ref, wo_bf_ref) = refs[1 + 2 * nb:]
    i = pl.program_id(0)
    rows = x_ref.shape[0]

    @pl.when(i == 0)
    def _():
        a = lax.broadcasted_iota(jnp.int32, (rows, rows), 0)
        b = lax.broadcasted_iota(jnp.int32, (rows, rows), 1)
        tri_ref[...] = jnp.where(a <= b, 1.0, 0.0).astype(jnp.bfloat16)
        carry_ref[...] = jnp.zeros_like(carry_ref)
        wo_bf_ref[...] = wo_ref[...].astype(jnp.bfloat16)

    outs, lses = [], []
    for bi, dil in enumerate(B_DILS):
        if dil == 1:
            outs.append(o_refs[bi][...].astype(jnp.float32))
            lses.append(lse_refs[bi][...])
        else:
            for res in range(dil):
                for j in range(B_W // LANES):
                    so_ref[bi, j, pl.ds(res, rows // dil, stride=dil), :] = (
                        o_refs[bi][0, res, :, j * LANES:(j + 1) * LANES].astype(jnp.float32))
                sl_ref[bi, pl.ds(res, rows // dil, stride=dil), :] = lse_refs[bi][0, res]
            outs.append(jnp.concatenate([so_ref[bi, j] for j in range(B_W // LANES)], axis=-1))
            lses.append(sl_ref[bi])

    mx = functools.reduce(jnp.maximum, lses)
    es = [jnp.exp2(l - mx) for l in lses]
    inv = 1.0 / functools.reduce(lambda a, b: a + b, es)
    eh = lax.broadcasted_iota(jnp.int32, (LANES, B_W), 0)
    ej = lax.broadcasted_iota(jnp.int32, (LANES, B_W), 1) // HEAD_DIM
    expand = jnp.where(eh == ej, 1.0, 0.0).astype(jnp.bfloat16)
    ob = jnp.zeros((rows, B_W), jnp.float32)
    for e, o in zip(es, outs):
        w = e * inv
        wide = jnp.dot(w.astype(jnp.bfloat16), expand, preferred_element_type=jnp.float32)
        ob = ob + wide * o

    attn = jnp.concatenate([oa_ref[...], ob.astype(jnp.bfloat16)], axis=-1)
    x1 = x_ref[...] + jnp.dot(attn, wo_bf_ref[...], preferred_element_type=jnp.float32)
    x1_ref[...] = x1
    hf = x1 * lax.rsqrt(jnp.mean(x1 * x1, axis=-1, keepdims=True) + NORM_EPS) * g_ref[...]
    hf_hi = hf.astype(jnp.bfloat16)
    hf_ref[...] = _pack_halves(hf)
    hf_lo = (hf - hf_hi.astype(jnp.float32)).astype(jnp.bfloat16)

    nt = (((1,), (1,)), ((), ()))
    lg_hi = lax.dot_general(wr_ref[...], hf_hi, nt, preferred_element_type=jnp.float32)
    lg_lo = lax.dot_general(wr_ref[0:N_EXPERTS, :], hf_lo, nt, preferred_element_type=jnp.float32)
    logits = lg_hi[0:N_EXPERTS] + lg_hi[N_EXPERTS:] + lg_lo + br_ref[:, 0:1]


    eidx = lax.broadcasted_iota(jnp.int32, (N_EXPERTS, rows), 0)
    work = logits
    vals, sels = [], []
    for k in range(TOP_K):
        mk = jnp.max(work, axis=0, keepdims=True)
        ik = jnp.min(jnp.where(work == mk, eidx, N_EXPERTS), axis=0, keepdims=True)
        sel = eidx == ik
        work = jnp.where(sel, -jnp.inf, work)
        vals.append(mk)
        sels.append(sel)
        topi_ref[k:k + 1, :] = ik
    exps = [jnp.exp(vk - vals[0]) for vk in vals]
    denom = exps[0] + exps[1] + exps[2] + exps[3]
    ginv = 1.0 / denom
    for k in range(TOP_K):
        gate_ref[k:k + 1, :] = exps[k] * ginv

    onehot = jnp.zeros((N_EXPERTS, rows), jnp.float32)
    for sel in sels:
        onehot = onehot + jnp.where(sel, 1.0, 0.0)
    incl = jnp.dot(onehot.astype(jnp.bfloat16), tri_ref[...], preferred_element_type=jnp.float32)
    before = incl - onehot + carry_ref[:, 0:1]
    for k in range(TOP_K):
        rk = jnp.sum(jnp.where(sels[k], before, 0.0), axis=0, keepdims=True)
        rank_ref[k:k + 1, :] = rk.astype(jnp.int32)
    carry = carry_ref[...] + jnp.sum(onehot, axis=1, keepdims=True)
    carry_ref[...] = carry
    cnt_ref[...] = carry.astype(jnp.int32)


def _out_proj_router(oa, outs_b, lses_b, x2, wo_bf, g, wr, br, seq):
    n = x2.shape[0]
    rows = PROJ_ROWS
    steps = seq // rows
    row_spec = lambda w: pl.BlockSpec((rows, w), lambda i: (i, 0))
    full = lambda a: pl.BlockSpec(a.shape, lambda i: (0,) * a.ndim)
    col_spec = pl.BlockSpec((TOP_K, rows), lambda i: (0, i))

    def branch_spec(dil, w):
        if dil == 1:
            return row_spec(w)
        return pl.BlockSpec((1, dil, rows // dil, w), lambda i: (i // steps, 0, i % steps, 0))

    in_specs = ([row_spec(A_Q_W)]
                + [branch_spec(d, B_W) for d in B_DILS]
                + [branch_spec(d, LANES) for d in B_DILS]
                + [row_spec(D_MODEL), full(wo_bf), full(g), full(wr), full(br)])
    return pl.pallas_call(
        _out_proj_router_kernel,
        out_shape=[
            jax.ShapeDtypeStruct((n, D_MODEL), jnp.float32),
            jax.ShapeDtypeStruct((n, HALF_D), jnp.uint32),
            jax.ShapeDtypeStruct((TOP_K, n), jnp.int32),
            jax.ShapeDtypeStruct((TOP_K, n), jnp.float32),
            jax.ShapeDtypeStruct((TOP_K, n), jnp.int32),
            jax.ShapeDtypeStruct((N_EXPERTS, LANES), jnp.int32),
        ],
        grid=(n // rows,),
        in_specs=in_specs,
        out_specs=[row_spec(D_MODEL), row_spec(HALF_D), col_spec, col_spec, col_spec,
                   pl.BlockSpec((N_EXPERTS, LANES), lambda i: (0, 0))],
        scratch_shapes=[pltpu.VMEM((rows, rows), jnp.bfloat16),
                        pltpu.VMEM((N_EXPERTS, LANES), jnp.float32),
                        pltpu.VMEM((len(B_DILS), B_W // LANES, rows, LANES), jnp.float32),
                        pltpu.VMEM((len(B_DILS), rows, LANES), jnp.float32),
                        pltpu.VMEM(wo_bf.shape, jnp.bfloat16)],
        compiler_params=pltpu.CompilerParams(
            dimension_semantics=("arbitrary",), vmem_limit_bytes=LARGE_VMEM_LIMIT),
        name="out_proj_router",
    )(oa, *outs_b, *lses_b, x2, wo_bf, g, wr, br)


def _mxu_dot(a_bf, w_f32):
    return lax.dot_general(a_bf, w_f32, (((1,), (0,)), ((), ())), preferred_element_type=jnp.float32)


def _moe_kernel(blk_exp_ref, first_ref, slot_ref, next_exp_ref, n_used_ref,
                x_ref, w1_hbm, b1_ref, w2_hbm, b2_ref, y_ref, w1_buf, w2_buf, sem):
    step = pl.program_id(0)

    def weight_copies(expert, slot):
        return (pltpu.make_async_copy(w1_hbm.at[expert], w1_buf.at[slot], sem.at[slot, 0]),
                pltpu.make_async_copy(w2_hbm.at[expert], w2_buf.at[slot], sem.at[slot, 1]))

    @pl.when(step * MOE_STEP_BLOCKS < n_used_ref[0])
    def _():
        @pl.when(step == 0)
        def _():
            for cp in weight_copies(blk_exp_ref[0], slot_ref[0]):
                cp.start()

        for j in range(MOE_STEP_BLOCKS):
            i = step * MOE_STEP_BLOCKS + j

            @pl.when(first_ref[i] == 1)
            def _():
                slot = slot_ref[i]
                for cp in weight_copies(blk_exp_ref[i], slot):
                    cp.wait()

                @pl.when(next_exp_ref[i] >= 0)
                def _():
                    for cp in weight_copies(next_exp_ref[i], (slot + 1) % MOE_WEIGHT_BUFS):
                        cp.start()

        for j in range(MOE_STEP_BLOCKS):
            i = step * MOE_STEP_BLOCKS + j
            slot = slot_ref[i]
            expert = blk_exp_ref[i]
            rs = slice(j * MOE_ROWS, (j + 1) * MOE_ROWS)
            x = jnp.concatenate(_unpack_halves(x_ref[rs, :]), axis=-1).astype(jnp.bfloat16)
            acc = jnp.zeros((MOE_ROWS, D_MODEL), jnp.float32)
            for c in range(D_FF // FF_CHUNK):
                lo = c * FF_CHUNK
                glu = _mxu_dot(x, w1_buf[slot, :, lo:lo + FF_CHUNK]) + b1_ref[expert, :, lo:lo + FF_CHUNK]
                lin = (_mxu_dot(x, w1_buf[slot, :, D_FF + lo:D_FF + lo + FF_CHUNK])
                       + b1_ref[expert, :, D_FF + lo:D_FF + lo + FF_CHUNK])
                glu = jnp.minimum(glu, SWIGLU_LIMIT)
                lin = jnp.clip(lin, -SWIGLU_LIMIT, SWIGLU_LIMIT)
                act = glu * (1.0 / (1.0 + jnp.exp(-SWIGLU_ALPHA * glu))) * (lin + 1.0)
                acc = acc + _mxu_dot(act.astype(jnp.bfloat16), w2_buf[slot, lo:lo + FF_CHUNK, :])
            y_ref[rs, :] = _pack_halves(acc + b2_ref[expert])


def _moe_plan(pends, n_blk):
    g = MOE_ROWS
    blk_row0 = jnp.arange(n_blk, dtype=jnp.int32) * g
    blk_exp = jnp.minimum(jnp.sum(pends[None, :] <= blk_row0[:, None], axis=-1),
                          N_EXPERTS - 1).astype(jnp.int32)
    n_used = (pends[-1] // g).astype(jnp.int32)
    used = blk_row0 < pends[-1]
    prev_exp = jnp.concatenate([jnp.full((1,), -1, jnp.int32), blk_exp[:-1]])
    first = (used & (blk_exp != prev_exp)).astype(jnp.int32)
    slot = ((jnp.cumsum(first) - 1) % MOE_WEIGHT_BUFS).astype(jnp.int32)
    pstarts = jnp.concatenate([jnp.zeros((1,), pends.dtype), pends[:-1]])
    nonempty = pends > pstarts
    experts = jnp.arange(N_EXPERTS, dtype=jnp.int32)
    later = nonempty[None, :] & (experts[None, :] > experts[:, None])
    next_nonempty = jnp.min(jnp.where(later, experts[None, :], N_EXPERTS), axis=-1)
    next_nonempty = jnp.where(next_nonempty == N_EXPERTS, -1, next_nonempty).astype(jnp.int32)
    next_exp = jnp.sum(jnp.where(blk_exp[:, None] == experts[None, :], next_nonempty[None, :], 0),
                       axis=-1).astype(jnp.int32)
    return blk_exp, first, slot, next_exp, n_used.reshape(1)


def _moe_experts(plan, xb, w1, b1, w2, b2):
    n_rows = xb.shape[0]
    n_blk = n_rows // MOE_ROWS

    step_rows = MOE_STEP_BLOCKS * MOE_ROWS

    def blk(i, *p):
        return jnp.minimum(i, (p[-1][0] - 1) // MOE_STEP_BLOCKS)

    grid_spec = pltpu.PrefetchScalarGridSpec(
        num_scalar_prefetch=len(plan),
        grid=(n_blk // MOE_STEP_BLOCKS,),
        in_specs=[
            pl.BlockSpec((step_rows, HALF_D), lambda i, *p: (blk(i, *p), 0)),
            pl.BlockSpec(memory_space=pl.ANY),
            pl.BlockSpec(b1.shape, lambda i, *p: (0, 0, 0)),
            pl.BlockSpec(memory_space=pl.ANY),
            pl.BlockSpec(b2.shape, lambda i, *p: (0, 0, 0)),
        ],
        out_specs=pl.BlockSpec((step_rows, HALF_D), lambda i, *p: (blk(i, *p), 0)),
        scratch_shapes=[pltpu.VMEM((MOE_WEIGHT_BUFS, D_MODEL, 2 * D_FF), jnp.float32),
                        pltpu.VMEM((MOE_WEIGHT_BUFS, D_FF, D_MODEL), jnp.float32),
                        pltpu.SemaphoreType.DMA((MOE_WEIGHT_BUFS, 2))],
    )
    return pl.pallas_call(
        _moe_kernel,
        out_shape=jax.ShapeDtypeStruct((n_rows, HALF_D), jnp.uint32),
        grid_spec=grid_spec,
        compiler_params=pltpu.CompilerParams(
            dimension_semantics=("arbitrary",), vmem_limit_bytes=LARGE_VMEM_LIMIT),
        name="moe_experts",
    )(*plan, xb, w1, b1, w2, b2)


def _sc_worker_id():
    return lax.axis_index("s") * SC_CORES + lax.axis_index("c")


def _sc_dispatch(hf, dest3, n_rows):
    n = hf.shape[0]
    chunks_per_worker = n // SC_CHUNK // SC_WORKERS
    mesh = plsc.VectorSubcoreMesh(core_axis_name="c", subcore_axis_name="s")

    @functools.partial(
        pl.kernel, mesh=mesh,
        out_type=jax.ShapeDtypeStruct((n_rows, HALF_D), hf.dtype),
        scratch_types=[pltpu.VMEM((TOP_K, chunks_per_worker, SC_CHUNK), jnp.int32),
                       pltpu.VMEM((2, SC_CHUNK, HALF_D), hf.dtype),
                       pltpu.SemaphoreType.DMA((2,)),
                       pltpu.SemaphoreType.DMA((2,))],
        name="sc_dispatch")
    def run(hf_hbm, dest_hbm, xb_hbm, idx_v, rows_v, load_sem, scatter_sem):
        first = _sc_worker_id() * chunks_per_worker
        pltpu.sync_copy(dest_hbm.at[:, pl.ds(first, chunks_per_worker)], idx_v)

        def load(j):
            slot = j % 2
            return pltpu.async_copy(hf_hbm.at[pl.ds((first + j) * SC_CHUNK, SC_CHUNK)],
                                    rows_v.at[slot], load_sem.at[slot])

        loads = {0: load(0)}
        scatters = {}
        for j in range(chunks_per_worker):
            slot = j % 2
            loads.pop(j).wait()
            scatters[j] = [pltpu.async_copy(rows_v.at[slot], xb_hbm.at[idx_v.at[k, j]],
                                            scatter_sem.at[slot]) for k in range(TOP_K)]
            if j >= 1:
                for cp in scatters.pop(j - 1):
                    cp.wait()
            if j + 1 < chunks_per_worker:
                loads[j + 1] = load(j + 1)
        for cp in scatters.pop(chunks_per_worker - 1):
            cp.wait()

    return run(hf, dest3)


def _sc_collect_sum(y, dest3, gates16, first_chunk, n_chunks):
    C = SC_COLLECT_CHUNK
    n = n_chunks * C
    chunks_per_worker = n_chunks // SC_WORKERS
    lanes = 16
    mesh = plsc.VectorSubcoreMesh(core_axis_name="c", subcore_axis_name="s")

    @functools.partial(
        pl.kernel, mesh=mesh,
        out_type=jax.ShapeDtypeStruct((n, HALF_D), y.dtype),
        scratch_types=[pltpu.VMEM((TOP_K, chunks_per_worker, C), jnp.int32),
                       pltpu.VMEM((TOP_K, chunks_per_worker, C), jnp.uint32),
                       pltpu.VMEM((2, TOP_K, C, HALF_D), y.dtype),
                       pltpu.VMEM((2, C, HALF_D), y.dtype),
                       pltpu.SemaphoreType.DMA((2, TOP_K)),
                       pltpu.SemaphoreType.DMA((2,))],
        compiler_params=pltpu.CompilerParams(needs_layout_passes=False),
        name="sc_collect_sum")
    def run(y_hbm, dest_hbm, gate_hbm, out_hbm, idx_v, gate_v, rows_v, sum_v, gather_sem, write_sem):
        local0 = _sc_worker_id() * chunks_per_worker

        pltpu.sync_copy(dest_hbm.at[:, pl.ds(first_chunk + local0, chunks_per_worker)], idx_v)
        pltpu.sync_copy(gate_hbm.at[:, pl.ds(first_chunk + local0, chunks_per_worker)], gate_v)

        def gather_copies(i, slot):
            return [pltpu.make_async_copy(y_hbm.at[idx_v.at[k, i]], rows_v.at[slot, k],
                                          gather_sem.at[slot, k]) for k in range(TOP_K)]

        def write_copy(i, slot):
            return pltpu.make_async_copy(sum_v.at[slot], out_hbm.at[pl.ds((local0 + i) * C, C)],
                                         write_sem.at[slot])

        def fetch(i, slot):
            for cp in gather_copies(i, slot):
                cp.start()

        def reduce_chunk(i, slot):
            for cp in gather_copies(i, slot):
                cp.wait()

            @pl.when(i >= 2)
            def _():
                write_copy(i - 2, slot).wait()

            @plsc.parallel_loop(0, C)
            def _(j):
                lane_j = jnp.full((lanes,), j, jnp.int32)
                g = [plsc.bitcast(gate_v[k, i, :].at[lane_j].get(mode="promise_in_bounds"),
                                  jnp.bfloat16) for k in range(TOP_K)]
                for c in range(HALF_D // lanes):
                    sl = pl.ds(c * lanes, lanes)
                    acc = None
                    for k in range(TOP_K):
                        term = plsc.bitcast(rows_v[slot, k, j, sl], jnp.bfloat16) * g[k]
                        acc = term if acc is None else acc + term
                    sum_v[slot, j, sl] = plsc.bitcast(acc, y.dtype)

            write_copy(i, slot).start()

        fetch(0, 0)

        @pl.loop(0, chunks_per_worker, step=2)
        def _(i):
            fetch(i + 1, 1)
            reduce_chunk(i, 0)

            @pl.when(i + 2 < chunks_per_worker)
            def _():
                fetch(i + 2, 0)

            reduce_chunk(i + 1, 1)

        write_copy(chunks_per_worker - 2, 0).wait()
        write_copy(chunks_per_worker - 1, 1).wait()

    return run(y, dest3, gates16)


def _combine_sum_kernel(x1_ref, ys_ref, o_ref):
    lo, hi = _unpack_halves(ys_ref[...])
    o_ref[:, :HALF_D] = x1_ref[:, :HALF_D] + lo
    o_ref[:, HALF_D:] = x1_ref[:, HALF_D:] + hi


def _combine_sum(acc, ysum, split):
    n = acc.shape[0]
    rows = PROJ_ROWS
    steps = ysum.shape[0] // rows
    first = split * steps
    return pl.pallas_call(
        _combine_sum_kernel,
        out_shape=jax.ShapeDtypeStruct((n, D_MODEL), jnp.float32),
        grid=(steps,),
        in_specs=[pl.BlockSpec((rows, D_MODEL), lambda i: (first + i, 0)),
                  pl.BlockSpec((rows, HALF_D), lambda i: (i, 0))],
        out_specs=pl.BlockSpec((rows, D_MODEL), lambda i: (first + i, 0)),
        input_output_aliases={0: 0},
        compiler_params=pltpu.CompilerParams(
            dimension_semantics=("arbitrary",), vmem_limit_bytes=VMEM_LIMIT),
        name="moe_combine",
    )(acc, ysum)


def _layer(x2, batch, seq, attn_norm_g, w_in, a_q_g, a_k_g, a_sinks, b_q_g, b_k_g, w_out,
           ffn_norm_g, w_router, b_router, w1, b1, w2, b2):
    n = x2.shape[0]
    slopes = _alibi_slopes()
    q_scale = HEAD_DIM ** -0.5 * LOG2E
    reps = MXU_DIM // HEAD_DIM
    gains = jnp.stack([jnp.tile(a_q_g, reps) * q_scale, jnp.tile(a_k_g, reps),
                       jnp.tile(b_q_g, reps) * q_scale, jnp.tile(b_k_g, reps)]).astype(jnp.float32)

    proj = _in_proj(x2, attn_norm_g.reshape(1, -1), w_in, gains, batch, seq)
    qa, ka, va = proj[:3]
    nb = len(B_DILS)
    qbs, kbs, vbs = proj[3:3 + nb], proj[3 + nb:3 + 2 * nb], proj[3 + 2 * nb:]

    bias_a = _bias_tables(slopes[:A_Q_HEADS], A_STACK_HEADS, A_HALF_WINDOW, 1, Q_TILE + 2 * A_HALF_WINDOW)
    sink_col = jnp.repeat(a_sinks.astype(jnp.float32) * LOG2E, Q_TILE).reshape(
        A_Q_HEADS // A_STACK_HEADS, A_STACK_HEADS * Q_TILE, 1)
    as_seqs = lambda a: a.reshape(batch, seq, a.shape[-1])
    out_a = _banded_attention(as_seqs(qa), as_seqs(ka), as_seqs(va), bias_a, half_w=A_HALF_WINDOW,
                              sink=sink_col, name="attn_a")[0].reshape(n, A_Q_W)

    outs_b, lses_b = [], []
    for bi, (window, dil) in enumerate(B_BRANCHES):
        half_w = window // (2 * dil)
        bias_b = _bias_tables(slopes[A_Q_HEADS:], B_STACK_HEADS, half_w, dil, Q_TILE + 2 * half_w)
        L = seq // dil
        to_seqs = lambda a: a.reshape(batch * dil, L, a.shape[-1])
        o, lse = _banded_attention(to_seqs(qbs[bi]), to_seqs(kbs[bi]), to_seqs(vbs[bi]), bias_b,
                                   half_w=half_w, want_lse=True, name=f"attn_b_d{dil}")
        if dil == 1:
            outs_b.append(o.reshape(n, B_W))
            lses_b.append(lse.reshape(n, LANES))
        else:
            outs_b.append(o.reshape(batch, dil, L, B_W))
            lses_b.append(lse.reshape(batch, dil, L, LANES))

    wr_t = w_router.T.astype(jnp.float32)
    wr_hi = wr_t.astype(jnp.bfloat16)
    wr_lo = (wr_t - wr_hi.astype(jnp.float32)).astype(jnp.bfloat16)
    wr = jnp.concatenate([wr_hi, wr_lo], axis=0)
    br = jnp.broadcast_to(b_router.astype(jnp.float32)[:, None], (N_EXPERTS, LANES))
    x1, hf, topi, gates, ranks, counts = _out_proj_router(
        out_a, outs_b, lses_b, x2, w_out, ffn_norm_g.reshape(1, -1), wr, br, seq)

    g = MOE_ROWS
    nk = n * TOP_K
    step_rows = MOE_STEP_BLOCKS * g
    n_rows = -(-(nk + N_EXPERTS * g) // step_rows) * step_rows
    cnt = counts[:, 0]
    pcnt = (cnt + g - 1) // g * g
    pends = jnp.cumsum(pcnt)
    pstarts = pends - pcnt
    experts = jnp.arange(N_EXPERTS, dtype=jnp.int32)
    start_of = jnp.sum(jnp.where(topi[:, :, None] == experts, pstarts, 0), axis=-1)
    dest = (start_of + ranks).astype(jnp.int32)
    plan = _moe_plan(pends, n_rows // g)
    dest3 = dest.reshape(TOP_K, n // SC_CHUNK, SC_CHUNK)

    xb = _sc_dispatch(hf, dest3, n_rows)
    y = _moe_experts(plan, xb, w1, b1[:, None, :], w2, b2[:, None, :])
    smallest_range = max(PROJ_ROWS, SC_COLLECT_CHUNK * SC_WORKERS)
    n_splits = max(1, min(COMBINE_SPLITS, n // smallest_range))
    per_split = n // n_splits
    chunked = lambda a: a.reshape(TOP_K, n // SC_COLLECT_CHUNK, SC_COLLECT_CHUNK)
    dest_c = chunked(dest)
    gate_bits = lax.bitcast_convert_type(gates.astype(jnp.bfloat16), jnp.uint16).astype(jnp.uint32)
    gate_words = gate_bits | (gate_bits << 16)
    gates_c = chunked(gate_words)
    chunks_per_split = per_split // SC_COLLECT_CHUNK
    out = x1
    for s in range(n_splits):
        ysum = _sc_collect_sum(y, dest_c, gates_c, s * chunks_per_split, chunks_per_split)
        out = _combine_sum(out, ysum, s)
    return out


def kernel(x, attn_norm_g, w_in, a_q_norm_g, a_k_norm_g, a_sinks, b_q_norm_g, b_k_norm_g, w_out,
           ffn_norm_g, w_router, b_router, w1, b1, w2, b2):
    batch, seq, d = x.shape
    x2 = x.reshape(batch * seq, d)
    for i in range(attn_norm_g.shape[0]):
        x2 = _layer(x2, batch, seq, attn_norm_g[i], w_in[i], a_q_norm_g[i], a_k_norm_g[i],
                    a_sinks[i], b_q_norm_g[i], b_k_norm_g[i], w_out[i], ffn_norm_g[i],
                    w_router[i], b_router[i], w1[i], b1[i], w2[i], b2[i])
    return x2.reshape(batch, seq, d)
```

```python
import functools

import jax
import jax.numpy as jnp
import numpy as np
from jax import lax
from jax.experimental import pallas as pl
from jax.experimental.pallas import tpu as pltpu
from jax.experimental.pallas import tpu_sc as plsc

D_MODEL = 1024
HALF_D = D_MODEL // 2
HEAD_DIM = 64
LANES = 128
MXU_DIM = 256
A_Q_HEADS = 8
A_KV_HEADS = 2
B_HEADS = 8
A_HALF_WINDOW = 128
B_BRANCHES = ((128, 1), (512, 4), (2048, 16))
B_DILS = tuple(d for _, d in B_BRANCHES)
RESIDUE_STRIDE = 4
N_ALIBI_HEADS = 16
A_Q_W = A_Q_HEADS * HEAD_DIM
A_KV_W = A_KV_HEADS * HEAD_DIM
B_W = B_HEADS * HEAD_DIM
N_EXPERTS = 32
TOP_K = 4
D_FF = 1024
SWIGLU_ALPHA = 1.702
SWIGLU_LIMIT = 7.0
NORM_EPS = 1e-5
MASK_VALUE = -1e30
LOG2E = 1.4426950408889634

Q_TILE = 128
ATTN_STEP_ROWS = 2048
A_STACK_HEADS = 4
B_STACK_HEADS = 2
PROJ_ROWS = 1024
IN_PROJ_ROWS = 1024
MOE_ROWS = 512
FF_CHUNK = 512
MOE_STEP_BLOCKS = 2
MOE_WEIGHT_BUFS = 3
VMEM_LIMIT = 48 * 1024 * 1024
LARGE_VMEM_LIMIT = 58 * 1024 * 1024
SC_CORES = 2
SC_SUBCORES = 16
SC_WORKERS = SC_CORES * SC_SUBCORES
SC_CHUNK = 64
SC_COLLECT_CHUNK = 16
COMBINE_SPLITS = 2


def _pack_halves(v):
    lo = v[:, :HALF_D].astype(jnp.bfloat16).astype(jnp.float32)
    hi = v[:, HALF_D:].astype(jnp.bfloat16).astype(jnp.float32)
    return (pltpu.bitcast(lo, jnp.uint32) >> 16) | pltpu.bitcast(hi, jnp.uint32)


def _unpack_halves(w):
    lo = pltpu.bitcast(w << 16, jnp.float32)
    hi = pltpu.bitcast(w & jnp.uint32(0xFFFF0000), jnp.float32)
    return lo, hi


def _alibi_slopes():
    return np.exp2(-8.0 * np.arange(1, N_ALIBI_HEADS + 1, dtype=np.float32) / N_ALIBI_HEADS).astype(np.float32)


def _bias_tables(head_slopes, heads_per_group, half_w, dist_scale, tk):
    i = np.arange(Q_TILE)[:, None]
    j = np.arange(tk)[None, :]
    tabs = []
    for shift in (0, half_w, tk - Q_TILE):
        dist = np.abs(j - shift - i)
        valid = dist <= half_w
        per_head = []
        for sl in head_slopes:
            b = (-np.float64(sl) * LOG2E * (dist * dist_scale)).astype(np.float32)
            per_head.append(np.where(valid, b, np.float32(MASK_VALUE)).astype(np.float32))
        t = np.stack(per_head).reshape(-1, heads_per_group * Q_TILE, tk)
        tabs.append(t)
    return jnp.asarray(np.stack(tabs))


def _in_proj_kernel(x_ref, g_ref, w_hbm, gains_ref, qa_ref, ka_ref, va_ref, *rest):
    b_refs, (scr_ref, scr2_ref, w_ref, stage_ref, sem) = rest[:-5], rest[-5:]

    @pl.when(pl.program_id(0) == 0)
    def _():
        width = stage_ref.shape[1]
        for c0 in range(0, w_ref.shape[1], width):
            cp = pltpu.make_async_copy(w_hbm.at[:, c0:c0 + width], stage_ref, sem)
            cp.start()
            cp.wait()
            w_ref[:, c0:c0 + width] = stage_ref[...].astype(jnp.bfloat16)

    x = x_ref[...]
    xn = x * lax.rsqrt(jnp.mean(x * x, axis=-1, keepdims=True) + NORM_EPS) * g_ref[...]
    xn = xn.astype(jnp.bfloat16)
    r = lax.broadcasted_iota(jnp.int32, (MXU_DIM, MXU_DIM), 0) // HEAD_DIM
    c = lax.broadcasted_iota(jnp.int32, (MXU_DIM, MXU_DIM), 1) // HEAD_DIM
    blockdiag = jnp.where(r == c, 1.0, 0.0).astype(jnp.bfloat16)

    def head_rms(sec, gain_row):
        width = sec.shape[1]
        parts = []
        step = min(width, MXU_DIM)
        for j in range(width // step):
            p = sec[:, j * step:(j + 1) * step]
            ss = jnp.dot((p * p).astype(jnp.bfloat16), blockdiag[:step, :step],
                         preferred_element_type=jnp.float32)
            parts.append(p * lax.rsqrt(ss * (1.0 / HEAD_DIM) + NORM_EPS)
                         * gains_ref[gain_row:gain_row + 1, :step])
        return parts

    def project(col0, width, gain_row):
        sec = jnp.dot(xn, w_ref[:, col0:col0 + width], preferred_element_type=jnp.float32)
        return [sec] if gain_row is None else head_rms(sec, gain_row)

    def store(out_ref, parts):
        w = parts[0].shape[1]
        for j, p in enumerate(parts):
            out_ref[:, j * w:(j + 1) * w] = p.astype(out_ref.dtype)

    def per_kv_head(p):
        lane = lax.broadcasted_iota(jnp.int32, p.shape, 1)
        swapped = pltpu.roll(p, HEAD_DIM, axis=1)
        low = lane < HEAD_DIM
        return [jnp.where(low, p, swapped), jnp.where(low, swapped, p)]

    rows = x_ref.shape[0]
    col0 = A_Q_W + 2 * A_KV_W
    for t, gain_row in enumerate((2, 3, None)):
        parts = project(col0 + t * B_W, B_W, gain_row)
        sec = jnp.concatenate(parts, axis=-1) if len(parts) > 1 else parts[0]
        for j in range(B_W // LANES):
            scr_ref[j] = sec[:, j * LANES:(j + 1) * LANES]
        n_lane_chunks = B_W // LANES
        prev_dil = 1
        for bi, dil in enumerate(B_DILS):
            out_ref = b_refs[t * len(B_DILS) + bi]
            if dil == 1:
                out_ref[...] = sec.astype(out_ref.dtype)
                continue
            assert dil == prev_dil * RESIDUE_STRIDE
            last = dil == B_DILS[-1]
            for res in range(dil):
                r_prev, r_sub = res % prev_dil, res // prev_dil
                for j in range(n_lane_chunks):
                    if prev_dil == 1:
                        v = scr_ref[j, pl.ds(r_sub, rows // dil, stride=RESIDUE_STRIDE), :]
                    else:
                        v = scr2_ref[r_prev * n_lane_chunks + j,
                                     pl.ds(r_sub, rows // dil, stride=RESIDUE_STRIDE), :]
                    out_ref[0, res, :, j * LANES:(j + 1) * LANES] = v.astype(out_ref.dtype)
                    if not last:
                        scr2_ref[res * n_lane_chunks + j] = v
            prev_dil = dil

    store(qa_ref, project(0, A_Q_W, 0))
    kva = project(A_Q_W, 2 * A_KV_W, None)[0]
    store(ka_ref, per_kv_head(head_rms(kva[:, :A_KV_W], 1)[0]))
    store(va_ref, per_kv_head(kva[:, A_KV_W:]))


def _in_proj(x2, g, w_in, gains, batch, seq):
    n = x2.shape[0]
    rows = IN_PROJ_ROWS
    steps = seq // rows
    a_widths = (A_Q_W, 2 * A_KV_W, 2 * A_KV_W)
    out_shape = [jax.ShapeDtypeStruct((n, w), jnp.bfloat16) for w in a_widths]
    out_specs = [pl.BlockSpec((rows, w), lambda i: (i, 0)) for w in a_widths]
    for _ in range(3):
        for dil in B_DILS:
            if dil == 1:
                out_shape.append(jax.ShapeDtypeStruct((n, B_W), jnp.bfloat16))
                out_specs.append(pl.BlockSpec((rows, B_W), lambda i: (i, 0)))
            else:
                out_shape.append(jax.ShapeDtypeStruct((batch, dil, seq // dil, B_W), jnp.bfloat16))
                out_specs.append(pl.BlockSpec((1, dil, rows // dil, B_W),
                                              lambda i: (i // steps, 0, i % steps, 0)))
    return pl.pallas_call(
        _in_proj_kernel,
        out_shape=out_shape,
        grid=(n // rows,),
        in_specs=[
            pl.BlockSpec((rows, D_MODEL), lambda i: (i, 0)),
            pl.BlockSpec((1, D_MODEL), lambda i: (0, 0)),
            pl.BlockSpec(memory_space=pl.ANY),
            pl.BlockSpec(gains.shape, lambda i: (0, 0)),
        ],
        out_specs=out_specs,
        scratch_shapes=[pltpu.VMEM((B_W // LANES, rows, LANES), jnp.float32),
                        pltpu.VMEM((RESIDUE_STRIDE * B_W // LANES, rows // RESIDUE_STRIDE, LANES),
                                   jnp.float32),
                        pltpu.VMEM(w_in.shape, jnp.bfloat16),
                        pltpu.VMEM((w_in.shape[0], w_in.shape[1] // 3), jnp.float32),
                        pltpu.SemaphoreType.DMA],
        compiler_params=pltpu.CompilerParams(
            dimension_semantics=("arbitrary",), vmem_limit_bytes=LARGE_VMEM_LIMIT),
        name="in_proj",
    )(x2, g, w_in, gains)


def _attn_kernel(*refs, n_chunks, kv_chunks, heads_per_stack, tk, half_w, seq_len, rows, has_sink,
                 want_lse):
    it = iter(refs)
    q_ref, k_ref, v_ref, bias_ref = next(it), next(it), next(it), next(it)
    sink_ref = next(it) if has_sink else None
    o_ref = next(it)
    lse_ref = next(it) if want_lse else None

    n_tiles = seq_len // Q_TILE
    tiles_per_step = rows // Q_TILE
    chunks_per_group = n_chunks // kv_chunks
    assert (2 * chunks_per_group) % heads_per_stack == 0
    step = pl.program_id(1)
    lane = lax.broadcasted_iota(jnp.int32, (Q_TILE, LANES), 1)
    low_half = lane < HEAD_DIM
    ones = jnp.ones((tk, LANES), jnp.bfloat16)

    for sq, t in [(a, b) for a in range(q_ref.shape[0]) for b in range(tiles_per_step)]:
        tile = step * tiles_per_step + t
        q0 = tile * Q_TILE
        kv_rows = k_ref.shape[1]
        kv_row0 = jnp.clip(step * rows - half_w, 0, seq_len - kv_rows)
        start = pl.multiple_of(jnp.clip(q0 - half_w, 0, seq_len - tk) - kv_row0, HEAD_DIM)
        variant = jnp.where(tile == 0, 0, jnp.where(tile == n_tiles - 1, 2, 1))
        r0 = t * Q_TILE
        lse_tile = jnp.zeros((Q_TILE, LANES), jnp.float32)
        head_o, head_lse = {}, {}
        for g in range(2 * n_chunks // heads_per_stack):
            heads = range(g * heads_per_stack, (g + 1) * heads_per_stack)
            kv = (heads[0] // 2) // chunks_per_group
            kc = k_ref[sq, pl.ds(start, tk), kv * LANES:(kv + 1) * LANES]
            vc = v_ref[sq, pl.ds(start, tk), kv * LANES:(kv + 1) * LANES]
            v_aug = jnp.concatenate([vc, ones], axis=1)
            q_parts = []
            for h in heads:
                c = h // 2
                q2 = q_ref[sq, r0:r0 + Q_TILE, c * LANES:(c + 1) * LANES]
                keep = low_half if h % 2 == 0 else ~low_half
                q_parts.append(jnp.where(keep, q2, jnp.zeros_like(q2)))
            qs = q_parts[0] if len(q_parts) == 1 else jnp.concatenate(q_parts, axis=0)
            s = lax.dot_general(qs, kc, (((1,), (1,)), ((), ())),
                                preferred_element_type=jnp.float32)
            s = s + bias_ref[variant, g]
            m = jnp.max(s, axis=-1, keepdims=True)
            if has_sink:
                sink = sink_ref[g]
                m = jnp.maximum(jnp.broadcast_to(m, sink.shape), sink)
                p = jnp.exp2(s - jnp.concatenate([m] * (tk // LANES), axis=1))
            else:
                p = jnp.exp2(s - m)
            ov = jnp.dot(p.astype(jnp.bfloat16), v_aug, preferred_element_type=jnp.float32)
            o, l = ov[:, :LANES], ov[:, LANES:]
            if has_sink:
                l = l + jnp.exp2(sink - m)
            o = o * (1.0 / l)
            if want_lse:
                lse = m + jnp.log(l) * LOG2E
            for idx, h in enumerate(heads):
                head_o[h] = o[idx * Q_TILE:(idx + 1) * Q_TILE]
                if want_lse:
                    head_lse[h] = lse[idx * Q_TILE:(idx + 1) * Q_TILE]
                if h % 2 == 1:
                    c = h // 2
                    o2 = jnp.where(low_half, head_o.pop(h - 1), head_o.pop(h))
                    o_ref[sq, r0:r0 + Q_TILE, c * LANES:(c + 1) * LANES] = o2.astype(o_ref.dtype)
                    if want_lse:
                        lse_tile = jnp.where(lane == h - 1, head_lse.pop(h - 1),
                                             jnp.where(lane == h, head_lse.pop(h), lse_tile))
        if want_lse:
            lse_ref[sq, r0:r0 + Q_TILE, :] = lse_tile


def _banded_attention(q, k, v, bias, *, half_w, sink=None, want_lse=False, name):
    n_seq, L, qw = q.shape
    kw = k.shape[2]
    tk = Q_TILE + 2 * half_w
    rows = min(ATTN_STEP_ROWS, L)
    seqs = ATTN_STEP_ROWS // rows
    kv_rows = min(L, rows + 2 * half_w)

    def kv_index(s, i):
        row0 = pl.multiple_of(jnp.clip(i * rows - half_w, 0, L - kv_rows), HEAD_DIM)
        return s * seqs, row0, 0

    args = [q, k, v, bias]
    in_specs = [
        pl.BlockSpec((seqs, rows, qw), lambda s, i: (s, i, 0)),
        pl.BlockSpec((pl.Element(seqs), pl.Element(kv_rows), pl.Element(kw)), kv_index),
        pl.BlockSpec((pl.Element(seqs), pl.Element(kv_rows), pl.Element(kw)), kv_index),
        pl.BlockSpec(bias.shape, lambda s, i: (0, 0, 0, 0)),
    ]
    if sink is not None:
        args.append(sink)
        in_specs.append(pl.BlockSpec(sink.shape, lambda s, i: (0, 0, 0)))
    out_shape = [jax.ShapeDtypeStruct((n_seq, L, qw), jnp.bfloat16)]
    out_specs = [pl.BlockSpec((seqs, rows, qw), lambda s, i: (s, i, 0))]
    if want_lse:
        out_shape.append(jax.ShapeDtypeStruct((n_seq, L, LANES), jnp.float32))
        out_specs.append(pl.BlockSpec((seqs, rows, LANES), lambda s, i: (s, i, 0)))

    kern = functools.partial(
        _attn_kernel, n_chunks=qw // LANES, kv_chunks=kw // LANES,
        heads_per_stack=bias.shape[2] // Q_TILE, tk=tk, half_w=half_w, seq_len=L,
        rows=rows, has_sink=sink is not None, want_lse=want_lse)
    return pl.pallas_call(
        kern,
        out_shape=out_shape,
        grid=(n_seq // seqs, L // rows),
        in_specs=in_specs,
        out_specs=out_specs,
        compiler_params=pltpu.CompilerParams(
            dimension_semantics=("arbitrary", "arbitrary"), vmem_limit_bytes=VMEM_LIMIT),
        name=name,
    )(*args)


def _out_proj_router_kernel(*refs):
    nb = len(B_DILS)
    oa_ref = refs[0]
    o_refs = refs[1:1 + nb]
    lse_refs = refs[1 + nb:1 + 2 * nb]
    (x_ref, wo_ref, g_ref, wr_ref, br_ref,
     x1_ref, hf_ref, topi_ref, gate_ref, rank_ref, cnt_ref,
     tri_ref, carry_ref, so_ref, sl_ref, wo_bf_ref) = refs[1 + 2 * nb:]
    i = pl.program_id(0)
    rows = x_ref.shape[0]

    @pl.when(i == 0)
    def _():
        a = lax.broadcasted_iota(jnp.int32, (rows, rows), 0)
        b = lax.broadcasted_iota(jnp.int32, (rows, rows), 1)
        tri_ref[...] = jnp.where(a <= b, 1.0, 0.0).astype(jnp.bfloat16)
        carry_ref[...] = jnp.zeros_like(carry_ref)
        wo_bf_ref[...] = wo_ref[...].astype(jnp.bfloat16)

    outs, lses = [], []
    for bi, dil in enumerate(B_DILS):
        if dil == 1:
            outs.append(o_refs[bi][...].astype(jnp.float32))
            lses.append(lse_refs[bi][...])
        else:
            for res in range(dil):
                for j in range(B_W // LANES):
                    so_ref[bi, j, pl.ds(res, rows // dil, stride=dil), :] = (
                        o_refs[bi][0, res, :, j * LANES:(j + 1) * LANES].astype(jnp.float32))
                sl_ref[bi, pl.ds(res, rows // dil, stride=dil), :] = lse_refs[bi][0, res]
            outs.append(jnp.concatenate([so_ref[bi, j] for j in range(B_W // LANES)], axis=-1))
            lses.append(sl_ref[bi])

    mx = functools.reduce(jnp.maximum, lses)
    es = [jnp.exp2(l - mx) for l in lses]
    inv = 1.0 / functools.reduce(lambda a, b: a + b, es)
    eh = lax.broadcasted_iota(jnp.int32, (LANES, B_W), 0)
    ej = lax.broadcasted_iota(jnp.int32, (LANES, B_W), 1) // HEAD_DIM
    expand = jnp.where(eh == ej, 1.0, 0.0).astype(jnp.bfloat16)
    ob = jnp.zeros((rows, B_W), jnp.float32)
    for e, o in zip(es, outs):
        w = e * inv
        wide = jnp.dot(w.astype(jnp.bfloat16), expand, preferred_element_type=jnp.float32)
        ob = ob + wide * o

    attn = jnp.concatenate([oa_ref[...], ob.astype(jnp.bfloat16)], axis=-1)
    x1 = x_ref[...] + jnp.dot(attn, wo_bf_ref[...], preferred_element_type=jnp.float32)
    x1_ref[...] = x1
    hf = x1 * lax.rsqrt(jnp.mean(x1 * x1, axis=-1, keepdims=True) + NORM_EPS) * g_ref[...]
    hf_hi = hf.astype(jnp.bfloat16)
    hf_ref[...] = _pack_halves(hf)
    hf_lo = (hf - hf_hi.astype(jnp.float32)).astype(jnp.bfloat16)

    nt = (((1,), (1,)), ((), ()))
    lg_hi = lax.dot_general(wr_ref[...], hf_hi, nt, preferred_element_type=jnp.float32)
    lg_lo = lax.dot_general(wr_ref[0:N_EXPERTS, :], hf_lo, nt, preferred_element_type=jnp.float32)
    logits = lg_hi[0:N_EXPERTS] + lg_hi[N_EXPERTS:] + lg_lo + br_ref[:, 0:1]


    eidx = lax.broadcasted_iota(jnp.int32, (N_EXPERTS, rows), 0)
    work = logits
    vals, sels = [], []
    for k in range(TOP_K):
        mk = jnp.max(work, axis=0, keepdims=True)
        ik = jnp.min(jnp.where(work == mk, eidx, N_EXPERTS), axis=0, keepdims=True)
        sel = eidx == ik
        work = jnp.where(sel, -jnp.inf, work)
        vals.append(mk)
        sels.append(sel)
        topi_ref[k:k + 1, :] = ik
    exps = [jnp.exp(vk - vals[0]) for vk in vals]
    denom = exps[0] + exps[1] + exps[2] + exps[3]
    ginv = 1.0 / denom
    for k in range(TOP_K):
        gate_ref[k:k + 1, :] = exps[k] * ginv

    onehot = jnp.zeros((N_EXPERTS, rows), jnp.float32)
    for sel in sels:
        onehot = onehot + jnp.where(sel, 1.0, 0.0)
    incl = jnp.dot(onehot.astype(jnp.bfloat16), tri_ref[...], preferred_element_type=jnp.float32)
    before = incl - onehot + carry_ref[:, 0:1]
    for k in range(TOP_K):
        rk = jnp.sum(jnp.where(sels[k], before, 0.0), axis=0, keepdims=True)
        rank_ref[k:k + 1, :] = rk.astype(jnp.int32)
    carry = carry_ref[...] + jnp.sum(onehot, axis=1, keepdims=True)
    carry_ref[...] = carry
    cnt_ref[...] = carry.astype(jnp.int32)


def _out_proj_router(oa, outs_b, lses_b, x2, wo_bf, g, wr, br, seq):
    n = x2.shape[0]
    rows = PROJ_ROWS
    steps = seq // rows
    row_spec = lambda w: pl.BlockSpec((rows, w), lambda i: (i, 0))
    full = lambda a: pl.BlockSpec(a.shape, lambda i: (0,) * a.ndim)
    col_spec = pl.BlockSpec((TOP_K, rows), lambda i: (0, i))

    def branch_spec(dil, w):
        if dil == 1:
            return row_spec(w)
        return pl.BlockSpec((1, dil, rows // dil, w), lambda i: (i // steps, 0, i % steps, 0))

    in_specs = ([row_spec(A_Q_W)]
                + [branch_spec(d, B_W) for d in B_DILS]
                + [branch_spec(d, LANES) for d in B_DILS]
                + [row_spec(D_MODEL), full(wo_bf), full(g), full(wr), full(br)])
    return pl.pallas_call(
        _out_proj_router_kernel,
        out_shape=[
            jax.ShapeDtypeStruct((n, D_MODEL), jnp.float32),
            jax.ShapeDtypeStruct((n, HALF_D), jnp.uint32),
            jax.ShapeDtypeStruct((TOP_K, n), jnp.int32),
            jax.ShapeDtypeStruct((TOP_K, n), jnp.float32),
            jax.ShapeDtypeStruct((TOP_K, n), jnp.int32),
            jax.ShapeDtypeStruct((N_EXPERTS, LANES), jnp.int32),
        ],
        grid=(n // rows,),
        in_specs=in_specs,
        out_specs=[row_spec(D_MODEL), row_spec(HALF_D), col_spec, col_spec, col_spec,
                   pl.BlockSpec((N_EXPERTS, LANES), lambda i: (0, 0))],
        scratch_shapes=[pltpu.VMEM((rows, rows), jnp.bfloat16),
                        pltpu.VMEM((N_EXPERTS, LANES), jnp.float32),
                        pltpu.VMEM((len(B_DILS), B_W // LANES, rows, LANES), jnp.float32),
                        pltpu.VMEM((len(B_DILS), rows, LANES), jnp.float32),
                        pltpu.VMEM(wo_bf.shape, jnp.bfloat16)],
        compiler_params=pltpu.CompilerParams(
            dimension_semantics=("arbitrary",), vmem_limit_bytes=LARGE_VMEM_LIMIT),
        name="out_proj_router",
    )(oa, *outs_b, *lses_b, x2, wo_bf, g, wr, br)


def _mxu_dot(a_bf, w_f32):
    return lax.dot_general(a_bf, w_f32, (((1,), (0,)), ((), ())), preferred_element_type=jnp.float32)


def _moe_kernel(blk_exp_ref, first_ref, slot_ref, next_exp_ref, n_used_ref,
                x_ref, w1_hbm, b1_ref, w2_hbm, b2_ref, y_ref, w1_buf, w2_buf, sem):
    step = pl.program_id(0)

    def weight_copies(expert, slot):
        return (pltpu.make_async_copy(w1_hbm.at[expert], w1_buf.at[slot], sem.at[slot, 0]),
                pltpu.make_async_copy(w2_hbm.at[expert], w2_buf.at[slot], sem.at[slot, 1]))

    @pl.when(step * MOE_STEP_BLOCKS < n_used_ref[0])
    def _():
        @pl.when(step == 0)
        def _():
            for cp in weight_copies(blk_exp_ref[0], slot_ref[0]):
                cp.start()

        for j in range(MOE_STEP_BLOCKS):
            i = step * MOE_STEP_BLOCKS + j

            @pl.when(first_ref[i] == 1)
            def _():
                slot = slot_ref[i]
                for cp in weight_copies(blk_exp_ref[i], slot):
                    cp.wait()

                @pl.when(next_exp_ref[i] >= 0)
                def _():
                    for cp in weight_copies(next_exp_ref[i], (slot + 1) % MOE_WEIGHT_BUFS):
                        cp.start()

        for j in range(MOE_STEP_BLOCKS):
            i = step * MOE_STEP_BLOCKS + j
            slot = slot_ref[i]
            expert = blk_exp_ref[i]
            rs = slice(j * MOE_ROWS, (j + 1) * MOE_ROWS)
            x = jnp.concatenate(_unpack_halves(x_ref[rs, :]), axis=-1).astype(jnp.bfloat16)
            acc = jnp.zeros((MOE_ROWS, D_MODEL), jnp.float32)
            for c in range(D_FF // FF_CHUNK):
                lo = c * FF_CHUNK
                glu = _mxu_dot(x, w1_buf[slot, :, lo:lo + FF_CHUNK]) + b1_ref[expert, :, lo:lo + FF_CHUNK]
                lin = (_mxu_dot(x, w1_buf[slot, :, D_FF + lo:D_FF + lo + FF_CHUNK])
                       + b1_ref[expert, :, D_FF + lo:D_FF + lo + FF_CHUNK])
                glu = jnp.minimum(glu, SWIGLU_LIMIT)
                lin = jnp.clip(lin, -SWIGLU_LIMIT, SWIGLU_LIMIT)
                act = glu * (1.0 / (1.0 + jnp.exp(-SWIGLU_ALPHA * glu))) * (lin + 1.0)
                acc = acc + _mxu_dot(act.astype(jnp.bfloat16), w2_buf[slot, lo:lo + FF_CHUNK, :])
            y_ref[rs, :] = _pack_halves(acc + b2_ref[expert])


def _moe_plan(pends, n_blk):
    g = MOE_ROWS
    blk_row0 = jnp.arange(n_blk, dtype=jnp.int32) * g
    blk_exp = jnp.minimum(jnp.sum(pends[None, :] <= blk_row0[:, None], axis=-1),
                          N_EXPERTS - 1).astype(jnp.int32)
    n_used = (pends[-1] // g).astype(jnp.int32)
    used = blk_row0 < pends[-1]
    prev_exp = jnp.concatenate([jnp.full((1,), -1, jnp.int32), blk_exp[:-1]])
    first = (used & (blk_exp != prev_exp)).astype(jnp.int32)
    slot = ((jnp.cumsum(first) - 1) % MOE_WEIGHT_BUFS).astype(jnp.int32)
    pstarts = jnp.concatenate([jnp.zeros((1,), pends.dtype), pends[:-1]])
    nonempty = pends > pstarts
    experts = jnp.arange(N_EXPERTS, dtype=jnp.int32)
    later = nonempty[None, :] & (experts[None, :] > experts[:, None])
    next_nonempty = jnp.min(jnp.where(later, experts[None, :], N_EXPERTS), axis=-1)
    next_nonempty = jnp.where(next_nonempty == N_EXPERTS, -1, next_nonempty).astype(jnp.int32)
    next_exp = jnp.sum(jnp.where(blk_exp[:, None] == experts[None, :], next_nonempty[None, :], 0),
                       axis=-1).astype(jnp.int32)
    return blk_exp, first, slot, next_exp, n_used.reshape(1)


def _moe_experts(plan, xb, w1, b1, w2, b2):
    n_rows = xb.shape[0]
    n_blk = n_rows // MOE_ROWS

    step_rows = MOE_STEP_BLOCKS * MOE_ROWS

    def blk(i, *p):
        return jnp.minimum(i, (p[-1][0] - 1) // MOE_STEP_BLOCKS)

    grid_spec = pltpu.PrefetchScalarGridSpec(
        num_scalar_prefetch=len(plan),
        grid=(n_blk // MOE_STEP_BLOCKS,),
        in_specs=[
            pl.BlockSpec((step_rows, HALF_D), lambda i, *p: (blk(i, *p), 0)),
            pl.BlockSpec(memory_space=pl.ANY),
            pl.BlockSpec(b1.shape, lambda i, *p: (0, 0, 0)),
            pl.BlockSpec(memory_space=pl.ANY),
            pl.BlockSpec(b2.shape, lambda i, *p: (0, 0, 0)),
        ],
        out_specs=pl.BlockSpec((step_rows, HALF_D), lambda i, *p: (blk(i, *p), 0)),
        scratch_shapes=[pltpu.VMEM((MOE_WEIGHT_BUFS, D_MODEL, 2 * D_FF), jnp.float32),
                        pltpu.VMEM((MOE_WEIGHT_BUFS, D_FF, D_MODEL), jnp.float32),
                        pltpu.SemaphoreType.DMA((MOE_WEIGHT_BUFS, 2))],
    )
    return pl.pallas_call(
        _moe_kernel,
        out_shape=jax.ShapeDtypeStruct((n_rows, HALF_D), jnp.uint32),
        grid_spec=grid_spec,
        compiler_params=pltpu.CompilerParams(
            dimension_semantics=("arbitrary",), vmem_limit_bytes=LARGE_VMEM_LIMIT),
        name="moe_experts",
    )(*plan, xb, w1, b1, w2, b2)


def _sc_worker_id():
    return lax.axis_index("s") * SC_CORES + lax.axis_index("c")


def _sc_dispatch(hf, dest3, n_rows):
    n = hf.shape[0]
    chunks_per_worker = n // SC_CHUNK // SC_WORKERS
    mesh = plsc.VectorSubcoreMesh(core_axis_name="c", subcore_axis_name="s")

    @functools.partial(
        pl.kernel, mesh=mesh,
        out_type=jax.ShapeDtypeStruct((n_rows, HALF_D), hf.dtype),
        scratch_types=[pltpu.VMEM((TOP_K, chunks_per_worker, SC_CHUNK), jnp.int32),
                       pltpu.VMEM((2, SC_CHUNK, HALF_D), hf.dtype),
                       pltpu.SemaphoreType.DMA((2,)),
                       pltpu.SemaphoreType.DMA((2,))],
        name="sc_dispatch")
    def run(hf_hbm, dest_hbm, xb_hbm, idx_v, rows_v, load_sem, scatter_sem):
        first = _sc_worker_id() * chunks_per_worker
        pltpu.sync_copy(dest_hbm.at[:, pl.ds(first, chunks_per_worker)], idx_v)

        def load(j):
            slot = j % 2
            return pltpu.async_copy(hf_hbm.at[pl.ds((first + j) * SC_CHUNK, SC_CHUNK)],
                                    rows_v.at[slot], load_sem.at[slot])

        loads = {0: load(0)}
        scatters = {}
        for j in range(chunks_per_worker):
            slot = j % 2
            loads.pop(j).wait()
            scatters[j] = [pltpu.async_copy(rows_v.at[slot], xb_hbm.at[idx_v.at[k, j]],
                                            scatter_sem.at[slot]) for k in range(TOP_K)]
            if j >= 1:
                for cp in scatters.pop(j - 1):
                    cp.wait()
            if j + 1 < chunks_per_worker:
                loads[j + 1] = load(j + 1)
        for cp in scatters.pop(chunks_per_worker - 1):
            cp.wait()

    return run(hf, dest3)


def _sc_collect_sum(y, dest3, gates16, first_chunk, n_chunks):
    C = SC_COLLECT_CHUNK
    n = n_chunks * C
    chunks_per_worker = n_chunks // SC_WORKERS
    lanes = 16
    mesh = plsc.VectorSubcoreMesh(core_axis_name="c", subcore_axis_name="s")

    @functools.partial(
        pl.kernel, mesh=mesh,
        out_type=jax.ShapeDtypeStruct((n, HALF_D), y.dtype),
        scratch_types=[pltpu.VMEM((TOP_K, chunks_per_worker, C), jnp.int32),
                       pltpu.VMEM((TOP_K, chunks_per_worker, C), jnp.uint32),
                       pltpu.VMEM((2, TOP_K, C, HALF_D), y.dtype),
                       pltpu.VMEM((2, C, HALF_D), y.dtype),
                       pltpu.SemaphoreType.DMA((2, TOP_K)),
                       pltpu.SemaphoreType.DMA((2,))],
        compiler_params=pltpu.CompilerParams(needs_layout_passes=False),
        name="sc_collect_sum")
    def run(y_hbm, dest_hbm, gate_hbm, out_hbm, idx_v, gate_v, rows_v, sum_v, gather_sem, write_sem):
        local0 = _sc_worker_id() * chunks_per_worker

        pltpu.sync_copy(dest_hbm.at[:, pl.ds(first_chunk + local0, chunks_per_worker)], idx_v)
        pltpu.sync_copy(gate_hbm.at[:, pl.ds(first_chunk + local0, chunks_per_worker)], gate_v)

        def gather_copies(i, slot):
            return [pltpu.make_async_copy(y_hbm.at[idx_v.at[k, i]], rows_v.at[slot, k],
                                          gather_sem.at[slot, k]) for k in range(TOP_K)]

        def write_copy(i, slot):
            return pltpu.make_async_copy(sum_v.at[slot], out_hbm.at[pl.ds((local0 + i) * C, C)],
                                         write_sem.at[slot])

        def fetch(i, slot):
            for cp in gather_copies(i, slot):
                cp.start()

        def reduce_chunk(i, slot):
            for cp in gather_copies(i, slot):
                cp.wait()

            @pl.when(i >= 2)
            def _():
                write_copy(i - 2, slot).wait()

            @plsc.parallel_loop(0, C)
            def _(j):
                lane_j = jnp.full((lanes,), j, jnp.int32)
                g = [plsc.bitcast(gate_v[k, i, :].at[lane_j].get(mode="promise_in_bounds"),
                                  jnp.bfloat16) for k in range(TOP_K)]
                for c in range(HALF_D // lanes):
                    sl = pl.ds(c * lanes, lanes)
                    acc = None
                    for k in range(TOP_K):
                        term = plsc.bitcast(rows_v[slot, k, j, sl], jnp.bfloat16) * g[k]
                        acc = term if acc is None else acc + term
                    sum_v[slot, j, sl] = plsc.bitcast(acc, y.dtype)

            write_copy(i, slot).start()

        fetch(0, 0)

        @pl.loop(0, chunks_per_worker, step=2)
        def _(i):
            fetch(i + 1, 1)
            reduce_chunk(i, 0)

            @pl.when(i + 2 < chunks_per_worker)
            def _():
                fetch(i + 2, 0)

            reduce_chunk(i + 1, 1)

        write_copy(chunks_per_worker - 2, 0).wait()
        write_copy(chunks_per_worker - 1, 1).wait()

    return run(y, dest3, gates16)


def _combine_sum_kernel(x1_ref, ys_ref, o_ref):
    lo, hi = _unpack_halves(ys_ref[...])
    o_ref[:, :HALF_D] = x1_ref[:, :HALF_D] + lo
    o_ref[:, HALF_D:] = x1_ref[:, HALF_D:] + hi


def _combine_sum(acc, ysum, split):
    n = acc.shape[0]
    rows = PROJ_ROWS
    steps = ysum.shape[0] // rows
    first = split * steps
    return pl.pallas_call(
        _combine_sum_kernel,
        out_shape=jax.ShapeDtypeStruct((n, D_MODEL), jnp.float32),
        grid=(steps,),
        in_specs=[pl.BlockSpec((rows, D_MODEL), lambda i: (first + i, 0)),
                  pl.BlockSpec((rows, HALF_D), lambda i: (i, 0))],
        out_specs=pl.BlockSpec((rows, D_MODEL), lambda i: (first + i, 0)),
        input_output_aliases={0: 0},
        compiler_params=pltpu.CompilerParams(
            dimension_semantics=("arbitrary",), vmem_limit_bytes=VMEM_LIMIT),
        name="moe_combine",
    )(acc, ysum)


def _layer(x2, batch, seq, attn_norm_g, w_in, a_q_g, a_k_g, a_sinks, b_q_g, b_k_g, w_out,
           ffn_norm_g, w_router, b_router, w1, b1, w2, b2):
    n = x2.shape[0]
    slopes = _alibi_slopes()
    q_scale = HEAD_DIM ** -0.5 * LOG2E
    reps = MXU_DIM // HEAD_DIM
    gains = jnp.stack([jnp.tile(a_q_g, reps) * q_scale, jnp.tile(a_k_g, reps),
                       jnp.tile(b_q_g, reps) * q_scale, jnp.tile(b_k_g, reps)]).astype(jnp.float32)

    proj = _in_proj(x2, attn_norm_g.reshape(1, -1), w_in, gains, batch, seq)
    qa, ka, va = proj[:3]
    nb = len(B_DILS)
    qbs, kbs, vbs = proj[3:3 + nb], proj[3 + nb:3 + 2 * nb], proj[3 + 2 * nb:]

    bias_a = _bias_tables(slopes[:A_Q_HEADS], A_STACK_HEADS, A_HALF_WINDOW, 1, Q_TILE + 2 * A_HALF_WINDOW)
    sink_col = jnp.repeat(a_sinks.astype(jnp.float32) * LOG2E, Q_TILE).reshape(
        A_Q_HEADS // A_STACK_HEADS, A_STACK_HEADS * Q_TILE, 1)
    sink_col = jnp.broadcast_to(sink_col, sink_col.shape[:2] + (LANES,))
    as_seqs = lambda a: a.reshape(batch, seq, a.shape[-1])
    out_a = _banded_attention(as_seqs(qa), as_seqs(ka), as_seqs(va), bias_a, half_w=A_HALF_WINDOW,
                              sink=sink_col, name="attn_a")[0].reshape(n, A_Q_W)

    outs_b, lses_b = [], []
    for bi, (window, dil) in enumerate(B_BRANCHES):
        half_w = window // (2 * dil)
        bias_b = _bias_tables(slopes[A_Q_HEADS:], B_STACK_HEADS, half_w, dil, Q_TILE + 2 * half_w)
        L = seq // dil
        to_seqs = lambda a: a.reshape(batch * dil, L, a.shape[-1])
        o, lse = _banded_attention(to_seqs(qbs[bi]), to_seqs(kbs[bi]), to_seqs(vbs[bi]), bias_b,
                                   half_w=half_w, want_lse=True, name=f"attn_b_d{dil}")
        if dil == 1:
            outs_b.append(o.reshape(n, B_W))
            lses_b.append(lse.reshape(n, LANES))
        else:
            outs_b.append(o.reshape(batch, dil, L, B_W))
            lses_b.append(lse.reshape(batch, dil, L, LANES))

    wr_t = w_router.T.astype(jnp.float32)
    wr_hi = wr_t.astype(jnp.bfloat16)
    wr_lo = (wr_t - wr_hi.astype(jnp.float32)).astype(jnp.bfloat16)
    wr = jnp.concatenate([wr_hi, wr_lo], axis=0)
    br = jnp.broadcast_to(b_router.astype(jnp.float32)[:, None], (N_EXPERTS, LANES))
    x1, hf, topi, gates, ranks, counts = _out_proj_router(
        out_a, outs_b, lses_b, x2, w_out, ffn_norm_g.reshape(1, -1), wr, br, seq)

    g = MOE_ROWS
    nk = n * TOP_K
    step_rows = MOE_STEP_BLOCKS * g
    n_rows = -(-(nk + N_EXPERTS * g) // step_rows) * step_rows
    cnt = counts[:, 0]
    pcnt = (cnt + g - 1) // g * g
    pends = jnp.cumsum(pcnt)
    pstarts = pends - pcnt
    experts = jnp.arange(N_EXPERTS, dtype=jnp.int32)
    start_of = jnp.sum(jnp.where(topi[:, :, None] == experts, pstarts, 0), axis=-1)
    dest = (start_of + ranks).astype(jnp.int32)
    plan = _moe_plan(pends, n_rows // g)
    dest3 = dest.reshape(TOP_K, n // SC_CHUNK, SC_CHUNK)

    xb = _sc_dispatch(hf, dest3, n_rows)
    y = _moe_experts(plan, xb, w1, b1[:, None, :], w2, b2[:, None, :])
    smallest_range = max(PROJ_ROWS, SC_COLLECT_CHUNK * SC_WORKERS)
    n_splits = max(1, min(COMBINE_SPLITS, n // smallest_range))
    per_split = n // n_splits
    chunked = lambda a: a.reshape(TOP_K, n // SC_COLLECT_CHUNK, SC_COLLECT_CHUNK)
    dest_c = chunked(dest)
    gate_bits = lax.bitcast_convert_type(gates.astype(jnp.bfloat16), jnp.uint16).astype(jnp.uint32)
    gate_words = gate_bits | (gate_bits << 16)
    gates_c = chunked(gate_words)
    chunks_per_split = per_split // SC_COLLECT_CHUNK
    out = x1
    for s in range(n_splits):
        ysum = _sc_collect_sum(y, dest_c, gates_c, s * chunks_per_split, chunks_per_split)
        out = _combine_sum(out, ysum, s)
    return out


def kernel(x, attn_norm_g, w_in, a_q_norm_g, a_k_norm_g, a_sinks, b_q_norm_g, b_k_norm_g, w_out,
           ffn_norm_g, w_router, b_router, w1, b1, w2, b2):
    batch, seq, d = x.shape
    x2 = x.reshape(batch * seq, d)
    for i in range(attn_norm_g.shape[0]):
        x2 = _layer(x2, batch, seq, attn_norm_g[i], w_in[i], a_q_norm_g[i], a_k_norm_g[i],
                    a_sinks[i], b_q_norm_g[i], b_k_norm_g[i], w_out[i], ffn_norm_g[i],
                    w_router[i], b_router[i], w1[i], b1[i], w2[i], b2[i])
    return x2.reshape(batch, seq, d)
```
